```python
import jax, jax.numpy as jnp
from jax import lax
import numpy as np

D_MODEL = 2048
BATCH = 8
SEQ = 4096
DEPTH = 2

CHUNK = 64
N_EVEN = (DEPTH + 1) // 2
N_ODD = DEPTH // 2

A_HEADS = 8
A_DK = 128
A_DV = 128
A_CONV = 4
B_GROUPS = 8
B_DG = 128
B_BLOCK = 128
A_QK = A_HEADS * A_DK
A_V = A_HEADS * A_DV
B_W = B_GROUPS * B_DG
A_QKV = 2 * A_QK + A_V
IN_COLS = A_QKV + A_V + 2 * A_HEADS + 2 * B_W
MIX_W = A_V + B_W
C_WIDTH = 31
D_FF = 4 * D_MODEL
EPS = 1e-6

kernel_name = "hybrid_deltanet_gmlp_conformer_block"


def rmsnorm(x, g):
    xf = x.astype(jnp.float32)
    y = xf * lax.rsqrt(jnp.mean(xf * xf, axis=-1, keepdims=True) + EPS)
    return (y * g.astype(jnp.float32)).astype(x.dtype)


def layernorm(x, g, b):
    xf = x.astype(jnp.float32)
    mu = jnp.mean(xf, axis=-1, keepdims=True)
    xc = xf - mu
    y = xc * lax.rsqrt(jnp.mean(xc * xc, axis=-1, keepdims=True) + EPS)
    return (y * g.astype(jnp.float32) + b.astype(jnp.float32)).astype(x.dtype)


def l2norm(x):
    return x * lax.rsqrt(jnp.sum(x * x, axis=-1, keepdims=True) + EPS)


def causal_dwconv(x, w):
    k = w.shape[0]
    return lax.conv_general_dilated(
        x, w[:, None, :].astype(x.dtype), window_strides=(1,), padding=[(k - 1, 0)],
        dimension_numbers=("NWC", "WIO", "NWC"), feature_group_count=x.shape[-1])


def gated_delta_rule(q, k, v, g, beta):
    bsz, seq, heads, dk = q.shape
    dv = v.shape[-1]
    n = seq // CHUNK

    def chunks(t):
        return jnp.moveaxis(t.astype(jnp.float32).reshape(bsz, n, CHUNK, heads, -1), 3, 1)

    q = chunks(q) * (dk ** -0.5)
    k = chunks(k)
    v = chunks(v)
    g = jnp.moveaxis(g.astype(jnp.float32).reshape(bsz, n, CHUNK, heads), 3, 1)
    beta = jnp.moveaxis(beta.astype(jnp.float32).reshape(bsz, n, CHUNK, heads), 3, 1)
    g = jnp.cumsum(g, axis=-1)

    tri = jnp.tril(jnp.ones((CHUNK, CHUNK), dtype=bool))
    strict = jnp.tril(jnp.ones((CHUNK, CHUNK), dtype=bool), -1)
    eye = jnp.eye(CHUNK, dtype=jnp.float32)
    decay = jnp.exp(jnp.where(tri, g[..., :, None] - g[..., None, :], -jnp.inf))

    k_beta = k * beta[..., None]
    v_beta = v * beta[..., None]
    a = jnp.einsum("bhnid,bhnjd->bhnij", k_beta, k) * decay
    a = jnp.where(strict, a, 0.0) + eye
    rhs = jnp.concatenate([v_beta, k_beta * jnp.exp(g)[..., None]], axis=-1)
    sol = lax.linalg.triangular_solve(a, rhs, left_side=True, lower=True, unit_diagonal=True)
    u = sol[..., :dv]
    w = sol[..., dv:]

    attn = jnp.einsum("bhnid,bhnjd->bhnij", q, k) * decay
    q_dec = q * jnp.exp(g)[..., None]
    k_tail = k * jnp.exp(g[..., -1:] - g)[..., None]
    g_last = jnp.exp(g[..., -1])

    xs = tuple(jnp.moveaxis(t, 2, 0) for t in (attn, q_dec, k_tail, u, w, g_last))

    def step(state, inp):
        attn_c, qd_c, kt_c, u_c, w_c, gl_c = inp
        v_new = u_c - jnp.einsum("bhck,bhkv->bhcv", w_c, state)
        o_c = (jnp.einsum("bhck,bhkv->bhcv", qd_c, state)
               + jnp.einsum("bhij,bhjv->bhiv", attn_c, v_new))
        state = state * gl_c[..., None, None] + jnp.einsum("bhck,bhcv->bhkv", kt_c, v_new)
        return state, o_c

    s0 = jnp.zeros((bsz, heads, dk, dv), jnp.float32)
    _, o = lax.scan(step, s0, xs)
    return jnp.transpose(o, (1, 0, 3, 2, 4)).reshape(bsz, seq, heads, dv)


def even_mixer(h, w_in, conv_w, a_log, dt_bias, o_norm, ln_g, ln_b, w_s, b_s, w_out):
    bsz, seq, _ = h.shape
    proj = h @ w_in.astype(h.dtype)
    cuts = [int(c) for c in np.cumsum([A_QKV, A_V, A_HEADS, A_HEADS, B_W])]
    qkv, z, b_raw, a_raw, u_raw, v_raw = jnp.split(proj, cuts, axis=-1)

    qkv = jax.nn.silu(causal_dwconv(qkv, conv_w))
    q, k, v = jnp.split(qkv, [A_QK, 2 * A_QK], axis=-1)
    q = l2norm(q.astype(jnp.float32).reshape(bsz, seq, A_HEADS, A_DK))
    k = l2norm(k.astype(jnp.float32).reshape(bsz, seq, A_HEADS, A_DK))
    v = v.reshape(bsz, seq, A_HEADS, A_DV)
    beta = jax.nn.sigmoid(b_raw.astype(jnp.float32))
    g = -jnp.exp(a_log.astype(jnp.float32)) * jax.nn.softplus(
        a_raw.astype(jnp.float32) + dt_bias.astype(jnp.float32))
    o = gated_delta_rule(q, k, v, g, beta)
    o = rmsnorm(o, o_norm) * jax.nn.silu(z.astype(jnp.float32).reshape(bsz, seq, A_HEADS, A_DV))
    out_a = o.reshape(bsz, seq, A_V).astype(h.dtype)

    u = jax.nn.gelu(u_raw, approximate=False)
    vg = jax.nn.gelu(v_raw, approximate=False).reshape(bsz, seq, B_GROUPS, B_DG)
    vg = layernorm(vg, ln_g.reshape(B_GROUPS, B_DG), ln_b.reshape(B_GROUPS, B_DG))
    vg = vg.reshape(bsz, seq // B_BLOCK, B_BLOCK, B_GROUPS, B_DG)
    pos = np.arange(B_BLOCK)
    mask = (pos[None, :] // CHUNK) <= (pos[:, None] // CHUNK)
    w_masked = jnp.where(mask, w_s, 0.0).astype(h.dtype)
    mixed = jnp.einsum("gij,bmjgc->bmigc", w_masked, vg) + b_s.T.astype(h.dtype)[None, None, :, :, None]
    out_b = u * mixed.reshape(bsz, seq, B_W)

    return jnp.concatenate([out_a, out_b], axis=-1) @ w_out.astype(h.dtype)


def conformer_conv(h, pw1, pw1_b, dw, dw_b, ln_g, ln_b, pw2, pw2_b):
    z = h @ pw1.astype(h.dtype) + pw1_b.astype(h.dtype)
    z = z[..., :D_MODEL] * jax.nn.sigmoid(z[..., D_MODEL:])
    z = causal_dwconv(z, dw) + dw_b.astype(h.dtype)
    z = jax.nn.silu(layernorm(z, ln_g, ln_b))
    return z @ pw2.astype(h.dtype) + pw2_b.astype(h.dtype)


def sq_relu_mlp(h, w1, w2):
    a = jax.nn.relu(h @ w1.astype(h.dtype))
    return (a * a) @ w2.astype(h.dtype)


def _fwd_setup_inputs(seed: int = 0) -> dict:
    key = jax.random.key(seed)
    ks = iter(jax.random.split(key, 40))

    def nrm(shape, scale):
        return jax.random.normal(next(ks), shape, jnp.float32) * scale

    def gain(shape):
        return 1.0 + nrm(shape, 0.05)

    dt = jnp.exp(jax.random.uniform(next(ks), (N_EVEN, A_HEADS), jnp.float32,
                                    np.log(1e-3), np.log(1e-1)))
    return {
        "x": nrm((BATCH, SEQ, D_MODEL), 1.0),
        "e_norm": gain((N_EVEN, D_MODEL)),
        "e_w_in": nrm((N_EVEN, D_MODEL, IN_COLS), D_MODEL ** -0.5),
        "e_conv_w": nrm((N_EVEN, A_CONV, A_QKV), A_CONV ** -0.5),
        "e_a_log": jnp.log(jax.random.uniform(next(ks), (N_EVEN, A_HEADS), jnp.float32, 1.0, 16.0)),
        "e_dt_bias": dt + jnp.log(-jnp.expm1(-dt)),
        "e_o_norm": gain((N_EVEN, A_DV)),
        "e_ln_g": gain((N_EVEN, B_W)),
        "e_ln_b": nrm((N_EVEN, B_W), 0.02),
        "e_w_s": nrm((N_EVEN, B_GROUPS, B_BLOCK, B_BLOCK), B_BLOCK ** -0.5),
        "e_b_s": gain((N_EVEN, B_GROUPS, B_BLOCK)),
        "e_w_out": nrm((N_EVEN, MIX_W, D_MODEL), MIX_W ** -0.5),
        "o_norm": gain((N_ODD, D_MODEL)),
        "o_pw1": nrm((N_ODD, D_MODEL, 2 * D_MODEL), D_MODEL ** -0.5),
        "o_pw1_b": nrm((N_ODD, 2 * D_MODEL), 0.02),
        "o_dw": nrm((N_ODD, C_WIDTH, D_MODEL), C_WIDTH ** -0.5),
        "o_dw_b": nrm((N_ODD, D_MODEL), 0.02),
        "o_ln_g": gain((N_ODD, D_MODEL)),
        "o_ln_b": nrm((N_ODD, D_MODEL), 0.02),
        "o_pw2": nrm((N_ODD, D_MODEL, D_MODEL), D_MODEL ** -0.5),
        "o_pw2_b": nrm((N_ODD, D_MODEL), 0.02),
        "f_norm": gain((DEPTH, D_MODEL)),
        "f_w1": nrm((DEPTH, D_MODEL, D_FF), D_MODEL ** -0.5),
        "f_w2": nrm((DEPTH, D_FF, D_MODEL), D_FF ** -0.5),
        "final_norm": gain((D_MODEL,)),
    }


def _fwd_reference(x, e_norm, e_w_in, e_conv_w, e_a_log, e_dt_bias, e_o_norm, e_ln_g, e_ln_b,
              e_w_s, e_b_s, e_w_out, o_norm, o_pw1, o_pw1_b, o_dw, o_dw_b, o_ln_g, o_ln_b,
              o_pw2, o_pw2_b, f_norm, f_w1, f_w2, final_norm):
    for layer in range(DEPTH):
        i = layer // 2
        if layer % 2 == 0:
            x = x + even_mixer(rmsnorm(x, e_norm[i]), e_w_in[i], e_conv_w[i], e_a_log[i],
                               e_dt_bias[i], e_o_norm[i], e_ln_g[i], e_ln_b[i], e_w_s[i],
                               e_b_s[i], e_w_out[i])
        else:
            x = x + conformer_conv(rmsnorm(x, o_norm[i]), o_pw1[i], o_pw1_b[i], o_dw[i],
                                   o_dw_b[i], o_ln_g[i], o_ln_b[i], o_pw2[i], o_pw2_b[i])
        x = x + sq_relu_mlp(rmsnorm(x, f_norm[layer]), f_w1[layer], f_w2[layer])
    return rmsnorm(x, final_norm)


import jax as _jax
import jax.numpy as _jnp

TWIN_FORMAT = 'train_step'
FWD_PARAMS = ['x', 'e_norm', 'e_w_in', 'e_conv_w', 'e_a_log', 'e_dt_bias', 'e_o_norm', 'e_ln_g', 'e_ln_b', 'e_w_s', 'e_b_s', 'e_w_out', 'o_norm', 'o_pw1', 'o_pw1_b', 'o_dw', 'o_dw_b', 'o_ln_g', 'o_ln_b', 'o_pw2', 'o_pw2_b', 'f_norm', 'f_w1', 'f_w2', 'final_norm']
TWIN_WEIGHTS = ['e_norm', 'e_w_in', 'e_conv_w', 'e_a_log', 'e_dt_bias', 'e_o_norm', 'e_ln_g', 'e_ln_b', 'e_w_s', 'e_b_s', 'e_w_out', 'o_norm', 'o_pw1', 'o_pw1_b', 'o_dw', 'o_dw_b', 'o_ln_g', 'o_ln_b', 'o_pw2', 'o_pw2_b', 'f_norm', 'f_w1', 'f_w2', 'final_norm']
TWIN_DIFF_INPUT = 'x'
TWIN_INPUTS = ['x', 'e_norm', 'e_w_in', 'e_conv_w', 'e_a_log', 'e_dt_bias', 'e_o_norm', 'e_ln_g', 'e_ln_b', 'e_w_s', 'e_b_s', 'e_w_out', 'o_norm', 'o_pw1', 'o_pw1_b', 'o_dw', 'o_dw_b', 'o_ln_g', 'o_ln_b', 'o_pw2', 'o_pw2_b', 'f_norm', 'f_w1', 'f_w2', 'final_norm', 'loss_target', 'm_e_norm', 'm_e_w_in', 'm_e_conv_w', 'm_e_a_log', 'm_e_dt_bias', 'm_e_o_norm', 'm_e_ln_g', 'm_e_ln_b', 'm_e_w_s', 'm_e_b_s', 'm_e_w_out', 'm_o_norm', 'm_o_pw1', 'm_o_pw1_b', 'm_o_dw', 'm_o_dw_b', 'm_o_ln_g', 'm_o_ln_b', 'm_o_pw2', 'm_o_pw2_b', 'm_f_norm', 'm_f_w1', 'm_f_w2', 'm_final_norm', 'v_e_norm', 'v_e_w_in', 'v_e_conv_w', 'v_e_a_log', 'v_e_dt_bias', 'v_e_o_norm', 'v_e_ln_g', 'v_e_ln_b', 'v_e_w_s', 'v_e_b_s', 'v_e_w_out', 'v_o_norm', 'v_o_pw1', 'v_o_pw1_b', 'v_o_dw', 'v_o_dw_b', 'v_o_ln_g', 'v_o_ln_b', 'v_o_pw2', 'v_o_pw2_b', 'v_f_norm', 'v_f_w1', 'v_f_w2', 'v_final_norm']
TWIN_OUTPUTS = ['loss', 'grad_x', 'grad_e_norm', 'grad_e_w_in', 'grad_e_conv_w', 'grad_e_a_log', 'grad_e_dt_bias', 'grad_e_o_norm', 'grad_e_ln_g', 'grad_e_ln_b', 'grad_e_w_s', 'grad_e_b_s', 'grad_e_w_out', 'grad_o_norm', 'grad_o_pw1', 'grad_o_pw1_b', 'grad_o_dw', 'grad_o_dw_b', 'grad_o_ln_g', 'grad_o_ln_b', 'grad_o_pw2', 'grad_o_pw2_b', 'grad_f_norm', 'grad_f_w1', 'grad_f_w2', 'grad_final_norm', 'delta_e_norm', 'delta_e_w_in', 'delta_e_conv_w', 'delta_e_a_log', 'delta_e_dt_bias', 'delta_e_o_norm', 'delta_e_ln_g', 'delta_e_ln_b', 'delta_e_w_s', 'delta_e_b_s', 'delta_e_w_out', 'delta_o_norm', 'delta_o_pw1', 'delta_o_pw1_b', 'delta_o_dw', 'delta_o_dw_b', 'delta_o_ln_g', 'delta_o_ln_b', 'delta_o_pw2', 'delta_o_pw2_b', 'delta_f_norm', 'delta_f_w1', 'delta_f_w2', 'delta_final_norm', 'new_m_e_norm', 'new_m_e_w_in', 'new_m_e_conv_w', 'new_m_e_a_log', 'new_m_e_dt_bias', 'new_m_e_o_norm', 'new_m_e_ln_g', 'new_m_e_ln_b', 'new_m_e_w_s', 'new_m_e_b_s', 'new_m_e_w_out', 'new_m_o_norm', 'new_m_o_pw1', 'new_m_o_pw1_b', 'new_m_o_dw', 'new_m_o_dw_b', 'new_m_o_ln_g', 'new_m_o_ln_b', 'new_m_o_pw2', 'new_m_o_pw2_b', 'new_m_f_norm', 'new_m_f_w1', 'new_m_f_w2', 'new_m_final_norm', 'new_v_e_norm', 'new_v_e_w_in', 'new_v_e_conv_w', 'new_v_e_a_log', 'new_v_e_dt_bias', 'new_v_e_o_norm', 'new_v_e_ln_g', 'new_v_e_ln_b', 'new_v_e_w_s', 'new_v_e_b_s', 'new_v_e_w_out', 'new_v_o_norm', 'new_v_o_pw1', 'new_v_o_pw1_b', 'new_v_o_dw', 'new_v_o_dw_b', 'new_v_o_ln_g', 'new_v_o_ln_b', 'new_v_o_pw2', 'new_v_o_pw2_b', 'new_v_f_norm', 'new_v_f_w1', 'new_v_f_w2', 'new_v_final_norm']
TWIN_LEAF_KINDS = {'loss': 'loss', 'grad_x': 'grad_x', 'grad_e_norm': 'grad_w', 'grad_e_w_in': 'grad_w', 'grad_e_conv_w': 'grad_w', 'grad_e_a_log': 'grad_w', 'grad_e_dt_bias': 'grad_w', 'grad_e_o_norm': 'grad_w', 'grad_e_ln_g': 'grad_w', 'grad_e_ln_b': 'grad_w', 'grad_e_w_s': 'grad_w', 'grad_e_b_s': 'grad_w', 'grad_e_w_out': 'grad_w', 'grad_o_norm': 'grad_w', 'grad_o_pw1': 'grad_w', 'grad_o_pw1_b': 'grad_w', 'grad_o_dw': 'grad_w', 'grad_o_dw_b': 'grad_w', 'grad_o_ln_g': 'grad_w', 'grad_o_ln_b': 'grad_w', 'grad_o_pw2': 'grad_w', 'grad_o_pw2_b': 'grad_w', 'grad_f_norm': 'grad_w', 'grad_f_w1': 'grad_w', 'grad_f_w2': 'grad_w', 'grad_final_norm': 'grad_w', 'delta_e_norm': 'delta_w', 'delta_e_w_in': 'delta_w', 'delta_e_conv_w': 'delta_w', 'delta_e_a_log': 'delta_w', 'delta_e_dt_bias': 'delta_w', 'delta_e_o_norm': 'delta_w', 'delta_e_ln_g': 'delta_w', 'delta_e_ln_b': 'delta_w', 'delta_e_w_s': 'delta_w', 'delta_e_b_s': 'delta_w', 'delta_e_w_out': 'delta_w', 'delta_o_norm': 'delta_w', 'delta_o_pw1': 'delta_w', 'delta_o_pw1_b': 'delta_w', 'delta_o_dw': 'delta_w', 'delta_o_dw_b': 'delta_w', 'delta_o_ln_g': 'delta_w', 'delta_o_ln_b': 'delta_w', 'delta_o_pw2': 'delta_w', 'delta_o_pw2_b': 'delta_w', 'delta_f_norm': 'delta_w', 'delta_f_w1': 'delta_w', 'delta_f_w2': 'delta_w', 'delta_final_norm': 'delta_w', 'new_m_e_norm': 'new_m', 'new_m_e_w_in': 'new_m', 'new_m_e_conv_w': 'new_m', 'new_m_e_a_log': 'new_m', 'new_m_e_dt_bias': 'new_m', 'new_m_e_o_norm': 'new_m', 'new_m_e_ln_g': 'new_m', 'new_m_e_ln_b': 'new_m', 'new_m_e_w_s': 'new_m', 'new_m_e_b_s': 'new_m', 'new_m_e_w_out': 'new_m', 'new_m_o_norm': 'new_m', 'new_m_o_pw1': 'new_m', 'new_m_o_pw1_b': 'new_m', 'new_m_o_dw': 'new_m', 'new_m_o_dw_b': 'new_m', 'new_m_o_ln_g': 'new_m', 'new_m_o_ln_b': 'new_m', 'new_m_o_pw2': 'new_m', 'new_m_o_pw2_b': 'new_m', 'new_m_f_norm': 'new_m', 'new_m_f_w1': 'new_m', 'new_m_f_w2': 'new_m', 'new_m_final_norm': 'new_m', 'new_v_e_norm': 'new_v', 'new_v_e_w_in': 'new_v', 'new_v_e_conv_w': 'new_v', 'new_v_e_a_log': 'new_v', 'new_v_e_dt_bias': 'new_v', 'new_v_e_o_norm': 'new_v', 'new_v_e_ln_g': 'new_v', 'new_v_e_ln_b': 'new_v', 'new_v_e_w_s': 'new_v', 'new_v_e_b_s': 'new_v', 'new_v_e_w_out': 'new_v', 'new_v_o_norm': 'new_v', 'new_v_o_pw1': 'new_v', 'new_v_o_pw1_b': 'new_v', 'new_v_o_dw': 'new_v', 'new_v_o_dw_b': 'new_v', 'new_v_o_ln_g': 'new_v', 'new_v_o_ln_b': 'new_v', 'new_v_o_pw2': 'new_v', 'new_v_o_pw2_b': 'new_v', 'new_v_f_norm': 'new_v', 'new_v_f_w1': 'new_v', 'new_v_f_w2': 'new_v', 'new_v_final_norm': 'new_v'}


def _forward(args):
    return _fwd_reference(*[args[k] for k in FWD_PARAMS])


def _output_shape():
    def fwd():
        inp = _fwd_setup_inputs(0)
        return _fwd_reference(*[inp[k] for k in FWD_PARAMS])
    out = _jax.eval_shape(fwd)
    return out.shape, out.dtype

N_MICROBATCH = 1
ADAM_LR = 0.001
ADAM_B1 = 0.9
ADAM_B2 = 0.999
ADAM_EPS = 1e-08
ADAM_WD = 0.01
ADAM_STEP = 10
PER_EXAMPLE_BATCH_AXIS = {'x': 0, 'loss_target': 0}
SHARED_INPUTS = []
_WEIGHT_DTYPES = {'e_norm': _jnp.float32, 'e_w_in': _jnp.float32, 'e_conv_w': _jnp.float32, 'e_a_log': _jnp.float32, 'e_dt_bias': _jnp.float32, 'e_o_norm': _jnp.float32, 'e_ln_g': _jnp.float32, 'e_ln_b': _jnp.float32, 'e_w_s': _jnp.float32, 'e_b_s': _jnp.float32, 'e_w_out': _jnp.float32, 'o_norm': _jnp.float32, 'o_pw1': _jnp.float32, 'o_pw1_b': _jnp.float32, 'o_dw': _jnp.float32, 'o_dw_b': _jnp.float32, 'o_ln_g': _jnp.float32, 'o_ln_b': _jnp.float32, 'o_pw2': _jnp.float32, 'o_pw2_b': _jnp.float32, 'f_norm': _jnp.float32, 'f_w1': _jnp.float32, 'f_w2': _jnp.float32, 'final_norm': _jnp.float32}
MOMENT_SCALE = {'e_norm': 8.632184e-02, 'e_w_in': 5.019449e-02, 'e_conv_w': 4.024012e-02, 'e_a_log': 3.019998e-01, 'e_dt_bias': 2.945509e-01, 'e_o_norm': 1.604701e-01, 'e_ln_g': 4.446126e-02, 'e_ln_b': 4.972860e-02, 'e_w_s': 4.644787e-02, 'e_b_s': 5.758354e-02, 'e_w_out': 8.304999e-02, 'o_norm': 4.761373e-02, 'o_pw1': 3.385417e-02, 'o_pw1_b': 9.549737e-02, 'o_dw': 4.749893e-02, 'o_dw_b': 2.293477e-01, 'o_ln_g': 9.613911e-02, 'o_ln_b': 1.324429e-01, 'o_pw2': 6.846678e-02, 'o_pw2_b': 2.710417e-01, 'f_norm': 8.101698e-02, 'f_w1': 4.056786e-02, 'f_w2': 1.356228e-01, 'final_norm': 1.638058e+01}


def _to_microbatches(a, axis):
    t = _jnp.moveaxis(a, axis, 0)
    t = t.reshape((N_MICROBATCH, t.shape[0] // N_MICROBATCH) + t.shape[1:])
    return _jnp.moveaxis(t, 1, axis + 1)


def setup_inputs(seed: int = 0) -> dict:
    inp = _fwd_setup_inputs(seed)
    key = _jax.random.fold_in(_jax.random.key(seed), 7919)
    shape, _ = _output_shape()
    out = dict(inp)
    out["loss_target"] = _jax.random.normal(_jax.random.fold_in(key, 0), shape, _jnp.float32)
    for i, name in enumerate(TWIN_WEIGHTS):
        w = inp[name].astype(_jnp.float32)
        if MOMENT_SCALE is None:
            s = _jnp.sqrt(_jnp.mean(_jnp.square(w)) + 1e-30)
        else:
            s = MOMENT_SCALE[name]
        km, kv = _jax.random.split(_jax.random.fold_in(key, i + 1))
        out[name] = w
        out["m_" + name] = s * _jax.random.normal(km, w.shape, _jnp.float32)
        out["v_" + name] = (s * s) * _jax.random.uniform(kv, w.shape, _jnp.float32, 0.5, 1.5)
    if N_MICROBATCH > 1:
        for name, axis in PER_EXAMPLE_BATCH_AXIS.items():
            out[name] = _to_microbatches(out[name], axis)
    return {'x': out['x'], 'e_norm': out['e_norm'], 'e_w_in': out['e_w_in'], 'e_conv_w': out['e_conv_w'], 'e_a_log': out['e_a_log'], 'e_dt_bias': out['e_dt_bias'], 'e_o_norm': out['e_o_norm'], 'e_ln_g': out['e_ln_g'], 'e_ln_b': out['e_ln_b'], 'e_w_s': out['e_w_s'], 'e_b_s': out['e_b_s'], 'e_w_out': out['e_w_out'], 'o_norm': out['o_norm'], 'o_pw1': out['o_pw1'], 'o_pw1_b': out['o_pw1_b'], 'o_dw': out['o_dw'], 'o_dw_b': out['o_dw_b'], 'o_ln_g': out['o_ln_g'], 'o_ln_b': out['o_ln_b'], 'o_pw2': out['o_pw2'], 'o_pw2_b': out['o_pw2_b'], 'f_norm': out['f_norm'], 'f_w1': out['f_w1'], 'f_w2': out['f_w2'], 'final_norm': out['final_norm'], 'loss_target': out['loss_target'], 'm_e_norm': out['m_e_norm'], 'm_e_w_in': out['m_e_w_in'], 'm_e_conv_w': out['m_e_conv_w'], 'm_e_a_log': out['m_e_a_log'], 'm_e_dt_bias': out['m_e_dt_bias'], 'm_e_o_norm': out['m_e_o_norm'], 'm_e_ln_g': out['m_e_ln_g'], 'm_e_ln_b': out['m_e_ln_b'], 'm_e_w_s': out['m_e_w_s'], 'm_e_b_s': out['m_e_b_s'], 'm_e_w_out': out['m_e_w_out'], 'm_o_norm': out['m_o_norm'], 'm_o_pw1': out['m_o_pw1'], 'm_o_pw1_b': out['m_o_pw1_b'], 'm_o_dw': out['m_o_dw'], 'm_o_dw_b': out['m_o_dw_b'], 'm_o_ln_g': out['m_o_ln_g'], 'm_o_ln_b': out['m_o_ln_b'], 'm_o_pw2': out['m_o_pw2'], 'm_o_pw2_b': out['m_o_pw2_b'], 'm_f_norm': out['m_f_norm'], 'm_f_w1': out['m_f_w1'], 'm_f_w2': out['m_f_w2'], 'm_final_norm': out['m_final_norm'], 'v_e_norm': out['v_e_norm'], 'v_e_w_in': out['v_e_w_in'], 'v_e_conv_w': out['v_e_conv_w'], 'v_e_a_log': out['v_e_a_log'], 'v_e_dt_bias': out['v_e_dt_bias'], 'v_e_o_norm': out['v_e_o_norm'], 'v_e_ln_g': out['v_e_ln_g'], 'v_e_ln_b': out['v_e_ln_b'], 'v_e_w_s': out['v_e_w_s'], 'v_e_b_s': out['v_e_b_s'], 'v_e_w_out': out['v_e_w_out'], 'v_o_norm': out['v_o_norm'], 'v_o_pw1': out['v_o_pw1'], 'v_o_pw1_b': out['v_o_pw1_b'], 'v_o_dw': out['v_o_dw'], 'v_o_dw_b': out['v_o_dw_b'], 'v_o_ln_g': out['v_o_ln_g'], 'v_o_ln_b': out['v_o_ln_b'], 'v_o_pw2': out['v_o_pw2'], 'v_o_pw2_b': out['v_o_pw2_b'], 'v_f_norm': out['v_f_norm'], 'v_f_w1': out['v_f_w1'], 'v_f_w2': out['v_f_w2'], 'v_final_norm': out['v_final_norm']}


def _loss(weights, diff, rest, loss_target):
    with _jax.named_scope("forward"):
        args = {**rest, TWIN_DIFF_INPUT: diff, **{k: w.astype(_WEIGHT_DTYPES[k]) for k, w in weights.items()}}
        y = _forward(args)
    with _jax.named_scope("loss_head"):
        err = _jnp.square(y.astype(_jnp.float32) - loss_target)
        return 0.5 * _jnp.sum(_jnp.mean(err, axis=-1)) if err.ndim else 0.5 * err


def _adamw(w, g, m, v):
    m = ADAM_B1 * m + (1.0 - ADAM_B1) * g
    v = ADAM_B2 * v + (1.0 - ADAM_B2) * _jnp.square(g)
    m_hat = m / (1.0 - ADAM_B1 ** ADAM_STEP)
    v_hat = v / (1.0 - ADAM_B2 ** ADAM_STEP)
    delta = -ADAM_LR * (m_hat / (_jnp.sqrt(v_hat) + ADAM_EPS) + ADAM_WD * w)
    return delta, m, v


def reference(x, e_norm, e_w_in, e_conv_w, e_a_log, e_dt_bias, e_o_norm, e_ln_g, e_ln_b, e_w_s, e_b_s, e_w_out, o_norm, o_pw1, o_pw1_b, o_dw, o_dw_b, o_ln_g, o_ln_b, o_pw2, o_pw2_b, f_norm, f_w1, f_w2, final_norm, loss_target, m_e_norm, m_e_w_in, m_e_conv_w, m_e_a_log, m_e_dt_bias, m_e_o_norm, m_e_ln_g, m_e_ln_b, m_e_w_s, m_e_b_s, m_e_w_out, m_o_norm, m_o_pw1, m_o_pw1_b, m_o_dw, m_o_dw_b, m_o_ln_g, m_o_ln_b, m_o_pw2, m_o_pw2_b, m_f_norm, m_f_w1, m_f_w2, m_final_norm, v_e_norm, v_e_w_in, v_e_conv_w, v_e_a_log, v_e_dt_bias, v_e_o_norm, v_e_ln_g, v_e_ln_b, v_e_w_s, v_e_b_s, v_e_w_out, v_o_norm, v_o_pw1, v_o_pw1_b, v_o_dw, v_o_dw_b, v_o_ln_g, v_o_ln_b, v_o_pw2, v_o_pw2_b, v_f_norm, v_f_w1, v_f_w2, v_final_norm):
    given = dict(x=x, e_norm=e_norm, e_w_in=e_w_in, e_conv_w=e_conv_w, e_a_log=e_a_log, e_dt_bias=e_dt_bias, e_o_norm=e_o_norm, e_ln_g=e_ln_g, e_ln_b=e_ln_b, e_w_s=e_w_s, e_b_s=e_b_s, e_w_out=e_w_out, o_norm=o_norm, o_pw1=o_pw1, o_pw1_b=o_pw1_b, o_dw=o_dw, o_dw_b=o_dw_b, o_ln_g=o_ln_g, o_ln_b=o_ln_b, o_pw2=o_pw2, o_pw2_b=o_pw2_b, f_norm=f_norm, f_w1=f_w1, f_w2=f_w2, final_norm=final_norm, loss_target=loss_target, m_e_norm=m_e_norm, m_e_w_in=m_e_w_in, m_e_conv_w=m_e_conv_w, m_e_a_log=m_e_a_log, m_e_dt_bias=m_e_dt_bias, m_e_o_norm=m_e_o_norm, m_e_ln_g=m_e_ln_g, m_e_ln_b=m_e_ln_b, m_e_w_s=m_e_w_s, m_e_b_s=m_e_b_s, m_e_w_out=m_e_w_out, m_o_norm=m_o_norm, m_o_pw1=m_o_pw1, m_o_pw1_b=m_o_pw1_b, m_o_dw=m_o_dw, m_o_dw_b=m_o_dw_b, m_o_ln_g=m_o_ln_g, m_o_ln_b=m_o_ln_b, m_o_pw2=m_o_pw2, m_o_pw2_b=m_o_pw2_b, m_f_norm=m_f_norm, m_f_w1=m_f_w1, m_f_w2=m_f_w2, m_final_norm=m_final_norm, v_e_norm=v_e_norm, v_e_w_in=v_e_w_in, v_e_conv_w=v_e_conv_w, v_e_a_log=v_e_a_log, v_e_dt_bias=v_e_dt_bias, v_e_o_norm=v_e_o_norm, v_e_ln_g=v_e_ln_g, v_e_ln_b=v_e_ln_b, v_e_w_s=v_e_w_s, v_e_b_s=v_e_b_s, v_e_w_out=v_e_w_out, v_o_norm=v_o_norm, v_o_pw1=v_o_pw1, v_o_pw1_b=v_o_pw1_b, v_o_dw=v_o_dw, v_o_dw_b=v_o_dw_b, v_o_ln_g=v_o_ln_g, v_o_ln_b=v_o_ln_b, v_o_pw2=v_o_pw2, v_o_pw2_b=v_o_pw2_b, v_f_norm=v_f_norm, v_f_w1=v_f_w1, v_f_w2=v_f_w2, v_final_norm=v_final_norm)
    weights = {n: given[n] for n in TWIN_WEIGHTS}
    shared = {n: given[n] for n in SHARED_INPUTS}
    per_example = {n: given[n] for n in ['x']}
    grad_fn = _jax.value_and_grad(_loss, argnums=(0, 1))

    def one_microbatch(ex, loss_target):
        ex = dict(ex)
        diff = ex.pop(TWIN_DIFF_INPUT)
        return grad_fn(weights, diff, {**shared, **ex}, loss_target)

    if N_MICROBATCH == 1:
        loss, (grad_w, grad_x) = one_microbatch(per_example, given["loss_target"])
    else:
        def body(carry, xs):
            loss_sum, grad_sum = carry
            l_k, (gw_k, gx_k) = one_microbatch(xs[0], xs[1])
            with _jax.named_scope("update"):
                return (loss_sum + l_k, _jax.tree.map(_jnp.add, grad_sum, gw_k)), gx_k

        init = (_jnp.zeros((), _jnp.float32), _jax.tree.map(_jnp.zeros_like, weights))
        (loss, grad_w), grad_x = _jax.lax.scan(body, init, (per_example, given["loss_target"]))
    with _jax.named_scope("update"):
        delta_w, new_m, new_v = {}, {}, {}
        for n in TWIN_WEIGHTS:
            delta_w[n], new_m[n], new_v[n] = _adamw(weights[n], grad_w[n], given["m_" + n], given["v_" + n])
    return (loss, grad_x, *[grad_w[n] for n in TWIN_WEIGHTS], *[delta_w[n] for n in TWIN_WEIGHTS],
            *[new_m[n] for n in TWIN_WEIGHTS], *[new_v[n] for n in TWIN_WEIGHTS])
```

```python
import functools
import math

import jax
import jax.numpy as jnp
import numpy as np
from jax import lax
from jax.experimental import pallas as pl
from jax.experimental.pallas import tpu as pltpu

F32 = jnp.float32
BF16 = jnp.bfloat16
HI = lax.Precision.HIGHEST

EPS = 1e-6
CHUNK = 64
PAIR = 2 * CHUNK
HEAD_DIM = 128
A_CONV = 4
C_WIDTH = 31
N_CHIPS = 4
ADAM_LR, ADAM_B1, ADAM_B2, ADAM_EPS, ADAM_WD, ADAM_STEP = 0.001, 0.9, 0.999, 1e-08, 0.01, 10

VMEM_LIMIT = 56 * 1024 * 1024


def _cparams(sem=None):
    return pltpu.CompilerParams(dimension_semantics=sem, vmem_limit_bytes=VMEM_LIMIT)


def _pick(n, prefs):
    for p in prefs:
        if n % p == 0:
            return p
    return n


def mm_nn(a, w, *, name, epilogue=None, out_dtype=F32):
    m, k = a.shape
    s, _, ns = w.shape
    n = s * ns
    tm = _pick(m, (1024, 512, 256, 128))
    tn = _pick(ns, (1024, 512, 256, 128))
    tk = _pick(k, (2048, 1024, 512, 256, 128))
    nk = k // tk
    npb = ns // tn

    def body(a_ref, w_ref, *rest):
        if epilogue == "relu2":
            o1_ref, o2_ref = rest[0], rest[1]
            acc_ref = rest[2] if nk > 1 else None
        else:
            o1_ref = rest[0]
            acc_ref = rest[1] if nk > 1 else None
        part = jnp.dot(a_ref[...], w_ref[...], preferred_element_type=F32)

        def finish(c):
            if epilogue == "relu2":
                r = jnp.maximum(c, 0.0)
                o1_ref[...] = r.astype(o1_ref.dtype)
                o2_ref[...] = (r * r).astype(o2_ref.dtype)
            else:
                o1_ref[...] = c.astype(o1_ref.dtype)

        if nk == 1:
            finish(part)
        else:
            kk = pl.program_id(2)

            @pl.when(kk == 0)
            def _():
                acc_ref[...] = part

            @pl.when(kk > 0)
            def _():
                acc_ref[...] += part

            @pl.when(kk == nk - 1)
            def _():
                finish(acc_ref[...])

    o_spec = pl.BlockSpec((tm, tn), lambda i, j, kk: (i, j))
    if epilogue == "relu2":
        out_shape = (jax.ShapeDtypeStruct((m, n), BF16), jax.ShapeDtypeStruct((m, n), BF16))
        out_specs = (o_spec, o_spec)
    else:
        out_shape = jax.ShapeDtypeStruct((m, n), out_dtype)
        out_specs = o_spec
    return pl.pallas_call(
        body, name=name, out_shape=out_shape,
        grid=(m // tm, n // tn, nk),
        in_specs=[pl.BlockSpec((tm, tk), lambda i, j, kk: (i, kk)),
                  pl.BlockSpec((None, tk, tn), lambda i, j, kk: (j // npb, kk, j % npb))],
        out_specs=out_specs,
        scratch_shapes=[pltpu.VMEM((tm, tn), F32)] if nk > 1 else [],
        compiler_params=_cparams(("parallel", "parallel", "arbitrary")),
    )(a, w)


def mm_nt(a, w, *, name, mul=None, add=None, out_dtype=F32):
    assert mul is None or add is None
    if add is not None:
        mul = add
    m, n = a.shape
    s, k, ns = w.shape
    assert n == s * ns
    tm = _pick(m, (1024, 512, 256, 128))
    tko = _pick(k, (1024, 512, 256, 128))
    tn = _pick(ns, (2048, 1024, 512, 256, 128))
    nn = n // tn
    npb = ns // tn

    def body(a_ref, w_ref, *rest):
        if mul is not None:
            m_ref, o_ref = rest[0], rest[1]
            acc_ref = rest[2] if nn > 1 else None
        else:
            m_ref, o_ref = None, rest[0]
            acc_ref = rest[1] if nn > 1 else None
        part = lax.dot_general(a_ref[...], w_ref[...], (((1,), (1,)), ((), ())), preferred_element_type=F32)

        def finish(c):
            if add is not None:
                c = c + m_ref[...].astype(F32)
            elif m_ref is not None:
                c = c * (2.0 * m_ref[...].astype(F32))
            o_ref[...] = c.astype(o_ref.dtype)

        if nn == 1:
            finish(part)
        else:
            kk = pl.program_id(2)

            @pl.when(kk == 0)
            def _():
                acc_ref[...] = part

            @pl.when(kk > 0)
            def _():
                acc_ref[...] += part

            @pl.when(kk == nn - 1)
            def _():
                finish(acc_ref[...])

    in_specs = [pl.BlockSpec((tm, tn), lambda i, j, kk: (i, kk)),
                pl.BlockSpec((None, tko, tn), lambda i, j, kk: (kk // npb, j, kk % npb))]
    args = [a, w]
    if mul is not None:
        in_specs.append(pl.BlockSpec((tm, tko), lambda i, j, kk: (i, j)))
        args.append(mul)
    return pl.pallas_call(
        body, name=name, out_shape=jax.ShapeDtypeStruct((m, k), out_dtype),
        grid=(m // tm, k // tko, nn),
        in_specs=in_specs,
        out_specs=pl.BlockSpec((tm, tko), lambda i, j, kk: (i, j)),
        scratch_shapes=[pltpu.VMEM((tm, tko), F32)] if nn > 1 else [],
        compiler_params=_cparams(("parallel", "parallel", "arbitrary")),
    )(*args)


def mm_tn(a, b, *, groups, name):
    m, k = a.shape
    _, n = b.shape
    ns = n // groups
    tko = _pick(k, (1024, 512, 256, 128))
    tn = _pick(ns, (1024, 512, 256, 128))
    tc = _pick(m, (1024, 512, 256, 128))
    nc = m // tc
    npb = ns // tn

    def body(a_ref, b_ref, o_ref):
        part = lax.dot_general(a_ref[...], b_ref[...], (((0,), (0,)), ((), ())), preferred_element_type=F32)
        kk = pl.program_id(2)

        @pl.when(kk == 0)
        def _():
            o_ref[...] = part

        @pl.when(kk > 0)
        def _():
            o_ref[...] += part

    return pl.pallas_call(
        body, name=name, out_shape=jax.ShapeDtypeStruct((groups, k, ns), F32),
        grid=(k // tko, n // tn, nc),
        in_specs=[pl.BlockSpec((tc, tko), lambda i, j, kk: (kk, i)),
                  pl.BlockSpec((tc, tn), lambda i, j, kk: (kk, j))],
        out_specs=pl.BlockSpec((None, tko, tn), lambda i, j, kk: (j // npb, i, j % npb)),
        compiler_params=_cparams(("parallel", "parallel", "arbitrary")),
    )(a, b)


ROWS = 128


def _full_spec(arr):
    nd = arr.ndim
    return pl.BlockSpec(arr.shape, lambda i, _nd=nd: (0,) * _nd)


def _row_spec(x, rows):
    if isinstance(x, tuple):
        _, w, cb = x
        return pl.BlockSpec((rows, w), lambda i, _cb=cb: (i, _cb))
    return pl.BlockSpec((rows, x.shape[1]), lambda i: (i, 0))


def _arr(x):
    return x[0] if isinstance(x, tuple) else x


def _width(x):
    return x[1] if isinstance(x, tuple) else x.shape[1]


def rows_fwd(fn, params, xs, stores, *, name, rows=ROWS):
    t = _arr(xs[0]).shape[0]
    np_, nx = len(params), len(xs)

    def body(*refs):
        p_refs, x_refs, o_refs = refs[:np_], refs[np_:np_ + nx], refs[np_ + nx:]
        outs = fn(*[r[...].astype(F32) for r in p_refs], *[r[...].astype(F32) for r in x_refs])
        for (idx, dt, _), o_ref in zip(stores, o_refs):
            o_ref[...] = outs[idx].astype(dt)

    res = pl.pallas_call(
        body, name=name,
        out_shape=tuple(jax.ShapeDtypeStruct((t, w), dt) for _, dt, w in stores),
        grid=(t // rows,),
        in_specs=[_full_spec(p) for p in params] + [_row_spec(x, rows) for x in xs],
        out_specs=tuple(pl.BlockSpec((rows, w), lambda i: (i, 0)) for _, _, w in stores),
        compiler_params=_cparams(("parallel",)),
    )(*params, *[_arr(x) for x in xs])
    return res


def rows_bwd(fn, params, xs, cts, dx_stores, *, name, rows=ROWS):
    t = _arr(xs[0]).shape[0]
    np_, nx = len(params), len(xs)
    ct_idx = [i for i, c in enumerate(cts) if c is not None]
    ct_arrs = [cts[i] for i in ct_idx]
    nct = len(ct_arrs)
    nds = len(dx_stores)

    def body(*refs):
        p_refs = refs[:np_]
        x_refs = refs[np_:np_ + nx]
        c_refs = refs[np_ + nx:np_ + nx + nct]
        d_refs = refs[np_ + nx + nct:np_ + nx + nct + nds]
        g_refs = refs[np_ + nx + nct + nds:]
        pv = [r[...].astype(F32) for r in p_refs]
        xv = [r[...].astype(F32) for r in x_refs]
        outs, vjp = jax.vjp(lambda *a: tuple(fn(*a)), *pv, *xv)
        ct_full = [jnp.zeros_like(o) for o in outs]
        for i, r in zip(ct_idx, c_refs):
            ct_full[i] = r[...].astype(F32)
        grads = vjp(tuple(ct_full))
        for (xi, dt), d_ref in zip(dx_stores, d_refs):
            d_ref[...] = grads[np_ + xi].astype(dt)
        step = pl.program_id(0)
        for j, g_ref in enumerate(g_refs):
            @pl.when(step == 0)
            def _(g_ref=g_ref, j=j):
                g_ref[...] = grads[j]

            @pl.when(step > 0)
            def _(g_ref=g_ref, j=j):
                g_ref[...] += grads[j]

    out_shape = tuple(jax.ShapeDtypeStruct((t, _width(xs[xi])), dt) for xi, dt in dx_stores) + \
        tuple(jax.ShapeDtypeStruct(p.shape, F32) for p in params)
    out_specs = tuple(pl.BlockSpec((rows, _width(xs[xi])), lambda i: (i, 0)) for xi, _ in dx_stores) + \
        tuple(_full_spec(p) for p in params)
    return pl.pallas_call(
        body, name=name, out_shape=out_shape, grid=(t // rows,),
        in_specs=[_full_spec(p) for p in params] + [_row_spec(x, rows) for x in xs] + [_row_spec(c, rows) for c in ct_arrs],
        out_specs=out_specs,
        compiler_params=_cparams(("arbitrary",)),
    )(*params, *[_arr(x) for x in xs], *[_arr(c) for c in ct_arrs])


def _rms(x, g):
    return x * lax.rsqrt(jnp.mean(x * x, axis=-1, keepdims=True) + EPS) * g


def _sigmoid(x):
    return 1.0 / (1.0 + jnp.exp(-x))


def _silu(x):
    return x * _sigmoid(x)


def _gelu(x):
    return 0.5 * x * (1.0 + lax.erf(x * (1.0 / math.sqrt(2.0))))


def stage_norm(g, x):
    return x, _rms(x, g)


def stage_res_norm(g, x, y):
    xn = x + y
    return xn, _rms(xn, g)


def stage_res_bias_norm(b, g, x, y):
    xn = x + y + b
    return xn, _rms(xn, g)


def stage_glu(ba, bb, za, zb):
    return ((za + ba) * _sigmoid(zb + bb),)


def stage_ln_silu(dw_b, ln_g, ln_b, cv):
    z = cv + dw_b
    mu = jnp.mean(z, axis=-1, keepdims=True)
    zc = z - mu
    y = zc * lax.rsqrt(jnp.mean(zc * zc, axis=-1, keepdims=True) + EPS) * ln_g + ln_b
    return (_silu(y),)


CONV_ROWS = 128
CONV_COLS = 256
SUBLANES = 8


def _halo(k):
    return SUBLANES * ((k - 1 + SUBLANES - 1) // SUBLANES)


def _taps_by_roll(k):
    out = {}
    for s in range(k):
        out.setdefault(s % SUBLANES, []).append((s // SUBLANES, s))
    return out


def _shifted_down(win, k, r):
    halo, n = _halo(k), win.shape[0]
    segs = {}
    for b, lst in _taps_by_roll(k).items():
        rolled = win if b == 0 else pltpu.roll(win, b, axis=0)
        for a, s in lst:
            segs[s] = rolled[halo - SUBLANES * a: halo - SUBLANES * a + r]
    return segs


def _shifted_up(win, k, r):
    n = win.shape[0]
    segs = {}
    for b, lst in _taps_by_roll(k).items():
        rolled = win if b == 0 else pltpu.roll(win, n - b, axis=0)
        for a, s in lst:
            segs[s] = rolled[SUBLANES * a: SUBLANES * a + r]
    return segs


def _for_blocks(nblk, fn):
    fn(0, True, nblk == 1)
    if nblk > 2:
        def step(i, c):
            fn(i, False, False)
            return c
        lax.fori_loop(1, nblk - 1, step, 0)
    if nblk > 1:
        fn(nblk - 1, False, True)


def _base(i, r):
    return i * r if isinstance(i, int) else pl.multiple_of(i * r, r)


def _win_top(ref, i, first, r, halo):
    if first:
        return jnp.concatenate([jnp.zeros((halo, ref.shape[1]), F32), ref[pl.ds(0, r), :]], axis=0)
    base = _base(i, r)
    return ref[pl.ds(base - halo, r + halo), :]


def _win_bottom(ref, i, last, r, halo):
    base = _base(i, r)
    if last:
        return jnp.concatenate([ref[pl.ds(base, r), :], jnp.zeros((halo, ref.shape[1]), F32)], axis=0)
    return ref[pl.ds(base, r + halo), :]


def conv_fwd(x, w, *, mode, name, g=None):
    t = x.shape[0]
    k, c = w.shape
    r, cb, halo = min(CONV_ROWS, t), min(CONV_COLS, c), _halo(k)
    nblk = t // r

    def body(*refs):
        if mode == "silu_bwd":
            x_ref, w_ref, g_ref, o_ref = refs
        else:
            x_ref, w_ref, o_ref = refs

        def blk(i, first, last):
            segs = _shifted_down(_win_top(x_ref, i, first, r, halo), k, r)
            acc = None
            for s in range(k):
                term = w_ref[pl.ds(k - 1 - s, 1), :] * segs[s]
                acc = term if acc is None else acc + term
            base = _base(i, r)
            if mode == "silu":
                acc = _silu(acc)
            elif mode == "silu_bwd":
                sg = _sigmoid(acc)
                acc = g_ref[pl.ds(base, r), :] * (sg * (1.0 + acc * (1.0 - sg)))
            o_ref[pl.ds(base, r), :] = acc

        _for_blocks(nblk, blk)

    col = pl.BlockSpec((t, cb), lambda j: (0, j))
    in_specs = [col, pl.BlockSpec((k, cb), lambda j: (0, j))] + ([col] if mode == "silu_bwd" else [])
    args = [x, w] + ([g] if mode == "silu_bwd" else [])
    return pl.pallas_call(
        body, name=name, out_shape=jax.ShapeDtypeStruct((t, c), F32), grid=(c // cb,),
        in_specs=in_specs, out_specs=col, compiler_params=_cparams(("parallel",)),
    )(*args)


def conv_bwd(x, w, dy, *, name):
    t = x.shape[0]
    k, c = w.shape
    r, cb, halo = min(CONV_ROWS, t), min(CONV_COLS, c), _halo(k)
    nblk = t // r

    def body(x_ref, w_ref, dy_ref, dx_ref, dw_ref):
        dw_ref[...] = jnp.zeros_like(dw_ref)

        def blk(i, first, last):
            base = _base(i, r)
            up = _shifted_up(_win_bottom(dy_ref, i, last, r, halo), k, r)
            down = _shifted_down(_win_top(x_ref, i, first, r, halo), k, r)
            dyb = up[0]
            acc = None
            for s in range(k):
                term = w_ref[pl.ds(k - 1 - s, 1), :] * up[s]
                acc = term if acc is None else acc + term
                dw_ref[pl.ds(k - 1 - s, 1), :] += jnp.sum(down[s] * dyb, axis=0, keepdims=True)
            dx_ref[pl.ds(base, r), :] = acc

        _for_blocks(nblk, blk)

    col = pl.BlockSpec((t, cb), lambda j: (0, j))
    wsp = pl.BlockSpec((k, cb), lambda j: (0, j))
    return pl.pallas_call(
        body, name=name,
        out_shape=(jax.ShapeDtypeStruct((t, c), F32), jax.ShapeDtypeStruct((k, c), F32)), grid=(c // cb,),
        in_specs=[col, wsp, col], out_specs=(col, wsp), compiler_params=_cparams(("parallel",)),
    )(x, w, dy)


_DIMS = {"nn": (((1,), (0,)), ((), ())), "nt": (((1,), (1,)), ((), ())), "tn": (((0,), (0,)), ((), ()))}


def _mxu(a, b, mode):
    return lax.dot_general(a, b, _DIMS[mode], preferred_element_type=F32)


def _split(x):
    hi = x.astype(BF16)
    return hi, (x - hi.astype(F32)).astype(BF16)


def _dot_raw(a, b, mode, prec):
    if prec == "bf16":
        return _mxu(a.astype(BF16), b.astype(BF16), mode)
    if prec == "x3":
        ah, al = _split(a)
        bh, bl = _split(b)
        return _mxu(ah, bh, mode) + (_mxu(ah, bl, mode) + _mxu(al, bh, mode))
    if prec == "x3r":
        bh, bm = _split(b)
        bl = (b - bh.astype(F32) - bm.astype(F32)).astype(BF16)
        ah = a.astype(BF16)
        return _mxu(ah, bh, mode) + (_mxu(ah, bm, mode) + _mxu(ah, bl, mode))
    raise ValueError(prec)


@functools.lru_cache(maxsize=None)
def _dot_fn(mode, prec):
    bprec = "x3" if prec == "x3r" else prec

    @jax.custom_vjp
    def f(a, b):
        return _dot_raw(a, b, mode, prec)

    def fwd(a, b):
        return _dot_raw(a, b, mode, prec), (a, b)

    def bwd(res, ct):
        a, b = res
        if mode == "nn":
            return _dot_raw(ct, b, "nt", bprec), _dot_raw(a, ct, "tn", bprec)
        if mode == "nt":
            return _dot_raw(ct, b, "nn", bprec), _dot_raw(ct, a, "tn", bprec)
        return _dot_raw(b, ct, "nt", bprec), _dot_raw(a, ct, "nn", bprec)

    f.defvjp(fwd, bwd)
    return f


def _dot(a, b, mode="nn", prec="bf16"):
    return _dot_fn(mode, prec)(a, b)


def _inv_product(l):
    n = l.shape[0]
    eye = (lax.broadcasted_iota(jnp.int32, (n, n), 0) == lax.broadcasted_iota(jnp.int32, (n, n), 1)).astype(F32)
    p = eye - l
    pw = l
    for _ in range(5):
        pw = _dot_raw(pw, pw, "nn", "x3")
        p = _dot_raw(p, eye + pw, "nn", "x3")
    return p


@jax.custom_vjp
def _inv_unit_lower(l, t_saved):
    return t_saved


def _inv_fwd(l, t_saved):
    return t_saved, t_saved


def _inv_bwd(t, ct):
    tmp = _dot_raw(t, ct, "tn", "x3")
    return -_dot_raw(tmp, t, "nt", "x3"), jnp.zeros_like(t)


_inv_unit_lower.defvjp(_inv_fwd, _inv_bwd)


def _softplus(x):
    pos = x > 0
    return jnp.where(pos, x, 0.0) + jnp.log(1.0 + jnp.exp(jnp.where(pos, -x, x)))


def _l2n(x):
    return x * lax.rsqrt(jnp.sum(x * x, axis=-1, keepdims=True) + EPS)


def delta_pair(s0, qc, kc, vc, z, braw, araw, alog, dtb, onorm, t_saved=None):
    n = PAIR
    assert qc.shape == (n, HEAD_DIM) and n == HEAD_DIM
    ri = lax.broadcasted_iota(jnp.int32, (n, n), 0)
    ci = lax.broadcasted_iota(jnp.int32, (n, n), 1)
    same = (ri // CHUNK) == (ci // CHUNK)
    tri = same & (ci <= ri)
    same_f, tri_f = same.astype(F32), tri.astype(F32)
    strict_f = (same & (ci < ri)).astype(F32)
    m0 = (lax.broadcasted_iota(jnp.int32, (n, 1), 0) < CHUNK).astype(F32)
    m1 = 1.0 - m0

    q = _l2n(qc) * (HEAD_DIM ** -0.5)
    k = _l2n(kc)
    beta = _sigmoid(braw)
    g = -jnp.exp(alog) * _softplus(araw + dtb)
    gb = jnp.broadcast_to(g, (n, n))
    gc = _dot(tri_f, gb, "nn", "x3r")
    gtot = _dot(same_f, gb, "nn", "x3r")
    decay = jnp.exp(jnp.where(tri, gc - gc.T, -1e30))
    eg = jnp.exp(gc)
    kb, vb = k * beta, vc * beta
    l = _dot(kb, k, "nt") * decay * strict_f
    if t_saved is None:
        tinv = _inv_product(l)
    else:
        tinv = _inv_unit_lower(l, t_saved)
    u = _dot(tinv, vb, "nn", "x3")
    w = _dot(tinv, kb * eg, "nn", "x3")
    attn = _dot(q, k, "nt") * decay
    q_dec = q * eg
    k_tail = k * jnp.exp(gtot - gc)
    gl0 = jnp.exp(jnp.sum(gb * m0, axis=0, keepdims=True))
    gl1 = jnp.exp(jnp.sum(gb * m1, axis=0, keepdims=True))

    vn0 = m0 * (u - _dot(w, s0))
    s1 = s0 * gl0 + _dot(k_tail, vn0, "tn")
    vn1 = m1 * (u - _dot(w, s1))
    o = m0 * _dot(q_dec, s0) + m1 * _dot(q_dec, s1) + _dot(attn, vn0 + vn1)
    s2 = s1 * gl1 + _dot(k_tail, vn1, "tn")

    on = o * lax.rsqrt(jnp.mean(o * o, axis=-1, keepdims=True) + EPS) * onorm
    return on * _silu(z), s2, tinv


def delta_fwd(qkv, pm, z_blk0, bcol, acol, alog, dtb, onorm, cat_width, *, name):
    t = qkv.shape[0]
    h = bcol.shape[0]
    npair = t // PAIR
    blk = lambda off: pl.BlockSpec((PAIR, HEAD_DIM), lambda hh, p, _o=off: (p, _o + hh))
    vec = pl.BlockSpec((None, PAIR, 1), lambda hh, p: (hh, p, 0))
    sc = pl.BlockSpec((None, 1, 1), lambda hh, p: (hh, 0, 0))
    mat = pl.BlockSpec((None, None, PAIR, HEAD_DIM), lambda hh, p: (hh, p, 0, 0))

    def body(q_ref, k_ref, v_ref, z_ref, b_ref, a_ref, al_ref, dt_ref, on_ref, o_ref, st_ref, ti_ref, s_scr):
        @pl.when(pl.program_id(1) == 0)
        def _():
            s_scr[...] = jnp.zeros_like(s_scr)

        s0 = s_scr[...]
        st_ref[...] = s0
        out, s2, tinv = delta_pair(s0, q_ref[...], k_ref[...], v_ref[...], z_ref[...], b_ref[...], a_ref[...],
                                   al_ref[...], dt_ref[...], on_ref[...])
        o_ref[...] = out.astype(o_ref.dtype)
        ti_ref[...] = tinv
        s_scr[...] = s2

    return pl.pallas_call(
        body, name=name, grid=(h, npair),
        out_shape=(jax.ShapeDtypeStruct((t, cat_width), BF16),
                   jax.ShapeDtypeStruct((h, npair, PAIR, HEAD_DIM), F32),
                   jax.ShapeDtypeStruct((h, npair, PAIR, PAIR), F32)),
        in_specs=[blk(0), blk(h), blk(2 * h), blk(z_blk0), vec, vec, sc, sc,
                  pl.BlockSpec((1, HEAD_DIM), lambda hh, p: (0, 0))],
        out_specs=(blk(0), mat, mat),
        scratch_shapes=[pltpu.VMEM((HEAD_DIM, HEAD_DIM), F32)],
        compiler_params=_cparams(("parallel", "arbitrary")),
    )(qkv, qkv, qkv, pm, bcol, acol, alog, dtb, onorm)


def delta_bwd(qkv, pm, z_blk0, bcol, acol, alog, dtb, onorm, states, tinvs, dcat, *, name):
    t = qkv.shape[0]
    h = bcol.shape[0]
    npair = t // PAIR
    rev = lambda p: npair - 1 - p
    blk = lambda off: pl.BlockSpec((PAIR, HEAD_DIM), lambda hh, p, _o=off: (rev(p), _o + hh))
    vec = pl.BlockSpec((None, PAIR, 1), lambda hh, p: (hh, rev(p), 0))
    sc = pl.BlockSpec((None, 1, 1), lambda hh, p: (hh, 0, 0))
    mat = pl.BlockSpec((None, None, PAIR, HEAD_DIM), lambda hh, p: (hh, rev(p), 0, 0))
    onsp = pl.BlockSpec((1, HEAD_DIM), lambda hh, p: (0, 0))

    def body(q_ref, k_ref, v_ref, z_ref, b_ref, a_ref, al_ref, dt_ref, on_ref, st_ref, ti_ref, dc_ref,
             dq_ref, dk_ref, dv_ref, dz_ref, db_ref, da_ref, dal_ref, ddt_ref, don_ref, ds_scr):
        hh, p = pl.program_id(0), pl.program_id(1)

        @pl.when(p == 0)
        def _():
            ds_scr[...] = jnp.zeros_like(ds_scr)
            dal_ref[...] = jnp.zeros_like(dal_ref)
            ddt_ref[...] = jnp.zeros_like(ddt_ref)

        @pl.when((p == 0) & (hh == 0))
        def _():
            don_ref[...] = jnp.zeros_like(don_ref)

        tsv = ti_ref[...]
        fn = lambda s0, qc, kc, vc, z, br, ar, al, dt, on: delta_pair(s0, qc, kc, vc, z, br, ar, al, dt, on, tsv)[:2]
        _, vjp = jax.vjp(fn, st_ref[...], q_ref[...], k_ref[...], v_ref[...], z_ref[...], b_ref[...], a_ref[...],
                         al_ref[...], dt_ref[...], on_ref[...])
        ds0, dq, dk, dv, dz, db, da, dal, ddt, don = vjp((dc_ref[...], ds_scr[...]))
        ds_scr[...] = ds0
        dq_ref[...] = dq
        dk_ref[...] = dk
        dv_ref[...] = dv
        dz_ref[...] = dz
        db_ref[...] = db
        da_ref[...] = da
        dal_ref[...] += dal
        ddt_ref[...] += ddt
        don_ref[...] += don

    hd = h * HEAD_DIM
    return pl.pallas_call(
        body, name=name, grid=(h, npair),
        out_shape=(jax.ShapeDtypeStruct((t, hd), F32), jax.ShapeDtypeStruct((t, hd), F32), jax.ShapeDtypeStruct((t, hd), F32),
                   jax.ShapeDtypeStruct((t, hd), F32),
                   jax.ShapeDtypeStruct((h, t, 1), F32), jax.ShapeDtypeStruct((h, t, 1), F32),
                   jax.ShapeDtypeStruct((h, 1, 1), F32), jax.ShapeDtypeStruct((h, 1, 1), F32),
                   jax.ShapeDtypeStruct((1, HEAD_DIM), F32)),
        in_specs=[blk(0), blk(h), blk(2 * h), blk(z_blk0), vec, vec, sc, sc, onsp, mat, mat, blk(0)],
        out_specs=(blk(0), blk(0), blk(0), blk(0), vec, vec, sc, sc, onsp),
        scratch_shapes=[pltpu.VMEM((HEAD_DIM, HEAD_DIM), F32)],
        compiler_params=_cparams(("arbitrary", "arbitrary")),
    )(qkv, qkv, qkv, pm, bcol, acol, alog, dtb, onorm, states, tinvs, dcat)


def gmlp_block(ln_g, ln_b, w, bcol, u_raw, v_raw):
    n = w.shape[0]
    ri = lax.broadcasted_iota(jnp.int32, (n, n), 0)
    ci = lax.broadcasted_iota(jnp.int32, (n, n), 1)
    mask = ((ci // CHUNK) <= (ri // CHUNK)).astype(F32)
    vg = _gelu(v_raw)
    vc = vg - jnp.mean(vg, axis=-1, keepdims=True)
    vgn = vc * lax.rsqrt(jnp.mean(vc * vc, axis=-1, keepdims=True) + EPS) * ln_g + ln_b
    return _gelu(u_raw) * (_dot(w * mask, vgn) + bcol)


def gmlp_fwd(pm, u_blk0, v_blk0, ln_g, ln_b, w_s, bcol, cat, cat_blk0, *, name):
    t = pm.shape[0]
    g = w_s.shape[0]
    blk = lambda off: pl.BlockSpec((HEAD_DIM, HEAD_DIM), lambda gg, m, _o=off: (m, _o + gg))
    row = pl.BlockSpec((None, 1, HEAD_DIM), lambda gg, m: (gg, 0, 0))

    def body(u_ref, v_ref, lg_ref, lb_ref, w_ref, b_ref, cat_in, o_ref):
        del cat_in
        o_ref[...] = gmlp_block(lg_ref[...], lb_ref[...], w_ref[...], b_ref[...], u_ref[...], v_ref[...]).astype(o_ref.dtype)

    return pl.pallas_call(
        body, name=name, grid=(g, t // HEAD_DIM),
        out_shape=jax.ShapeDtypeStruct(cat.shape, cat.dtype),
        in_specs=[blk(u_blk0), blk(v_blk0), row, row,
                  pl.BlockSpec((None, HEAD_DIM, HEAD_DIM), lambda gg, m: (gg, 0, 0)),
                  pl.BlockSpec((None, HEAD_DIM, 1), lambda gg, m: (gg, 0, 0)),
                  pl.BlockSpec(memory_space=pl.ANY)],
        out_specs=blk(cat_blk0),
        input_output_aliases={6: 0},
        compiler_params=_cparams(("parallel", "arbitrary")),
    )(pm, pm, ln_g, ln_b, w_s, bcol, cat)


def gmlp_bwd(pm, u_blk0, v_blk0, ln_g, ln_b, w_s, bcol, dcat, cat_blk0, *, name):
    t = pm.shape[0]
    g = w_s.shape[0]
    blk = lambda off: pl.BlockSpec((HEAD_DIM, HEAD_DIM), lambda gg, m, _o=off: (m, _o + gg))
    row = pl.BlockSpec((None, 1, HEAD_DIM), lambda gg, m: (gg, 0, 0))
    wsp = pl.BlockSpec((None, HEAD_DIM, HEAD_DIM), lambda gg, m: (gg, 0, 0))
    bsp = pl.BlockSpec((None, HEAD_DIM, 1), lambda gg, m: (gg, 0, 0))

    def body(u_ref, v_ref, lg_ref, lb_ref, w_ref, b_ref, dc_ref, du_ref, dv_ref, dlg_ref, dlb_ref, dw_ref, db_ref):
        _, vjp = jax.vjp(gmlp_block, lg_ref[...], lb_ref[...], w_ref[...], b_ref[...], u_ref[...], v_ref[...])
        dlg, dlb, dw, db, du, dv = vjp(dc_ref[...])
        du_ref[...] = du
        dv_ref[...] = dv
        first = pl.program_id(1) == 0
        for ref, val in ((dlg_ref, dlg), (dlb_ref, dlb), (dw_ref, dw), (db_ref, db)):
            @pl.when(first)
            def _(ref=ref, val=val):
                ref[...] = val

            @pl.when(jnp.logical_not(first))
            def _(ref=ref, val=val):
                ref[...] += val

    gw = g * HEAD_DIM
    return pl.pallas_call(
        body, name=name, grid=(g, t // HEAD_DIM),
        out_shape=(jax.ShapeDtypeStruct((t, gw), F32), jax.ShapeDtypeStruct((t, gw), F32),
                   jax.ShapeDtypeStruct(ln_g.shape, F32), jax.ShapeDtypeStruct(ln_b.shape, F32),
                   jax.ShapeDtypeStruct(w_s.shape, F32), jax.ShapeDtypeStruct(bcol.shape, F32)),
        in_specs=[blk(u_blk0), blk(v_blk0), row, row, wsp, bsp, blk(cat_blk0)],
        out_specs=(blk(0), blk(0), row, row, wsp, bsp),
        compiler_params=_cparams(("parallel", "arbitrary")),
    )(pm, pm, ln_g, ln_b, w_s, bcol, dcat)


def loss_head(g, x, r, tgt, *, name, rows=ROWS):
    t, d = x.shape

    def body(g_ref, x_ref, r_ref, t_ref, l_ref, dx_ref, dxb_ref, dg_ref):
        y, vjp = jax.vjp(lambda gg, xx: _rms(xx, gg), g_ref[...], x_ref[...] + r_ref[...])
        e = y - t_ref[...]
        part = (0.5 / d) * jnp.sum(jnp.sum(e * e, axis=1, keepdims=True), axis=0, keepdims=True)
        dg, dx = vjp(e * (1.0 / d))
        dx_ref[...] = dx
        dxb_ref[...] = dx.astype(BF16)
        first = pl.program_id(0) == 0

        @pl.when(first)
        def _():
            l_ref[...] = part
            dg_ref[...] = dg

        @pl.when(jnp.logical_not(first))
        def _():
            l_ref[...] += part
            dg_ref[...] += dg

    rs = pl.BlockSpec((rows, d), lambda i: (i, 0))
    gs = pl.BlockSpec((1, d), lambda i: (0, 0))
    return pl.pallas_call(
        body, name=name, grid=(t // rows,),
        out_shape=(jax.ShapeDtypeStruct((1, 1), F32), jax.ShapeDtypeStruct((t, d), F32),
                   jax.ShapeDtypeStruct((t, d), BF16), jax.ShapeDtypeStruct((1, d), F32)),
        in_specs=[gs, rs, rs, rs],
        out_specs=(pl.BlockSpec((1, 1), lambda i: (0, 0)), rs, rs, gs),
        compiler_params=_cparams(("arbitrary",)),
    )(g, x, r, tgt)


def adamw(w, gs, m, v, *, name):
    nl, r, c = w.shape
    assert len(gs) == nl
    tr = _pick(r, (256, 128, 64, 32, 16, 8))
    k1 = 1.0 - ADAM_B1 ** ADAM_STEP
    k2 = 1.0 - ADAM_B2 ** ADAM_STEP

    def body(*refs):
        w_ref, m_ref, v_ref = refs[0], refs[1], refs[2]
        g_refs = refs[3:3 + nl]
        go_ref, d_ref, mo_ref, vo_ref = refs[3 + nl:]
        gg = g_refs[0][...]
        for li in range(1, nl):
            gg = jnp.where(pl.program_id(0) == li, g_refs[li][...], gg)
        mn = ADAM_B1 * m_ref[...] + (1.0 - ADAM_B1) * gg
        vn = ADAM_B2 * v_ref[...] + (1.0 - ADAM_B2) * (gg * gg)
        go_ref[...] = gg
        d_ref[...] = -ADAM_LR * ((mn / k1) / (jnp.sqrt(vn / k2) + ADAM_EPS) + ADAM_WD * w_ref[...])
        mo_ref[...] = mn
        vo_ref[...] = vn

    sp = pl.BlockSpec((None, tr, c), lambda l, i: (l, i, 0))
    gsp = pl.BlockSpec((tr, c), lambda l, i: (i, 0))
    sds = jax.ShapeDtypeStruct((nl, r, c), F32)
    return pl.pallas_call(
        body, name=name, grid=(nl, r // tr), out_shape=(sds, sds, sds, sds),
        in_specs=[sp, sp, sp] + [gsp] * nl, out_specs=(sp, sp, sp, sp),
        compiler_params=_cparams(("parallel", "parallel")),
    )(w, m, v, *gs)


MESH = pl.DeviceIdType.MESH
ANY = pl.BlockSpec(memory_space=pl.ANY)


def _me():
    return lax.axis_index("x"), lax.axis_index("y"), lax.axis_index("c")


def _other_chips(x, y):
    return [(1 - x, y), (x, 1 - y), (1 - x, 1 - y)]


def gather_shards(arrs, *, name):
    n = len(arrs)

    def body(*refs):
        ins, outs = refs[:n], refs[n:2 * n]
        send_sems, recv_sems, local_sems = refs[2 * n:]
        x, y, c = _me()
        chip = 2 * x + y
        sib = (x, y, 1 - c)
        chips = _other_chips(x, y)

        def rc(i, k, src, dst, to):
            return pltpu.make_async_remote_copy(src_ref=src, dst_ref=dst, send_sem=send_sems.at[6 * i + k],
                                                recv_sem=recv_sems.at[6 * i + k], device_id=to, device_id_type=MESH)

        def half(i, which):
            hrows = arrs[i].shape[0] // 2
            return pl.ds(which * hrows, hrows)

        local = [pltpu.make_async_copy(ins[i], outs[i].at[chip], local_sems.at[i]) for i in range(n)]
        for cp in local:
            cp.start()
        started = []
        for i in range(n):
            for j, ch in enumerate(chips):
                cp = rc(i, j, ins[i].at[half(i, c)], outs[i].at[chip, half(i, c)], (ch[0], ch[1], c))
                cp.start()
                started.append(cp)
        for i in range(n):
            for j, ch in enumerate(chips):
                sc = 2 * ch[0] + ch[1]
                landed = outs[i].at[sc, half(i, c)]
                rc(i, j, ins[i].at[half(i, c)], landed, (ch[0], ch[1], c)).wait_recv()
                fw = rc(i, 3 + j, landed, landed, sib)
                fw.start()
                started.append(fw)
        for i in range(n):
            for j, ch in enumerate(chips):
                sc = 2 * ch[0] + ch[1]
                theirs = outs[i].at[sc, half(i, 1 - c)]
                rc(i, 3 + j, theirs, theirs, sib).wait_recv()
        for cp in started:
            cp.wait_send()
        for cp in local:
            cp.wait()

    return pl.pallas_call(
        body, name=name,
        out_shape=tuple(jax.ShapeDtypeStruct((N_CHIPS,) + a.shape, a.dtype) for a in arrs),
        in_specs=[ANY] * n, out_specs=tuple([ANY] * n),
        scratch_shapes=[pltpu.SemaphoreType.DMA((6 * n,)), pltpu.SemaphoreType.DMA((6 * n,)), pltpu.SemaphoreType.DMA((n,))],
        compiler_params=pltpu.CompilerParams(has_side_effects=True),
    )(*arrs)


def sibling_send_other_half(arrs, *, name):
    n = len(arrs)

    def body(*refs):
        ins, outs = refs[:n], refs[n:2 * n]
        send_sems, recv_sems = refs[2 * n:]
        x, y, c = _me()
        cps = []
        for i in range(n):
            hrows = arrs[i].shape[1] // 2
            cps.append(pltpu.make_async_remote_copy(
                src_ref=ins[i].at[:, pl.ds((1 - c) * hrows, hrows), :], dst_ref=outs[i], send_sem=send_sems.at[i],
                recv_sem=recv_sems.at[i], device_id=(x, y, 1 - c), device_id_type=MESH))
        for cp in cps:
            cp.start()
        for cp in cps:
            cp.wait_recv()
        for cp in cps:
            cp.wait_send()

    return pl.pallas_call(
        body, name=name,
        out_shape=tuple(jax.ShapeDtypeStruct((a.shape[0], a.shape[1] // 2, a.shape[2]), a.dtype) for a in arrs),
        in_specs=[ANY] * n, out_specs=tuple([ANY] * n),
        scratch_shapes=[pltpu.SemaphoreType.DMA((n,)), pltpu.SemaphoreType.DMA((n,))],
        compiler_params=pltpu.CompilerParams(has_side_effects=True),
    )(*arrs)


def chip_scatter(arrs, *, name):
    n = len(arrs)

    def body(*refs):
        ins, outs = refs[:n], refs[n:2 * n]
        send_sems, recv_sems = refs[2 * n:]
        x, y, c = _me()
        cps = []
        for i in range(n):
            for j, ch in enumerate(_other_chips(x, y)):
                cps.append(pltpu.make_async_remote_copy(
                    src_ref=ins[i].at[2 * ch[0] + ch[1]], dst_ref=outs[i].at[j], send_sem=send_sems.at[3 * i + j],
                    recv_sem=recv_sems.at[3 * i + j], device_id=(ch[0], ch[1], c), device_id_type=MESH))
        for cp in cps:
            cp.start()
        for cp in cps:
            cp.wait_recv()
        for cp in cps:
            cp.wait_send()

    return pl.pallas_call(
        body, name=name,
        out_shape=tuple(jax.ShapeDtypeStruct((3,) + a.shape[1:], a.dtype) for a in arrs),
        in_specs=[ANY] * n, out_specs=tuple([ANY] * n),
        scratch_shapes=[pltpu.SemaphoreType.DMA((3 * n,)), pltpu.SemaphoreType.DMA((3 * n,))],
        compiler_params=pltpu.CompilerParams(has_side_effects=True),
    )(*arrs)


def sibling_join_halves(arrs, *, name):
    n = len(arrs)

    def body(*refs):
        ins, outs = refs[:n], refs[n:2 * n]
        send_sems, recv_sems, local_sems = refs[2 * n:]
        x, y, c = _me()
        cps, loc = [], []
        for i in range(n):
            hrows = arrs[i].shape[0]
            mine = outs[i].at[pl.ds(c * hrows, hrows)]
            loc.append(pltpu.make_async_copy(ins[i], mine, local_sems.at[i]))
            cps.append(pltpu.make_async_remote_copy(src_ref=ins[i], dst_ref=mine, send_sem=send_sems.at[i],
                                                    recv_sem=recv_sems.at[i], device_id=(x, y, 1 - c), device_id_type=MESH))
        for cp in loc + cps:
            cp.start()
        for i in range(n):
            hrows = arrs[i].shape[0]
            theirs = outs[i].at[pl.ds((1 - c) * hrows, hrows)]
            pltpu.make_async_remote_copy(src_ref=ins[i], dst_ref=theirs, send_sem=send_sems.at[i], recv_sem=recv_sems.at[i],
                                         device_id=(x, y, 1 - c), device_id_type=MESH).wait_recv()
        for cp in cps:
            cp.wait_send()
        for cp in loc:
            cp.wait()

    return pl.pallas_call(
        body, name=name,
        out_shape=tuple(jax.ShapeDtypeStruct((2 * a.shape[0], a.shape[1]), a.dtype) for a in arrs),
        in_specs=[ANY] * n, out_specs=tuple([ANY] * n),
        scratch_shapes=[pltpu.SemaphoreType.DMA((n,)), pltpu.SemaphoreType.DMA((n,)), pltpu.SemaphoreType.DMA((n,))],
        compiler_params=pltpu.CompilerParams(has_side_effects=True),
    )(*arrs)


def pre_reduce(gd, sib, who, *, name):
    _, r, c = gd.shape
    h = r // 2
    tr = _pick(h, (256, 128, 64, 32, 16, 8))
    nrb = h // tr

    def body(who_ref, gd_ref, sib_ref, pb_ref, own_ref):
        p = gd_ref[...] + sib_ref[...]
        pb_ref[...] = p.astype(BF16)

        @pl.when(pl.program_id(1) == who_ref[1])
        def _():
            own_ref[...] = p

    return pl.pallas_call(
        body, name=name,
        out_shape=(jax.ShapeDtypeStruct((N_CHIPS, h, c), BF16), jax.ShapeDtypeStruct((h, c), F32)),
        grid_spec=pltpu.PrefetchScalarGridSpec(
            num_scalar_prefetch=1, grid=(nrb, N_CHIPS),
            in_specs=[pl.BlockSpec((None, tr, c), lambda i, s, w: (s, w[0] * nrb + i, 0)),
                      pl.BlockSpec((None, tr, c), lambda i, s, w: (s, i, 0))],
            out_specs=(pl.BlockSpec((None, tr, c), lambda i, s, w: (s, i, 0)),
                       pl.BlockSpec((tr, c), lambda i, s, w: (i, 0)))),
        compiler_params=_cparams(("parallel", "arbitrary")),
    )(who, gd, sib)


def final_reduce(own, rcv, *, name):
    h, c = own.shape
    tr = _pick(h, (256, 128, 64, 32, 16, 8))

    def body(own_ref, rcv_ref, o_ref):
        acc = own_ref[...]
        for j in range(3):
            acc = acc + rcv_ref[j].astype(F32)
        o_ref[...] = acc

    return pl.pallas_call(
        body, name=name, out_shape=jax.ShapeDtypeStruct((h, c), F32), grid=(h // tr,),
        in_specs=[pl.BlockSpec((tr, c), lambda i: (i, 0)), pl.BlockSpec((3, tr, c), lambda i: (0, i, 0))],
        out_specs=pl.BlockSpec((tr, c), lambda i: (i, 0)),
        compiler_params=_cparams(("parallel",)),
    )(own, rcv)


def sum8(a, *, name):
    _, r, c = a.shape
    tr = _pick(r, (256, 128, 64, 32, 16, 8))

    def body(a_ref, o_ref):
        acc = a_ref[0]
        for j in range(1, 8):
            acc = acc + a_ref[j]
        o_ref[...] = acc

    return pl.pallas_call(
        body, name=name, out_shape=jax.ShapeDtypeStruct((r, c), F32), grid=(r // tr,),
        in_specs=[pl.BlockSpec((8, tr, c), lambda i: (0, i, 0))],
        out_specs=pl.BlockSpec((tr, c), lambda i: (i, 0)),
        compiler_params=_cparams(("parallel",)),
    )(a)


def allgather8(a, *, name):
    def body(in_ref, out_ref, send_sems, recv_sems, local_sem):
        x, y, c = _me()
        me = 4 * x + 2 * y + c
        loc = pltpu.make_async_copy(in_ref, out_ref.at[me], local_sem)
        loc.start()
        cps = []
        for k in range(1, 8):
            fx, fy, fc = (k >> 2) & 1, (k >> 1) & 1, k & 1
            to = (x + fx - 2 * x * fx, y + fy - 2 * y * fy, c + fc - 2 * c * fc)
            cps.append(pltpu.make_async_remote_copy(src_ref=in_ref, dst_ref=out_ref.at[me], send_sem=send_sems.at[k - 1],
                                                    recv_sem=recv_sems.at[k - 1], device_id=to, device_id_type=MESH))
        for cp in cps:
            cp.start()
        for k in range(1, 8):
            fx, fy, fc = (k >> 2) & 1, (k >> 1) & 1, k & 1
            src = 4 * (x + fx - 2 * x * fx) + 2 * (y + fy - 2 * y * fy) + (c + fc - 2 * c * fc)
            pltpu.make_async_remote_copy(src_ref=in_ref, dst_ref=out_ref.at[src], send_sem=send_sems.at[k - 1],
                                         recv_sem=recv_sems.at[k - 1], device_id=(x, y, c), device_id_type=MESH).wait_recv()
        for cp in cps:
            cp.wait_send()
        loc.wait()

    return pl.pallas_call(
        body, name=name, out_shape=jax.ShapeDtypeStruct((8,) + a.shape, a.dtype),
        in_specs=[ANY], out_specs=ANY,
        scratch_shapes=[pltpu.SemaphoreType.DMA((7,)), pltpu.SemaphoreType.DMA((7,)), pltpu.SemaphoreType.DMA],
        compiler_params=pltpu.CompilerParams(has_side_effects=True),
    )(a)


def local_step(x, tgt, wb, ws):
    t, d = x.shape
    nh = ws["alog"].shape[0]
    ng = ws["w_s"].shape[0]
    z_blk0, u_blk0, v_blk0 = 3 * nh, 4 * nh, 4 * nh + ng
    mix_w = (nh + ng) * HEAD_DIM

    (h0,) = rows_fwd(stage_norm, [ws["e_norm"]], [x], [(1, BF16, d)], name="f_norm_e")
    pm = mm_nn(h0, wb["w_main"], name="f_proj_main")
    pba = mm_nn(h0, wb["w_ba"], name="f_proj_ba")
    bcol = pba[:, :nh].T.reshape(nh, t, 1)
    acol = pba[:, nh:2 * nh].T.reshape(nh, t, 1)
    qkv = conv_fwd(pm, ws["conv_w"], mode="silu", name="f_conv4")
    cat, states, tinvs = delta_fwd(qkv, pm, z_blk0, bcol, acol, ws["alog"], ws["dtb"], ws["onorm"], mix_w, name="f_delta")
    cat = gmlp_fwd(pm, u_blk0, v_blk0, ws["lng"], ws["lnb"], ws["w_s"], ws["bs"], cat, nh, name="f_gmlp")
    y0 = mm_nn(cat, wb["w_out"], name="f_out")
    x1, h1 = rows_fwd(stage_res_norm, [ws["f_norm0"]], [x, y0], [(0, F32, d), (1, BF16, d)], name="f_norm_f0")
    a1, s1 = mm_nn(h1, wb["w1_0"], name="f_mlp0_up", epilogue="relu2")
    y1 = mm_nn(s1, wb["w2_0"], name="f_mlp0_down")
    x2, h2 = rows_fwd(stage_res_norm, [ws["o_norm"]], [x1, y1], [(0, F32, d), (1, BF16, d)], name="f_norm_o")
    zz = mm_nn(h2, wb["pw1"], name="f_pw1")
    zparts = [(zz, d, 0), (zz, d, 1)]
    (gl,) = rows_fwd(stage_glu, [ws["b1a"], ws["b1b"]], zparts, [(0, F32, d)], name="f_glu")
    cv = conv_fwd(gl, ws["dw"], mode="plain", name="f_conv31")
    ln_params = [ws["dw_b"], ws["ln_g"], ws["ln_b"]]
    (sl,) = rows_fwd(stage_ln_silu, ln_params, [cv], [(0, BF16, d)], name="f_ln_silu")
    y2 = mm_nn(sl, wb["pw2"], name="f_pw2")
    x3, h3 = rows_fwd(stage_res_bias_norm, [ws["b2"], ws["f_norm1"]], [x2, y2], [(0, F32, d), (1, BF16, d)], name="f_norm_f1")
    a3, s3 = mm_nn(h3, wb["w1_1"], name="f_mlp1_up", epilogue="relu2")
    y3 = mm_nn(s3, wb["w2_1"], name="f_mlp1_down")
    loss, d4, d4b, g_final = loss_head(ws["final_norm"], x3, y3, tgt, name="loss_head")

    gb, gs = {}, {"final_norm": g_final}
    s_up = wb["w1_0"].shape[0]

    gb["w2_1"] = mm_tn(s3, d4b, groups=1, name="b_mlp1_down_w")
    dpre3 = mm_nt(d4b, wb["w2_1"], name="b_mlp1_down_x", mul=a3, out_dtype=BF16)
    gb["w1_1"] = mm_tn(h3, dpre3, groups=s_up, name="b_mlp1_up_w")
    dh3 = mm_nt(dpre3, wb["w1_1"], name="b_mlp1_up_x")
    d3, d3b, gs["b2"], gs["f_norm1"] = rows_bwd(stage_res_bias_norm, [ws["b2"], ws["f_norm1"]], [x2, y2], [d4, dh3],
                                                [(0, F32), (1, BF16)], name="b_norm_f1")
    gb["pw2"] = mm_tn(sl, d3b, groups=1, name="b_pw2_w")
    dsl = mm_nt(d3b, wb["pw2"], name="b_pw2_x")
    dcv, gs["dw_b"], gs["ln_g"], gs["ln_b"] = rows_bwd(stage_ln_silu, ln_params, [cv], [dsl], [(0, F32)], name="b_ln_silu")
    dgl, gs["dw"] = conv_bwd(gl, ws["dw"], dcv, name="b_conv31")
    dza, dzb, gs["b1a"], gs["b1b"] = rows_bwd(stage_glu, [ws["b1a"], ws["b1b"]], zparts, [dgl], [(0, BF16), (1, BF16)], name="b_glu")
    dzz = jnp.concatenate([dza, dzb], axis=1)
    gb["pw1"] = mm_tn(h2, dzz, groups=wb["pw1"].shape[0], name="b_pw1_w")
    dh2 = mm_nt(dzz, wb["pw1"], name="b_pw1_x")
    d2, d2b, gs["o_norm"] = rows_bwd(stage_res_norm, [ws["o_norm"]], [x1, y1], [d3, dh2], [(0, F32), (1, BF16)], name="b_norm_o")
    gb["w2_0"] = mm_tn(s1, d2b, groups=1, name="b_mlp0_down_w")
    dpre1 = mm_nt(d2b, wb["w2_0"], name="b_mlp0_down_x", mul=a1, out_dtype=BF16)
    gb["w1_0"] = mm_tn(h1, dpre1, groups=s_up, name="b_mlp0_up_w")
    dh1 = mm_nt(dpre1, wb["w1_0"], name="b_mlp0_up_x")
    d1, d1b, gs["f_norm0"] = rows_bwd(stage_res_norm, [ws["f_norm0"]], [x, y0], [d2, dh1], [(0, F32), (1, BF16)], name="b_norm_f0")
    gb["w_out"] = mm_tn(cat, d1b, groups=1, name="b_out_w")
    dcat = mm_nt(d1b, wb["w_out"], name="b_out_x")
    dq, dk, dv, dz, dbc, dac, gs["alog"], gs["dtb"], gs["onorm"] = delta_bwd(
        qkv, pm, z_blk0, bcol, acol, ws["alog"], ws["dtb"], ws["onorm"], states, tinvs, dcat, name="b_delta")
    du, dvg, gs["lng"], gs["lnb"], gs["w_s"], gs["bs"] = gmlp_bwd(
        pm, u_blk0, v_blk0, ws["lng"], ws["lnb"], ws["w_s"], ws["bs"], dcat, nh, name="b_gmlp")
    dconv = conv_fwd(pm, ws["conv_w"], mode="silu_bwd", g=jnp.concatenate([dq, dk, dv], axis=1), name="b_conv4_silu")
    dqkv, gs["conv_w"] = conv_bwd(pm, ws["conv_w"], dconv, name="b_conv4")
    dpm = jnp.concatenate([dqkv, dz, du, dvg], axis=1).astype(BF16)
    dba = jnp.concatenate([dbc.reshape(nh, t).T, dac.reshape(nh, t).T,
                           jnp.zeros((t, wb["w_ba"].shape[2] - 2 * nh), F32)], axis=1).astype(BF16)
    gb["w_main"] = mm_tn(h0, dpm, groups=1, name="b_proj_main_w")
    gb["w_ba"] = mm_tn(h0, dba, groups=1, name="b_proj_ba_w")
    dh0 = mm_nt(dpm, wb["w_main"], name="b_proj_main_x")
    dh0 = mm_nt(dba, wb["w_ba"], name="b_proj_ba_x", add=dh0)
    grad_x, gs["e_norm"] = rows_bwd(stage_norm, [ws["e_norm"]], [x], [d1, dh0], [(0, F32)], name="b_norm_e")
    return loss, grad_x, gb, gs


WEIGHTS = ["e_norm", "e_w_in", "e_conv_w", "e_a_log", "e_dt_bias", "e_o_norm", "e_ln_g", "e_ln_b", "e_w_s", "e_b_s", "e_w_out",
           "o_norm", "o_pw1", "o_pw1_b", "o_dw", "o_dw_b", "o_ln_g", "o_ln_b", "o_pw2", "o_pw2_b", "f_norm", "f_w1", "f_w2",
           "final_norm"]
BIG = ["e_w_in", "e_w_out", "o_pw1", "o_pw2", "f_w1", "f_w2"]
SMALL_SHARDED = ["e_conv_w", "o_norm", "o_pw1_b", "o_dw", "o_dw_b", "o_ln_g", "o_ln_b", "o_pw2_b"]
SMALL = [n for n in WEIGHTS if n not in BIG]
LANES = 128
PACK_ROWS = 16


def _pack(arrs):
    flat = jnp.concatenate([a.reshape(-1).astype(F32) for a in arrs])
    n = flat.shape[0]
    rows = -(-n // (LANES * PACK_ROWS)) * PACK_ROWS
    return jnp.pad(flat, (0, rows * LANES - n)).reshape(rows, LANES)


def _unpack(flat, shapes, lead=()):
    outs, off = [], 0
    for s in shapes:
        n = int(np.prod(s))
        outs.append(flat[..., off:off + n].reshape(lead + tuple(s)))
        off += n
    return outs


def _chip_major(g, s=N_CHIPS):
    k, n = g.shape
    return g.reshape(k, s, n // s).transpose(1, 0, 2)


def kernel(x, e_norm, e_w_in, e_conv_w, e_a_log, e_dt_bias, e_o_norm, e_ln_g, e_ln_b, e_w_s, e_b_s, e_w_out, o_norm, o_pw1, o_pw1_b, o_dw, o_dw_b, o_ln_g, o_ln_b, o_pw2, o_pw2_b, f_norm, f_w1, f_w2, final_norm, loss_target, m_e_norm, m_e_w_in, m_e_conv_w, m_e_a_log, m_e_dt_bias, m_e_o_norm, m_e_ln_g, m_e_ln_b, m_e_w_s, m_e_b_s, m_e_w_out, m_o_norm, m_o_pw1, m_o_pw1_b, m_o_dw, m_o_dw_b, m_o_ln_g, m_o_ln_b, m_o_pw2, m_o_pw2_b, m_f_norm, m_f_w1, m_f_w2, m_final_norm, v_e_norm, v_e_w_in, v_e_conv_w, v_e_a_log, v_e_dt_bias, v_e_o_norm, v_e_ln_g, v_e_ln_b, v_e_w_s, v_e_b_s, v_e_w_out, v_o_norm, v_o_pw1, v_o_pw1_b, v_o_dw, v_o_dw_b, v_o_ln_g, v_o_ln_b, v_o_pw2, v_o_pw2_b, v_f_norm, v_f_w1, v_f_w2, v_final_norm):
    a = dict(locals())
    xi, yi, ci = _me()
    chip = 2 * xi + yi
    who = jnp.stack([ci, chip]).astype(jnp.int32)
    t, d = x.shape[1], x.shape[2]
    nh, ng = e_a_log.shape[1], e_w_s.shape[1]
    n_qkv, n_av, n_bw = 3 * nh * HEAD_DIM, nh * HEAD_DIM, ng * HEAD_DIM
    in_cols = n_qkv + n_av + 2 * nh + 2 * n_bw
    c_ba = n_qkv + n_av

    big_local = [e_w_in[0], e_w_out[0], o_pw1[0], o_pw2[0], f_w1[0], f_w1[1], f_w2[0], f_w2[1]]
    small_local = [a[n] for n in SMALL_SHARDED]
    gathered = gather_shards([w.astype(BF16) for w in big_local] + [_pack(small_local)], name="gather_weights")
    g_in, g_out, g_pw1, g_pw2, g_w1_0, g_w1_1, g_w2_0, g_w2_1, g_small = gathered
    w_in = g_in.transpose(1, 0, 2).reshape(d, in_cols)
    wb = {
        "w_main": jnp.concatenate([w_in[:, :c_ba], w_in[:, c_ba + 2 * nh:]], axis=1)[None],
        "w_ba": jnp.pad(w_in[:, c_ba:c_ba + 2 * nh], ((0, 0), (0, LANES - 2 * nh)))[None],
        "w_out": g_out.reshape(1, -1, d), "pw1": g_pw1, "pw2": g_pw2.reshape(1, -1, d),
        "w1_0": g_w1_0, "w1_1": g_w1_1, "w2_0": g_w2_0.reshape(1, -1, d), "w2_1": g_w2_1.reshape(1, -1, d),
    }
    pieces = _unpack(g_small.reshape(N_CHIPS, -1), [w.shape for w in small_local], lead=(N_CHIPS,))
    full = {n: jnp.moveaxis(p, 0, -2).reshape(p.shape[1:-1] + (N_CHIPS * p.shape[-1],)) for n, p in zip(SMALL_SHARDED, pieces)}
    ws = {
        "e_norm": e_norm, "conv_w": full["e_conv_w"][0], "alog": e_a_log.reshape(nh, 1, 1), "dtb": e_dt_bias.reshape(nh, 1, 1),
        "onorm": e_o_norm, "lng": e_ln_g.reshape(ng, 1, HEAD_DIM), "lnb": e_ln_b.reshape(ng, 1, HEAD_DIM), "w_s": e_w_s[0],
        "bs": e_b_s.reshape(ng, HEAD_DIM, 1), "o_norm": full["o_norm"], "b1a": full["o_pw1_b"][:, :d], "b1b": full["o_pw1_b"][:, d:],
        "dw": full["o_dw"][0], "dw_b": full["o_dw_b"], "ln_g": full["o_ln_g"], "ln_b": full["o_ln_b"], "b2": full["o_pw2_b"],
        "f_norm0": f_norm[0:1], "f_norm1": f_norm[1:2], "final_norm": final_norm.reshape(1, d),
    }

    loss, grad_x, gb, gs = local_step(x[0], loss_target[0], wb, ws)

    gm, gba = gb["w_main"][0], gb["w_ba"][0]
    g_w_in = jnp.concatenate([gm[:, :c_ba], gba[:, :2 * nh], gm[:, c_ba:]], axis=1)
    by_chip = [_chip_major(g_w_in), gb["w_out"].reshape(N_CHIPS, -1, d), gb["pw1"], gb["pw2"].reshape(N_CHIPS, -1, d),
               gb["w1_0"], gb["w1_1"], gb["w2_0"].reshape(N_CHIPS, -1, d), gb["w2_1"].reshape(N_CHIPS, -1, d)]
    from_sibling = sibling_send_other_half(by_chip, name="reduce_sibling")
    to_send, own = [], []
    for i, (gd, sb) in enumerate(zip(by_chip, from_sibling)):
        pb, ow = pre_reduce(gd, sb, who, name=f"reduce_pair_{i}")
        to_send.append(pb)
        own.append(ow)
    received = chip_scatter(to_send, name="reduce_chips")
    halves = [final_reduce(ow, rc, name=f"reduce_sum_{i}") for i, (ow, rc) in enumerate(zip(own, received))]
    r_in, r_out, r_pw1, r_pw2, r_w1_0, r_w1_1, r_w2_0, r_w2_1 = sibling_join_halves(halves, name="reduce_join")
    big_grads = {"e_w_in": [r_in], "e_w_out": [r_out], "o_pw1": [r_pw1], "o_pw2": [r_pw2], "f_w1": [r_w1_0, r_w1_1], "f_w2": [r_w2_0, r_w2_1]}

    small_global = {
        "e_norm": gs["e_norm"], "e_conv_w": gs["conv_w"][None], "e_a_log": gs["alog"].reshape(1, nh), "e_dt_bias": gs["dtb"].reshape(1, nh),
        "e_o_norm": gs["onorm"], "e_ln_g": gs["lng"].reshape(1, n_bw), "e_ln_b": gs["lnb"].reshape(1, n_bw), "e_w_s": gs["w_s"][None],
        "e_b_s": gs["bs"].reshape(1, ng, HEAD_DIM), "o_norm": gs["o_norm"], "o_pw1_b": jnp.concatenate([gs["b1a"], gs["b1b"]], axis=1),
        "o_dw": gs["dw"][None], "o_dw_b": gs["dw_b"], "o_ln_g": gs["ln_g"], "o_ln_b": gs["ln_b"], "o_pw2_b": gs["b2"],
        "f_norm": jnp.concatenate([gs["f_norm0"], gs["f_norm1"]], axis=0), "final_norm": gs["final_norm"].reshape(d),
    }
    packed = _pack([small_global[n] for n in SMALL])
    summed = sum8(allgather8(packed, name="reduce_small"), name="reduce_small_sum")
    small_full = _unpack(summed.reshape(-1), [small_global[n].shape for n in SMALL])
    small_grads = {}
    for n, g in zip(SMALL, small_full):
        if n in SMALL_SHARDED:
            width = a[n].shape[-1]
            g = lax.dynamic_slice_in_dim(g, chip * width, width, axis=g.ndim - 1)
        small_grads[n] = g

    out = {}
    for n in BIG:
        w, m, v = a[n], a["m_" + n], a["v_" + n]
        out[n] = adamw(w, big_grads[n], m, v, name="adamw_" + n)
    sw, sm, sv, sg = (_pack([src[n] for n in SMALL])[None] for src in
                      ({n: a[n] for n in SMALL}, {n: a["m_" + n] for n in SMALL}, {n: a["v_" + n] for n in SMALL}, small_grads))
    res = adamw(sw, [sg[0]], sm, sv, name="adamw_small")
    shapes = [a[n].shape for n in SMALL]
    unpacked = [_unpack(r.reshape(-1), shapes) for r in res]
    for i, n in enumerate(SMALL):
        out[n] = tuple(u[i] for u in unpacked)

    total = lax.psum(loss[0, 0], ("x", "y", "c"))
    result = [total, grad_x[None]]
    for k in range(4):
        result += [out[n][k] for n in WEIGHTS]
    return tuple(result)
```

```python
import functools
import math

import jax
import jax.numpy as jnp
import numpy as np
from jax import lax
from jax.experimental import pallas as pl
from jax.experimental.pallas import tpu as pltpu

F32 = jnp.float32
BF16 = jnp.bfloat16
HI = lax.Precision.HIGHEST

EPS = 1e-6
CHUNK = 64
PAIR = 2 * CHUNK
HEAD_DIM = 128
A_CONV = 4
C_WIDTH = 31
N_CHIPS = 4
ADAM_LR, ADAM_B1, ADAM_B2, ADAM_EPS, ADAM_WD, ADAM_STEP = 0.001, 0.9, 0.999, 1e-08, 0.01, 10

VMEM_LIMIT = 56 * 1024 * 1024


def _cparams(sem=None):
    return pltpu.CompilerParams(dimension_semantics=sem, vmem_limit_bytes=VMEM_LIMIT)


def _pick(n, prefs):
    for p in prefs:
        if n % p == 0:
            return p
    return n


def mm_nn(a, w, *, name, epilogue=None, add=None, out_dtype=F32):
    m, k = a.shape
    s, _, ns = w.shape
    n = s * ns
    tm = _pick(m, (1024, 512, 256, 128))
    tn = _pick(ns, (1024, 512, 256, 128))
    tk = _pick(k, (2048, 1024, 512, 256, 128))
    nk = k // tk
    npb = ns // tn
    assert add is None or epilogue is None

    def body(a_ref, w_ref, *rest):
        if add is not None:
            add_ref, rest = rest[0], rest[1:]
        if epilogue == "relu2":
            o1_ref, o2_ref = rest[0], rest[1]
            acc_ref = rest[2] if nk > 1 else None
        else:
            o1_ref = rest[0]
            acc_ref = rest[1] if nk > 1 else None
        part = jnp.dot(a_ref[...], w_ref[...], preferred_element_type=F32)

        def finish(c):
            if epilogue == "relu2":
                r = jnp.maximum(c, 0.0)
                o1_ref[...] = r.astype(o1_ref.dtype)
                o2_ref[...] = (r * r).astype(o2_ref.dtype)
            elif add is not None:
                o1_ref[...] = (c + add_ref[...].astype(F32)).astype(o1_ref.dtype)
            else:
                o1_ref[...] = c.astype(o1_ref.dtype)

        if nk == 1:
            finish(part)
        else:
            kk = pl.program_id(2)

            @pl.when(kk == 0)
            def _():
                acc_ref[...] = part

            @pl.when(kk > 0)
            def _():
                acc_ref[...] += part

            @pl.when(kk == nk - 1)
            def _():
                finish(acc_ref[...])

    o_spec = pl.BlockSpec((tm, tn), lambda i, j, kk: (i, j))
    if epilogue == "relu2":
        out_shape = (jax.ShapeDtypeStruct((m, n), BF16), jax.ShapeDtypeStruct((m, n), BF16))
        out_specs = (o_spec, o_spec)
    else:
        out_shape = jax.ShapeDtypeStruct((m, n), out_dtype)
        out_specs = o_spec
    return pl.pallas_call(
        body, name=name, out_shape=out_shape,
        grid=(m // tm, n // tn, nk),
        in_specs=[pl.BlockSpec((tm, tk), lambda i, j, kk: (i, kk)),
                  pl.BlockSpec((None, tk, tn), lambda i, j, kk: (j // npb, kk, j % npb))] + ([o_spec] if add is not None else []),
        out_specs=out_specs,
        scratch_shapes=[pltpu.VMEM((tm, tn), F32)] if nk > 1 else [],
        compiler_params=_cparams(("parallel", "parallel", "arbitrary")),
    )(*([a, w] + ([add] if add is not None else [])))


def mm_nt(a, w, *, name, mul=None, add=None, out_dtype=F32):
    assert mul is None or add is None
    if add is not None:
        mul = add
    m, n = a.shape
    s, k, ns = w.shape
    assert n == s * ns
    tm = _pick(m, (1024, 512, 256, 128))
    tko = _pick(k, (1024, 512, 256, 128))
    tn = _pick(ns, (2048, 1024, 512, 256, 128))
    nn = n // tn
    npb = ns // tn

    def body(a_ref, w_ref, *rest):
        if mul is not None:
            m_ref, o_ref = rest[0], rest[1]
            acc_ref = rest[2] if nn > 1 else None
        else:
            m_ref, o_ref = None, rest[0]
            acc_ref = rest[1] if nn > 1 else None
        part = lax.dot_general(a_ref[...], w_ref[...], (((1,), (1,)), ((), ())), preferred_element_type=F32)

        def finish(c):
            if add is not None:
                c = c + m_ref[...].astype(F32)
            elif m_ref is not None:
                c = c * (2.0 * m_ref[...].astype(F32))
            o_ref[...] = c.astype(o_ref.dtype)

        if nn == 1:
            finish(part)
        else:
            kk = pl.program_id(2)

            @pl.when(kk == 0)
            def _():
                acc_ref[...] = part

            @pl.when(kk > 0)
            def _():
                acc_ref[...] += part

            @pl.when(kk == nn - 1)
            def _():
                finish(acc_ref[...])

    in_specs = [pl.BlockSpec((tm, tn), lambda i, j, kk: (i, kk)),
                pl.BlockSpec((None, tko, tn), lambda i, j, kk: (kk // npb, j, kk % npb))]
    args = [a, w]
    if mul is not None:
        in_specs.append(pl.BlockSpec((tm, tko), lambda i, j, kk: (i, j)))
        args.append(mul)
    return pl.pallas_call(
        body, name=name, out_shape=jax.ShapeDtypeStruct((m, k), out_dtype),
        grid=(m // tm, k // tko, nn),
        in_specs=in_specs,
        out_specs=pl.BlockSpec((tm, tko), lambda i, j, kk: (i, j)),
        scratch_shapes=[pltpu.VMEM((tm, tko), F32)] if nn > 1 else [],
        compiler_params=_cparams(("parallel", "parallel", "arbitrary")),
    )(*args)


def mm_tn(a, b, *, groups, name):
    m, k = a.shape
    _, n = b.shape
    ns = n // groups
    tko = _pick(k, (1024, 512, 256, 128))
    tn = _pick(ns, (1024, 512, 256, 128))
    tc = _pick(m, (1024, 512, 256, 128))
    nc = m // tc
    npb = ns // tn

    def body(a_ref, b_ref, o_ref):
        part = lax.dot_general(a_ref[...], b_ref[...], (((0,), (0,)), ((), ())), preferred_element_type=F32)
        kk = pl.program_id(2)

        @pl.when(kk == 0)
        def _():
            o_ref[...] = part

        @pl.when(kk > 0)
        def _():
            o_ref[...] += part

    return pl.pallas_call(
        body, name=name, out_shape=jax.ShapeDtypeStruct((groups, k, ns), F32),
        grid=(k // tko, n // tn, nc),
        in_specs=[pl.BlockSpec((tc, tko), lambda i, j, kk: (kk, i)),
                  pl.BlockSpec((tc, tn), lambda i, j, kk: (kk, j))],
        out_specs=pl.BlockSpec((None, tko, tn), lambda i, j, kk: (j // npb, i, j % npb)),
        compiler_params=_cparams(("parallel", "parallel", "arbitrary")),
    )(a, b)


ROWS = 128


def _full_spec(arr):
    nd = arr.ndim
    return pl.BlockSpec(arr.shape, lambda i, _nd=nd: (0,) * _nd)


def _row_spec(x, rows):
    if isinstance(x, tuple):
        _, w, cb = x
        return pl.BlockSpec((rows, w), lambda i, _cb=cb: (i, _cb))
    return pl.BlockSpec((rows, x.shape[1]), lambda i: (i, 0))


def _arr(x):
    return x[0] if isinstance(x, tuple) else x


def _width(x):
    return x[1] if isinstance(x, tuple) else x.shape[1]


def rows_fwd(fn, params, xs, stores, *, name, rows=ROWS):
    t = _arr(xs[0]).shape[0]
    np_, nx = len(params), len(xs)

    def body(*refs):
        p_refs, x_refs, o_refs = refs[:np_], refs[np_:np_ + nx], refs[np_ + nx:]
        outs = fn(*[r[...].astype(F32) for r in p_refs], *[r[...].astype(F32) for r in x_refs])
        for (idx, dt, _), o_ref in zip(stores, o_refs):
            o_ref[...] = outs[idx].astype(dt)

    res = pl.pallas_call(
        body, name=name,
        out_shape=tuple(jax.ShapeDtypeStruct((t, w), dt) for _, dt, w in stores),
        grid=(t // rows,),
        in_specs=[_full_spec(p) for p in params] + [_row_spec(x, rows) for x in xs],
        out_specs=tuple(pl.BlockSpec((rows, w), lambda i: (i, 0)) for _, _, w in stores),
        compiler_params=_cparams(("parallel",)),
    )(*params, *[_arr(x) for x in xs])
    return res


def rows_bwd(fn, params, xs, cts, dx_stores, *, name, rows=ROWS, concat=False):
    t = _arr(xs[0]).shape[0]
    np_, nx = len(params), len(xs)
    ct_idx = [i for i, c in enumerate(cts) if c is not None]
    ct_arrs = [cts[i] for i in ct_idx]
    nct = len(ct_arrs)
    nds = 1 if concat else len(dx_stores)
    widths = [_width(xs[xi]) for xi, _ in dx_stores]

    def body(*refs):
        p_refs = refs[:np_]
        x_refs = refs[np_:np_ + nx]
        c_refs = refs[np_ + nx:np_ + nx + nct]
        d_refs = refs[np_ + nx + nct:np_ + nx + nct + nds]
        g_refs = refs[np_ + nx + nct + nds:]
        pv = [r[...].astype(F32) for r in p_refs]
        xv = [r[...].astype(F32) for r in x_refs]
        outs, vjp = jax.vjp(lambda *a: tuple(fn(*a)), *pv, *xv)
        ct_full = [jnp.zeros_like(o) for o in outs]
        for i, r in zip(ct_idx, c_refs):
            ct_full[i] = r[...].astype(F32)
        grads = vjp(tuple(ct_full))
        if concat:
            off = 0
            for (xi, dt), wd in zip(dx_stores, widths):
                d_refs[0][:, off:off + wd] = grads[np_ + xi].astype(dt)
                off += wd
        else:
            for (xi, dt), d_ref in zip(dx_stores, d_refs):
                d_ref[...] = grads[np_ + xi].astype(dt)
        step = pl.program_id(0)
        for j, g_ref in enumerate(g_refs):
            @pl.when(step == 0)
            def _(g_ref=g_ref, j=j):
                g_ref[...] = grads[j]

            @pl.when(step > 0)
            def _(g_ref=g_ref, j=j):
                g_ref[...] += grads[j]

    dx_shapes = [(sum(widths), dx_stores[0][1])] if concat else [(wd, dt) for wd, (_, dt) in zip(widths, dx_stores)]
    out_shape = tuple(jax.ShapeDtypeStruct((t, wd), dt) for wd, dt in dx_shapes) + \
        tuple(jax.ShapeDtypeStruct(p.shape, F32) for p in params)
    out_specs = tuple(pl.BlockSpec((rows, wd), lambda i: (i, 0)) for wd, _ in dx_shapes) + \
        tuple(_full_spec(p) for p in params)
    return pl.pallas_call(
        body, name=name, out_shape=out_shape, grid=(t // rows,),
        in_specs=[_full_spec(p) for p in params] + [_row_spec(x, rows) for x in xs] + [_row_spec(c, rows) for c in ct_arrs],
        out_specs=out_specs,
        compiler_params=_cparams(("arbitrary",)),
    )(*params, *[_arr(x) for x in xs], *[_arr(c) for c in ct_arrs])


def _rms(x, g):
    return x * lax.rsqrt(jnp.mean(x * x, axis=-1, keepdims=True) + EPS) * g


def _sigmoid(x):
    return 1.0 / (1.0 + jnp.exp(-x))


def _silu(x):
    return x * _sigmoid(x)


def _gelu(x):
    return 0.5 * x * (1.0 + lax.erf(x * (1.0 / math.sqrt(2.0))))


def stage_norm(g, x):
    return x, _rms(x, g)


def stage_res_norm(g, x, y):
    xn = x + y
    return xn, _rms(xn, g)


def stage_res_bias_norm(b, g, x, y):
    xn = x + y + b
    return xn, _rms(xn, g)


def stage_glu(ba, bb, za, zb):
    return ((za + ba) * _sigmoid(zb + bb),)


def stage_ln_silu(dw_b, ln_g, ln_b, cv):
    z = cv + dw_b
    mu = jnp.mean(z, axis=-1, keepdims=True)
    zc = z - mu
    y = zc * lax.rsqrt(jnp.mean(zc * zc, axis=-1, keepdims=True) + EPS) * ln_g + ln_b
    return (_silu(y),)


CONV_ROWS = 128
CONV_COLS = 256
SUBLANES = 8


def _halo(k):
    return SUBLANES * ((k - 1 + SUBLANES - 1) // SUBLANES)


def _taps_by_roll(k):
    out = {}
    for s in range(k):
        out.setdefault(s % SUBLANES, []).append((s // SUBLANES, s))
    return out


def _shifted_down(win, k, r):
    halo, n = _halo(k), win.shape[0]
    segs = {}
    for b, lst in _taps_by_roll(k).items():
        rolled = win if b == 0 else pltpu.roll(win, b, axis=0)
        for a, s in lst:
            segs[s] = rolled[halo - SUBLANES * a: halo - SUBLANES * a + r]
    return segs


def _shifted_up(win, k, r):
    n = win.shape[0]
    segs = {}
    for b, lst in _taps_by_roll(k).items():
        rolled = win if b == 0 else pltpu.roll(win, n - b, axis=0)
        for a, s in lst:
            segs[s] = rolled[SUBLANES * a: SUBLANES * a + r]
    return segs


def _for_blocks(nblk, fn):
    fn(0, True, nblk == 1)
    if nblk > 2:
        def step(i, c):
            fn(i, False, False)
            return c
        lax.fori_loop(1, nblk - 1, step, 0)
    if nblk > 1:
        fn(nblk - 1, False, True)


def _base(i, r):
    return i * r if isinstance(i, int) else pl.multiple_of(i * r, r)


def _win_top(ref, i, first, r, halo):
    if first:
        return jnp.concatenate([jnp.zeros((halo, ref.shape[1]), F32), ref[pl.ds(0, r), :]], axis=0)
    base = _base(i, r)
    return ref[pl.ds(base - halo, r + halo), :]


def _win_bottom(ref, i, last, r, halo):
    base = _base(i, r)
    if last:
        return jnp.concatenate([ref[pl.ds(base, r), :], jnp.zeros((halo, ref.shape[1]), F32)], axis=0)
    return ref[pl.ds(base, r + halo), :]


def conv_fwd(x, w, *, mode, name, g=None):
    t = x.shape[0]
    k, c = w.shape
    r, cb, halo = min(CONV_ROWS, t), min(CONV_COLS, c), _halo(k)
    nblk = t // r

    def body(*refs):
        if mode == "silu_bwd":
            x_ref, w_ref, g_ref, o_ref = refs
        else:
            x_ref, w_ref, o_ref = refs

        def blk(i, first, last):
            segs = _shifted_down(_win_top(x_ref, i, first, r, halo), k, r)
            acc = None
            for s in range(k):
                term = w_ref[pl.ds(k - 1 - s, 1), :] * segs[s]
                acc = term if acc is None else acc + term
            base = _base(i, r)
            if mode == "silu":
                acc = _silu(acc)
            elif mode == "silu_bwd":
                sg = _sigmoid(acc)
                acc = g_ref[pl.ds(base, r), :] * (sg * (1.0 + acc * (1.0 - sg)))
            o_ref[pl.ds(base, r), :] = acc

        _for_blocks(nblk, blk)

    col = pl.BlockSpec((t, cb), lambda j: (0, j))
    in_specs = [col, pl.BlockSpec((k, cb), lambda j: (0, j))] + ([col] if mode == "silu_bwd" else [])
    args = [x, w] + ([g] if mode == "silu_bwd" else [])
    return pl.pallas_call(
        body, name=name, out_shape=jax.ShapeDtypeStruct((t, c), F32), grid=(c // cb,),
        in_specs=in_specs, out_specs=col, compiler_params=_cparams(("parallel",)),
    )(*args)


def conv_bwd(x, w, dy, *, name, into=None):
    t = x.shape[0]
    k, c = w.shape
    r, cb, halo = min(CONV_ROWS, t), min(CONV_COLS, c), _halo(k)
    nblk = t // r

    def body(x_ref, w_ref, dy_ref, *rest):
        dx_ref, dw_ref = rest[-2], rest[-1]
        dw_ref[...] = jnp.zeros_like(dw_ref)

        def blk(i, first, last):
            base = _base(i, r)
            up = _shifted_up(_win_bottom(dy_ref, i, last, r, halo), k, r)
            down = _shifted_down(_win_top(x_ref, i, first, r, halo), k, r)
            dyb = up[0]
            acc = None
            for s in range(k):
                term = w_ref[pl.ds(k - 1 - s, 1), :] * up[s]
                acc = term if acc is None else acc + term
                dw_ref[pl.ds(k - 1 - s, 1), :] += jnp.sum(down[s] * dyb, axis=0, keepdims=True)
            dx_ref[pl.ds(base, r), :] = acc.astype(dx_ref.dtype)

        _for_blocks(nblk, blk)

    col = pl.BlockSpec((t, cb), lambda j: (0, j))
    wsp = pl.BlockSpec((k, cb), lambda j: (0, j))
    dx_shape = jax.ShapeDtypeStruct((t, c), F32) if into is None else jax.ShapeDtypeStruct(into.shape, into.dtype)
    return pl.pallas_call(
        body, name=name,
        out_shape=(dx_shape, jax.ShapeDtypeStruct((k, c), F32)), grid=(c // cb,),
        in_specs=[col, wsp, col] + ([] if into is None else [pl.BlockSpec(memory_space=pl.ANY)]),
        out_specs=(col, wsp),
        input_output_aliases={} if into is None else {3: 0},
        compiler_params=_cparams(("parallel",)),
    )(*([x, w, dy] + ([] if into is None else [into])))


_DIMS = {"nn": (((1,), (0,)), ((), ())), "nt": (((1,), (1,)), ((), ())), "tn": (((0,), (0,)), ((), ()))}


def _mxu(a, b, mode):
    return lax.dot_general(a, b, _DIMS[mode], preferred_element_type=F32)


def _split(x):
    hi = x.astype(BF16)
    return hi, (x - hi.astype(F32)).astype(BF16)


def _dot_raw(a, b, mode, prec):
    if prec == "bf16":
        return _mxu(a.astype(BF16), b.astype(BF16), mode)
    if prec == "x3":
        ah, al = _split(a)
        bh, bl = _split(b)
        return _mxu(ah, bh, mode) + (_mxu(ah, bl, mode) + _mxu(al, bh, mode))
    if prec == "x3r":
        bh, bm = _split(b)
        bl = (b - bh.astype(F32) - bm.astype(F32)).astype(BF16)
        ah = a.astype(BF16)
        return _mxu(ah, bh, mode) + (_mxu(ah, bm, mode) + _mxu(ah, bl, mode))
    raise ValueError(prec)


@functools.lru_cache(maxsize=None)
def _dot_fn(mode, prec):
    bprec = "x3" if prec == "x3r" else prec

    @jax.custom_vjp
    def f(a, b):
        return _dot_raw(a, b, mode, prec)

    def fwd(a, b):
        return _dot_raw(a, b, mode, prec), (a, b)

    def bwd(res, ct):
        a, b = res
        if mode == "nn":
            return _dot_raw(ct, b, "nt", bprec), _dot_raw(a, ct, "tn", bprec)
        if mode == "nt":
            return _dot_raw(ct, b, "nn", bprec), _dot_raw(ct, a, "tn", bprec)
        return _dot_raw(b, ct, "nt", bprec), _dot_raw(a, ct, "nn", bprec)

    f.defvjp(fwd, bwd)
    return f


def _dot(a, b, mode="nn", prec="bf16"):
    return _dot_fn(mode, prec)(a, b)


def _inv_product(l):
    n = l.shape[0]
    eye = (lax.broadcasted_iota(jnp.int32, (n, n), 0) == lax.broadcasted_iota(jnp.int32, (n, n), 1)).astype(F32)
    p = eye - l
    pw = l
    for _ in range(5):
        pw = _dot_raw(pw, pw, "nn", "x3")
        p = _dot_raw(p, eye + pw, "nn", "x3")
    return p


@jax.custom_vjp
def _inv_unit_lower(l, t_saved):
    return t_saved


def _inv_fwd(l, t_saved):
    return t_saved, t_saved


def _inv_bwd(t, ct):
    tmp = _dot_raw(t, ct, "tn", "x3")
    return -_dot_raw(tmp, t, "nt", "x3"), jnp.zeros_like(t)


_inv_unit_lower.defvjp(_inv_fwd, _inv_bwd)


def _softplus(x):
    pos = x > 0
    return jnp.where(pos, x, 0.0) + jnp.log(1.0 + jnp.exp(jnp.where(pos, -x, x)))


def _l2n(x):
    return x * lax.rsqrt(jnp.sum(x * x, axis=-1, keepdims=True) + EPS)


def delta_pair(s0, qc, kc, vc, z, pba, alog, dtb, onorm, head, nh, t_saved=None):
    n = PAIR
    assert qc.shape == (n, HEAD_DIM) and n == HEAD_DIM
    lane = lax.broadcasted_iota(jnp.int32, (1, pba.shape[1]), 1)
    braw = jnp.sum(pba * (lane == head).astype(F32), axis=1, keepdims=True)
    araw = jnp.sum(pba * (lane == nh + head).astype(F32), axis=1, keepdims=True)
    ri = lax.broadcasted_iota(jnp.int32, (n, n), 0)
    ci = lax.broadcasted_iota(jnp.int32, (n, n), 1)
    same = (ri // CHUNK) == (ci // CHUNK)
    tri = same & (ci <= ri)
    same_f, tri_f = same.astype(F32), tri.astype(F32)
    strict_f = (same & (ci < ri)).astype(F32)
    m0 = (lax.broadcasted_iota(jnp.int32, (n, 1), 0) < CHUNK).astype(F32)
    m1 = 1.0 - m0

    q = _l2n(qc) * (HEAD_DIM ** -0.5)
    k = _l2n(kc)
    beta = _sigmoid(braw)
    g = -jnp.exp(alog) * _softplus(araw + dtb)
    gb = jnp.broadcast_to(g, (n, n))
    gc = _dot(tri_f, gb, "nn", "x3r")
    gtot = _dot(same_f, gb, "nn", "x3r")
    decay = jnp.exp(jnp.where(tri, gc - gc.T, -1e30))
    eg = jnp.exp(gc)
    kb, vb = k * beta, vc * beta
    l = _dot(kb, k, "nt") * decay * strict_f
    if t_saved is None:
        tinv = _inv_product(l)
    else:
        tinv = _inv_unit_lower(l, t_saved)
    u = _dot(tinv, vb, "nn", "x3")
    w = _dot(tinv, kb * eg, "nn", "x3")
    attn = _dot(q, k, "nt") * decay
    q_dec = q * eg
    k_tail = k * jnp.exp(gtot - gc)
    gl0 = jnp.exp(jnp.sum(gb * m0, axis=0, keepdims=True))
    gl1 = jnp.exp(jnp.sum(gb * m1, axis=0, keepdims=True))

    vn0 = m0 * (u - _dot(w, s0))
    s1 = s0 * gl0 + _dot(k_tail, vn0, "tn")
    vn1 = m1 * (u - _dot(w, s1))
    o = m0 * _dot(q_dec, s0) + m1 * _dot(q_dec, s1) + _dot(attn, vn0 + vn1)
    s2 = s1 * gl1 + _dot(k_tail, vn1, "tn")

    on = o * lax.rsqrt(jnp.mean(o * o, axis=-1, keepdims=True) + EPS) * onorm
    return on * _silu(z), s2, tinv


def _hcols(i):
    return slice(i * HEAD_DIM, (i + 1) * HEAD_DIM)


def delta_fwd(qkv, pm, pba, alog, dtb, onorm, cat_width, *, name):
    t = qkv.shape[0]
    h = alog.shape[0]
    hd = h * HEAD_DIM
    npair = t // PAIR
    mat = pl.BlockSpec((h, None, PAIR, HEAD_DIM), lambda p: (0, p, 0, 0))
    par = pl.BlockSpec((h, 1, 1), lambda p: (0, 0, 0))

    def body(qkv_ref, z_ref, pba_ref, al_ref, dt_ref, on_ref, o_ref, st_ref, ti_ref, s_scr):
        @pl.when(pl.program_id(0) == 0)
        def _():
            s_scr[...] = jnp.zeros_like(s_scr)

        pba_v = pba_ref[...]
        for i in range(h):
            s0 = s_scr[i]
            st_ref[i] = s0
            out, s2, tinv = delta_pair(s0, qkv_ref[:, _hcols(i)], qkv_ref[:, _hcols(h + i)], qkv_ref[:, _hcols(2 * h + i)],
                                       z_ref[:, _hcols(i)], pba_v, al_ref[i], dt_ref[i], on_ref[...], i, h)
            o_ref[:, _hcols(i)] = out.astype(o_ref.dtype)
            ti_ref[i] = tinv
            s_scr[i] = s2

    return pl.pallas_call(
        body, name=name, grid=(npair,),
        out_shape=(jax.ShapeDtypeStruct((t, cat_width), BF16),
                   jax.ShapeDtypeStruct((h, npair, PAIR, HEAD_DIM), F32),
                   jax.ShapeDtypeStruct((h, npair, PAIR, PAIR), F32)),
        in_specs=[pl.BlockSpec((PAIR, 3 * hd), lambda p: (p, 0)), pl.BlockSpec((PAIR, hd), lambda p: (p, 3)),
                  pl.BlockSpec((PAIR, pba.shape[1]), lambda p: (p, 0)), par, par,
                  pl.BlockSpec((1, HEAD_DIM), lambda p: (0, 0))],
        out_specs=(pl.BlockSpec((PAIR, hd), lambda p: (p, 0)), mat, mat),
        scratch_shapes=[pltpu.VMEM((h, HEAD_DIM, HEAD_DIM), F32)],
        compiler_params=_cparams(("arbitrary",)),
    )(qkv, pm, pba, alog, dtb, onorm)


def delta_bwd(qkv, pm, pba, alog, dtb, onorm, states, tinvs, dcat, *, name):
    t = qkv.shape[0]
    h = alog.shape[0]
    hd = h * HEAD_DIM
    npair = t // PAIR
    rev = lambda p: npair - 1 - p
    mat = pl.BlockSpec((h, None, PAIR, HEAD_DIM), lambda p: (0, rev(p), 0, 0))
    par = pl.BlockSpec((h, 1, 1), lambda p: (0, 0, 0))
    onsp = pl.BlockSpec((1, HEAD_DIM), lambda p: (0, 0))
    wide = pl.BlockSpec((PAIR, 3 * hd), lambda p: (rev(p), 0))
    zsp = pl.BlockSpec((PAIR, hd), lambda p: (rev(p), 3))
    bsp = pl.BlockSpec((PAIR, pba.shape[1]), lambda p: (rev(p), 0))

    def body(qkv_ref, z_ref, pba_ref, al_ref, dt_ref, on_ref, st_ref, ti_ref, dc_ref,
             dqkv_ref, dz_ref, dpba_ref, dal_ref, ddt_ref, don_ref, ds_scr):
        @pl.when(pl.program_id(0) == 0)
        def _():
            ds_scr[...] = jnp.zeros_like(ds_scr)
            dal_ref[...] = jnp.zeros_like(dal_ref)
            ddt_ref[...] = jnp.zeros_like(ddt_ref)
            don_ref[...] = jnp.zeros_like(don_ref)

        pba_v = pba_ref[...]
        dpba = jnp.zeros_like(pba_v)
        don = jnp.zeros_like(on_ref[...])
        for i in range(h):
            tsv = ti_ref[i]
            fn = lambda s0, qc, kc, vc, z, pb, al, dt, on, _i=i, _t=tsv: delta_pair(s0, qc, kc, vc, z, pb, al, dt, on, _i, h, _t)[:2]
            _, vjp = jax.vjp(fn, st_ref[i], qkv_ref[:, _hcols(i)], qkv_ref[:, _hcols(h + i)], qkv_ref[:, _hcols(2 * h + i)],
                             z_ref[:, _hcols(i)], pba_v, al_ref[i], dt_ref[i], on_ref[...])
            ds0, dq, dk, dv, dz, dpb, dal, ddt, don_i = vjp((dc_ref[:, _hcols(i)], ds_scr[i]))
            ds_scr[i] = ds0
            dqkv_ref[:, _hcols(i)] = dq
            dqkv_ref[:, _hcols(h + i)] = dk
            dqkv_ref[:, _hcols(2 * h + i)] = dv
            dz_ref[:, _hcols(i)] = dz.astype(dz_ref.dtype)
            dal_ref[i] += dal
            ddt_ref[i] += ddt
            dpba = dpba + dpb
            don = don + don_i
        dpba_ref[...] = dpba.astype(dpba_ref.dtype)
        don_ref[...] += don

    return pl.pallas_call(
        body, name=name, grid=(npair,),
        out_shape=(jax.ShapeDtypeStruct((t, 3 * hd), F32), jax.ShapeDtypeStruct(pm.shape, BF16),
                   jax.ShapeDtypeStruct(pba.shape, BF16),
                   jax.ShapeDtypeStruct((h, 1, 1), F32), jax.ShapeDtypeStruct((h, 1, 1), F32),
                   jax.ShapeDtypeStruct((1, HEAD_DIM), F32)),
        in_specs=[wide, zsp, bsp, par, par, onsp, mat, mat, pl.BlockSpec((PAIR, hd), lambda p: (rev(p), 0))],
        out_specs=(wide, zsp, bsp, par, par, onsp),
        scratch_shapes=[pltpu.VMEM((h, HEAD_DIM, HEAD_DIM), F32)],
        compiler_params=_cparams(("arbitrary",)),
    )(qkv, pm, pba, alog, dtb, onorm, states, tinvs, dcat)


def gmlp_block(ln_g, ln_b, w, bcol, u_raw, v_raw):
    n = w.shape[0]
    ri = lax.broadcasted_iota(jnp.int32, (n, n), 0)
    ci = lax.broadcasted_iota(jnp.int32, (n, n), 1)
    mask = ((ci // CHUNK) <= (ri // CHUNK)).astype(F32)
    vg = _gelu(v_raw)
    vc = vg - jnp.mean(vg, axis=-1, keepdims=True)
    vgn = vc * lax.rsqrt(jnp.mean(vc * vc, axis=-1, keepdims=True) + EPS) * ln_g + ln_b
    return _gelu(u_raw) * (_dot(w * mask, vgn) + bcol)


def gmlp_fwd(pm, ln_g, ln_b, w_s, bcol, cat, *, name):
    t = pm.shape[0]
    g = w_s.shape[0]
    gw = g * HEAD_DIM
    assert pm.shape[1] == 6 * gw and cat.shape[1] == 2 * gw

    def body(u_ref, v_ref, lg_ref, lb_ref, w_ref, b_ref, cat_in, o_ref):
        del cat_in
        for i in range(g):
            o_ref[:, _hcols(i)] = gmlp_block(lg_ref[i], lb_ref[i], w_ref[i], b_ref[i], u_ref[:, _hcols(i)],
                                             v_ref[:, _hcols(i)]).astype(o_ref.dtype)

    full = lambda a: pl.BlockSpec(a.shape, lambda m: (0, 0, 0))
    return pl.pallas_call(
        body, name=name, grid=(t // HEAD_DIM,),
        out_shape=jax.ShapeDtypeStruct(cat.shape, cat.dtype),
        in_specs=[pl.BlockSpec((HEAD_DIM, gw), lambda m: (m, 4)), pl.BlockSpec((HEAD_DIM, gw), lambda m: (m, 5)),
                  full(ln_g), full(ln_b), full(w_s), full(bcol), pl.BlockSpec(memory_space=pl.ANY)],
        out_specs=pl.BlockSpec((HEAD_DIM, gw), lambda m: (m, 1)),
        input_output_aliases={6: 0},
        compiler_params=_cparams(("arbitrary",)),
    )(pm, pm, ln_g, ln_b, w_s, bcol, cat)


def gmlp_bwd(pm, ln_g, ln_b, w_s, bcol, dcat, dpm, *, name):
    t = pm.shape[0]
    g = w_s.shape[0]
    gw = g * HEAD_DIM
    assert pm.shape[1] == 6 * gw and dpm.shape == pm.shape

    def body(u_ref, v_ref, lg_ref, lb_ref, w_ref, b_ref, dc_ref, dpm_in, duv_ref, dlg_ref, dlb_ref, dw_ref, db_ref):
        del dpm_in
        first = pl.program_id(0) == 0
        for i in range(g):
            _, vjp = jax.vjp(gmlp_block, lg_ref[i], lb_ref[i], w_ref[i], b_ref[i], u_ref[:, _hcols(i)], v_ref[:, _hcols(i)])
            dlg, dlb, dw, db, du, dv = vjp(dc_ref[:, _hcols(i)])
            duv_ref[:, _hcols(i)] = du.astype(duv_ref.dtype)
            duv_ref[:, _hcols(g + i)] = dv.astype(duv_ref.dtype)
            for ref, val in ((dlg_ref, dlg), (dlb_ref, dlb), (dw_ref, dw), (db_ref, db)):
                @pl.when(first)
                def _(ref=ref, val=val, i=i):
                    ref[i] = val

                @pl.when(jnp.logical_not(first))
                def _(ref=ref, val=val, i=i):
                    ref[i] += val

    full = lambda a: pl.BlockSpec(a.shape, lambda m: (0, 0, 0))
    return pl.pallas_call(
        body, name=name, grid=(t // HEAD_DIM,),
        out_shape=(jax.ShapeDtypeStruct(dpm.shape, dpm.dtype),
                   jax.ShapeDtypeStruct(ln_g.shape, F32), jax.ShapeDtypeStruct(ln_b.shape, F32),
                   jax.ShapeDtypeStruct(w_s.shape, F32), jax.ShapeDtypeStruct(bcol.shape, F32)),
        in_specs=[pl.BlockSpec((HEAD_DIM, gw), lambda m: (m, 4)), pl.BlockSpec((HEAD_DIM, gw), lambda m: (m, 5)),
                  full(ln_g), full(ln_b), full(w_s), full(bcol), pl.BlockSpec((HEAD_DIM, gw), lambda m: (m, 1)),
                  pl.BlockSpec(memory_space=pl.ANY)],
        out_specs=(pl.BlockSpec((HEAD_DIM, 2 * gw), lambda m: (m, 2)), full(ln_g), full(ln_b), full(w_s), full(bcol)),
        input_output_aliases={7: 0},
        compiler_params=_cparams(("arbitrary",)),
    )(pm, pm, ln_g, ln_b, w_s, bcol, dcat, dpm)


def loss_head(g, x, r, tgt, *, name, rows=ROWS):
    t, d = x.shape

    def body(g_ref, x_ref, r_ref, t_ref, l_ref, dx_ref, dxb_ref, dg_ref):
        y, vjp = jax.vjp(lambda gg, xx: _rms(xx, gg), g_ref[...], x_ref[...] + r_ref[...])
        e = y - t_ref[...]
        part = (0.5 / d) * jnp.sum(jnp.sum(e * e, axis=1, keepdims=True), axis=0, keepdims=True)
        dg, dx = vjp(e * (1.0 / d))
        dx_ref[...] = dx
        dxb_ref[...] = dx.astype(BF16)
        first = pl.program_id(0) == 0

        @pl.when(first)
        def _():
            l_ref[...] = part
            dg_ref[...] = dg

        @pl.when(jnp.logical_not(first))
        def _():
            l_ref[...] += part
            dg_ref[...] += dg

    rs = pl.BlockSpec((rows, d), lambda i: (i, 0))
    gs = pl.BlockSpec((1, d), lambda i: (0, 0))
    return pl.pallas_call(
        body, name=name, grid=(t // rows,),
        out_shape=(jax.ShapeDtypeStruct((1, 1), F32), jax.ShapeDtypeStruct((t, d), F32),
                   jax.ShapeDtypeStruct((t, d), BF16), jax.ShapeDtypeStruct((1, d), F32)),
        in_specs=[gs, rs, rs, rs],
        out_specs=(pl.BlockSpec((1, 1), lambda i: (0, 0)), rs, rs, gs),
        compiler_params=_cparams(("arbitrary",)),
    )(g, x, r, tgt)


def adamw(w, gs, m, v, *, name):
    nl, r, c = w.shape
    assert len(gs) == nl
    if r % SUBLANES == 0:
        tr, tc = _pick(r, (256, 128, 64, 32, 16, 8)), c
    else:
        tr, tc = r, _pick(c, (256, 128))
    k1 = 1.0 - ADAM_B1 ** ADAM_STEP
    k2 = 1.0 - ADAM_B2 ** ADAM_STEP

    def body(*refs):
        w_ref, m_ref, v_ref = refs[0], refs[1], refs[2]
        g_refs = refs[3:3 + nl]
        go_ref, d_ref, mo_ref, vo_ref = refs[3 + nl:]
        gg = g_refs[0][...]
        for li in range(1, nl):
            gg = jnp.where(pl.program_id(0) == li, g_refs[li][...], gg)
        mn = ADAM_B1 * m_ref[...] + (1.0 - ADAM_B1) * gg
        vn = ADAM_B2 * v_ref[...] + (1.0 - ADAM_B2) * (gg * gg)
        go_ref[...] = gg
        d_ref[...] = -ADAM_LR * ((mn / k1) / (jnp.sqrt(vn / k2) + ADAM_EPS) + ADAM_WD * w_ref[...])
        mo_ref[...] = mn
        vo_ref[...] = vn

    if tc == c:
        sp = pl.BlockSpec((None, tr, c), lambda l, i: (l, i, 0))
        gsp = pl.BlockSpec((tr, c), lambda l, i: (i, 0))
    else:
        sp = pl.BlockSpec((None, r, tc), lambda l, i: (l, 0, i))
        gsp = pl.BlockSpec((r, tc), lambda l, i: (0, i))
    sds = jax.ShapeDtypeStruct((nl, r, c), F32)
    return pl.pallas_call(
        body, name=name, grid=(nl, (r // tr) * (c // tc)), out_shape=(sds, sds, sds, sds),
        in_specs=[sp, sp, sp] + [gsp] * nl, out_specs=(sp, sp, sp, sp),
        compiler_params=_cparams(("parallel", "parallel")),
    )(w, m, v, *gs)


MESH = pl.DeviceIdType.MESH
ANY = pl.BlockSpec(memory_space=pl.ANY)


def _me():
    return lax.axis_index("x"), lax.axis_index("y"), lax.axis_index("c")


def _other_chips(x, y):
    return [(1 - x, y), (x, 1 - y), (1 - x, 1 - y)]


def gather_shards(arrs, *, name):
    n = len(arrs)
    per = 7

    def body(*refs):
        ins, outs = refs[:n], refs[n:2 * n]
        send_sems, recv_sems = refs[2 * n:]
        x, y, c = _me()
        chip = 2 * x + y
        sib = (x, y, 1 - c)
        chips = _other_chips(x, y)

        def rc(i, k, src, dst, to):
            return pltpu.make_async_remote_copy(src_ref=src, dst_ref=dst, send_sem=send_sems.at[per * i + k],
                                                recv_sem=recv_sems.at[per * i + k], device_id=to, device_id_type=MESH)

        def half(i, which):
            hrows = arrs[i].shape[0] // 2
            return pl.ds(which * hrows, hrows)

        own = [rc(i, 6, ins[i], outs[i].at[chip], sib) for i in range(n)]
        for cp in own:
            cp.start()
        started = []
        for i in range(n):
            for j, ch in enumerate(chips):
                cp = rc(i, j, ins[i].at[half(i, c)], outs[i].at[chip, half(i, c)], (ch[0], ch[1], c))
                cp.start()
                started.append(cp)
        for i in range(n):
            for j, ch in enumerate(chips):
                sc = 2 * ch[0] + ch[1]
                landed = outs[i].at[sc, half(i, c)]
                rc(i, j, ins[i].at[half(i, c)], landed, (ch[0], ch[1], c)).wait_recv()
                fw = rc(i, 3 + j, landed, landed, sib)
                fw.start()
                started.append(fw)
        for i in range(n):
            for j, ch in enumerate(chips):
                sc = 2 * ch[0] + ch[1]
                theirs = outs[i].at[sc, half(i, 1 - c)]
                rc(i, 3 + j, theirs, theirs, sib).wait_recv()
        for cp in own:
            cp.wait_recv()
        for cp in started + own:
            cp.wait_send()

    return pl.pallas_call(
        body, name=name,
        out_shape=tuple(jax.ShapeDtypeStruct((N_CHIPS,) + a.shape, a.dtype) for a in arrs),
        in_specs=[ANY] * n, out_specs=tuple([ANY] * n),
        scratch_shapes=[pltpu.SemaphoreType.DMA((per * n,)), pltpu.SemaphoreType.DMA((per * n,))],
        compiler_params=pltpu.CompilerParams(has_side_effects=True),
    )(*arrs)


def sibling_send_other_half(arrs, *, name):
    n = len(arrs)

    def body(*refs):
        ins, outs = refs[:n], refs[n:2 * n]
        send_sems, recv_sems = refs[2 * n:]
        x, y, c = _me()
        cps = []
        for i in range(n):
            hrows = arrs[i].shape[1] // 2
            cps.append(pltpu.make_async_remote_copy(
                src_ref=ins[i].at[:, pl.ds((1 - c) * hrows, hrows), :], dst_ref=outs[i], send_sem=send_sems.at[i],
                recv_sem=recv_sems.at[i], device_id=(x, y, 1 - c), device_id_type=MESH))
        for cp in cps:
            cp.start()
        for cp in cps:
            cp.wait_recv()
        for cp in cps:
            cp.wait_send()

    return pl.pallas_call(
        body, name=name,
        out_shape=tuple(jax.ShapeDtypeStruct((a.shape[0], a.shape[1] // 2, a.shape[2]), a.dtype) for a in arrs),
        in_specs=[ANY] * n, out_specs=tuple([ANY] * n),
        scratch_shapes=[pltpu.SemaphoreType.DMA((n,)), pltpu.SemaphoreType.DMA((n,))],
        compiler_params=pltpu.CompilerParams(has_side_effects=True),
    )(*arrs)


def chip_scatter(arrs, *, name):
    n = len(arrs)

    def body(*refs):
        ins, outs = refs[:n], refs[n:2 * n]
        send_sems, recv_sems = refs[2 * n:]
        x, y, c = _me()
        cps = []
        for i in range(n):
            for j, ch in enumerate(_other_chips(x, y)):
                cps.append(pltpu.make_async_remote_copy(
                    src_ref=ins[i].at[2 * ch[0] + ch[1]], dst_ref=outs[i].at[j], send_sem=send_sems.at[3 * i + j],
                    recv_sem=recv_sems.at[3 * i + j], device_id=(ch[0], ch[1], c), device_id_type=MESH))
        for cp in cps:
            cp.start()
        for cp in cps:
            cp.wait_recv()
        for cp in cps:
            cp.wait_send()

    return pl.pallas_call(
        body, name=name,
        out_shape=tuple(jax.ShapeDtypeStruct((3,) + a.shape[1:], a.dtype) for a in arrs),
        in_specs=[ANY] * n, out_specs=tuple([ANY] * n),
        scratch_shapes=[pltpu.SemaphoreType.DMA((3 * n,)), pltpu.SemaphoreType.DMA((3 * n,))],
        compiler_params=pltpu.CompilerParams(has_side_effects=True),
    )(*arrs)


def sibling_join_halves(arrs, *, name):
    n = len(arrs)

    def body(*refs):
        outs = refs[n:2 * n]
        send_sems, recv_sems = refs[2 * n:]
        x, y, c = _me()
        cps = []
        for i in range(n):
            hrows = arrs[i].shape[0] // 2
            mine = outs[i].at[pl.ds(c * hrows, hrows)]
            cps.append(pltpu.make_async_remote_copy(src_ref=mine, dst_ref=mine, send_sem=send_sems.at[i],
                                                    recv_sem=recv_sems.at[i], device_id=(x, y, 1 - c), device_id_type=MESH))
        for cp in cps:
            cp.start()
        for i in range(n):
            hrows = arrs[i].shape[0] // 2
            theirs = outs[i].at[pl.ds((1 - c) * hrows, hrows)]
            pltpu.make_async_remote_copy(src_ref=theirs, dst_ref=theirs, send_sem=send_sems.at[i], recv_sem=recv_sems.at[i],
                                         device_id=(x, y, 1 - c), device_id_type=MESH).wait_recv()
        for cp in cps:
            cp.wait_send()

    return pl.pallas_call(
        body, name=name,
        out_shape=tuple(jax.ShapeDtypeStruct(a.shape, a.dtype) for a in arrs),
        in_specs=[ANY] * n, out_specs=tuple([ANY] * n),
        input_output_aliases={i: i for i in range(n)},
        scratch_shapes=[pltpu.SemaphoreType.DMA((n,)), pltpu.SemaphoreType.DMA((n,))],
        compiler_params=pltpu.CompilerParams(has_side_effects=True),
    )(*arrs)


def pre_reduce(gd, sib, who, *, name):
    _, r, c = gd.shape
    h = r // 2
    tr = _pick(h, (256, 128, 64, 32, 16, 8))
    nrb = h // tr

    def body(who_ref, gd_ref, sib_ref, pb_ref, own_ref):
        p = gd_ref[...] + sib_ref[...]
        pb_ref[...] = p.astype(BF16)

        @pl.when(pl.program_id(1) == who_ref[1])
        def _():
            own_ref[...] = p

    return pl.pallas_call(
        body, name=name,
        out_shape=(jax.ShapeDtypeStruct((N_CHIPS, h, c), BF16), jax.ShapeDtypeStruct((h, c), F32)),
        grid_spec=pltpu.PrefetchScalarGridSpec(
            num_scalar_prefetch=1, grid=(nrb, N_CHIPS),
            in_specs=[pl.BlockSpec((None, tr, c), lambda i, s, w: (s, w[0] * nrb + i, 0)),
                      pl.BlockSpec((None, tr, c), lambda i, s, w: (s, i, 0))],
            out_specs=(pl.BlockSpec((None, tr, c), lambda i, s, w: (s, i, 0)),
                       pl.BlockSpec((tr, c), lambda i, s, w: (i, 0)))),
        compiler_params=_cparams(("parallel", "arbitrary")),
    )(who, gd, sib)


def final_reduce(own, rcv, who, *, name):
    h, c = own.shape
    tr = _pick(h, (256, 128, 64, 32, 16, 8))
    nrb = h // tr

    def body(who_ref, own_ref, rcv_ref, o_ref):
        del who_ref
        acc = own_ref[...]
        for j in range(3):
            acc = acc + rcv_ref[j].astype(F32)
        o_ref[...] = acc

    return pl.pallas_call(
        body, name=name, out_shape=jax.ShapeDtypeStruct((2 * h, c), F32),
        grid_spec=pltpu.PrefetchScalarGridSpec(
            num_scalar_prefetch=1, grid=(nrb,),
            in_specs=[pl.BlockSpec((tr, c), lambda i, w: (i, 0)), pl.BlockSpec((3, tr, c), lambda i, w: (0, i, 0))],
            out_specs=pl.BlockSpec((tr, c), lambda i, w: (w[0] * nrb + i, 0))),
        compiler_params=_cparams(("parallel",)),
    )(who, own, rcv)


def sum8(a, *, name):
    _, r, c = a.shape
    tr = _pick(r, (256, 128, 64, 32, 16, 8))

    def body(a_ref, o_ref):
        acc = a_ref[0]
        for j in range(1, 8):
            acc = acc + a_ref[j]
        o_ref[...] = acc

    return pl.pallas_call(
        body, name=name, out_shape=jax.ShapeDtypeStruct((r, c), F32), grid=(r // tr,),
        in_specs=[pl.BlockSpec((8, tr, c), lambda i: (0, i, 0))],
        out_specs=pl.BlockSpec((tr, c), lambda i: (i, 0)),
        compiler_params=_cparams(("parallel",)),
    )(a)


def allgather8(a, *, name):
    def body(in_ref, out_ref, send_sems, recv_sems, local_sem):
        x, y, c = _me()
        me = 4 * x + 2 * y + c
        loc = pltpu.make_async_copy(in_ref, out_ref.at[me], local_sem)
        loc.start()
        cps = []
        for k in range(1, 8):
            fx, fy, fc = (k >> 2) & 1, (k >> 1) & 1, k & 1
            to = (x + fx - 2 * x * fx, y + fy - 2 * y * fy, c + fc - 2 * c * fc)
            cps.append(pltpu.make_async_remote_copy(src_ref=in_ref, dst_ref=out_ref.at[me], send_sem=send_sems.at[k - 1],
                                                    recv_sem=recv_sems.at[k - 1], device_id=to, device_id_type=MESH))
        for cp in cps:
            cp.start()
        for k in range(1, 8):
            fx, fy, fc = (k >> 2) & 1, (k >> 1) & 1, k & 1
            src = 4 * (x + fx - 2 * x * fx) + 2 * (y + fy - 2 * y * fy) + (c + fc - 2 * c * fc)
            pltpu.make_async_remote_copy(src_ref=in_ref, dst_ref=out_ref.at[src], send_sem=send_sems.at[k - 1],
                                         recv_sem=recv_sems.at[k - 1], device_id=(x, y, c), device_id_type=MESH).wait_recv()
        for cp in cps:
            cp.wait_send()
        loc.wait()

    return pl.pallas_call(
        body, name=name, out_shape=jax.ShapeDtypeStruct((8,) + a.shape, a.dtype),
        in_specs=[ANY], out_specs=ANY,
        scratch_shapes=[pltpu.SemaphoreType.DMA((7,)), pltpu.SemaphoreType.DMA((7,)), pltpu.SemaphoreType.DMA],
        compiler_params=pltpu.CompilerParams(has_side_effects=True),
    )(a)


def local_step(x, tgt, wb, ws):
    t, d = x.shape
    nh = ws["alog"].shape[0]
    ng = ws["w_s"].shape[0]
    assert nh == ng
    mix_w = (nh + ng) * HEAD_DIM

    (h0,) = rows_fwd(stage_norm, [ws["e_norm"]], [x], [(1, BF16, d)], name="f_norm_e")
    pm = mm_nt(h0, wb["w_main_t"], name="f_proj_main")
    pba = mm_nt(h0, wb["w_ba_t"], name="f_proj_ba")
    qkv = conv_fwd(pm, ws["conv_w"], mode="silu", name="f_conv4")
    cat, states, tinvs = delta_fwd(qkv, pm, pba, ws["alog"], ws["dtb"], ws["onorm"], mix_w, name="f_delta")
    cat = gmlp_fwd(pm, ws["lng"], ws["lnb"], ws["w_s"], ws["bs"], cat, name="f_gmlp")
    y0 = mm_nn(cat, wb["w_out"], name="f_out")
    x1, h1 = rows_fwd(stage_res_norm, [ws["f_norm0"]], [x, y0], [(0, F32, d), (1, BF16, d)], name="f_norm_f0")
    a1, s1 = mm_nn(h1, wb["w1_0"], name="f_mlp0_up", epilogue="relu2")
    y1 = mm_nn(s1, wb["w2_0"], name="f_mlp0_down")
    x2, h2 = rows_fwd(stage_res_norm, [ws["o_norm"]], [x1, y1], [(0, F32, d), (1, BF16, d)], name="f_norm_o")
    zz = mm_nn(h2, wb["pw1"], name="f_pw1")
    zparts = [(zz, d, 0), (zz, d, 1)]
    (gl,) = rows_fwd(stage_glu, [ws["b1a"], ws["b1b"]], zparts, [(0, F32, d)], name="f_glu")
    cv = conv_fwd(gl, ws["dw"], mode="plain", name="f_conv31")
    ln_params = [ws["dw_b"], ws["ln_g"], ws["ln_b"]]
    (sl,) = rows_fwd(stage_ln_silu, ln_params, [cv], [(0, BF16, d)], name="f_ln_silu")
    y2 = mm_nn(sl, wb["pw2"], name="f_pw2")
    x3, h3 = rows_fwd(stage_res_bias_norm, [ws["b2"], ws["f_norm1"]], [x2, y2], [(0, F32, d), (1, BF16, d)], name="f_norm_f1")
    a3, s3 = mm_nn(h3, wb["w1_1"], name="f_mlp1_up", epilogue="relu2")
    y3 = mm_nn(s3, wb["w2_1"], name="f_mlp1_down")
    loss, d4, d4b, g_final = loss_head(ws["final_norm"], x3, y3, tgt, name="loss_head")

    gb, gs = {}, {"final_norm": g_final}
    s_up = wb["w1_0"].shape[0]

    gb["w2_1"] = mm_tn(s3, d4b, groups=1, name="b_mlp1_down_w")
    dpre3 = mm_nt(d4b, wb["w2_1"], name="b_mlp1_down_x", mul=a3, out_dtype=BF16)
    gb["w1_1"] = mm_tn(h3, dpre3, groups=s_up, name="b_mlp1_up_w")
    dh3 = mm_nt(dpre3, wb["w1_1"], name="b_mlp1_up_x")
    d3, d3b, gs["b2"], gs["f_norm1"] = rows_bwd(stage_res_bias_norm, [ws["b2"], ws["f_norm1"]], [x2, y2], [d4, dh3],
                                                [(0, F32), (1, BF16)], name="b_norm_f1")
    gb["pw2"] = mm_tn(sl, d3b, groups=1, name="b_pw2_w")
    dsl = mm_nt(d3b, wb["pw2"], name="b_pw2_x")
    dcv, gs["dw_b"], gs["ln_g"], gs["ln_b"] = rows_bwd(stage_ln_silu, ln_params, [cv], [dsl], [(0, F32)], name="b_ln_silu")
    dgl, gs["dw"] = conv_bwd(gl, ws["dw"], dcv, name="b_conv31")
    dzz, gs["b1a"], gs["b1b"] = rows_bwd(stage_glu, [ws["b1a"], ws["b1b"]], zparts, [dgl], [(0, BF16), (1, BF16)],
                                         name="b_glu", concat=True)
    gb["pw1"] = mm_tn(h2, dzz, groups=wb["pw1"].shape[0], name="b_pw1_w")
    dh2 = mm_nt(dzz, wb["pw1"], name="b_pw1_x")
    d2, d2b, gs["o_norm"] = rows_bwd(stage_res_norm, [ws["o_norm"]], [x1, y1], [d3, dh2], [(0, F32), (1, BF16)], name="b_norm_o")
    gb["w2_0"] = mm_tn(s1, d2b, groups=1, name="b_mlp0_down_w")
    dpre1 = mm_nt(d2b, wb["w2_0"], name="b_mlp0_down_x", mul=a1, out_dtype=BF16)
    gb["w1_0"] = mm_tn(h1, dpre1, groups=s_up, name="b_mlp0_up_w")
    dh1 = mm_nt(dpre1, wb["w1_0"], name="b_mlp0_up_x")
    d1, d1b, gs["f_norm0"] = rows_bwd(stage_res_norm, [ws["f_norm0"]], [x, y0], [d2, dh1], [(0, F32), (1, BF16)], name="b_norm_f0")
    gb["w_out"] = mm_tn(cat, d1b, groups=1, name="b_out_w")
    dcat = mm_nt(d1b, wb["w_out"], name="b_out_x")
    dqkv_c, dpm, dpba, gs["alog"], gs["dtb"], gs["onorm"] = delta_bwd(
        qkv, pm, pba, ws["alog"], ws["dtb"], ws["onorm"], states, tinvs, dcat, name="b_delta")
    dpm, gs["lng"], gs["lnb"], gs["w_s"], gs["bs"] = gmlp_bwd(
        pm, ws["lng"], ws["lnb"], ws["w_s"], ws["bs"], dcat, dpm, name="b_gmlp")
    dconv = conv_fwd(pm, ws["conv_w"], mode="silu_bwd", g=dqkv_c, name="b_conv4_silu")
    dpm, gs["conv_w"] = conv_bwd(pm, ws["conv_w"], dconv, name="b_conv4", into=dpm)
    gb["w_main_t"] = mm_tn(dpm, h0, groups=1, name="b_proj_main_w")
    gb["w_ba_t"] = mm_tn(dpba, h0, groups=1, name="b_proj_ba_w")
    dh0 = mm_nn(dpm, wb["w_main_t"], name="b_proj_main_x")
    dh0 = mm_nn(dpba, wb["w_ba_t"], name="b_proj_ba_x", add=dh0)
    grad_x, gs["e_norm"] = rows_bwd(stage_norm, [ws["e_norm"]], [x], [d1, dh0], [(0, F32)], name="b_norm_e")
    return loss, grad_x, gb, gs


WEIGHTS = ["e_norm", "e_w_in", "e_conv_w", "e_a_log", "e_dt_bias", "e_o_norm", "e_ln_g", "e_ln_b", "e_w_s", "e_b_s", "e_w_out",
           "o_norm", "o_pw1", "o_pw1_b", "o_dw", "o_dw_b", "o_ln_g", "o_ln_b", "o_pw2", "o_pw2_b", "f_norm", "f_w1", "f_w2",
           "final_norm"]
BIG = ["e_w_in", "e_w_out", "o_pw1", "o_pw2", "f_w1", "f_w2"]
SMALL_SHARDED = ["e_conv_w", "o_norm", "o_pw1_b", "o_dw", "o_dw_b", "o_ln_g", "o_ln_b", "o_pw2_b"]
SMALL = [n for n in WEIGHTS if n not in BIG]
LANES = 128
PACK_ROWS = 16
IN_ROW_MULT = 256


def _pack(arrs):
    flat = jnp.concatenate([a.reshape(-1).astype(F32) for a in arrs])
    n = flat.shape[0]
    rows = -(-n // (LANES * PACK_ROWS)) * PACK_ROWS
    return jnp.pad(flat, (0, rows * LANES - n)).reshape(rows, LANES)


def _unpack(flat, shapes, lead=()):
    outs, off = [], 0
    for s in shapes:
        n = int(np.prod(s))
        outs.append(flat[..., off:off + n].reshape(lead + tuple(s)))
        off += n
    return outs


def _chip_major(g, s=N_CHIPS):
    k, n = g.shape
    return g.reshape(k, s, n // s).transpose(1, 0, 2)


def kernel(x, e_norm, e_w_in, e_conv_w, e_a_log, e_dt_bias, e_o_norm, e_ln_g, e_ln_b, e_w_s, e_b_s, e_w_out, o_norm, o_pw1, o_pw1_b, o_dw, o_dw_b, o_ln_g, o_ln_b, o_pw2, o_pw2_b, f_norm, f_w1, f_w2, final_norm, loss_target, m_e_norm, m_e_w_in, m_e_conv_w, m_e_a_log, m_e_dt_bias, m_e_o_norm, m_e_ln_g, m_e_ln_b, m_e_w_s, m_e_b_s, m_e_w_out, m_o_norm, m_o_pw1, m_o_pw1_b, m_o_dw, m_o_dw_b, m_o_ln_g, m_o_ln_b, m_o_pw2, m_o_pw2_b, m_f_norm, m_f_w1, m_f_w2, m_final_norm, v_e_norm, v_e_w_in, v_e_conv_w, v_e_a_log, v_e_dt_bias, v_e_o_norm, v_e_ln_g, v_e_ln_b, v_e_w_s, v_e_b_s, v_e_w_out, v_o_norm, v_o_pw1, v_o_pw1_b, v_o_dw, v_o_dw_b, v_o_ln_g, v_o_ln_b, v_o_pw2, v_o_pw2_b, v_f_norm, v_f_w1, v_f_w2, v_final_norm):
    a = dict(locals())
    xi, yi, ci = _me()
    chip = 2 * xi + yi
    who = jnp.stack([ci, chip]).astype(jnp.int32)
    t, d = x.shape[1], x.shape[2]
    nh, ng = e_a_log.shape[1], e_w_s.shape[1]
    n_qkv, n_av, n_bw = 3 * nh * HEAD_DIM, nh * HEAD_DIM, ng * HEAD_DIM
    in_cols = n_qkv + n_av + 2 * nh + 2 * n_bw
    c_ba = n_qkv + n_av

    sh_in = in_cols // N_CHIPS
    pad_in = -(-sh_in // IN_ROW_MULT) * IN_ROW_MULT - sh_in
    w_in_t_local = jnp.pad(e_w_in[0].T.astype(BF16), ((0, pad_in), (0, 0)))
    big_local = [e_w_out[0], o_pw1[0], o_pw2[0], f_w1[0], f_w1[1], f_w2[0], f_w2[1]]
    small_local = [a[n] for n in SMALL_SHARDED]
    gathered = gather_shards([w_in_t_local] + [w.astype(BF16) for w in big_local] + [_pack(small_local)], name="gather_weights")
    g_in, g_out, g_pw1, g_pw2, g_w1_0, g_w1_1, g_w2_0, g_w2_1, g_small = gathered

    def in_rows(lo, hi):
        parts = []
        for s in range(N_CHIPS):
            a0, a1 = max(lo, s * sh_in), min(hi, (s + 1) * sh_in)
            if a0 < a1:
                parts.append(g_in[s, a0 - s * sh_in:a1 - s * sh_in])
        return parts

    wb = {
        "w_main_t": jnp.concatenate(in_rows(0, c_ba) + in_rows(c_ba + 2 * nh, in_cols), axis=0)[None],
        "w_ba_t": jnp.pad(jnp.concatenate(in_rows(c_ba, c_ba + 2 * nh), axis=0), ((0, LANES - 2 * nh), (0, 0)))[None],
        "w_out": g_out.reshape(1, -1, d), "pw1": g_pw1, "pw2": g_pw2.reshape(1, -1, d),
        "w1_0": g_w1_0, "w1_1": g_w1_1, "w2_0": g_w2_0.reshape(1, -1, d), "w2_1": g_w2_1.reshape(1, -1, d),
    }
    pieces = _unpack(g_small.reshape(N_CHIPS, -1), [w.shape for w in small_local], lead=(N_CHIPS,))
    full = {n: jnp.moveaxis(p, 0, -2).reshape(p.shape[1:-1] + (N_CHIPS * p.shape[-1],)) for n, p in zip(SMALL_SHARDED, pieces)}
    ws = {
        "e_norm": e_norm, "conv_w": full["e_conv_w"][0], "alog": e_a_log.reshape(nh, 1, 1), "dtb": e_dt_bias.reshape(nh, 1, 1),
        "onorm": e_o_norm, "lng": e_ln_g.reshape(ng, 1, HEAD_DIM), "lnb": e_ln_b.reshape(ng, 1, HEAD_DIM), "w_s": e_w_s[0],
        "bs": e_b_s.reshape(ng, HEAD_DIM, 1), "o_norm": full["o_norm"], "b1a": full["o_pw1_b"][:, :d], "b1b": full["o_pw1_b"][:, d:],
        "dw": full["o_dw"][0], "dw_b": full["o_dw_b"], "ln_g": full["o_ln_g"], "ln_b": full["o_ln_b"], "b2": full["o_pw2_b"],
        "f_norm0": f_norm[0:1], "f_norm1": f_norm[1:2], "final_norm": final_norm.reshape(1, d),
    }

    loss, grad_x, gb, gs = local_step(x[0], loss_target[0], wb, ws)

    gm, gba = gb["w_main_t"][0], gb["w_ba_t"][0]
    g_in_t = jnp.concatenate([gm[:c_ba], gba[:2 * nh], gm[c_ba:]], axis=0).reshape(N_CHIPS, sh_in, d)
    by_chip = [jnp.pad(g_in_t, ((0, 0), (0, pad_in), (0, 0))), gb["w_out"].reshape(N_CHIPS, -1, d), gb["pw1"],
               gb["pw2"].reshape(N_CHIPS, -1, d),
               gb["w1_0"], gb["w1_1"], gb["w2_0"].reshape(N_CHIPS, -1, d), gb["w2_1"].reshape(N_CHIPS, -1, d)]
    from_sibling = sibling_send_other_half(by_chip, name="reduce_sibling")
    to_send, own = [], []
    for i, (gd, sb) in enumerate(zip(by_chip, from_sibling)):
        pb, ow = pre_reduce(gd, sb, who, name=f"reduce_pair_{i}")
        to_send.append(pb)
        own.append(ow)
    received = chip_scatter(to_send, name="reduce_chips")
    halves = [final_reduce(ow, rc, who, name=f"reduce_sum_{i}") for i, (ow, rc) in enumerate(zip(own, received))]
    r_in, r_out, r_pw1, r_pw2, r_w1_0, r_w1_1, r_w2_0, r_w2_1 = sibling_join_halves(halves, name="reduce_join")
    big_grads = {"e_w_in": [r_in[:sh_in]], "e_w_out": [r_out], "o_pw1": [r_pw1], "o_pw2": [r_pw2], "f_w1": [r_w1_0, r_w1_1],
                 "f_w2": [r_w2_0, r_w2_1]}

    small_global = {
        "e_norm": gs["e_norm"], "e_conv_w": gs["conv_w"][None], "e_a_log": gs["alog"].reshape(1, nh), "e_dt_bias": gs["dtb"].reshape(1, nh),
        "e_o_norm": gs["onorm"], "e_ln_g": gs["lng"].reshape(1, n_bw), "e_ln_b": gs["lnb"].reshape(1, n_bw), "e_w_s": gs["w_s"][None],
        "e_b_s": gs["bs"].reshape(1, ng, HEAD_DIM), "o_norm": gs["o_norm"], "o_pw1_b": jnp.concatenate([gs["b1a"], gs["b1b"]], axis=1),
        "o_dw": gs["dw"][None], "o_dw_b": gs["dw_b"], "o_ln_g": gs["ln_g"], "o_ln_b": gs["ln_b"], "o_pw2_b": gs["b2"],
        "f_norm": jnp.concatenate([gs["f_norm0"], gs["f_norm1"]], axis=0), "final_norm": gs["final_norm"].reshape(d),
    }
    packed = _pack([small_global[n] for n in SMALL])
    summed = sum8(allgather8(packed, name="reduce_small"), name="reduce_small_sum")
    small_full = _unpack(summed.reshape(-1), [small_global[n].shape for n in SMALL])
    small_grads = {}
    for n, g in zip(SMALL, small_full):
        if n in SMALL_SHARDED:
            width = a[n].shape[-1]
            g = lax.dynamic_slice_in_dim(g, chip * width, width, axis=g.ndim - 1)
        small_grads[n] = g

    out = {}
    for n in BIG:
        w, m, v = a[n], a["m_" + n], a["v_" + n]
        if n == "e_w_in":
            res = adamw(w.transpose(0, 2, 1), big_grads[n], m.transpose(0, 2, 1), v.transpose(0, 2, 1), name="adamw_" + n)
            out[n] = tuple(r.transpose(0, 2, 1) for r in res)
        else:
            out[n] = adamw(w, big_grads[n], m, v, name="adamw_" + n)
    sw, sm, sv, sg = (_pack([src[n] for n in SMALL])[None] for src in
                      ({n: a[n] for n in SMALL}, {n: a["m_" + n] for n in SMALL}, {n: a["v_" + n] for n in SMALL}, small_grads))
    res = adamw(sw, [sg[0]], sm, sv, name="adamw_small")
    shapes = [a[n].shape for n in SMALL]
    unpacked = [_unpack(r.reshape(-1), shapes) for r in res]
    for i, n in enumerate(SMALL):
        out[n] = tuple(u[i] for u in unpacked)

    total = lax.psum(loss[0, 0], ("x", "y", "c"))
    result = [total, grad_x[None]]
    for k in range(4):
        result += [out[n][k] for n in WEIGHTS]
    return tuple(result)
```

```python
import functools
import math

import jax
import jax.numpy as jnp
import numpy as np
from jax import lax
from jax.experimental import pallas as pl
from jax.experimental.pallas import tpu as pltpu

F32 = jnp.float32
BF16 = jnp.bfloat16
HI = lax.Precision.HIGHEST

EPS = 1e-6
CHUNK = 64
PAIR = 2 * CHUNK
HEAD_DIM = 128
A_CONV = 4
C_WIDTH = 31
N_CHIPS = 4
ADAM_LR, ADAM_B1, ADAM_B2, ADAM_EPS, ADAM_WD, ADAM_STEP = 0.001, 0.9, 0.999, 1e-08, 0.01, 10

VMEM_LIMIT = 56 * 1024 * 1024


def _cparams(sem=None):
    return pltpu.CompilerParams(dimension_semantics=sem, vmem_limit_bytes=VMEM_LIMIT)


def _pick(n, prefs):
    for p in prefs:
        if n % p == 0:
            return p
    return n


def mm_nn(a, w, *, name, epilogue=None, add=None, out_dtype=F32):
    m, k = a.shape
    s, _, ns = w.shape
    n = s * ns
    tm = _pick(m, (1024, 512, 256, 128))
    tn = _pick(ns, (1024, 512, 256, 128))
    tk = _pick(k, (2048, 1024, 512, 256, 128))
    nk = k // tk
    npb = ns // tn
    assert add is None or epilogue is None

    def body(a_ref, w_ref, *rest):
        if add is not None:
            add_ref, rest = rest[0], rest[1:]
        if epilogue == "relu2":
            o1_ref, o2_ref = rest[0], rest[1]
            acc_ref = rest[2] if nk > 1 else None
        else:
            o1_ref = rest[0]
            acc_ref = rest[1] if nk > 1 else None
        part = jnp.dot(a_ref[...], w_ref[...], preferred_element_type=F32)

        def finish(c):
            if epilogue == "relu2":
                r = jnp.maximum(c, 0.0)
                o1_ref[...] = r.astype(o1_ref.dtype)
                o2_ref[...] = (r * r).astype(o2_ref.dtype)
            elif add is not None:
                o1_ref[...] = (c + add_ref[...].astype(F32)).astype(o1_ref.dtype)
            else:
                o1_ref[...] = c.astype(o1_ref.dtype)

        if nk == 1:
            finish(part)
        else:
            kk = pl.program_id(2)

            @pl.when(kk == 0)
            def _():
                acc_ref[...] = part

            @pl.when(kk > 0)
            def _():
                acc_ref[...] += part

            @pl.when(kk == nk - 1)
            def _():
                finish(acc_ref[...])

    o_spec = pl.BlockSpec((tm, tn), lambda i, j, kk: (i, j))
    if epilogue == "relu2":
        out_shape = (jax.ShapeDtypeStruct((m, n), BF16), jax.ShapeDtypeStruct((m, n), BF16))
        out_specs = (o_spec, o_spec)
    else:
        out_shape = jax.ShapeDtypeStruct((m, n), out_dtype)
        out_specs = o_spec
    return pl.pallas_call(
        body, name=name, out_shape=out_shape,
        grid=(m // tm, n // tn, nk),
        in_specs=[pl.BlockSpec((tm, tk), lambda i, j, kk: (i, kk)),
                  pl.BlockSpec((None, tk, tn), lambda i, j, kk: (j // npb, kk, j % npb))] + ([o_spec] if add is not None else []),
        out_specs=out_specs,
        scratch_shapes=[pltpu.VMEM((tm, tn), F32)] if nk > 1 else [],
        compiler_params=_cparams(("parallel", "parallel", "arbitrary")),
    )(*([a, w] + ([add] if add is not None else [])))


def mm_nt(a, w, *, name, mul=None, add=None, out_dtype=F32):
    assert mul is None or add is None
    if add is not None:
        mul = add
    m, n = a.shape
    s, k, ns = w.shape
    assert n == s * ns
    tm = _pick(m, (1024, 512, 256, 128))
    tko = _pick(k, (1024, 512, 256, 128))
    tn = _pick(ns, (2048, 1024, 512, 256, 128))
    nn = n // tn
    npb = ns // tn

    def body(a_ref, w_ref, *rest):
        if mul is not None:
            m_ref, o_ref = rest[0], rest[1]
            acc_ref = rest[2] if nn > 1 else None
        else:
            m_ref, o_ref = None, rest[0]
            acc_ref = rest[1] if nn > 1 else None
        part = lax.dot_general(a_ref[...], w_ref[...], (((1,), (1,)), ((), ())), preferred_element_type=F32)

        def finish(c):
            if add is not None:
                c = c + m_ref[...].astype(F32)
            elif m_ref is not None:
                c = c * (2.0 * m_ref[...].astype(F32))
            o_ref[...] = c.astype(o_ref.dtype)

        if nn == 1:
            finish(part)
        else:
            kk = pl.program_id(2)

            @pl.when(kk == 0)
            def _():
                acc_ref[...] = part

            @pl.when(kk > 0)
            def _():
                acc_ref[...] += part

            @pl.when(kk == nn - 1)
            def _():
                finish(acc_ref[...])

    in_specs = [pl.BlockSpec((tm, tn), lambda i, j, kk: (i, kk)),
                pl.BlockSpec((None, tko, tn), lambda i, j, kk: (kk // npb, j, kk % npb))]
    args = [a, w]
    if mul is not None:
        in_specs.append(pl.BlockSpec((tm, tko), lambda i, j, kk: (i, j)))
        args.append(mul)
    return pl.pallas_call(
        body, name=name, out_shape=jax.ShapeDtypeStruct((m, k), out_dtype),
        grid=(m // tm, k // tko, nn),
        in_specs=in_specs,
        out_specs=pl.BlockSpec((tm, tko), lambda i, j, kk: (i, j)),
        scratch_shapes=[pltpu.VMEM((tm, tko), F32)] if nn > 1 else [],
        compiler_params=_cparams(("parallel", "parallel", "arbitrary")),
    )(*args)


def mm_tn(a, b, *, groups, name):
    m, k = a.shape
    _, n = b.shape
    ns = n // groups
    tko = _pick(k, (1024, 512, 256, 128))
    tn = _pick(ns, (1024, 512, 256, 128))
    tc = _pick(m, (2048, 1024, 512, 256, 128))
    nc = m // tc
    npb = ns // tn

    def body(a_ref, b_ref, o_ref):
        part = lax.dot_general(a_ref[...], b_ref[...], (((0,), (0,)), ((), ())), preferred_element_type=F32)
        kk = pl.program_id(2)

        @pl.when(kk == 0)
        def _():
            o_ref[...] = part

        @pl.when(kk > 0)
        def _():
            o_ref[...] += part

    return pl.pallas_call(
        body, name=name, out_shape=jax.ShapeDtypeStruct((groups, k, ns), F32),
        grid=(k // tko, n // tn, nc),
        in_specs=[pl.BlockSpec((tc, tko), lambda i, j, kk: (kk, i)),
                  pl.BlockSpec((tc, tn), lambda i, j, kk: (kk, j))],
        out_specs=pl.BlockSpec((None, tko, tn), lambda i, j, kk: (j // npb, i, j % npb)),
        compiler_params=_cparams(("parallel", "parallel", "arbitrary")),
    )(a, b)


ROWS = 128


def _full_spec(arr):
    nd = arr.ndim
    return pl.BlockSpec(arr.shape, lambda i, _nd=nd: (0,) * _nd)


def _row_spec(x, rows):
    if isinstance(x, tuple):
        _, w, cb = x
        return pl.BlockSpec((rows, w), lambda i, _cb=cb: (i, _cb))
    return pl.BlockSpec((rows, x.shape[1]), lambda i: (i, 0))


def _arr(x):
    return x[0] if isinstance(x, tuple) else x


def _width(x):
    return x[1] if isinstance(x, tuple) else x.shape[1]


def rows_fwd(fn, params, xs, stores, *, name, rows=ROWS):
    t = _arr(xs[0]).shape[0]
    np_, nx = len(params), len(xs)

    def body(*refs):
        p_refs, x_refs, o_refs = refs[:np_], refs[np_:np_ + nx], refs[np_ + nx:]
        outs = fn(*[r[...].astype(F32) for r in p_refs], *[r[...].astype(F32) for r in x_refs])
        for (idx, dt, _), o_ref in zip(stores, o_refs):
            o_ref[...] = outs[idx].astype(dt)

    res = pl.pallas_call(
        body, name=name,
        out_shape=tuple(jax.ShapeDtypeStruct((t, w), dt) for _, dt, w in stores),
        grid=(t // rows,),
        in_specs=[_full_spec(p) for p in params] + [_row_spec(x, rows) for x in xs],
        out_specs=tuple(pl.BlockSpec((rows, w), lambda i: (i, 0)) for _, _, w in stores),
        compiler_params=_cparams(("parallel",)),
    )(*params, *[_arr(x) for x in xs])
    return res


def rows_bwd(fn, params, xs, cts, dx_stores, *, name, rows=ROWS, concat=False):
    t = _arr(xs[0]).shape[0]
    np_, nx = len(params), len(xs)
    ct_idx = [i for i, c in enumerate(cts) if c is not None]
    ct_arrs = [cts[i] for i in ct_idx]
    nct = len(ct_arrs)
    nds = 1 if concat else len(dx_stores)
    widths = [_width(xs[xi]) for xi, _ in dx_stores]

    def body(*refs):
        p_refs = refs[:np_]
        x_refs = refs[np_:np_ + nx]
        c_refs = refs[np_ + nx:np_ + nx + nct]
        d_refs = refs[np_ + nx + nct:np_ + nx + nct + nds]
        g_refs = refs[np_ + nx + nct + nds:]
        pv = [r[...].astype(F32) for r in p_refs]
        xv = [r[...].astype(F32) for r in x_refs]
        outs, vjp = jax.vjp(lambda *a: tuple(fn(*a)), *pv, *xv)
        ct_full = [jnp.zeros_like(o) for o in outs]
        for i, r in zip(ct_idx, c_refs):
            ct_full[i] = r[...].astype(F32)
        grads = vjp(tuple(ct_full))
        if concat:
            off = 0
            for (xi, dt), wd in zip(dx_stores, widths):
                d_refs[0][:, off:off + wd] = grads[np_ + xi].astype(dt)
                off += wd
        else:
            for (xi, dt), d_ref in zip(dx_stores, d_refs):
                d_ref[...] = grads[np_ + xi].astype(dt)
        step = pl.program_id(0)
        for j, g_ref in enumerate(g_refs):
            @pl.when(step == 0)
            def _(g_ref=g_ref, j=j):
                g_ref[...] = grads[j]

            @pl.when(step > 0)
            def _(g_ref=g_ref, j=j):
                g_ref[...] += grads[j]

    dx_shapes = [(sum(widths), dx_stores[0][1])] if concat else [(wd, dt) for wd, (_, dt) in zip(widths, dx_stores)]
    out_shape = tuple(jax.ShapeDtypeStruct((t, wd), dt) for wd, dt in dx_shapes) + \
        tuple(jax.ShapeDtypeStruct(p.shape, F32) for p in params)
    out_specs = tuple(pl.BlockSpec((rows, wd), lambda i: (i, 0)) for wd, _ in dx_shapes) + \
        tuple(_full_spec(p) for p in params)
    return pl.pallas_call(
        body, name=name, out_shape=out_shape, grid=(t // rows,),
        in_specs=[_full_spec(p) for p in params] + [_row_spec(x, rows) for x in xs] + [_row_spec(c, rows) for c in ct_arrs],
        out_specs=out_specs,
        compiler_params=_cparams(("arbitrary",)),
    )(*params, *[_arr(x) for x in xs], *[_arr(c) for c in ct_arrs])


def _rms(x, g):
    return x * lax.rsqrt(jnp.mean(x * x, axis=-1, keepdims=True) + EPS) * g


def _sigmoid(x):
    return 1.0 / (1.0 + jnp.exp(-x))


def _silu(x):
    return x * _sigmoid(x)


def _gelu(x):
    return 0.5 * x * (1.0 + lax.erf(x * (1.0 / math.sqrt(2.0))))


def stage_norm(g, x):
    return x, _rms(x, g)


def stage_res_norm(g, x, y):
    xn = x + y
    return xn, _rms(xn, g)


def stage_res_bias_norm(b, g, x, y):
    xn = x + y + b
    return xn, _rms(xn, g)


def stage_glu(ba, bb, za, zb):
    return ((za + ba) * _sigmoid(zb + bb),)


def stage_ln_silu(dw_b, ln_g, ln_b, cv):
    z = cv + dw_b
    mu = jnp.mean(z, axis=-1, keepdims=True)
    zc = z - mu
    y = zc * lax.rsqrt(jnp.mean(zc * zc, axis=-1, keepdims=True) + EPS) * ln_g + ln_b
    return (_silu(y),)


CONV_ROWS = 128
CONV_COLS = 256
SUBLANES = 8


def _halo(k):
    return SUBLANES * ((k - 1 + SUBLANES - 1) // SUBLANES)


def _taps_by_roll(k):
    out = {}
    for s in range(k):
        out.setdefault(s % SUBLANES, []).append((s // SUBLANES, s))
    return out


def _shifted_down(win, k, r):
    halo, n = _halo(k), win.shape[0]
    segs = {}
    for b, lst in _taps_by_roll(k).items():
        rolled = win if b == 0 else pltpu.roll(win, b, axis=0)
        for a, s in lst:
            segs[s] = rolled[halo - SUBLANES * a: halo - SUBLANES * a + r]
    return segs


def _shifted_up(win, k, r):
    n = win.shape[0]
    segs = {}
    for b, lst in _taps_by_roll(k).items():
        rolled = win if b == 0 else pltpu.roll(win, n - b, axis=0)
        for a, s in lst:
            segs[s] = rolled[SUBLANES * a: SUBLANES * a + r]
    return segs


def _for_blocks(nblk, fn):
    fn(0, True, nblk == 1)
    if nblk > 2:
        def step(i, c):
            fn(i, False, False)
            return c
        lax.fori_loop(1, nblk - 1, step, 0)
    if nblk > 1:
        fn(nblk - 1, False, True)


def _base(i, r):
    return i * r if isinstance(i, int) else pl.multiple_of(i * r, r)


def _win_top(ref, i, first, r, halo):
    if first:
        return jnp.concatenate([jnp.zeros((halo, ref.shape[1]), F32), ref[pl.ds(0, r), :]], axis=0)
    base = _base(i, r)
    return ref[pl.ds(base - halo, r + halo), :]


def _win_bottom(ref, i, last, r, halo):
    base = _base(i, r)
    if last:
        return jnp.concatenate([ref[pl.ds(base, r), :], jnp.zeros((halo, ref.shape[1]), F32)], axis=0)
    return ref[pl.ds(base, r + halo), :]


def conv_fwd(x, w, *, mode, name, g=None):
    t = x.shape[0]
    k, c = w.shape
    r, cb, halo = min(CONV_ROWS, t), min(CONV_COLS, c), _halo(k)
    nblk = t // r

    def body(*refs):
        if mode == "silu_bwd":
            x_ref, w_ref, g_ref, o_ref = refs
        else:
            x_ref, w_ref, o_ref = refs

        def blk(i, first, last):
            segs = _shifted_down(_win_top(x_ref, i, first, r, halo), k, r)
            acc = None
            for s in range(k):
                term = w_ref[pl.ds(k - 1 - s, 1), :] * segs[s]
                acc = term if acc is None else acc + term
            base = _base(i, r)
            if mode == "silu":
                acc = _silu(acc)
            elif mode == "silu_bwd":
                sg = _sigmoid(acc)
                acc = g_ref[pl.ds(base, r), :] * (sg * (1.0 + acc * (1.0 - sg)))
            o_ref[pl.ds(base, r), :] = acc

        _for_blocks(nblk, blk)

    col = pl.BlockSpec((t, cb), lambda j: (0, j))
    in_specs = [col, pl.BlockSpec((k, cb), lambda j: (0, j))] + ([col] if mode == "silu_bwd" else [])
    args = [x, w] + ([g] if mode == "silu_bwd" else [])
    return pl.pallas_call(
        body, name=name, out_shape=jax.ShapeDtypeStruct((t, c), F32), grid=(c // cb,),
        in_specs=in_specs, out_specs=col, compiler_params=_cparams(("parallel",)),
    )(*args)


def conv_bwd(x, w, dy, *, name, into=None):
    t = x.shape[0]
    k, c = w.shape
    r, cb, halo = min(CONV_ROWS, t), min(CONV_COLS, c), _halo(k)
    nblk = t // r

    def body(x_ref, w_ref, dy_ref, *rest):
        dx_ref, dw_ref = rest[-2], rest[-1]
        dw_ref[...] = jnp.zeros_like(dw_ref)

        def blk(i, first, last):
            base = _base(i, r)
            up = _shifted_up(_win_bottom(dy_ref, i, last, r, halo), k, r)
            down = _shifted_down(_win_top(x_ref, i, first, r, halo), k, r)
            dyb = up[0]
            acc = None
            for s in range(k):
                term = w_ref[pl.ds(k - 1 - s, 1), :] * up[s]
                acc = term if acc is None else acc + term
                dw_ref[pl.ds(k - 1 - s, 1), :] += jnp.sum(down[s] * dyb, axis=0, keepdims=True)
            dx_ref[pl.ds(base, r), :] = acc.astype(dx_ref.dtype)

        _for_blocks(nblk, blk)

    col = pl.BlockSpec((t, cb), lambda j: (0, j))
    wsp = pl.BlockSpec((k, cb), lambda j: (0, j))
    dx_shape = jax.ShapeDtypeStruct((t, c), F32) if into is None else jax.ShapeDtypeStruct(into.shape, into.dtype)
    return pl.pallas_call(
        body, name=name,
        out_shape=(dx_shape, jax.ShapeDtypeStruct((k, c), F32)), grid=(c // cb,),
        in_specs=[col, wsp, col] + ([] if into is None else [pl.BlockSpec(memory_space=pl.ANY)]),
        out_specs=(col, wsp),
        input_output_aliases={} if into is None else {3: 0},
        compiler_params=_cparams(("parallel",)),
    )(*([x, w, dy] + ([] if into is None else [into])))


_DIMS = {"nn": (((1,), (0,)), ((), ())), "nt": (((1,), (1,)), ((), ())), "tn": (((0,), (0,)), ((), ()))}
_DIMS_BATCHED = {"nn": (((2,), (1,)), ((0,), (0,))), "nt": (((2,), (2,)), ((0,), (0,))), "tn": (((1,), (1,)), ((0,), (0,)))}


def _mxu(a, b, mode):
    dims = _DIMS_BATCHED if a.ndim == 3 else _DIMS
    return lax.dot_general(a, b, dims[mode], preferred_element_type=F32)


def _split(x):
    hi = x.astype(BF16)
    return hi, (x - hi.astype(F32)).astype(BF16)


def _dot_raw(a, b, mode, prec):
    if prec == "bf16":
        return _mxu(a.astype(BF16), b.astype(BF16), mode)
    if prec == "x3":
        ah, al = _split(a)
        bh, bl = _split(b)
        return _mxu(ah, bh, mode) + (_mxu(ah, bl, mode) + _mxu(al, bh, mode))
    if prec == "x3r":
        bh, bm = _split(b)
        bl = (b - bh.astype(F32) - bm.astype(F32)).astype(BF16)
        ah = a.astype(BF16)
        return _mxu(ah, bh, mode) + (_mxu(ah, bm, mode) + _mxu(ah, bl, mode))
    raise ValueError(prec)


@functools.lru_cache(maxsize=None)
def _dot_fn(mode, prec):
    bprec = "x3" if prec == "x3r" else prec

    @jax.custom_vjp
    def f(a, b):
        return _dot_raw(a, b, mode, prec)

    def fwd(a, b):
        return _dot_raw(a, b, mode, prec), (a, b)

    def bwd(res, ct):
        a, b = res
        if mode == "nn":
            return _dot_raw(ct, b, "nt", bprec), _dot_raw(a, ct, "tn", bprec)
        if mode == "nt":
            return _dot_raw(ct, b, "nn", bprec), _dot_raw(ct, a, "tn", bprec)
        return _dot_raw(b, ct, "nt", bprec), _dot_raw(a, ct, "nn", bprec)

    f.defvjp(fwd, bwd)
    return f


def _dot(a, b, mode="nn", prec="bf16"):
    return _dot_fn(mode, prec)(a, b)


def _inv_product(l):
    n = l.shape[-1]
    eye = (lax.broadcasted_iota(jnp.int32, (n, n), 0) == lax.broadcasted_iota(jnp.int32, (n, n), 1)).astype(F32)
    p = eye - l
    pw = l
    for _ in range(5):
        pw = _dot_raw(pw, pw, "nn", "x3")
        p = _dot_raw(p, eye + pw, "nn", "x3")
    return p


@jax.custom_vjp
def _inv_unit_lower(l, t_saved):
    return t_saved


def _inv_fwd(l, t_saved):
    return t_saved, t_saved


def _inv_bwd(t, ct):
    tmp = _dot_raw(t, ct, "tn", "x3")
    return -_dot_raw(tmp, t, "nt", "x3"), jnp.zeros_like(t)


_inv_unit_lower.defvjp(_inv_fwd, _inv_bwd)


def _softplus(x):
    pos = x > 0
    return jnp.where(pos, x, 0.0) + jnp.log(1.0 + jnp.exp(jnp.where(pos, -x, x)))


def _l2n(x):
    return x * lax.rsqrt(jnp.sum(x * x, axis=-1, keepdims=True) + EPS)


def delta_pair(s0, qc, kc, vc, z, pba, alog, dtb, onorm, t_saved=None):
    n = PAIR
    nh = qc.shape[0]
    assert qc.shape == (nh, n, HEAD_DIM) and n == HEAD_DIM
    hi = lax.broadcasted_iota(jnp.int32, (nh, 1, pba.shape[1]), 0)
    li = lax.broadcasted_iota(jnp.int32, (nh, 1, pba.shape[1]), 2)
    braw = jnp.sum(pba[None] * (li == hi).astype(F32), axis=2, keepdims=True)
    araw = jnp.sum(pba[None] * (li == hi + nh).astype(F32), axis=2, keepdims=True)
    ri = lax.broadcasted_iota(jnp.int32, (n, n), 0)
    ci = lax.broadcasted_iota(jnp.int32, (n, n), 1)
    same = (ri // CHUNK) == (ci // CHUNK)
    tri = same & (ci <= ri)
    same_f = jnp.broadcast_to(same.astype(F32), (nh, n, n))
    tri_f = jnp.broadcast_to(tri.astype(F32), (nh, n, n))
    strict_f = (same & (ci < ri)).astype(F32)
    m0 = (lax.broadcasted_iota(jnp.int32, (n, 1), 0) < CHUNK).astype(F32)
    m1 = 1.0 - m0

    q = _l2n(qc) * (HEAD_DIM ** -0.5)
    k = _l2n(kc)
    beta = _sigmoid(braw)
    g = -jnp.exp(alog) * _softplus(araw + dtb)
    gb = jnp.broadcast_to(g, (nh, n, n))
    gc = _dot(tri_f, gb, "nn", "x3r")
    gtot = _dot(same_f, gb, "nn", "x3r")
    decay = jnp.exp(jnp.where(tri, gc - jnp.swapaxes(gc, 1, 2), -1e30))
    eg = jnp.exp(gc)
    kb, vb = k * beta, vc * beta
    l = _dot(kb, k, "nt") * decay * strict_f
    if t_saved is None:
        tinv = _inv_product(l)
    else:
        tinv = _inv_unit_lower(l, t_saved)
    u = _dot(tinv, vb, "nn", "x3")
    w = _dot(tinv, kb * eg, "nn", "x3")
    attn = _dot(q, k, "nt") * decay
    q_dec = q * eg
    k_tail = k * jnp.exp(gtot - gc)
    gl0 = jnp.exp(jnp.sum(gb * m0, axis=1, keepdims=True))
    gl1 = jnp.exp(jnp.sum(gb * m1, axis=1, keepdims=True))

    vn0 = m0 * (u - _dot(w, s0))
    s1 = s0 * gl0 + _dot(k_tail, vn0, "tn")
    vn1 = m1 * (u - _dot(w, s1))
    o = m0 * _dot(q_dec, s0) + m1 * _dot(q_dec, s1) + _dot(attn, vn0 + vn1)
    s2 = s1 * gl1 + _dot(k_tail, vn1, "tn")

    on = o * lax.rsqrt(jnp.mean(o * o, axis=-1, keepdims=True) + EPS) * onorm
    return on * _silu(z), s2, tinv


def _hcols(i):
    return slice(i * HEAD_DIM, (i + 1) * HEAD_DIM)


def _heads(ref, first, count):
    return jnp.stack([ref[:, _hcols(first + i)] for i in range(count)])


def delta_fwd(qkv, pm, pba, alog, dtb, onorm, cat_width, *, name):
    t = qkv.shape[0]
    h = alog.shape[0]
    hd = h * HEAD_DIM
    npair = t // PAIR
    mat = pl.BlockSpec((h, None, PAIR, HEAD_DIM), lambda p: (0, p, 0, 0))
    par = pl.BlockSpec((h, 1, 1), lambda p: (0, 0, 0))

    def body(qkv_ref, z_ref, pba_ref, al_ref, dt_ref, on_ref, o_ref, st_ref, ti_ref, s_scr):
        @pl.when(pl.program_id(0) == 0)
        def _():
            s_scr[...] = jnp.zeros_like(s_scr)

        s0 = s_scr[...]
        st_ref[...] = s0
        out, s2, tinv = delta_pair(s0, _heads(qkv_ref, 0, h), _heads(qkv_ref, h, h), _heads(qkv_ref, 2 * h, h), _heads(z_ref, 0, h),
                                   pba_ref[...], al_ref[...], dt_ref[...], on_ref[...])
        for i in range(h):
            o_ref[:, _hcols(i)] = out[i].astype(o_ref.dtype)
        ti_ref[...] = tinv
        s_scr[...] = s2

    return pl.pallas_call(
        body, name=name, grid=(npair,),
        out_shape=(jax.ShapeDtypeStruct((t, cat_width), BF16),
                   jax.ShapeDtypeStruct((h, npair, PAIR, HEAD_DIM), F32),
                   jax.ShapeDtypeStruct((h, npair, PAIR, PAIR), F32)),
        in_specs=[pl.BlockSpec((PAIR, 3 * hd), lambda p: (p, 0)), pl.BlockSpec((PAIR, hd), lambda p: (p, 3)),
                  pl.BlockSpec((PAIR, pba.shape[1]), lambda p: (p, 0)), par, par,
                  pl.BlockSpec((1, HEAD_DIM), lambda p: (0, 0))],
        out_specs=(pl.BlockSpec((PAIR, hd), lambda p: (p, 0)), mat, mat),
        scratch_shapes=[pltpu.VMEM((h, HEAD_DIM, HEAD_DIM), F32)],
        compiler_params=_cparams(("arbitrary",)),
    )(qkv, pm, pba, alog, dtb, onorm)


def delta_bwd(qkv, pm, pba, alog, dtb, onorm, states, tinvs, dcat, *, name):
    t = qkv.shape[0]
    h = alog.shape[0]
    hd = h * HEAD_DIM
    npair = t // PAIR
    rev = lambda p: npair - 1 - p
    mat = pl.BlockSpec((h, None, PAIR, HEAD_DIM), lambda p: (0, rev(p), 0, 0))
    par = pl.BlockSpec((h, 1, 1), lambda p: (0, 0, 0))
    onsp = pl.BlockSpec((1, HEAD_DIM), lambda p: (0, 0))
    wide = pl.BlockSpec((PAIR, 3 * hd), lambda p: (rev(p), 0))
    zsp = pl.BlockSpec((PAIR, hd), lambda p: (rev(p), 3))
    bsp = pl.BlockSpec((PAIR, pba.shape[1]), lambda p: (rev(p), 0))

    def body(qkv_ref, z_ref, pba_ref, al_ref, dt_ref, on_ref, st_ref, ti_ref, dc_ref,
             dqkv_ref, dz_ref, dpba_ref, dal_ref, ddt_ref, don_ref, ds_scr):
        @pl.when(pl.program_id(0) == 0)
        def _():
            ds_scr[...] = jnp.zeros_like(ds_scr)
            dal_ref[...] = jnp.zeros_like(dal_ref)
            ddt_ref[...] = jnp.zeros_like(ddt_ref)
            don_ref[...] = jnp.zeros_like(don_ref)

        tsv = ti_ref[...]
        fn = lambda s0, qc, kc, vc, z, pb, al, dt, on: delta_pair(s0, qc, kc, vc, z, pb, al, dt, on, tsv)[:2]
        _, vjp = jax.vjp(fn, st_ref[...], _heads(qkv_ref, 0, h), _heads(qkv_ref, h, h), _heads(qkv_ref, 2 * h, h),
                         _heads(z_ref, 0, h), pba_ref[...], al_ref[...], dt_ref[...], on_ref[...])
        ds0, dq, dk, dv, dz, dpba, dal, ddt, don = vjp((_heads(dc_ref, 0, h), ds_scr[...]))
        ds_scr[...] = ds0
        for i in range(h):
            dqkv_ref[:, _hcols(i)] = dq[i]
            dqkv_ref[:, _hcols(h + i)] = dk[i]
            dqkv_ref[:, _hcols(2 * h + i)] = dv[i]
            dz_ref[:, _hcols(i)] = dz[i].astype(dz_ref.dtype)
        dal_ref[...] += dal
        ddt_ref[...] += ddt
        dpba_ref[...] = dpba.astype(dpba_ref.dtype)
        don_ref[...] += don

    return pl.pallas_call(
        body, name=name, grid=(npair,),
        out_shape=(jax.ShapeDtypeStruct((t, 3 * hd), F32), jax.ShapeDtypeStruct(pm.shape, BF16),
                   jax.ShapeDtypeStruct(pba.shape, BF16),
                   jax.ShapeDtypeStruct((h, 1, 1), F32), jax.ShapeDtypeStruct((h, 1, 1), F32),
                   jax.ShapeDtypeStruct((1, HEAD_DIM), F32)),
        in_specs=[wide, zsp, bsp, par, par, onsp, mat, mat, pl.BlockSpec((PAIR, hd), lambda p: (rev(p), 0))],
        out_specs=(wide, zsp, bsp, par, par, onsp),
        scratch_shapes=[pltpu.VMEM((h, HEAD_DIM, HEAD_DIM), F32)],
        compiler_params=_cparams(("arbitrary",)),
    )(qkv, pm, pba, alog, dtb, onorm, states, tinvs, dcat)


def gmlp_block(ln_g, ln_b, w, bcol, u_raw, v_raw):
    n = w.shape[-1]
    ri = lax.broadcasted_iota(jnp.int32, (n, n), 0)
    ci = lax.broadcasted_iota(jnp.int32, (n, n), 1)
    mask = ((ci // CHUNK) <= (ri // CHUNK)).astype(F32)
    vg = _gelu(v_raw)
    vc = vg - jnp.mean(vg, axis=-1, keepdims=True)
    vgn = vc * lax.rsqrt(jnp.mean(vc * vc, axis=-1, keepdims=True) + EPS) * ln_g + ln_b
    return _gelu(u_raw) * (_dot(w * mask, vgn) + bcol)


def gmlp_fwd(pm, ln_g, ln_b, w_s, bcol, cat, *, name):
    t = pm.shape[0]
    g = w_s.shape[0]
    gw = g * HEAD_DIM
    assert pm.shape[1] == 6 * gw and cat.shape[1] == 2 * gw

    def body(u_ref, v_ref, lg_ref, lb_ref, w_ref, b_ref, cat_in, o_ref):
        del cat_in
        out = gmlp_block(lg_ref[...], lb_ref[...], w_ref[...], b_ref[...], _heads(u_ref, 0, g), _heads(v_ref, 0, g))
        for i in range(g):
            o_ref[:, _hcols(i)] = out[i].astype(o_ref.dtype)

    full = lambda a: pl.BlockSpec(a.shape, lambda m: (0, 0, 0))
    return pl.pallas_call(
        body, name=name, grid=(t // HEAD_DIM,),
        out_shape=jax.ShapeDtypeStruct(cat.shape, cat.dtype),
        in_specs=[pl.BlockSpec((HEAD_DIM, gw), lambda m: (m, 4)), pl.BlockSpec((HEAD_DIM, gw), lambda m: (m, 5)),
                  full(ln_g), full(ln_b), full(w_s), full(bcol), pl.BlockSpec(memory_space=pl.ANY)],
        out_specs=pl.BlockSpec((HEAD_DIM, gw), lambda m: (m, 1)),
        input_output_aliases={6: 0},
        compiler_params=_cparams(("arbitrary",)),
    )(pm, pm, ln_g, ln_b, w_s, bcol, cat)


def gmlp_bwd(pm, ln_g, ln_b, w_s, bcol, dcat, dpm, *, name):
    t = pm.shape[0]
    g = w_s.shape[0]
    gw = g * HEAD_DIM
    assert pm.shape[1] == 6 * gw and dpm.shape == pm.shape

    def body(u_ref, v_ref, lg_ref, lb_ref, w_ref, b_ref, dc_ref, dpm_in, duv_ref, dlg_ref, dlb_ref, dw_ref, db_ref):
        del dpm_in
        first = pl.program_id(0) == 0
        _, vjp = jax.vjp(gmlp_block, lg_ref[...], lb_ref[...], w_ref[...], b_ref[...], _heads(u_ref, 0, g), _heads(v_ref, 0, g))
        dlg, dlb, dw, db, du, dv = vjp(_heads(dc_ref, 0, g))
        for i in range(g):
            duv_ref[:, _hcols(i)] = du[i].astype(duv_ref.dtype)
            duv_ref[:, _hcols(g + i)] = dv[i].astype(duv_ref.dtype)
        for ref, val in ((dlg_ref, dlg), (dlb_ref, dlb), (dw_ref, dw), (db_ref, db)):
            @pl.when(first)
            def _(ref=ref, val=val):
                ref[...] = val

            @pl.when(jnp.logical_not(first))
            def _(ref=ref, val=val):
                ref[...] += val

    full = lambda a: pl.BlockSpec(a.shape, lambda m: (0, 0, 0))
    return pl.pallas_call(
        body, name=name, grid=(t // HEAD_DIM,),
        out_shape=(jax.ShapeDtypeStruct(dpm.shape, dpm.dtype),
                   jax.ShapeDtypeStruct(ln_g.shape, F32), jax.ShapeDtypeStruct(ln_b.shape, F32),
                   jax.ShapeDtypeStruct(w_s.shape, F32), jax.ShapeDtypeStruct(bcol.shape, F32)),
        in_specs=[pl.BlockSpec((HEAD_DIM, gw), lambda m: (m, 4)), pl.BlockSpec((HEAD_DIM, gw), lambda m: (m, 5)),
                  full(ln_g), full(ln_b), full(w_s), full(bcol), pl.BlockSpec((HEAD_DIM, gw), lambda m: (m, 1)),
                  pl.BlockSpec(memory_space=pl.ANY)],
        out_specs=(pl.BlockSpec((HEAD_DIM, 2 * gw), lambda m: (m, 2)), full(ln_g), full(ln_b), full(w_s), full(bcol)),
        input_output_aliases={7: 0},
        compiler_params=_cparams(("arbitrary",)),
    )(pm, pm, ln_g, ln_b, w_s, bcol, dcat, dpm)


def loss_head(g, x, r, tgt, *, name, rows=ROWS):
    t, d = x.shape

    def body(g_ref, x_ref, r_ref, t_ref, l_ref, dx_ref, dxb_ref, dg_ref):
        y, vjp = jax.vjp(lambda gg, xx: _rms(xx, gg), g_ref[...], x_ref[...] + r_ref[...])
        e = y - t_ref[...]
        part = (0.5 / d) * jnp.sum(jnp.sum(e * e, axis=1, keepdims=True), axis=0, keepdims=True)
        dg, dx = vjp(e * (1.0 / d))
        dx_ref[...] = dx
        dxb_ref[...] = dx.astype(BF16)
        first = pl.program_id(0) == 0

        @pl.when(first)
        def _():
            l_ref[...] = part
            dg_ref[...] = dg

        @pl.when(jnp.logical_not(first))
        def _():
            l_ref[...] += part
            dg_ref[...] += dg

    rs = pl.BlockSpec((rows, d), lambda i: (i, 0))
    gs = pl.BlockSpec((1, d), lambda i: (0, 0))
    return pl.pallas_call(
        body, name=name, grid=(t // rows,),
        out_shape=(jax.ShapeDtypeStruct((1, 1), F32), jax.ShapeDtypeStruct((t, d), F32),
                   jax.ShapeDtypeStruct((t, d), BF16), jax.ShapeDtypeStruct((1, d), F32)),
        in_specs=[gs, rs, rs, rs],
        out_specs=(pl.BlockSpec((1, 1), lambda i: (0, 0)), rs, rs, gs),
        compiler_params=_cparams(("arbitrary",)),
    )(g, x, r, tgt)


def adamw(w, gs, m, v, *, name):
    nl, r, c = w.shape
    assert len(gs) == nl
    if r % SUBLANES == 0:
        tr, tc = _pick(r, (256, 128, 64, 32, 16, 8)), c
    else:
        tr, tc = r, _pick(c, (256, 128))
    k1 = 1.0 - ADAM_B1 ** ADAM_STEP
    k2 = 1.0 - ADAM_B2 ** ADAM_STEP

    def body(*refs):
        w_ref, m_ref, v_ref = refs[0], refs[1], refs[2]
        g_refs = refs[3:3 + nl]
        go_ref, d_ref, mo_ref, vo_ref = refs[3 + nl:]
        gg = g_refs[0][...]
        for li in range(1, nl):
            gg = jnp.where(pl.program_id(0) == li, g_refs[li][...], gg)
        mn = ADAM_B1 * m_ref[...] + (1.0 - ADAM_B1) * gg
        vn = ADAM_B2 * v_ref[...] + (1.0 - ADAM_B2) * (gg * gg)
        go_ref[...] = gg
        d_ref[...] = -ADAM_LR * ((mn / k1) / (jnp.sqrt(vn / k2) + ADAM_EPS) + ADAM_WD * w_ref[...])
        mo_ref[...] = mn
        vo_ref[...] = vn

    if tc == c:
        sp = pl.BlockSpec((None, tr, c), lambda l, i: (l, i, 0))
        gsp = pl.BlockSpec((tr, c), lambda l, i: (i, 0))
    else:
        sp = pl.BlockSpec((None, r, tc), lambda l, i: (l, 0, i))
        gsp = pl.BlockSpec((r, tc), lambda l, i: (0, i))
    sds = jax.ShapeDtypeStruct((nl, r, c), F32)
    return pl.pallas_call(
        body, name=name, grid=(nl, (r // tr) * (c // tc)), out_shape=(sds, sds, sds, sds),
        in_specs=[sp, sp, sp] + [gsp] * nl, out_specs=(sp, sp, sp, sp),
        compiler_params=_cparams(("parallel", "parallel")),
    )(w, m, v, *gs)


MESH = pl.DeviceIdType.MESH
ANY = pl.BlockSpec(memory_space=pl.ANY)


def _me():
    return lax.axis_index("x"), lax.axis_index("y"), lax.axis_index("c")


def _other_chips(x, y):
    return [(1 - x, y), (x, 1 - y), (1 - x, 1 - y)]


def gather_shards(arrs, *, name):
    n = len(arrs)
    per = 7

    def body(*refs):
        ins, outs = refs[:n], refs[n:2 * n]
        send_sems, recv_sems = refs[2 * n:]
        x, y, c = _me()
        chip = 2 * x + y
        sib = (x, y, 1 - c)
        chips = _other_chips(x, y)

        def rc(i, k, src, dst, to):
            return pltpu.make_async_remote_copy(src_ref=src, dst_ref=dst, send_sem=send_sems.at[per * i + k],
                                                recv_sem=recv_sems.at[per * i + k], device_id=to, device_id_type=MESH)

        def half(i, which):
            hrows = arrs[i].shape[0] // 2
            return pl.ds(which * hrows, hrows)

        own = [rc(i, 6, ins[i], outs[i].at[chip], sib) for i in range(n)]
        for cp in own:
            cp.start()
        started = []
        for i in range(n):
            for j, ch in enumerate(chips):
                cp = rc(i, j, ins[i].at[half(i, c)], outs[i].at[chip, half(i, c)], (ch[0], ch[1], c))
                cp.start()
                started.append(cp)
        for i in range(n):
            for j, ch in enumerate(chips):
                sc = 2 * ch[0] + ch[1]
                landed = outs[i].at[sc, half(i, c)]
                rc(i, j, ins[i].at[half(i, c)], landed, (ch[0], ch[1], c)).wait_recv()
                fw = rc(i, 3 + j, landed, landed, sib)
                fw.start()
                started.append(fw)
        for i in range(n):
            for j, ch in enumerate(chips):
                sc = 2 * ch[0] + ch[1]
                theirs = outs[i].at[sc, half(i, 1 - c)]
                rc(i, 3 + j, theirs, theirs, sib).wait_recv()
        for cp in own:
            cp.wait_recv()
        for cp in started + own:
            cp.wait_send()

    return pl.pallas_call(
        body, name=name,
        out_shape=tuple(jax.ShapeDtypeStruct((N_CHIPS,) + a.shape, a.dtype) for a in arrs),
        in_specs=[ANY] * n, out_specs=tuple([ANY] * n),
        scratch_shapes=[pltpu.SemaphoreType.DMA((per * n,)), pltpu.SemaphoreType.DMA((per * n,))],
        compiler_params=pltpu.CompilerParams(has_side_effects=True),
    )(*arrs)


def sibling_send_other_half(arrs, *, name):
    n = len(arrs)

    def body(*refs):
        ins, outs = refs[:n], refs[n:2 * n]
        send_sems, recv_sems = refs[2 * n:]
        x, y, c = _me()
        cps = []
        for i in range(n):
            hrows = arrs[i].shape[1] // 2
            cps.append(pltpu.make_async_remote_copy(
                src_ref=ins[i].at[:, pl.ds((1 - c) * hrows, hrows), :], dst_ref=outs[i], send_sem=send_sems.at[i],
                recv_sem=recv_sems.at[i], device_id=(x, y, 1 - c), device_id_type=MESH))
        for cp in cps:
            cp.start()
        for cp in cps:
            cp.wait_recv()
        for cp in cps:
            cp.wait_send()

    return pl.pallas_call(
        body, name=name,
        out_shape=tuple(jax.ShapeDtypeStruct((a.shape[0], a.shape[1] // 2, a.shape[2]), a.dtype) for a in arrs),
        in_specs=[ANY] * n, out_specs=tuple([ANY] * n),
        scratch_shapes=[pltpu.SemaphoreType.DMA((n,)), pltpu.SemaphoreType.DMA((n,))],
        compiler_params=pltpu.CompilerParams(has_side_effects=True),
    )(*arrs)


def chip_scatter(arrs, *, name):
    n = len(arrs)

    def body(*refs):
        ins, outs = refs[:n], refs[n:2 * n]
        send_sems, recv_sems = refs[2 * n:]
        x, y, c = _me()
        cps = []
        for i in range(n):
            for j, ch in enumerate(_other_chips(x, y)):
                cps.append(pltpu.make_async_remote_copy(
                    src_ref=ins[i].at[2 * ch[0] + ch[1]], dst_ref=outs[i].at[j], send_sem=send_sems.at[3 * i + j],
                    recv_sem=recv_sems.at[3 * i + j], device_id=(ch[0], ch[1], c), device_id_type=MESH))
        for cp in cps:
            cp.start()
        for cp in cps:
            cp.wait_recv()
        for cp in cps:
            cp.wait_send()

    return pl.pallas_call(
        body, name=name,
        out_shape=tuple(jax.ShapeDtypeStruct((3,) + a.shape[1:], a.dtype) for a in arrs),
        in_specs=[ANY] * n, out_specs=tuple([ANY] * n),
        scratch_shapes=[pltpu.SemaphoreType.DMA((3 * n,)), pltpu.SemaphoreType.DMA((3 * n,))],
        compiler_params=pltpu.CompilerParams(has_side_effects=True),
    )(*arrs)


def sibling_join_halves(arrs, *, name):
    n = len(arrs)

    def body(*refs):
        outs = refs[n:2 * n]
        send_sems, recv_sems = refs[2 * n:]
        x, y, c = _me()
        cps = []
        for i in range(n):
            hrows = arrs[i].shape[0] // 2
            mine = outs[i].at[pl.ds(c * hrows, hrows)]
            cps.append(pltpu.make_async_remote_copy(src_ref=mine, dst_ref=mine, send_sem=send_sems.at[i],
                                                    recv_sem=recv_sems.at[i], device_id=(x, y, 1 - c), device_id_type=MESH))
        for cp in cps:
            cp.start()
        for i in range(n):
            hrows = arrs[i].shape[0] // 2
            theirs = outs[i].at[pl.ds((1 - c) * hrows, hrows)]
            pltpu.make_async_remote_copy(src_ref=theirs, dst_ref=theirs, send_sem=send_sems.at[i], recv_sem=recv_sems.at[i],
                                         device_id=(x, y, 1 - c), device_id_type=MESH).wait_recv()
        for cp in cps:
            cp.wait_send()

    return pl.pallas_call(
        body, name=name,
        out_shape=tuple(jax.ShapeDtypeStruct(a.shape, a.dtype) for a in arrs),
        in_specs=[ANY] * n, out_specs=tuple([ANY] * n),
        input_output_aliases={i: i for i in range(n)},
        scratch_shapes=[pltpu.SemaphoreType.DMA((n,)), pltpu.SemaphoreType.DMA((n,))],
        compiler_params=pltpu.CompilerParams(has_side_effects=True),
    )(*arrs)


def pre_reduce(gd, sib, who, *, name):
    _, r, c = gd.shape
    h = r // 2
    tr = _pick(h, (256, 128, 64, 32, 16, 8))
    nrb = h // tr

    def body(who_ref, gd_ref, sib_ref, pb_ref, own_ref):
        p = gd_ref[...] + sib_ref[...]
        pb_ref[...] = p.astype(BF16)

        @pl.when(pl.program_id(1) == who_ref[1])
        def _():
            own_ref[...] = p

    return pl.pallas_call(
        body, name=name,
        out_shape=(jax.ShapeDtypeStruct((N_CHIPS, h, c), BF16), jax.ShapeDtypeStruct((h, c), F32)),
        grid_spec=pltpu.PrefetchScalarGridSpec(
            num_scalar_prefetch=1, grid=(nrb, N_CHIPS),
            in_specs=[pl.BlockSpec((None, tr, c), lambda i, s, w: (s, w[0] * nrb + i, 0)),
                      pl.BlockSpec((None, tr, c), lambda i, s, w: (s, i, 0))],
            out_specs=(pl.BlockSpec((None, tr, c), lambda i, s, w: (s, i, 0)),
                       pl.BlockSpec((tr, c), lambda i, s, w: (i, 0)))),
        compiler_params=_cparams(("parallel", "arbitrary")),
    )(who, gd, sib)


def final_reduce(own, rcv, who, *, name):
    h, c = own.shape
    tr = _pick(h, (256, 128, 64, 32, 16, 8))
    nrb = h // tr

    def body(who_ref, own_ref, rcv_ref, o_ref):
        del who_ref
        acc = own_ref[...]
        for j in range(3):
            acc = acc + rcv_ref[j].astype(F32)
        o_ref[...] = acc

    return pl.pallas_call(
        body, name=name, out_shape=jax.ShapeDtypeStruct((2 * h, c), F32),
        grid_spec=pltpu.PrefetchScalarGridSpec(
            num_scalar_prefetch=1, grid=(nrb,),
            in_specs=[pl.BlockSpec((tr, c), lambda i, w: (i, 0)), pl.BlockSpec((3, tr, c), lambda i, w: (0, i, 0))],
            out_specs=pl.BlockSpec((tr, c), lambda i, w: (w[0] * nrb + i, 0))),
        compiler_params=_cparams(("parallel",)),
    )(who, own, rcv)


def sum8(a, *, name):
    _, r, c = a.shape
    tr = _pick(r, (256, 128, 64, 32, 16, 8))

    def body(a_ref, o_ref):
        acc = a_ref[0]
        for j in range(1, 8):
            acc = acc + a_ref[j]
        o_ref[...] = acc

    return pl.pallas_call(
        body, name=name, out_shape=jax.ShapeDtypeStruct((r, c), F32), grid=(r // tr,),
        in_specs=[pl.BlockSpec((8, tr, c), lambda i: (0, i, 0))],
        out_specs=pl.BlockSpec((tr, c), lambda i: (i, 0)),
        compiler_params=_cparams(("parallel",)),
    )(a)


def allgather8(a, *, name):
    def body(in_ref, out_ref, send_sems, recv_sems, local_sem):
        x, y, c = _me()
        me = 4 * x + 2 * y + c
        loc = pltpu.make_async_copy(in_ref, out_ref.at[me], local_sem)
        loc.start()
        cps = []
        for k in range(1, 8):
            fx, fy, fc = (k >> 2) & 1, (k >> 1) & 1, k & 1
            to = (x + fx - 2 * x * fx, y + fy - 2 * y * fy, c + fc - 2 * c * fc)
            cps.append(pltpu.make_async_remote_copy(src_ref=in_ref, dst_ref=out_ref.at[me], send_sem=send_sems.at[k - 1],
                                                    recv_sem=recv_sems.at[k - 1], device_id=to, device_id_type=MESH))
        for cp in cps:
            cp.start()
        for k in range(1, 8):
            fx, fy, fc = (k >> 2) & 1, (k >> 1) & 1, k & 1
            src = 4 * (x + fx - 2 * x * fx) + 2 * (y + fy - 2 * y * fy) + (c + fc - 2 * c * fc)
            pltpu.make_async_remote_copy(src_ref=in_ref, dst_ref=out_ref.at[src], send_sem=send_sems.at[k - 1],
                                         recv_sem=recv_sems.at[k - 1], device_id=(x, y, c), device_id_type=MESH).wait_recv()
        for cp in cps:
            cp.wait_send()
        loc.wait()

    return pl.pallas_call(
        body, name=name, out_shape=jax.ShapeDtypeStruct((8,) + a.shape, a.dtype),
        in_specs=[ANY], out_specs=ANY,
        scratch_shapes=[pltpu.SemaphoreType.DMA((7,)), pltpu.SemaphoreType.DMA((7,)), pltpu.SemaphoreType.DMA],
        compiler_params=pltpu.CompilerParams(has_side_effects=True),
    )(a)


def local_step(x, tgt, wb, ws):
    t, d = x.shape
    nh = ws["alog"].shape[0]
    ng = ws["w_s"].shape[0]
    assert nh == ng
    mix_w = (nh + ng) * HEAD_DIM

    (h0,) = rows_fwd(stage_norm, [ws["e_norm"]], [x], [(1, BF16, d)], name="f_norm_e")
    pm = mm_nt(h0, wb["w_main_t"], name="f_proj_main")
    pba = mm_nt(h0, wb["w_ba_t"], name="f_proj_ba")
    qkv = conv_fwd(pm, ws["conv_w"], mode="silu", name="f_conv4")
    cat, states, tinvs = delta_fwd(qkv, pm, pba, ws["alog"], ws["dtb"], ws["onorm"], mix_w, name="f_delta")
    cat = gmlp_fwd(pm, ws["lng"], ws["lnb"], ws["w_s"], ws["bs"], cat, name="f_gmlp")
    y0 = mm_nn(cat, wb["w_out"], name="f_out")
    x1, h1 = rows_fwd(stage_res_norm, [ws["f_norm0"]], [x, y0], [(0, F32, d), (1, BF16, d)], name="f_norm_f0")
    a1, s1 = mm_nn(h1, wb["w1_0"], name="f_mlp0_up", epilogue="relu2")
    y1 = mm_nn(s1, wb["w2_0"], name="f_mlp0_down")
    x2, h2 = rows_fwd(stage_res_norm, [ws["o_norm"]], [x1, y1], [(0, F32, d), (1, BF16, d)], name="f_norm_o")
    zz = mm_nn(h2, wb["pw1"], name="f_pw1")
    zparts = [(zz, d, 0), (zz, d, 1)]
    (gl,) = rows_fwd(stage_glu, [ws["b1a"], ws["b1b"]], zparts, [(0, F32, d)], name="f_glu")
    cv = conv_fwd(gl, ws["dw"], mode="plain", name="f_conv31")
    ln_params = [ws["dw_b"], ws["ln_g"], ws["ln_b"]]
    (sl,) = rows_fwd(stage_ln_silu, ln_params, [cv], [(0, BF16, d)], name="f_ln_silu")
    y2 = mm_nn(sl, wb["pw2"], name="f_pw2")
    x3, h3 = rows_fwd(stage_res_bias_norm, [ws["b2"], ws["f_norm1"]], [x2, y2], [(0, F32, d), (1, BF16, d)], name="f_norm_f1")
    a3, s3 = mm_nn(h3, wb["w1_1"], name="f_mlp1_up", epilogue="relu2")
    y3 = mm_nn(s3, wb["w2_1"], name="f_mlp1_down")
    loss, d4, d4b, g_final = loss_head(ws["final_norm"], x3, y3, tgt, name="loss_head")

    gb, gs = {}, {"final_norm": g_final}
    s_up = wb["w1_0"].shape[0]

    gb["w2_1"] = mm_tn(s3, d4b, groups=1, name="b_mlp1_down_w")
    dpre3 = mm_nt(d4b, wb["w2_1"], name="b_mlp1_down_x", mul=a3, out_dtype=BF16)
    gb["w1_1"] = mm_tn(h3, dpre3, groups=s_up, name="b_mlp1_up_w")
    dh3 = mm_nt(dpre3, wb["w1_1"], name="b_mlp1_up_x")
    d3, d3b, gs["b2"], gs["f_norm1"] = rows_bwd(stage_res_bias_norm, [ws["b2"], ws["f_norm1"]], [x2, y2], [d4, dh3],
                                                [(0, F32), (1, BF16)], name="b_norm_f1")
    gb["pw2"] = mm_tn(sl, d3b, groups=1, name="b_pw2_w")
    dsl = mm_nt(d3b, wb["pw2"], name="b_pw2_x")
    dcv, gs["dw_b"], gs["ln_g"], gs["ln_b"] = rows_bwd(stage_ln_silu, ln_params, [cv], [dsl], [(0, F32)], name="b_ln_silu")
    dgl, gs["dw"] = conv_bwd(gl, ws["dw"], dcv, name="b_conv31")
    dzz, gs["b1a"], gs["b1b"] = rows_bwd(stage_glu, [ws["b1a"], ws["b1b"]], zparts, [dgl], [(0, BF16), (1, BF16)],
                                         name="b_glu", concat=True)
    gb["pw1"] = mm_tn(h2, dzz, groups=wb["pw1"].shape[0], name="b_pw1_w")
    dh2 = mm_nt(dzz, wb["pw1"], name="b_pw1_x")
    d2, d2b, gs["o_norm"] = rows_bwd(stage_res_norm, [ws["o_norm"]], [x1, y1], [d3, dh2], [(0, F32), (1, BF16)], name="b_norm_o")
    gb["w2_0"] = mm_tn(s1, d2b, groups=1, name="b_mlp0_down_w")
    dpre1 = mm_nt(d2b, wb["w2_0"], name="b_mlp0_down_x", mul=a1, out_dtype=BF16)
    gb["w1_0"] = mm_tn(h1, dpre1, groups=s_up, name="b_mlp0_up_w")
    dh1 = mm_nt(dpre1, wb["w1_0"], name="b_mlp0_up_x")
    d1, d1b, gs["f_norm0"] = rows_bwd(stage_res_norm, [ws["f_norm0"]], [x, y0], [d2, dh1], [(0, F32), (1, BF16)], name="b_norm_f0")
    gb["w_out"] = mm_tn(cat, d1b, groups=1, name="b_out_w")
    dcat = mm_nt(d1b, wb["w_out"], name="b_out_x")
    dqkv_c, dpm, dpba, gs["alog"], gs["dtb"], gs["onorm"] = delta_bwd(
        qkv, pm, pba, ws["alog"], ws["dtb"], ws["onorm"], states, tinvs, dcat, name="b_delta")
    dpm, gs["lng"], gs["lnb"], gs["w_s"], gs["bs"] = gmlp_bwd(
        pm, ws["lng"], ws["lnb"], ws["w_s"], ws["bs"], dcat, dpm, name="b_gmlp")
    dconv = conv_fwd(pm, ws["conv_w"], mode="silu_bwd", g=dqkv_c, name="b_conv4_silu")
    dpm, gs["conv_w"] = conv_bwd(pm, ws["conv_w"], dconv, name="b_conv4", into=dpm)
    gb["w_main_t"] = mm_tn(dpm, h0, groups=1, name="b_proj_main_w")
    gb["w_ba_t"] = mm_tn(dpba, h0, groups=1, name="b_proj_ba_w")
    dh0 = mm_nn(dpm, wb["w_main_t"], name="b_proj_main_x")
    dh0 = mm_nn(dpba, wb["w_ba_t"], name="b_proj_ba_x", add=dh0)
    grad_x, gs["e_norm"] = rows_bwd(stage_norm, [ws["e_norm"]], [x], [d1, dh0], [(0, F32)], name="b_norm_e")
    return loss, grad_x, gb, gs


WEIGHTS = ["e_norm", "e_w_in", "e_conv_w", "e_a_log", "e_dt_bias", "e_o_norm", "e_ln_g", "e_ln_b", "e_w_s", "e_b_s", "e_w_out",
           "o_norm", "o_pw1", "o_pw1_b", "o_dw", "o_dw_b", "o_ln_g", "o_ln_b", "o_pw2", "o_pw2_b", "f_norm", "f_w1", "f_w2",
           "final_norm"]
BIG = ["e_w_in", "e_w_out", "o_pw1", "o_pw2", "f_w1", "f_w2"]
SMALL_SHARDED = ["e_conv_w", "o_norm", "o_pw1_b", "o_dw", "o_dw_b", "o_ln_g", "o_ln_b", "o_pw2_b"]
SMALL = [n for n in WEIGHTS if n not in BIG]
LANES = 128
PACK_ROWS = 16
IN_ROW_MULT = 256


def _pack(arrs):
    flat = jnp.concatenate([a.reshape(-1).astype(F32) for a in arrs])
    n = flat.shape[0]
    rows = -(-n // (LANES * PACK_ROWS)) * PACK_ROWS
    return jnp.pad(flat, (0, rows * LANES - n)).reshape(rows, LANES)


def _unpack(flat, shapes, lead=()):
    outs, off = [], 0
    for s in shapes:
        n = int(np.prod(s))
        outs.append(flat[..., off:off + n].reshape(lead + tuple(s)))
        off += n
    return outs


def _chip_major(g, s=N_CHIPS):
    k, n = g.shape
    return g.reshape(k, s, n // s).transpose(1, 0, 2)


def kernel(x, e_norm, e_w_in, e_conv_w, e_a_log, e_dt_bias, e_o_norm, e_ln_g, e_ln_b, e_w_s, e_b_s, e_w_out, o_norm, o_pw1, o_pw1_b, o_dw, o_dw_b, o_ln_g, o_ln_b, o_pw2, o_pw2_b, f_norm, f_w1, f_w2, final_norm, loss_target, m_e_norm, m_e_w_in, m_e_conv_w, m_e_a_log, m_e_dt_bias, m_e_o_norm, m_e_ln_g, m_e_ln_b, m_e_w_s, m_e_b_s, m_e_w_out, m_o_norm, m_o_pw1, m_o_pw1_b, m_o_dw, m_o_dw_b, m_o_ln_g, m_o_ln_b, m_o_pw2, m_o_pw2_b, m_f_norm, m_f_w1, m_f_w2, m_final_norm, v_e_norm, v_e_w_in, v_e_conv_w, v_e_a_log, v_e_dt_bias, v_e_o_norm, v_e_ln_g, v_e_ln_b, v_e_w_s, v_e_b_s, v_e_w_out, v_o_norm, v_o_pw1, v_o_pw1_b, v_o_dw, v_o_dw_b, v_o_ln_g, v_o_ln_b, v_o_pw2, v_o_pw2_b, v_f_norm, v_f_w1, v_f_w2, v_final_norm):
    a = dict(locals())
    xi, yi, ci = _me()
    chip = 2 * xi + yi
    who = jnp.stack([ci, chip]).astype(jnp.int32)
    t, d = x.shape[1], x.shape[2]
    nh, ng = e_a_log.shape[1], e_w_s.shape[1]
    n_qkv, n_av, n_bw = 3 * nh * HEAD_DIM, nh * HEAD_DIM, ng * HEAD_DIM
    in_cols = n_qkv + n_av + 2 * nh + 2 * n_bw
    c_ba = n_qkv + n_av

    sh_in = in_cols // N_CHIPS
    pad_in = -(-sh_in // IN_ROW_MULT) * IN_ROW_MULT - sh_in
    w_in_t_local = jnp.pad(e_w_in[0].T.astype(BF16), ((0, pad_in), (0, 0)))
    big_local = [e_w_out[0], o_pw1[0], o_pw2[0], f_w1[0], f_w1[1], f_w2[0], f_w2[1]]
    small_local = [a[n] for n in SMALL_SHARDED]
    gathered = gather_shards([w_in_t_local] + [w.astype(BF16) for w in big_local] + [_pack(small_local)], name="gather_weights")
    g_in, g_out, g_pw1, g_pw2, g_w1_0, g_w1_1, g_w2_0, g_w2_1, g_small = gathered

    def in_rows(lo, hi):
        parts = []
        for s in range(N_CHIPS):
            a0, a1 = max(lo, s * sh_in), min(hi, (s + 1) * sh_in)
            if a0 < a1:
                parts.append(g_in[s, a0 - s * sh_in:a1 - s * sh_in])
        return parts

    wb = {
        "w_main_t": jnp.concatenate(in_rows(0, c_ba) + in_rows(c_ba + 2 * nh, in_cols), axis=0)[None],
        "w_ba_t": jnp.pad(jnp.concatenate(in_rows(c_ba, c_ba + 2 * nh), axis=0), ((0, LANES - 2 * nh), (0, 0)))[None],
        "w_out": g_out.reshape(1, -1, d), "pw1": g_pw1, "pw2": g_pw2.reshape(1, -1, d),
        "w1_0": g_w1_0, "w1_1": g_w1_1, "w2_0": g_w2_0.reshape(1, -1, d), "w2_1": g_w2_1.reshape(1, -1, d),
    }
    pieces = _unpack(g_small.reshape(N_CHIPS, -1), [w.shape for w in small_local], lead=(N_CHIPS,))
    full = {n: jnp.moveaxis(p, 0, -2).reshape(p.shape[1:-1] + (N_CHIPS * p.shape[-1],)) for n, p in zip(SMALL_SHARDED, pieces)}
    ws = {
        "e_norm": e_norm, "conv_w": full["e_conv_w"][0], "alog": e_a_log.reshape(nh, 1, 1), "dtb": e_dt_bias.reshape(nh, 1, 1),
        "onorm": e_o_norm, "lng": e_ln_g.reshape(ng, 1, HEAD_DIM), "lnb": e_ln_b.reshape(ng, 1, HEAD_DIM), "w_s": e_w_s[0],
        "bs": e_b_s.reshape(ng, HEAD_DIM, 1), "o_norm": full["o_norm"], "b1a": full["o_pw1_b"][:, :d], "b1b": full["o_pw1_b"][:, d:],
        "dw": full["o_dw"][0], "dw_b": full["o_dw_b"], "ln_g": full["o_ln_g"], "ln_b": full["o_ln_b"], "b2": full["o_pw2_b"],
        "f_norm0": f_norm[0:1], "f_norm1": f_norm[1:2], "final_norm": final_norm.reshape(1, d),
    }

    loss, grad_x, gb, gs = local_step(x[0], loss_target[0], wb, ws)

    gm, gba = gb["w_main_t"][0], gb["w_ba_t"][0]
    g_in_t = jnp.concatenate([gm[:c_ba], gba[:2 * nh], gm[c_ba:]], axis=0).reshape(N_CHIPS, sh_in, d)
    by_chip = [jnp.pad(g_in_t, ((0, 0), (0, pad_in), (0, 0))), gb["w_out"].reshape(N_CHIPS, -1, d), gb["pw1"],
               gb["pw2"].reshape(N_CHIPS, -1, d),
               gb["w1_0"], gb["w1_1"], gb["w2_0"].reshape(N_CHIPS, -1, d), gb["w2_1"].reshape(N_CHIPS, -1, d)]
    from_sibling = sibling_send_other_half(by_chip, name="reduce_sibling")
    to_send, own = [], []
    for i, (gd, sb) in enumerate(zip(by_chip, from_sibling)):
        pb, ow = pre_reduce(gd, sb, who, name=f"reduce_pair_{i}")
        to_send.append(pb)
        own.append(ow)
    received = chip_scatter(to_send, name="reduce_chips")
    halves = [final_reduce(ow, rc, who, name=f"reduce_sum_{i}") for i, (ow, rc) in enumerate(zip(own, received))]
    r_in, r_out, r_pw1, r_pw2, r_w1_0, r_w1_1, r_w2_0, r_w2_1 = sibling_join_halves(halves, name="reduce_join")
    big_grads = {"e_w_in": [r_in[:sh_in]], "e_w_out": [r_out], "o_pw1": [r_pw1], "o_pw2": [r_pw2], "f_w1": [r_w1_0, r_w1_1],
                 "f_w2": [r_w2_0, r_w2_1]}

    small_global = {
        "e_norm": gs["e_norm"], "e_conv_w": gs["conv_w"][None], "e_a_log": gs["alog"].reshape(1, nh), "e_dt_bias": gs["dtb"].reshape(1, nh),
        "e_o_norm": gs["onorm"], "e_ln_g": gs["lng"].reshape(1, n_bw), "e_ln_b": gs["lnb"].reshape(1, n_bw), "e_w_s": gs["w_s"][None],
        "e_b_s": gs["bs"].reshape(1, ng, HEAD_DIM), "o_norm": gs["o_norm"], "o_pw1_b": jnp.concatenate([gs["b1a"], gs["b1b"]], axis=1),
        "o_dw": gs["dw"][None], "o_dw_b": gs["dw_b"], "o_ln_g": gs["ln_g"], "o_ln_b": gs["ln_b"], "o_pw2_b": gs["b2"],
        "f_norm": jnp.concatenate([gs["f_norm0"], gs["f_norm1"]], axis=0), "final_norm": gs["final_norm"].reshape(d),
    }
    packed = _pack([small_global[n] for n in SMALL])
    summed = sum8(allgather8(packed, name="reduce_small"), name="reduce_small_sum")
    small_full = _unpack(summed.reshape(-1), [small_global[n].shape for n in SMALL])
    small_grads = {}
    for n, g in zip(SMALL, small_full):
        if n in SMALL_SHARDED:
            width = a[n].shape[-1]
            g = lax.dynamic_slice_in_dim(g, chip * width, width, axis=g.ndim - 1)
        small_grads[n] = g

    out = {}
    for n in BIG:
        w, m, v = a[n], a["m_" + n], a["v_" + n]
        if n == "e_w_in":
            res = adamw(w.transpose(0, 2, 1), big_grads[n], m.transpose(0, 2, 1), v.transpose(0, 2, 1), name="adamw_" + n)
            out[n] = tuple(r.transpose(0, 2, 1) for r in res)
        else:
            out[n] = adamw(w, big_grads[n], m, v, name="adamw_" + n)
    sw, sm, sv, sg = (_pack([src[n] for n in SMALL])[None] for src in
                      ({n: a[n] for n in SMALL}, {n: a["m_" + n] for n in SMALL}, {n: a["v_" + n] for n in SMALL}, small_grads))
    res = adamw(sw, [sg[0]], sm, sv, name="adamw_small")
    shapes = [a[n].shape for n in SMALL]
    unpacked = [_unpack(r.reshape(-1), shapes) for r in res]
    for i, n in enumerate(SMALL):
        out[n] = tuple(u[i] for u in unpacked)

    total = lax.psum(loss[0, 0], ("x", "y", "c"))
    result = [total, grad_x[None]]
    for k in range(4):
        result += [out[n][k] for n in WEIGHTS]
    return tuple(result)
```

```python
import functools
import math

import jax
import jax.numpy as jnp
import numpy as np
from jax import lax
from jax.experimental import pallas as pl
from jax.experimental.pallas import tpu as pltpu

F32 = jnp.float32
BF16 = jnp.bfloat16
HI = lax.Precision.HIGHEST

EPS = 1e-6
CHUNK = 64
PAIR = 2 * CHUNK
HEAD_DIM = 128
A_CONV = 4
C_WIDTH = 31
N_CHIPS = 4
ADAM_LR, ADAM_B1, ADAM_B2, ADAM_EPS, ADAM_WD, ADAM_STEP = 0.001, 0.9, 0.999, 1e-08, 0.01, 10

VMEM_LIMIT = 56 * 1024 * 1024


def _cparams(sem=None):
    return pltpu.CompilerParams(dimension_semantics=sem, vmem_limit_bytes=VMEM_LIMIT)


def _pick(n, prefs):
    for p in prefs:
        if n % p == 0:
            return p
    return n


def mm_nn(a, w, *, name, epilogue=None, add=None, out_dtype=F32):
    m, k = a.shape
    s, _, ns = w.shape
    n = s * ns
    tm = _pick(m, (1024, 512, 256, 128))
    tn = _pick(ns, (1024, 512, 256, 128))
    tk = _pick(k, (2048, 1024, 512, 256, 128))
    nk = k // tk
    npb = ns // tn
    assert add is None or epilogue is None

    def body(a_ref, w_ref, *rest):
        if add is not None:
            add_ref, rest = rest[0], rest[1:]
        if epilogue == "relu2":
            o1_ref, o2_ref = rest[0], rest[1]
            acc_ref = rest[2] if nk > 1 else None
        else:
            o1_ref = rest[0]
            acc_ref = rest[1] if nk > 1 else None
        part = jnp.dot(a_ref[...], w_ref[...], preferred_element_type=F32)

        def finish(c):
            if epilogue == "relu2":
                r = jnp.maximum(c, 0.0)
                o1_ref[...] = r.astype(o1_ref.dtype)
                o2_ref[...] = (r * r).astype(o2_ref.dtype)
            elif add is not None:
                o1_ref[...] = (c + add_ref[...].astype(F32)).astype(o1_ref.dtype)
            else:
                o1_ref[...] = c.astype(o1_ref.dtype)

        if nk == 1:
            finish(part)
        else:
            kk = pl.program_id(2)

            @pl.when(kk == 0)
            def _():
                acc_ref[...] = part

            @pl.when(kk > 0)
            def _():
                acc_ref[...] += part

            @pl.when(kk == nk - 1)
            def _():
                finish(acc_ref[...])

    o_spec = pl.BlockSpec((tm, tn), lambda i, j, kk: (i, j))
    if epilogue == "relu2":
        out_shape = (jax.ShapeDtypeStruct((m, n), BF16), jax.ShapeDtypeStruct((m, n), BF16))
        out_specs = (o_spec, o_spec)
    else:
        out_shape = jax.ShapeDtypeStruct((m, n), out_dtype)
        out_specs = o_spec
    return pl.pallas_call(
        body, name=name, out_shape=out_shape,
        grid=(m // tm, n // tn, nk),
        in_specs=[pl.BlockSpec((tm, tk), lambda i, j, kk: (i, kk)),
                  pl.BlockSpec((None, tk, tn), lambda i, j, kk: (j // npb, kk, j % npb))] + ([o_spec] if add is not None else []),
        out_specs=out_specs,
        scratch_shapes=[pltpu.VMEM((tm, tn), F32)] if nk > 1 else [],
        compiler_params=_cparams(("parallel", "parallel", "arbitrary")),
    )(*([a, w] + ([add] if add is not None else [])))


def mm_nt(a, w, *, name, mul=None, add=None, out_dtype=F32):
    assert mul is None or add is None
    if add is not None:
        mul = add
    m, n = a.shape
    s, k, ns = w.shape
    assert n == s * ns
    tm = _pick(m, (1024, 512, 256, 128))
    tko = _pick(k, (1024, 512, 256, 128))
    tn = _pick(ns, (2048, 1024, 512, 256, 128))
    nn = n // tn
    npb = ns // tn

    def body(a_ref, w_ref, *rest):
        if mul is not None:
            m_ref, o_ref = rest[0], rest[1]
            acc_ref = rest[2] if nn > 1 else None
        else:
            m_ref, o_ref = None, rest[0]
            acc_ref = rest[1] if nn > 1 else None
        part = lax.dot_general(a_ref[...], w_ref[...], (((1,), (1,)), ((), ())), preferred_element_type=F32)

        def finish(c):
            if add is not None:
                c = c + m_ref[...].astype(F32)
            elif m_ref is not None:
                c = c * (2.0 * m_ref[...].astype(F32))
            o_ref[...] = c.astype(o_ref.dtype)

        if nn == 1:
            finish(part)
        else:
            kk = pl.program_id(2)

            @pl.when(kk == 0)
            def _():
                acc_ref[...] = part

            @pl.when(kk > 0)
            def _():
                acc_ref[...] += part

            @pl.when(kk == nn - 1)
            def _():
                finish(acc_ref[...])

    in_specs = [pl.BlockSpec((tm, tn), lambda i, j, kk: (i, kk)),
                pl.BlockSpec((None, tko, tn), lambda i, j, kk: (kk // npb, j, kk % npb))]
    args = [a, w]
    if mul is not None:
        in_specs.append(pl.BlockSpec((tm, tko), lambda i, j, kk: (i, j)))
        args.append(mul)
    return pl.pallas_call(
        body, name=name, out_shape=jax.ShapeDtypeStruct((m, k), out_dtype),
        grid=(m // tm, k // tko, nn),
        in_specs=in_specs,
        out_specs=pl.BlockSpec((tm, tko), lambda i, j, kk: (i, j)),
        scratch_shapes=[pltpu.VMEM((tm, tko), F32)] if nn > 1 else [],
        compiler_params=_cparams(("parallel", "parallel", "arbitrary")),
    )(*args)


def mm_tn(a, b, *, groups, name):
    m, k = a.shape
    _, n = b.shape
    ns = n // groups
    tko = _pick(k, (1024, 512, 256, 128))
    tn = _pick(ns, (1024, 512, 256, 128))
    tc = _pick(m, (2048, 1024, 512, 256, 128))
    nc = m // tc
    npb = ns // tn

    def body(a_ref, b_ref, o_ref):
        part = lax.dot_general(a_ref[...], b_ref[...], (((0,), (0,)), ((), ())), preferred_element_type=F32)
        kk = pl.program_id(2)

        @pl.when(kk == 0)
        def _():
            o_ref[...] = part

        @pl.when(kk > 0)
        def _():
            o_ref[...] += part

    return pl.pallas_call(
        body, name=name, out_shape=jax.ShapeDtypeStruct((groups, k, ns), F32),
        grid=(k // tko, n // tn, nc),
        in_specs=[pl.BlockSpec((tc, tko), lambda i, j, kk: (kk, i)),
                  pl.BlockSpec((tc, tn), lambda i, j, kk: (kk, j))],
        out_specs=pl.BlockSpec((None, tko, tn), lambda i, j, kk: (j // npb, i, j % npb)),
        compiler_params=_cparams(("parallel", "parallel", "arbitrary")),
    )(a, b)


ROWS = 128


def _full_spec(arr):
    nd = arr.ndim
    return pl.BlockSpec(arr.shape, lambda i, _nd=nd: (0,) * _nd)


def _row_spec(x, rows):
    if isinstance(x, tuple):
        _, w, cb = x
        return pl.BlockSpec((rows, w), lambda i, _cb=cb: (i, _cb))
    return pl.BlockSpec((rows, x.shape[1]), lambda i: (i, 0))


def _arr(x):
    return x[0] if isinstance(x, tuple) else x


def _width(x):
    return x[1] if isinstance(x, tuple) else x.shape[1]


def rows_fwd(fn, params, xs, stores, *, name, rows=ROWS):
    t = _arr(xs[0]).shape[0]
    np_, nx = len(params), len(xs)

    def body(*refs):
        p_refs, x_refs, o_refs = refs[:np_], refs[np_:np_ + nx], refs[np_ + nx:]
        outs = fn(*[r[...].astype(F32) for r in p_refs], *[r[...].astype(F32) for r in x_refs])
        for (idx, dt, _), o_ref in zip(stores, o_refs):
            o_ref[...] = outs[idx].astype(dt)

    res = pl.pallas_call(
        body, name=name,
        out_shape=tuple(jax.ShapeDtypeStruct((t, w), dt) for _, dt, w in stores),
        grid=(t // rows,),
        in_specs=[_full_spec(p) for p in params] + [_row_spec(x, rows) for x in xs],
        out_specs=tuple(pl.BlockSpec((rows, w), lambda i: (i, 0)) for _, _, w in stores),
        compiler_params=_cparams(("parallel",)),
    )(*params, *[_arr(x) for x in xs])
    return res


def rows_bwd(fn, params, xs, cts, dx_stores, *, name, rows=ROWS, concat=False):
    t = _arr(xs[0]).shape[0]
    np_, nx = len(params), len(xs)
    ct_idx = [i for i, c in enumerate(cts) if c is not None]
    ct_arrs = [cts[i] for i in ct_idx]
    nct = len(ct_arrs)
    nds = 1 if concat else len(dx_stores)
    widths = [_width(xs[xi]) for xi, _ in dx_stores]

    def body(*refs):
        p_refs = refs[:np_]
        x_refs = refs[np_:np_ + nx]
        c_refs = refs[np_ + nx:np_ + nx + nct]
        d_refs = refs[np_ + nx + nct:np_ + nx + nct + nds]
        g_refs = refs[np_ + nx + nct + nds:]
        pv = [r[...].astype(F32) for r in p_refs]
        xv = [r[...].astype(F32) for r in x_refs]
        outs, vjp = jax.vjp(lambda *a: tuple(fn(*a)), *pv, *xv)
        ct_full = [jnp.zeros_like(o) for o in outs]
        for i, r in zip(ct_idx, c_refs):
            ct_full[i] = r[...].astype(F32)
        grads = vjp(tuple(ct_full))
        if concat:
            off = 0
            for (xi, dt), wd in zip(dx_stores, widths):
                d_refs[0][:, off:off + wd] = grads[np_ + xi].astype(dt)
                off += wd
        else:
            for (xi, dt), d_ref in zip(dx_stores, d_refs):
                d_ref[...] = grads[np_ + xi].astype(dt)
        step = pl.program_id(0)
        for j, g_ref in enumerate(g_refs):
            @pl.when(step == 0)
            def _(g_ref=g_ref, j=j):
                g_ref[...] = grads[j]

            @pl.when(step > 0)
            def _(g_ref=g_ref, j=j):
                g_ref[...] += grads[j]

    dx_shapes = [(sum(widths), dx_stores[0][1])] if concat else [(wd, dt) for wd, (_, dt) in zip(widths, dx_stores)]
    out_shape = tuple(jax.ShapeDtypeStruct((t, wd), dt) for wd, dt in dx_shapes) + \
        tuple(jax.ShapeDtypeStruct(p.shape, F32) for p in params)
    out_specs = tuple(pl.BlockSpec((rows, wd), lambda i: (i, 0)) for wd, _ in dx_shapes) + \
        tuple(_full_spec(p) for p in params)
    return pl.pallas_call(
        body, name=name, out_shape=out_shape, grid=(t // rows,),
        in_specs=[_full_spec(p) for p in params] + [_row_spec(x, rows) for x in xs] + [_row_spec(c, rows) for c in ct_arrs],
        out_specs=out_specs,
        compiler_params=_cparams(("arbitrary",)),
    )(*params, *[_arr(x) for x in xs], *[_arr(c) for c in ct_arrs])


def _rms(x, g):
    return x * lax.rsqrt(jnp.mean(x * x, axis=-1, keepdims=True) + EPS) * g


def _sigmoid(x):
    return 1.0 / (1.0 + jnp.exp(-x))


def _silu(x):
    return x * _sigmoid(x)


def _gelu(x):
    return 0.5 * x * (1.0 + lax.erf(x * (1.0 / math.sqrt(2.0))))


def stage_norm(g, x):
    return x, _rms(x, g)


def stage_res_norm(g, x, y):
    xn = x + y
    return xn, _rms(xn, g)


def stage_res_bias_norm(b, g, x, y):
    xn = x + y + b
    return xn, _rms(xn, g)


def stage_glu(ba, bb, za, zb):
    return ((za + ba) * _sigmoid(zb + bb),)


def stage_ln_silu(dw_b, ln_g, ln_b, cv):
    z = cv + dw_b
    mu = jnp.mean(z, axis=-1, keepdims=True)
    zc = z - mu
    y = zc * lax.rsqrt(jnp.mean(zc * zc, axis=-1, keepdims=True) + EPS) * ln_g + ln_b
    return (_silu(y),)


CONV_ROWS = 128
CONV_COLS = 256
SUBLANES = 8


def _halo(k):
    return SUBLANES * ((k - 1 + SUBLANES - 1) // SUBLANES)


def _taps_by_roll(k):
    out = {}
    for s in range(k):
        out.setdefault(s % SUBLANES, []).append((s // SUBLANES, s))
    return out


def _shifted_down(win, k, r):
    halo, n = _halo(k), win.shape[0]
    segs = {}
    for b, lst in _taps_by_roll(k).items():
        rolled = win if b == 0 else pltpu.roll(win, b, axis=0)
        for a, s in lst:
            segs[s] = rolled[halo - SUBLANES * a: halo - SUBLANES * a + r]
    return segs


def _shifted_up(win, k, r):
    n = win.shape[0]
    segs = {}
    for b, lst in _taps_by_roll(k).items():
        rolled = win if b == 0 else pltpu.roll(win, n - b, axis=0)
        for a, s in lst:
            segs[s] = rolled[SUBLANES * a: SUBLANES * a + r]
    return segs


def _for_blocks(nblk, fn):
    fn(0, True, nblk == 1)
    if nblk > 2:
        def step(i, c):
            fn(i, False, False)
            return c
        lax.fori_loop(1, nblk - 1, step, 0)
    if nblk > 1:
        fn(nblk - 1, False, True)


def _base(i, r):
    return i * r if isinstance(i, int) else pl.multiple_of(i * r, r)


def _win_top(ref, i, first, r, halo):
    if first:
        return jnp.concatenate([jnp.zeros((halo, ref.shape[1]), F32), ref[pl.ds(0, r), :]], axis=0)
    base = _base(i, r)
    return ref[pl.ds(base - halo, r + halo), :]


def _win_bottom(ref, i, last, r, halo):
    base = _base(i, r)
    if last:
        return jnp.concatenate([ref[pl.ds(base, r), :], jnp.zeros((halo, ref.shape[1]), F32)], axis=0)
    return ref[pl.ds(base, r + halo), :]


def conv_fwd(x, w, *, mode, name, g=None):
    t = x.shape[0]
    k, c = w.shape
    r, cb, halo = min(CONV_ROWS, t), min(CONV_COLS, c), _halo(k)
    nblk = t // r

    def body(*refs):
        if mode == "silu_bwd":
            x_ref, w_ref, g_ref, o_ref = refs
        else:
            x_ref, w_ref, o_ref = refs

        def blk(i, first, last):
            segs = _shifted_down(_win_top(x_ref, i, first, r, halo), k, r)
            acc = None
            for s in range(k):
                term = w_ref[pl.ds(k - 1 - s, 1), :] * segs[s]
                acc = term if acc is None else acc + term
            base = _base(i, r)
            if mode == "silu":
                acc = _silu(acc)
            elif mode == "silu_bwd":
                sg = _sigmoid(acc)
                acc = g_ref[pl.ds(base, r), :] * (sg * (1.0 + acc * (1.0 - sg)))
            o_ref[pl.ds(base, r), :] = acc

        _for_blocks(nblk, blk)

    col = pl.BlockSpec((t, cb), lambda j: (0, j))
    in_specs = [col, pl.BlockSpec((k, cb), lambda j: (0, j))] + ([col] if mode == "silu_bwd" else [])
    args = [x, w] + ([g] if mode == "silu_bwd" else [])
    return pl.pallas_call(
        body, name=name, out_shape=jax.ShapeDtypeStruct((t, c), F32), grid=(c // cb,),
        in_specs=in_specs, out_specs=col, compiler_params=_cparams(("parallel",)),
    )(*args)


def conv_bwd(x, w, dy, *, name, into=None):
    t = x.shape[0]
    k, c = w.shape
    r, cb, halo = min(CONV_ROWS, t), min(CONV_COLS, c), _halo(k)
    nblk = t // r

    def body(x_ref, w_ref, dy_ref, *rest):
        dx_ref, dw_ref = rest[-2], rest[-1]
        dw_ref[...] = jnp.zeros_like(dw_ref)

        def blk(i, first, last):
            base = _base(i, r)
            up = _shifted_up(_win_bottom(dy_ref, i, last, r, halo), k, r)
            down = _shifted_down(_win_top(x_ref, i, first, r, halo), k, r)
            dyb = up[0]
            acc = None
            for s in range(k):
                term = w_ref[pl.ds(k - 1 - s, 1), :] * up[s]
                acc = term if acc is None else acc + term
                dw_ref[pl.ds(k - 1 - s, 1), :] += jnp.sum(down[s] * dyb, axis=0, keepdims=True)
            dx_ref[pl.ds(base, r), :] = acc.astype(dx_ref.dtype)

        _for_blocks(nblk, blk)

    col = pl.BlockSpec((t, cb), lambda j: (0, j))
    wsp = pl.BlockSpec((k, cb), lambda j: (0, j))
    dx_shape = jax.ShapeDtypeStruct((t, c), F32) if into is None else jax.ShapeDtypeStruct(into.shape, into.dtype)
    return pl.pallas_call(
        body, name=name,
        out_shape=(dx_shape, jax.ShapeDtypeStruct((k, c), F32)), grid=(c // cb,),
        in_specs=[col, wsp, col] + ([] if into is None else [pl.BlockSpec(memory_space=pl.ANY)]),
        out_specs=(col, wsp),
        input_output_aliases={} if into is None else {3: 0},
        compiler_params=_cparams(("parallel",)),
    )(*([x, w, dy] + ([] if into is None else [into])))


_DIMS = {"nn": (((1,), (0,)), ((), ())), "nt": (((1,), (1,)), ((), ())), "tn": (((0,), (0,)), ((), ()))}
_DIMS_BATCHED = {"nn": (((2,), (1,)), ((0,), (0,))), "nt": (((2,), (2,)), ((0,), (0,))), "tn": (((1,), (1,)), ((0,), (0,)))}


def _mxu(a, b, mode):
    dims = _DIMS_BATCHED if a.ndim == 3 else _DIMS
    return lax.dot_general(a, b, dims[mode], preferred_element_type=F32)


def _split(x):
    hi = x.astype(BF16)
    return hi, (x - hi.astype(F32)).astype(BF16)


def _dot_raw(a, b, mode, prec):
    if prec == "bf16":
        return _mxu(a.astype(BF16), b.astype(BF16), mode)
    if prec == "x3":
        ah, al = _split(a)
        bh, bl = _split(b)
        return _mxu(ah, bh, mode) + (_mxu(ah, bl, mode) + _mxu(al, bh, mode))
    if prec == "x3r":
        bh, bm = _split(b)
        bl = (b - bh.astype(F32) - bm.astype(F32)).astype(BF16)
        ah = a.astype(BF16)
        return _mxu(ah, bh, mode) + (_mxu(ah, bm, mode) + _mxu(ah, bl, mode))
    raise ValueError(prec)


@functools.lru_cache(maxsize=None)
def _dot_fn(mode, prec):
    bprec = "x3" if prec == "x3r" else prec

    @jax.custom_vjp
    def f(a, b):
        return _dot_raw(a, b, mode, prec)

    def fwd(a, b):
        return _dot_raw(a, b, mode, prec), (a, b)

    def bwd(res, ct):
        a, b = res
        if mode == "nn":
            return _dot_raw(ct, b, "nt", bprec), _dot_raw(a, ct, "tn", bprec)
        if mode == "nt":
            return _dot_raw(ct, b, "nn", bprec), _dot_raw(ct, a, "tn", bprec)
        return _dot_raw(b, ct, "nt", bprec), _dot_raw(a, ct, "nn", bprec)

    f.defvjp(fwd, bwd)
    return f


def _dot(a, b, mode="nn", prec="bf16"):
    return _dot_fn(mode, prec)(a, b)


def _inv_product(l):
    n = l.shape[-1]
    eye = (lax.broadcasted_iota(jnp.int32, (n, n), 0) == lax.broadcasted_iota(jnp.int32, (n, n), 1)).astype(F32)
    p = eye - l
    pw = l
    for _ in range(5):
        pw = _dot_raw(pw, pw, "nn", "x3")
        p = _dot_raw(p, eye + pw, "nn", "x3")
    return p


@jax.custom_vjp
def _inv_unit_lower(l, t_saved):
    return t_saved


def _inv_fwd(l, t_saved):
    return t_saved, t_saved


def _inv_bwd(t, ct):
    tmp = _dot_raw(t, ct, "tn", "x3")
    return -_dot_raw(tmp, t, "nt", "x3"), jnp.zeros_like(t)


_inv_unit_lower.defvjp(_inv_fwd, _inv_bwd)


def _softplus(x):
    pos = x > 0
    return jnp.where(pos, x, 0.0) + jnp.log(1.0 + jnp.exp(jnp.where(pos, -x, x)))


def _l2n(x):
    return x * lax.rsqrt(jnp.sum(x * x, axis=-1, keepdims=True) + EPS)


def delta_pair(s0, qc, kc, vc, z, pba, alog, dtb, onorm, t_saved=None):
    n = PAIR
    nh = qc.shape[0]
    assert qc.shape == (nh, n, HEAD_DIM) and n == HEAD_DIM
    hi = lax.broadcasted_iota(jnp.int32, (nh, 1, pba.shape[1]), 0)
    li = lax.broadcasted_iota(jnp.int32, (nh, 1, pba.shape[1]), 2)
    braw = jnp.sum(pba[None] * (li == hi).astype(F32), axis=2, keepdims=True)
    araw = jnp.sum(pba[None] * (li == hi + nh).astype(F32), axis=2, keepdims=True)
    ri = lax.broadcasted_iota(jnp.int32, (n, n), 0)
    ci = lax.broadcasted_iota(jnp.int32, (n, n), 1)
    same = (ri // CHUNK) == (ci // CHUNK)
    tri = same & (ci <= ri)
    same_f = jnp.broadcast_to(same.astype(F32), (nh, n, n))
    tri_f = jnp.broadcast_to(tri.astype(F32), (nh, n, n))
    strict_f = (same & (ci < ri)).astype(F32)
    m0 = (lax.broadcasted_iota(jnp.int32, (n, 1), 0) < CHUNK).astype(F32)
    m1 = 1.0 - m0

    q = _l2n(qc) * (HEAD_DIM ** -0.5)
    k = _l2n(kc)
    beta = _sigmoid(braw)
    g = -jnp.exp(alog) * _softplus(araw + dtb)
    gb = jnp.broadcast_to(g, (nh, n, n))
    gc = _dot(tri_f, gb, "nn", "x3r")
    gtot = _dot(same_f, gb, "nn", "x3r")
    decay = jnp.exp(jnp.where(tri, gc - jnp.swapaxes(gc, 1, 2), -1e30))
    eg = jnp.exp(gc)
    kb, vb = k * beta, vc * beta
    l = _dot(kb, k, "nt") * decay * strict_f
    if t_saved is None:
        tinv = _inv_product(l)
    else:
        tinv = _inv_unit_lower(l, t_saved)
    u = _dot(tinv, vb, "nn", "x3")
    w = _dot(tinv, kb * eg, "nn", "x3")
    attn = _dot(q, k, "nt") * decay
    q_dec = q * eg
    k_tail = k * jnp.exp(gtot - gc)
    gl0 = jnp.exp(jnp.sum(gb * m0, axis=1, keepdims=True))
    gl1 = jnp.exp(jnp.sum(gb * m1, axis=1, keepdims=True))

    vn0 = m0 * (u - _dot(w, s0))
    s1 = s0 * gl0 + _dot(k_tail, vn0, "tn")
    vn1 = m1 * (u - _dot(w, s1))
    o = m0 * _dot(q_dec, s0) + m1 * _dot(q_dec, s1) + _dot(attn, vn0 + vn1)
    s2 = s1 * gl1 + _dot(k_tail, vn1, "tn")

    on = o * lax.rsqrt(jnp.mean(o * o, axis=-1, keepdims=True) + EPS) * onorm
    return on * _silu(z), s2, tinv


def _hcols(i):
    return slice(i * HEAD_DIM, (i + 1) * HEAD_DIM)


def _heads(ref, first, count):
    return jnp.stack([ref[:, _hcols(first + i)] for i in range(count)])


def delta_fwd(qkv, pm, pba, alog, dtb, onorm, cat_width, *, name):
    t = qkv.shape[0]
    h = alog.shape[0]
    hd = h * HEAD_DIM
    npair = t // PAIR
    mat = pl.BlockSpec((h, None, PAIR, HEAD_DIM), lambda p: (0, p, 0, 0))
    par = pl.BlockSpec((h, 1, 1), lambda p: (0, 0, 0))

    def body(qkv_ref, z_ref, pba_ref, al_ref, dt_ref, on_ref, o_ref, st_ref, ti_ref, s_scr):
        @pl.when(pl.program_id(0) == 0)
        def _():
            s_scr[...] = jnp.zeros_like(s_scr)

        s0 = s_scr[...]
        st_ref[...] = s0
        out, s2, tinv = delta_pair(s0, _heads(qkv_ref, 0, h), _heads(qkv_ref, h, h), _heads(qkv_ref, 2 * h, h), _heads(z_ref, 0, h),
                                   pba_ref[...], al_ref[...], dt_ref[...], on_ref[...])
        for i in range(h):
            o_ref[:, _hcols(i)] = out[i].astype(o_ref.dtype)
        ti_ref[...] = tinv
        s_scr[...] = s2

    return pl.pallas_call(
        body, name=name, grid=(npair,),
        out_shape=(jax.ShapeDtypeStruct((t, cat_width), BF16),
                   jax.ShapeDtypeStruct((h, npair, PAIR, HEAD_DIM), F32),
                   jax.ShapeDtypeStruct((h, npair, PAIR, PAIR), F32)),
        in_specs=[pl.BlockSpec((PAIR, 3 * hd), lambda p: (p, 0)), pl.BlockSpec((PAIR, hd), lambda p: (p, 3)),
                  pl.BlockSpec((PAIR, pba.shape[1]), lambda p: (p, 0)), par, par,
                  pl.BlockSpec((1, HEAD_DIM), lambda p: (0, 0))],
        out_specs=(pl.BlockSpec((PAIR, hd), lambda p: (p, 0)), mat, mat),
        scratch_shapes=[pltpu.VMEM((h, HEAD_DIM, HEAD_DIM), F32)],
        compiler_params=_cparams(("arbitrary",)),
    )(qkv, pm, pba, alog, dtb, onorm)


def delta_bwd(qkv, pm, pba, alog, dtb, onorm, states, tinvs, dcat, *, name):
    t = qkv.shape[0]
    h = alog.shape[0]
    hd = h * HEAD_DIM
    npair = t // PAIR
    rev = lambda p: npair - 1 - p
    mat = pl.BlockSpec((h, None, PAIR, HEAD_DIM), lambda p: (0, rev(p), 0, 0))
    par = pl.BlockSpec((h, 1, 1), lambda p: (0, 0, 0))
    onsp = pl.BlockSpec((1, HEAD_DIM), lambda p: (0, 0))
    wide = pl.BlockSpec((PAIR, 3 * hd), lambda p: (rev(p), 0))
    zsp = pl.BlockSpec((PAIR, hd), lambda p: (rev(p), 3))
    bsp = pl.BlockSpec((PAIR, pba.shape[1]), lambda p: (rev(p), 0))

    def body(qkv_ref, z_ref, pba_ref, al_ref, dt_ref, on_ref, st_ref, ti_ref, dc_ref,
             dqkv_ref, dz_ref, dpba_ref, dal_ref, ddt_ref, don_ref, ds_scr):
        @pl.when(pl.program_id(0) == 0)
        def _():
            ds_scr[...] = jnp.zeros_like(ds_scr)
            dal_ref[...] = jnp.zeros_like(dal_ref)
            ddt_ref[...] = jnp.zeros_like(ddt_ref)
            don_ref[...] = jnp.zeros_like(don_ref)

        tsv = ti_ref[...]
        fn = lambda s0, qc, kc, vc, z, pb, al, dt, on: delta_pair(s0, qc, kc, vc, z, pb, al, dt, on, tsv)[:2]
        _, vjp = jax.vjp(fn, st_ref[...], _heads(qkv_ref, 0, h), _heads(qkv_ref, h, h), _heads(qkv_ref, 2 * h, h),
                         _heads(z_ref, 0, h), pba_ref[...], al_ref[...], dt_ref[...], on_ref[...])
        ds0, dq, dk, dv, dz, dpba, dal, ddt, don = vjp((_heads(dc_ref, 0, h), ds_scr[...]))
        ds_scr[...] = ds0
        for i in range(h):
            dqkv_ref[:, _hcols(i)] = dq[i]
            dqkv_ref[:, _hcols(h + i)] = dk[i]
            dqkv_ref[:, _hcols(2 * h + i)] = dv[i]
            dz_ref[:, _hcols(i)] = dz[i].astype(dz_ref.dtype)
        dal_ref[...] += dal
        ddt_ref[...] += ddt
        dpba_ref[...] = dpba.astype(dpba_ref.dtype)
        don_ref[...] += don

    return pl.pallas_call(
        body, name=name, grid=(npair,),
        out_shape=(jax.ShapeDtypeStruct((t, 3 * hd), F32), jax.ShapeDtypeStruct(pm.shape, BF16),
                   jax.ShapeDtypeStruct(pba.shape, BF16),
                   jax.ShapeDtypeStruct((h, 1, 1), F32), jax.ShapeDtypeStruct((h, 1, 1), F32),
                   jax.ShapeDtypeStruct((1, HEAD_DIM), F32)),
        in_specs=[wide, zsp, bsp, par, par, onsp, mat, mat, pl.BlockSpec((PAIR, hd), lambda p: (rev(p), 0))],
        out_specs=(wide, zsp, bsp, par, par, onsp),
        scratch_shapes=[pltpu.VMEM((h, HEAD_DIM, HEAD_DIM), F32)],
        compiler_params=_cparams(("arbitrary",)),
    )(qkv, pm, pba, alog, dtb, onorm, states, tinvs, dcat)


def gmlp_block(ln_g, ln_b, w, bcol, u_raw, v_raw):
    n = w.shape[-1]
    ri = lax.broadcasted_iota(jnp.int32, (n, n), 0)
    ci = lax.broadcasted_iota(jnp.int32, (n, n), 1)
    mask = ((ci // CHUNK) <= (ri // CHUNK)).astype(F32)
    vg = _gelu(v_raw)
    vc = vg - jnp.mean(vg, axis=-1, keepdims=True)
    vgn = vc * lax.rsqrt(jnp.mean(vc * vc, axis=-1, keepdims=True) + EPS) * ln_g + ln_b
    return _gelu(u_raw) * (_dot(w * mask, vgn) + bcol)


def gmlp_fwd(pm, ln_g, ln_b, w_s, bcol, cat, *, name):
    t = pm.shape[0]
    g = w_s.shape[0]
    gw = g * HEAD_DIM
    assert pm.shape[1] == 6 * gw and cat.shape[1] == 2 * gw

    def body(u_ref, v_ref, lg_ref, lb_ref, w_ref, b_ref, cat_in, o_ref):
        del cat_in
        out = gmlp_block(lg_ref[...], lb_ref[...], w_ref[...], b_ref[...], _heads(u_ref, 0, g), _heads(v_ref, 0, g))
        for i in range(g):
            o_ref[:, _hcols(i)] = out[i].astype(o_ref.dtype)

    full = lambda a: pl.BlockSpec(a.shape, lambda m: (0, 0, 0))
    return pl.pallas_call(
        body, name=name, grid=(t // HEAD_DIM,),
        out_shape=jax.ShapeDtypeStruct(cat.shape, cat.dtype),
        in_specs=[pl.BlockSpec((HEAD_DIM, gw), lambda m: (m, 4)), pl.BlockSpec((HEAD_DIM, gw), lambda m: (m, 5)),
                  full(ln_g), full(ln_b), full(w_s), full(bcol), pl.BlockSpec(memory_space=pl.ANY)],
        out_specs=pl.BlockSpec((HEAD_DIM, gw), lambda m: (m, 1)),
        input_output_aliases={6: 0},
        compiler_params=_cparams(("arbitrary",)),
    )(pm, pm, ln_g, ln_b, w_s, bcol, cat)


def gmlp_bwd(pm, ln_g, ln_b, w_s, bcol, dcat, dpm, *, name):
    t = pm.shape[0]
    g = w_s.shape[0]
    gw = g * HEAD_DIM
    assert pm.shape[1] == 6 * gw and dpm.shape == pm.shape

    def body(u_ref, v_ref, lg_ref, lb_ref, w_ref, b_ref, dc_ref, dpm_in, duv_ref, dlg_ref, dlb_ref, dw_ref, db_ref):
        del dpm_in
        first = pl.program_id(0) == 0
        _, vjp = jax.vjp(gmlp_block, lg_ref[...], lb_ref[...], w_ref[...], b_ref[...], _heads(u_ref, 0, g), _heads(v_ref, 0, g))
        dlg, dlb, dw, db, du, dv = vjp(_heads(dc_ref, 0, g))
        for i in range(g):
            duv_ref[:, _hcols(i)] = du[i].astype(duv_ref.dtype)
            duv_ref[:, _hcols(g + i)] = dv[i].astype(duv_ref.dtype)
        for ref, val in ((dlg_ref, dlg), (dlb_ref, dlb), (dw_ref, dw), (db_ref, db)):
            @pl.when(first)
            def _(ref=ref, val=val):
                ref[...] = val

            @pl.when(jnp.logical_not(first))
            def _(ref=ref, val=val):
                ref[...] += val

    full = lambda a: pl.BlockSpec(a.shape, lambda m: (0, 0, 0))
    return pl.pallas_call(
        body, name=name, grid=(t // HEAD_DIM,),
        out_shape=(jax.ShapeDtypeStruct(dpm.shape, dpm.dtype),
                   jax.ShapeDtypeStruct(ln_g.shape, F32), jax.ShapeDtypeStruct(ln_b.shape, F32),
                   jax.ShapeDtypeStruct(w_s.shape, F32), jax.ShapeDtypeStruct(bcol.shape, F32)),
        in_specs=[pl.BlockSpec((HEAD_DIM, gw), lambda m: (m, 4)), pl.BlockSpec((HEAD_DIM, gw), lambda m: (m, 5)),
                  full(ln_g), full(ln_b), full(w_s), full(bcol), pl.BlockSpec((HEAD_DIM, gw), lambda m: (m, 1)),
                  pl.BlockSpec(memory_space=pl.ANY)],
        out_specs=(pl.BlockSpec((HEAD_DIM, 2 * gw), lambda m: (m, 2)), full(ln_g), full(ln_b), full(w_s), full(bcol)),
        input_output_aliases={7: 0},
        compiler_params=_cparams(("arbitrary",)),
    )(pm, pm, ln_g, ln_b, w_s, bcol, dcat, dpm)


def loss_head(g, x, r, tgt, *, name, rows=ROWS):
    t, d = x.shape

    def body(g_ref, x_ref, r_ref, t_ref, l_ref, dx_ref, dxb_ref, dg_ref):
        y, vjp = jax.vjp(lambda gg, xx: _rms(xx, gg), g_ref[...], x_ref[...] + r_ref[...])
        e = y - t_ref[...]
        part = (0.5 / d) * jnp.sum(jnp.sum(e * e, axis=1, keepdims=True), axis=0, keepdims=True)
        dg, dx = vjp(e * (1.0 / d))
        dx_ref[...] = dx
        dxb_ref[...] = dx.astype(BF16)
        first = pl.program_id(0) == 0

        @pl.when(first)
        def _():
            l_ref[...] = part
            dg_ref[...] = dg

        @pl.when(jnp.logical_not(first))
        def _():
            l_ref[...] += part
            dg_ref[...] += dg

    rs = pl.BlockSpec((rows, d), lambda i: (i, 0))
    gs = pl.BlockSpec((1, d), lambda i: (0, 0))
    return pl.pallas_call(
        body, name=name, grid=(t // rows,),
        out_shape=(jax.ShapeDtypeStruct((1, 1), F32), jax.ShapeDtypeStruct((t, d), F32),
                   jax.ShapeDtypeStruct((t, d), BF16), jax.ShapeDtypeStruct((1, d), F32)),
        in_specs=[gs, rs, rs, rs],
        out_specs=(pl.BlockSpec((1, 1), lambda i: (0, 0)), rs, rs, gs),
        compiler_params=_cparams(("arbitrary",)),
    )(g, x, r, tgt)


def adamw(w, gs, m, v, *, name):
    nl, r, c = w.shape
    assert len(gs) == nl
    if r % SUBLANES == 0:
        tr, tc = _pick(r, (256, 128, 64, 32, 16, 8)), c
    else:
        tr, tc = r, _pick(c, (256, 128))
    k1 = 1.0 - ADAM_B1 ** ADAM_STEP
    k2 = 1.0 - ADAM_B2 ** ADAM_STEP

    def body(*refs):
        w_ref, m_ref, v_ref = refs[0], refs[1], refs[2]
        g_refs = refs[3:3 + nl]
        go_ref, d_ref, mo_ref, vo_ref = refs[3 + nl:]
        gg = g_refs[0][...]
        for li in range(1, nl):
            gg = jnp.where(pl.program_id(0) == li, g_refs[li][...], gg)
        mn = ADAM_B1 * m_ref[...] + (1.0 - ADAM_B1) * gg
        vn = ADAM_B2 * v_ref[...] + (1.0 - ADAM_B2) * (gg * gg)
        go_ref[...] = gg
        d_ref[...] = -ADAM_LR * ((mn / k1) / (jnp.sqrt(vn / k2) + ADAM_EPS) + ADAM_WD * w_ref[...])
        mo_ref[...] = mn
        vo_ref[...] = vn

    if tc == c:
        sp = pl.BlockSpec((None, tr, c), lambda l, i: (l, i, 0))
        gsp = pl.BlockSpec((tr, c), lambda l, i: (i, 0))
    else:
        sp = pl.BlockSpec((None, r, tc), lambda l, i: (l, 0, i))
        gsp = pl.BlockSpec((r, tc), lambda l, i: (0, i))
    sds = jax.ShapeDtypeStruct((nl, r, c), F32)
    return pl.pallas_call(
        body, name=name, grid=(nl, (r // tr) * (c // tc)), out_shape=(sds, sds, sds, sds),
        in_specs=[sp, sp, sp] + [gsp] * nl, out_specs=(sp, sp, sp, sp),
        compiler_params=_cparams(("parallel", "parallel")),
    )(w, m, v, *gs)


MESH = pl.DeviceIdType.MESH
ANY = pl.BlockSpec(memory_space=pl.ANY)


def _me():
    return lax.axis_index("x"), lax.axis_index("y"), lax.axis_index("c")


def _other_chips(x, y):
    return [(1 - x, y), (x, 1 - y), (1 - x, 1 - y)]


def gather_shards(arrs, *, name):
    n = len(arrs)
    per = 7

    def body(*refs):
        ins, outs = refs[:n], refs[n:2 * n]
        send_sems, recv_sems = refs[2 * n:]
        x, y, c = _me()
        chip = 2 * x + y
        sib = (x, y, 1 - c)
        chips = _other_chips(x, y)

        def rc(i, k, src, dst, to):
            return pltpu.make_async_remote_copy(src_ref=src, dst_ref=dst, send_sem=send_sems.at[per * i + k],
                                                recv_sem=recv_sems.at[per * i + k], device_id=to, device_id_type=MESH)

        def half(i, which):
            hrows = arrs[i].shape[0] // 2
            return pl.ds(which * hrows, hrows)

        own = [rc(i, 6, ins[i], outs[i].at[chip], sib) for i in range(n)]
        for cp in own:
            cp.start()
        started = []
        for i in range(n):
            for j, ch in enumerate(chips):
                cp = rc(i, j, ins[i].at[half(i, c)], outs[i].at[chip, half(i, c)], (ch[0], ch[1], c))
                cp.start()
                started.append(cp)
        for i in range(n):
            for j, ch in enumerate(chips):
                sc = 2 * ch[0] + ch[1]
                landed = outs[i].at[sc, half(i, c)]
                rc(i, j, ins[i].at[half(i, c)], landed, (ch[0], ch[1], c)).wait_recv()
                fw = rc(i, 3 + j, landed, landed, sib)
                fw.start()
                started.append(fw)
        for i in range(n):
            for j, ch in enumerate(chips):
                sc = 2 * ch[0] + ch[1]
                theirs = outs[i].at[sc, half(i, 1 - c)]
                rc(i, 3 + j, theirs, theirs, sib).wait_recv()
        for cp in own:
            cp.wait_recv()
        for cp in started + own:
            cp.wait_send()

    return pl.pallas_call(
        body, name=name,
        out_shape=tuple(jax.ShapeDtypeStruct((N_CHIPS,) + a.shape, a.dtype) for a in arrs),
        in_specs=[ANY] * n, out_specs=tuple([ANY] * n),
        scratch_shapes=[pltpu.SemaphoreType.DMA((per * n,)), pltpu.SemaphoreType.DMA((per * n,))],
        compiler_params=pltpu.CompilerParams(has_side_effects=True),
    )(*arrs)


HBM = pl.BlockSpec(memory_space=pltpu.HBM)
SEM = pl.BlockSpec(memory_space=pltpu.SEMAPHORE)
EFFECT = pltpu.SideEffectType.DATAFLOW_SIDE_EFFECTING
GATHER_COPIES = 4


def _hbm(a):
    return pltpu.with_memory_space_constraint(a, pltpu.HBM)


def _gather_copies(srcs, lands, send_sems, recv_sems):
    x, y, c = _me()
    chip = 2 * x + y
    targets = [(ch[0], ch[1], c) for ch in _other_chips(x, y)] + [(x, y, 1 - c)]
    return [pltpu.make_async_remote_copy(src_ref=srcs[i], dst_ref=lands[i].at[chip], send_sem=send_sems.at[GATHER_COPIES * i + k],
                                         recv_sem=recv_sems.at[GATHER_COPIES * i + k], device_id=to, device_id_type=MESH)
            for i in range(len(srcs)) for k, to in enumerate(targets)]


def gather_start(groups, *, name):
    arrs = [a for g in groups for a in g]
    n, ng = len(arrs), len(groups)
    bounds = np.cumsum([0] + [len(g) for g in groups])

    def body(*refs):
        srcs, lands = refs[:n], refs[n:2 * n]
        sems = refs[2 * n:2 * n + 2 * ng]
        token = refs[-1]
        for gi in range(ng):
            lo, hi = bounds[gi], bounds[gi + 1]
            for cp in _gather_copies(srcs[lo:hi], lands[lo:hi], sems[2 * gi], sems[2 * gi + 1]):
                cp.start()
        token[...] = jnp.zeros_like(token)

    sem_shapes = []
    for g in groups:
        sem_shapes += [pltpu.SemaphoreType.DMA((GATHER_COPIES * len(g),))] * 2
    res = pl.pallas_call(
        body, name=name,
        out_shape=tuple(sem_shapes) + tuple(pltpu.HBM(a.shape, a.dtype) for a in arrs)
        + tuple(pltpu.HBM((N_CHIPS,) + a.shape, a.dtype) for a in arrs) + (jax.ShapeDtypeStruct((SUBLANES, LANES), F32),),
        in_specs=[HBM] * (2 * n),
        out_specs=tuple([SEM] * (2 * ng)) + tuple([HBM] * (2 * n)) + (pl.BlockSpec(memory_space=pltpu.VMEM),),
        input_output_aliases={i: 2 * ng + i for i in range(2 * n)},
        compiler_params=pltpu.CompilerParams(has_side_effects=EFFECT),
    )(*[_hbm(a) for a in arrs], *[_hbm(lax.empty((N_CHIPS,) + a.shape, a.dtype)) for a in arrs])
    sems, thru, lands, token = res[:2 * ng], res[2 * ng:2 * ng + n], res[2 * ng + n:2 * ng + 2 * n], res[-1]
    out = [(sems[2 * gi], sems[2 * gi + 1], list(thru[bounds[gi]:bounds[gi + 1]]), list(lands[bounds[gi]:bounds[gi + 1]]))
           for gi in range(ng)]
    return out, token


def gather_wait(group, after, *, name):
    send_sems, recv_sems, thru, lands = group
    n = len(thru)

    def body(*refs):
        srcs, lands_r = refs[:n], refs[n:2 * n]
        ssem, rsem = refs[2 * n], refs[2 * n + 1]
        for cp in _gather_copies(srcs, lands_r, ssem, rsem):
            cp.wait_send()
            cp.wait_recv()

    res = pl.pallas_call(
        body, name=name,
        out_shape=tuple(pltpu.HBM(a.shape, a.dtype) for a in thru) + tuple(pltpu.HBM(a.shape, a.dtype) for a in lands),
        in_specs=[HBM] * (2 * n) + [SEM, SEM, pl.BlockSpec(memory_space=pl.ANY)],
        out_specs=tuple([HBM] * (2 * n)),
        input_output_aliases={i: i for i in range(2 * n)},
        compiler_params=pltpu.CompilerParams(has_side_effects=EFFECT),
    )(*thru, *lands, send_sems, recv_sems, after)
    return list(res[n:])


def sibling_send_other_half(arrs, *, name):
    n = len(arrs)

    def body(*refs):
        ins, outs = refs[:n], refs[n:2 * n]
        send_sems, recv_sems = refs[2 * n:]
        x, y, c = _me()
        cps = []
        for i in range(n):
            hrows = arrs[i].shape[1] // 2
            cps.append(pltpu.make_async_remote_copy(
                src_ref=ins[i].at[:, pl.ds((1 - c) * hrows, hrows), :], dst_ref=outs[i], send_sem=send_sems.at[i],
                recv_sem=recv_sems.at[i], device_id=(x, y, 1 - c), device_id_type=MESH))
        for cp in cps:
            cp.start()
        for cp in cps:
            cp.wait_recv()
        for cp in cps:
            cp.wait_send()

    return pl.pallas_call(
        body, name=name,
        out_shape=tuple(jax.ShapeDtypeStruct((a.shape[0], a.shape[1] // 2, a.shape[2]), a.dtype) for a in arrs),
        in_specs=[ANY] * n, out_specs=tuple([ANY] * n),
        scratch_shapes=[pltpu.SemaphoreType.DMA((n,)), pltpu.SemaphoreType.DMA((n,))],
        compiler_params=pltpu.CompilerParams(has_side_effects=True),
    )(*arrs)


def chip_scatter(arrs, *, name):
    n = len(arrs)

    def body(*refs):
        ins, outs = refs[:n], refs[n:2 * n]
        send_sems, recv_sems = refs[2 * n:]
        x, y, c = _me()
        cps = []
        for i in range(n):
            for j, ch in enumerate(_other_chips(x, y)):
                cps.append(pltpu.make_async_remote_copy(
                    src_ref=ins[i].at[2 * ch[0] + ch[1]], dst_ref=outs[i].at[j], send_sem=send_sems.at[3 * i + j],
                    recv_sem=recv_sems.at[3 * i + j], device_id=(ch[0], ch[1], c), device_id_type=MESH))
        for cp in cps:
            cp.start()
        for cp in cps:
            cp.wait_recv()
        for cp in cps:
            cp.wait_send()

    return pl.pallas_call(
        body, name=name,
        out_shape=tuple(jax.ShapeDtypeStruct((3,) + a.shape[1:], a.dtype) for a in arrs),
        in_specs=[ANY] * n, out_specs=tuple([ANY] * n),
        scratch_shapes=[pltpu.SemaphoreType.DMA((3 * n,)), pltpu.SemaphoreType.DMA((3 * n,))],
        compiler_params=pltpu.CompilerParams(has_side_effects=True),
    )(*arrs)


def sibling_join_halves(arrs, *, name):
    n = len(arrs)

    def body(*refs):
        outs = refs[n:2 * n]
        send_sems, recv_sems = refs[2 * n:]
        x, y, c = _me()
        cps = []
        for i in range(n):
            hrows = arrs[i].shape[0] // 2
            mine = outs[i].at[pl.ds(c * hrows, hrows)]
            cps.append(pltpu.make_async_remote_copy(src_ref=mine, dst_ref=mine, send_sem=send_sems.at[i],
                                                    recv_sem=recv_sems.at[i], device_id=(x, y, 1 - c), device_id_type=MESH))
        for cp in cps:
            cp.start()
        for i in range(n):
            hrows = arrs[i].shape[0] // 2
            theirs = outs[i].at[pl.ds((1 - c) * hrows, hrows)]
            pltpu.make_async_remote_copy(src_ref=theirs, dst_ref=theirs, send_sem=send_sems.at[i], recv_sem=recv_sems.at[i],
                                         device_id=(x, y, 1 - c), device_id_type=MESH).wait_recv()
        for cp in cps:
            cp.wait_send()

    return pl.pallas_call(
        body, name=name,
        out_shape=tuple(jax.ShapeDtypeStruct(a.shape, a.dtype) for a in arrs),
        in_specs=[ANY] * n, out_specs=tuple([ANY] * n),
        input_output_aliases={i: i for i in range(n)},
        scratch_shapes=[pltpu.SemaphoreType.DMA((n,)), pltpu.SemaphoreType.DMA((n,))],
        compiler_params=pltpu.CompilerParams(has_side_effects=True),
    )(*arrs)


def pre_reduce(gd, sib, who, *, name):
    _, r, c = gd.shape
    h = r // 2
    tr = _pick(h, (256, 128, 64, 32, 16, 8))
    nrb = h // tr

    def body(who_ref, gd_ref, sib_ref, pb_ref, own_ref):
        p = gd_ref[...] + sib_ref[...]
        pb_ref[...] = p.astype(BF16)

        @pl.when(pl.program_id(1) == who_ref[1])
        def _():
            own_ref[...] = p

    return pl.pallas_call(
        body, name=name,
        out_shape=(jax.ShapeDtypeStruct((N_CHIPS, h, c), BF16), jax.ShapeDtypeStruct((h, c), F32)),
        grid_spec=pltpu.PrefetchScalarGridSpec(
            num_scalar_prefetch=1, grid=(nrb, N_CHIPS),
            in_specs=[pl.BlockSpec((None, tr, c), lambda i, s, w: (s, w[0] * nrb + i, 0)),
                      pl.BlockSpec((None, tr, c), lambda i, s, w: (s, i, 0))],
            out_specs=(pl.BlockSpec((None, tr, c), lambda i, s, w: (s, i, 0)),
                       pl.BlockSpec((tr, c), lambda i, s, w: (i, 0)))),
        compiler_params=_cparams(("parallel", "arbitrary")),
    )(who, gd, sib)


def final_reduce(own, rcv, who, *, name):
    h, c = own.shape
    tr = _pick(h, (256, 128, 64, 32, 16, 8))
    nrb = h // tr

    def body(who_ref, own_ref, rcv_ref, o_ref):
        del who_ref
        acc = own_ref[...]
        for j in range(3):
            acc = acc + rcv_ref[j].astype(F32)
        o_ref[...] = acc

    return pl.pallas_call(
        body, name=name, out_shape=jax.ShapeDtypeStruct((2 * h, c), F32),
        grid_spec=pltpu.PrefetchScalarGridSpec(
            num_scalar_prefetch=1, grid=(nrb,),
            in_specs=[pl.BlockSpec((tr, c), lambda i, w: (i, 0)), pl.BlockSpec((3, tr, c), lambda i, w: (0, i, 0))],
            out_specs=pl.BlockSpec((tr, c), lambda i, w: (w[0] * nrb + i, 0))),
        compiler_params=_cparams(("parallel",)),
    )(who, own, rcv)


def sum8(a, *, name):
    _, r, c = a.shape
    tr = _pick(r, (256, 128, 64, 32, 16, 8))

    def body(a_ref, o_ref):
        acc = a_ref[0]
        for j in range(1, 8):
            acc = acc + a_ref[j]
        o_ref[...] = acc

    return pl.pallas_call(
        body, name=name, out_shape=jax.ShapeDtypeStruct((r, c), F32), grid=(r // tr,),
        in_specs=[pl.BlockSpec((8, tr, c), lambda i: (0, i, 0))],
        out_specs=pl.BlockSpec((tr, c), lambda i: (i, 0)),
        compiler_params=_cparams(("parallel",)),
    )(a)


def allgather8(a, *, name):
    def body(in_ref, out_ref, send_sems, recv_sems, local_sem):
        x, y, c = _me()
        me = 4 * x + 2 * y + c
        loc = pltpu.make_async_copy(in_ref, out_ref.at[me], local_sem)
        loc.start()
        cps = []
        for k in range(1, 8):
            fx, fy, fc = (k >> 2) & 1, (k >> 1) & 1, k & 1
            to = (x + fx - 2 * x * fx, y + fy - 2 * y * fy, c + fc - 2 * c * fc)
            cps.append(pltpu.make_async_remote_copy(src_ref=in_ref, dst_ref=out_ref.at[me], send_sem=send_sems.at[k - 1],
                                                    recv_sem=recv_sems.at[k - 1], device_id=to, device_id_type=MESH))
        for cp in cps:
            cp.start()
        for k in range(1, 8):
            fx, fy, fc = (k >> 2) & 1, (k >> 1) & 1, k & 1
            src = 4 * (x + fx - 2 * x * fx) + 2 * (y + fy - 2 * y * fy) + (c + fc - 2 * c * fc)
            pltpu.make_async_remote_copy(src_ref=in_ref, dst_ref=out_ref.at[src], send_sem=send_sems.at[k - 1],
                                         recv_sem=recv_sems.at[k - 1], device_id=(x, y, c), device_id_type=MESH).wait_recv()
        for cp in cps:
            cp.wait_send()
        loc.wait()

    return pl.pallas_call(
        body, name=name, out_shape=jax.ShapeDtypeStruct((8,) + a.shape, a.dtype),
        in_specs=[ANY], out_specs=ANY,
        scratch_shapes=[pltpu.SemaphoreType.DMA((7,)), pltpu.SemaphoreType.DMA((7,)), pltpu.SemaphoreType.DMA],
        compiler_params=pltpu.CompilerParams(has_side_effects=True),
    )(a)


def local_step(x, tgt, wb, ws, arrive=lambda group, after: {}):
    t, d = x.shape
    nh = ws["alog"].shape[0]
    ng = ws["w_s"].shape[0]
    assert nh == ng
    mix_w = (nh + ng) * HEAD_DIM

    (h0,) = rows_fwd(stage_norm, [ws["e_norm"]], [x], [(1, BF16, d)], name="f_norm_e")
    pm = mm_nt(h0, wb["w_main_t"], name="f_proj_main")
    pba = mm_nt(h0, wb["w_ba_t"], name="f_proj_ba")
    qkv = conv_fwd(pm, ws["conv_w"], mode="silu", name="f_conv4")
    cat, states, tinvs = delta_fwd(qkv, pm, pba, ws["alog"], ws["dtb"], ws["onorm"], mix_w, name="f_delta")
    cat = gmlp_fwd(pm, ws["lng"], ws["lnb"], ws["w_s"], ws["bs"], cat, name="f_gmlp")
    wb = {**wb, **arrive("layer0", cat)}
    y0 = mm_nn(cat, wb["w_out"], name="f_out")
    x1, h1 = rows_fwd(stage_res_norm, [ws["f_norm0"]], [x, y0], [(0, F32, d), (1, BF16, d)], name="f_norm_f0")
    a1, s1 = mm_nn(h1, wb["w1_0"], name="f_mlp0_up", epilogue="relu2")
    y1 = mm_nn(s1, wb["w2_0"], name="f_mlp0_down")
    x2, h2 = rows_fwd(stage_res_norm, [ws["o_norm"]], [x1, y1], [(0, F32, d), (1, BF16, d)], name="f_norm_o")
    wb = {**wb, **arrive("layer1", h2)}
    zz = mm_nn(h2, wb["pw1"], name="f_pw1")
    zparts = [(zz, d, 0), (zz, d, 1)]
    (gl,) = rows_fwd(stage_glu, [ws["b1a"], ws["b1b"]], zparts, [(0, F32, d)], name="f_glu")
    cv = conv_fwd(gl, ws["dw"], mode="plain", name="f_conv31")
    ln_params = [ws["dw_b"], ws["ln_g"], ws["ln_b"]]
    (sl,) = rows_fwd(stage_ln_silu, ln_params, [cv], [(0, BF16, d)], name="f_ln_silu")
    y2 = mm_nn(sl, wb["pw2"], name="f_pw2")
    x3, h3 = rows_fwd(stage_res_bias_norm, [ws["b2"], ws["f_norm1"]], [x2, y2], [(0, F32, d), (1, BF16, d)], name="f_norm_f1")
    a3, s3 = mm_nn(h3, wb["w1_1"], name="f_mlp1_up", epilogue="relu2")
    y3 = mm_nn(s3, wb["w2_1"], name="f_mlp1_down")
    loss, d4, d4b, g_final = loss_head(ws["final_norm"], x3, y3, tgt, name="loss_head")

    gb, gs = {}, {"final_norm": g_final}
    s_up = wb["w1_0"].shape[0]

    gb["w2_1"] = mm_tn(s3, d4b, groups=1, name="b_mlp1_down_w")
    dpre3 = mm_nt(d4b, wb["w2_1"], name="b_mlp1_down_x", mul=a3, out_dtype=BF16)
    gb["w1_1"] = mm_tn(h3, dpre3, groups=s_up, name="b_mlp1_up_w")
    dh3 = mm_nt(dpre3, wb["w1_1"], name="b_mlp1_up_x")
    d3, d3b, gs["b2"], gs["f_norm1"] = rows_bwd(stage_res_bias_norm, [ws["b2"], ws["f_norm1"]], [x2, y2], [d4, dh3],
                                                [(0, F32), (1, BF16)], name="b_norm_f1")
    gb["pw2"] = mm_tn(sl, d3b, groups=1, name="b_pw2_w")
    dsl = mm_nt(d3b, wb["pw2"], name="b_pw2_x")
    dcv, gs["dw_b"], gs["ln_g"], gs["ln_b"] = rows_bwd(stage_ln_silu, ln_params, [cv], [dsl], [(0, F32)], name="b_ln_silu")
    dgl, gs["dw"] = conv_bwd(gl, ws["dw"], dcv, name="b_conv31")
    dzz, gs["b1a"], gs["b1b"] = rows_bwd(stage_glu, [ws["b1a"], ws["b1b"]], zparts, [dgl], [(0, BF16), (1, BF16)],
                                         name="b_glu", concat=True)
    gb["pw1"] = mm_tn(h2, dzz, groups=wb["pw1"].shape[0], name="b_pw1_w")
    dh2 = mm_nt(dzz, wb["pw1"], name="b_pw1_x")
    d2, d2b, gs["o_norm"] = rows_bwd(stage_res_norm, [ws["o_norm"]], [x1, y1], [d3, dh2], [(0, F32), (1, BF16)], name="b_norm_o")
    gb["w2_0"] = mm_tn(s1, d2b, groups=1, name="b_mlp0_down_w")
    dpre1 = mm_nt(d2b, wb["w2_0"], name="b_mlp0_down_x", mul=a1, out_dtype=BF16)
    gb["w1_0"] = mm_tn(h1, dpre1, groups=s_up, name="b_mlp0_up_w")
    dh1 = mm_nt(dpre1, wb["w1_0"], name="b_mlp0_up_x")
    d1, d1b, gs["f_norm0"] = rows_bwd(stage_res_norm, [ws["f_norm0"]], [x, y0], [d2, dh1], [(0, F32), (1, BF16)], name="b_norm_f0")
    gb["w_out"] = mm_tn(cat, d1b, groups=1, name="b_out_w")
    dcat = mm_nt(d1b, wb["w_out"], name="b_out_x")
    dqkv_c, dpm, dpba, gs["alog"], gs["dtb"], gs["onorm"] = delta_bwd(
        qkv, pm, pba, ws["alog"], ws["dtb"], ws["onorm"], states, tinvs, dcat, name="b_delta")
    dpm, gs["lng"], gs["lnb"], gs["w_s"], gs["bs"] = gmlp_bwd(
        pm, ws["lng"], ws["lnb"], ws["w_s"], ws["bs"], dcat, dpm, name="b_gmlp")
    dconv = conv_fwd(pm, ws["conv_w"], mode="silu_bwd", g=dqkv_c, name="b_conv4_silu")
    dpm, gs["conv_w"] = conv_bwd(pm, ws["conv_w"], dconv, name="b_conv4", into=dpm)
    gb["w_main_t"] = mm_tn(dpm, h0, groups=1, name="b_proj_main_w")
    gb["w_ba_t"] = mm_tn(dpba, h0, groups=1, name="b_proj_ba_w")
    dh0 = mm_nn(dpm, wb["w_main_t"], name="b_proj_main_x")
    dh0 = mm_nn(dpba, wb["w_ba_t"], name="b_proj_ba_x", add=dh0)
    grad_x, gs["e_norm"] = rows_bwd(stage_norm, [ws["e_norm"]], [x], [d1, dh0], [(0, F32)], name="b_norm_e")
    return loss, grad_x, gb, gs


WEIGHTS = ["e_norm", "e_w_in", "e_conv_w", "e_a_log", "e_dt_bias", "e_o_norm", "e_ln_g", "e_ln_b", "e_w_s", "e_b_s", "e_w_out",
           "o_norm", "o_pw1", "o_pw1_b", "o_dw", "o_dw_b", "o_ln_g", "o_ln_b", "o_pw2", "o_pw2_b", "f_norm", "f_w1", "f_w2",
           "final_norm"]
BIG = ["e_w_in", "e_w_out", "o_pw1", "o_pw2", "f_w1", "f_w2"]
SMALL_SHARDED = ["e_conv_w", "o_norm", "o_pw1_b", "o_dw", "o_dw_b", "o_ln_g", "o_ln_b", "o_pw2_b"]
SMALL = [n for n in WEIGHTS if n not in BIG]
LANES = 128
PACK_ROWS = 16
IN_ROW_MULT = 256


def _pack(arrs):
    flat = jnp.concatenate([a.reshape(-1).astype(F32) for a in arrs])
    n = flat.shape[0]
    rows = -(-n // (LANES * PACK_ROWS)) * PACK_ROWS
    return jnp.pad(flat, (0, rows * LANES - n)).reshape(rows, LANES)


def _unpack(flat, shapes, lead=()):
    outs, off = [], 0
    for s in shapes:
        n = int(np.prod(s))
        outs.append(flat[..., off:off + n].reshape(lead + tuple(s)))
        off += n
    return outs


def _chip_major(g, s=N_CHIPS):
    k, n = g.shape
    return g.reshape(k, s, n // s).transpose(1, 0, 2)


def kernel(x, e_norm, e_w_in, e_conv_w, e_a_log, e_dt_bias, e_o_norm, e_ln_g, e_ln_b, e_w_s, e_b_s, e_w_out, o_norm, o_pw1, o_pw1_b, o_dw, o_dw_b, o_ln_g, o_ln_b, o_pw2, o_pw2_b, f_norm, f_w1, f_w2, final_norm, loss_target, m_e_norm, m_e_w_in, m_e_conv_w, m_e_a_log, m_e_dt_bias, m_e_o_norm, m_e_ln_g, m_e_ln_b, m_e_w_s, m_e_b_s, m_e_w_out, m_o_norm, m_o_pw1, m_o_pw1_b, m_o_dw, m_o_dw_b, m_o_ln_g, m_o_ln_b, m_o_pw2, m_o_pw2_b, m_f_norm, m_f_w1, m_f_w2, m_final_norm, v_e_norm, v_e_w_in, v_e_conv_w, v_e_a_log, v_e_dt_bias, v_e_o_norm, v_e_ln_g, v_e_ln_b, v_e_w_s, v_e_b_s, v_e_w_out, v_o_norm, v_o_pw1, v_o_pw1_b, v_o_dw, v_o_dw_b, v_o_ln_g, v_o_ln_b, v_o_pw2, v_o_pw2_b, v_f_norm, v_f_w1, v_f_w2, v_final_norm):
    a = dict(locals())
    xi, yi, ci = _me()
    chip = 2 * xi + yi
    who = jnp.stack([ci, chip]).astype(jnp.int32)
    t, d = x.shape[1], x.shape[2]
    nh, ng = e_a_log.shape[1], e_w_s.shape[1]
    n_qkv, n_av, n_bw = 3 * nh * HEAD_DIM, nh * HEAD_DIM, ng * HEAD_DIM
    in_cols = n_qkv + n_av + 2 * nh + 2 * n_bw
    c_ba = n_qkv + n_av

    sh_in = in_cols // N_CHIPS
    pad_in = -(-sh_in // IN_ROW_MULT) * IN_ROW_MULT - sh_in
    w_in_t_local = jnp.pad(e_w_in[0].T.astype(BF16), ((0, pad_in), (0, 0)))
    small_local = [a[n] for n in SMALL_SHARDED]
    g_in, g_small = gather_shards([w_in_t_local, _pack(small_local)], name="gather_weights_in")
    travelling, token = gather_start([[w.astype(BF16) for w in (e_w_out[0], f_w1[0], f_w2[0])],
                                      [w.astype(BF16) for w in (o_pw1[0], o_pw2[0], f_w1[1], f_w2[1])]], name="gather_start")

    def arrive(group, after):
        if group == "layer0":
            g_out, g_w1, g_w2 = gather_wait(travelling[0], after, name="gather_wait_layer0")
            return {"w_out": g_out.reshape(1, -1, d), "w1_0": g_w1, "w2_0": g_w2.reshape(1, -1, d)}
        g_pw1, g_pw2, g_w1, g_w2 = gather_wait(travelling[1], after, name="gather_wait_layer1")
        return {"pw1": g_pw1, "pw2": g_pw2.reshape(1, -1, d), "w1_1": g_w1, "w2_1": g_w2.reshape(1, -1, d)}

    def in_rows(lo, hi):
        parts = []
        for s in range(N_CHIPS):
            a0, a1 = max(lo, s * sh_in), min(hi, (s + 1) * sh_in)
            if a0 < a1:
                parts.append(g_in[s, a0 - s * sh_in:a1 - s * sh_in])
        return parts

    wb = {
        "w_main_t": jnp.concatenate(in_rows(0, c_ba) + in_rows(c_ba + 2 * nh, in_cols), axis=0)[None],
        "w_ba_t": jnp.pad(jnp.concatenate(in_rows(c_ba, c_ba + 2 * nh), axis=0), ((0, LANES - 2 * nh), (0, 0)))[None],
    }
    pieces = _unpack(g_small.reshape(N_CHIPS, -1), [w.shape for w in small_local], lead=(N_CHIPS,))
    full = {n: jnp.moveaxis(p, 0, -2).reshape(p.shape[1:-1] + (N_CHIPS * p.shape[-1],)) for n, p in zip(SMALL_SHARDED, pieces)}
    ws = {
        "e_norm": e_norm + token[0, 0],
        "conv_w": full["e_conv_w"][0], "alog": e_a_log.reshape(nh, 1, 1), "dtb": e_dt_bias.reshape(nh, 1, 1),
        "onorm": e_o_norm, "lng": e_ln_g.reshape(ng, 1, HEAD_DIM), "lnb": e_ln_b.reshape(ng, 1, HEAD_DIM), "w_s": e_w_s[0],
        "bs": e_b_s.reshape(ng, HEAD_DIM, 1), "o_norm": full["o_norm"], "b1a": full["o_pw1_b"][:, :d], "b1b": full["o_pw1_b"][:, d:],
        "dw": full["o_dw"][0], "dw_b": full["o_dw_b"], "ln_g": full["o_ln_g"], "ln_b": full["o_ln_b"], "b2": full["o_pw2_b"],
        "f_norm0": f_norm[0:1], "f_norm1": f_norm[1:2], "final_norm": final_norm.reshape(1, d),
    }

    loss, grad_x, gb, gs = local_step(x[0], loss_target[0], wb, ws, arrive)

    gm, gba = gb["w_main_t"][0], gb["w_ba_t"][0]
    g_in_t = jnp.concatenate([gm[:c_ba], gba[:2 * nh], gm[c_ba:]], axis=0).reshape(N_CHIPS, sh_in, d)
    by_chip = [jnp.pad(g_in_t, ((0, 0), (0, pad_in), (0, 0))), gb["w_out"].reshape(N_CHIPS, -1, d), gb["pw1"],
               gb["pw2"].reshape(N_CHIPS, -1, d),
               gb["w1_0"], gb["w1_1"], gb["w2_0"].reshape(N_CHIPS, -1, d), gb["w2_1"].reshape(N_CHIPS, -1, d)]
    from_sibling = sibling_send_other_half(by_chip, name="reduce_sibling")
    to_send, own = [], []
    for i, (gd, sb) in enumerate(zip(by_chip, from_sibling)):
        pb, ow = pre_reduce(gd, sb, who, name=f"reduce_pair_{i}")
        to_send.append(pb)
        own.append(ow)
    received = chip_scatter(to_send, name="reduce_chips")
    halves = [final_reduce(ow, rc, who, name=f"reduce_sum_{i}") for i, (ow, rc) in enumerate(zip(own, received))]
    r_in, r_out, r_pw1, r_pw2, r_w1_0, r_w1_1, r_w2_0, r_w2_1 = sibling_join_halves(halves, name="reduce_join")
    big_grads = {"e_w_in": [r_in[:sh_in]], "e_w_out": [r_out], "o_pw1": [r_pw1], "o_pw2": [r_pw2], "f_w1": [r_w1_0, r_w1_1],
                 "f_w2": [r_w2_0, r_w2_1]}

    small_global = {
        "e_norm": gs["e_norm"], "e_conv_w": gs["conv_w"][None], "e_a_log": gs["alog"].reshape(1, nh), "e_dt_bias": gs["dtb"].reshape(1, nh),
        "e_o_norm": gs["onorm"], "e_ln_g": gs["lng"].reshape(1, n_bw), "e_ln_b": gs["lnb"].reshape(1, n_bw), "e_w_s": gs["w_s"][None],
        "e_b_s": gs["bs"].reshape(1, ng, HEAD_DIM), "o_norm": gs["o_norm"], "o_pw1_b": jnp.concatenate([gs["b1a"], gs["b1b"]], axis=1),
        "o_dw": gs["dw"][None], "o_dw_b": gs["dw_b"], "o_ln_g": gs["ln_g"], "o_ln_b": gs["ln_b"], "o_pw2_b": gs["b2"],
        "f_norm": jnp.concatenate([gs["f_norm0"], gs["f_norm1"]], axis=0), "final_norm": gs["final_norm"].reshape(d),
    }
    packed = _pack([small_global[n] for n in SMALL])
    summed = sum8(allgather8(packed, name="reduce_small"), name="reduce_small_sum")
    small_full = _unpack(summed.reshape(-1), [small_global[n].shape for n in SMALL])
    small_grads = {}
    for n, g in zip(SMALL, small_full):
        if n in SMALL_SHARDED:
            width = a[n].shape[-1]
            g = lax.dynamic_slice_in_dim(g, chip * width, width, axis=g.ndim - 1)
        small_grads[n] = g

    out = {}
    for n in BIG:
        w, m, v = a[n], a["m_" + n], a["v_" + n]
        if n == "e_w_in":
            res = adamw(w.transpose(0, 2, 1), big_grads[n], m.transpose(0, 2, 1), v.transpose(0, 2, 1), name="adamw_" + n)
            out[n] = tuple(r.transpose(0, 2, 1) for r in res)
        else:
            out[n] = adamw(w, big_grads[n], m, v, name="adamw_" + n)
    sw, sm, sv, sg = (_pack([src[n] for n in SMALL])[None] for src in
                      ({n: a[n] for n in SMALL}, {n: a["m_" + n] for n in SMALL}, {n: a["v_" + n] for n in SMALL}, small_grads))
    res = adamw(sw, [sg[0]], sm, sv, name="adamw_small")
    shapes = [a[n].shape for n in SMALL]
    unpacked = [_unpack(r.reshape(-1), shapes) for r in res]
    for i, n in enumerate(SMALL):
        out[n] = tuple(u[i] for u in unpacked)

    total = lax.psum(loss[0, 0], ("x", "y", "c"))
    result = [total, grad_x[None]]
    for k in range(4):
        result += [out[n][k] for n in WEIGHTS]
    return tuple(result)
```

```python
import functools
import math

import jax
import jax.numpy as jnp
import numpy as np
from jax import lax
from jax.experimental import pallas as pl
from jax.experimental.pallas import tpu as pltpu

F32 = jnp.float32
BF16 = jnp.bfloat16
HI = lax.Precision.HIGHEST

EPS = 1e-6
CHUNK = 64
PAIR = 2 * CHUNK
HEAD_DIM = 128
A_CONV = 4
C_WIDTH = 31
N_CHIPS = 4
ADAM_LR, ADAM_B1, ADAM_B2, ADAM_EPS, ADAM_WD, ADAM_STEP = 0.001, 0.9, 0.999, 1e-08, 0.01, 10

VMEM_LIMIT = 56 * 1024 * 1024


def _cparams(sem=None):
    return pltpu.CompilerParams(dimension_semantics=sem, vmem_limit_bytes=VMEM_LIMIT)


def _pick(n, prefs):
    for p in prefs:
        if n % p == 0:
            return p
    return n


def mm_nn(a, w, *, name, epilogue=None, add=None, out_dtype=F32):
    m, k = a.shape
    s, _, ns = w.shape
    n = s * ns
    tm = _pick(m, (1024, 512, 256, 128))
    tn = _pick(ns, (1024, 512, 256, 128))
    tk = _pick(k, (2048, 1024, 512, 256, 128))
    nk = k // tk
    npb = ns // tn
    assert add is None or epilogue is None

    def body(a_ref, w_ref, *rest):
        if add is not None:
            add_ref, rest = rest[0], rest[1:]
        if epilogue == "relu2":
            o1_ref, o2_ref = rest[0], rest[1]
            acc_ref = rest[2] if nk > 1 else None
        else:
            o1_ref = rest[0]
            acc_ref = rest[1] if nk > 1 else None
        part = jnp.dot(a_ref[...], w_ref[...], preferred_element_type=F32)

        def finish(c):
            if epilogue == "relu2":
                r = jnp.maximum(c, 0.0)
                o1_ref[...] = r.astype(o1_ref.dtype)
                o2_ref[...] = (r * r).astype(o2_ref.dtype)
            elif add is not None:
                o1_ref[...] = (c + add_ref[...].astype(F32)).astype(o1_ref.dtype)
            else:
                o1_ref[...] = c.astype(o1_ref.dtype)

        if nk == 1:
            finish(part)
        else:
            kk = pl.program_id(2)

            @pl.when(kk == 0)
            def _():
                acc_ref[...] = part

            @pl.when(kk > 0)
            def _():
                acc_ref[...] += part

            @pl.when(kk == nk - 1)
            def _():
                finish(acc_ref[...])

    o_spec = pl.BlockSpec((tm, tn), lambda i, j, kk: (i, j))
    if epilogue == "relu2":
        out_shape = (jax.ShapeDtypeStruct((m, n), BF16), jax.ShapeDtypeStruct((m, n), BF16))
        out_specs = (o_spec, o_spec)
    else:
        out_shape = jax.ShapeDtypeStruct((m, n), out_dtype)
        out_specs = o_spec
    return pl.pallas_call(
        body, name=name, out_shape=out_shape,
        grid=(m // tm, n // tn, nk),
        in_specs=[pl.BlockSpec((tm, tk), lambda i, j, kk: (i, kk)),
                  pl.BlockSpec((None, tk, tn), lambda i, j, kk: (j // npb, kk, j % npb))] + ([o_spec] if add is not None else []),
        out_specs=out_specs,
        scratch_shapes=[pltpu.VMEM((tm, tn), F32)] if nk > 1 else [],
        compiler_params=_cparams(("parallel", "parallel", "arbitrary")),
    )(*([a, w] + ([add] if add is not None else [])))


def mm_nt(a, w, *, name, mul=None, add=None, out_dtype=F32):
    assert mul is None or add is None
    if add is not None:
        mul = add
    m, n = a.shape
    s, k, ns = w.shape
    assert n == s * ns
    tm = _pick(m, (1024, 512, 256, 128))
    tko = _pick(k, (1024, 512, 256, 128))
    tn = _pick(ns, (2048, 1024, 512, 256, 128))
    nn = n // tn
    npb = ns // tn

    def body(a_ref, w_ref, *rest):
        if mul is not None:
            m_ref, o_ref = rest[0], rest[1]
            acc_ref = rest[2] if nn > 1 else None
        else:
            m_ref, o_ref = None, rest[0]
            acc_ref = rest[1] if nn > 1 else None
        part = lax.dot_general(a_ref[...], w_ref[...], (((1,), (1,)), ((), ())), preferred_element_type=F32)

        def finish(c):
            if add is not None:
                c = c + m_ref[...].astype(F32)
            elif m_ref is not None:
                c = c * (2.0 * m_ref[...].astype(F32))
            o_ref[...] = c.astype(o_ref.dtype)

        if nn == 1:
            finish(part)
        else:
            kk = pl.program_id(2)

            @pl.when(kk == 0)
            def _():
                acc_ref[...] = part

            @pl.when(kk > 0)
            def _():
                acc_ref[...] += part

            @pl.when(kk == nn - 1)
            def _():
                finish(acc_ref[...])

    in_specs = [pl.BlockSpec((tm, tn), lambda i, j, kk: (i, kk)),
                pl.BlockSpec((None, tko, tn), lambda i, j, kk: (kk // npb, j, kk % npb))]
    args = [a, w]
    if mul is not None:
        in_specs.append(pl.BlockSpec((tm, tko), lambda i, j, kk: (i, j)))
        args.append(mul)
    return pl.pallas_call(
        body, name=name, out_shape=jax.ShapeDtypeStruct((m, k), out_dtype),
        grid=(m // tm, k // tko, nn),
        in_specs=in_specs,
        out_specs=pl.BlockSpec((tm, tko), lambda i, j, kk: (i, j)),
        scratch_shapes=[pltpu.VMEM((tm, tko), F32)] if nn > 1 else [],
        compiler_params=_cparams(("parallel", "parallel", "arbitrary")),
    )(*args)


def mm_tn(a, b, *, groups, name):
    m, k = a.shape
    _, n = b.shape
    ns = n // groups
    tko = _pick(k, (1024, 512, 256, 128))
    tn = _pick(ns, (1024, 512, 256, 128))
    tc = _pick(m, (2048, 1024, 512, 256, 128))
    nc = m // tc
    npb = ns // tn

    def body(a_ref, b_ref, o_ref):
        part = lax.dot_general(a_ref[...], b_ref[...], (((0,), (0,)), ((), ())), preferred_element_type=F32)
        kk = pl.program_id(2)

        @pl.when(kk == 0)
        def _():
            o_ref[...] = part

        @pl.when(kk > 0)
        def _():
            o_ref[...] += part

    return pl.pallas_call(
        body, name=name, out_shape=jax.ShapeDtypeStruct((groups, k, ns), F32),
        grid=(k // tko, n // tn, nc),
        in_specs=[pl.BlockSpec((tc, tko), lambda i, j, kk: (kk, i)),
                  pl.BlockSpec((tc, tn), lambda i, j, kk: (kk, j))],
        out_specs=pl.BlockSpec((None, tko, tn), lambda i, j, kk: (j // npb, i, j % npb)),
        compiler_params=_cparams(("parallel", "parallel", "arbitrary")),
    )(a, b)


ROWS = 128


def _full_spec(arr):
    nd = arr.ndim
    return pl.BlockSpec(arr.shape, lambda i, _nd=nd: (0,) * _nd)


def _row_spec(x, rows):
    if isinstance(x, tuple):
        _, w, cb = x
        return pl.BlockSpec((rows, w), lambda i, _cb=cb: (i, _cb))
    return pl.BlockSpec((rows, x.shape[1]), lambda i: (i, 0))


def _arr(x):
    return x[0] if isinstance(x, tuple) else x


def _width(x):
    return x[1] if isinstance(x, tuple) else x.shape[1]


def rows_fwd(fn, params, xs, stores, *, name, rows=ROWS):
    t = _arr(xs[0]).shape[0]
    np_, nx = len(params), len(xs)

    def body(*refs):
        p_refs, x_refs, o_refs = refs[:np_], refs[np_:np_ + nx], refs[np_ + nx:]
        outs = fn(*[r[...].astype(F32) for r in p_refs], *[r[...].astype(F32) for r in x_refs])
        for (idx, dt, _), o_ref in zip(stores, o_refs):
            o_ref[...] = outs[idx].astype(dt)

    res = pl.pallas_call(
        body, name=name,
        out_shape=tuple(jax.ShapeDtypeStruct((t, w), dt) for _, dt, w in stores),
        grid=(t // rows,),
        in_specs=[_full_spec(p) for p in params] + [_row_spec(x, rows) for x in xs],
        out_specs=tuple(pl.BlockSpec((rows, w), lambda i: (i, 0)) for _, _, w in stores),
        compiler_params=_cparams(("parallel",)),
    )(*params, *[_arr(x) for x in xs])
    return res


def rows_bwd(fn, params, xs, cts, dx_stores, *, name, rows=ROWS, concat=False):
    t = _arr(xs[0]).shape[0]
    np_, nx = len(params), len(xs)
    ct_idx = [i for i, c in enumerate(cts) if c is not None]
    ct_arrs = [cts[i] for i in ct_idx]
    nct = len(ct_arrs)
    nds = 1 if concat else len(dx_stores)
    widths = [_width(xs[xi]) for xi, _ in dx_stores]

    def body(*refs):
        p_refs = refs[:np_]
        x_refs = refs[np_:np_ + nx]
        c_refs = refs[np_ + nx:np_ + nx + nct]
        d_refs = refs[np_ + nx + nct:np_ + nx + nct + nds]
        g_refs = refs[np_ + nx + nct + nds:]
        pv = [r[...].astype(F32) for r in p_refs]
        xv = [r[...].astype(F32) for r in x_refs]
        outs, vjp = jax.vjp(lambda *a: tuple(fn(*a)), *pv, *xv)
        ct_full = [jnp.zeros_like(o) for o in outs]
        for i, r in zip(ct_idx, c_refs):
            ct_full[i] = r[...].astype(F32)
        grads = vjp(tuple(ct_full))
        if concat:
            off = 0
            for (xi, dt), wd in zip(dx_stores, widths):
                d_refs[0][:, off:off + wd] = grads[np_ + xi].astype(dt)
                off += wd
        else:
            for (xi, dt), d_ref in zip(dx_stores, d_refs):
                d_ref[...] = grads[np_ + xi].astype(dt)
        step = pl.program_id(0)
        for j, g_ref in enumerate(g_refs):
            @pl.when(step == 0)
            def _(g_ref=g_ref, j=j):
                g_ref[...] = grads[j]

            @pl.when(step > 0)
            def _(g_ref=g_ref, j=j):
                g_ref[...] += grads[j]

    dx_shapes = [(sum(widths), dx_stores[0][1])] if concat else [(wd, dt) for wd, (_, dt) in zip(widths, dx_stores)]
    out_shape = tuple(jax.ShapeDtypeStruct((t, wd), dt) for wd, dt in dx_shapes) + \
        tuple(jax.ShapeDtypeStruct(p.shape, F32) for p in params)
    out_specs = tuple(pl.BlockSpec((rows, wd), lambda i: (i, 0)) for wd, _ in dx_shapes) + \
        tuple(_full_spec(p) for p in params)
    return pl.pallas_call(
        body, name=name, out_shape=out_shape, grid=(t // rows,),
        in_specs=[_full_spec(p) for p in params] + [_row_spec(x, rows) for x in xs] + [_row_spec(c, rows) for c in ct_arrs],
        out_specs=out_specs,
        compiler_params=_cparams(("arbitrary",)),
    )(*params, *[_arr(x) for x in xs], *[_arr(c) for c in ct_arrs])


def _rms(x, g):
    return x * lax.rsqrt(jnp.mean(x * x, axis=-1, keepdims=True) + EPS) * g


def _sigmoid(x):
    return 1.0 / (1.0 + jnp.exp(-x))


def _silu(x):
    return x * _sigmoid(x)


def _gelu(x):
    return 0.5 * x * (1.0 + lax.erf(x * (1.0 / math.sqrt(2.0))))


def stage_norm(g, x):
    return x, _rms(x, g)


def stage_res_norm(g, x, y):
    xn = x + y
    return xn, _rms(xn, g)


def stage_res_bias_norm(b, g, x, y):
    xn = x + y + b
    return xn, _rms(xn, g)


def stage_glu(ba, bb, za, zb):
    return ((za + ba) * _sigmoid(zb + bb),)


def stage_ln_silu(dw_b, ln_g, ln_b, cv):
    z = cv + dw_b
    mu = jnp.mean(z, axis=-1, keepdims=True)
    zc = z - mu
    y = zc * lax.rsqrt(jnp.mean(zc * zc, axis=-1, keepdims=True) + EPS) * ln_g + ln_b
    return (_silu(y),)


CONV_ROWS = 128
CONV_COLS = 256
SUBLANES = 8


def _halo(k):
    return SUBLANES * ((k - 1 + SUBLANES - 1) // SUBLANES)


def _taps_by_roll(k):
    out = {}
    for s in range(k):
        out.setdefault(s % SUBLANES, []).append((s // SUBLANES, s))
    return out


def _shifted_down(win, k, r):
    halo, n = _halo(k), win.shape[0]
    segs = {}
    for b, lst in _taps_by_roll(k).items():
        rolled = win if b == 0 else pltpu.roll(win, b, axis=0)
        for a, s in lst:
            segs[s] = rolled[halo - SUBLANES * a: halo - SUBLANES * a + r]
    return segs


def _shifted_up(win, k, r):
    n = win.shape[0]
    segs = {}
    for b, lst in _taps_by_roll(k).items():
        rolled = win if b == 0 else pltpu.roll(win, n - b, axis=0)
        for a, s in lst:
            segs[s] = rolled[SUBLANES * a: SUBLANES * a + r]
    return segs


def _for_blocks(nblk, fn):
    fn(0, True, nblk == 1)
    if nblk > 2:
        def step(i, c):
            fn(i, False, False)
            return c
        lax.fori_loop(1, nblk - 1, step, 0)
    if nblk > 1:
        fn(nblk - 1, False, True)


def _base(i, r):
    return i * r if isinstance(i, int) else pl.multiple_of(i * r, r)


def _win_top(ref, i, first, r, halo):
    if first:
        return jnp.concatenate([jnp.zeros((halo, ref.shape[1]), F32), ref[pl.ds(0, r), :]], axis=0)
    base = _base(i, r)
    return ref[pl.ds(base - halo, r + halo), :]


def _win_bottom(ref, i, last, r, halo):
    base = _base(i, r)
    if last:
        return jnp.concatenate([ref[pl.ds(base, r), :], jnp.zeros((halo, ref.shape[1]), F32)], axis=0)
    return ref[pl.ds(base, r + halo), :]


def conv_fwd(x, w, *, mode, name, g=None):
    t = x.shape[0]
    k, c = w.shape
    r, cb, halo = min(CONV_ROWS, t), min(CONV_COLS, c), _halo(k)
    nblk = t // r

    def body(*refs):
        if mode == "silu_bwd":
            x_ref, w_ref, g_ref, o_ref = refs
        else:
            x_ref, w_ref, o_ref = refs

        def blk(i, first, last):
            segs = _shifted_down(_win_top(x_ref, i, first, r, halo), k, r)
            acc = None
            for s in range(k):
                term = w_ref[pl.ds(k - 1 - s, 1), :] * segs[s]
                acc = term if acc is None else acc + term
            base = _base(i, r)
            if mode == "silu":
                acc = _silu(acc)
            elif mode == "silu_bwd":
                sg = _sigmoid(acc)
                acc = g_ref[pl.ds(base, r), :] * (sg * (1.0 + acc * (1.0 - sg)))
            o_ref[pl.ds(base, r), :] = acc

        _for_blocks(nblk, blk)

    col = pl.BlockSpec((t, cb), lambda j: (0, j))
    in_specs = [col, pl.BlockSpec((k, cb), lambda j: (0, j))] + ([col] if mode == "silu_bwd" else [])
    args = [x, w] + ([g] if mode == "silu_bwd" else [])
    return pl.pallas_call(
        body, name=name, out_shape=jax.ShapeDtypeStruct((t, c), F32), grid=(c // cb,),
        in_specs=in_specs, out_specs=col, compiler_params=_cparams(("parallel",)),
    )(*args)


def conv_bwd(x, w, dy, *, name, into=None):
    t = x.shape[0]
    k, c = w.shape
    r, cb, halo = min(CONV_ROWS, t), min(CONV_COLS, c), _halo(k)
    nblk = t // r

    def body(x_ref, w_ref, dy_ref, *rest):
        dx_ref, dw_ref = rest[-2], rest[-1]
        dw_ref[...] = jnp.zeros_like(dw_ref)

        def blk(i, first, last):
            base = _base(i, r)
            up = _shifted_up(_win_bottom(dy_ref, i, last, r, halo), k, r)
            down = _shifted_down(_win_top(x_ref, i, first, r, halo), k, r)
            dyb = up[0]
            acc = None
            for s in range(k):
                term = w_ref[pl.ds(k - 1 - s, 1), :] * up[s]
                acc = term if acc is None else acc + term
                dw_ref[pl.ds(k - 1 - s, 1), :] += jnp.sum(down[s] * dyb, axis=0, keepdims=True)
            dx_ref[pl.ds(base, r), :] = acc.astype(dx_ref.dtype)

        _for_blocks(nblk, blk)

    col = pl.BlockSpec((t, cb), lambda j: (0, j))
    wsp = pl.BlockSpec((k, cb), lambda j: (0, j))
    dx_shape = jax.ShapeDtypeStruct((t, c), F32) if into is None else jax.ShapeDtypeStruct(into.shape, into.dtype)
    return pl.pallas_call(
        body, name=name,
        out_shape=(dx_shape, jax.ShapeDtypeStruct((k, c), F32)), grid=(c // cb,),
        in_specs=[col, wsp, col] + ([] if into is None else [pl.BlockSpec(memory_space=pl.ANY)]),
        out_specs=(col, wsp),
        input_output_aliases={} if into is None else {3: 0},
        compiler_params=_cparams(("parallel",)),
    )(*([x, w, dy] + ([] if into is None else [into])))


_DIMS = {"nn": (((1,), (0,)), ((), ())), "nt": (((1,), (1,)), ((), ())), "tn": (((0,), (0,)), ((), ()))}
_DIMS_BATCHED = {"nn": (((2,), (1,)), ((0,), (0,))), "nt": (((2,), (2,)), ((0,), (0,))), "tn": (((1,), (1,)), ((0,), (0,)))}


def _mxu(a, b, mode):
    dims = _DIMS_BATCHED if a.ndim == 3 else _DIMS
    return lax.dot_general(a, b, dims[mode], preferred_element_type=F32)


def _split(x):
    hi = x.astype(BF16)
    return hi, (x - hi.astype(F32)).astype(BF16)


def _dot_raw(a, b, mode, prec):
    if prec == "bf16":
        return _mxu(a.astype(BF16), b.astype(BF16), mode)
    if prec == "x3":
        ah, al = _split(a)
        bh, bl = _split(b)
        return _mxu(ah, bh, mode) + (_mxu(ah, bl, mode) + _mxu(al, bh, mode))
    if prec == "x3r":
        bh, bm = _split(b)
        bl = (b - bh.astype(F32) - bm.astype(F32)).astype(BF16)
        ah = a.astype(BF16)
        return _mxu(ah, bh, mode) + (_mxu(ah, bm, mode) + _mxu(ah, bl, mode))
    raise ValueError(prec)


@functools.lru_cache(maxsize=None)
def _dot_fn(mode, prec):
    bprec = "x3" if prec == "x3r" else prec

    @jax.custom_vjp
    def f(a, b):
        return _dot_raw(a, b, mode, prec)

    def fwd(a, b):
        return _dot_raw(a, b, mode, prec), (a, b)

    def bwd(res, ct):
        a, b = res
        if mode == "nn":
            return _dot_raw(ct, b, "nt", bprec), _dot_raw(a, ct, "tn", bprec)
        if mode == "nt":
            return _dot_raw(ct, b, "nn", bprec), _dot_raw(ct, a, "tn", bprec)
        return _dot_raw(b, ct, "nt", bprec), _dot_raw(a, ct, "nn", bprec)

    f.defvjp(fwd, bwd)
    return f


def _dot(a, b, mode="nn", prec="bf16"):
    return _dot_fn(mode, prec)(a, b)


def _inv_product(l):
    n = l.shape[-1]
    eye = (lax.broadcasted_iota(jnp.int32, (n, n), 0) == lax.broadcasted_iota(jnp.int32, (n, n), 1)).astype(F32)
    p = eye - l
    pw = l
    for _ in range(5):
        pw = _dot_raw(pw, pw, "nn", "x3")
        p = _dot_raw(p, eye + pw, "nn", "x3")
    return p


@jax.custom_vjp
def _inv_unit_lower(l, t_saved):
    return t_saved


def _inv_fwd(l, t_saved):
    return t_saved, t_saved


def _inv_bwd(t, ct):
    tmp = _dot_raw(t, ct, "tn", "x3")
    return -_dot_raw(tmp, t, "nt", "x3"), jnp.zeros_like(t)


_inv_unit_lower.defvjp(_inv_fwd, _inv_bwd)


def _softplus(x):
    pos = x > 0
    return jnp.where(pos, x, 0.0) + jnp.log(1.0 + jnp.exp(jnp.where(pos, -x, x)))


def _l2n(x):
    return x * lax.rsqrt(jnp.sum(x * x, axis=-1, keepdims=True) + EPS)


def delta_pair(s0, qc, kc, vc, z, pba, alog, dtb, onorm, t_saved=None):
    n = PAIR
    nh = qc.shape[0]
    assert qc.shape == (nh, n, HEAD_DIM) and n == HEAD_DIM
    hi = lax.broadcasted_iota(jnp.int32, (nh, 1, pba.shape[1]), 0)
    li = lax.broadcasted_iota(jnp.int32, (nh, 1, pba.shape[1]), 2)
    braw = jnp.sum(pba[None] * (li == hi).astype(F32), axis=2, keepdims=True)
    araw = jnp.sum(pba[None] * (li == hi + nh).astype(F32), axis=2, keepdims=True)
    ri = lax.broadcasted_iota(jnp.int32, (n, n), 0)
    ci = lax.broadcasted_iota(jnp.int32, (n, n), 1)
    same = (ri // CHUNK) == (ci // CHUNK)
    tri = same & (ci <= ri)
    same_f = jnp.broadcast_to(same.astype(F32), (nh, n, n))
    tri_f = jnp.broadcast_to(tri.astype(F32), (nh, n, n))
    strict_f = (same & (ci < ri)).astype(F32)
    m0 = (lax.broadcasted_iota(jnp.int32, (n, 1), 0) < CHUNK).astype(F32)
    m1 = 1.0 - m0

    q = _l2n(qc) * (HEAD_DIM ** -0.5)
    k = _l2n(kc)
    beta = _sigmoid(braw)
    g = -jnp.exp(alog) * _softplus(araw + dtb)
    gb = jnp.broadcast_to(g, (nh, n, n))
    gc = _dot(tri_f, gb, "nn", "x3r")
    gtot = _dot(same_f, gb, "nn", "x3r")
    decay = jnp.exp(jnp.where(tri, gc - jnp.swapaxes(gc, 1, 2), -1e30))
    eg = jnp.exp(gc)
    kb, vb = k * beta, vc * beta
    l = _dot(kb, k, "nt") * decay * strict_f
    if t_saved is None:
        tinv = _inv_product(l)
    else:
        tinv = _inv_unit_lower(l, t_saved)
    u = _dot(tinv, vb, "nn", "x3")
    w = _dot(tinv, kb * eg, "nn", "x3")
    attn = _dot(q, k, "nt") * decay
    q_dec = q * eg
    k_tail = k * jnp.exp(gtot - gc)
    gl0 = jnp.exp(jnp.sum(gb * m0, axis=1, keepdims=True))
    gl1 = jnp.exp(jnp.sum(gb * m1, axis=1, keepdims=True))

    vn0 = m0 * (u - _dot(w, s0))
    s1 = s0 * gl0 + _dot(k_tail, vn0, "tn")
    vn1 = m1 * (u - _dot(w, s1))
    o = m0 * _dot(q_dec, s0) + m1 * _dot(q_dec, s1) + _dot(attn, vn0 + vn1)
    s2 = s1 * gl1 + _dot(k_tail, vn1, "tn")

    on = o * lax.rsqrt(jnp.mean(o * o, axis=-1, keepdims=True) + EPS) * onorm
    return on * _silu(z), s2, tinv


def _hcols(i):
    return slice(i * HEAD_DIM, (i + 1) * HEAD_DIM)


def _heads(ref, first, count):
    return jnp.stack([ref[:, _hcols(first + i)] for i in range(count)])


def delta_fwd(qkv, pm, pba, alog, dtb, onorm, cat_width, *, name):
    t = qkv.shape[0]
    h = alog.shape[0]
    hd = h * HEAD_DIM
    npair = t // PAIR
    mat = pl.BlockSpec((h, None, PAIR, HEAD_DIM), lambda p: (0, p, 0, 0))
    par = pl.BlockSpec((h, 1, 1), lambda p: (0, 0, 0))

    def body(qkv_ref, z_ref, pba_ref, al_ref, dt_ref, on_ref, o_ref, st_ref, ti_ref, s_scr):
        @pl.when(pl.program_id(0) == 0)
        def _():
            s_scr[...] = jnp.zeros_like(s_scr)

        s0 = s_scr[...]
        st_ref[...] = s0
        out, s2, tinv = delta_pair(s0, _heads(qkv_ref, 0, h), _heads(qkv_ref, h, h), _heads(qkv_ref, 2 * h, h), _heads(z_ref, 0, h),
                                   pba_ref[...], al_ref[...], dt_ref[...], on_ref[...])
        for i in range(h):
            o_ref[:, _hcols(i)] = out[i].astype(o_ref.dtype)
        ti_ref[...] = tinv
        s_scr[...] = s2

    return pl.pallas_call(
        body, name=name, grid=(npair,),
        out_shape=(jax.ShapeDtypeStruct((t, cat_width), BF16),
                   jax.ShapeDtypeStruct((h, npair, PAIR, HEAD_DIM), F32),
                   jax.ShapeDtypeStruct((h, npair, PAIR, PAIR), F32)),
        in_specs=[pl.BlockSpec((PAIR, 3 * hd), lambda p: (p, 0)), pl.BlockSpec((PAIR, hd), lambda p: (p, 3)),
                  pl.BlockSpec((PAIR, pba.shape[1]), lambda p: (p, 0)), par, par,
                  pl.BlockSpec((1, HEAD_DIM), lambda p: (0, 0))],
        out_specs=(pl.BlockSpec((PAIR, hd), lambda p: (p, 0)), mat, mat),
        scratch_shapes=[pltpu.VMEM((h, HEAD_DIM, HEAD_DIM), F32)],
        compiler_params=_cparams(("arbitrary",)),
    )(qkv, pm, pba, alog, dtb, onorm)


def delta_bwd(qkv, pm, pba, alog, dtb, onorm, states, tinvs, dcat, *, name):
    t = qkv.shape[0]
    h = alog.shape[0]
    hd = h * HEAD_DIM
    npair = t // PAIR
    rev = lambda p: npair - 1 - p
    mat = pl.BlockSpec((h, None, PAIR, HEAD_DIM), lambda p: (0, rev(p), 0, 0))
    par = pl.BlockSpec((h, 1, 1), lambda p: (0, 0, 0))
    onsp = pl.BlockSpec((1, HEAD_DIM), lambda p: (0, 0))
    wide = pl.BlockSpec((PAIR, 3 * hd), lambda p: (rev(p), 0))
    zsp = pl.BlockSpec((PAIR, hd), lambda p: (rev(p), 3))
    bsp = pl.BlockSpec((PAIR, pba.shape[1]), lambda p: (rev(p), 0))

    def body(qkv_ref, z_ref, pba_ref, al_ref, dt_ref, on_ref, st_ref, ti_ref, dc_ref,
             dqkv_ref, dz_ref, dpba_ref, dal_ref, ddt_ref, don_ref, ds_scr):
        @pl.when(pl.program_id(0) == 0)
        def _():
            ds_scr[...] = jnp.zeros_like(ds_scr)
            dal_ref[...] = jnp.zeros_like(dal_ref)
            ddt_ref[...] = jnp.zeros_like(ddt_ref)
            don_ref[...] = jnp.zeros_like(don_ref)

        tsv = ti_ref[...]
        fn = lambda s0, qc, kc, vc, z, pb, al, dt, on: delta_pair(s0, qc, kc, vc, z, pb, al, dt, on, tsv)[:2]
        _, vjp = jax.vjp(fn, st_ref[...], _heads(qkv_ref, 0, h), _heads(qkv_ref, h, h), _heads(qkv_ref, 2 * h, h),
                         _heads(z_ref, 0, h), pba_ref[...], al_ref[...], dt_ref[...], on_ref[...])
        ds0, dq, dk, dv, dz, dpba, dal, ddt, don = vjp((_heads(dc_ref, 0, h), ds_scr[...]))
        ds_scr[...] = ds0
        for i in range(h):
            dqkv_ref[:, _hcols(i)] = dq[i]
            dqkv_ref[:, _hcols(h + i)] = dk[i]
            dqkv_ref[:, _hcols(2 * h + i)] = dv[i]
            dz_ref[:, _hcols(i)] = dz[i].astype(dz_ref.dtype)
        dal_ref[...] += dal
        ddt_ref[...] += ddt
        dpba_ref[...] = dpba.astype(dpba_ref.dtype)
        don_ref[...] += don

    return pl.pallas_call(
        body, name=name, grid=(npair,),
        out_shape=(jax.ShapeDtypeStruct((t, 3 * hd), F32), jax.ShapeDtypeStruct(pm.shape, BF16),
                   jax.ShapeDtypeStruct(pba.shape, BF16),
                   jax.ShapeDtypeStruct((h, 1, 1), F32), jax.ShapeDtypeStruct((h, 1, 1), F32),
                   jax.ShapeDtypeStruct((1, HEAD_DIM), F32)),
        in_specs=[wide, zsp, bsp, par, par, onsp, mat, mat, pl.BlockSpec((PAIR, hd), lambda p: (rev(p), 0))],
        out_specs=(wide, zsp, bsp, par, par, onsp),
        scratch_shapes=[pltpu.VMEM((h, HEAD_DIM, HEAD_DIM), F32)],
        compiler_params=_cparams(("arbitrary",)),
    )(qkv, pm, pba, alog, dtb, onorm, states, tinvs, dcat)


def gmlp_block(ln_g, ln_b, w, bcol, u_raw, v_raw):
    n = w.shape[-1]
    ri = lax.broadcasted_iota(jnp.int32, (n, n), 0)
    ci = lax.broadcasted_iota(jnp.int32, (n, n), 1)
    mask = ((ci // CHUNK) <= (ri // CHUNK)).astype(F32)
    vg = _gelu(v_raw)
    vc = vg - jnp.mean(vg, axis=-1, keepdims=True)
    vgn = vc * lax.rsqrt(jnp.mean(vc * vc, axis=-1, keepdims=True) + EPS) * ln_g + ln_b
    return _gelu(u_raw) * (_dot(w * mask, vgn) + bcol)


def gmlp_fwd(pm, ln_g, ln_b, w_s, bcol, cat, *, name):
    t = pm.shape[0]
    g = w_s.shape[0]
    gw = g * HEAD_DIM
    assert pm.shape[1] == 6 * gw and cat.shape[1] == 2 * gw

    def body(u_ref, v_ref, lg_ref, lb_ref, w_ref, b_ref, cat_in, o_ref):
        del cat_in
        out = gmlp_block(lg_ref[...], lb_ref[...], w_ref[...], b_ref[...], _heads(u_ref, 0, g), _heads(v_ref, 0, g))
        for i in range(g):
            o_ref[:, _hcols(i)] = out[i].astype(o_ref.dtype)

    full = lambda a: pl.BlockSpec(a.shape, lambda m: (0, 0, 0))
    return pl.pallas_call(
        body, name=name, grid=(t // HEAD_DIM,),
        out_shape=jax.ShapeDtypeStruct(cat.shape, cat.dtype),
        in_specs=[pl.BlockSpec((HEAD_DIM, gw), lambda m: (m, 4)), pl.BlockSpec((HEAD_DIM, gw), lambda m: (m, 5)),
                  full(ln_g), full(ln_b), full(w_s), full(bcol), pl.BlockSpec(memory_space=pl.ANY)],
        out_specs=pl.BlockSpec((HEAD_DIM, gw), lambda m: (m, 1)),
        input_output_aliases={6: 0},
        compiler_params=_cparams(("arbitrary",)),
    )(pm, pm, ln_g, ln_b, w_s, bcol, cat)


def gmlp_bwd(pm, ln_g, ln_b, w_s, bcol, dcat, dpm, *, name):
    t = pm.shape[0]
    g = w_s.shape[0]
    gw = g * HEAD_DIM
    assert pm.shape[1] == 6 * gw and dpm.shape == pm.shape

    def body(u_ref, v_ref, lg_ref, lb_ref, w_ref, b_ref, dc_ref, dpm_in, duv_ref, dlg_ref, dlb_ref, dw_ref, db_ref):
        del dpm_in
        first = pl.program_id(0) == 0
        _, vjp = jax.vjp(gmlp_block, lg_ref[...], lb_ref[...], w_ref[...], b_ref[...], _heads(u_ref, 0, g), _heads(v_ref, 0, g))
        dlg, dlb, dw, db, du, dv = vjp(_heads(dc_ref, 0, g))
        for i in range(g):
            duv_ref[:, _hcols(i)] = du[i].astype(duv_ref.dtype)
            duv_ref[:, _hcols(g + i)] = dv[i].astype(duv_ref.dtype)
        for ref, val in ((dlg_ref, dlg), (dlb_ref, dlb), (dw_ref, dw), (db_ref, db)):
            @pl.when(first)
            def _(ref=ref, val=val):
                ref[...] = val

            @pl.when(jnp.logical_not(first))
            def _(ref=ref, val=val):
                ref[...] += val

    full = lambda a: pl.BlockSpec(a.shape, lambda m: (0, 0, 0))
    return pl.pallas_call(
        body, name=name, grid=(t // HEAD_DIM,),
        out_shape=(jax.ShapeDtypeStruct(dpm.shape, dpm.dtype),
                   jax.ShapeDtypeStruct(ln_g.shape, F32), jax.ShapeDtypeStruct(ln_b.shape, F32),
                   jax.ShapeDtypeStruct(w_s.shape, F32), jax.ShapeDtypeStruct(bcol.shape, F32)),
        in_specs=[pl.BlockSpec((HEAD_DIM, gw), lambda m: (m, 4)), pl.BlockSpec((HEAD_DIM, gw), lambda m: (m, 5)),
                  full(ln_g), full(ln_b), full(w_s), full(bcol), pl.BlockSpec((HEAD_DIM, gw), lambda m: (m, 1)),
                  pl.BlockSpec(memory_space=pl.ANY)],
        out_specs=(pl.BlockSpec((HEAD_DIM, 2 * gw), lambda m: (m, 2)), full(ln_g), full(ln_b), full(w_s), full(bcol)),
        input_output_aliases={7: 0},
        compiler_params=_cparams(("arbitrary",)),
    )(pm, pm, ln_g, ln_b, w_s, bcol, dcat, dpm)


def loss_head(g, x, r, tgt, *, name, rows=ROWS):
    t, d = x.shape

    def body(g_ref, x_ref, r_ref, t_ref, l_ref, dx_ref, dxb_ref, dg_ref):
        y, vjp = jax.vjp(lambda gg, xx: _rms(xx, gg), g_ref[...], x_ref[...] + r_ref[...])
        e = y - t_ref[...]
        part = (0.5 / d) * jnp.sum(jnp.sum(e * e, axis=1, keepdims=True), axis=0, keepdims=True)
        dg, dx = vjp(e * (1.0 / d))
        dx_ref[...] = dx
        dxb_ref[...] = dx.astype(BF16)
        first = pl.program_id(0) == 0

        @pl.when(first)
        def _():
            l_ref[...] = part
            dg_ref[...] = dg

        @pl.when(jnp.logical_not(first))
        def _():
            l_ref[...] += part
            dg_ref[...] += dg

    rs = pl.BlockSpec((rows, d), lambda i: (i, 0))
    gs = pl.BlockSpec((1, d), lambda i: (0, 0))
    return pl.pallas_call(
        body, name=name, grid=(t // rows,),
        out_shape=(jax.ShapeDtypeStruct((1, 1), F32), jax.ShapeDtypeStruct((t, d), F32),
                   jax.ShapeDtypeStruct((t, d), BF16), jax.ShapeDtypeStruct((1, d), F32)),
        in_specs=[gs, rs, rs, rs],
        out_specs=(pl.BlockSpec((1, 1), lambda i: (0, 0)), rs, rs, gs),
        compiler_params=_cparams(("arbitrary",)),
    )(g, x, r, tgt)


def adamw(w, gs, m, v, *, name):
    nl, r, c = w.shape
    assert len(gs) == nl
    if r % SUBLANES == 0:
        tr, tc = _pick(r, (256, 128, 64, 32, 16, 8)), c
    else:
        tr, tc = r, _pick(c, (256, 128))
    k1 = 1.0 - ADAM_B1 ** ADAM_STEP
    k2 = 1.0 - ADAM_B2 ** ADAM_STEP

    def body(*refs):
        w_ref, m_ref, v_ref = refs[0], refs[1], refs[2]
        g_refs = refs[3:3 + nl]
        go_ref, d_ref, mo_ref, vo_ref = refs[3 + nl:]
        gg = g_refs[0][...]
        for li in range(1, nl):
            gg = jnp.where(pl.program_id(0) == li, g_refs[li][...], gg)
        mn = ADAM_B1 * m_ref[...] + (1.0 - ADAM_B1) * gg
        vn = ADAM_B2 * v_ref[...] + (1.0 - ADAM_B2) * (gg * gg)
        go_ref[...] = gg
        d_ref[...] = -ADAM_LR * ((mn / k1) / (jnp.sqrt(vn / k2) + ADAM_EPS) + ADAM_WD * w_ref[...])
        mo_ref[...] = mn
        vo_ref[...] = vn

    if tc == c:
        sp = pl.BlockSpec((None, tr, c), lambda l, i: (l, i, 0))
        gsp = pl.BlockSpec((tr, c), lambda l, i: (i, 0))
    else:
        sp = pl.BlockSpec((None, r, tc), lambda l, i: (l, 0, i))
        gsp = pl.BlockSpec((r, tc), lambda l, i: (0, i))
    sds = jax.ShapeDtypeStruct((nl, r, c), F32)
    return pl.pallas_call(
        body, name=name, grid=(nl, (r // tr) * (c // tc)), out_shape=(sds, sds, sds, sds),
        in_specs=[sp, sp, sp] + [gsp] * nl, out_specs=(sp, sp, sp, sp),
        compiler_params=_cparams(("parallel", "parallel")),
    )(w, m, v, *gs)


MESH = pl.DeviceIdType.MESH
ANY = pl.BlockSpec(memory_space=pl.ANY)


def _me():
    return lax.axis_index("x"), lax.axis_index("y"), lax.axis_index("c")


def _other_chips(x, y):
    return [(1 - x, y), (x, 1 - y), (1 - x, 1 - y)]


def gather_shards(arrs, *, name):
    n = len(arrs)
    per = 7

    def body(*refs):
        ins, outs = refs[:n], refs[n:2 * n]
        send_sems, recv_sems = refs[2 * n:]
        x, y, c = _me()
        chip = 2 * x + y
        sib = (x, y, 1 - c)
        chips = _other_chips(x, y)

        def rc(i, k, src, dst, to):
            return pltpu.make_async_remote_copy(src_ref=src, dst_ref=dst, send_sem=send_sems.at[per * i + k],
                                                recv_sem=recv_sems.at[per * i + k], device_id=to, device_id_type=MESH)

        def half(i, which):
            hrows = arrs[i].shape[0] // 2
            return pl.ds(which * hrows, hrows)

        own = [rc(i, 6, ins[i], outs[i].at[chip], sib) for i in range(n)]
        for cp in own:
            cp.start()
        started = []
        for i in range(n):
            for j, ch in enumerate(chips):
                cp = rc(i, j, ins[i].at[half(i, c)], outs[i].at[chip, half(i, c)], (ch[0], ch[1], c))
                cp.start()
                started.append(cp)
        for i in range(n):
            for j, ch in enumerate(chips):
                sc = 2 * ch[0] + ch[1]
                landed = outs[i].at[sc, half(i, c)]
                rc(i, j, ins[i].at[half(i, c)], landed, (ch[0], ch[1], c)).wait_recv()
                fw = rc(i, 3 + j, landed, landed, sib)
                fw.start()
                started.append(fw)
        for i in range(n):
            for j, ch in enumerate(chips):
                sc = 2 * ch[0] + ch[1]
                theirs = outs[i].at[sc, half(i, 1 - c)]
                rc(i, 3 + j, theirs, theirs, sib).wait_recv()
        for cp in own:
            cp.wait_recv()
        for cp in started + own:
            cp.wait_send()

    return pl.pallas_call(
        body, name=name,
        out_shape=tuple(jax.ShapeDtypeStruct((N_CHIPS,) + a.shape, a.dtype) for a in arrs),
        in_specs=[ANY] * n, out_specs=tuple([ANY] * n),
        scratch_shapes=[pltpu.SemaphoreType.DMA((per * n,)), pltpu.SemaphoreType.DMA((per * n,))],
        compiler_params=pltpu.CompilerParams(has_side_effects=True),
    )(*arrs)


HBM = pl.BlockSpec(memory_space=pltpu.HBM)
SEM = pl.BlockSpec(memory_space=pltpu.SEMAPHORE)
EFFECT = pltpu.SideEffectType.DATAFLOW_SIDE_EFFECTING
def _hbm(a):
    return pltpu.with_memory_space_constraint(a, pltpu.HBM)


def _half_rows(arr, which):
    hrows = arr.shape[-2] // 2
    return pl.ds(which * hrows, hrows)


def _ici_copies(srcs, lands, sems):
    x, y, c = _me()
    chip = 2 * x + y
    return [pltpu.make_async_remote_copy(src_ref=srcs[i].at[_half_rows(srcs[i], c)], dst_ref=lands[i].at[chip, _half_rows(srcs[i], c)],
                                         send_sem=sems[0].at[3 * i + j], recv_sem=sems[1].at[3 * i + j],
                                         device_id=(ch[0], ch[1], c), device_id_type=MESH)
            for i in range(len(srcs)) for j, ch in enumerate(_other_chips(x, y))]


def _own_copies(srcs, lands, sems):
    x, y, c = _me()
    return [pltpu.make_async_remote_copy(src_ref=srcs[i], dst_ref=lands[i].at[2 * x + y], send_sem=sems[0].at[i],
                                         recv_sem=sems[1].at[i], device_id=(x, y, 1 - c), device_id_type=MESH)
            for i in range(len(srcs))]


def _fwd_copies(srcs, lands, sems, half):
    x, y, c = _me()
    cps = []
    for i in range(len(srcs)):
        for j, ch in enumerate(_other_chips(x, y)):
            part = lands[i].at[2 * ch[0] + ch[1], _half_rows(srcs[i], half)]
            cps.append(pltpu.make_async_remote_copy(src_ref=part, dst_ref=part, send_sem=sems[0].at[3 * i + j],
                                                    recv_sem=sems[1].at[3 * i + j], device_id=(x, y, 1 - c), device_id_type=MESH))
    return cps


def gather_start(groups, *, name):
    arrs = [a for g in groups for a in g]
    n, ng = len(arrs), len(groups)
    bounds = np.cumsum([0] + [len(g) for g in groups])

    def body(*refs):
        srcs, lands = refs[:n], refs[n:2 * n]
        sems = refs[2 * n:2 * n + 4 * ng]
        token = refs[-1]
        for gi in range(ng):
            lo, hi = bounds[gi], bounds[gi + 1]
            for cp in _ici_copies(srcs[lo:hi], lands[lo:hi], sems[4 * gi:4 * gi + 2]):
                cp.start()
        for gi in range(ng):
            lo, hi = bounds[gi], bounds[gi + 1]
            for cp in _own_copies(srcs[lo:hi], lands[lo:hi], sems[4 * gi + 2:4 * gi + 4]):
                cp.start()
        token[...] = jnp.zeros_like(token)

    sem_shapes = []
    for g in groups:
        sem_shapes += [pltpu.SemaphoreType.DMA((3 * len(g),))] * 2 + [pltpu.SemaphoreType.DMA((len(g),))] * 2
    res = pl.pallas_call(
        body, name=name,
        out_shape=tuple(sem_shapes) + tuple(pltpu.HBM(a.shape, a.dtype) for a in arrs)
        + tuple(pltpu.HBM((N_CHIPS,) + a.shape, a.dtype) for a in arrs) + (jax.ShapeDtypeStruct((SUBLANES, LANES), F32),),
        in_specs=[HBM] * (2 * n),
        out_specs=tuple([SEM] * (4 * ng)) + tuple([HBM] * (2 * n)) + (pl.BlockSpec(memory_space=pltpu.VMEM),),
        input_output_aliases={i: 4 * ng + i for i in range(2 * n)},
        compiler_params=pltpu.CompilerParams(has_side_effects=EFFECT),
    )(*[_hbm(a) for a in arrs], *[_hbm(lax.empty((N_CHIPS,) + a.shape, a.dtype)) for a in arrs])
    sems, thru, lands, token = res[:4 * ng], res[4 * ng:4 * ng + n], res[4 * ng + n:4 * ng + 2 * n], res[-1]
    out = [{"ici": (sems[4 * gi], sems[4 * gi + 1]), "own": (sems[4 * gi + 2], sems[4 * gi + 3]),
            "thru": list(thru[bounds[gi]:bounds[gi + 1]]), "lands": list(lands[bounds[gi]:bounds[gi + 1]])} for gi in range(ng)]
    return out, token


def gather_forward(group, after, *, name):
    thru, lands = group["thru"], group["lands"]
    n = len(thru)

    def body(*refs):
        srcs, lands_r = refs[:n], refs[n:2 * n]
        ici = refs[2 * n:2 * n + 2]
        fwd = refs[2 * n + 3 + 2 * n:2 * n + 3 + 2 * n + 2]
        for cp in _ici_copies(srcs, lands_r, ici):
            cp.wait_send()
            cp.wait_recv()
        for cp in _fwd_copies(srcs, lands_r, fwd, _me()[2]):
            cp.start()

    res = pl.pallas_call(
        body, name=name,
        out_shape=tuple(pltpu.HBM(a.shape, a.dtype) for a in thru) + tuple(pltpu.HBM(a.shape, a.dtype) for a in lands)
        + (pltpu.SemaphoreType.DMA((3 * n,)),) * 2,
        in_specs=[HBM] * (2 * n) + [SEM, SEM, pl.BlockSpec(memory_space=pl.ANY)],
        out_specs=tuple([HBM] * (2 * n)) + (SEM, SEM),
        input_output_aliases={i: i for i in range(2 * n)},
        compiler_params=pltpu.CompilerParams(has_side_effects=EFFECT),
    )(*thru, *lands, *group["ici"], after)
    return {"own": group["own"], "fwd": (res[2 * n], res[2 * n + 1]), "thru": list(res[:n]), "lands": list(res[n:2 * n])}


def gather_wait(group, after, *, name):
    thru, lands = group["thru"], group["lands"]
    n = len(thru)

    def body(*refs):
        srcs, lands_r = refs[:n], refs[n:2 * n]
        own, fwd = refs[2 * n:2 * n + 2], refs[2 * n + 2:2 * n + 4]
        c = _me()[2]
        for mine, theirs in zip(_fwd_copies(srcs, lands_r, fwd, c), _fwd_copies(srcs, lands_r, fwd, 1 - c)):
            mine.wait_send()
            theirs.wait_recv()
        for cp in _own_copies(srcs, lands_r, own):
            cp.wait_send()
            cp.wait_recv()

    res = pl.pallas_call(
        body, name=name,
        out_shape=tuple(pltpu.HBM(a.shape, a.dtype) for a in thru) + tuple(pltpu.HBM(a.shape, a.dtype) for a in lands),
        in_specs=[HBM] * (2 * n) + [SEM] * 4 + [pl.BlockSpec(memory_space=pl.ANY)],
        out_specs=tuple([HBM] * (2 * n)),
        input_output_aliases={i: i for i in range(2 * n)},
        compiler_params=pltpu.CompilerParams(has_side_effects=EFFECT),
    )(*thru, *lands, *group["own"], *group["fwd"], after)
    return list(res[n:])


def _pair_copies(srcs, lands, sems):
    x, y, c = _me()
    return [pltpu.make_async_remote_copy(src_ref=srcs[i].at[:, _half_rows(srcs[i], 1 - c), :], dst_ref=lands[i], send_sem=sems[0].at[i],
                                         recv_sem=sems[1].at[i], device_id=(x, y, 1 - c), device_id_type=MESH)
            for i in range(len(srcs))]


def _scatter_copies(srcs, lands, sems):
    x, y, c = _me()
    return [pltpu.make_async_remote_copy(src_ref=srcs[i].at[2 * ch[0] + ch[1]], dst_ref=lands[i].at[j], send_sem=sems[0].at[3 * i + j],
                                         recv_sem=sems[1].at[3 * i + j], device_id=(ch[0], ch[1], c), device_id_type=MESH)
            for i in range(len(srcs)) for j, ch in enumerate(_other_chips(x, y))]


def split_start(copies, arrs, land_shapes, nsem, *, name):
    n = len(arrs)

    def body(*refs):
        for cp in copies(refs[:n], refs[n:2 * n], refs[2 * n:2 * n + 2]):
            cp.start()

    res = pl.pallas_call(
        body, name=name,
        out_shape=(pltpu.SemaphoreType.DMA((nsem,)),) * 2 + tuple(pltpu.HBM(a.shape, a.dtype) for a in arrs)
        + tuple(pltpu.HBM(s, a.dtype) for s, a in zip(land_shapes, arrs)),
        in_specs=[HBM] * (2 * n), out_specs=(SEM, SEM) + tuple([HBM] * (2 * n)),
        input_output_aliases={i: 2 + i for i in range(2 * n)},
        compiler_params=pltpu.CompilerParams(has_side_effects=EFFECT),
    )(*[_hbm(a) for a in arrs], *[_hbm(lax.empty(s, a.dtype)) for s, a in zip(land_shapes, arrs)])
    return {"copies": copies, "sems": (res[0], res[1]), "thru": list(res[2:2 + n]), "lands": list(res[2 + n:])}


def split_wait(state, after, *, name):
    thru, lands, copies = state["thru"], state["lands"], state["copies"]
    n = len(thru)

    def body(*refs):
        for cp in copies(refs[:n], refs[n:2 * n], refs[2 * n:2 * n + 2]):
            cp.wait_send()
            cp.wait_recv()

    res = pl.pallas_call(
        body, name=name,
        out_shape=tuple(pltpu.HBM(a.shape, a.dtype) for a in thru) + tuple(pltpu.HBM(a.shape, a.dtype) for a in lands),
        in_specs=[HBM] * (2 * n) + [SEM, SEM, pl.BlockSpec(memory_space=pl.ANY)], out_specs=tuple([HBM] * (2 * n)),
        input_output_aliases={i: i for i in range(2 * n)},
        compiler_params=pltpu.CompilerParams(has_side_effects=EFFECT),
    )(*thru, *lands, *state["sems"], after)
    return list(res[:n]), list(res[n:])


def sibling_send_other_half(arrs, *, name):
    n = len(arrs)

    def body(*refs):
        ins, outs = refs[:n], refs[n:2 * n]
        send_sems, recv_sems = refs[2 * n:]
        x, y, c = _me()
        cps = []
        for i in range(n):
            hrows = arrs[i].shape[1] // 2
            cps.append(pltpu.make_async_remote_copy(
                src_ref=ins[i].at[:, pl.ds((1 - c) * hrows, hrows), :], dst_ref=outs[i], send_sem=send_sems.at[i],
                recv_sem=recv_sems.at[i], device_id=(x, y, 1 - c), device_id_type=MESH))
        for cp in cps:
            cp.start()
        for cp in cps:
            cp.wait_recv()
        for cp in cps:
            cp.wait_send()

    return pl.pallas_call(
        body, name=name,
        out_shape=tuple(jax.ShapeDtypeStruct((a.shape[0], a.shape[1] // 2, a.shape[2]), a.dtype) for a in arrs),
        in_specs=[ANY] * n, out_specs=tuple([ANY] * n),
        scratch_shapes=[pltpu.SemaphoreType.DMA((n,)), pltpu.SemaphoreType.DMA((n,))],
        compiler_params=pltpu.CompilerParams(has_side_effects=True),
    )(*arrs)


def chip_scatter(arrs, *, name):
    n = len(arrs)

    def body(*refs):
        ins, outs = refs[:n], refs[n:2 * n]
        send_sems, recv_sems = refs[2 * n:]
        x, y, c = _me()
        cps = []
        for i in range(n):
            for j, ch in enumerate(_other_chips(x, y)):
                cps.append(pltpu.make_async_remote_copy(
                    src_ref=ins[i].at[2 * ch[0] + ch[1]], dst_ref=outs[i].at[j], send_sem=send_sems.at[3 * i + j],
                    recv_sem=recv_sems.at[3 * i + j], device_id=(ch[0], ch[1], c), device_id_type=MESH))
        for cp in cps:
            cp.start()
        for cp in cps:
            cp.wait_recv()
        for cp in cps:
            cp.wait_send()

    return pl.pallas_call(
        body, name=name,
        out_shape=tuple(jax.ShapeDtypeStruct((3,) + a.shape[1:], a.dtype) for a in arrs),
        in_specs=[ANY] * n, out_specs=tuple([ANY] * n),
        scratch_shapes=[pltpu.SemaphoreType.DMA((3 * n,)), pltpu.SemaphoreType.DMA((3 * n,))],
        compiler_params=pltpu.CompilerParams(has_side_effects=True),
    )(*arrs)


def sibling_join_halves(arrs, *, name):
    n = len(arrs)

    def body(*refs):
        outs = refs[n:2 * n]
        send_sems, recv_sems = refs[2 * n:]
        x, y, c = _me()
        cps = []
        for i in range(n):
            hrows = arrs[i].shape[0] // 2
            mine = outs[i].at[pl.ds(c * hrows, hrows)]
            cps.append(pltpu.make_async_remote_copy(src_ref=mine, dst_ref=mine, send_sem=send_sems.at[i],
                                                    recv_sem=recv_sems.at[i], device_id=(x, y, 1 - c), device_id_type=MESH))
        for cp in cps:
            cp.start()
        for i in range(n):
            hrows = arrs[i].shape[0] // 2
            theirs = outs[i].at[pl.ds((1 - c) * hrows, hrows)]
            pltpu.make_async_remote_copy(src_ref=theirs, dst_ref=theirs, send_sem=send_sems.at[i], recv_sem=recv_sems.at[i],
                                         device_id=(x, y, 1 - c), device_id_type=MESH).wait_recv()
        for cp in cps:
            cp.wait_send()

    return pl.pallas_call(
        body, name=name,
        out_shape=tuple(jax.ShapeDtypeStruct(a.shape, a.dtype) for a in arrs),
        in_specs=[ANY] * n, out_specs=tuple([ANY] * n),
        input_output_aliases={i: i for i in range(n)},
        scratch_shapes=[pltpu.SemaphoreType.DMA((n,)), pltpu.SemaphoreType.DMA((n,))],
        compiler_params=pltpu.CompilerParams(has_side_effects=True),
    )(*arrs)


def pre_reduce(gd, sib, who, *, name):
    _, r, c = gd.shape
    h = r // 2
    tr = _pick(h, (256, 128, 64, 32, 16, 8))
    nrb = h // tr

    def body(who_ref, gd_ref, sib_ref, pb_ref, own_ref):
        p = gd_ref[...] + sib_ref[...]
        pb_ref[...] = p.astype(BF16)

        @pl.when(pl.program_id(1) == who_ref[1])
        def _():
            own_ref[...] = p

    return pl.pallas_call(
        body, name=name,
        out_shape=(jax.ShapeDtypeStruct((N_CHIPS, h, c), BF16), jax.ShapeDtypeStruct((h, c), F32)),
        grid_spec=pltpu.PrefetchScalarGridSpec(
            num_scalar_prefetch=1, grid=(nrb, N_CHIPS),
            in_specs=[pl.BlockSpec((None, tr, c), lambda i, s, w: (s, w[0] * nrb + i, 0)),
                      pl.BlockSpec((None, tr, c), lambda i, s, w: (s, i, 0))],
            out_specs=(pl.BlockSpec((None, tr, c), lambda i, s, w: (s, i, 0)),
                       pl.BlockSpec((tr, c), lambda i, s, w: (i, 0)))),
        compiler_params=_cparams(("parallel", "arbitrary")),
    )(who, gd, sib)


def final_reduce(own, rcv, who, *, name):
    h, c = own.shape
    tr = _pick(h, (256, 128, 64, 32, 16, 8))
    nrb = h // tr

    def body(who_ref, own_ref, rcv_ref, o_ref):
        del who_ref
        acc = own_ref[...]
        for j in range(3):
            acc = acc + rcv_ref[j].astype(F32)
        o_ref[...] = acc

    return pl.pallas_call(
        body, name=name, out_shape=jax.ShapeDtypeStruct((2 * h, c), F32),
        grid_spec=pltpu.PrefetchScalarGridSpec(
            num_scalar_prefetch=1, grid=(nrb,),
            in_specs=[pl.BlockSpec((tr, c), lambda i, w: (i, 0)), pl.BlockSpec((3, tr, c), lambda i, w: (0, i, 0))],
            out_specs=pl.BlockSpec((tr, c), lambda i, w: (w[0] * nrb + i, 0))),
        compiler_params=_cparams(("parallel",)),
    )(who, own, rcv)


def sum8(a, *, name):
    _, r, c = a.shape
    tr = _pick(r, (256, 128, 64, 32, 16, 8))

    def body(a_ref, o_ref):
        acc = a_ref[0]
        for j in range(1, 8):
            acc = acc + a_ref[j]
        o_ref[...] = acc

    return pl.pallas_call(
        body, name=name, out_shape=jax.ShapeDtypeStruct((r, c), F32), grid=(r // tr,),
        in_specs=[pl.BlockSpec((8, tr, c), lambda i: (0, i, 0))],
        out_specs=pl.BlockSpec((tr, c), lambda i: (i, 0)),
        compiler_params=_cparams(("parallel",)),
    )(a)


def allgather8(a, *, name):
    def body(in_ref, out_ref, send_sems, recv_sems, local_sem):
        x, y, c = _me()
        me = 4 * x + 2 * y + c
        loc = pltpu.make_async_copy(in_ref, out_ref.at[me], local_sem)
        loc.start()
        cps = []
        for k in range(1, 8):
            fx, fy, fc = (k >> 2) & 1, (k >> 1) & 1, k & 1
            to = (x + fx - 2 * x * fx, y + fy - 2 * y * fy, c + fc - 2 * c * fc)
            cps.append(pltpu.make_async_remote_copy(src_ref=in_ref, dst_ref=out_ref.at[me], send_sem=send_sems.at[k - 1],
                                                    recv_sem=recv_sems.at[k - 1], device_id=to, device_id_type=MESH))
        for cp in cps:
            cp.start()
        for k in range(1, 8):
            fx, fy, fc = (k >> 2) & 1, (k >> 1) & 1, k & 1
            src = 4 * (x + fx - 2 * x * fx) + 2 * (y + fy - 2 * y * fy) + (c + fc - 2 * c * fc)
            pltpu.make_async_remote_copy(src_ref=in_ref, dst_ref=out_ref.at[src], send_sem=send_sems.at[k - 1],
                                         recv_sem=recv_sems.at[k - 1], device_id=(x, y, c), device_id_type=MESH).wait_recv()
        for cp in cps:
            cp.wait_send()
        loc.wait()

    return pl.pallas_call(
        body, name=name, out_shape=jax.ShapeDtypeStruct((8,) + a.shape, a.dtype),
        in_specs=[ANY], out_specs=ANY,
        scratch_shapes=[pltpu.SemaphoreType.DMA((7,)), pltpu.SemaphoreType.DMA((7,)), pltpu.SemaphoreType.DMA],
        compiler_params=pltpu.CompilerParams(has_side_effects=True),
    )(a)


def local_step(x, tgt, wb, ws, arrive=lambda layer, event, after: {}, reduce=lambda group, event, gb, after: None):
    t, d = x.shape
    nh = ws["alog"].shape[0]
    ng = ws["w_s"].shape[0]
    assert nh == ng
    mix_w = (nh + ng) * HEAD_DIM

    (h0,) = rows_fwd(stage_norm, [ws["e_norm"]], [x], [(1, BF16, d)], name="f_norm_e")
    pm = mm_nt(h0, wb["w_main_t"], name="f_proj_main")
    pba = mm_nt(h0, wb["w_ba_t"], name="f_proj_ba")
    qkv = conv_fwd(pm, ws["conv_w"], mode="silu", name="f_conv4")
    arrive("layer0", "landed", qkv)
    cat, states, tinvs = delta_fwd(qkv, pm, pba, ws["alog"], ws["dtb"], ws["onorm"], mix_w, name="f_delta")
    cat = gmlp_fwd(pm, ws["lng"], ws["lnb"], ws["w_s"], ws["bs"], cat, name="f_gmlp")
    wb = {**wb, **arrive("layer0", "joined", cat)}
    y0 = mm_nn(cat, wb["w_out"], name="f_out")
    x1, h1 = rows_fwd(stage_res_norm, [ws["f_norm0"]], [x, y0], [(0, F32, d), (1, BF16, d)], name="f_norm_f0")
    a1, s1 = mm_nn(h1, wb["w1_0"], name="f_mlp0_up", epilogue="relu2")
    arrive("layer1", "landed", s1)
    y1 = mm_nn(s1, wb["w2_0"], name="f_mlp0_down")
    x2, h2 = rows_fwd(stage_res_norm, [ws["o_norm"]], [x1, y1], [(0, F32, d), (1, BF16, d)], name="f_norm_o")
    wb = {**wb, **arrive("layer1", "joined", h2)}
    zz = mm_nn(h2, wb["pw1"], name="f_pw1")
    zparts = [(zz, d, 0), (zz, d, 1)]
    (gl,) = rows_fwd(stage_glu, [ws["b1a"], ws["b1b"]], zparts, [(0, F32, d)], name="f_glu")
    cv = conv_fwd(gl, ws["dw"], mode="plain", name="f_conv31")
    ln_params = [ws["dw_b"], ws["ln_g"], ws["ln_b"]]
    (sl,) = rows_fwd(stage_ln_silu, ln_params, [cv], [(0, BF16, d)], name="f_ln_silu")
    y2 = mm_nn(sl, wb["pw2"], name="f_pw2")
    x3, h3 = rows_fwd(stage_res_bias_norm, [ws["b2"], ws["f_norm1"]], [x2, y2], [(0, F32, d), (1, BF16, d)], name="f_norm_f1")
    a3, s3 = mm_nn(h3, wb["w1_1"], name="f_mlp1_up", epilogue="relu2")
    y3 = mm_nn(s3, wb["w2_1"], name="f_mlp1_down")
    loss, d4, d4b, g_final = loss_head(ws["final_norm"], x3, y3, tgt, name="loss_head")

    gb, gs = {}, {"final_norm": g_final}
    s_up = wb["w1_0"].shape[0]

    gb["w2_1"] = mm_tn(s3, d4b, groups=1, name="b_mlp1_down_w")
    dpre3 = mm_nt(d4b, wb["w2_1"], name="b_mlp1_down_x", mul=a3, out_dtype=BF16)
    gb["w1_1"] = mm_tn(h3, dpre3, groups=s_up, name="b_mlp1_up_w")
    reduce("mlp1", "ready", gb, None)
    dh3 = mm_nt(dpre3, wb["w1_1"], name="b_mlp1_up_x")
    d3, d3b, gs["b2"], gs["f_norm1"] = rows_bwd(stage_res_bias_norm, [ws["b2"], ws["f_norm1"]], [x2, y2], [d4, dh3],
                                                [(0, F32), (1, BF16)], name="b_norm_f1")
    reduce("mlp1", "paired", gb, d3)
    gb["pw2"] = mm_tn(sl, d3b, groups=1, name="b_pw2_w")
    dsl = mm_nt(d3b, wb["pw2"], name="b_pw2_x")
    dcv, gs["dw_b"], gs["ln_g"], gs["ln_b"] = rows_bwd(stage_ln_silu, ln_params, [cv], [dsl], [(0, F32)], name="b_ln_silu")
    dgl, gs["dw"] = conv_bwd(gl, ws["dw"], dcv, name="b_conv31")
    dzz, gs["b1a"], gs["b1b"] = rows_bwd(stage_glu, [ws["b1a"], ws["b1b"]], zparts, [dgl], [(0, BF16), (1, BF16)],
                                         name="b_glu", concat=True)
    gb["pw1"] = mm_tn(h2, dzz, groups=wb["pw1"].shape[0], name="b_pw1_w")
    reduce("conv", "ready", gb, None)
    dh2 = mm_nt(dzz, wb["pw1"], name="b_pw1_x")
    d2, d2b, gs["o_norm"] = rows_bwd(stage_res_norm, [ws["o_norm"]], [x1, y1], [d3, dh2], [(0, F32), (1, BF16)], name="b_norm_o")
    reduce("conv", "paired", gb, d2)
    gb["w2_0"] = mm_tn(s1, d2b, groups=1, name="b_mlp0_down_w")
    dpre1 = mm_nt(d2b, wb["w2_0"], name="b_mlp0_down_x", mul=a1, out_dtype=BF16)
    gb["w1_0"] = mm_tn(h1, dpre1, groups=s_up, name="b_mlp0_up_w")
    reduce("mlp0", "ready", gb, None)
    dh1 = mm_nt(dpre1, wb["w1_0"], name="b_mlp0_up_x")
    d1, d1b, gs["f_norm0"] = rows_bwd(stage_res_norm, [ws["f_norm0"]], [x, y0], [d2, dh1], [(0, F32), (1, BF16)], name="b_norm_f0")
    reduce("mlp0", "paired", gb, d1)
    gb["w_out"] = mm_tn(cat, d1b, groups=1, name="b_out_w")
    dcat = mm_nt(d1b, wb["w_out"], name="b_out_x")
    dqkv_c, dpm, dpba, gs["alog"], gs["dtb"], gs["onorm"] = delta_bwd(
        qkv, pm, pba, ws["alog"], ws["dtb"], ws["onorm"], states, tinvs, dcat, name="b_delta")
    dpm, gs["lng"], gs["lnb"], gs["w_s"], gs["bs"] = gmlp_bwd(
        pm, ws["lng"], ws["lnb"], ws["w_s"], ws["bs"], dcat, dpm, name="b_gmlp")
    dconv = conv_fwd(pm, ws["conv_w"], mode="silu_bwd", g=dqkv_c, name="b_conv4_silu")
    dpm, gs["conv_w"] = conv_bwd(pm, ws["conv_w"], dconv, name="b_conv4", into=dpm)
    gb["w_main_t"] = mm_tn(dpm, h0, groups=1, name="b_proj_main_w")
    gb["w_ba_t"] = mm_tn(dpba, h0, groups=1, name="b_proj_ba_w")
    dh0 = mm_nn(dpm, wb["w_main_t"], name="b_proj_main_x")
    dh0 = mm_nn(dpba, wb["w_ba_t"], name="b_proj_ba_x", add=dh0)
    grad_x, gs["e_norm"] = rows_bwd(stage_norm, [ws["e_norm"]], [x], [d1, dh0], [(0, F32)], name="b_norm_e")
    return loss, grad_x, gb, gs


WEIGHTS = ["e_norm", "e_w_in", "e_conv_w", "e_a_log", "e_dt_bias", "e_o_norm", "e_ln_g", "e_ln_b", "e_w_s", "e_b_s", "e_w_out",
           "o_norm", "o_pw1", "o_pw1_b", "o_dw", "o_dw_b", "o_ln_g", "o_ln_b", "o_pw2", "o_pw2_b", "f_norm", "f_w1", "f_w2",
           "final_norm"]
BIG = ["e_w_in", "e_w_out", "o_pw1", "o_pw2", "f_w1", "f_w2"]
SMALL_SHARDED = ["e_conv_w", "o_norm", "o_pw1_b", "o_dw", "o_dw_b", "o_ln_g", "o_ln_b", "o_pw2_b"]
SMALL = [n for n in WEIGHTS if n not in BIG]
LANES = 128
PACK_ROWS = 16
IN_ROW_MULT = 256
REDUCE_GROUPS = {"mlp1": ("w2_1", "w1_1"), "conv": ("pw2", "pw1"), "mlp0": ("w2_0", "w1_0")}


def _pack(arrs):
    flat = jnp.concatenate([a.reshape(-1).astype(F32) for a in arrs])
    n = flat.shape[0]
    rows = -(-n // (LANES * PACK_ROWS)) * PACK_ROWS
    return jnp.pad(flat, (0, rows * LANES - n)).reshape(rows, LANES)


def _unpack(flat, shapes, lead=()):
    outs, off = [], 0
    for s in shapes:
        n = int(np.prod(s))
        outs.append(flat[..., off:off + n].reshape(lead + tuple(s)))
        off += n
    return outs


def _chip_major(g, s=N_CHIPS):
    k, n = g.shape
    return g.reshape(k, s, n // s).transpose(1, 0, 2)


def kernel(x, e_norm, e_w_in, e_conv_w, e_a_log, e_dt_bias, e_o_norm, e_ln_g, e_ln_b, e_w_s, e_b_s, e_w_out, o_norm, o_pw1, o_pw1_b, o_dw, o_dw_b, o_ln_g, o_ln_b, o_pw2, o_pw2_b, f_norm, f_w1, f_w2, final_norm, loss_target, m_e_norm, m_e_w_in, m_e_conv_w, m_e_a_log, m_e_dt_bias, m_e_o_norm, m_e_ln_g, m_e_ln_b, m_e_w_s, m_e_b_s, m_e_w_out, m_o_norm, m_o_pw1, m_o_pw1_b, m_o_dw, m_o_dw_b, m_o_ln_g, m_o_ln_b, m_o_pw2, m_o_pw2_b, m_f_norm, m_f_w1, m_f_w2, m_final_norm, v_e_norm, v_e_w_in, v_e_conv_w, v_e_a_log, v_e_dt_bias, v_e_o_norm, v_e_ln_g, v_e_ln_b, v_e_w_s, v_e_b_s, v_e_w_out, v_o_norm, v_o_pw1, v_o_pw1_b, v_o_dw, v_o_dw_b, v_o_ln_g, v_o_ln_b, v_o_pw2, v_o_pw2_b, v_f_norm, v_f_w1, v_f_w2, v_final_norm):
    a = dict(locals())
    xi, yi, ci = _me()
    chip = 2 * xi + yi
    who = jnp.stack([ci, chip]).astype(jnp.int32)
    t, d = x.shape[1], x.shape[2]
    nh, ng = e_a_log.shape[1], e_w_s.shape[1]
    n_qkv, n_av, n_bw = 3 * nh * HEAD_DIM, nh * HEAD_DIM, ng * HEAD_DIM
    in_cols = n_qkv + n_av + 2 * nh + 2 * n_bw
    c_ba = n_qkv + n_av

    sh_in = in_cols // N_CHIPS
    pad_in = -(-sh_in // IN_ROW_MULT) * IN_ROW_MULT - sh_in
    w_in_t_local = jnp.pad(e_w_in[0].T.astype(BF16), ((0, pad_in), (0, 0)))
    small_local = [a[n] for n in SMALL_SHARDED]
    g_in, g_small = gather_shards([w_in_t_local, _pack(small_local)], name="gather_weights_in")
    travelling, token = gather_start([[w.astype(BF16) for w in (e_w_out[0], f_w1[0], f_w2[0])],
                                      [w.astype(BF16) for w in (o_pw1[0], o_pw2[0], f_w1[1], f_w2[1])]], name="gather_start")

    state = {"layer0": travelling[0], "layer1": travelling[1]}

    def arrive(layer, event, after):
        if event == "landed":
            state[layer] = gather_forward(state[layer], after, name="gather_forward_" + layer)
            return {}
        got = gather_wait(state[layer], after, name="gather_wait_" + layer)
        if layer == "layer0":
            g_out, g_w1, g_w2 = got
            return {"w_out": g_out.reshape(1, -1, d), "w1_0": g_w1, "w2_0": g_w2.reshape(1, -1, d)}
        g_pw1, g_pw2, g_w1, g_w2 = got
        return {"pw1": g_pw1, "pw2": g_pw2.reshape(1, -1, d), "w1_1": g_w1, "w2_1": g_w2.reshape(1, -1, d)}

    def in_rows(lo, hi):
        parts = []
        for s in range(N_CHIPS):
            a0, a1 = max(lo, s * sh_in), min(hi, (s + 1) * sh_in)
            if a0 < a1:
                parts.append(g_in[s, a0 - s * sh_in:a1 - s * sh_in])
        return parts

    wb = {
        "w_main_t": jnp.concatenate(in_rows(0, c_ba) + in_rows(c_ba + 2 * nh, in_cols), axis=0)[None],
        "w_ba_t": jnp.pad(jnp.concatenate(in_rows(c_ba, c_ba + 2 * nh), axis=0), ((0, LANES - 2 * nh), (0, 0)))[None],
    }
    pieces = _unpack(g_small.reshape(N_CHIPS, -1), [w.shape for w in small_local], lead=(N_CHIPS,))
    full = {n: jnp.moveaxis(p, 0, -2).reshape(p.shape[1:-1] + (N_CHIPS * p.shape[-1],)) for n, p in zip(SMALL_SHARDED, pieces)}
    ws = {
        "e_norm": e_norm + token[0, 0],
        "conv_w": full["e_conv_w"][0], "alog": e_a_log.reshape(nh, 1, 1), "dtb": e_dt_bias.reshape(nh, 1, 1),
        "onorm": e_o_norm, "lng": e_ln_g.reshape(ng, 1, HEAD_DIM), "lnb": e_ln_b.reshape(ng, 1, HEAD_DIM), "w_s": e_w_s[0],
        "bs": e_b_s.reshape(ng, HEAD_DIM, 1), "o_norm": full["o_norm"], "b1a": full["o_pw1_b"][:, :d], "b1b": full["o_pw1_b"][:, d:],
        "dw": full["o_dw"][0], "dw_b": full["o_dw_b"], "ln_g": full["o_ln_g"], "ln_b": full["o_ln_b"], "b2": full["o_pw2_b"],
        "f_norm0": f_norm[0:1], "f_norm1": f_norm[1:2], "final_norm": final_norm.reshape(1, d),
    }

    pending = {}

    def reduce(group, event, gb, after):
        names = REDUCE_GROUPS[group]
        if event == "ready":
            arrs = [gb[nme] if gb[nme].shape[0] == N_CHIPS else gb[nme].reshape(N_CHIPS, -1, d) for nme in names]
            pending[group] = split_start(_pair_copies, arrs, [(N_CHIPS, g.shape[1] // 2, g.shape[2]) for g in arrs], len(arrs),
                                         name="reduce_sibling_start_" + group)
        else:
            mine, theirs = split_wait(pending[group], after, name="reduce_sibling_wait_" + group)
            pairs = [pre_reduce(gd, sb, who, name="reduce_pair_" + nme) for nme, gd, sb in zip(names, mine, theirs)]
            sums = [p[0] for p in pairs]
            pending[group] = {"own": [p[1] for p in pairs],
                              "scatter": split_start(_scatter_copies, sums, [(3,) + s.shape[1:] for s in sums], 3 * len(sums),
                                                     name="reduce_chips_start_" + group)}

    loss, grad_x, gb, gs = local_step(x[0], loss_target[0], wb, ws, arrive, reduce)

    gm, gba = gb["w_main_t"][0], gb["w_ba_t"][0]
    g_in_t = jnp.concatenate([gm[:c_ba], gba[:2 * nh], gm[c_ba:]], axis=0).reshape(N_CHIPS, sh_in, d)
    by_chip = [jnp.pad(g_in_t, ((0, 0), (0, pad_in), (0, 0))), gb["w_out"].reshape(N_CHIPS, -1, d)]
    from_sibling = sibling_send_other_half(by_chip, name="reduce_sibling")
    to_send, own = [], []
    for nme, gd, sb in zip(("w_in", "w_out"), by_chip, from_sibling):
        pb, ow = pre_reduce(gd, sb, who, name="reduce_pair_" + nme)
        to_send.append(pb)
        own.append(ow)
    received = chip_scatter(to_send, name="reduce_chips")
    half = {nme: final_reduce(ow, rc, who, name="reduce_sum_" + nme) for nme, ow, rc in zip(("w_in", "w_out"), own, received)}
    for group, names in REDUCE_GROUPS.items():
        _, landed = split_wait(pending[group]["scatter"], grad_x, name="reduce_chips_wait_" + group)
        for nme, ow, rc in zip(names, pending[group]["own"], landed):
            half[nme] = final_reduce(ow, rc, who, name="reduce_sum_" + nme)
    r_in, r_out, r_pw1, r_pw2, r_w1_0, r_w1_1, r_w2_0, r_w2_1 = sibling_join_halves(
        [half[nme] for nme in ("w_in", "w_out", "pw1", "pw2", "w1_0", "w1_1", "w2_0", "w2_1")], name="reduce_join")
    big_grads = {"e_w_in": [r_in[:sh_in]], "e_w_out": [r_out], "o_pw1": [r_pw1], "o_pw2": [r_pw2], "f_w1": [r_w1_0, r_w1_1],
                 "f_w2": [r_w2_0, r_w2_1]}

    small_global = {
        "e_norm": gs["e_norm"], "e_conv_w": gs["conv_w"][None], "e_a_log": gs["alog"].reshape(1, nh), "e_dt_bias": gs["dtb"].reshape(1, nh),
        "e_o_norm": gs["onorm"], "e_ln_g": gs["lng"].reshape(1, n_bw), "e_ln_b": gs["lnb"].reshape(1, n_bw), "e_w_s": gs["w_s"][None],
        "e_b_s": gs["bs"].reshape(1, ng, HEAD_DIM), "o_norm": gs["o_norm"], "o_pw1_b": jnp.concatenate([gs["b1a"], gs["b1b"]], axis=1),
        "o_dw": gs["dw"][None], "o_dw_b": gs["dw_b"], "o_ln_g": gs["ln_g"], "o_ln_b": gs["ln_b"], "o_pw2_b": gs["b2"],
        "f_norm": jnp.concatenate([gs["f_norm0"], gs["f_norm1"]], axis=0), "final_norm": gs["final_norm"].reshape(d),
    }
    packed = _pack([small_global[n] for n in SMALL])
    summed = sum8(allgather8(packed, name="reduce_small"), name="reduce_small_sum")
    small_full = _unpack(summed.reshape(-1), [small_global[n].shape for n in SMALL])
    small_grads = {}
    for n, g in zip(SMALL, small_full):
        if n in SMALL_SHARDED:
            width = a[n].shape[-1]
            g = lax.dynamic_slice_in_dim(g, chip * width, width, axis=g.ndim - 1)
        small_grads[n] = g

    out = {}
    for n in BIG:
        w, m, v = a[n], a["m_" + n], a["v_" + n]
        if n == "e_w_in":
            res = adamw(w.transpose(0, 2, 1), big_grads[n], m.transpose(0, 2, 1), v.transpose(0, 2, 1), name="adamw_" + n)
            out[n] = tuple(r.transpose(0, 2, 1) for r in res)
        else:
            out[n] = adamw(w, big_grads[n], m, v, name="adamw_" + n)
    sw, sm, sv, sg = (_pack([src[n] for n in SMALL])[None] for src in
                      ({n: a[n] for n in SMALL}, {n: a["m_" + n] for n in SMALL}, {n: a["v_" + n] for n in SMALL}, small_grads))
    res = adamw(sw, [sg[0]], sm, sv, name="adamw_small")
    shapes = [a[n].shape for n in SMALL]
    unpacked = [_unpack(r.reshape(-1), shapes) for r in res]
    for i, n in enumerate(SMALL):
        out[n] = tuple(u[i] for u in unpacked)

    total = lax.psum(loss[0, 0], ("x", "y", "c"))
    result = [total, grad_x[None]]
    for k in range(4):
        result += [out[n][k] for n in WEIGHTS]
    return tuple(result)
```

```python
import functools
import math

import jax
import jax.numpy as jnp
import numpy as np
from jax import lax
from jax.experimental import pallas as pl
from jax.experimental.pallas import tpu as pltpu

F32 = jnp.float32
BF16 = jnp.bfloat16
HI = lax.Precision.HIGHEST

EPS = 1e-6
CHUNK = 64
PAIR = 2 * CHUNK
HEAD_DIM = 128
A_CONV = 4
C_WIDTH = 31
N_CHIPS = 4
ADAM_LR, ADAM_B1, ADAM_B2, ADAM_EPS, ADAM_WD, ADAM_STEP = 0.001, 0.9, 0.999, 1e-08, 0.01, 10

VMEM_LIMIT = 56 * 1024 * 1024


def _cparams(sem=None):
    return pltpu.CompilerParams(dimension_semantics=sem, vmem_limit_bytes=VMEM_LIMIT)


def _pick(n, prefs):
    for p in prefs:
        if n % p == 0:
            return p
    return n


def mm_nn(a, w, *, name, epilogue=None, add=None, out_dtype=F32):
    m, k = a.shape
    s, _, ns = w.shape
    n = s * ns
    tm = _pick(m, (1024, 512, 256, 128))
    tn = _pick(ns, (1024, 512, 256, 128))
    tk = _pick(k, (2048, 1024, 512, 256, 128))
    nk = k // tk
    npb = ns // tn
    assert add is None or epilogue is None

    def body(a_ref, w_ref, *rest):
        if add is not None:
            add_ref, rest = rest[0], rest[1:]
        if epilogue == "relu2":
            o1_ref, o2_ref = rest[0], rest[1]
            acc_ref = rest[2] if nk > 1 else None
        else:
            o1_ref = rest[0]
            acc_ref = rest[1] if nk > 1 else None
        part = jnp.dot(a_ref[...], w_ref[...], preferred_element_type=F32)

        def finish(c):
            if epilogue == "relu2":
                r = jnp.maximum(c, 0.0)
                o1_ref[...] = r.astype(o1_ref.dtype)
                o2_ref[...] = (r * r).astype(o2_ref.dtype)
            elif add is not None:
                o1_ref[...] = (c + add_ref[...].astype(F32)).astype(o1_ref.dtype)
            else:
                o1_ref[...] = c.astype(o1_ref.dtype)

        if nk == 1:
            finish(part)
        else:
            kk = pl.program_id(2)

            @pl.when(kk == 0)
            def _():
                acc_ref[...] = part

            @pl.when(kk > 0)
            def _():
                acc_ref[...] += part

            @pl.when(kk == nk - 1)
            def _():
                finish(acc_ref[...])

    o_spec = pl.BlockSpec((tm, tn), lambda i, j, kk: (i, j))
    if epilogue == "relu2":
        out_shape = (jax.ShapeDtypeStruct((m, n), BF16), jax.ShapeDtypeStruct((m, n), BF16))
        out_specs = (o_spec, o_spec)
    else:
        out_shape = jax.ShapeDtypeStruct((m, n), out_dtype)
        out_specs = o_spec
    return pl.pallas_call(
        body, name=name, out_shape=out_shape,
        grid=(m // tm, n // tn, nk),
        in_specs=[pl.BlockSpec((tm, tk), lambda i, j, kk: (i, kk)),
                  pl.BlockSpec((None, tk, tn), lambda i, j, kk: (j // npb, kk, j % npb))] + ([o_spec] if add is not None else []),
        out_specs=out_specs,
        scratch_shapes=[pltpu.VMEM((tm, tn), F32)] if nk > 1 else [],
        compiler_params=_cparams(("parallel", "parallel", "arbitrary")),
    )(*([a, w] + ([add] if add is not None else [])))


def mm_nt(a, w, *, name, mul=None, add=None, out_dtype=F32):
    assert mul is None or add is None
    if add is not None:
        mul = add
    m, n = a.shape
    s, k, ns = w.shape
    assert n == s * ns
    tm = _pick(m, (1024, 512, 256, 128))
    tko = _pick(k, (1024, 512, 256, 128))
    tn = _pick(ns, (2048, 1024, 512, 256, 128))
    nn = n // tn
    npb = ns // tn

    def body(a_ref, w_ref, *rest):
        if mul is not None:
            m_ref, o_ref = rest[0], rest[1]
            acc_ref = rest[2] if nn > 1 else None
        else:
            m_ref, o_ref = None, rest[0]
            acc_ref = rest[1] if nn > 1 else None
        part = lax.dot_general(a_ref[...], w_ref[...], (((1,), (1,)), ((), ())), preferred_element_type=F32)

        def finish(c):
            if add is not None:
                c = c + m_ref[...].astype(F32)
            elif m_ref is not None:
                c = c * (2.0 * m_ref[...].astype(F32))
            o_ref[...] = c.astype(o_ref.dtype)

        if nn == 1:
            finish(part)
        else:
            kk = pl.program_id(2)

            @pl.when(kk == 0)
            def _():
                acc_ref[...] = part

            @pl.when(kk > 0)
            def _():
                acc_ref[...] += part

            @pl.when(kk == nn - 1)
            def _():
                finish(acc_ref[...])

    in_specs = [pl.BlockSpec((tm, tn), lambda i, j, kk: (i, kk)),
                pl.BlockSpec((None, tko, tn), lambda i, j, kk: (kk // npb, j, kk % npb))]
    args = [a, w]
    if mul is not None:
        in_specs.append(pl.BlockSpec((tm, tko), lambda i, j, kk: (i, j)))
        args.append(mul)
    return pl.pallas_call(
        body, name=name, out_shape=jax.ShapeDtypeStruct((m, k), out_dtype),
        grid=(m // tm, k // tko, nn),
        in_specs=in_specs,
        out_specs=pl.BlockSpec((tm, tko), lambda i, j, kk: (i, j)),
        scratch_shapes=[pltpu.VMEM((tm, tko), F32)] if nn > 1 else [],
        compiler_params=_cparams(("parallel", "parallel", "arbitrary")),
    )(*args)


def mm_tn(a, b, *, groups, name):
    m, k = a.shape
    _, n = b.shape
    ns = n // groups
    tko = _pick(k, (1024, 512, 256, 128))
    tn = _pick(ns, (1024, 512, 256, 128))
    tc = _pick(m, (2048, 1024, 512, 256, 128))
    nc = m // tc
    npb = ns // tn

    def body(a_ref, b_ref, o_ref):
        part = lax.dot_general(a_ref[...], b_ref[...], (((0,), (0,)), ((), ())), preferred_element_type=F32)
        kk = pl.program_id(2)

        @pl.when(kk == 0)
        def _():
            o_ref[...] = part

        @pl.when(kk > 0)
        def _():
            o_ref[...] += part

    return pl.pallas_call(
        body, name=name, out_shape=jax.ShapeDtypeStruct((groups, k, ns), F32),
        grid=(k // tko, n // tn, nc),
        in_specs=[pl.BlockSpec((tc, tko), lambda i, j, kk: (kk, i)),
                  pl.BlockSpec((tc, tn), lambda i, j, kk: (kk, j))],
        out_specs=pl.BlockSpec((None, tko, tn), lambda i, j, kk: (j // npb, i, j % npb)),
        compiler_params=_cparams(("parallel", "parallel", "arbitrary")),
    )(a, b)


ROWS = 128


def _full_spec(arr):
    nd = arr.ndim
    return pl.BlockSpec(arr.shape, lambda i, _nd=nd: (0,) * _nd)


def _row_spec(x, rows):
    if isinstance(x, tuple):
        _, w, cb = x
        return pl.BlockSpec((rows, w), lambda i, _cb=cb: (i, _cb))
    return pl.BlockSpec((rows, x.shape[1]), lambda i: (i, 0))


def _arr(x):
    return x[0] if isinstance(x, tuple) else x


def _width(x):
    return x[1] if isinstance(x, tuple) else x.shape[1]


def rows_fwd(fn, params, xs, stores, *, name, rows=ROWS):
    t = _arr(xs[0]).shape[0]
    np_, nx = len(params), len(xs)

    def body(*refs):
        p_refs, x_refs, o_refs = refs[:np_], refs[np_:np_ + nx], refs[np_ + nx:]
        outs = fn(*[r[...].astype(F32) for r in p_refs], *[r[...].astype(F32) for r in x_refs])
        for (idx, dt, _), o_ref in zip(stores, o_refs):
            o_ref[...] = outs[idx].astype(dt)

    res = pl.pallas_call(
        body, name=name,
        out_shape=tuple(jax.ShapeDtypeStruct((t, w), dt) for _, dt, w in stores),
        grid=(t // rows,),
        in_specs=[_full_spec(p) for p in params] + [_row_spec(x, rows) for x in xs],
        out_specs=tuple(pl.BlockSpec((rows, w), lambda i: (i, 0)) for _, _, w in stores),
        compiler_params=_cparams(("parallel",)),
    )(*params, *[_arr(x) for x in xs])
    return res


def rows_bwd(fn, params, xs, cts, dx_stores, *, name, rows=ROWS, concat=False):
    t = _arr(xs[0]).shape[0]
    np_, nx = len(params), len(xs)
    ct_idx = [i for i, c in enumerate(cts) if c is not None]
    ct_arrs = [cts[i] for i in ct_idx]
    nct = len(ct_arrs)
    nds = 1 if concat else len(dx_stores)
    widths = [_width(xs[xi]) for xi, _ in dx_stores]

    def body(*refs):
        p_refs = refs[:np_]
        x_refs = refs[np_:np_ + nx]
        c_refs = refs[np_ + nx:np_ + nx + nct]
        d_refs = refs[np_ + nx + nct:np_ + nx + nct + nds]
        g_refs = refs[np_ + nx + nct + nds:]
        pv = [r[...].astype(F32) for r in p_refs]
        xv = [r[...].astype(F32) for r in x_refs]
        outs, vjp = jax.vjp(lambda *a: tuple(fn(*a)), *pv, *xv)
        ct_full = [jnp.zeros_like(o) for o in outs]
        for i, r in zip(ct_idx, c_refs):
            ct_full[i] = r[...].astype(F32)
        grads = vjp(tuple(ct_full))
        if concat:
            off = 0
            for (xi, dt), wd in zip(dx_stores, widths):
                d_refs[0][:, off:off + wd] = grads[np_ + xi].astype(dt)
                off += wd
        else:
            for (xi, dt), d_ref in zip(dx_stores, d_refs):
                d_ref[...] = grads[np_ + xi].astype(dt)
        step = pl.program_id(0)
        for j, g_ref in enumerate(g_refs):
            @pl.when(step == 0)
            def _(g_ref=g_ref, j=j):
                g_ref[...] = grads[j]

            @pl.when(step > 0)
            def _(g_ref=g_ref, j=j):
                g_ref[...] += grads[j]

    dx_shapes = [(sum(widths), dx_stores[0][1])] if concat else [(wd, dt) for wd, (_, dt) in zip(widths, dx_stores)]
    out_shape = tuple(jax.ShapeDtypeStruct((t, wd), dt) for wd, dt in dx_shapes) + \
        tuple(jax.ShapeDtypeStruct(p.shape, F32) for p in params)
    out_specs = tuple(pl.BlockSpec((rows, wd), lambda i: (i, 0)) for wd, _ in dx_shapes) + \
        tuple(_full_spec(p) for p in params)
    return pl.pallas_call(
        body, name=name, out_shape=out_shape, grid=(t // rows,),
        in_specs=[_full_spec(p) for p in params] + [_row_spec(x, rows) for x in xs] + [_row_spec(c, rows) for c in ct_arrs],
        out_specs=out_specs,
        compiler_params=_cparams(("arbitrary",)),
    )(*params, *[_arr(x) for x in xs], *[_arr(c) for c in ct_arrs])


def _rms(x, g):
    return x * lax.rsqrt(jnp.mean(x * x, axis=-1, keepdims=True) + EPS) * g


def _sigmoid(x):
    return 1.0 / (1.0 + jnp.exp(-x))


def _silu(x):
    return x * _sigmoid(x)


def _gelu(x):
    return 0.5 * x * (1.0 + lax.erf(x * (1.0 / math.sqrt(2.0))))


def stage_norm(g, x):
    return x, _rms(x, g)


def stage_res_norm(g, x, y):
    xn = x + y
    return xn, _rms(xn, g)


def stage_res_bias_norm(b, g, x, y):
    xn = x + y + b
    return xn, _rms(xn, g)


def stage_glu(ba, bb, za, zb):
    return ((za + ba) * _sigmoid(zb + bb),)


def stage_ln_silu(dw_b, ln_g, ln_b, cv):
    z = cv + dw_b
    mu = jnp.mean(z, axis=-1, keepdims=True)
    zc = z - mu
    y = zc * lax.rsqrt(jnp.mean(zc * zc, axis=-1, keepdims=True) + EPS) * ln_g + ln_b
    return (_silu(y),)


CONV_ROWS = 128
CONV_COLS = 256
SUBLANES = 8


def _halo(k):
    return SUBLANES * ((k - 1 + SUBLANES - 1) // SUBLANES)


def _taps_by_roll(k):
    out = {}
    for s in range(k):
        out.setdefault(s % SUBLANES, []).append((s // SUBLANES, s))
    return out


def _shifted_down(win, k, r):
    halo, n = _halo(k), win.shape[0]
    segs = {}
    for b, lst in _taps_by_roll(k).items():
        rolled = win if b == 0 else pltpu.roll(win, b, axis=0)
        for a, s in lst:
            segs[s] = rolled[halo - SUBLANES * a: halo - SUBLANES * a + r]
    return segs


def _shifted_up(win, k, r):
    n = win.shape[0]
    segs = {}
    for b, lst in _taps_by_roll(k).items():
        rolled = win if b == 0 else pltpu.roll(win, n - b, axis=0)
        for a, s in lst:
            segs[s] = rolled[SUBLANES * a: SUBLANES * a + r]
    return segs


def _for_blocks(nblk, fn):
    fn(0, True, nblk == 1)
    if nblk > 2:
        def step(i, c):
            fn(i, False, False)
            return c
        lax.fori_loop(1, nblk - 1, step, 0)
    if nblk > 1:
        fn(nblk - 1, False, True)


def _base(i, r):
    return i * r if isinstance(i, int) else pl.multiple_of(i * r, r)


def _win_top(ref, i, first, r, halo):
    if first:
        return jnp.concatenate([jnp.zeros((halo, ref.shape[1]), F32), ref[pl.ds(0, r), :]], axis=0)
    base = _base(i, r)
    return ref[pl.ds(base - halo, r + halo), :]


def _win_bottom(ref, i, last, r, halo):
    base = _base(i, r)
    if last:
        return jnp.concatenate([ref[pl.ds(base, r), :], jnp.zeros((halo, ref.shape[1]), F32)], axis=0)
    return ref[pl.ds(base, r + halo), :]


def conv_fwd(x, w, *, mode, name, g=None):
    t = x.shape[0]
    k, c = w.shape
    r, cb, halo = min(CONV_ROWS, t), min(CONV_COLS, c), _halo(k)
    nblk = t // r

    def body(*refs):
        if mode == "silu_bwd":
            x_ref, w_ref, g_ref, o_ref = refs
        else:
            x_ref, w_ref, o_ref = refs

        def blk(i, first, last):
            segs = _shifted_down(_win_top(x_ref, i, first, r, halo), k, r)
            acc = None
            for s in range(k):
                term = w_ref[pl.ds(k - 1 - s, 1), :] * segs[s]
                acc = term if acc is None else acc + term
            base = _base(i, r)
            if mode == "silu":
                acc = _silu(acc)
            elif mode == "silu_bwd":
                sg = _sigmoid(acc)
                acc = g_ref[pl.ds(base, r), :] * (sg * (1.0 + acc * (1.0 - sg)))
            o_ref[pl.ds(base, r), :] = acc

        _for_blocks(nblk, blk)

    col = pl.BlockSpec((t, cb), lambda j: (0, j))
    in_specs = [col, pl.BlockSpec((k, cb), lambda j: (0, j))] + ([col] if mode == "silu_bwd" else [])
    args = [x, w] + ([g] if mode == "silu_bwd" else [])
    return pl.pallas_call(
        body, name=name, out_shape=jax.ShapeDtypeStruct((t, c), F32), grid=(c // cb,),
        in_specs=in_specs, out_specs=col, compiler_params=_cparams(("parallel",)),
    )(*args)


def conv_bwd(x, w, dy, *, name, into=None):
    t = x.shape[0]
    k, c = w.shape
    r, cb, halo = min(CONV_ROWS, t), min(CONV_COLS, c), _halo(k)
    nblk = t // r

    def body(x_ref, w_ref, dy_ref, *rest):
        dx_ref, dw_ref = rest[-2], rest[-1]
        dw_ref[...] = jnp.zeros_like(dw_ref)

        def blk(i, first, last):
            base = _base(i, r)
            up = _shifted_up(_win_bottom(dy_ref, i, last, r, halo), k, r)
            down = _shifted_down(_win_top(x_ref, i, first, r, halo), k, r)
            dyb = up[0]
            acc = None
            for s in range(k):
                term = w_ref[pl.ds(k - 1 - s, 1), :] * up[s]
                acc = term if acc is None else acc + term
                dw_ref[pl.ds(k - 1 - s, 1), :] += jnp.sum(down[s] * dyb, axis=0, keepdims=True)
            dx_ref[pl.ds(base, r), :] = acc.astype(dx_ref.dtype)

        _for_blocks(nblk, blk)

    col = pl.BlockSpec((t, cb), lambda j: (0, j))
    wsp = pl.BlockSpec((k, cb), lambda j: (0, j))
    dx_shape = jax.ShapeDtypeStruct((t, c), F32) if into is None else jax.ShapeDtypeStruct(into.shape, into.dtype)
    return pl.pallas_call(
        body, name=name,
        out_shape=(dx_shape, jax.ShapeDtypeStruct((k, c), F32)), grid=(c // cb,),
        in_specs=[col, wsp, col] + ([] if into is None else [pl.BlockSpec(memory_space=pl.ANY)]),
        out_specs=(col, wsp),
        input_output_aliases={} if into is None else {3: 0},
        compiler_params=_cparams(("parallel",)),
    )(*([x, w, dy] + ([] if into is None else [into])))


_DIMS = {"nn": (((1,), (0,)), ((), ())), "nt": (((1,), (1,)), ((), ())), "tn": (((0,), (0,)), ((), ()))}
_DIMS_BATCHED = {"nn": (((2,), (1,)), ((0,), (0,))), "nt": (((2,), (2,)), ((0,), (0,))), "tn": (((1,), (1,)), ((0,), (0,)))}


def _mxu(a, b, mode):
    dims = _DIMS_BATCHED if a.ndim == 3 else _DIMS
    return lax.dot_general(a, b, dims[mode], preferred_element_type=F32)


def _split(x):
    hi = x.astype(BF16)
    return hi, (x - hi.astype(F32)).astype(BF16)


def _dot_raw(a, b, mode, prec):
    if prec == "bf16":
        return _mxu(a.astype(BF16), b.astype(BF16), mode)
    if prec == "x3":
        ah, al = _split(a)
        bh, bl = _split(b)
        return _mxu(ah, bh, mode) + (_mxu(ah, bl, mode) + _mxu(al, bh, mode))
    if prec == "x3r":
        bh, bm = _split(b)
        bl = (b - bh.astype(F32) - bm.astype(F32)).astype(BF16)
        ah = a.astype(BF16)
        return _mxu(ah, bh, mode) + (_mxu(ah, bm, mode) + _mxu(ah, bl, mode))
    raise ValueError(prec)


@functools.lru_cache(maxsize=None)
def _dot_fn(mode, prec):
    bprec = "x3" if prec == "x3r" else prec

    @jax.custom_vjp
    def f(a, b):
        return _dot_raw(a, b, mode, prec)

    def fwd(a, b):
        return _dot_raw(a, b, mode, prec), (a, b)

    def bwd(res, ct):
        a, b = res
        if mode == "nn":
            return _dot_raw(ct, b, "nt", bprec), _dot_raw(a, ct, "tn", bprec)
        if mode == "nt":
            return _dot_raw(ct, b, "nn", bprec), _dot_raw(ct, a, "tn", bprec)
        return _dot_raw(b, ct, "nt", bprec), _dot_raw(a, ct, "nn", bprec)

    f.defvjp(fwd, bwd)
    return f


def _dot(a, b, mode="nn", prec="bf16"):
    return _dot_fn(mode, prec)(a, b)


def _inv_product(l):
    n = l.shape[-1]
    eye = (lax.broadcasted_iota(jnp.int32, (n, n), 0) == lax.broadcasted_iota(jnp.int32, (n, n), 1)).astype(F32)
    p = eye - l
    pw = l
    for _ in range(5):
        pw = _dot_raw(pw, pw, "nn", "x3")
        p = _dot_raw(p, eye + pw, "nn", "x3")
    return p


@jax.custom_vjp
def _inv_unit_lower(l, t_saved):
    return t_saved


def _inv_fwd(l, t_saved):
    return t_saved, t_saved


def _inv_bwd(t, ct):
    tmp = _dot_raw(t, ct, "tn", "x3")
    return -_dot_raw(tmp, t, "nt", "x3"), jnp.zeros_like(t)


_inv_unit_lower.defvjp(_inv_fwd, _inv_bwd)


def _softplus(x):
    pos = x > 0
    return jnp.where(pos, x, 0.0) + jnp.log(1.0 + jnp.exp(jnp.where(pos, -x, x)))


def _l2n(x):
    return x * lax.rsqrt(jnp.sum(x * x, axis=-1, keepdims=True) + EPS)


def delta_pair(s0, qc, kc, vc, z, pba, alog, dtb, onorm, t_saved=None):
    n = PAIR
    nh = qc.shape[0]
    assert qc.shape == (nh, n, HEAD_DIM) and n == HEAD_DIM
    hi = lax.broadcasted_iota(jnp.int32, (nh, 1, pba.shape[1]), 0)
    li = lax.broadcasted_iota(jnp.int32, (nh, 1, pba.shape[1]), 2)
    braw = jnp.sum(pba[None] * (li == hi).astype(F32), axis=2, keepdims=True)
    araw = jnp.sum(pba[None] * (li == hi + nh).astype(F32), axis=2, keepdims=True)
    ri = lax.broadcasted_iota(jnp.int32, (n, n), 0)
    ci = lax.broadcasted_iota(jnp.int32, (n, n), 1)
    same = (ri // CHUNK) == (ci // CHUNK)
    tri = same & (ci <= ri)
    same_f = jnp.broadcast_to(same.astype(F32), (nh, n, n))
    tri_f = jnp.broadcast_to(tri.astype(F32), (nh, n, n))
    strict_f = (same & (ci < ri)).astype(F32)
    m0 = (lax.broadcasted_iota(jnp.int32, (n, 1), 0) < CHUNK).astype(F32)
    m1 = 1.0 - m0

    q = _l2n(qc) * (HEAD_DIM ** -0.5)
    k = _l2n(kc)
    beta = _sigmoid(braw)
    g = -jnp.exp(alog) * _softplus(araw + dtb)
    gb = jnp.broadcast_to(g, (nh, n, n))
    gc = _dot(tri_f, gb, "nn", "x3r")
    gtot = _dot(same_f, gb, "nn", "x3r")
    decay = jnp.exp(jnp.where(tri, gc - jnp.swapaxes(gc, 1, 2), -1e30))
    eg = jnp.exp(gc)
    kb, vb = k * beta, vc * beta
    l = _dot(kb, k, "nt") * decay * strict_f
    if t_saved is None:
        tinv = _inv_product(l)
    else:
        tinv = _inv_unit_lower(l, t_saved)
    u = _dot(tinv, vb, "nn", "x3")
    w = _dot(tinv, kb * eg, "nn", "x3")
    attn = _dot(q, k, "nt") * decay
    q_dec = q * eg
    k_tail = k * jnp.exp(gtot - gc)
    gl0 = jnp.exp(jnp.sum(gb * m0, axis=1, keepdims=True))
    gl1 = jnp.exp(jnp.sum(gb * m1, axis=1, keepdims=True))

    vn0 = m0 * (u - _dot(w, s0))
    s1 = s0 * gl0 + _dot(k_tail, vn0, "tn")
    vn1 = m1 * (u - _dot(w, s1))
    o = m0 * _dot(q_dec, s0) + m1 * _dot(q_dec, s1) + _dot(attn, vn0 + vn1)
    s2 = s1 * gl1 + _dot(k_tail, vn1, "tn")

    on = o * lax.rsqrt(jnp.mean(o * o, axis=-1, keepdims=True) + EPS) * onorm
    return on * _silu(z), s2, tinv


def _hcols(i):
    return slice(i * HEAD_DIM, (i + 1) * HEAD_DIM)


def _heads(ref, first, count):
    return jnp.stack([ref[:, _hcols(first + i)] for i in range(count)])


def delta_fwd(qkv, pm, pba, alog, dtb, onorm, cat_width, *, name):
    t = qkv.shape[0]
    h = alog.shape[0]
    hd = h * HEAD_DIM
    npair = t // PAIR
    mat = pl.BlockSpec((h, None, PAIR, HEAD_DIM), lambda p: (0, p, 0, 0))
    par = pl.BlockSpec((h, 1, 1), lambda p: (0, 0, 0))

    def body(qkv_ref, z_ref, pba_ref, al_ref, dt_ref, on_ref, o_ref, st_ref, ti_ref, s_scr):
        @pl.when(pl.program_id(0) == 0)
        def _():
            s_scr[...] = jnp.zeros_like(s_scr)

        s0 = s_scr[...]
        st_ref[...] = s0
        out, s2, tinv = delta_pair(s0, _heads(qkv_ref, 0, h), _heads(qkv_ref, h, h), _heads(qkv_ref, 2 * h, h), _heads(z_ref, 0, h),
                                   pba_ref[...], al_ref[...], dt_ref[...], on_ref[...])
        for i in range(h):
            o_ref[:, _hcols(i)] = out[i].astype(o_ref.dtype)
        ti_ref[...] = tinv
        s_scr[...] = s2

    return pl.pallas_call(
        body, name=name, grid=(npair,),
        out_shape=(jax.ShapeDtypeStruct((t, cat_width), BF16),
                   jax.ShapeDtypeStruct((h, npair, PAIR, HEAD_DIM), F32),
                   jax.ShapeDtypeStruct((h, npair, PAIR, PAIR), F32)),
        in_specs=[pl.BlockSpec((PAIR, 3 * hd), lambda p: (p, 0)), pl.BlockSpec((PAIR, hd), lambda p: (p, 3)),
                  pl.BlockSpec((PAIR, pba.shape[1]), lambda p: (p, 0)), par, par,
                  pl.BlockSpec((1, HEAD_DIM), lambda p: (0, 0))],
        out_specs=(pl.BlockSpec((PAIR, hd), lambda p: (p, 0)), mat, mat),
        scratch_shapes=[pltpu.VMEM((h, HEAD_DIM, HEAD_DIM), F32)],
        compiler_params=_cparams(("arbitrary",)),
    )(qkv, pm, pba, alog, dtb, onorm)


def delta_bwd(qkv, pm, pba, alog, dtb, onorm, states, tinvs, dcat, *, name):
    t = qkv.shape[0]
    h = alog.shape[0]
    hd = h * HEAD_DIM
    npair = t // PAIR
    rev = lambda p: npair - 1 - p
    mat = pl.BlockSpec((h, None, PAIR, HEAD_DIM), lambda p: (0, rev(p), 0, 0))
    par = pl.BlockSpec((h, 1, 1), lambda p: (0, 0, 0))
    onsp = pl.BlockSpec((1, HEAD_DIM), lambda p: (0, 0))
    wide = pl.BlockSpec((PAIR, 3 * hd), lambda p: (rev(p), 0))
    zsp = pl.BlockSpec((PAIR, hd), lambda p: (rev(p), 3))
    bsp = pl.BlockSpec((PAIR, pba.shape[1]), lambda p: (rev(p), 0))

    def body(qkv_ref, z_ref, pba_ref, al_ref, dt_ref, on_ref, st_ref, ti_ref, dc_ref,
             dqkv_ref, dz_ref, dpba_ref, dal_ref, ddt_ref, don_ref, ds_scr):
        @pl.when(pl.program_id(0) == 0)
        def _():
            ds_scr[...] = jnp.zeros_like(ds_scr)
            dal_ref[...] = jnp.zeros_like(dal_ref)
            ddt_ref[...] = jnp.zeros_like(ddt_ref)
            don_ref[...] = jnp.zeros_like(don_ref)

        tsv = ti_ref[...]
        fn = lambda s0, qc, kc, vc, z, pb, al, dt, on: delta_pair(s0, qc, kc, vc, z, pb, al, dt, on, tsv)[:2]
        _, vjp = jax.vjp(fn, st_ref[...], _heads(qkv_ref, 0, h), _heads(qkv_ref, h, h), _heads(qkv_ref, 2 * h, h),
                         _heads(z_ref, 0, h), pba_ref[...], al_ref[...], dt_ref[...], on_ref[...])
        ds0, dq, dk, dv, dz, dpba, dal, ddt, don = vjp((_heads(dc_ref, 0, h), ds_scr[...]))
        ds_scr[...] = ds0
        for i in range(h):
            dqkv_ref[:, _hcols(i)] = dq[i]
            dqkv_ref[:, _hcols(h + i)] = dk[i]
            dqkv_ref[:, _hcols(2 * h + i)] = dv[i]
            dz_ref[:, _hcols(i)] = dz[i].astype(dz_ref.dtype)
        dal_ref[...] += dal
        ddt_ref[...] += ddt
        dpba_ref[...] = dpba.astype(dpba_ref.dtype)
        don_ref[...] += don

    return pl.pallas_call(
        body, name=name, grid=(npair,),
        out_shape=(jax.ShapeDtypeStruct((t, 3 * hd), F32), jax.ShapeDtypeStruct(pm.shape, BF16),
                   jax.ShapeDtypeStruct(pba.shape, BF16),
                   jax.ShapeDtypeStruct((h, 1, 1), F32), jax.ShapeDtypeStruct((h, 1, 1), F32),
                   jax.ShapeDtypeStruct((1, HEAD_DIM), F32)),
        in_specs=[wide, zsp, bsp, par, par, onsp, mat, mat, pl.BlockSpec((PAIR, hd), lambda p: (rev(p), 0))],
        out_specs=(wide, zsp, bsp, par, par, onsp),
        scratch_shapes=[pltpu.VMEM((h, HEAD_DIM, HEAD_DIM), F32)],
        compiler_params=_cparams(("arbitrary",)),
    )(qkv, pm, pba, alog, dtb, onorm, states, tinvs, dcat)


def gmlp_block(ln_g, ln_b, w, bcol, u_raw, v_raw):
    n = w.shape[-1]
    ri = lax.broadcasted_iota(jnp.int32, (n, n), 0)
    ci = lax.broadcasted_iota(jnp.int32, (n, n), 1)
    mask = ((ci // CHUNK) <= (ri // CHUNK)).astype(F32)
    vg = _gelu(v_raw)
    vc = vg - jnp.mean(vg, axis=-1, keepdims=True)
    vgn = vc * lax.rsqrt(jnp.mean(vc * vc, axis=-1, keepdims=True) + EPS) * ln_g + ln_b
    return _gelu(u_raw) * (_dot(w * mask, vgn) + bcol)


def gmlp_fwd(pm, ln_g, ln_b, w_s, bcol, cat, *, name):
    t = pm.shape[0]
    g = w_s.shape[0]
    gw = g * HEAD_DIM
    assert pm.shape[1] == 6 * gw and cat.shape[1] == 2 * gw

    def body(u_ref, v_ref, lg_ref, lb_ref, w_ref, b_ref, cat_in, o_ref):
        del cat_in
        out = gmlp_block(lg_ref[...], lb_ref[...], w_ref[...], b_ref[...], _heads(u_ref, 0, g), _heads(v_ref, 0, g))
        for i in range(g):
            o_ref[:, _hcols(i)] = out[i].astype(o_ref.dtype)

    full = lambda a: pl.BlockSpec(a.shape, lambda m: (0, 0, 0))
    return pl.pallas_call(
        body, name=name, grid=(t // HEAD_DIM,),
        out_shape=jax.ShapeDtypeStruct(cat.shape, cat.dtype),
        in_specs=[pl.BlockSpec((HEAD_DIM, gw), lambda m: (m, 4)), pl.BlockSpec((HEAD_DIM, gw), lambda m: (m, 5)),
                  full(ln_g), full(ln_b), full(w_s), full(bcol), pl.BlockSpec(memory_space=pl.ANY)],
        out_specs=pl.BlockSpec((HEAD_DIM, gw), lambda m: (m, 1)),
        input_output_aliases={6: 0},
        compiler_params=_cparams(("arbitrary",)),
    )(pm, pm, ln_g, ln_b, w_s, bcol, cat)


def gmlp_bwd(pm, ln_g, ln_b, w_s, bcol, dcat, dpm, *, name):
    t = pm.shape[0]
    g = w_s.shape[0]
    gw = g * HEAD_DIM
    assert pm.shape[1] == 6 * gw and dpm.shape == pm.shape

    def body(u_ref, v_ref, lg_ref, lb_ref, w_ref, b_ref, dc_ref, dpm_in, duv_ref, dlg_ref, dlb_ref, dw_ref, db_ref):
        del dpm_in
        first = pl.program_id(0) == 0
        _, vjp = jax.vjp(gmlp_block, lg_ref[...], lb_ref[...], w_ref[...], b_ref[...], _heads(u_ref, 0, g), _heads(v_ref, 0, g))
        dlg, dlb, dw, db, du, dv = vjp(_heads(dc_ref, 0, g))
        for i in range(g):
            duv_ref[:, _hcols(i)] = du[i].astype(duv_ref.dtype)
            duv_ref[:, _hcols(g + i)] = dv[i].astype(duv_ref.dtype)
        for ref, val in ((dlg_ref, dlg), (dlb_ref, dlb), (dw_ref, dw), (db_ref, db)):
            @pl.when(first)
            def _(ref=ref, val=val):
                ref[...] = val

            @pl.when(jnp.logical_not(first))
            def _(ref=ref, val=val):
                ref[...] += val

    full = lambda a: pl.BlockSpec(a.shape, lambda m: (0, 0, 0))
    return pl.pallas_call(
        body, name=name, grid=(t // HEAD_DIM,),
        out_shape=(jax.ShapeDtypeStruct(dpm.shape, dpm.dtype),
                   jax.ShapeDtypeStruct(ln_g.shape, F32), jax.ShapeDtypeStruct(ln_b.shape, F32),
                   jax.ShapeDtypeStruct(w_s.shape, F32), jax.ShapeDtypeStruct(bcol.shape, F32)),
        in_specs=[pl.BlockSpec((HEAD_DIM, gw), lambda m: (m, 4)), pl.BlockSpec((HEAD_DIM, gw), lambda m: (m, 5)),
                  full(ln_g), full(ln_b), full(w_s), full(bcol), pl.BlockSpec((HEAD_DIM, gw), lambda m: (m, 1)),
                  pl.BlockSpec(memory_space=pl.ANY)],
        out_specs=(pl.BlockSpec((HEAD_DIM, 2 * gw), lambda m: (m, 2)), full(ln_g), full(ln_b), full(w_s), full(bcol)),
        input_output_aliases={7: 0},
        compiler_params=_cparams(("arbitrary",)),
    )(pm, pm, ln_g, ln_b, w_s, bcol, dcat, dpm)


def loss_head(g, x, r, tgt, *, name, rows=ROWS):
    t, d = x.shape

    def body(g_ref, x_ref, r_ref, t_ref, l_ref, dx_ref, dxb_ref, dg_ref):
        y, vjp = jax.vjp(lambda gg, xx: _rms(xx, gg), g_ref[...], x_ref[...] + r_ref[...])
        e = y - t_ref[...]
        part = (0.5 / d) * jnp.sum(jnp.sum(e * e, axis=1, keepdims=True), axis=0, keepdims=True)
        dg, dx = vjp(e * (1.0 / d))
        dx_ref[...] = dx
        dxb_ref[...] = dx.astype(BF16)
        first = pl.program_id(0) == 0

        @pl.when(first)
        def _():
            l_ref[...] = part
            dg_ref[...] = dg

        @pl.when(jnp.logical_not(first))
        def _():
            l_ref[...] += part
            dg_ref[...] += dg

    rs = pl.BlockSpec((rows, d), lambda i: (i, 0))
    gs = pl.BlockSpec((1, d), lambda i: (0, 0))
    return pl.pallas_call(
        body, name=name, grid=(t // rows,),
        out_shape=(jax.ShapeDtypeStruct((1, 1), F32), jax.ShapeDtypeStruct((t, d), F32),
                   jax.ShapeDtypeStruct((t, d), BF16), jax.ShapeDtypeStruct((1, d), F32)),
        in_specs=[gs, rs, rs, rs],
        out_specs=(pl.BlockSpec((1, 1), lambda i: (0, 0)), rs, rs, gs),
        compiler_params=_cparams(("arbitrary",)),
    )(g, x, r, tgt)


def adamw(w, gs, m, v, *, name):
    nl, r, c = w.shape
    assert len(gs) == nl
    if r % SUBLANES == 0:
        tr, tc = _pick(r, (256, 128, 64, 32, 16, 8)), c
    else:
        tr, tc = r, _pick(c, (256, 128))
    k1 = 1.0 - ADAM_B1 ** ADAM_STEP
    k2 = 1.0 - ADAM_B2 ** ADAM_STEP

    def body(*refs):
        w_ref, m_ref, v_ref = refs[0], refs[1], refs[2]
        g_refs = refs[3:3 + nl]
        go_ref, d_ref, mo_ref, vo_ref = refs[3 + nl:]
        gg = g_refs[0][...]
        for li in range(1, nl):
            gg = jnp.where(pl.program_id(0) == li, g_refs[li][...], gg)
        mn = ADAM_B1 * m_ref[...] + (1.0 - ADAM_B1) * gg
        vn = ADAM_B2 * v_ref[...] + (1.0 - ADAM_B2) * (gg * gg)
        go_ref[...] = gg
        d_ref[...] = -ADAM_LR * ((mn / k1) / (jnp.sqrt(vn / k2) + ADAM_EPS) + ADAM_WD * w_ref[...])
        mo_ref[...] = mn
        vo_ref[...] = vn

    if tc == c:
        sp = pl.BlockSpec((None, tr, c), lambda l, i: (l, i, 0))
        gsp = pl.BlockSpec((tr, c), lambda l, i: (i, 0))
    else:
        sp = pl.BlockSpec((None, r, tc), lambda l, i: (l, 0, i))
        gsp = pl.BlockSpec((r, tc), lambda l, i: (0, i))
    sds = jax.ShapeDtypeStruct((nl, r, c), F32)
    return pl.pallas_call(
        body, name=name, grid=(nl, (r // tr) * (c // tc)), out_shape=(sds, sds, sds, sds),
        in_specs=[sp, sp, sp] + [gsp] * nl, out_specs=(sp, sp, sp, sp),
        compiler_params=_cparams(("parallel", "parallel")),
    )(w, m, v, *gs)


MESH = pl.DeviceIdType.MESH
ANY = pl.BlockSpec(memory_space=pl.ANY)


def _me():
    return lax.axis_index("x"), lax.axis_index("y"), lax.axis_index("c")


def _other_chips(x, y):
    return [(1 - x, y), (x, 1 - y), (1 - x, 1 - y)]


def gather_shards(arrs, *, name):
    n = len(arrs)
    per = 7

    def body(*refs):
        ins, outs = refs[:n], refs[n:2 * n]
        send_sems, recv_sems = refs[2 * n:]
        x, y, c = _me()
        chip = 2 * x + y
        sib = (x, y, 1 - c)
        chips = _other_chips(x, y)

        def rc(i, k, src, dst, to):
            return pltpu.make_async_remote_copy(src_ref=src, dst_ref=dst, send_sem=send_sems.at[per * i + k],
                                                recv_sem=recv_sems.at[per * i + k], device_id=to, device_id_type=MESH)

        def half(i, which):
            hrows = arrs[i].shape[0] // 2
            return pl.ds(which * hrows, hrows)

        own = [rc(i, 6, ins[i], outs[i].at[chip], sib) for i in range(n)]
        for cp in own:
            cp.start()
        started = []
        for i in range(n):
            for j, ch in enumerate(chips):
                cp = rc(i, j, ins[i].at[half(i, c)], outs[i].at[chip, half(i, c)], (ch[0], ch[1], c))
                cp.start()
                started.append(cp)
        for i in range(n):
            for j, ch in enumerate(chips):
                sc = 2 * ch[0] + ch[1]
                landed = outs[i].at[sc, half(i, c)]
                rc(i, j, ins[i].at[half(i, c)], landed, (ch[0], ch[1], c)).wait_recv()
                fw = rc(i, 3 + j, landed, landed, sib)
                fw.start()
                started.append(fw)
        for i in range(n):
            for j, ch in enumerate(chips):
                sc = 2 * ch[0] + ch[1]
                theirs = outs[i].at[sc, half(i, 1 - c)]
                rc(i, 3 + j, theirs, theirs, sib).wait_recv()
        for cp in own:
            cp.wait_recv()
        for cp in started + own:
            cp.wait_send()

    return pl.pallas_call(
        body, name=name,
        out_shape=tuple(jax.ShapeDtypeStruct((N_CHIPS,) + a.shape, a.dtype) for a in arrs),
        in_specs=[ANY] * n, out_specs=tuple([ANY] * n),
        scratch_shapes=[pltpu.SemaphoreType.DMA((per * n,)), pltpu.SemaphoreType.DMA((per * n,))],
        compiler_params=pltpu.CompilerParams(has_side_effects=True),
    )(*arrs)


HBM = pl.BlockSpec(memory_space=pltpu.HBM)
SEM = pl.BlockSpec(memory_space=pltpu.SEMAPHORE)
EFFECT = pltpu.SideEffectType.DATAFLOW_SIDE_EFFECTING
def _hbm(a):
    return pltpu.with_memory_space_constraint(a, pltpu.HBM)


def _half_rows(arr, which):
    hrows = arr.shape[-2] // 2
    return pl.ds(which * hrows, hrows)


def _ici_copies(srcs, lands, sems):
    x, y, c = _me()
    chip = 2 * x + y
    return [pltpu.make_async_remote_copy(src_ref=srcs[i].at[_half_rows(srcs[i], c)], dst_ref=lands[i].at[chip, _half_rows(srcs[i], c)],
                                         send_sem=sems[0].at[3 * i + j], recv_sem=sems[1].at[3 * i + j],
                                         device_id=(ch[0], ch[1], c), device_id_type=MESH)
            for i in range(len(srcs)) for j, ch in enumerate(_other_chips(x, y))]


def _own_copies(srcs, lands, sems):
    x, y, c = _me()
    return [pltpu.make_async_remote_copy(src_ref=srcs[i], dst_ref=lands[i].at[2 * x + y], send_sem=sems[0].at[i],
                                         recv_sem=sems[1].at[i], device_id=(x, y, 1 - c), device_id_type=MESH)
            for i in range(len(srcs))]


def _fwd_copies(srcs, lands, sems, half):
    x, y, c = _me()
    cps = []
    for i in range(len(srcs)):
        for j, ch in enumerate(_other_chips(x, y)):
            part = lands[i].at[2 * ch[0] + ch[1], _half_rows(srcs[i], half)]
            cps.append(pltpu.make_async_remote_copy(src_ref=part, dst_ref=part, send_sem=sems[0].at[3 * i + j],
                                                    recv_sem=sems[1].at[3 * i + j], device_id=(x, y, 1 - c), device_id_type=MESH))
    return cps


def gather_start(groups, *, name):
    arrs = [a for g in groups for a in g]
    n, ng = len(arrs), len(groups)
    bounds = np.cumsum([0] + [len(g) for g in groups])

    def body(*refs):
        srcs, lands = refs[:n], refs[n:2 * n]
        sems = refs[2 * n:2 * n + 4 * ng]
        token = refs[-1]
        for gi in range(ng):
            lo, hi = bounds[gi], bounds[gi + 1]
            for cp in _ici_copies(srcs[lo:hi], lands[lo:hi], sems[4 * gi:4 * gi + 2]):
                cp.start()
        for gi in range(ng):
            lo, hi = bounds[gi], bounds[gi + 1]
            for cp in _own_copies(srcs[lo:hi], lands[lo:hi], sems[4 * gi + 2:4 * gi + 4]):
                cp.start()
        token[...] = jnp.zeros_like(token)

    sem_shapes = []
    for g in groups:
        sem_shapes += [pltpu.SemaphoreType.DMA((3 * len(g),))] * 2 + [pltpu.SemaphoreType.DMA((len(g),))] * 2
    res = pl.pallas_call(
        body, name=name,
        out_shape=tuple(sem_shapes) + tuple(pltpu.HBM(a.shape, a.dtype) for a in arrs)
        + tuple(pltpu.HBM((N_CHIPS,) + a.shape, a.dtype) for a in arrs) + (jax.ShapeDtypeStruct((SUBLANES, LANES), F32),),
        in_specs=[HBM] * (2 * n),
        out_specs=tuple([SEM] * (4 * ng)) + tuple([HBM] * (2 * n)) + (pl.BlockSpec(memory_space=pltpu.VMEM),),
        input_output_aliases={i: 4 * ng + i for i in range(2 * n)},
        compiler_params=pltpu.CompilerParams(has_side_effects=EFFECT),
    )(*[_hbm(a) for a in arrs], *[_hbm(lax.empty((N_CHIPS,) + a.shape, a.dtype)) for a in arrs])
    sems, thru, lands, token = res[:4 * ng], res[4 * ng:4 * ng + n], res[4 * ng + n:4 * ng + 2 * n], res[-1]
    out = [{"ici": (sems[4 * gi], sems[4 * gi + 1]), "own": (sems[4 * gi + 2], sems[4 * gi + 3]),
            "thru": list(thru[bounds[gi]:bounds[gi + 1]]), "lands": list(lands[bounds[gi]:bounds[gi + 1]])} for gi in range(ng)]
    return out, token


def gather_forward(group, after, *, name):
    thru, lands = group["thru"], group["lands"]
    n = len(thru)

    def body(*refs):
        srcs, lands_r = refs[:n], refs[n:2 * n]
        ici = refs[2 * n:2 * n + 2]
        fwd = refs[2 * n + 3 + 2 * n:2 * n + 3 + 2 * n + 2]
        for cp in _ici_copies(srcs, lands_r, ici):
            cp.wait_send()
            cp.wait_recv()
        for cp in _fwd_copies(srcs, lands_r, fwd, _me()[2]):
            cp.start()

    res = pl.pallas_call(
        body, name=name,
        out_shape=tuple(pltpu.HBM(a.shape, a.dtype) for a in thru) + tuple(pltpu.HBM(a.shape, a.dtype) for a in lands)
        + (pltpu.SemaphoreType.DMA((3 * n,)),) * 2,
        in_specs=[HBM] * (2 * n) + [SEM, SEM, pl.BlockSpec(memory_space=pl.ANY)],
        out_specs=tuple([HBM] * (2 * n)) + (SEM, SEM),
        input_output_aliases={i: i for i in range(2 * n)},
        compiler_params=pltpu.CompilerParams(has_side_effects=EFFECT),
    )(*thru, *lands, *group["ici"], after)
    return {"own": group["own"], "fwd": (res[2 * n], res[2 * n + 1]), "thru": list(res[:n]), "lands": list(res[n:2 * n])}


def gather_wait(group, after, *, name):
    thru, lands = group["thru"], group["lands"]
    n = len(thru)

    def body(*refs):
        srcs, lands_r = refs[:n], refs[n:2 * n]
        own, fwd = refs[2 * n:2 * n + 2], refs[2 * n + 2:2 * n + 4]
        c = _me()[2]
        for mine, theirs in zip(_fwd_copies(srcs, lands_r, fwd, c), _fwd_copies(srcs, lands_r, fwd, 1 - c)):
            mine.wait_send()
            theirs.wait_recv()
        for cp in _own_copies(srcs, lands_r, own):
            cp.wait_send()
            cp.wait_recv()

    res = pl.pallas_call(
        body, name=name,
        out_shape=tuple(pltpu.HBM(a.shape, a.dtype) for a in thru) + tuple(pltpu.HBM(a.shape, a.dtype) for a in lands),
        in_specs=[HBM] * (2 * n) + [SEM] * 4 + [pl.BlockSpec(memory_space=pl.ANY)],
        out_specs=tuple([HBM] * (2 * n)),
        input_output_aliases={i: i for i in range(2 * n)},
        compiler_params=pltpu.CompilerParams(has_side_effects=EFFECT),
    )(*thru, *lands, *group["own"], *group["fwd"], after)
    return list(res[n:])


def _pair_copies(srcs, lands, sems):
    x, y, c = _me()
    return [pltpu.make_async_remote_copy(src_ref=srcs[i].at[:, _half_rows(srcs[i], 1 - c), :], dst_ref=lands[i], send_sem=sems[0].at[i],
                                         recv_sem=sems[1].at[i], device_id=(x, y, 1 - c), device_id_type=MESH)
            for i in range(len(srcs))]


def _scatter_copies(srcs, lands, sems):
    x, y, c = _me()
    return [pltpu.make_async_remote_copy(src_ref=srcs[i].at[2 * ch[0] + ch[1]], dst_ref=lands[i].at[j], send_sem=sems[0].at[3 * i + j],
                                         recv_sem=sems[1].at[3 * i + j], device_id=(ch[0], ch[1], c), device_id_type=MESH)
            for i in range(len(srcs)) for j, ch in enumerate(_other_chips(x, y))]


def split_start(copies, arrs, land_shapes, nsem, *, name):
    n = len(arrs)

    def body(*refs):
        for cp in copies(refs[:n], refs[n:2 * n], refs[2 * n:2 * n + 2]):
            cp.start()
        refs[-1][...] = jnp.zeros_like(refs[-1])

    res = pl.pallas_call(
        body, name=name,
        out_shape=(pltpu.SemaphoreType.DMA((nsem,)),) * 2 + tuple(pltpu.HBM(a.shape, a.dtype) for a in arrs)
        + tuple(pltpu.HBM(s, a.dtype) for s, a in zip(land_shapes, arrs)) + (jax.ShapeDtypeStruct((SUBLANES, LANES), F32),),
        in_specs=[HBM] * (2 * n), out_specs=(SEM, SEM) + tuple([HBM] * (2 * n)) + (pl.BlockSpec(memory_space=pltpu.VMEM),),
        input_output_aliases={i: 2 + i for i in range(2 * n)},
        compiler_params=pltpu.CompilerParams(has_side_effects=EFFECT),
    )(*[_hbm(a) for a in arrs], *[_hbm(lax.empty(s, a.dtype)) for s, a in zip(land_shapes, arrs)])
    return {"copies": copies, "sems": (res[0], res[1]), "thru": list(res[2:2 + n]), "lands": list(res[2 + n:2 + 2 * n]),
            "token": res[-1]}


def split_wait(state, after, *, name):
    thru, lands, copies = state["thru"], state["lands"], state["copies"]
    n = len(thru)

    def body(*refs):
        for cp in copies(refs[:n], refs[n:2 * n], refs[2 * n:2 * n + 2]):
            cp.wait_send()
            cp.wait_recv()

    res = pl.pallas_call(
        body, name=name,
        out_shape=tuple(pltpu.HBM(a.shape, a.dtype) for a in thru) + tuple(pltpu.HBM(a.shape, a.dtype) for a in lands),
        in_specs=[HBM] * (2 * n) + [SEM, SEM, pl.BlockSpec(memory_space=pl.ANY)], out_specs=tuple([HBM] * (2 * n)),
        input_output_aliases={i: i for i in range(2 * n)},
        compiler_params=pltpu.CompilerParams(has_side_effects=EFFECT),
    )(*thru, *lands, *state["sems"], after)
    return list(res[:n]), list(res[n:])


def sibling_send_other_half(arrs, *, name):
    n = len(arrs)

    def body(*refs):
        ins, outs = refs[:n], refs[n:2 * n]
        send_sems, recv_sems = refs[2 * n:]
        x, y, c = _me()
        cps = []
        for i in range(n):
            hrows = arrs[i].shape[1] // 2
            cps.append(pltpu.make_async_remote_copy(
                src_ref=ins[i].at[:, pl.ds((1 - c) * hrows, hrows), :], dst_ref=outs[i], send_sem=send_sems.at[i],
                recv_sem=recv_sems.at[i], device_id=(x, y, 1 - c), device_id_type=MESH))
        for cp in cps:
            cp.start()
        for cp in cps:
            cp.wait_recv()
        for cp in cps:
            cp.wait_send()

    return pl.pallas_call(
        body, name=name,
        out_shape=tuple(jax.ShapeDtypeStruct((a.shape[0], a.shape[1] // 2, a.shape[2]), a.dtype) for a in arrs),
        in_specs=[ANY] * n, out_specs=tuple([ANY] * n),
        scratch_shapes=[pltpu.SemaphoreType.DMA((n,)), pltpu.SemaphoreType.DMA((n,))],
        compiler_params=pltpu.CompilerParams(has_side_effects=True),
    )(*arrs)


def chip_scatter(arrs, *, name):
    n = len(arrs)

    def body(*refs):
        ins, outs = refs[:n], refs[n:2 * n]
        send_sems, recv_sems = refs[2 * n:]
        x, y, c = _me()
        cps = []
        for i in range(n):
            for j, ch in enumerate(_other_chips(x, y)):
                cps.append(pltpu.make_async_remote_copy(
                    src_ref=ins[i].at[2 * ch[0] + ch[1]], dst_ref=outs[i].at[j], send_sem=send_sems.at[3 * i + j],
                    recv_sem=recv_sems.at[3 * i + j], device_id=(ch[0], ch[1], c), device_id_type=MESH))
        for cp in cps:
            cp.start()
        for cp in cps:
            cp.wait_recv()
        for cp in cps:
            cp.wait_send()

    return pl.pallas_call(
        body, name=name,
        out_shape=tuple(jax.ShapeDtypeStruct((3,) + a.shape[1:], a.dtype) for a in arrs),
        in_specs=[ANY] * n, out_specs=tuple([ANY] * n),
        scratch_shapes=[pltpu.SemaphoreType.DMA((3 * n,)), pltpu.SemaphoreType.DMA((3 * n,))],
        compiler_params=pltpu.CompilerParams(has_side_effects=True),
    )(*arrs)


def sibling_join_halves(arrs, *, name):
    n = len(arrs)

    def body(*refs):
        outs = refs[n:2 * n]
        send_sems, recv_sems = refs[2 * n:]
        x, y, c = _me()
        cps = []
        for i in range(n):
            hrows = arrs[i].shape[0] // 2
            mine = outs[i].at[pl.ds(c * hrows, hrows)]
            cps.append(pltpu.make_async_remote_copy(src_ref=mine, dst_ref=mine, send_sem=send_sems.at[i],
                                                    recv_sem=recv_sems.at[i], device_id=(x, y, 1 - c), device_id_type=MESH))
        for cp in cps:
            cp.start()
        for i in range(n):
            hrows = arrs[i].shape[0] // 2
            theirs = outs[i].at[pl.ds((1 - c) * hrows, hrows)]
            pltpu.make_async_remote_copy(src_ref=theirs, dst_ref=theirs, send_sem=send_sems.at[i], recv_sem=recv_sems.at[i],
                                         device_id=(x, y, 1 - c), device_id_type=MESH).wait_recv()
        for cp in cps:
            cp.wait_send()

    return pl.pallas_call(
        body, name=name,
        out_shape=tuple(jax.ShapeDtypeStruct(a.shape, a.dtype) for a in arrs),
        in_specs=[ANY] * n, out_specs=tuple([ANY] * n),
        input_output_aliases={i: i for i in range(n)},
        scratch_shapes=[pltpu.SemaphoreType.DMA((n,)), pltpu.SemaphoreType.DMA((n,))],
        compiler_params=pltpu.CompilerParams(has_side_effects=True),
    )(*arrs)


def pre_reduce(gd, sib, who, *, name):
    _, r, c = gd.shape
    h = r // 2
    tr = _pick(h, (256, 128, 64, 32, 16, 8))
    nrb = h // tr

    def body(who_ref, gd_ref, sib_ref, pb_ref, own_ref):
        p = gd_ref[...] + sib_ref[...]
        pb_ref[...] = p.astype(BF16)

        @pl.when(pl.program_id(1) == who_ref[1])
        def _():
            own_ref[...] = p

    return pl.pallas_call(
        body, name=name,
        out_shape=(jax.ShapeDtypeStruct((N_CHIPS, h, c), BF16), jax.ShapeDtypeStruct((h, c), F32)),
        grid_spec=pltpu.PrefetchScalarGridSpec(
            num_scalar_prefetch=1, grid=(nrb, N_CHIPS),
            in_specs=[pl.BlockSpec((None, tr, c), lambda i, s, w: (s, w[0] * nrb + i, 0)),
                      pl.BlockSpec((None, tr, c), lambda i, s, w: (s, i, 0))],
            out_specs=(pl.BlockSpec((None, tr, c), lambda i, s, w: (s, i, 0)),
                       pl.BlockSpec((tr, c), lambda i, s, w: (i, 0)))),
        compiler_params=_cparams(("parallel", "arbitrary")),
    )(who, gd, sib)


def final_reduce(own, rcv, who, *, name):
    h, c = own.shape
    tr = _pick(h, (256, 128, 64, 32, 16, 8))
    nrb = h // tr

    def body(who_ref, own_ref, rcv_ref, o_ref):
        del who_ref
        acc = own_ref[...]
        for j in range(3):
            acc = acc + rcv_ref[j].astype(F32)
        o_ref[...] = acc

    return pl.pallas_call(
        body, name=name, out_shape=jax.ShapeDtypeStruct((2 * h, c), F32),
        grid_spec=pltpu.PrefetchScalarGridSpec(
            num_scalar_prefetch=1, grid=(nrb,),
            in_specs=[pl.BlockSpec((tr, c), lambda i, w: (i, 0)), pl.BlockSpec((3, tr, c), lambda i, w: (0, i, 0))],
            out_specs=pl.BlockSpec((tr, c), lambda i, w: (w[0] * nrb + i, 0))),
        compiler_params=_cparams(("parallel",)),
    )(who, own, rcv)


def sum8(a, *, name):
    _, r, c = a.shape
    tr = _pick(r, (256, 128, 64, 32, 16, 8))

    def body(a_ref, o_ref):
        acc = a_ref[0]
        for j in range(1, 8):
            acc = acc + a_ref[j]
        o_ref[...] = acc

    return pl.pallas_call(
        body, name=name, out_shape=jax.ShapeDtypeStruct((r, c), F32), grid=(r // tr,),
        in_specs=[pl.BlockSpec((8, tr, c), lambda i: (0, i, 0))],
        out_specs=pl.BlockSpec((tr, c), lambda i: (i, 0)),
        compiler_params=_cparams(("parallel",)),
    )(a)


def allgather8(a, *, name):
    def body(in_ref, out_ref, send_sems, recv_sems, local_sem):
        x, y, c = _me()
        me = 4 * x + 2 * y + c
        loc = pltpu.make_async_copy(in_ref, out_ref.at[me], local_sem)
        loc.start()
        cps = []
        for k in range(1, 8):
            fx, fy, fc = (k >> 2) & 1, (k >> 1) & 1, k & 1
            to = (x + fx - 2 * x * fx, y + fy - 2 * y * fy, c + fc - 2 * c * fc)
            cps.append(pltpu.make_async_remote_copy(src_ref=in_ref, dst_ref=out_ref.at[me], send_sem=send_sems.at[k - 1],
                                                    recv_sem=recv_sems.at[k - 1], device_id=to, device_id_type=MESH))
        for cp in cps:
            cp.start()
        for k in range(1, 8):
            fx, fy, fc = (k >> 2) & 1, (k >> 1) & 1, k & 1
            src = 4 * (x + fx - 2 * x * fx) + 2 * (y + fy - 2 * y * fy) + (c + fc - 2 * c * fc)
            pltpu.make_async_remote_copy(src_ref=in_ref, dst_ref=out_ref.at[src], send_sem=send_sems.at[k - 1],
                                         recv_sem=recv_sems.at[k - 1], device_id=(x, y, c), device_id_type=MESH).wait_recv()
        for cp in cps:
            cp.wait_send()
        loc.wait()

    return pl.pallas_call(
        body, name=name, out_shape=jax.ShapeDtypeStruct((8,) + a.shape, a.dtype),
        in_specs=[ANY], out_specs=ANY,
        scratch_shapes=[pltpu.SemaphoreType.DMA((7,)), pltpu.SemaphoreType.DMA((7,)), pltpu.SemaphoreType.DMA],
        compiler_params=pltpu.CompilerParams(has_side_effects=True),
    )(a)


def _behind(p, *tokens):
    for tk in tokens:
        if tk is not None:
            p = p + tk[0, 0]
    return p


def local_step(x, tgt, wb, ws, arrive=lambda layer, event, after: {}, reduce=lambda group, event, gb, after: None):
    t, d = x.shape
    nh = ws["alog"].shape[0]
    ng = ws["w_s"].shape[0]
    assert nh == ng
    mix_w = (nh + ng) * HEAD_DIM

    (h0,) = rows_fwd(stage_norm, [ws["e_norm"]], [x], [(1, BF16, d)], name="f_norm_e")
    pm = mm_nt(h0, wb["w_main_t"], name="f_proj_main")
    pba = mm_nt(h0, wb["w_ba_t"], name="f_proj_ba")
    qkv = conv_fwd(pm, ws["conv_w"], mode="silu", name="f_conv4")
    arrive("layer0", "landed", qkv)
    cat, states, tinvs = delta_fwd(qkv, pm, pba, ws["alog"], ws["dtb"], ws["onorm"], mix_w, name="f_delta")
    cat = gmlp_fwd(pm, ws["lng"], ws["lnb"], ws["w_s"], ws["bs"], cat, name="f_gmlp")
    wb = {**wb, **arrive("layer0", "joined", cat)}
    y0 = mm_nn(cat, wb["w_out"], name="f_out")
    x1, h1 = rows_fwd(stage_res_norm, [ws["f_norm0"]], [x, y0], [(0, F32, d), (1, BF16, d)], name="f_norm_f0")
    a1, s1 = mm_nn(h1, wb["w1_0"], name="f_mlp0_up", epilogue="relu2")
    arrive("layer1", "landed", s1)
    y1 = mm_nn(s1, wb["w2_0"], name="f_mlp0_down")
    x2, h2 = rows_fwd(stage_res_norm, [ws["o_norm"]], [x1, y1], [(0, F32, d), (1, BF16, d)], name="f_norm_o")
    wb = {**wb, **arrive("layer1", "joined", h2)}
    zz = mm_nn(h2, wb["pw1"], name="f_pw1")
    zparts = [(zz, d, 0), (zz, d, 1)]
    (gl,) = rows_fwd(stage_glu, [ws["b1a"], ws["b1b"]], zparts, [(0, F32, d)], name="f_glu")
    cv = conv_fwd(gl, ws["dw"], mode="plain", name="f_conv31")
    ln_params = [ws["dw_b"], ws["ln_g"], ws["ln_b"]]
    (sl,) = rows_fwd(stage_ln_silu, ln_params, [cv], [(0, BF16, d)], name="f_ln_silu")
    y2 = mm_nn(sl, wb["pw2"], name="f_pw2")
    x3, h3 = rows_fwd(stage_res_bias_norm, [ws["b2"], ws["f_norm1"]], [x2, y2], [(0, F32, d), (1, BF16, d)], name="f_norm_f1")
    a3, s3 = mm_nn(h3, wb["w1_1"], name="f_mlp1_up", epilogue="relu2")
    y3 = mm_nn(s3, wb["w2_1"], name="f_mlp1_down")
    loss, d4, d4b, g_final = loss_head(ws["final_norm"], x3, y3, tgt, name="loss_head")

    gb, gs = {}, {"final_norm": g_final}
    s_up = wb["w1_0"].shape[0]

    gb["w2_1"] = mm_tn(s3, d4b, groups=1, name="b_mlp1_down_w")
    dpre3 = mm_nt(d4b, wb["w2_1"], name="b_mlp1_down_x", mul=a3, out_dtype=BF16)
    gb["w1_1"] = mm_tn(h3, dpre3, groups=s_up, name="b_mlp1_up_w")
    tok = reduce("mlp1", "ready", gb, None)
    dh3 = mm_nt(dpre3, wb["w1_1"], name="b_mlp1_up_x")
    d3, d3b, gs["b2"], gs["f_norm1"] = rows_bwd(stage_res_bias_norm, [ws["b2"], _behind(ws["f_norm1"], tok)], [x2, y2], [d4, dh3],
                                                [(0, F32), (1, BF16)], name="b_norm_f1")
    tok = reduce("mlp1", "paired", gb, d3)
    gb["pw2"] = mm_tn(sl, d3b, groups=1, name="b_pw2_w")
    dsl = mm_nt(d3b, wb["pw2"], name="b_pw2_x")
    dcv, gs["dw_b"], gs["ln_g"], gs["ln_b"] = rows_bwd(stage_ln_silu, [_behind(ln_params[0], tok)] + ln_params[1:], [cv], [dsl],
                                                       [(0, F32)], name="b_ln_silu")
    dgl, gs["dw"] = conv_bwd(gl, ws["dw"], dcv, name="b_conv31")
    dzz, gs["b1a"], gs["b1b"] = rows_bwd(stage_glu, [ws["b1a"], ws["b1b"]], zparts, [dgl], [(0, BF16), (1, BF16)],
                                         name="b_glu", concat=True)
    gb["pw1"] = mm_tn(h2, dzz, groups=wb["pw1"].shape[0], name="b_pw1_w")
    tok = reduce("conv", "ready", gb, None)
    dh2 = mm_nt(dzz, wb["pw1"], name="b_pw1_x")
    d2, d2b, gs["o_norm"] = rows_bwd(stage_res_norm, [_behind(ws["o_norm"], tok)], [x1, y1], [d3, dh2], [(0, F32), (1, BF16)],
                                     name="b_norm_o")
    tok_conv = reduce("conv", "paired", gb, d2)
    gb["w2_0"] = mm_tn(s1, d2b, groups=1, name="b_mlp0_down_w")
    dpre1 = mm_nt(d2b, wb["w2_0"], name="b_mlp0_down_x", mul=a1, out_dtype=BF16)
    gb["w1_0"] = mm_tn(h1, dpre1, groups=s_up, name="b_mlp0_up_w")
    tok = reduce("mlp0", "ready", gb, None)
    dh1 = mm_nt(dpre1, wb["w1_0"], name="b_mlp0_up_x")
    d1, d1b, gs["f_norm0"] = rows_bwd(stage_res_norm, [_behind(ws["f_norm0"], tok_conv, tok)], [x, y0], [d2, dh1],
                                      [(0, F32), (1, BF16)], name="b_norm_f0")
    tok = reduce("mlp0", "paired", gb, d1)
    gb["w_out"] = mm_tn(cat, d1b, groups=1, name="b_out_w")
    dcat = mm_nt(d1b, wb["w_out"], name="b_out_x")
    dqkv_c, dpm, dpba, gs["alog"], gs["dtb"], gs["onorm"] = delta_bwd(
        qkv, pm, pba, ws["alog"], ws["dtb"], _behind(ws["onorm"], tok), states, tinvs, dcat, name="b_delta")
    dpm, gs["lng"], gs["lnb"], gs["w_s"], gs["bs"] = gmlp_bwd(
        pm, ws["lng"], ws["lnb"], ws["w_s"], ws["bs"], dcat, dpm, name="b_gmlp")
    dconv = conv_fwd(pm, ws["conv_w"], mode="silu_bwd", g=dqkv_c, name="b_conv4_silu")
    dpm, gs["conv_w"] = conv_bwd(pm, ws["conv_w"], dconv, name="b_conv4", into=dpm)
    gb["w_main_t"] = mm_tn(dpm, h0, groups=1, name="b_proj_main_w")
    gb["w_ba_t"] = mm_tn(dpba, h0, groups=1, name="b_proj_ba_w")
    dh0 = mm_nn(dpm, wb["w_main_t"], name="b_proj_main_x")
    dh0 = mm_nn(dpba, wb["w_ba_t"], name="b_proj_ba_x", add=dh0)
    grad_x, gs["e_norm"] = rows_bwd(stage_norm, [ws["e_norm"]], [x], [d1, dh0], [(0, F32)], name="b_norm_e")
    return loss, grad_x, gb, gs


WEIGHTS = ["e_norm", "e_w_in", "e_conv_w", "e_a_log", "e_dt_bias", "e_o_norm", "e_ln_g", "e_ln_b", "e_w_s", "e_b_s", "e_w_out",
           "o_norm", "o_pw1", "o_pw1_b", "o_dw", "o_dw_b", "o_ln_g", "o_ln_b", "o_pw2", "o_pw2_b", "f_norm", "f_w1", "f_w2",
           "final_norm"]
BIG = ["e_w_in", "e_w_out", "o_pw1", "o_pw2", "f_w1", "f_w2"]
SMALL_SHARDED = ["e_conv_w", "o_norm", "o_pw1_b", "o_dw", "o_dw_b", "o_ln_g", "o_ln_b", "o_pw2_b"]
SMALL = [n for n in WEIGHTS if n not in BIG]
LANES = 128
PACK_ROWS = 16
IN_ROW_MULT = 256
REDUCE_GROUPS = {"mlp1": ("w2_1", "w1_1"), "conv": ("pw2", "pw1"), "mlp0": ("w2_0", "w1_0")}


def _pack(arrs):
    flat = jnp.concatenate([a.reshape(-1).astype(F32) for a in arrs])
    n = flat.shape[0]
    rows = -(-n // (LANES * PACK_ROWS)) * PACK_ROWS
    return jnp.pad(flat, (0, rows * LANES - n)).reshape(rows, LANES)


def _unpack(flat, shapes, lead=()):
    outs, off = [], 0
    for s in shapes:
        n = int(np.prod(s))
        outs.append(flat[..., off:off + n].reshape(lead + tuple(s)))
        off += n
    return outs


def _chip_major(g, s=N_CHIPS):
    k, n = g.shape
    return g.reshape(k, s, n // s).transpose(1, 0, 2)


def kernel(x, e_norm, e_w_in, e_conv_w, e_a_log, e_dt_bias, e_o_norm, e_ln_g, e_ln_b, e_w_s, e_b_s, e_w_out, o_norm, o_pw1, o_pw1_b, o_dw, o_dw_b, o_ln_g, o_ln_b, o_pw2, o_pw2_b, f_norm, f_w1, f_w2, final_norm, loss_target, m_e_norm, m_e_w_in, m_e_conv_w, m_e_a_log, m_e_dt_bias, m_e_o_norm, m_e_ln_g, m_e_ln_b, m_e_w_s, m_e_b_s, m_e_w_out, m_o_norm, m_o_pw1, m_o_pw1_b, m_o_dw, m_o_dw_b, m_o_ln_g, m_o_ln_b, m_o_pw2, m_o_pw2_b, m_f_norm, m_f_w1, m_f_w2, m_final_norm, v_e_norm, v_e_w_in, v_e_conv_w, v_e_a_log, v_e_dt_bias, v_e_o_norm, v_e_ln_g, v_e_ln_b, v_e_w_s, v_e_b_s, v_e_w_out, v_o_norm, v_o_pw1, v_o_pw1_b, v_o_dw, v_o_dw_b, v_o_ln_g, v_o_ln_b, v_o_pw2, v_o_pw2_b, v_f_norm, v_f_w1, v_f_w2, v_final_norm):
    a = dict(locals())
    xi, yi, ci = _me()
    chip = 2 * xi + yi
    who = jnp.stack([ci, chip]).astype(jnp.int32)
    t, d = x.shape[1], x.shape[2]
    nh, ng = e_a_log.shape[1], e_w_s.shape[1]
    n_qkv, n_av, n_bw = 3 * nh * HEAD_DIM, nh * HEAD_DIM, ng * HEAD_DIM
    in_cols = n_qkv + n_av + 2 * nh + 2 * n_bw
    c_ba = n_qkv + n_av

    sh_in = in_cols // N_CHIPS
    pad_in = -(-sh_in // IN_ROW_MULT) * IN_ROW_MULT - sh_in
    w_in_t_local = jnp.pad(e_w_in[0].T.astype(BF16), ((0, pad_in), (0, 0)))
    small_local = [a[n] for n in SMALL_SHARDED]
    g_in, g_small = gather_shards([w_in_t_local, _pack(small_local)], name="gather_weights_in")
    travelling, token = gather_start([[w.astype(BF16) for w in (e_w_out[0], f_w1[0], f_w2[0])],
                                      [w.astype(BF16) for w in (o_pw1[0], o_pw2[0], f_w1[1], f_w2[1])]], name="gather_start")

    state = {"layer0": travelling[0], "layer1": travelling[1]}

    def arrive(layer, event, after):
        if event == "landed":
            state[layer] = gather_forward(state[layer], after, name="gather_forward_" + layer)
            return {}
        got = gather_wait(state[layer], after, name="gather_wait_" + layer)
        if layer == "layer0":
            g_out, g_w1, g_w2 = got
            return {"w_out": g_out.reshape(1, -1, d), "w1_0": g_w1, "w2_0": g_w2.reshape(1, -1, d)}
        g_pw1, g_pw2, g_w1, g_w2 = got
        return {"pw1": g_pw1, "pw2": g_pw2.reshape(1, -1, d), "w1_1": g_w1, "w2_1": g_w2.reshape(1, -1, d)}

    def in_rows(lo, hi):
        parts = []
        for s in range(N_CHIPS):
            a0, a1 = max(lo, s * sh_in), min(hi, (s + 1) * sh_in)
            if a0 < a1:
                parts.append(g_in[s, a0 - s * sh_in:a1 - s * sh_in])
        return parts

    wb = {
        "w_main_t": jnp.concatenate(in_rows(0, c_ba) + in_rows(c_ba + 2 * nh, in_cols), axis=0)[None],
        "w_ba_t": jnp.pad(jnp.concatenate(in_rows(c_ba, c_ba + 2 * nh), axis=0), ((0, LANES - 2 * nh), (0, 0)))[None],
    }
    pieces = _unpack(g_small.reshape(N_CHIPS, -1), [w.shape for w in small_local], lead=(N_CHIPS,))
    full = {n: jnp.moveaxis(p, 0, -2).reshape(p.shape[1:-1] + (N_CHIPS * p.shape[-1],)) for n, p in zip(SMALL_SHARDED, pieces)}
    ws = {
        "e_norm": e_norm + token[0, 0],
        "conv_w": full["e_conv_w"][0], "alog": e_a_log.reshape(nh, 1, 1), "dtb": e_dt_bias.reshape(nh, 1, 1),
        "onorm": e_o_norm, "lng": e_ln_g.reshape(ng, 1, HEAD_DIM), "lnb": e_ln_b.reshape(ng, 1, HEAD_DIM), "w_s": e_w_s[0],
        "bs": e_b_s.reshape(ng, HEAD_DIM, 1), "o_norm": full["o_norm"], "b1a": full["o_pw1_b"][:, :d], "b1b": full["o_pw1_b"][:, d:],
        "dw": full["o_dw"][0], "dw_b": full["o_dw_b"], "ln_g": full["o_ln_g"], "ln_b": full["o_ln_b"], "b2": full["o_pw2_b"],
        "f_norm0": f_norm[0:1], "f_norm1": f_norm[1:2], "final_norm": final_norm.reshape(1, d),
    }

    pending = {}

    def reduce(group, event, gb, after):
        names = REDUCE_GROUPS[group]
        if event == "ready":
            arrs = [gb[nme] if gb[nme].shape[0] == N_CHIPS else gb[nme].reshape(N_CHIPS, -1, d) for nme in names]
            pending[group] = split_start(_pair_copies, arrs, [(N_CHIPS, g.shape[1] // 2, g.shape[2]) for g in arrs], len(arrs),
                                         name="reduce_sibling_start_" + group)
            return pending[group]["token"]
        mine, theirs = split_wait(pending[group], after, name="reduce_sibling_wait_" + group)
        pairs = [pre_reduce(gd, sb, who, name="reduce_pair_" + nme) for nme, gd, sb in zip(names, mine, theirs)]
        sums = [p[0] for p in pairs]
        pending[group] = {"own": [p[1] for p in pairs],
                          "scatter": split_start(_scatter_copies, sums, [(3,) + s.shape[1:] for s in sums], 3 * len(sums),
                                                 name="reduce_chips_start_" + group)}
        return pending[group]["scatter"]["token"]

    loss, grad_x, gb, gs = local_step(x[0], loss_target[0], wb, ws, arrive, reduce)

    gm, gba = gb["w_main_t"][0], gb["w_ba_t"][0]
    g_in_t = jnp.concatenate([gm[:c_ba], gba[:2 * nh], gm[c_ba:]], axis=0).reshape(N_CHIPS, sh_in, d)
    by_chip = [jnp.pad(g_in_t, ((0, 0), (0, pad_in), (0, 0))), gb["w_out"].reshape(N_CHIPS, -1, d)]
    from_sibling = sibling_send_other_half(by_chip, name="reduce_sibling")
    to_send, own = [], []
    for nme, gd, sb in zip(("w_in", "w_out"), by_chip, from_sibling):
        pb, ow = pre_reduce(gd, sb, who, name="reduce_pair_" + nme)
        to_send.append(pb)
        own.append(ow)
    received = chip_scatter(to_send, name="reduce_chips")
    half = {nme: final_reduce(ow, rc, who, name="reduce_sum_" + nme) for nme, ow, rc in zip(("w_in", "w_out"), own, received)}
    for group, names in REDUCE_GROUPS.items():
        _, landed = split_wait(pending[group]["scatter"], grad_x, name="reduce_chips_wait_" + group)
        for nme, ow, rc in zip(names, pending[group]["own"], landed):
            half[nme] = final_reduce(ow, rc, who, name="reduce_sum_" + nme)
    r_in, r_out, r_pw1, r_pw2, r_w1_0, r_w1_1, r_w2_0, r_w2_1 = sibling_join_halves(
        [half[nme] for nme in ("w_in", "w_out", "pw1", "pw2", "w1_0", "w1_1", "w2_0", "w2_1")], name="reduce_join")
    big_grads = {"e_w_in": [r_in[:sh_in]], "e_w_out": [r_out], "o_pw1": [r_pw1], "o_pw2": [r_pw2], "f_w1": [r_w1_0, r_w1_1],
                 "f_w2": [r_w2_0, r_w2_1]}

    small_global = {
        "e_norm": gs["e_norm"], "e_conv_w": gs["conv_w"][None], "e_a_log": gs["alog"].reshape(1, nh), "e_dt_bias": gs["dtb"].reshape(1, nh),
        "e_o_norm": gs["onorm"], "e_ln_g": gs["lng"].reshape(1, n_bw), "e_ln_b": gs["lnb"].reshape(1, n_bw), "e_w_s": gs["w_s"][None],
        "e_b_s": gs["bs"].reshape(1, ng, HEAD_DIM), "o_norm": gs["o_norm"], "o_pw1_b": jnp.concatenate([gs["b1a"], gs["b1b"]], axis=1),
        "o_dw": gs["dw"][None], "o_dw_b": gs["dw_b"], "o_ln_g": gs["ln_g"], "o_ln_b": gs["ln_b"], "o_pw2_b": gs["b2"],
        "f_norm": jnp.concatenate([gs["f_norm0"], gs["f_norm1"]], axis=0), "final_norm": gs["final_norm"].reshape(d),
    }
    packed = _pack([small_global[n] for n in SMALL])
    summed = sum8(allgather8(packed, name="reduce_small"), name="reduce_small_sum")
    small_full = _unpack(summed.reshape(-1), [small_global[n].shape for n in SMALL])
    small_grads = {}
    for n, g in zip(SMALL, small_full):
        if n in SMALL_SHARDED:
            width = a[n].shape[-1]
            g = lax.dynamic_slice_in_dim(g, chip * width, width, axis=g.ndim - 1)
        small_grads[n] = g

    out = {}
    for n in BIG:
        w, m, v = a[n], a["m_" + n], a["v_" + n]
        if n == "e_w_in":
            res = adamw(w.transpose(0, 2, 1), big_grads[n], m.transpose(0, 2, 1), v.transpose(0, 2, 1), name="adamw_" + n)
            out[n] = tuple(r.transpose(0, 2, 1) for r in res)
        else:
            out[n] = adamw(w, big_grads[n], m, v, name="adamw_" + n)
    sw, sm, sv, sg = (_pack([src[n] for n in SMALL])[None] for src in
                      ({n: a[n] for n in SMALL}, {n: a["m_" + n] for n in SMALL}, {n: a["v_" + n] for n in SMALL}, small_grads))
    res = adamw(sw, [sg[0]], sm, sv, name="adamw_small")
    shapes = [a[n].shape for n in SMALL]
    unpacked = [_unpack(r.reshape(-1), shapes) for r in res]
    for i, n in enumerate(SMALL):
        out[n] = tuple(u[i] for u in unpacked)

    total = lax.psum(loss[0, 0], ("x", "y", "c"))
    result = [total, grad_x[None]]
    for k in range(4):
        result += [out[n][k] for n in WEIGHTS]
    return tuple(result)
```

```python
import functools
import math

import jax
import jax.numpy as jnp
import numpy as np
from jax import lax
from jax.experimental import pallas as pl
from jax.experimental.pallas import tpu as pltpu

F32 = jnp.float32
BF16 = jnp.bfloat16
HI = lax.Precision.HIGHEST

EPS = 1e-6
CHUNK = 64
PAIR = 2 * CHUNK
HEAD_DIM = 128
A_CONV = 4
C_WIDTH = 31
N_CHIPS = 4
ADAM_LR, ADAM_B1, ADAM_B2, ADAM_EPS, ADAM_WD, ADAM_STEP = 0.001, 0.9, 0.999, 1e-08, 0.01, 10

VMEM_LIMIT = 56 * 1024 * 1024


def _cparams(sem=None):
    return pltpu.CompilerParams(dimension_semantics=sem, vmem_limit_bytes=VMEM_LIMIT)


def _pick(n, prefs):
    for p in prefs:
        if n % p == 0:
            return p
    return n


def mm_nn(a, w, *, name, epilogue=None, add=None, out_dtype=F32):
    m, k = a.shape
    s, _, ns = w.shape
    n = s * ns
    tm = _pick(m, (1024, 512, 256, 128))
    tn = _pick(ns, (1024, 512, 256, 128))
    tk = _pick(k, (2048, 1024, 512, 256, 128))
    nk = k // tk
    npb = ns // tn
    assert add is None or epilogue is None

    def body(a_ref, w_ref, *rest):
        if add is not None:
            add_ref, rest = rest[0], rest[1:]
        if epilogue == "relu2":
            o1_ref, o2_ref = rest[0], rest[1]
            acc_ref = rest[2] if nk > 1 else None
        else:
            o1_ref = rest[0]
            acc_ref = rest[1] if nk > 1 else None
        part = jnp.dot(a_ref[...], w_ref[...], preferred_element_type=F32)

        def finish(c):
            if epilogue == "relu2":
                r = jnp.maximum(c, 0.0)
                o1_ref[...] = r.astype(o1_ref.dtype)
                o2_ref[...] = (r * r).astype(o2_ref.dtype)
            elif add is not None:
                o1_ref[...] = (c + add_ref[...].astype(F32)).astype(o1_ref.dtype)
            else:
                o1_ref[...] = c.astype(o1_ref.dtype)

        if nk == 1:
            finish(part)
        else:
            kk = pl.program_id(2)

            @pl.when(kk == 0)
            def _():
                acc_ref[...] = part

            @pl.when(kk > 0)
            def _():
                acc_ref[...] += part

            @pl.when(kk == nk - 1)
            def _():
                finish(acc_ref[...])

    o_spec = pl.BlockSpec((tm, tn), lambda i, j, kk: (i, j))
    if epilogue == "relu2":
        out_shape = (jax.ShapeDtypeStruct((m, n), BF16), jax.ShapeDtypeStruct((m, n), BF16))
        out_specs = (o_spec, o_spec)
    else:
        out_shape = jax.ShapeDtypeStruct((m, n), out_dtype)
        out_specs = o_spec
    return pl.pallas_call(
        body, name=name, out_shape=out_shape,
        grid=(m // tm, n // tn, nk),
        in_specs=[pl.BlockSpec((tm, tk), lambda i, j, kk: (i, kk)),
                  pl.BlockSpec((None, tk, tn), lambda i, j, kk: (j // npb, kk, j % npb))] + ([o_spec] if add is not None else []),
        out_specs=out_specs,
        scratch_shapes=[pltpu.VMEM((tm, tn), F32)] if nk > 1 else [],
        compiler_params=_cparams(("parallel", "parallel", "arbitrary")),
    )(*([a, w] + ([add] if add is not None else [])))


def mm_nt(a, w, *, name, mul=None, add=None, out_dtype=F32):
    assert mul is None or add is None
    if add is not None:
        mul = add
    m, n = a.shape
    s, k, ns = w.shape
    assert n == s * ns
    tm = _pick(m, (1024, 512, 256, 128))
    tko = _pick(k, (1024, 512, 256, 128))
    tn = _pick(ns, (2048, 1024, 512, 256, 128))
    nn = n // tn
    npb = ns // tn

    def body(a_ref, w_ref, *rest):
        if mul is not None:
            m_ref, o_ref = rest[0], rest[1]
            acc_ref = rest[2] if nn > 1 else None
        else:
            m_ref, o_ref = None, rest[0]
            acc_ref = rest[1] if nn > 1 else None
        part = lax.dot_general(a_ref[...], w_ref[...], (((1,), (1,)), ((), ())), preferred_element_type=F32)

        def finish(c):
            if add is not None:
                c = c + m_ref[...].astype(F32)
            elif m_ref is not None:
                c = c * (2.0 * m_ref[...].astype(F32))
            o_ref[...] = c.astype(o_ref.dtype)

        if nn == 1:
            finish(part)
        else:
            kk = pl.program_id(2)

            @pl.when(kk == 0)
            def _():
                acc_ref[...] = part

            @pl.when(kk > 0)
            def _():
                acc_ref[...] += part

            @pl.when(kk == nn - 1)
            def _():
                finish(acc_ref[...])

    in_specs = [pl.BlockSpec((tm, tn), lambda i, j, kk: (i, kk)),
                pl.BlockSpec((None, tko, tn), lambda i, j, kk: (kk // npb, j, kk % npb))]
    args = [a, w]
    if mul is not None:
        in_specs.append(pl.BlockSpec((tm, tko), lambda i, j, kk: (i, j)))
        args.append(mul)
    return pl.pallas_call(
        body, name=name, out_shape=jax.ShapeDtypeStruct((m, k), out_dtype),
        grid=(m // tm, k // tko, nn),
        in_specs=in_specs,
        out_specs=pl.BlockSpec((tm, tko), lambda i, j, kk: (i, j)),
        scratch_shapes=[pltpu.VMEM((tm, tko), F32)] if nn > 1 else [],
        compiler_params=_cparams(("parallel", "parallel", "arbitrary")),
    )(*args)


def mm_tn(a, b, *, groups, name):
    m, k = a.shape
    _, n = b.shape
    ns = n // groups
    tko = _pick(k, (1024, 512, 256, 128))
    tn = _pick(ns, (1024, 512, 256, 128))
    tc = _pick(m, (2048, 1024, 512, 256, 128))
    nc = m // tc
    npb = ns // tn

    def body(a_ref, b_ref, o_ref):
        part = lax.dot_general(a_ref[...], b_ref[...], (((0,), (0,)), ((), ())), preferred_element_type=F32)
        kk = pl.program_id(2)

        @pl.when(kk == 0)
        def _():
            o_ref[...] = part

        @pl.when(kk > 0)
        def _():
            o_ref[...] += part

    return pl.pallas_call(
        body, name=name, out_shape=jax.ShapeDtypeStruct((groups, k, ns), F32),
        grid=(k // tko, n // tn, nc),
        in_specs=[pl.BlockSpec((tc, tko), lambda i, j, kk: (kk, i)),
                  pl.BlockSpec((tc, tn), lambda i, j, kk: (kk, j))],
        out_specs=pl.BlockSpec((None, tko, tn), lambda i, j, kk: (j // npb, i, j % npb)),
        compiler_params=_cparams(("parallel", "parallel", "arbitrary")),
    )(a, b)


ROWS = 128


def _full_spec(arr):
    nd = arr.ndim
    return pl.BlockSpec(arr.shape, lambda i, _nd=nd: (0,) * _nd)


def _row_spec(x, rows):
    if isinstance(x, tuple):
        _, w, cb = x
        return pl.BlockSpec((rows, w), lambda i, _cb=cb: (i, _cb))
    return pl.BlockSpec((rows, x.shape[1]), lambda i: (i, 0))


def _arr(x):
    return x[0] if isinstance(x, tuple) else x


def _width(x):
    return x[1] if isinstance(x, tuple) else x.shape[1]


def rows_fwd(fn, params, xs, stores, *, name, rows=ROWS):
    t = _arr(xs[0]).shape[0]
    np_, nx = len(params), len(xs)

    def body(*refs):
        p_refs, x_refs, o_refs = refs[:np_], refs[np_:np_ + nx], refs[np_ + nx:]
        outs = fn(*[r[...].astype(F32) for r in p_refs], *[r[...].astype(F32) for r in x_refs])
        for (idx, dt, _), o_ref in zip(stores, o_refs):
            o_ref[...] = outs[idx].astype(dt)

    res = pl.pallas_call(
        body, name=name,
        out_shape=tuple(jax.ShapeDtypeStruct((t, w), dt) for _, dt, w in stores),
        grid=(t // rows,),
        in_specs=[_full_spec(p) for p in params] + [_row_spec(x, rows) for x in xs],
        out_specs=tuple(pl.BlockSpec((rows, w), lambda i: (i, 0)) for _, _, w in stores),
        compiler_params=_cparams(("parallel",)),
    )(*params, *[_arr(x) for x in xs])
    return res


def rows_bwd(fn, params, xs, cts, dx_stores, *, name, rows=ROWS, concat=False):
    t = _arr(xs[0]).shape[0]
    np_, nx = len(params), len(xs)
    ct_idx = [i for i, c in enumerate(cts) if c is not None]
    ct_arrs = [cts[i] for i in ct_idx]
    nct = len(ct_arrs)
    nds = 1 if concat else len(dx_stores)
    widths = [_width(xs[xi]) for xi, _ in dx_stores]

    def body(*refs):
        p_refs = refs[:np_]
        x_refs = refs[np_:np_ + nx]
        c_refs = refs[np_ + nx:np_ + nx + nct]
        d_refs = refs[np_ + nx + nct:np_ + nx + nct + nds]
        g_refs = refs[np_ + nx + nct + nds:]
        pv = [r[...].astype(F32) for r in p_refs]
        xv = [r[...].astype(F32) for r in x_refs]
        outs, vjp = jax.vjp(lambda *a: tuple(fn(*a)), *pv, *xv)
        ct_full = [jnp.zeros_like(o) for o in outs]
        for i, r in zip(ct_idx, c_refs):
            ct_full[i] = r[...].astype(F32)
        grads = vjp(tuple(ct_full))
        if concat:
            off = 0
            for (xi, dt), wd in zip(dx_stores, widths):
                d_refs[0][:, off:off + wd] = grads[np_ + xi].astype(dt)
                off += wd
        else:
            for (xi, dt), d_ref in zip(dx_stores, d_refs):
                d_ref[...] = grads[np_ + xi].astype(dt)
        step = pl.program_id(0)
        for j, g_ref in enumerate(g_refs):
            @pl.when(step == 0)
            def _(g_ref=g_ref, j=j):
                g_ref[...] = grads[j]

            @pl.when(step > 0)
            def _(g_ref=g_ref, j=j):
                g_ref[...] += grads[j]

    dx_shapes = [(sum(widths), dx_stores[0][1])] if concat else [(wd, dt) for wd, (_, dt) in zip(widths, dx_stores)]
    out_shape = tuple(jax.ShapeDtypeStruct((t, wd), dt) for wd, dt in dx_shapes) + \
        tuple(jax.ShapeDtypeStruct(p.shape, F32) for p in params)
    out_specs = tuple(pl.BlockSpec((rows, wd), lambda i: (i, 0)) for wd, _ in dx_shapes) + \
        tuple(_full_spec(p) for p in params)
    return pl.pallas_call(
        body, name=name, out_shape=out_shape, grid=(t // rows,),
        in_specs=[_full_spec(p) for p in params] + [_row_spec(x, rows) for x in xs] + [_row_spec(c, rows) for c in ct_arrs],
        out_specs=out_specs,
        compiler_params=_cparams(("arbitrary",)),
    )(*params, *[_arr(x) for x in xs], *[_arr(c) for c in ct_arrs])


def _rms(x, g):
    return x * lax.rsqrt(jnp.mean(x * x, axis=-1, keepdims=True) + EPS) * g


def _sigmoid(x):
    return 1.0 / (1.0 + jnp.exp(-x))


def _silu(x):
    return x * _sigmoid(x)


def _gelu(x):
    return 0.5 * x * (1.0 + lax.erf(x * (1.0 / math.sqrt(2.0))))


def stage_norm(g, x):
    return x, _rms(x, g)


def stage_res_norm(g, x, y):
    xn = x + y
    return xn, _rms(xn, g)


def stage_res_bias_norm(b, g, x, y):
    xn = x + y + b
    return xn, _rms(xn, g)


def stage_glu(ba, bb, za, zb):
    return ((za + ba) * _sigmoid(zb + bb),)


def stage_ln_silu(dw_b, ln_g, ln_b, cv):
    z = cv + dw_b
    mu = jnp.mean(z, axis=-1, keepdims=True)
    zc = z - mu
    y = zc * lax.rsqrt(jnp.mean(zc * zc, axis=-1, keepdims=True) + EPS) * ln_g + ln_b
    return (_silu(y),)


CONV_ROWS = 128
CONV_COLS = 256
SUBLANES = 8


def _halo(k):
    return SUBLANES * ((k - 1 + SUBLANES - 1) // SUBLANES)


def _taps_by_roll(k):
    out = {}
    for s in range(k):
        out.setdefault(s % SUBLANES, []).append((s // SUBLANES, s))
    return out


def _shifted_down(win, k, r):
    halo, n = _halo(k), win.shape[0]
    segs = {}
    for b, lst in _taps_by_roll(k).items():
        rolled = win if b == 0 else pltpu.roll(win, b, axis=0)
        for a, s in lst:
            segs[s] = rolled[halo - SUBLANES * a: halo - SUBLANES * a + r]
    return segs


def _shifted_up(win, k, r):
    n = win.shape[0]
    segs = {}
    for b, lst in _taps_by_roll(k).items():
        rolled = win if b == 0 else pltpu.roll(win, n - b, axis=0)
        for a, s in lst:
            segs[s] = rolled[SUBLANES * a: SUBLANES * a + r]
    return segs


def _for_blocks(nblk, fn):
    fn(0, True, nblk == 1)
    if nblk > 2:
        def step(i, c):
            fn(i, False, False)
            return c
        lax.fori_loop(1, nblk - 1, step, 0)
    if nblk > 1:
        fn(nblk - 1, False, True)


def _base(i, r):
    return i * r if isinstance(i, int) else pl.multiple_of(i * r, r)


def _win_top(ref, i, first, r, halo):
    if first:
        return jnp.concatenate([jnp.zeros((halo, ref.shape[1]), F32), ref[pl.ds(0, r), :]], axis=0)
    base = _base(i, r)
    return ref[pl.ds(base - halo, r + halo), :]


def _win_bottom(ref, i, last, r, halo):
    base = _base(i, r)
    if last:
        return jnp.concatenate([ref[pl.ds(base, r), :], jnp.zeros((halo, ref.shape[1]), F32)], axis=0)
    return ref[pl.ds(base, r + halo), :]


def conv_fwd(x, w, *, mode, name, g=None):
    t = x.shape[0]
    k, c = w.shape
    r, cb, halo = min(CONV_ROWS, t), min(CONV_COLS, c), _halo(k)
    nblk = t // r

    def body(*refs):
        if mode == "silu_bwd":
            x_ref, w_ref, g_ref, o_ref = refs
        else:
            x_ref, w_ref, o_ref = refs

        def blk(i, first, last):
            segs = _shifted_down(_win_top(x_ref, i, first, r, halo), k, r)
            acc = None
            for s in range(k):
                term = w_ref[pl.ds(k - 1 - s, 1), :] * segs[s]
                acc = term if acc is None else acc + term
            base = _base(i, r)
            if mode == "silu":
                acc = _silu(acc)
            elif mode == "silu_bwd":
                sg = _sigmoid(acc)
                acc = g_ref[pl.ds(base, r), :] * (sg * (1.0 + acc * (1.0 - sg)))
            o_ref[pl.ds(base, r), :] = acc

        _for_blocks(nblk, blk)

    col = pl.BlockSpec((t, cb), lambda j: (0, j))
    in_specs = [col, pl.BlockSpec((k, cb), lambda j: (0, j))] + ([col] if mode == "silu_bwd" else [])
    args = [x, w] + ([g] if mode == "silu_bwd" else [])
    return pl.pallas_call(
        body, name=name, out_shape=jax.ShapeDtypeStruct((t, c), F32), grid=(c // cb,),
        in_specs=in_specs, out_specs=col, compiler_params=_cparams(("parallel",)),
    )(*args)


def conv_bwd(x, w, dy, *, name, into=None):
    t = x.shape[0]
    k, c = w.shape
    r, cb, halo = min(CONV_ROWS, t), min(CONV_COLS, c), _halo(k)
    nblk = t // r

    def body(x_ref, w_ref, dy_ref, *rest):
        dx_ref, dw_ref = rest[-2], rest[-1]
        dw_ref[...] = jnp.zeros_like(dw_ref)

        def blk(i, first, last):
            base = _base(i, r)
            up = _shifted_up(_win_bottom(dy_ref, i, last, r, halo), k, r)
            down = _shifted_down(_win_top(x_ref, i, first, r, halo), k, r)
            dyb = up[0]
            acc = None
            for s in range(k):
                term = w_ref[pl.ds(k - 1 - s, 1), :] * up[s]
                acc = term if acc is None else acc + term
                dw_ref[pl.ds(k - 1 - s, 1), :] += jnp.sum(down[s] * dyb, axis=0, keepdims=True)
            dx_ref[pl.ds(base, r), :] = acc.astype(dx_ref.dtype)

        _for_blocks(nblk, blk)

    col = pl.BlockSpec((t, cb), lambda j: (0, j))
    wsp = pl.BlockSpec((k, cb), lambda j: (0, j))
    dx_shape = jax.ShapeDtypeStruct((t, c), F32) if into is None else jax.ShapeDtypeStruct(into.shape, into.dtype)
    return pl.pallas_call(
        body, name=name,
        out_shape=(dx_shape, jax.ShapeDtypeStruct((k, c), F32)), grid=(c // cb,),
        in_specs=[col, wsp, col] + ([] if into is None else [pl.BlockSpec(memory_space=pl.ANY)]),
        out_specs=(col, wsp),
        input_output_aliases={} if into is None else {3: 0},
        compiler_params=_cparams(("parallel",)),
    )(*([x, w, dy] + ([] if into is None else [into])))


_DIMS = {"nn": (((1,), (0,)), ((), ())), "nt": (((1,), (1,)), ((), ())), "tn": (((0,), (0,)), ((), ()))}
_DIMS_BATCHED = {"nn": (((2,), (1,)), ((0,), (0,))), "nt": (((2,), (2,)), ((0,), (0,))), "tn": (((1,), (1,)), ((0,), (0,)))}


def _mxu(a, b, mode):
    dims = _DIMS_BATCHED if a.ndim == 3 else _DIMS
    return lax.dot_general(a, b, dims[mode], preferred_element_type=F32)


def _split(x):
    hi = x.astype(BF16)
    return hi, (x - hi.astype(F32)).astype(BF16)


def _dot_raw(a, b, mode, prec):
    if prec == "bf16":
        return _mxu(a.astype(BF16), b.astype(BF16), mode)
    if prec == "x3":
        ah, al = _split(a)
        bh, bl = _split(b)
        return _mxu(ah, bh, mode) + (_mxu(ah, bl, mode) + _mxu(al, bh, mode))
    if prec == "x3r":
        bh, bm = _split(b)
        bl = (b - bh.astype(F32) - bm.astype(F32)).astype(BF16)
        ah = a.astype(BF16)
        return _mxu(ah, bh, mode) + (_mxu(ah, bm, mode) + _mxu(ah, bl, mode))
    raise ValueError(prec)


@functools.lru_cache(maxsize=None)
def _dot_fn(mode, prec):
    bprec = "x3" if prec == "x3r" else prec

    @jax.custom_vjp
    def f(a, b):
        return _dot_raw(a, b, mode, prec)

    def fwd(a, b):
        return _dot_raw(a, b, mode, prec), (a, b)

    def bwd(res, ct):
        a, b = res
        if mode == "nn":
            return _dot_raw(ct, b, "nt", bprec), _dot_raw(a, ct, "tn", bprec)
        if mode == "nt":
            return _dot_raw(ct, b, "nn", bprec), _dot_raw(ct, a, "tn", bprec)
        return _dot_raw(b, ct, "nt", bprec), _dot_raw(a, ct, "nn", bprec)

    f.defvjp(fwd, bwd)
    return f


def _dot(a, b, mode="nn", prec="bf16"):
    return _dot_fn(mode, prec)(a, b)


def _inv_product(l):
    n = l.shape[-1]
    eye = (lax.broadcasted_iota(jnp.int32, (n, n), 0) == lax.broadcasted_iota(jnp.int32, (n, n), 1)).astype(F32)
    p = eye - l
    pw = l
    for _ in range(5):
        pw = _dot_raw(pw, pw, "nn", "x3")
        p = _dot_raw(p, eye + pw, "nn", "x3")
    return p


@jax.custom_vjp
def _inv_unit_lower(l, t_saved):
    return t_saved


def _inv_fwd(l, t_saved):
    return t_saved, t_saved


def _inv_bwd(t, ct):
    tmp = _dot_raw(t, ct, "tn", "x3")
    return -_dot_raw(tmp, t, "nt", "x3"), jnp.zeros_like(t)


_inv_unit_lower.defvjp(_inv_fwd, _inv_bwd)


def _softplus(x):
    pos = x > 0
    return jnp.where(pos, x, 0.0) + jnp.log(1.0 + jnp.exp(jnp.where(pos, -x, x)))


def _l2n(x):
    return x * lax.rsqrt(jnp.sum(x * x, axis=-1, keepdims=True) + EPS)


def delta_pair(s0, qc, kc, vc, z, pba, alog, dtb, onorm, t_saved=None):
    n = PAIR
    nh = qc.shape[0]
    assert qc.shape == (nh, n, HEAD_DIM) and n == HEAD_DIM
    hi = lax.broadcasted_iota(jnp.int32, (nh, 1, pba.shape[1]), 0)
    li = lax.broadcasted_iota(jnp.int32, (nh, 1, pba.shape[1]), 2)
    braw = jnp.sum(pba[None] * (li == hi).astype(F32), axis=2, keepdims=True)
    araw = jnp.sum(pba[None] * (li == hi + nh).astype(F32), axis=2, keepdims=True)
    ri = lax.broadcasted_iota(jnp.int32, (n, n), 0)
    ci = lax.broadcasted_iota(jnp.int32, (n, n), 1)
    same = (ri // CHUNK) == (ci // CHUNK)
    tri = same & (ci <= ri)
    same_f = jnp.broadcast_to(same.astype(F32), (nh, n, n))
    tri_f = jnp.broadcast_to(tri.astype(F32), (nh, n, n))
    strict_f = (same & (ci < ri)).astype(F32)
    m0 = (lax.broadcasted_iota(jnp.int32, (n, 1), 0) < CHUNK).astype(F32)
    m1 = 1.0 - m0

    q = _l2n(qc) * (HEAD_DIM ** -0.5)
    k = _l2n(kc)
    beta = _sigmoid(braw)
    g = -jnp.exp(alog) * _softplus(araw + dtb)
    gb = jnp.broadcast_to(g, (nh, n, n))
    gc = _dot(tri_f, gb, "nn", "x3r")
    gtot = _dot(same_f, gb, "nn", "x3r")
    decay = jnp.exp(jnp.where(tri, gc - jnp.swapaxes(gc, 1, 2), -1e30))
    eg = jnp.exp(gc)
    kb, vb = k * beta, vc * beta
    l = _dot(kb, k, "nt") * decay * strict_f
    if t_saved is None:
        tinv = _inv_product(l)
    else:
        tinv = _inv_unit_lower(l, t_saved)
    u = _dot(tinv, vb, "nn", "x3")
    w = _dot(tinv, kb * eg, "nn", "x3")
    attn = _dot(q, k, "nt") * decay
    q_dec = q * eg
    k_tail = k * jnp.exp(gtot - gc)
    gl0 = jnp.exp(jnp.sum(gb * m0, axis=1, keepdims=True))
    gl1 = jnp.exp(jnp.sum(gb * m1, axis=1, keepdims=True))

    vn0 = m0 * (u - _dot(w, s0))
    s1 = s0 * gl0 + _dot(k_tail, vn0, "tn")
    vn1 = m1 * (u - _dot(w, s1))
    o = m0 * _dot(q_dec, s0) + m1 * _dot(q_dec, s1) + _dot(attn, vn0 + vn1)
    s2 = s1 * gl1 + _dot(k_tail, vn1, "tn")

    on = o * lax.rsqrt(jnp.mean(o * o, axis=-1, keepdims=True) + EPS) * onorm
    return on * _silu(z), s2, tinv


def _hcols(i):
    return slice(i * HEAD_DIM, (i + 1) * HEAD_DIM)


def _heads(ref, first, count):
    return jnp.stack([ref[:, _hcols(first + i)] for i in range(count)])


def delta_fwd(qkv, pm, pba, alog, dtb, onorm, cat_width, *, name):
    t = qkv.shape[0]
    h = alog.shape[0]
    hd = h * HEAD_DIM
    npair = t // PAIR
    mat = pl.BlockSpec((h, None, PAIR, HEAD_DIM), lambda p: (0, p, 0, 0))
    par = pl.BlockSpec((h, 1, 1), lambda p: (0, 0, 0))

    def body(qkv_ref, z_ref, pba_ref, al_ref, dt_ref, on_ref, o_ref, st_ref, ti_ref, s_scr):
        @pl.when(pl.program_id(0) == 0)
        def _():
            s_scr[...] = jnp.zeros_like(s_scr)

        s0 = s_scr[...]
        st_ref[...] = s0
        out, s2, tinv = delta_pair(s0, _heads(qkv_ref, 0, h), _heads(qkv_ref, h, h), _heads(qkv_ref, 2 * h, h), _heads(z_ref, 0, h),
                                   pba_ref[...], al_ref[...], dt_ref[...], on_ref[...])
        for i in range(h):
            o_ref[:, _hcols(i)] = out[i].astype(o_ref.dtype)
        ti_ref[...] = tinv
        s_scr[...] = s2

    return pl.pallas_call(
        body, name=name, grid=(npair,),
        out_shape=(jax.ShapeDtypeStruct((t, cat_width), BF16),
                   jax.ShapeDtypeStruct((h, npair, PAIR, HEAD_DIM), F32),
                   jax.ShapeDtypeStruct((h, npair, PAIR, PAIR), F32)),
        in_specs=[pl.BlockSpec((PAIR, 3 * hd), lambda p: (p, 0)), pl.BlockSpec((PAIR, hd), lambda p: (p, 3)),
                  pl.BlockSpec((PAIR, pba.shape[1]), lambda p: (p, 0)), par, par,
                  pl.BlockSpec((1, HEAD_DIM), lambda p: (0, 0))],
        out_specs=(pl.BlockSpec((PAIR, hd), lambda p: (p, 0)), mat, mat),
        scratch_shapes=[pltpu.VMEM((h, HEAD_DIM, HEAD_DIM), F32)],
        compiler_params=_cparams(("arbitrary",)),
    )(qkv, pm, pba, alog, dtb, onorm)


def delta_bwd(qkv, pm, pba, alog, dtb, onorm, states, tinvs, dcat, *, name):
    t = qkv.shape[0]
    h = alog.shape[0]
    hd = h * HEAD_DIM
    npair = t // PAIR
    rev = lambda p: npair - 1 - p
    mat = pl.BlockSpec((h, None, PAIR, HEAD_DIM), lambda p: (0, rev(p), 0, 0))
    par = pl.BlockSpec((h, 1, 1), lambda p: (0, 0, 0))
    onsp = pl.BlockSpec((1, HEAD_DIM), lambda p: (0, 0))
    wide = pl.BlockSpec((PAIR, 3 * hd), lambda p: (rev(p), 0))
    zsp = pl.BlockSpec((PAIR, hd), lambda p: (rev(p), 3))
    bsp = pl.BlockSpec((PAIR, pba.shape[1]), lambda p: (rev(p), 0))

    def body(qkv_ref, z_ref, pba_ref, al_ref, dt_ref, on_ref, st_ref, ti_ref, dc_ref,
             dqkv_ref, dz_ref, dpba_ref, dal_ref, ddt_ref, don_ref, ds_scr):
        @pl.when(pl.program_id(0) == 0)
        def _():
            ds_scr[...] = jnp.zeros_like(ds_scr)
            dal_ref[...] = jnp.zeros_like(dal_ref)
            ddt_ref[...] = jnp.zeros_like(ddt_ref)
            don_ref[...] = jnp.zeros_like(don_ref)

        tsv = ti_ref[...]
        fn = lambda s0, qc, kc, vc, z, pb, al, dt, on: delta_pair(s0, qc, kc, vc, z, pb, al, dt, on, tsv)[:2]
        _, vjp = jax.vjp(fn, st_ref[...], _heads(qkv_ref, 0, h), _heads(qkv_ref, h, h), _heads(qkv_ref, 2 * h, h),
                         _heads(z_ref, 0, h), pba_ref[...], al_ref[...], dt_ref[...], on_ref[...])
        ds0, dq, dk, dv, dz, dpba, dal, ddt, don = vjp((_heads(dc_ref, 0, h), ds_scr[...]))
        ds_scr[...] = ds0
        for i in range(h):
            dqkv_ref[:, _hcols(i)] = dq[i]
            dqkv_ref[:, _hcols(h + i)] = dk[i]
            dqkv_ref[:, _hcols(2 * h + i)] = dv[i]
            dz_ref[:, _hcols(i)] = dz[i].astype(dz_ref.dtype)
        dal_ref[...] += dal
        ddt_ref[...] += ddt
        dpba_ref[...] = dpba.astype(dpba_ref.dtype)
        don_ref[...] += don

    return pl.pallas_call(
        body, name=name, grid=(npair,),
        out_shape=(jax.ShapeDtypeStruct((t, 3 * hd), F32), jax.ShapeDtypeStruct(pm.shape, BF16),
                   jax.ShapeDtypeStruct(pba.shape, BF16),
                   jax.ShapeDtypeStruct((h, 1, 1), F32), jax.ShapeDtypeStruct((h, 1, 1), F32),
                   jax.ShapeDtypeStruct((1, HEAD_DIM), F32)),
        in_specs=[wide, zsp, bsp, par, par, onsp, mat, mat, pl.BlockSpec((PAIR, hd), lambda p: (rev(p), 0))],
        out_specs=(wide, zsp, bsp, par, par, onsp),
        scratch_shapes=[pltpu.VMEM((h, HEAD_DIM, HEAD_DIM), F32)],
        compiler_params=_cparams(("arbitrary",)),
    )(qkv, pm, pba, alog, dtb, onorm, states, tinvs, dcat)


def gmlp_block(ln_g, ln_b, w, bcol, u_raw, v_raw):
    n = w.shape[-1]
    ri = lax.broadcasted_iota(jnp.int32, (n, n), 0)
    ci = lax.broadcasted_iota(jnp.int32, (n, n), 1)
    mask = ((ci // CHUNK) <= (ri // CHUNK)).astype(F32)
    vg = _gelu(v_raw)
    vc = vg - jnp.mean(vg, axis=-1, keepdims=True)
    vgn = vc * lax.rsqrt(jnp.mean(vc * vc, axis=-1, keepdims=True) + EPS) * ln_g + ln_b
    return _gelu(u_raw) * (_dot(w * mask, vgn) + bcol)


def gmlp_fwd(pm, ln_g, ln_b, w_s, bcol, cat, *, name):
    t = pm.shape[0]
    g = w_s.shape[0]
    gw = g * HEAD_DIM
    assert pm.shape[1] == 6 * gw and cat.shape[1] == 2 * gw

    def body(u_ref, v_ref, lg_ref, lb_ref, w_ref, b_ref, cat_in, o_ref):
        del cat_in
        out = gmlp_block(lg_ref[...], lb_ref[...], w_ref[...], b_ref[...], _heads(u_ref, 0, g), _heads(v_ref, 0, g))
        for i in range(g):
            o_ref[:, _hcols(i)] = out[i].astype(o_ref.dtype)

    full = lambda a: pl.BlockSpec(a.shape, lambda m: (0, 0, 0))
    return pl.pallas_call(
        body, name=name, grid=(t // HEAD_DIM,),
        out_shape=jax.ShapeDtypeStruct(cat.shape, cat.dtype),
        in_specs=[pl.BlockSpec((HEAD_DIM, gw), lambda m: (m, 4)), pl.BlockSpec((HEAD_DIM, gw), lambda m: (m, 5)),
                  full(ln_g), full(ln_b), full(w_s), full(bcol), pl.BlockSpec(memory_space=pl.ANY)],
        out_specs=pl.BlockSpec((HEAD_DIM, gw), lambda m: (m, 1)),
        input_output_aliases={6: 0},
        compiler_params=_cparams(("arbitrary",)),
    )(pm, pm, ln_g, ln_b, w_s, bcol, cat)


def gmlp_bwd(pm, ln_g, ln_b, w_s, bcol, dcat, dpm, *, name):
    t = pm.shape[0]
    g = w_s.shape[0]
    gw = g * HEAD_DIM
    assert pm.shape[1] == 6 * gw and dpm.shape == pm.shape

    def body(u_ref, v_ref, lg_ref, lb_ref, w_ref, b_ref, dc_ref, dpm_in, duv_ref, dlg_ref, dlb_ref, dw_ref, db_ref):
        del dpm_in
        first = pl.program_id(0) == 0
        _, vjp = jax.vjp(gmlp_block, lg_ref[...], lb_ref[...], w_ref[...], b_ref[...], _heads(u_ref, 0, g), _heads(v_ref, 0, g))
        dlg, dlb, dw, db, du, dv = vjp(_heads(dc_ref, 0, g))
        for i in range(g):
            duv_ref[:, _hcols(i)] = du[i].astype(duv_ref.dtype)
            duv_ref[:, _hcols(g + i)] = dv[i].astype(duv_ref.dtype)
        for ref, val in ((dlg_ref, dlg), (dlb_ref, dlb), (dw_ref, dw), (db_ref, db)):
            @pl.when(first)
            def _(ref=ref, val=val):
                ref[...] = val

            @pl.when(jnp.logical_not(first))
            def _(ref=ref, val=val):
                ref[...] += val

    full = lambda a: pl.BlockSpec(a.shape, lambda m: (0, 0, 0))
    return pl.pallas_call(
        body, name=name, grid=(t // HEAD_DIM,),
        out_shape=(jax.ShapeDtypeStruct(dpm.shape, dpm.dtype),
                   jax.ShapeDtypeStruct(ln_g.shape, F32), jax.ShapeDtypeStruct(ln_b.shape, F32),
                   jax.ShapeDtypeStruct(w_s.shape, F32), jax.ShapeDtypeStruct(bcol.shape, F32)),
        in_specs=[pl.BlockSpec((HEAD_DIM, gw), lambda m: (m, 4)), pl.BlockSpec((HEAD_DIM, gw), lambda m: (m, 5)),
                  full(ln_g), full(ln_b), full(w_s), full(bcol), pl.BlockSpec((HEAD_DIM, gw), lambda m: (m, 1)),
                  pl.BlockSpec(memory_space=pl.ANY)],
        out_specs=(pl.BlockSpec((HEAD_DIM, 2 * gw), lambda m: (m, 2)), full(ln_g), full(ln_b), full(w_s), full(bcol)),
        input_output_aliases={7: 0},
        compiler_params=_cparams(("arbitrary",)),
    )(pm, pm, ln_g, ln_b, w_s, bcol, dcat, dpm)


def loss_head(g, x, r, tgt, *, name, rows=ROWS):
    t, d = x.shape

    def body(g_ref, x_ref, r_ref, t_ref, l_ref, dx_ref, dxb_ref, dg_ref):
        y, vjp = jax.vjp(lambda gg, xx: _rms(xx, gg), g_ref[...], x_ref[...] + r_ref[...])
        e = y - t_ref[...]
        part = (0.5 / d) * jnp.sum(jnp.sum(e * e, axis=1, keepdims=True), axis=0, keepdims=True)
        dg, dx = vjp(e * (1.0 / d))
        dx_ref[...] = dx
        dxb_ref[...] = dx.astype(BF16)
        first = pl.program_id(0) == 0

        @pl.when(first)
        def _():
            l_ref[...] = part
            dg_ref[...] = dg

        @pl.when(jnp.logical_not(first))
        def _():
            l_ref[...] += part
            dg_ref[...] += dg

    rs = pl.BlockSpec((rows, d), lambda i: (i, 0))
    gs = pl.BlockSpec((1, d), lambda i: (0, 0))
    return pl.pallas_call(
        body, name=name, grid=(t // rows,),
        out_shape=(jax.ShapeDtypeStruct((1, 1), F32), jax.ShapeDtypeStruct((t, d), F32),
                   jax.ShapeDtypeStruct((t, d), BF16), jax.ShapeDtypeStruct((1, d), F32)),
        in_specs=[gs, rs, rs, rs],
        out_specs=(pl.BlockSpec((1, 1), lambda i: (0, 0)), rs, rs, gs),
        compiler_params=_cparams(("arbitrary",)),
    )(g, x, r, tgt)


def adamw(w, gs, m, v, *, name, behind=None):
    nl, r, c = w.shape
    assert len(gs) == nl
    if r % SUBLANES == 0:
        tr, tc = _pick(r, (256, 128, 64, 32, 16, 8)), c
    else:
        tr, tc = r, _pick(c, (256, 128))
    k1 = 1.0 - ADAM_B1 ** ADAM_STEP
    k2 = 1.0 - ADAM_B2 ** ADAM_STEP

    def body(*refs):
        w_ref, m_ref, v_ref = refs[0], refs[1], refs[2]
        g_refs = refs[3:3 + nl]
        go_ref, d_ref, mo_ref, vo_ref = refs[-4:]
        gg = g_refs[0][...]
        for li in range(1, nl):
            gg = jnp.where(pl.program_id(0) == li, g_refs[li][...], gg)
        mn = ADAM_B1 * m_ref[...] + (1.0 - ADAM_B1) * gg
        vn = ADAM_B2 * v_ref[...] + (1.0 - ADAM_B2) * (gg * gg)
        go_ref[...] = gg
        d_ref[...] = -ADAM_LR * ((mn / k1) / (jnp.sqrt(vn / k2) + ADAM_EPS) + ADAM_WD * w_ref[...])
        mo_ref[...] = mn
        vo_ref[...] = vn

    if tc == c:
        sp = pl.BlockSpec((None, tr, c), lambda l, i: (l, i, 0))
        gsp = pl.BlockSpec((tr, c), lambda l, i: (i, 0))
    else:
        sp = pl.BlockSpec((None, r, tc), lambda l, i: (l, 0, i))
        gsp = pl.BlockSpec((r, tc), lambda l, i: (0, i))
    sds = jax.ShapeDtypeStruct((nl, r, c), F32)
    extra = [] if behind is None else [behind]
    return pl.pallas_call(
        body, name=name, grid=(nl, (r // tr) * (c // tc)), out_shape=(sds, sds, sds, sds),
        in_specs=[sp, sp, sp] + [gsp] * nl + [pl.BlockSpec(e.shape, lambda l, i: (0, 0)) for e in extra], out_specs=(sp, sp, sp, sp),
        compiler_params=_cparams(("parallel", "parallel")),
    )(w, m, v, *gs, *extra)


MESH = pl.DeviceIdType.MESH
ANY = pl.BlockSpec(memory_space=pl.ANY)


def _me():
    return lax.axis_index("x"), lax.axis_index("y"), lax.axis_index("c")


def _other_chips(x, y):
    return [(1 - x, y), (x, 1 - y), (1 - x, 1 - y)]


def gather_shards(arrs, *, name):
    n = len(arrs)
    per = 7

    def body(*refs):
        ins, outs = refs[:n], refs[n:2 * n]
        send_sems, recv_sems = refs[2 * n:]
        x, y, c = _me()
        chip = 2 * x + y
        sib = (x, y, 1 - c)
        chips = _other_chips(x, y)

        def rc(i, k, src, dst, to):
            return pltpu.make_async_remote_copy(src_ref=src, dst_ref=dst, send_sem=send_sems.at[per * i + k],
                                                recv_sem=recv_sems.at[per * i + k], device_id=to, device_id_type=MESH)

        def half(i, which):
            hrows = arrs[i].shape[0] // 2
            return pl.ds(which * hrows, hrows)

        own = [rc(i, 6, ins[i], outs[i].at[chip], sib) for i in range(n)]
        for cp in own:
            cp.start()
        started = []
        for i in range(n):
            for j, ch in enumerate(chips):
                cp = rc(i, j, ins[i].at[half(i, c)], outs[i].at[chip, half(i, c)], (ch[0], ch[1], c))
                cp.start()
                started.append(cp)
        for i in range(n):
            for j, ch in enumerate(chips):
                sc = 2 * ch[0] + ch[1]
                landed = outs[i].at[sc, half(i, c)]
                rc(i, j, ins[i].at[half(i, c)], landed, (ch[0], ch[1], c)).wait_recv()
                fw = rc(i, 3 + j, landed, landed, sib)
                fw.start()
                started.append(fw)
        for i in range(n):
            for j, ch in enumerate(chips):
                sc = 2 * ch[0] + ch[1]
                theirs = outs[i].at[sc, half(i, 1 - c)]
                rc(i, 3 + j, theirs, theirs, sib).wait_recv()
        for cp in own:
            cp.wait_recv()
        for cp in started + own:
            cp.wait_send()

    return pl.pallas_call(
        body, name=name,
        out_shape=tuple(jax.ShapeDtypeStruct((N_CHIPS,) + a.shape, a.dtype) for a in arrs),
        in_specs=[ANY] * n, out_specs=tuple([ANY] * n),
        scratch_shapes=[pltpu.SemaphoreType.DMA((per * n,)), pltpu.SemaphoreType.DMA((per * n,))],
        compiler_params=pltpu.CompilerParams(has_side_effects=True),
    )(*arrs)


HBM = pl.BlockSpec(memory_space=pltpu.HBM)
SEM = pl.BlockSpec(memory_space=pltpu.SEMAPHORE)
EFFECT = pltpu.SideEffectType.DATAFLOW_SIDE_EFFECTING
def _hbm(a):
    return pltpu.with_memory_space_constraint(a, pltpu.HBM)


def _half_rows(arr, which):
    hrows = arr.shape[-2] // 2
    return pl.ds(which * hrows, hrows)


def _ici_copies(srcs, lands, sems):
    x, y, c = _me()
    chip = 2 * x + y
    return [pltpu.make_async_remote_copy(src_ref=srcs[i].at[_half_rows(srcs[i], c)], dst_ref=lands[i].at[chip, _half_rows(srcs[i], c)],
                                         send_sem=sems[0].at[3 * i + j], recv_sem=sems[1].at[3 * i + j],
                                         device_id=(ch[0], ch[1], c), device_id_type=MESH)
            for i in range(len(srcs)) for j, ch in enumerate(_other_chips(x, y))]


def _own_copies(srcs, lands, sems):
    x, y, c = _me()
    return [pltpu.make_async_remote_copy(src_ref=srcs[i], dst_ref=lands[i].at[2 * x + y], send_sem=sems[0].at[i],
                                         recv_sem=sems[1].at[i], device_id=(x, y, 1 - c), device_id_type=MESH)
            for i in range(len(srcs))]


def _fwd_copies(srcs, lands, sems, half):
    x, y, c = _me()
    cps = []
    for i in range(len(srcs)):
        for j, ch in enumerate(_other_chips(x, y)):
            part = lands[i].at[2 * ch[0] + ch[1], _half_rows(srcs[i], half)]
            cps.append(pltpu.make_async_remote_copy(src_ref=part, dst_ref=part, send_sem=sems[0].at[3 * i + j],
                                                    recv_sem=sems[1].at[3 * i + j], device_id=(x, y, 1 - c), device_id_type=MESH))
    return cps


def gather_start(groups, *, name):
    arrs = [a for g in groups for a in g]
    n, ng = len(arrs), len(groups)
    bounds = np.cumsum([0] + [len(g) for g in groups])

    def body(*refs):
        srcs, lands = refs[:n], refs[n:2 * n]
        sems = refs[2 * n:2 * n + 4 * ng]
        token = refs[-1]
        for gi in range(ng):
            lo, hi = bounds[gi], bounds[gi + 1]
            for cp in _ici_copies(srcs[lo:hi], lands[lo:hi], sems[4 * gi:4 * gi + 2]):
                cp.start()
        for gi in range(ng):
            lo, hi = bounds[gi], bounds[gi + 1]
            for cp in _own_copies(srcs[lo:hi], lands[lo:hi], sems[4 * gi + 2:4 * gi + 4]):
                cp.start()
        token[...] = jnp.zeros_like(token)

    sem_shapes = []
    for g in groups:
        sem_shapes += [pltpu.SemaphoreType.DMA((3 * len(g),))] * 2 + [pltpu.SemaphoreType.DMA((len(g),))] * 2
    res = pl.pallas_call(
        body, name=name,
        out_shape=tuple(sem_shapes) + tuple(pltpu.HBM(a.shape, a.dtype) for a in arrs)
        + tuple(pltpu.HBM((N_CHIPS,) + a.shape, a.dtype) for a in arrs) + (jax.ShapeDtypeStruct((SUBLANES, LANES), F32),),
        in_specs=[HBM] * (2 * n),
        out_specs=tuple([SEM] * (4 * ng)) + tuple([HBM] * (2 * n)) + (pl.BlockSpec(memory_space=pltpu.VMEM),),
        input_output_aliases={i: 4 * ng + i for i in range(2 * n)},
        compiler_params=pltpu.CompilerParams(has_side_effects=EFFECT),
    )(*[_hbm(a) for a in arrs], *[_hbm(lax.empty((N_CHIPS,) + a.shape, a.dtype)) for a in arrs])
    sems, thru, lands, token = res[:4 * ng], res[4 * ng:4 * ng + n], res[4 * ng + n:4 * ng + 2 * n], res[-1]
    out = [{"ici": (sems[4 * gi], sems[4 * gi + 1]), "own": (sems[4 * gi + 2], sems[4 * gi + 3]),
            "thru": list(thru[bounds[gi]:bounds[gi + 1]]), "lands": list(lands[bounds[gi]:bounds[gi + 1]])} for gi in range(ng)]
    return out, token


def gather_forward(group, after, *, name):
    thru, lands = group["thru"], group["lands"]
    n = len(thru)

    def body(*refs):
        srcs, lands_r = refs[:n], refs[n:2 * n]
        ici = refs[2 * n:2 * n + 2]
        fwd = refs[2 * n + 3 + 2 * n:2 * n + 3 + 2 * n + 2]
        for cp in _ici_copies(srcs, lands_r, ici):
            cp.wait_send()
            cp.wait_recv()
        for cp in _fwd_copies(srcs, lands_r, fwd, _me()[2]):
            cp.start()

    res = pl.pallas_call(
        body, name=name,
        out_shape=tuple(pltpu.HBM(a.shape, a.dtype) for a in thru) + tuple(pltpu.HBM(a.shape, a.dtype) for a in lands)
        + (pltpu.SemaphoreType.DMA((3 * n,)),) * 2,
        in_specs=[HBM] * (2 * n) + [SEM, SEM, pl.BlockSpec(memory_space=pl.ANY)],
        out_specs=tuple([HBM] * (2 * n)) + (SEM, SEM),
        input_output_aliases={i: i for i in range(2 * n)},
        compiler_params=pltpu.CompilerParams(has_side_effects=EFFECT),
    )(*thru, *lands, *group["ici"], after)
    return {"own": group["own"], "fwd": (res[2 * n], res[2 * n + 1]), "thru": list(res[:n]), "lands": list(res[n:2 * n])}


def gather_wait(group, after, *, name):
    thru, lands = group["thru"], group["lands"]
    n = len(thru)

    def body(*refs):
        srcs, lands_r = refs[:n], refs[n:2 * n]
        own, fwd = refs[2 * n:2 * n + 2], refs[2 * n + 2:2 * n + 4]
        c = _me()[2]
        for mine, theirs in zip(_fwd_copies(srcs, lands_r, fwd, c), _fwd_copies(srcs, lands_r, fwd, 1 - c)):
            mine.wait_send()
            theirs.wait_recv()
        for cp in _own_copies(srcs, lands_r, own):
            cp.wait_send()
            cp.wait_recv()

    res = pl.pallas_call(
        body, name=name,
        out_shape=tuple(pltpu.HBM(a.shape, a.dtype) for a in thru) + tuple(pltpu.HBM(a.shape, a.dtype) for a in lands),
        in_specs=[HBM] * (2 * n) + [SEM] * 4 + [pl.BlockSpec(memory_space=pl.ANY)],
        out_specs=tuple([HBM] * (2 * n)),
        input_output_aliases={i: i for i in range(2 * n)},
        compiler_params=pltpu.CompilerParams(has_side_effects=EFFECT),
    )(*thru, *lands, *group["own"], *group["fwd"], after)
    return list(res[n:])


def _pair_copies(srcs, lands, sems):
    x, y, c = _me()
    return [pltpu.make_async_remote_copy(src_ref=srcs[i].at[:, _half_rows(srcs[i], 1 - c), :], dst_ref=lands[i], send_sem=sems[0].at[i],
                                         recv_sem=sems[1].at[i], device_id=(x, y, 1 - c), device_id_type=MESH)
            for i in range(len(srcs))]


def _scatter_copies(srcs, lands, sems):
    x, y, c = _me()
    return [pltpu.make_async_remote_copy(src_ref=srcs[i].at[2 * ch[0] + ch[1]], dst_ref=lands[i].at[j], send_sem=sems[0].at[3 * i + j],
                                         recv_sem=sems[1].at[3 * i + j], device_id=(ch[0], ch[1], c), device_id_type=MESH)
            for i in range(len(srcs)) for j, ch in enumerate(_other_chips(x, y))]


def split_start(copies, arrs, land_shapes, nsem, *, name):
    n = len(arrs)

    def body(*refs):
        for cp in copies(refs[:n], refs[n:2 * n], refs[2 * n:2 * n + 2]):
            cp.start()
        refs[-1][...] = jnp.zeros_like(refs[-1])

    res = pl.pallas_call(
        body, name=name,
        out_shape=(pltpu.SemaphoreType.DMA((nsem,)),) * 2 + tuple(pltpu.HBM(a.shape, a.dtype) for a in arrs)
        + tuple(pltpu.HBM(s, a.dtype) for s, a in zip(land_shapes, arrs)) + (jax.ShapeDtypeStruct((SUBLANES, LANES), F32),),
        in_specs=[HBM] * (2 * n), out_specs=(SEM, SEM) + tuple([HBM] * (2 * n)) + (pl.BlockSpec(memory_space=pltpu.VMEM),),
        input_output_aliases={i: 2 + i for i in range(2 * n)},
        compiler_params=pltpu.CompilerParams(has_side_effects=EFFECT),
    )(*[_hbm(a) for a in arrs], *[_hbm(lax.empty(s, a.dtype)) for s, a in zip(land_shapes, arrs)])
    return {"copies": copies, "sems": (res[0], res[1]), "thru": list(res[2:2 + n]), "lands": list(res[2 + n:2 + 2 * n]),
            "token": res[-1]}


def split_wait(state, after, *, name):
    thru, lands, copies = state["thru"], state["lands"], state["copies"]
    n = len(thru)

    def body(*refs):
        for cp in copies(refs[:n], refs[n:2 * n], refs[2 * n:2 * n + 2]):
            cp.wait_send()
            cp.wait_recv()

    res = pl.pallas_call(
        body, name=name,
        out_shape=tuple(pltpu.HBM(a.shape, a.dtype) for a in thru) + tuple(pltpu.HBM(a.shape, a.dtype) for a in lands),
        in_specs=[HBM] * (2 * n) + [SEM, SEM, pl.BlockSpec(memory_space=pl.ANY)], out_specs=tuple([HBM] * (2 * n)),
        input_output_aliases={i: i for i in range(2 * n)},
        compiler_params=pltpu.CompilerParams(has_side_effects=EFFECT),
    )(*thru, *lands, *state["sems"], after)
    return list(res[:n]), list(res[n:])


def sibling_send_other_half(arrs, *, name):
    n = len(arrs)

    def body(*refs):
        ins, outs = refs[:n], refs[n:2 * n]
        send_sems, recv_sems = refs[2 * n:]
        x, y, c = _me()
        cps = []
        for i in range(n):
            hrows = arrs[i].shape[1] // 2
            cps.append(pltpu.make_async_remote_copy(
                src_ref=ins[i].at[:, pl.ds((1 - c) * hrows, hrows), :], dst_ref=outs[i], send_sem=send_sems.at[i],
                recv_sem=recv_sems.at[i], device_id=(x, y, 1 - c), device_id_type=MESH))
        for cp in cps:
            cp.start()
        for cp in cps:
            cp.wait_recv()
        for cp in cps:
            cp.wait_send()

    return pl.pallas_call(
        body, name=name,
        out_shape=tuple(jax.ShapeDtypeStruct((a.shape[0], a.shape[1] // 2, a.shape[2]), a.dtype) for a in arrs),
        in_specs=[ANY] * n, out_specs=tuple([ANY] * n),
        scratch_shapes=[pltpu.SemaphoreType.DMA((n,)), pltpu.SemaphoreType.DMA((n,))],
        compiler_params=pltpu.CompilerParams(has_side_effects=True),
    )(*arrs)


def chip_scatter(arrs, *, name):
    n = len(arrs)

    def body(*refs):
        ins, outs = refs[:n], refs[n:2 * n]
        send_sems, recv_sems = refs[2 * n:]
        x, y, c = _me()
        cps = []
        for i in range(n):
            for j, ch in enumerate(_other_chips(x, y)):
                cps.append(pltpu.make_async_remote_copy(
                    src_ref=ins[i].at[2 * ch[0] + ch[1]], dst_ref=outs[i].at[j], send_sem=send_sems.at[3 * i + j],
                    recv_sem=recv_sems.at[3 * i + j], device_id=(ch[0], ch[1], c), device_id_type=MESH))
        for cp in cps:
            cp.start()
        for cp in cps:
            cp.wait_recv()
        for cp in cps:
            cp.wait_send()

    return pl.pallas_call(
        body, name=name,
        out_shape=tuple(jax.ShapeDtypeStruct((3,) + a.shape[1:], a.dtype) for a in arrs),
        in_specs=[ANY] * n, out_specs=tuple([ANY] * n),
        scratch_shapes=[pltpu.SemaphoreType.DMA((3 * n,)), pltpu.SemaphoreType.DMA((3 * n,))],
        compiler_params=pltpu.CompilerParams(has_side_effects=True),
    )(*arrs)


def sibling_join_halves(arrs, *, name):
    n = len(arrs)

    def body(*refs):
        outs = refs[n:2 * n]
        send_sems, recv_sems = refs[2 * n:]
        x, y, c = _me()
        cps = []
        for i in range(n):
            hrows = arrs[i].shape[0] // 2
            mine = outs[i].at[pl.ds(c * hrows, hrows)]
            cps.append(pltpu.make_async_remote_copy(src_ref=mine, dst_ref=mine, send_sem=send_sems.at[i],
                                                    recv_sem=recv_sems.at[i], device_id=(x, y, 1 - c), device_id_type=MESH))
        for cp in cps:
            cp.start()
        for i in range(n):
            hrows = arrs[i].shape[0] // 2
            theirs = outs[i].at[pl.ds((1 - c) * hrows, hrows)]
            pltpu.make_async_remote_copy(src_ref=theirs, dst_ref=theirs, send_sem=send_sems.at[i], recv_sem=recv_sems.at[i],
                                         device_id=(x, y, 1 - c), device_id_type=MESH).wait_recv()
        for cp in cps:
            cp.wait_send()

    return pl.pallas_call(
        body, name=name,
        out_shape=tuple(jax.ShapeDtypeStruct(a.shape, a.dtype) for a in arrs),
        in_specs=[ANY] * n, out_specs=tuple([ANY] * n),
        input_output_aliases={i: i for i in range(n)},
        scratch_shapes=[pltpu.SemaphoreType.DMA((n,)), pltpu.SemaphoreType.DMA((n,))],
        compiler_params=pltpu.CompilerParams(has_side_effects=True),
    )(*arrs)


def pre_reduce(gd, sib, who, *, name):
    _, r, c = gd.shape
    h = r // 2
    tr = _pick(h, (256, 128, 64, 32, 16, 8))
    nrb = h // tr

    def body(who_ref, gd_ref, sib_ref, pb_ref, own_ref):
        p = gd_ref[...] + sib_ref[...]
        pb_ref[...] = p.astype(BF16)

        @pl.when(pl.program_id(1) == who_ref[1])
        def _():
            own_ref[...] = p

    return pl.pallas_call(
        body, name=name,
        out_shape=(jax.ShapeDtypeStruct((N_CHIPS, h, c), BF16), jax.ShapeDtypeStruct((h, c), F32)),
        grid_spec=pltpu.PrefetchScalarGridSpec(
            num_scalar_prefetch=1, grid=(nrb, N_CHIPS),
            in_specs=[pl.BlockSpec((None, tr, c), lambda i, s, w: (s, w[0] * nrb + i, 0)),
                      pl.BlockSpec((None, tr, c), lambda i, s, w: (s, i, 0))],
            out_specs=(pl.BlockSpec((None, tr, c), lambda i, s, w: (s, i, 0)),
                       pl.BlockSpec((tr, c), lambda i, s, w: (i, 0)))),
        compiler_params=_cparams(("parallel", "arbitrary")),
    )(who, gd, sib)


def final_reduce(own, rcv, who, *, name):
    h, c = own.shape
    tr = _pick(h, (256, 128, 64, 32, 16, 8))
    nrb = h // tr

    def body(who_ref, own_ref, rcv_ref, o_ref):
        del who_ref
        acc = own_ref[...]
        for j in range(3):
            acc = acc + rcv_ref[j].astype(F32)
        o_ref[...] = acc

    return pl.pallas_call(
        body, name=name, out_shape=jax.ShapeDtypeStruct((2 * h, c), F32),
        grid_spec=pltpu.PrefetchScalarGridSpec(
            num_scalar_prefetch=1, grid=(nrb,),
            in_specs=[pl.BlockSpec((tr, c), lambda i, w: (i, 0)), pl.BlockSpec((3, tr, c), lambda i, w: (0, i, 0))],
            out_specs=pl.BlockSpec((tr, c), lambda i, w: (w[0] * nrb + i, 0))),
        compiler_params=_cparams(("parallel",)),
    )(who, own, rcv)


def sum8(landed, own, me, *, name):
    _, r, c = landed.shape
    tr = _pick(r, (256, 128, 64, 32, 16, 8))

    def body(me_ref, a_ref, own_ref, o_ref):
        acc = None
        for j in range(8):
            term = jnp.where(me_ref[0] == j, own_ref[...], a_ref[j])
            acc = term if acc is None else acc + term
        o_ref[...] = acc

    return pl.pallas_call(
        body, name=name, out_shape=jax.ShapeDtypeStruct((r, c), F32),
        grid_spec=pltpu.PrefetchScalarGridSpec(
            num_scalar_prefetch=1, grid=(r // tr,),
            in_specs=[pl.BlockSpec((8, tr, c), lambda i, w: (0, i, 0)), pl.BlockSpec((tr, c), lambda i, w: (i, 0))],
            out_specs=pl.BlockSpec((tr, c), lambda i, w: (i, 0))),
        compiler_params=_cparams(("parallel",)),
    )(me, landed, own)


def _everyone_copies(srcs, lands, sems):
    x, y, c = _me()
    me = 4 * x + 2 * y + c
    cps = []
    for k in range(1, 8):
        fx, fy, fc = (k >> 2) & 1, (k >> 1) & 1, k & 1
        to = (x + fx - 2 * x * fx, y + fy - 2 * y * fy, c + fc - 2 * c * fc)
        cps.append(pltpu.make_async_remote_copy(src_ref=srcs[0], dst_ref=lands[0].at[me], send_sem=sems[0].at[k - 1],
                                                recv_sem=sems[1].at[k - 1], device_id=to, device_id_type=MESH))
    return cps


def _behind(p, *tokens):
    for tk in tokens:
        if tk is not None:
            p = p + tk[0, 0]
    return p


def local_step(x, tgt, wb, ws, arrive=lambda layer, event, after: {}, reduce=lambda group, event, gb, after: None):
    t, d = x.shape
    nh = ws["alog"].shape[0]
    ng = ws["w_s"].shape[0]
    assert nh == ng
    mix_w = (nh + ng) * HEAD_DIM

    (h0,) = rows_fwd(stage_norm, [ws["e_norm"]], [x], [(1, BF16, d)], name="f_norm_e")
    pm = mm_nt(h0, wb["w_main_t"], name="f_proj_main")
    pba = mm_nt(h0, wb["w_ba_t"], name="f_proj_ba")
    qkv = conv_fwd(pm, ws["conv_w"], mode="silu", name="f_conv4")
    arrive("layer0", "landed", qkv)
    cat, states, tinvs = delta_fwd(qkv, pm, pba, ws["alog"], ws["dtb"], ws["onorm"], mix_w, name="f_delta")
    cat = gmlp_fwd(pm, ws["lng"], ws["lnb"], ws["w_s"], ws["bs"], cat, name="f_gmlp")
    wb = {**wb, **arrive("layer0", "joined", cat)}
    y0 = mm_nn(cat, wb["w_out"], name="f_out")
    x1, h1 = rows_fwd(stage_res_norm, [ws["f_norm0"]], [x, y0], [(0, F32, d), (1, BF16, d)], name="f_norm_f0")
    a1, s1 = mm_nn(h1, wb["w1_0"], name="f_mlp0_up", epilogue="relu2")
    arrive("layer1", "landed", s1)
    y1 = mm_nn(s1, wb["w2_0"], name="f_mlp0_down")
    x2, h2 = rows_fwd(stage_res_norm, [ws["o_norm"]], [x1, y1], [(0, F32, d), (1, BF16, d)], name="f_norm_o")
    wb = {**wb, **arrive("layer1", "joined", h2)}
    zz = mm_nn(h2, wb["pw1"], name="f_pw1")
    zparts = [(zz, d, 0), (zz, d, 1)]
    (gl,) = rows_fwd(stage_glu, [ws["b1a"], ws["b1b"]], zparts, [(0, F32, d)], name="f_glu")
    cv = conv_fwd(gl, ws["dw"], mode="plain", name="f_conv31")
    ln_params = [ws["dw_b"], ws["ln_g"], ws["ln_b"]]
    (sl,) = rows_fwd(stage_ln_silu, ln_params, [cv], [(0, BF16, d)], name="f_ln_silu")
    y2 = mm_nn(sl, wb["pw2"], name="f_pw2")
    x3, h3 = rows_fwd(stage_res_bias_norm, [ws["b2"], ws["f_norm1"]], [x2, y2], [(0, F32, d), (1, BF16, d)], name="f_norm_f1")
    a3, s3 = mm_nn(h3, wb["w1_1"], name="f_mlp1_up", epilogue="relu2")
    y3 = mm_nn(s3, wb["w2_1"], name="f_mlp1_down")
    loss, d4, d4b, g_final = loss_head(ws["final_norm"], x3, y3, tgt, name="loss_head")

    gb, gs = {}, {"final_norm": g_final}
    s_up = wb["w1_0"].shape[0]

    gb["w2_1"] = mm_tn(s3, d4b, groups=1, name="b_mlp1_down_w")
    dpre3 = mm_nt(d4b, wb["w2_1"], name="b_mlp1_down_x", mul=a3, out_dtype=BF16)
    gb["w1_1"] = mm_tn(h3, dpre3, groups=s_up, name="b_mlp1_up_w")
    tok = reduce("mlp1", "ready", gb, None)
    dh3 = mm_nt(dpre3, wb["w1_1"], name="b_mlp1_up_x")
    d3, d3b, gs["b2"], gs["f_norm1"] = rows_bwd(stage_res_bias_norm, [ws["b2"], _behind(ws["f_norm1"], tok)], [x2, y2], [d4, dh3],
                                                [(0, F32), (1, BF16)], name="b_norm_f1")
    tok = reduce("mlp1", "paired", gb, d3)
    gb["pw2"] = mm_tn(sl, d3b, groups=1, name="b_pw2_w")
    dsl = mm_nt(d3b, wb["pw2"], name="b_pw2_x")
    dcv, gs["dw_b"], gs["ln_g"], gs["ln_b"] = rows_bwd(stage_ln_silu, [_behind(ln_params[0], tok)] + ln_params[1:], [cv], [dsl],
                                                       [(0, F32)], name="b_ln_silu")
    dgl, gs["dw"] = conv_bwd(gl, ws["dw"], dcv, name="b_conv31")
    dzz, gs["b1a"], gs["b1b"] = rows_bwd(stage_glu, [ws["b1a"], ws["b1b"]], zparts, [dgl], [(0, BF16), (1, BF16)],
                                         name="b_glu", concat=True)
    gb["pw1"] = mm_tn(h2, dzz, groups=wb["pw1"].shape[0], name="b_pw1_w")
    tok = reduce("conv", "ready", gb, None)
    dh2 = mm_nt(dzz, wb["pw1"], name="b_pw1_x")
    d2, d2b, gs["o_norm"] = rows_bwd(stage_res_norm, [_behind(ws["o_norm"], tok)], [x1, y1], [d3, dh2], [(0, F32), (1, BF16)],
                                     name="b_norm_o")
    tok_conv = reduce("conv", "paired", gb, d2)
    gb["w2_0"] = mm_tn(s1, d2b, groups=1, name="b_mlp0_down_w")
    dpre1 = mm_nt(d2b, wb["w2_0"], name="b_mlp0_down_x", mul=a1, out_dtype=BF16)
    gb["w1_0"] = mm_tn(h1, dpre1, groups=s_up, name="b_mlp0_up_w")
    tok = reduce("mlp0", "ready", gb, None)
    dh1 = mm_nt(dpre1, wb["w1_0"], name="b_mlp0_up_x")
    d1, d1b, gs["f_norm0"] = rows_bwd(stage_res_norm, [_behind(ws["f_norm0"], tok_conv, tok)], [x, y0], [d2, dh1],
                                      [(0, F32), (1, BF16)], name="b_norm_f0")
    tok_mlp0 = reduce("mlp0", "paired", gb, d1)
    gb["w_out"] = mm_tn(cat, d1b, groups=1, name="b_out_w")
    tok = reduce("mixer", "ready", gb, None)
    dcat = mm_nt(d1b, wb["w_out"], name="b_out_x")
    dqkv_c, dpm, dpba, gs["alog"], gs["dtb"], gs["onorm"] = delta_bwd(
        qkv, pm, pba, ws["alog"], ws["dtb"], _behind(ws["onorm"], tok_mlp0, tok), states, tinvs, dcat, name="b_delta")
    tok = reduce("mixer", "paired", gb, dqkv_c)
    dpm, gs["lng"], gs["lnb"], gs["w_s"], gs["bs"] = gmlp_bwd(
        pm, ws["lng"], ws["lnb"], ws["w_s"], ws["bs"], dcat, dpm, name="b_gmlp")
    conv_w = _behind(ws["conv_w"], tok)
    dconv = conv_fwd(pm, conv_w, mode="silu_bwd", g=dqkv_c, name="b_conv4_silu")
    dpm, gs["conv_w"] = conv_bwd(pm, conv_w, dconv, name="b_conv4", into=dpm)
    gb["w_main_t"] = mm_tn(dpm, h0, groups=1, name="b_proj_main_w")
    gb["w_ba_t"] = mm_tn(dpba, h0, groups=1, name="b_proj_ba_w")
    dh0 = mm_nn(dpm, wb["w_main_t"], name="b_proj_main_x")
    dh0 = mm_nn(dpba, wb["w_ba_t"], name="b_proj_ba_x", add=dh0)
    grad_x, gs["e_norm"] = rows_bwd(stage_norm, [ws["e_norm"]], [x], [d1, dh0], [(0, F32)], name="b_norm_e")
    return loss, grad_x, gb, gs


WEIGHTS = ["e_norm", "e_w_in", "e_conv_w", "e_a_log", "e_dt_bias", "e_o_norm", "e_ln_g", "e_ln_b", "e_w_s", "e_b_s", "e_w_out",
           "o_norm", "o_pw1", "o_pw1_b", "o_dw", "o_dw_b", "o_ln_g", "o_ln_b", "o_pw2", "o_pw2_b", "f_norm", "f_w1", "f_w2",
           "final_norm"]
BIG = ["e_w_in", "e_w_out", "o_pw1", "o_pw2", "f_w1", "f_w2"]
SMALL_SHARDED = ["e_conv_w", "o_norm", "o_pw1_b", "o_dw", "o_dw_b", "o_ln_g", "o_ln_b", "o_pw2_b"]
SMALL = [n for n in WEIGHTS if n not in BIG]
LANES = 128
PACK_ROWS = 16
IN_ROW_MULT = 256
REDUCE_GROUPS = {"mlp1": ("w2_1", "w1_1"), "conv": ("pw2", "pw1"), "mlp0": ("w2_0", "w1_0"), "mixer": ("w_out",)}


def _pack(arrs):
    flat = jnp.concatenate([a.reshape(-1).astype(F32) for a in arrs])
    n = flat.shape[0]
    rows = -(-n // (LANES * PACK_ROWS)) * PACK_ROWS
    return jnp.pad(flat, (0, rows * LANES - n)).reshape(rows, LANES)


def _unpack(flat, shapes, lead=()):
    outs, off = [], 0
    for s in shapes:
        n = int(np.prod(s))
        outs.append(flat[..., off:off + n].reshape(lead + tuple(s)))
        off += n
    return outs


def _chip_major(g, s=N_CHIPS):
    k, n = g.shape
    return g.reshape(k, s, n // s).transpose(1, 0, 2)


def kernel(x, e_norm, e_w_in, e_conv_w, e_a_log, e_dt_bias, e_o_norm, e_ln_g, e_ln_b, e_w_s, e_b_s, e_w_out, o_norm, o_pw1, o_pw1_b, o_dw, o_dw_b, o_ln_g, o_ln_b, o_pw2, o_pw2_b, f_norm, f_w1, f_w2, final_norm, loss_target, m_e_norm, m_e_w_in, m_e_conv_w, m_e_a_log, m_e_dt_bias, m_e_o_norm, m_e_ln_g, m_e_ln_b, m_e_w_s, m_e_b_s, m_e_w_out, m_o_norm, m_o_pw1, m_o_pw1_b, m_o_dw, m_o_dw_b, m_o_ln_g, m_o_ln_b, m_o_pw2, m_o_pw2_b, m_f_norm, m_f_w1, m_f_w2, m_final_norm, v_e_norm, v_e_w_in, v_e_conv_w, v_e_a_log, v_e_dt_bias, v_e_o_norm, v_e_ln_g, v_e_ln_b, v_e_w_s, v_e_b_s, v_e_w_out, v_o_norm, v_o_pw1, v_o_pw1_b, v_o_dw, v_o_dw_b, v_o_ln_g, v_o_ln_b, v_o_pw2, v_o_pw2_b, v_f_norm, v_f_w1, v_f_w2, v_final_norm):
    a = dict(locals())
    xi, yi, ci = _me()
    chip = 2 * xi + yi
    who = jnp.stack([ci, chip]).astype(jnp.int32)
    t, d = x.shape[1], x.shape[2]
    nh, ng = e_a_log.shape[1], e_w_s.shape[1]
    n_qkv, n_av, n_bw = 3 * nh * HEAD_DIM, nh * HEAD_DIM, ng * HEAD_DIM
    in_cols = n_qkv + n_av + 2 * nh + 2 * n_bw
    c_ba = n_qkv + n_av

    sh_in = in_cols // N_CHIPS
    pad_in = -(-sh_in // IN_ROW_MULT) * IN_ROW_MULT - sh_in
    w_in_t_local = jnp.pad(e_w_in[0].T.astype(BF16), ((0, pad_in), (0, 0)))
    small_local = [a[n] for n in SMALL_SHARDED]
    g_in, g_small = gather_shards([w_in_t_local, _pack(small_local)], name="gather_weights_in")
    travelling, token = gather_start([[w.astype(BF16) for w in (e_w_out[0], f_w1[0], f_w2[0])],
                                      [w.astype(BF16) for w in (o_pw1[0], o_pw2[0], f_w1[1], f_w2[1])]], name="gather_start")

    state = {"layer0": travelling[0], "layer1": travelling[1]}

    def arrive(layer, event, after):
        if event == "landed":
            state[layer] = gather_forward(state[layer], after, name="gather_forward_" + layer)
            return {}
        got = gather_wait(state[layer], after, name="gather_wait_" + layer)
        if layer == "layer0":
            g_out, g_w1, g_w2 = got
            return {"w_out": g_out.reshape(1, -1, d), "w1_0": g_w1, "w2_0": g_w2.reshape(1, -1, d)}
        g_pw1, g_pw2, g_w1, g_w2 = got
        return {"pw1": g_pw1, "pw2": g_pw2.reshape(1, -1, d), "w1_1": g_w1, "w2_1": g_w2.reshape(1, -1, d)}

    def in_rows(lo, hi):
        parts = []
        for s in range(N_CHIPS):
            a0, a1 = max(lo, s * sh_in), min(hi, (s + 1) * sh_in)
            if a0 < a1:
                parts.append(g_in[s, a0 - s * sh_in:a1 - s * sh_in])
        return parts

    wb = {
        "w_main_t": jnp.concatenate(in_rows(0, c_ba) + in_rows(c_ba + 2 * nh, in_cols), axis=0)[None],
        "w_ba_t": jnp.pad(jnp.concatenate(in_rows(c_ba, c_ba + 2 * nh), axis=0), ((0, LANES - 2 * nh), (0, 0)))[None],
    }
    pieces = _unpack(g_small.reshape(N_CHIPS, -1), [w.shape for w in small_local], lead=(N_CHIPS,))
    full = {n: jnp.moveaxis(p, 0, -2).reshape(p.shape[1:-1] + (N_CHIPS * p.shape[-1],)) for n, p in zip(SMALL_SHARDED, pieces)}
    ws = {
        "e_norm": e_norm + token[0, 0],
        "conv_w": full["e_conv_w"][0], "alog": e_a_log.reshape(nh, 1, 1), "dtb": e_dt_bias.reshape(nh, 1, 1),
        "onorm": e_o_norm, "lng": e_ln_g.reshape(ng, 1, HEAD_DIM), "lnb": e_ln_b.reshape(ng, 1, HEAD_DIM), "w_s": e_w_s[0],
        "bs": e_b_s.reshape(ng, HEAD_DIM, 1), "o_norm": full["o_norm"], "b1a": full["o_pw1_b"][:, :d], "b1b": full["o_pw1_b"][:, d:],
        "dw": full["o_dw"][0], "dw_b": full["o_dw_b"], "ln_g": full["o_ln_g"], "ln_b": full["o_ln_b"], "b2": full["o_pw2_b"],
        "f_norm0": f_norm[0:1], "f_norm1": f_norm[1:2], "final_norm": final_norm.reshape(1, d),
    }

    pending = {}

    def reduce(group, event, gb, after):
        names = REDUCE_GROUPS[group]
        if event == "ready":
            arrs = [gb[nme] if gb[nme].shape[0] == N_CHIPS else gb[nme].reshape(N_CHIPS, -1, d) for nme in names]
            pending[group] = split_start(_pair_copies, arrs, [(N_CHIPS, g.shape[1] // 2, g.shape[2]) for g in arrs], len(arrs),
                                         name="reduce_sibling_start_" + group)
            return pending[group]["token"]
        mine, theirs = split_wait(pending[group], after, name="reduce_sibling_wait_" + group)
        pairs = [pre_reduce(gd, sb, who, name="reduce_pair_" + nme) for nme, gd, sb in zip(names, mine, theirs)]
        sums = [p[0] for p in pairs]
        pending[group] = {"own": [p[1] for p in pairs],
                          "scatter": split_start(_scatter_copies, sums, [(3,) + s.shape[1:] for s in sums], 3 * len(sums),
                                                 name="reduce_chips_start_" + group)}
        return pending[group]["scatter"]["token"]

    loss, grad_x, gb, gs = local_step(x[0], loss_target[0], wb, ws, arrive, reduce)

    gm, gba = gb["w_main_t"][0], gb["w_ba_t"][0]
    g_in_t = jnp.concatenate([gm[:c_ba], gba[:2 * nh], gm[c_ba:]], axis=0).reshape(N_CHIPS, sh_in, d)
    in_pair = split_start(_pair_copies, [jnp.pad(g_in_t, ((0, 0), (0, pad_in), (0, 0)))], [(N_CHIPS, (sh_in + pad_in) // 2, d)], 1,
                          name="reduce_sibling_start_w_in")
    small_global = {
        "e_norm": gs["e_norm"], "e_conv_w": gs["conv_w"][None], "e_a_log": gs["alog"].reshape(1, nh), "e_dt_bias": gs["dtb"].reshape(1, nh),
        "e_o_norm": gs["onorm"], "e_ln_g": gs["lng"].reshape(1, n_bw), "e_ln_b": gs["lnb"].reshape(1, n_bw), "e_w_s": gs["w_s"][None],
        "e_b_s": gs["bs"].reshape(1, ng, HEAD_DIM), "o_norm": gs["o_norm"], "o_pw1_b": jnp.concatenate([gs["b1a"], gs["b1b"]], axis=1),
        "o_dw": gs["dw"][None], "o_dw_b": gs["dw_b"], "o_ln_g": gs["ln_g"], "o_ln_b": gs["ln_b"], "o_pw2_b": gs["b2"],
        "f_norm": jnp.concatenate([gs["f_norm0"], gs["f_norm1"]], axis=0), "final_norm": gs["final_norm"].reshape(d),
    }
    packed = _pack([small_global[n] for n in SMALL])
    everyone = split_start(_everyone_copies, [packed], [(8,) + packed.shape], 7, name="reduce_small_start")

    who_then = who + (in_pair["token"][0, 0] + everyone["token"][0, 0]).astype(jnp.int32)
    half = {}
    for group, names in REDUCE_GROUPS.items():
        _, landed = split_wait(pending[group]["scatter"], grad_x, name="reduce_chips_wait_" + group)
        for nme, ow, rc in zip(names, pending[group]["own"], landed):
            half[nme] = final_reduce(ow, rc, who_then, name="reduce_sum_" + nme)
    r_out, r_pw1, r_pw2, r_w1_0, r_w1_1, r_w2_0, r_w2_1 = sibling_join_halves(
        [half[nme] for nme in ("w_out", "pw1", "pw2", "w1_0", "w1_1", "w2_0", "w2_1")], name="reduce_join")
    big_grads = {"e_w_out": [r_out], "o_pw1": [r_pw1], "o_pw2": [r_pw2], "f_w1": [r_w1_0, r_w1_1], "f_w2": [r_w2_0, r_w2_1]}

    mine, theirs = split_wait(in_pair, r_out, name="reduce_sibling_wait_w_in")
    in_sum, in_own = pre_reduce(mine[0], theirs[0], who, name="reduce_pair_w_in")
    in_scatter = split_start(_scatter_copies, [in_sum], [(3,) + in_sum.shape[1:]], 3, name="reduce_chips_start_w_in")
    out = {}
    prev = in_scatter["token"]
    for n in BIG:
        if n != "e_w_in":
            out[n] = adamw(a[n], big_grads[n], a["m_" + n], a["v_" + n], name="adamw_" + n, behind=prev)
            prev = out[n][1][0, :SUBLANES, :LANES]

    own_packed, landed = split_wait(everyone, prev, name="reduce_small_wait")
    summed = sum8(landed[0], own_packed[0], (4 * xi + 2 * yi + ci).astype(jnp.int32).reshape(1), name="reduce_small_sum")
    small_full = _unpack(summed.reshape(-1), [small_global[n].shape for n in SMALL])
    small_grads = {}
    for n, g in zip(SMALL, small_full):
        if n in SMALL_SHARDED:
            width = a[n].shape[-1]
            g = lax.dynamic_slice_in_dim(g, chip * width, width, axis=g.ndim - 1)
        small_grads[n] = g
    sw, sm, sv, sg = (_pack([src[n] for n in SMALL])[None] for src in
                      ({n: a[n] for n in SMALL}, {n: a["m_" + n] for n in SMALL}, {n: a["v_" + n] for n in SMALL}, small_grads))
    res = adamw(sw, [sg[0]], sm, sv, name="adamw_small")

    _, landed = split_wait(in_scatter, res[1], name="reduce_chips_wait_w_in")
    (r_in,) = sibling_join_halves([final_reduce(in_own, landed[0], who, name="reduce_sum_w_in")], name="reduce_join_w_in")
    w_in_res = adamw(e_w_in.transpose(0, 2, 1), [r_in[:sh_in]], m_e_w_in.transpose(0, 2, 1), v_e_w_in.transpose(0, 2, 1),
                     name="adamw_e_w_in")
    out["e_w_in"] = tuple(r.transpose(0, 2, 1) for r in w_in_res)
    shapes = [a[n].shape for n in SMALL]
    unpacked = [_unpack(r.reshape(-1), shapes) for r in res]
    for i, n in enumerate(SMALL):
        out[n] = tuple(u[i] for u in unpacked)

    total = lax.psum(loss[0, 0], ("x", "y", "c"))
    result = [total, grad_x[None]]
    for k in range(4):
        result += [out[n][k] for n in WEIGHTS]
    return tuple(result)
```

```python
import functools
import math

import jax
import jax.numpy as jnp
import numpy as np
from jax import lax
from jax.experimental import pallas as pl
from jax.experimental.pallas import tpu as pltpu

F32 = jnp.float32
BF16 = jnp.bfloat16
HI = lax.Precision.HIGHEST

EPS = 1e-6
CHUNK = 64
PAIR = 2 * CHUNK
HEAD_DIM = 128
A_CONV = 4
C_WIDTH = 31
N_CHIPS = 4
ADAM_LR, ADAM_B1, ADAM_B2, ADAM_EPS, ADAM_WD, ADAM_STEP = 0.001, 0.9, 0.999, 1e-08, 0.01, 10

VMEM_LIMIT = 56 * 1024 * 1024


def _cparams(sem=None):
    return pltpu.CompilerParams(dimension_semantics=sem, vmem_limit_bytes=VMEM_LIMIT)


def _pick(n, prefs):
    for p in prefs:
        if n % p == 0:
            return p
    return n


def mm_nn(a, w, *, name, epilogue=None, add=None, out_dtype=F32):
    m, k = a.shape
    s, _, ns = w.shape
    n = s * ns
    tm = _pick(m, (1024, 512, 256, 128))
    tn = _pick(ns, (1024, 512, 256, 128))
    tk = _pick(k, (2048, 1024, 512, 256, 128))
    nk = k // tk
    npb = ns // tn
    assert add is None or epilogue is None

    def body(a_ref, w_ref, *rest):
        if add is not None:
            add_ref, rest = rest[0], rest[1:]
        if epilogue == "relu2":
            o1_ref, o2_ref = rest[0], rest[1]
            acc_ref = rest[2] if nk > 1 else None
        else:
            o1_ref = rest[0]
            acc_ref = rest[1] if nk > 1 else None
        part = jnp.dot(a_ref[...], w_ref[...], preferred_element_type=F32)

        def finish(c):
            if epilogue == "relu2":
                r = jnp.maximum(c, 0.0)
                o1_ref[...] = r.astype(o1_ref.dtype)
                o2_ref[...] = (r * r).astype(o2_ref.dtype)
            elif add is not None:
                o1_ref[...] = (c + add_ref[...].astype(F32)).astype(o1_ref.dtype)
            else:
                o1_ref[...] = c.astype(o1_ref.dtype)

        if nk == 1:
            finish(part)
        else:
            kk = pl.program_id(2)

            @pl.when(kk == 0)
            def _():
                acc_ref[...] = part

            @pl.when(kk > 0)
            def _():
                acc_ref[...] += part

            @pl.when(kk == nk - 1)
            def _():
                finish(acc_ref[...])

    o_spec = pl.BlockSpec((tm, tn), lambda i, j, kk: (i, j))
    if epilogue == "relu2":
        out_shape = (jax.ShapeDtypeStruct((m, n), BF16), jax.ShapeDtypeStruct((m, n), BF16))
        out_specs = (o_spec, o_spec)
    else:
        out_shape = jax.ShapeDtypeStruct((m, n), out_dtype)
        out_specs = o_spec
    return pl.pallas_call(
        body, name=name, out_shape=out_shape,
        grid=(m // tm, n // tn, nk),
        in_specs=[pl.BlockSpec((tm, tk), lambda i, j, kk: (i, kk)),
                  pl.BlockSpec((None, tk, tn), lambda i, j, kk: (j // npb, kk, j % npb))] + ([o_spec] if add is not None else []),
        out_specs=out_specs,
        scratch_shapes=[pltpu.VMEM((tm, tn), F32)] if nk > 1 else [],
        compiler_params=_cparams(("parallel", "parallel", "arbitrary")),
    )(*([a, w] + ([add] if add is not None else [])))


def mm_nt(a, w, *, name, mul=None, add=None, out_dtype=F32):
    assert mul is None or add is None
    if add is not None:
        mul = add
    m, n = a.shape
    s, k, ns = w.shape
    assert n == s * ns
    tm = _pick(m, (1024, 512, 256, 128))
    tko = _pick(k, (1024, 512, 256, 128))
    tn = _pick(ns, (2048, 1024, 512, 256, 128))
    nn = n // tn
    npb = ns // tn

    def body(a_ref, w_ref, *rest):
        if mul is not None:
            m_ref, o_ref = rest[0], rest[1]
            acc_ref = rest[2] if nn > 1 else None
        else:
            m_ref, o_ref = None, rest[0]
            acc_ref = rest[1] if nn > 1 else None
        part = lax.dot_general(a_ref[...], w_ref[...], (((1,), (1,)), ((), ())), preferred_element_type=F32)

        def finish(c):
            if add is not None:
                c = c + m_ref[...].astype(F32)
            elif m_ref is not None:
                c = c * (2.0 * m_ref[...].astype(F32))
            o_ref[...] = c.astype(o_ref.dtype)

        if nn == 1:
            finish(part)
        else:
            kk = pl.program_id(2)

            @pl.when(kk == 0)
            def _():
                acc_ref[...] = part

            @pl.when(kk > 0)
            def _():
                acc_ref[...] += part

            @pl.when(kk == nn - 1)
            def _():
                finish(acc_ref[...])

    in_specs = [pl.BlockSpec((tm, tn), lambda i, j, kk: (i, kk)),
                pl.BlockSpec((None, tko, tn), lambda i, j, kk: (kk // npb, j, kk % npb))]
    args = [a, w]
    if mul is not None:
        in_specs.append(pl.BlockSpec((tm, tko), lambda i, j, kk: (i, j)))
        args.append(mul)
    return pl.pallas_call(
        body, name=name, out_shape=jax.ShapeDtypeStruct((m, k), out_dtype),
        grid=(m // tm, k // tko, nn),
        in_specs=in_specs,
        out_specs=pl.BlockSpec((tm, tko), lambda i, j, kk: (i, j)),
        scratch_shapes=[pltpu.VMEM((tm, tko), F32)] if nn > 1 else [],
        compiler_params=_cparams(("parallel", "parallel", "arbitrary")),
    )(*args)


def mm_tn(a, b, *, groups, name):
    m, k = a.shape
    _, n = b.shape
    ns = n // groups
    tko = _pick(k, (1024, 512, 256, 128))
    tn = _pick(ns, (1024, 512, 256, 128))
    tc = _pick(m, (2048, 1024, 512, 256, 128))
    nc = m // tc
    npb = ns // tn

    def body(a_ref, b_ref, o_ref):
        part = lax.dot_general(a_ref[...], b_ref[...], (((0,), (0,)), ((), ())), preferred_element_type=F32)
        kk = pl.program_id(2)

        @pl.when(kk == 0)
        def _():
            o_ref[...] = part

        @pl.when(kk > 0)
        def _():
            o_ref[...] += part

    return pl.pallas_call(
        body, name=name, out_shape=jax.ShapeDtypeStruct((groups, k, ns), F32),
        grid=(k // tko, n // tn, nc),
        in_specs=[pl.BlockSpec((tc, tko), lambda i, j, kk: (kk, i)),
                  pl.BlockSpec((tc, tn), lambda i, j, kk: (kk, j))],
        out_specs=pl.BlockSpec((None, tko, tn), lambda i, j, kk: (j // npb, i, j % npb)),
        compiler_params=_cparams(("parallel", "parallel", "arbitrary")),
    )(a, b)


ROWS = 128


def _full_spec(arr):
    nd = arr.ndim
    return pl.BlockSpec(arr.shape, lambda i, _nd=nd: (0,) * _nd)


def _row_spec(x, rows):
    if isinstance(x, tuple):
        _, w, cb = x
        return pl.BlockSpec((rows, w), lambda i, _cb=cb: (i, _cb))
    return pl.BlockSpec((rows, x.shape[1]), lambda i: (i, 0))


def _arr(x):
    return x[0] if isinstance(x, tuple) else x


def _width(x):
    return x[1] if isinstance(x, tuple) else x.shape[1]


def rows_fwd(fn, params, xs, stores, *, name, rows=ROWS):
    t = _arr(xs[0]).shape[0]
    np_, nx = len(params), len(xs)

    def body(*refs):
        p_refs, x_refs, o_refs = refs[:np_], refs[np_:np_ + nx], refs[np_ + nx:]
        outs = fn(*[r[...].astype(F32) for r in p_refs], *[r[...].astype(F32) for r in x_refs])
        for (idx, dt, _), o_ref in zip(stores, o_refs):
            o_ref[...] = outs[idx].astype(dt)

    res = pl.pallas_call(
        body, name=name,
        out_shape=tuple(jax.ShapeDtypeStruct((t, w), dt) for _, dt, w in stores),
        grid=(t // rows,),
        in_specs=[_full_spec(p) for p in params] + [_row_spec(x, rows) for x in xs],
        out_specs=tuple(pl.BlockSpec((rows, w), lambda i: (i, 0)) for _, _, w in stores),
        compiler_params=_cparams(("parallel",)),
    )(*params, *[_arr(x) for x in xs])
    return res


def rows_bwd(fn, params, xs, cts, dx_stores, *, name, rows=ROWS, concat=False):
    t = _arr(xs[0]).shape[0]
    np_, nx = len(params), len(xs)
    ct_idx = [i for i, c in enumerate(cts) if c is not None]
    ct_arrs = [cts[i] for i in ct_idx]
    nct = len(ct_arrs)
    nds = 1 if concat else len(dx_stores)
    widths = [_width(xs[xi]) for xi, _ in dx_stores]

    def body(*refs):
        p_refs = refs[:np_]
        x_refs = refs[np_:np_ + nx]
        c_refs = refs[np_ + nx:np_ + nx + nct]
        d_refs = refs[np_ + nx + nct:np_ + nx + nct + nds]
        g_refs = refs[np_ + nx + nct + nds:]
        pv = [r[...].astype(F32) for r in p_refs]
        xv = [r[...].astype(F32) for r in x_refs]
        outs, vjp = jax.vjp(lambda *a: tuple(fn(*a)), *pv, *xv)
        ct_full = [jnp.zeros_like(o) for o in outs]
        for i, r in zip(ct_idx, c_refs):
            ct_full[i] = r[...].astype(F32)
        grads = vjp(tuple(ct_full))
        if concat:
            off = 0
            for (xi, dt), wd in zip(dx_stores, widths):
                d_refs[0][:, off:off + wd] = grads[np_ + xi].astype(dt)
                off += wd
        else:
            for (xi, dt), d_ref in zip(dx_stores, d_refs):
                d_ref[...] = grads[np_ + xi].astype(dt)
        step = pl.program_id(0)
        for j, g_ref in enumerate(g_refs):
            @pl.when(step == 0)
            def _(g_ref=g_ref, j=j):
                g_ref[...] = grads[j]

            @pl.when(step > 0)
            def _(g_ref=g_ref, j=j):
                g_ref[...] += grads[j]

    dx_shapes = [(sum(widths), dx_stores[0][1])] if concat else [(wd, dt) for wd, (_, dt) in zip(widths, dx_stores)]
    out_shape = tuple(jax.ShapeDtypeStruct((t, wd), dt) for wd, dt in dx_shapes) + \
        tuple(jax.ShapeDtypeStruct(p.shape, F32) for p in params)
    out_specs = tuple(pl.BlockSpec((rows, wd), lambda i: (i, 0)) for wd, _ in dx_shapes) + \
        tuple(_full_spec(p) for p in params)
    return pl.pallas_call(
        body, name=name, out_shape=out_shape, grid=(t // rows,),
        in_specs=[_full_spec(p) for p in params] + [_row_spec(x, rows) for x in xs] + [_row_spec(c, rows) for c in ct_arrs],
        out_specs=out_specs,
        compiler_params=_cparams(("arbitrary",)),
    )(*params, *[_arr(x) for x in xs], *[_arr(c) for c in ct_arrs])


def _rms(x, g):
    return x * lax.rsqrt(jnp.mean(x * x, axis=-1, keepdims=True) + EPS) * g


def _sigmoid(x):
    return 1.0 / (1.0 + jnp.exp(-x))


def _silu(x):
    return x * _sigmoid(x)


def _gelu(x):
    return 0.5 * x * (1.0 + lax.erf(x * (1.0 / math.sqrt(2.0))))


def stage_norm(g, x):
    return x, _rms(x, g)


def stage_res_norm(g, x, y):
    xn = x + y
    return xn, _rms(xn, g)


def stage_res_bias_norm(b, g, x, y):
    xn = x + y + b
    return xn, _rms(xn, g)


def stage_glu(ba, bb, za, zb):
    return ((za + ba) * _sigmoid(zb + bb),)


def stage_ln_silu(dw_b, ln_g, ln_b, cv):
    z = cv + dw_b
    mu = jnp.mean(z, axis=-1, keepdims=True)
    zc = z - mu
    y = zc * lax.rsqrt(jnp.mean(zc * zc, axis=-1, keepdims=True) + EPS) * ln_g + ln_b
    return (_silu(y),)


CONV_ROWS = 128
CONV_COLS = 256
SUBLANES = 8


def _halo(k):
    return SUBLANES * ((k - 1 + SUBLANES - 1) // SUBLANES)


def _taps_by_roll(k):
    out = {}
    for s in range(k):
        out.setdefault(s % SUBLANES, []).append((s // SUBLANES, s))
    return out


def _shifted_down(win, k, r):
    halo, n = _halo(k), win.shape[0]
    segs = {}
    for b, lst in _taps_by_roll(k).items():
        rolled = win if b == 0 else pltpu.roll(win, b, axis=0)
        for a, s in lst:
            segs[s] = rolled[halo - SUBLANES * a: halo - SUBLANES * a + r]
    return segs


def _shifted_up(win, k, r):
    n = win.shape[0]
    segs = {}
    for b, lst in _taps_by_roll(k).items():
        rolled = win if b == 0 else pltpu.roll(win, n - b, axis=0)
        for a, s in lst:
            segs[s] = rolled[SUBLANES * a: SUBLANES * a + r]
    return segs


def _for_blocks(nblk, fn):
    fn(0, True, nblk == 1)
    if nblk > 2:
        def step(i, c):
            fn(i, False, False)
            return c
        lax.fori_loop(1, nblk - 1, step, 0)
    if nblk > 1:
        fn(nblk - 1, False, True)


def _base(i, r):
    return i * r if isinstance(i, int) else pl.multiple_of(i * r, r)


def _win_top(ref, i, first, r, halo):
    if first:
        return jnp.concatenate([jnp.zeros((halo, ref.shape[1]), F32), ref[pl.ds(0, r), :]], axis=0)
    base = _base(i, r)
    return ref[pl.ds(base - halo, r + halo), :]


def _win_bottom(ref, i, last, r, halo):
    base = _base(i, r)
    if last:
        return jnp.concatenate([ref[pl.ds(base, r), :], jnp.zeros((halo, ref.shape[1]), F32)], axis=0)
    return ref[pl.ds(base, r + halo), :]


def conv_fwd(x, w, *, mode, name, g=None):
    t = x.shape[0]
    k, c = w.shape
    r, cb, halo = min(CONV_ROWS, t), min(CONV_COLS, c), _halo(k)
    nblk = t // r

    def body(*refs):
        if mode == "silu_bwd":
            x_ref, w_ref, g_ref, o_ref = refs
        else:
            x_ref, w_ref, o_ref = refs

        def blk(i, first, last):
            segs = _shifted_down(_win_top(x_ref, i, first, r, halo), k, r)
            acc = None
            for s in range(k):
                term = w_ref[pl.ds(k - 1 - s, 1), :] * segs[s]
                acc = term if acc is None else acc + term
            base = _base(i, r)
            if mode == "silu":
                acc = _silu(acc)
            elif mode == "silu_bwd":
                sg = _sigmoid(acc)
                acc = g_ref[pl.ds(base, r), :] * (sg * (1.0 + acc * (1.0 - sg)))
            o_ref[pl.ds(base, r), :] = acc

        _for_blocks(nblk, blk)

    col = pl.BlockSpec((t, cb), lambda j: (0, j))
    in_specs = [col, pl.BlockSpec((k, cb), lambda j: (0, j))] + ([col] if mode == "silu_bwd" else [])
    args = [x, w] + ([g] if mode == "silu_bwd" else [])
    return pl.pallas_call(
        body, name=name, out_shape=jax.ShapeDtypeStruct((t, c), F32), grid=(c // cb,),
        in_specs=in_specs, out_specs=col, compiler_params=_cparams(("parallel",)),
    )(*args)


def conv_bwd(x, w, dy, *, name, into=None):
    t = x.shape[0]
    k, c = w.shape
    r, cb, halo = min(CONV_ROWS, t), min(CONV_COLS, c), _halo(k)
    nblk = t // r

    def body(x_ref, w_ref, dy_ref, *rest):
        dx_ref, dw_ref = rest[-2], rest[-1]
        dw_ref[...] = jnp.zeros_like(dw_ref)

        def blk(i, first, last):
            base = _base(i, r)
            up = _shifted_up(_win_bottom(dy_ref, i, last, r, halo), k, r)
            down = _shifted_down(_win_top(x_ref, i, first, r, halo), k, r)
            dyb = up[0]
            acc = None
            for s in range(k):
                term = w_ref[pl.ds(k - 1 - s, 1), :] * up[s]
                acc = term if acc is None else acc + term
                dw_ref[pl.ds(k - 1 - s, 1), :] += jnp.sum(down[s] * dyb, axis=0, keepdims=True)
            dx_ref[pl.ds(base, r), :] = acc.astype(dx_ref.dtype)

        _for_blocks(nblk, blk)

    col = pl.BlockSpec((t, cb), lambda j: (0, j))
    wsp = pl.BlockSpec((k, cb), lambda j: (0, j))
    dx_shape = jax.ShapeDtypeStruct((t, c), F32) if into is None else jax.ShapeDtypeStruct(into.shape, into.dtype)
    return pl.pallas_call(
        body, name=name,
        out_shape=(dx_shape, jax.ShapeDtypeStruct((k, c), F32)), grid=(c // cb,),
        in_specs=[col, wsp, col] + ([] if into is None else [pl.BlockSpec(memory_space=pl.ANY)]),
        out_specs=(col, wsp),
        input_output_aliases={} if into is None else {3: 0},
        compiler_params=_cparams(("parallel",)),
    )(*([x, w, dy] + ([] if into is None else [into])))


_DIMS = {"nn": (((1,), (0,)), ((), ())), "nt": (((1,), (1,)), ((), ())), "tn": (((0,), (0,)), ((), ()))}
_DIMS_BATCHED = {"nn": (((2,), (1,)), ((0,), (0,))), "nt": (((2,), (2,)), ((0,), (0,))), "tn": (((1,), (1,)), ((0,), (0,)))}


def _mxu(a, b, mode):
    dims = _DIMS_BATCHED if a.ndim == 3 else _DIMS
    return lax.dot_general(a, b, dims[mode], preferred_element_type=F32)


def _split(x):
    hi = x.astype(BF16)
    return hi, (x - hi.astype(F32)).astype(BF16)


def _dot_raw(a, b, mode, prec):
    if prec == "bf16":
        return _mxu(a.astype(BF16), b.astype(BF16), mode)
    if prec == "x3":
        ah, al = _split(a)
        bh, bl = _split(b)
        return _mxu(ah, bh, mode) + (_mxu(ah, bl, mode) + _mxu(al, bh, mode))
    if prec == "x3r":
        bh, bm = _split(b)
        bl = (b - bh.astype(F32) - bm.astype(F32)).astype(BF16)
        ah = a.astype(BF16)
        return _mxu(ah, bh, mode) + (_mxu(ah, bm, mode) + _mxu(ah, bl, mode))
    raise ValueError(prec)


@functools.lru_cache(maxsize=None)
def _dot_fn(mode, prec):
    bprec = "x3" if prec == "x3r" else prec

    @jax.custom_vjp
    def f(a, b):
        return _dot_raw(a, b, mode, prec)

    def fwd(a, b):
        return _dot_raw(a, b, mode, prec), (a, b)

    def bwd(res, ct):
        a, b = res
        if mode == "nn":
            return _dot_raw(ct, b, "nt", bprec), _dot_raw(a, ct, "tn", bprec)
        if mode == "nt":
            return _dot_raw(ct, b, "nn", bprec), _dot_raw(ct, a, "tn", bprec)
        return _dot_raw(b, ct, "nt", bprec), _dot_raw(a, ct, "nn", bprec)

    f.defvjp(fwd, bwd)
    return f


def _dot(a, b, mode="nn", prec="bf16"):
    return _dot_fn(mode, prec)(a, b)


def _inv_product(l):
    n = l.shape[-1]
    eye = (lax.broadcasted_iota(jnp.int32, (n, n), 0) == lax.broadcasted_iota(jnp.int32, (n, n), 1)).astype(F32)
    p = eye - l
    pw = l
    for _ in range(5):
        pw = _dot_raw(pw, pw, "nn", "x3")
        p = _dot_raw(p, eye + pw, "nn", "x3")
    return p


@jax.custom_vjp
def _inv_unit_lower(l, t_saved):
    return t_saved


def _inv_fwd(l, t_saved):
    return t_saved, t_saved


def _inv_bwd(t, ct):
    tmp = _dot_raw(t, ct, "tn", "x3")
    return -_dot_raw(tmp, t, "nt", "x3"), jnp.zeros_like(t)


_inv_unit_lower.defvjp(_inv_fwd, _inv_bwd)


def _softplus(x):
    pos = x > 0
    return jnp.where(pos, x, 0.0) + jnp.log(1.0 + jnp.exp(jnp.where(pos, -x, x)))


def _l2n(x):
    return x * lax.rsqrt(jnp.sum(x * x, axis=-1, keepdims=True) + EPS)


def delta_pair(s0, qc, kc, vc, z, pba, alog, dtb, onorm, t_saved=None):
    n = PAIR
    nh = qc.shape[0]
    assert qc.shape == (nh, n, HEAD_DIM) and n == HEAD_DIM
    hi = lax.broadcasted_iota(jnp.int32, (nh, 1, pba.shape[1]), 0)
    li = lax.broadcasted_iota(jnp.int32, (nh, 1, pba.shape[1]), 2)
    braw = jnp.sum(pba[None] * (li == hi).astype(F32), axis=2, keepdims=True)
    araw = jnp.sum(pba[None] * (li == hi + nh).astype(F32), axis=2, keepdims=True)
    ri = lax.broadcasted_iota(jnp.int32, (n, n), 0)
    ci = lax.broadcasted_iota(jnp.int32, (n, n), 1)
    same = (ri // CHUNK) == (ci // CHUNK)
    tri = same & (ci <= ri)
    same_f = jnp.broadcast_to(same.astype(F32), (nh, n, n))
    tri_f = jnp.broadcast_to(tri.astype(F32), (nh, n, n))
    strict_f = (same & (ci < ri)).astype(F32)
    m0 = (lax.broadcasted_iota(jnp.int32, (n, 1), 0) < CHUNK).astype(F32)
    m1 = 1.0 - m0

    q = _l2n(qc) * (HEAD_DIM ** -0.5)
    k = _l2n(kc)
    beta = _sigmoid(braw)
    g = -jnp.exp(alog) * _softplus(araw + dtb)
    gb = jnp.broadcast_to(g, (nh, n, n))
    gc = _dot(tri_f, gb, "nn", "x3r")
    gtot = _dot(same_f, gb, "nn", "x3r")
    decay = jnp.exp(jnp.where(tri, gc - jnp.swapaxes(gc, 1, 2), -1e30))
    eg = jnp.exp(gc)
    kb, vb = k * beta, vc * beta
    l = _dot(kb, k, "nt") * decay * strict_f
    if t_saved is None:
        tinv = _inv_product(l)
    else:
        tinv = _inv_unit_lower(l, t_saved)
    u = _dot(tinv, vb, "nn", "x3")
    w = _dot(tinv, kb * eg, "nn", "x3")
    attn = _dot(q, k, "nt") * decay
    q_dec = q * eg
    k_tail = k * jnp.exp(gtot - gc)
    gl0 = jnp.exp(jnp.sum(gb * m0, axis=1, keepdims=True))
    gl1 = jnp.exp(jnp.sum(gb * m1, axis=1, keepdims=True))

    vn0 = m0 * (u - _dot(w, s0))
    s1 = s0 * gl0 + _dot(k_tail, vn0, "tn")
    vn1 = m1 * (u - _dot(w, s1))
    o = m0 * _dot(q_dec, s0) + m1 * _dot(q_dec, s1) + _dot(attn, vn0 + vn1)
    s2 = s1 * gl1 + _dot(k_tail, vn1, "tn")

    on = o * lax.rsqrt(jnp.mean(o * o, axis=-1, keepdims=True) + EPS) * onorm
    return on * _silu(z), s2, tinv


def _hcols(i):
    return slice(i * HEAD_DIM, (i + 1) * HEAD_DIM)


def _heads(ref, first, count):
    return jnp.stack([ref[:, _hcols(first + i)] for i in range(count)])


def delta_fwd(qkv, pm, pba, alog, dtb, onorm, cat_width, *, name):
    t = qkv.shape[0]
    h = alog.shape[0]
    hd = h * HEAD_DIM
    npair = t // PAIR
    mat = pl.BlockSpec((h, None, PAIR, HEAD_DIM), lambda p: (0, p, 0, 0))
    par = pl.BlockSpec((h, 1, 1), lambda p: (0, 0, 0))

    def body(qkv_ref, z_ref, pba_ref, al_ref, dt_ref, on_ref, o_ref, st_ref, ti_ref, s_scr):
        @pl.when(pl.program_id(0) == 0)
        def _():
            s_scr[...] = jnp.zeros_like(s_scr)

        s0 = s_scr[...]
        st_ref[...] = s0
        out, s2, tinv = delta_pair(s0, _heads(qkv_ref, 0, h), _heads(qkv_ref, h, h), _heads(qkv_ref, 2 * h, h), _heads(z_ref, 0, h),
                                   pba_ref[...], al_ref[...], dt_ref[...], on_ref[...])
        for i in range(h):
            o_ref[:, _hcols(i)] = out[i].astype(o_ref.dtype)
        ti_ref[...] = tinv
        s_scr[...] = s2

    return pl.pallas_call(
        body, name=name, grid=(npair,),
        out_shape=(jax.ShapeDtypeStruct((t, cat_width), BF16),
                   jax.ShapeDtypeStruct((h, npair, PAIR, HEAD_DIM), F32),
                   jax.ShapeDtypeStruct((h, npair, PAIR, PAIR), F32)),
        in_specs=[pl.BlockSpec((PAIR, 3 * hd), lambda p: (p, 0)), pl.BlockSpec((PAIR, hd), lambda p: (p, 3)),
                  pl.BlockSpec((PAIR, pba.shape[1]), lambda p: (p, 0)), par, par,
                  pl.BlockSpec((1, HEAD_DIM), lambda p: (0, 0))],
        out_specs=(pl.BlockSpec((PAIR, hd), lambda p: (p, 0)), mat, mat),
        scratch_shapes=[pltpu.VMEM((h, HEAD_DIM, HEAD_DIM), F32)],
        compiler_params=_cparams(("arbitrary",)),
    )(qkv, pm, pba, alog, dtb, onorm)


def delta_bwd(qkv, pm, pba, alog, dtb, onorm, states, tinvs, dcat, *, name):
    t = qkv.shape[0]
    h = alog.shape[0]
    hd = h * HEAD_DIM
    npair = t // PAIR
    rev = lambda p: npair - 1 - p
    mat = pl.BlockSpec((h, None, PAIR, HEAD_DIM), lambda p: (0, rev(p), 0, 0))
    par = pl.BlockSpec((h, 1, 1), lambda p: (0, 0, 0))
    onsp = pl.BlockSpec((1, HEAD_DIM), lambda p: (0, 0))
    wide = pl.BlockSpec((PAIR, 3 * hd), lambda p: (rev(p), 0))
    zsp = pl.BlockSpec((PAIR, hd), lambda p: (rev(p), 3))
    bsp = pl.BlockSpec((PAIR, pba.shape[1]), lambda p: (rev(p), 0))

    def body(qkv_ref, z_ref, pba_ref, al_ref, dt_ref, on_ref, st_ref, ti_ref, dc_ref,
             dqkv_ref, dz_ref, dpba_ref, dal_ref, ddt_ref, don_ref, ds_scr):
        @pl.when(pl.program_id(0) == 0)
        def _():
            ds_scr[...] = jnp.zeros_like(ds_scr)
            dal_ref[...] = jnp.zeros_like(dal_ref)
            ddt_ref[...] = jnp.zeros_like(ddt_ref)
            don_ref[...] = jnp.zeros_like(don_ref)

        tsv = ti_ref[...]
        fn = lambda s0, qc, kc, vc, z, pb, al, dt, on: delta_pair(s0, qc, kc, vc, z, pb, al, dt, on, tsv)[:2]
        _, vjp = jax.vjp(fn, st_ref[...], _heads(qkv_ref, 0, h), _heads(qkv_ref, h, h), _heads(qkv_ref, 2 * h, h),
                         _heads(z_ref, 0, h), pba_ref[...], al_ref[...], dt_ref[...], on_ref[...])
        ds0, dq, dk, dv, dz, dpba, dal, ddt, don = vjp((_heads(dc_ref, 0, h), ds_scr[...]))
        ds_scr[...] = ds0
        for i in range(h):
            dqkv_ref[:, _hcols(i)] = dq[i]
            dqkv_ref[:, _hcols(h + i)] = dk[i]
            dqkv_ref[:, _hcols(2 * h + i)] = dv[i]
            dz_ref[:, _hcols(i)] = dz[i].astype(dz_ref.dtype)
        dal_ref[...] += dal
        ddt_ref[...] += ddt
        dpba_ref[...] = dpba.astype(dpba_ref.dtype)
        don_ref[...] += don

    return pl.pallas_call(
        body, name=name, grid=(npair,),
        out_shape=(jax.ShapeDtypeStruct((t, 3 * hd), F32), jax.ShapeDtypeStruct(pm.shape, BF16),
                   jax.ShapeDtypeStruct(pba.shape, BF16),
                   jax.ShapeDtypeStruct((h, 1, 1), F32), jax.ShapeDtypeStruct((h, 1, 1), F32),
                   jax.ShapeDtypeStruct((1, HEAD_DIM), F32)),
        in_specs=[wide, zsp, bsp, par, par, onsp, mat, mat, pl.BlockSpec((PAIR, hd), lambda p: (rev(p), 0))],
        out_specs=(wide, zsp, bsp, par, par, onsp),
        scratch_shapes=[pltpu.VMEM((h, HEAD_DIM, HEAD_DIM), F32)],
        compiler_params=_cparams(("arbitrary",)),
    )(qkv, pm, pba, alog, dtb, onorm, states, tinvs, dcat)


def gmlp_block(ln_g, ln_b, w, bcol, u_raw, v_raw):
    n = w.shape[-1]
    ri = lax.broadcasted_iota(jnp.int32, (n, n), 0)
    ci = lax.broadcasted_iota(jnp.int32, (n, n), 1)
    mask = ((ci // CHUNK) <= (ri // CHUNK)).astype(F32)
    vg = _gelu(v_raw)
    vc = vg - jnp.mean(vg, axis=-1, keepdims=True)
    vgn = vc * lax.rsqrt(jnp.mean(vc * vc, axis=-1, keepdims=True) + EPS) * ln_g + ln_b
    return _gelu(u_raw) * (_dot(w * mask, vgn) + bcol)


def gmlp_fwd(pm, ln_g, ln_b, w_s, bcol, cat, *, name):
    t = pm.shape[0]
    g = w_s.shape[0]
    gw = g * HEAD_DIM
    assert pm.shape[1] == 6 * gw and cat.shape[1] == 2 * gw

    def body(u_ref, v_ref, lg_ref, lb_ref, w_ref, b_ref, cat_in, o_ref):
        del cat_in
        out = gmlp_block(lg_ref[...], lb_ref[...], w_ref[...], b_ref[...], _heads(u_ref, 0, g), _heads(v_ref, 0, g))
        for i in range(g):
            o_ref[:, _hcols(i)] = out[i].astype(o_ref.dtype)

    full = lambda a: pl.BlockSpec(a.shape, lambda m: (0, 0, 0))
    return pl.pallas_call(
        body, name=name, grid=(t // HEAD_DIM,),
        out_shape=jax.ShapeDtypeStruct(cat.shape, cat.dtype),
        in_specs=[pl.BlockSpec((HEAD_DIM, gw), lambda m: (m, 4)), pl.BlockSpec((HEAD_DIM, gw), lambda m: (m, 5)),
                  full(ln_g), full(ln_b), full(w_s), full(bcol), pl.BlockSpec(memory_space=pl.ANY)],
        out_specs=pl.BlockSpec((HEAD_DIM, gw), lambda m: (m, 1)),
        input_output_aliases={6: 0},
        compiler_params=_cparams(("arbitrary",)),
    )(pm, pm, ln_g, ln_b, w_s, bcol, cat)


def gmlp_bwd(pm, ln_g, ln_b, w_s, bcol, dcat, dpm, *, name):
    t = pm.shape[0]
    g = w_s.shape[0]
    gw = g * HEAD_DIM
    assert pm.shape[1] == 6 * gw and dpm.shape == pm.shape

    def body(u_ref, v_ref, lg_ref, lb_ref, w_ref, b_ref, dc_ref, dpm_in, duv_ref, dlg_ref, dlb_ref, dw_ref, db_ref):
        del dpm_in
        first = pl.program_id(0) == 0
        _, vjp = jax.vjp(gmlp_block, lg_ref[...], lb_ref[...], w_ref[...], b_ref[...], _heads(u_ref, 0, g), _heads(v_ref, 0, g))
        dlg, dlb, dw, db, du, dv = vjp(_heads(dc_ref, 0, g))
        for i in range(g):
            duv_ref[:, _hcols(i)] = du[i].astype(duv_ref.dtype)
            duv_ref[:, _hcols(g + i)] = dv[i].astype(duv_ref.dtype)
        for ref, val in ((dlg_ref, dlg), (dlb_ref, dlb), (dw_ref, dw), (db_ref, db)):
            @pl.when(first)
            def _(ref=ref, val=val):
                ref[...] = val

            @pl.when(jnp.logical_not(first))
            def _(ref=ref, val=val):
                ref[...] += val

    full = lambda a: pl.BlockSpec(a.shape, lambda m: (0, 0, 0))
    return pl.pallas_call(
        body, name=name, grid=(t // HEAD_DIM,),
        out_shape=(jax.ShapeDtypeStruct(dpm.shape, dpm.dtype),
                   jax.ShapeDtypeStruct(ln_g.shape, F32), jax.ShapeDtypeStruct(ln_b.shape, F32),
                   jax.ShapeDtypeStruct(w_s.shape, F32), jax.ShapeDtypeStruct(bcol.shape, F32)),
        in_specs=[pl.BlockSpec((HEAD_DIM, gw), lambda m: (m, 4)), pl.BlockSpec((HEAD_DIM, gw), lambda m: (m, 5)),
                  full(ln_g), full(ln_b), full(w_s), full(bcol), pl.BlockSpec((HEAD_DIM, gw), lambda m: (m, 1)),
                  pl.BlockSpec(memory_space=pl.ANY)],
        out_specs=(pl.BlockSpec((HEAD_DIM, 2 * gw), lambda m: (m, 2)), full(ln_g), full(ln_b), full(w_s), full(bcol)),
        input_output_aliases={7: 0},
        compiler_params=_cparams(("arbitrary",)),
    )(pm, pm, ln_g, ln_b, w_s, bcol, dcat, dpm)


def loss_head(g, x, r, tgt, *, name, rows=ROWS):
    t, d = x.shape

    def body(g_ref, x_ref, r_ref, t_ref, l_ref, dx_ref, dxb_ref, dg_ref):
        y, vjp = jax.vjp(lambda gg, xx: _rms(xx, gg), g_ref[...], x_ref[...] + r_ref[...])
        e = y - t_ref[...]
        part = (0.5 / d) * jnp.sum(jnp.sum(e * e, axis=1, keepdims=True), axis=0, keepdims=True)
        dg, dx = vjp(e * (1.0 / d))
        dx_ref[...] = dx
        dxb_ref[...] = dx.astype(BF16)
        first = pl.program_id(0) == 0

        @pl.when(first)
        def _():
            l_ref[...] = part
            dg_ref[...] = dg

        @pl.when(jnp.logical_not(first))
        def _():
            l_ref[...] += part
            dg_ref[...] += dg

    rs = pl.BlockSpec((rows, d), lambda i: (i, 0))
    gs = pl.BlockSpec((1, d), lambda i: (0, 0))
    return pl.pallas_call(
        body, name=name, grid=(t // rows,),
        out_shape=(jax.ShapeDtypeStruct((1, 1), F32), jax.ShapeDtypeStruct((t, d), F32),
                   jax.ShapeDtypeStruct((t, d), BF16), jax.ShapeDtypeStruct((1, d), F32)),
        in_specs=[gs, rs, rs, rs],
        out_specs=(pl.BlockSpec((1, 1), lambda i: (0, 0)), rs, rs, gs),
        compiler_params=_cparams(("arbitrary",)),
    )(g, x, r, tgt)


def adamw(w, gs, m, v, *, name, behind=None):
    nl, r, c = w.shape
    assert len(gs) == nl
    if r % SUBLANES == 0:
        tr, tc = _pick(r, (256, 128, 64, 32, 16, 8)), c
    else:
        tr, tc = r, _pick(c, (256, 128))
    k1 = 1.0 - ADAM_B1 ** ADAM_STEP
    k2 = 1.0 - ADAM_B2 ** ADAM_STEP

    def body(*refs):
        w_ref, m_ref, v_ref = refs[0], refs[1], refs[2]
        g_refs = refs[3:3 + nl]
        go_ref, d_ref, mo_ref, vo_ref = refs[-4:]
        gg = g_refs[0][...]
        for li in range(1, nl):
            gg = jnp.where(pl.program_id(0) == li, g_refs[li][...], gg)
        mn = ADAM_B1 * m_ref[...] + (1.0 - ADAM_B1) * gg
        vn = ADAM_B2 * v_ref[...] + (1.0 - ADAM_B2) * (gg * gg)
        go_ref[...] = gg
        d_ref[...] = -ADAM_LR * ((mn / k1) / (jnp.sqrt(vn / k2) + ADAM_EPS) + ADAM_WD * w_ref[...])
        mo_ref[...] = mn
        vo_ref[...] = vn

    if tc == c:
        sp = pl.BlockSpec((None, tr, c), lambda l, i: (l, i, 0))
        gsp = pl.BlockSpec((tr, c), lambda l, i: (i, 0))
    else:
        sp = pl.BlockSpec((None, r, tc), lambda l, i: (l, 0, i))
        gsp = pl.BlockSpec((r, tc), lambda l, i: (0, i))
    sds = jax.ShapeDtypeStruct((nl, r, c), F32)
    extra = [] if behind is None else [behind]
    return pl.pallas_call(
        body, name=name, grid=(nl, (r // tr) * (c // tc)), out_shape=(sds, sds, sds, sds),
        in_specs=[sp, sp, sp] + [gsp] * nl + [pl.BlockSpec(e.shape, lambda l, i: (0, 0)) for e in extra], out_specs=(sp, sp, sp, sp),
        compiler_params=_cparams(("parallel", "parallel")),
    )(w, m, v, *gs, *extra)


MESH = pl.DeviceIdType.MESH
ANY = pl.BlockSpec(memory_space=pl.ANY)


def _me():
    return lax.axis_index("x"), lax.axis_index("y"), lax.axis_index("c")


def _other_chips(x, y):
    return [(1 - x, y), (x, 1 - y), (1 - x, 1 - y)]


def gather_shards(arrs, *, name):
    n = len(arrs)
    per = 7

    def body(*refs):
        ins, outs = refs[:n], refs[n:2 * n]
        send_sems, recv_sems = refs[2 * n:]
        x, y, c = _me()
        chip = 2 * x + y
        sib = (x, y, 1 - c)
        chips = _other_chips(x, y)

        def rc(i, k, src, dst, to):
            return pltpu.make_async_remote_copy(src_ref=src, dst_ref=dst, send_sem=send_sems.at[per * i + k],
                                                recv_sem=recv_sems.at[per * i + k], device_id=to, device_id_type=MESH)

        def half(i, which):
            hrows = arrs[i].shape[0] // 2
            return pl.ds(which * hrows, hrows)

        own = [rc(i, 6, ins[i], outs[i].at[chip], sib) for i in range(n)]
        for cp in own:
            cp.start()
        started = []
        for i in range(n):
            for j, ch in enumerate(chips):
                cp = rc(i, j, ins[i].at[half(i, c)], outs[i].at[chip, half(i, c)], (ch[0], ch[1], c))
                cp.start()
                started.append(cp)
        for i in range(n):
            for j, ch in enumerate(chips):
                sc = 2 * ch[0] + ch[1]
                landed = outs[i].at[sc, half(i, c)]
                rc(i, j, ins[i].at[half(i, c)], landed, (ch[0], ch[1], c)).wait_recv()
                fw = rc(i, 3 + j, landed, landed, sib)
                fw.start()
                started.append(fw)
        for i in range(n):
            for j, ch in enumerate(chips):
                sc = 2 * ch[0] + ch[1]
                theirs = outs[i].at[sc, half(i, 1 - c)]
                rc(i, 3 + j, theirs, theirs, sib).wait_recv()
        for cp in own:
            cp.wait_recv()
        for cp in started + own:
            cp.wait_send()

    return pl.pallas_call(
        body, name=name,
        out_shape=tuple(jax.ShapeDtypeStruct((N_CHIPS,) + a.shape, a.dtype) for a in arrs),
        in_specs=[ANY] * n, out_specs=tuple([ANY] * n),
        scratch_shapes=[pltpu.SemaphoreType.DMA((per * n,)), pltpu.SemaphoreType.DMA((per * n,))],
        compiler_params=pltpu.CompilerParams(has_side_effects=True),
    )(*arrs)


HBM = pl.BlockSpec(memory_space=pltpu.HBM)
SEM = pl.BlockSpec(memory_space=pltpu.SEMAPHORE)
EFFECT = pltpu.SideEffectType.DATAFLOW_SIDE_EFFECTING
def _hbm(a):
    return pltpu.with_memory_space_constraint(a, pltpu.HBM)


def _half_rows(arr, which):
    hrows = arr.shape[-2] // 2
    return pl.ds(which * hrows, hrows)


def _ici_copies(srcs, lands, sems):
    x, y, c = _me()
    chip = 2 * x + y
    return [pltpu.make_async_remote_copy(src_ref=srcs[i].at[_half_rows(srcs[i], c)], dst_ref=lands[i].at[chip, _half_rows(srcs[i], c)],
                                         send_sem=sems[0].at[3 * i + j], recv_sem=sems[1].at[3 * i + j],
                                         device_id=(ch[0], ch[1], c), device_id_type=MESH)
            for i in range(len(srcs)) for j, ch in enumerate(_other_chips(x, y))]


def _own_copies(srcs, lands, sems):
    x, y, c = _me()
    return [pltpu.make_async_remote_copy(src_ref=srcs[i], dst_ref=lands[i].at[2 * x + y], send_sem=sems[0].at[i],
                                         recv_sem=sems[1].at[i], device_id=(x, y, 1 - c), device_id_type=MESH)
            for i in range(len(srcs))]


def _fwd_copies(srcs, lands, sems, half):
    x, y, c = _me()
    cps = []
    for i in range(len(srcs)):
        for j, ch in enumerate(_other_chips(x, y)):
            part = lands[i].at[2 * ch[0] + ch[1], _half_rows(srcs[i], half)]
            cps.append(pltpu.make_async_remote_copy(src_ref=part, dst_ref=part, send_sem=sems[0].at[3 * i + j],
                                                    recv_sem=sems[1].at[3 * i + j], device_id=(x, y, 1 - c), device_id_type=MESH))
    return cps


def gather_start(groups, *, name):
    arrs = [a for g in groups for a in g]
    n, ng = len(arrs), len(groups)
    bounds = np.cumsum([0] + [len(g) for g in groups])

    def body(*refs):
        srcs, lands = refs[:n], refs[n:2 * n]
        sems = refs[2 * n:2 * n + 4 * ng]
        token = refs[-1]
        for gi in range(ng):
            lo, hi = bounds[gi], bounds[gi + 1]
            for cp in _ici_copies(srcs[lo:hi], lands[lo:hi], sems[4 * gi:4 * gi + 2]):
                cp.start()
        for gi in range(ng):
            lo, hi = bounds[gi], bounds[gi + 1]
            for cp in _own_copies(srcs[lo:hi], lands[lo:hi], sems[4 * gi + 2:4 * gi + 4]):
                cp.start()
        token[...] = jnp.zeros_like(token)

    sem_shapes = []
    for g in groups:
        sem_shapes += [pltpu.SemaphoreType.DMA((3 * len(g),))] * 2 + [pltpu.SemaphoreType.DMA((len(g),))] * 2
    res = pl.pallas_call(
        body, name=name,
        out_shape=tuple(sem_shapes) + tuple(pltpu.HBM(a.shape, a.dtype) for a in arrs)
        + tuple(pltpu.HBM((N_CHIPS,) + a.shape, a.dtype) for a in arrs) + (jax.ShapeDtypeStruct((SUBLANES, LANES), F32),),
        in_specs=[HBM] * (2 * n),
        out_specs=tuple([SEM] * (4 * ng)) + tuple([HBM] * (2 * n)) + (pl.BlockSpec(memory_space=pltpu.VMEM),),
        input_output_aliases={i: 4 * ng + i for i in range(2 * n)},
        compiler_params=pltpu.CompilerParams(has_side_effects=EFFECT),
    )(*[_hbm(a) for a in arrs], *[_hbm(lax.empty((N_CHIPS,) + a.shape, a.dtype)) for a in arrs])
    sems, thru, lands, token = res[:4 * ng], res[4 * ng:4 * ng + n], res[4 * ng + n:4 * ng + 2 * n], res[-1]
    out = [{"ici": (sems[4 * gi], sems[4 * gi + 1]), "own": (sems[4 * gi + 2], sems[4 * gi + 3]),
            "thru": list(thru[bounds[gi]:bounds[gi + 1]]), "lands": list(lands[bounds[gi]:bounds[gi + 1]])} for gi in range(ng)]
    return out, token


def gather_forward(group, after, *, name):
    thru, lands = group["thru"], group["lands"]
    n = len(thru)

    def body(*refs):
        srcs, lands_r = refs[:n], refs[n:2 * n]
        ici = refs[2 * n:2 * n + 2]
        fwd = refs[2 * n + 3 + 2 * n:2 * n + 3 + 2 * n + 2]
        for cp in _ici_copies(srcs, lands_r, ici):
            cp.wait_send()
            cp.wait_recv()
        for cp in _fwd_copies(srcs, lands_r, fwd, _me()[2]):
            cp.start()

    res = pl.pallas_call(
        body, name=name,
        out_shape=tuple(pltpu.HBM(a.shape, a.dtype) for a in thru) + tuple(pltpu.HBM(a.shape, a.dtype) for a in lands)
        + (pltpu.SemaphoreType.DMA((3 * n,)),) * 2,
        in_specs=[HBM] * (2 * n) + [SEM, SEM, pl.BlockSpec(memory_space=pl.ANY)],
        out_specs=tuple([HBM] * (2 * n)) + (SEM, SEM),
        input_output_aliases={i: i for i in range(2 * n)},
        compiler_params=pltpu.CompilerParams(has_side_effects=EFFECT),
    )(*thru, *lands, *group["ici"], after)
    return {"own": group["own"], "fwd": (res[2 * n], res[2 * n + 1]), "thru": list(res[:n]), "lands": list(res[n:2 * n])}


def gather_wait(group, after, *, name):
    thru, lands = group["thru"], group["lands"]
    n = len(thru)

    def body(*refs):
        srcs, lands_r = refs[:n], refs[n:2 * n]
        own, fwd = refs[2 * n:2 * n + 2], refs[2 * n + 2:2 * n + 4]
        c = _me()[2]
        for mine, theirs in zip(_fwd_copies(srcs, lands_r, fwd, c), _fwd_copies(srcs, lands_r, fwd, 1 - c)):
            mine.wait_send()
            theirs.wait_recv()
        for cp in _own_copies(srcs, lands_r, own):
            cp.wait_send()
            cp.wait_recv()

    res = pl.pallas_call(
        body, name=name,
        out_shape=tuple(pltpu.HBM(a.shape, a.dtype) for a in thru) + tuple(pltpu.HBM(a.shape, a.dtype) for a in lands),
        in_specs=[HBM] * (2 * n) + [SEM] * 4 + [pl.BlockSpec(memory_space=pl.ANY)],
        out_specs=tuple([HBM] * (2 * n)),
        input_output_aliases={i: i for i in range(2 * n)},
        compiler_params=pltpu.CompilerParams(has_side_effects=EFFECT),
    )(*thru, *lands, *group["own"], *group["fwd"], after)
    return list(res[n:])


def _pair_copies(srcs, lands, sems):
    x, y, c = _me()
    return [pltpu.make_async_remote_copy(src_ref=srcs[i].at[:, _half_rows(srcs[i], 1 - c), :], dst_ref=lands[i], send_sem=sems[0].at[i],
                                         recv_sem=sems[1].at[i], device_id=(x, y, 1 - c), device_id_type=MESH)
            for i in range(len(srcs))]


def _scatter_copies(srcs, lands, sems):
    x, y, c = _me()
    return [pltpu.make_async_remote_copy(src_ref=srcs[i].at[2 * ch[0] + ch[1]], dst_ref=lands[i].at[j], send_sem=sems[0].at[3 * i + j],
                                         recv_sem=sems[1].at[3 * i + j], device_id=(ch[0], ch[1], c), device_id_type=MESH)
            for i in range(len(srcs)) for j, ch in enumerate(_other_chips(x, y))]


def split_start(copies, arrs, land_shapes, nsem, *, name):
    n = len(arrs)

    def body(*refs):
        for cp in copies(refs[:n], refs[n:2 * n], refs[2 * n:2 * n + 2]):
            cp.start()
        refs[-1][...] = jnp.zeros_like(refs[-1])

    res = pl.pallas_call(
        body, name=name,
        out_shape=(pltpu.SemaphoreType.DMA((nsem,)),) * 2 + tuple(pltpu.HBM(a.shape, a.dtype) for a in arrs)
        + tuple(pltpu.HBM(s, a.dtype) for s, a in zip(land_shapes, arrs)) + (jax.ShapeDtypeStruct((SUBLANES, LANES), F32),),
        in_specs=[HBM] * (2 * n), out_specs=(SEM, SEM) + tuple([HBM] * (2 * n)) + (pl.BlockSpec(memory_space=pltpu.VMEM),),
        input_output_aliases={i: 2 + i for i in range(2 * n)},
        compiler_params=pltpu.CompilerParams(has_side_effects=EFFECT),
    )(*[_hbm(a) for a in arrs], *[_hbm(lax.empty(s, a.dtype)) for s, a in zip(land_shapes, arrs)])
    return {"copies": copies, "sems": (res[0], res[1]), "thru": list(res[2:2 + n]), "lands": list(res[2 + n:2 + 2 * n]),
            "token": res[-1]}


def split_wait(state, after, *, name):
    thru, lands, copies = state["thru"], state["lands"], state["copies"]
    n = len(thru)

    def body(*refs):
        for cp in copies(refs[:n], refs[n:2 * n], refs[2 * n:2 * n + 2]):
            cp.wait_send()
            cp.wait_recv()

    res = pl.pallas_call(
        body, name=name,
        out_shape=tuple(pltpu.HBM(a.shape, a.dtype) for a in thru) + tuple(pltpu.HBM(a.shape, a.dtype) for a in lands),
        in_specs=[HBM] * (2 * n) + [SEM, SEM, pl.BlockSpec(memory_space=pl.ANY)], out_specs=tuple([HBM] * (2 * n)),
        input_output_aliases={i: i for i in range(2 * n)},
        compiler_params=pltpu.CompilerParams(has_side_effects=EFFECT),
    )(*thru, *lands, *state["sems"], after)
    return list(res[:n]), list(res[n:])


def sibling_send_other_half(arrs, *, name):
    n = len(arrs)

    def body(*refs):
        ins, outs = refs[:n], refs[n:2 * n]
        send_sems, recv_sems = refs[2 * n:]
        x, y, c = _me()
        cps = []
        for i in range(n):
            hrows = arrs[i].shape[1] // 2
            cps.append(pltpu.make_async_remote_copy(
                src_ref=ins[i].at[:, pl.ds((1 - c) * hrows, hrows), :], dst_ref=outs[i], send_sem=send_sems.at[i],
                recv_sem=recv_sems.at[i], device_id=(x, y, 1 - c), device_id_type=MESH))
        for cp in cps:
            cp.start()
        for cp in cps:
            cp.wait_recv()
        for cp in cps:
            cp.wait_send()

    return pl.pallas_call(
        body, name=name,
        out_shape=tuple(jax.ShapeDtypeStruct((a.shape[0], a.shape[1] // 2, a.shape[2]), a.dtype) for a in arrs),
        in_specs=[ANY] * n, out_specs=tuple([ANY] * n),
        scratch_shapes=[pltpu.SemaphoreType.DMA((n,)), pltpu.SemaphoreType.DMA((n,))],
        compiler_params=pltpu.CompilerParams(has_side_effects=True),
    )(*arrs)


def chip_scatter(arrs, *, name):
    n = len(arrs)

    def body(*refs):
        ins, outs = refs[:n], refs[n:2 * n]
        send_sems, recv_sems = refs[2 * n:]
        x, y, c = _me()
        cps = []
        for i in range(n):
            for j, ch in enumerate(_other_chips(x, y)):
                cps.append(pltpu.make_async_remote_copy(
                    src_ref=ins[i].at[2 * ch[0] + ch[1]], dst_ref=outs[i].at[j], send_sem=send_sems.at[3 * i + j],
                    recv_sem=recv_sems.at[3 * i + j], device_id=(ch[0], ch[1], c), device_id_type=MESH))
        for cp in cps:
            cp.start()
        for cp in cps:
            cp.wait_recv()
        for cp in cps:
            cp.wait_send()

    return pl.pallas_call(
        body, name=name,
        out_shape=tuple(jax.ShapeDtypeStruct((3,) + a.shape[1:], a.dtype) for a in arrs),
        in_specs=[ANY] * n, out_specs=tuple([ANY] * n),
        scratch_shapes=[pltpu.SemaphoreType.DMA((3 * n,)), pltpu.SemaphoreType.DMA((3 * n,))],
        compiler_params=pltpu.CompilerParams(has_side_effects=True),
    )(*arrs)


def sibling_join_halves(arrs, *, name):
    n = len(arrs)

    def body(*refs):
        outs = refs[n:2 * n]
        send_sems, recv_sems = refs[2 * n:]
        x, y, c = _me()
        cps = []
        for i in range(n):
            hrows = arrs[i].shape[0] // 2
            mine = outs[i].at[pl.ds(c * hrows, hrows)]
            cps.append(pltpu.make_async_remote_copy(src_ref=mine, dst_ref=mine, send_sem=send_sems.at[i],
                                                    recv_sem=recv_sems.at[i], device_id=(x, y, 1 - c), device_id_type=MESH))
        for cp in cps:
            cp.start()
        for i in range(n):
            hrows = arrs[i].shape[0] // 2
            theirs = outs[i].at[pl.ds((1 - c) * hrows, hrows)]
            pltpu.make_async_remote_copy(src_ref=theirs, dst_ref=theirs, send_sem=send_sems.at[i], recv_sem=recv_sems.at[i],
                                         device_id=(x, y, 1 - c), device_id_type=MESH).wait_recv()
        for cp in cps:
            cp.wait_send()

    return pl.pallas_call(
        body, name=name,
        out_shape=tuple(jax.ShapeDtypeStruct(a.shape, a.dtype) for a in arrs),
        in_specs=[ANY] * n, out_specs=tuple([ANY] * n),
        input_output_aliases={i: i for i in range(n)},
        scratch_shapes=[pltpu.SemaphoreType.DMA((n,)), pltpu.SemaphoreType.DMA((n,))],
        compiler_params=pltpu.CompilerParams(has_side_effects=True),
    )(*arrs)


def pre_reduce(gd, sib, who, *, name):
    _, r, c = gd.shape
    h = r // 2
    tr = _pick(h, (256, 128, 64, 32, 16, 8))
    nrb = h // tr

    def body(who_ref, gd_ref, sib_ref, pb_ref, own_ref):
        p = gd_ref[...] + sib_ref[...]
        pb_ref[...] = p.astype(BF16)

        @pl.when(pl.program_id(1) == who_ref[1])
        def _():
            own_ref[...] = p

    return pl.pallas_call(
        body, name=name,
        out_shape=(jax.ShapeDtypeStruct((N_CHIPS, h, c), BF16), jax.ShapeDtypeStruct((h, c), F32)),
        grid_spec=pltpu.PrefetchScalarGridSpec(
            num_scalar_prefetch=1, grid=(nrb, N_CHIPS),
            in_specs=[pl.BlockSpec((None, tr, c), lambda i, s, w: (s, w[0] * nrb + i, 0)),
                      pl.BlockSpec((None, tr, c), lambda i, s, w: (s, i, 0))],
            out_specs=(pl.BlockSpec((None, tr, c), lambda i, s, w: (s, i, 0)),
                       pl.BlockSpec((tr, c), lambda i, s, w: (i, 0)))),
        compiler_params=_cparams(("parallel", "arbitrary")),
    )(who, gd, sib)


def final_reduce(own, rcv, who, *, name):
    h, c = own.shape
    tr = _pick(h, (256, 128, 64, 32, 16, 8))
    nrb = h // tr

    def body(who_ref, own_ref, rcv_ref, o_ref):
        del who_ref
        acc = own_ref[...]
        for j in range(3):
            acc = acc + rcv_ref[j].astype(F32)
        o_ref[...] = acc

    return pl.pallas_call(
        body, name=name, out_shape=jax.ShapeDtypeStruct((2 * h, c), F32),
        grid_spec=pltpu.PrefetchScalarGridSpec(
            num_scalar_prefetch=1, grid=(nrb,),
            in_specs=[pl.BlockSpec((tr, c), lambda i, w: (i, 0)), pl.BlockSpec((3, tr, c), lambda i, w: (0, i, 0))],
            out_specs=pl.BlockSpec((tr, c), lambda i, w: (w[0] * nrb + i, 0))),
        compiler_params=_cparams(("parallel",)),
    )(who, own, rcv)


def sum8(landed, own, me, *, name):
    _, r, c = landed.shape
    tr = _pick(r, (256, 128, 64, 32, 16, 8))

    def body(me_ref, a_ref, own_ref, o_ref):
        acc = None
        for j in range(8):
            term = jnp.where(me_ref[0] == j, own_ref[...], a_ref[j])
            acc = term if acc is None else acc + term
        o_ref[...] = acc

    return pl.pallas_call(
        body, name=name, out_shape=jax.ShapeDtypeStruct((r, c), F32),
        grid_spec=pltpu.PrefetchScalarGridSpec(
            num_scalar_prefetch=1, grid=(r // tr,),
            in_specs=[pl.BlockSpec((8, tr, c), lambda i, w: (0, i, 0)), pl.BlockSpec((tr, c), lambda i, w: (i, 0))],
            out_specs=pl.BlockSpec((tr, c), lambda i, w: (i, 0))),
        compiler_params=_cparams(("parallel",)),
    )(me, landed, own)


def _everyone_copies(srcs, lands, sems):
    x, y, c = _me()
    me = 4 * x + 2 * y + c
    cps = []
    for k in range(1, 8):
        fx, fy, fc = (k >> 2) & 1, (k >> 1) & 1, k & 1
        to = (x + fx - 2 * x * fx, y + fy - 2 * y * fy, c + fc - 2 * c * fc)
        cps.append(pltpu.make_async_remote_copy(src_ref=srcs[0], dst_ref=lands[0].at[me], send_sem=sems[0].at[k - 1],
                                                recv_sem=sems[1].at[k - 1], device_id=to, device_id_type=MESH))
    return cps


def _behind(p, *tokens):
    for tk in tokens:
        if tk is not None:
            p = p + tk[0, 0]
    return p


def local_step(x, tgt, wb, ws, arrive=lambda layer, event, after: {}, reduce=lambda group, event, gb, after: None):
    t, d = x.shape
    nh = ws["alog"].shape[0]
    ng = ws["w_s"].shape[0]
    assert nh == ng
    mix_w = (nh + ng) * HEAD_DIM

    (h0,) = rows_fwd(stage_norm, [ws["e_norm"]], [x], [(1, BF16, d)], name="f_norm_e")
    pm = mm_nt(h0, wb["w_main_t"], name="f_proj_main")
    pba = mm_nt(h0, wb["w_ba_t"], name="f_proj_ba")
    qkv = conv_fwd(pm, ws["conv_w"], mode="silu", name="f_conv4")
    arrive("layer0", "landed", qkv)
    cat, states, tinvs = delta_fwd(qkv, pm, pba, ws["alog"], ws["dtb"], ws["onorm"], mix_w, name="f_delta")
    cat = gmlp_fwd(pm, ws["lng"], ws["lnb"], ws["w_s"], ws["bs"], cat, name="f_gmlp")
    wb = {**wb, **arrive("layer0", "joined", cat)}
    y0 = mm_nn(cat, wb["w_out"], name="f_out")
    x1, h1 = rows_fwd(stage_res_norm, [ws["f_norm0"]], [x, y0], [(0, F32, d), (1, BF16, d)], name="f_norm_f0")
    a1, s1 = mm_nn(h1, wb["w1_0"], name="f_mlp0_up", epilogue="relu2")
    arrive("layer1", "landed", s1)
    y1 = mm_nn(s1, wb["w2_0"], name="f_mlp0_down")
    x2, h2 = rows_fwd(stage_res_norm, [ws["o_norm"]], [x1, y1], [(0, F32, d), (1, BF16, d)], name="f_norm_o")
    wb = {**wb, **arrive("layer1", "joined", h2)}
    zz = mm_nn(h2, wb["pw1"], name="f_pw1")
    zparts = [(zz, d, 0), (zz, d, 1)]
    (gl,) = rows_fwd(stage_glu, [ws["b1a"], ws["b1b"]], zparts, [(0, F32, d)], name="f_glu")
    cv = conv_fwd(gl, ws["dw"], mode="plain", name="f_conv31")
    ln_params = [ws["dw_b"], ws["ln_g"], ws["ln_b"]]
    (sl,) = rows_fwd(stage_ln_silu, ln_params, [cv], [(0, BF16, d)], name="f_ln_silu")
    y2 = mm_nn(sl, wb["pw2"], name="f_pw2")
    x3, h3 = rows_fwd(stage_res_bias_norm, [ws["b2"], ws["f_norm1"]], [x2, y2], [(0, F32, d), (1, BF16, d)], name="f_norm_f1")
    a3, s3 = mm_nn(h3, wb["w1_1"], name="f_mlp1_up", epilogue="relu2")
    y3 = mm_nn(s3, wb["w2_1"], name="f_mlp1_down")
    loss, d4, d4b, g_final = loss_head(ws["final_norm"], x3, y3, tgt, name="loss_head")

    gb, gs = {}, {"final_norm": g_final}
    s_up = wb["w1_0"].shape[0]

    gb["w2_1"] = mm_tn(s3, d4b, groups=1, name="b_mlp1_down_w")
    dpre3 = mm_nt(d4b, wb["w2_1"], name="b_mlp1_down_x", mul=a3, out_dtype=BF16)
    gb["w1_1"] = mm_tn(h3, dpre3, groups=s_up, name="b_mlp1_up_w")
    tok = reduce("mlp1", "ready", gb, None)
    dh3 = mm_nt(dpre3, wb["w1_1"], name="b_mlp1_up_x", out_dtype=BF16)
    d3, d3b, gs["b2"], gs["f_norm1"] = rows_bwd(stage_res_bias_norm, [ws["b2"], _behind(ws["f_norm1"], tok)], [x2, y2], [d4, dh3],
                                                [(0, F32), (1, BF16)], name="b_norm_f1")
    tok = reduce("mlp1", "paired", gb, d3)
    gb["pw2"] = mm_tn(sl, d3b, groups=1, name="b_pw2_w")
    dsl = mm_nt(d3b, wb["pw2"], name="b_pw2_x", out_dtype=BF16)
    dcv, gs["dw_b"], gs["ln_g"], gs["ln_b"] = rows_bwd(stage_ln_silu, [_behind(ln_params[0], tok)] + ln_params[1:], [cv], [dsl],
                                                       [(0, F32)], name="b_ln_silu")
    dgl, gs["dw"] = conv_bwd(gl, ws["dw"], dcv, name="b_conv31")
    dzz, gs["b1a"], gs["b1b"] = rows_bwd(stage_glu, [ws["b1a"], ws["b1b"]], zparts, [dgl], [(0, BF16), (1, BF16)],
                                         name="b_glu", concat=True)
    gb["pw1"] = mm_tn(h2, dzz, groups=wb["pw1"].shape[0], name="b_pw1_w")
    tok = reduce("conv", "ready", gb, None)
    dh2 = mm_nt(dzz, wb["pw1"], name="b_pw1_x", out_dtype=BF16)
    d2, d2b, gs["o_norm"] = rows_bwd(stage_res_norm, [_behind(ws["o_norm"], tok)], [x1, y1], [d3, dh2], [(0, F32), (1, BF16)],
                                     name="b_norm_o")
    tok_conv = reduce("conv", "paired", gb, d2)
    gb["w2_0"] = mm_tn(s1, d2b, groups=1, name="b_mlp0_down_w")
    dpre1 = mm_nt(d2b, wb["w2_0"], name="b_mlp0_down_x", mul=a1, out_dtype=BF16)
    gb["w1_0"] = mm_tn(h1, dpre1, groups=s_up, name="b_mlp0_up_w")
    tok = reduce("mlp0", "ready", gb, None)
    dh1 = mm_nt(dpre1, wb["w1_0"], name="b_mlp0_up_x", out_dtype=BF16)
    d1, d1b, gs["f_norm0"] = rows_bwd(stage_res_norm, [_behind(ws["f_norm0"], tok_conv, tok)], [x, y0], [d2, dh1],
                                      [(0, F32), (1, BF16)], name="b_norm_f0")
    tok_mlp0 = reduce("mlp0", "paired", gb, d1)
    gb["w_out"] = mm_tn(cat, d1b, groups=1, name="b_out_w")
    tok = reduce("mixer", "ready", gb, None)
    dcat = mm_nt(d1b, wb["w_out"], name="b_out_x")
    dqkv_c, dpm, dpba, gs["alog"], gs["dtb"], gs["onorm"] = delta_bwd(
        qkv, pm, pba, ws["alog"], ws["dtb"], _behind(ws["onorm"], tok_mlp0, tok), states, tinvs, dcat, name="b_delta")
    tok = reduce("mixer", "paired", gb, dqkv_c)
    dpm, gs["lng"], gs["lnb"], gs["w_s"], gs["bs"] = gmlp_bwd(
        pm, ws["lng"], ws["lnb"], ws["w_s"], ws["bs"], dcat, dpm, name="b_gmlp")
    conv_w = _behind(ws["conv_w"], tok)
    dconv = conv_fwd(pm, conv_w, mode="silu_bwd", g=dqkv_c, name="b_conv4_silu")
    dpm, gs["conv_w"] = conv_bwd(pm, conv_w, dconv, name="b_conv4", into=dpm)
    gb["w_main_t"] = mm_tn(dpm, h0, groups=1, name="b_proj_main_w")
    gb["w_ba_t"] = mm_tn(dpba, h0, groups=1, name="b_proj_ba_w")
    dh0 = mm_nn(dpm, wb["w_main_t"], name="b_proj_main_x")
    dh0 = mm_nn(dpba, wb["w_ba_t"], name="b_proj_ba_x", add=dh0, out_dtype=BF16)
    grad_x, gs["e_norm"] = rows_bwd(stage_norm, [ws["e_norm"]], [x], [d1, dh0], [(0, F32)], name="b_norm_e")
    return loss, grad_x, gb, gs


WEIGHTS = ["e_norm", "e_w_in", "e_conv_w", "e_a_log", "e_dt_bias", "e_o_norm", "e_ln_g", "e_ln_b", "e_w_s", "e_b_s", "e_w_out",
           "o_norm", "o_pw1", "o_pw1_b", "o_dw", "o_dw_b", "o_ln_g", "o_ln_b", "o_pw2", "o_pw2_b", "f_norm", "f_w1", "f_w2",
           "final_norm"]
BIG = ["e_w_in", "e_w_out", "o_pw1", "o_pw2", "f_w1", "f_w2"]
SMALL_SHARDED = ["e_conv_w", "o_norm", "o_pw1_b", "o_dw", "o_dw_b", "o_ln_g", "o_ln_b", "o_pw2_b"]
SMALL = [n for n in WEIGHTS if n not in BIG]
LANES = 128
PACK_ROWS = 16
IN_ROW_MULT = 256
REDUCE_GROUPS = {"mlp1": ("w2_1", "w1_1"), "conv": ("pw2", "pw1"), "mlp0": ("w2_0", "w1_0"), "mixer": ("w_out",)}


def _pack(arrs):
    flat = jnp.concatenate([a.reshape(-1).astype(F32) for a in arrs])
    n = flat.shape[0]
    rows = -(-n // (LANES * PACK_ROWS)) * PACK_ROWS
    return jnp.pad(flat, (0, rows * LANES - n)).reshape(rows, LANES)


def _unpack(flat, shapes, lead=()):
    outs, off = [], 0
    for s in shapes:
        n = int(np.prod(s))
        outs.append(flat[..., off:off + n].reshape(lead + tuple(s)))
        off += n
    return outs


def _chip_major(g, s=N_CHIPS):
    k, n = g.shape
    return g.reshape(k, s, n // s).transpose(1, 0, 2)


def kernel(x, e_norm, e_w_in, e_conv_w, e_a_log, e_dt_bias, e_o_norm, e_ln_g, e_ln_b, e_w_s, e_b_s, e_w_out, o_norm, o_pw1, o_pw1_b, o_dw, o_dw_b, o_ln_g, o_ln_b, o_pw2, o_pw2_b, f_norm, f_w1, f_w2, final_norm, loss_target, m_e_norm, m_e_w_in, m_e_conv_w, m_e_a_log, m_e_dt_bias, m_e_o_norm, m_e_ln_g, m_e_ln_b, m_e_w_s, m_e_b_s, m_e_w_out, m_o_norm, m_o_pw1, m_o_pw1_b, m_o_dw, m_o_dw_b, m_o_ln_g, m_o_ln_b, m_o_pw2, m_o_pw2_b, m_f_norm, m_f_w1, m_f_w2, m_final_norm, v_e_norm, v_e_w_in, v_e_conv_w, v_e_a_log, v_e_dt_bias, v_e_o_norm, v_e_ln_g, v_e_ln_b, v_e_w_s, v_e_b_s, v_e_w_out, v_o_norm, v_o_pw1, v_o_pw1_b, v_o_dw, v_o_dw_b, v_o_ln_g, v_o_ln_b, v_o_pw2, v_o_pw2_b, v_f_norm, v_f_w1, v_f_w2, v_final_norm):
    a = dict(locals())
    xi, yi, ci = _me()
    chip = 2 * xi + yi
    who = jnp.stack([ci, chip]).astype(jnp.int32)
    t, d = x.shape[1], x.shape[2]
    nh, ng = e_a_log.shape[1], e_w_s.shape[1]
    n_qkv, n_av, n_bw = 3 * nh * HEAD_DIM, nh * HEAD_DIM, ng * HEAD_DIM
    in_cols = n_qkv + n_av + 2 * nh + 2 * n_bw
    c_ba = n_qkv + n_av

    sh_in = in_cols // N_CHIPS
    pad_in = -(-sh_in // IN_ROW_MULT) * IN_ROW_MULT - sh_in
    w_in_t_local = jnp.pad(e_w_in[0].T.astype(BF16), ((0, pad_in), (0, 0)))
    small_local = [a[n] for n in SMALL_SHARDED]
    g_in, g_small = gather_shards([w_in_t_local, _pack(small_local)], name="gather_weights_in")
    travelling, token = gather_start([[w.astype(BF16) for w in (e_w_out[0], f_w1[0], f_w2[0])],
                                      [w.astype(BF16) for w in (o_pw1[0], o_pw2[0], f_w1[1], f_w2[1])]], name="gather_start")

    state = {"layer0": travelling[0], "layer1": travelling[1]}

    def arrive(layer, event, after):
        if event == "landed":
            state[layer] = gather_forward(state[layer], after, name="gather_forward_" + layer)
            return {}
        got = gather_wait(state[layer], after, name="gather_wait_" + layer)
        if layer == "layer0":
            g_out, g_w1, g_w2 = got
            return {"w_out": g_out.reshape(1, -1, d), "w1_0": g_w1, "w2_0": g_w2.reshape(1, -1, d)}
        g_pw1, g_pw2, g_w1, g_w2 = got
        return {"pw1": g_pw1, "pw2": g_pw2.reshape(1, -1, d), "w1_1": g_w1, "w2_1": g_w2.reshape(1, -1, d)}

    def in_rows(lo, hi):
        parts = []
        for s in range(N_CHIPS):
            a0, a1 = max(lo, s * sh_in), min(hi, (s + 1) * sh_in)
            if a0 < a1:
                parts.append(g_in[s, a0 - s * sh_in:a1 - s * sh_in])
        return parts

    wb = {
        "w_main_t": jnp.concatenate(in_rows(0, c_ba) + in_rows(c_ba + 2 * nh, in_cols), axis=0)[None],
        "w_ba_t": jnp.pad(jnp.concatenate(in_rows(c_ba, c_ba + 2 * nh), axis=0), ((0, LANES - 2 * nh), (0, 0)))[None],
    }
    pieces = _unpack(g_small.reshape(N_CHIPS, -1), [w.shape for w in small_local], lead=(N_CHIPS,))
    full = {n: jnp.moveaxis(p, 0, -2).reshape(p.shape[1:-1] + (N_CHIPS * p.shape[-1],)) for n, p in zip(SMALL_SHARDED, pieces)}
    ws = {
        "e_norm": e_norm + token[0, 0],
        "conv_w": full["e_conv_w"][0], "alog": e_a_log.reshape(nh, 1, 1), "dtb": e_dt_bias.reshape(nh, 1, 1),
        "onorm": e_o_norm, "lng": e_ln_g.reshape(ng, 1, HEAD_DIM), "lnb": e_ln_b.reshape(ng, 1, HEAD_DIM), "w_s": e_w_s[0],
        "bs": e_b_s.reshape(ng, HEAD_DIM, 1), "o_norm": full["o_norm"], "b1a": full["o_pw1_b"][:, :d], "b1b": full["o_pw1_b"][:, d:],
        "dw": full["o_dw"][0], "dw_b": full["o_dw_b"], "ln_g": full["o_ln_g"], "ln_b": full["o_ln_b"], "b2": full["o_pw2_b"],
        "f_norm0": f_norm[0:1], "f_norm1": f_norm[1:2], "final_norm": final_norm.reshape(1, d),
    }

    pending = {}

    def reduce(group, event, gb, after):
        names = REDUCE_GROUPS[group]
        if event == "ready":
            arrs = [gb[nme] if gb[nme].shape[0] == N_CHIPS else gb[nme].reshape(N_CHIPS, -1, d) for nme in names]
            pending[group] = split_start(_pair_copies, arrs, [(N_CHIPS, g.shape[1] // 2, g.shape[2]) for g in arrs], len(arrs),
                                         name="reduce_sibling_start_" + group)
            return pending[group]["token"]
        mine, theirs = split_wait(pending[group], after, name="reduce_sibling_wait_" + group)
        pairs = [pre_reduce(gd, sb, who, name="reduce_pair_" + nme) for nme, gd, sb in zip(names, mine, theirs)]
        sums = [p[0] for p in pairs]
        pending[group] = {"own": [p[1] for p in pairs],
                          "scatter": split_start(_scatter_copies, sums, [(3,) + s.shape[1:] for s in sums], 3 * len(sums),
                                                 name="reduce_chips_start_" + group)}
        return pending[group]["scatter"]["token"]

    loss, grad_x, gb, gs = local_step(x[0], loss_target[0], wb, ws, arrive, reduce)

    gm, gba = gb["w_main_t"][0], gb["w_ba_t"][0]
    g_in_t = jnp.concatenate([gm[:c_ba], gba[:2 * nh], gm[c_ba:]], axis=0).reshape(N_CHIPS, sh_in, d)
    g_in_padded = jnp.pad(g_in_t, ((0, 0), (0, pad_in), (0, 0)))
    in_pair = split_start(_pair_copies, [g_in_padded], [(N_CHIPS, (sh_in + pad_in) // 2, d)], 1, name="reduce_sibling_start_w_in")
    small_global = {
        "e_norm": gs["e_norm"], "e_conv_w": gs["conv_w"][None], "e_a_log": gs["alog"].reshape(1, nh), "e_dt_bias": gs["dtb"].reshape(1, nh),
        "e_o_norm": gs["onorm"], "e_ln_g": gs["lng"].reshape(1, n_bw), "e_ln_b": gs["lnb"].reshape(1, n_bw), "e_w_s": gs["w_s"][None],
        "e_b_s": gs["bs"].reshape(1, ng, HEAD_DIM), "o_norm": gs["o_norm"], "o_pw1_b": jnp.concatenate([gs["b1a"], gs["b1b"]], axis=1),
        "o_dw": gs["dw"][None], "o_dw_b": gs["dw_b"], "o_ln_g": gs["ln_g"], "o_ln_b": gs["ln_b"], "o_pw2_b": gs["b2"],
        "f_norm": jnp.concatenate([gs["f_norm0"], gs["f_norm1"]], axis=0), "final_norm": gs["final_norm"].reshape(d),
    }
    packed = _pack([small_global[n] for n in SMALL] + [loss])
    everyone = split_start(_everyone_copies, [packed], [(8,) + packed.shape], 7, name="reduce_small_start")

    who_then = who + (in_pair["token"][0, 0] + everyone["token"][0, 0]).astype(jnp.int32)
    half = {}
    for group, names in REDUCE_GROUPS.items():
        _, landed = split_wait(pending[group]["scatter"], grad_x, name="reduce_chips_wait_" + group)
        for nme, ow, rc in zip(names, pending[group]["own"], landed):
            half[nme] = final_reduce(ow, rc, who_then, name="reduce_sum_" + nme)
    r_out, r_pw1, r_pw2, r_w1_0, r_w1_1, r_w2_0, r_w2_1 = sibling_join_halves(
        [half[nme] for nme in ("w_out", "pw1", "pw2", "w1_0", "w1_1", "w2_0", "w2_1")], name="reduce_join")
    big_grads = {"e_w_out": [r_out], "o_pw1": [r_pw1], "o_pw2": [r_pw2], "f_w1": [r_w1_0, r_w1_1], "f_w2": [r_w2_0, r_w2_1]}

    mine, theirs = split_wait(in_pair, r_out, name="reduce_sibling_wait_w_in")
    in_sum, in_own = pre_reduce(mine[0], theirs[0], who, name="reduce_pair_w_in")
    in_scatter = split_start(_scatter_copies, [in_sum], [(3,) + in_sum.shape[1:]], 3, name="reduce_chips_start_w_in")
    out = {}
    prev = in_scatter["token"]
    for n in BIG:
        if n != "e_w_in":
            out[n] = adamw(a[n], big_grads[n], a["m_" + n], a["v_" + n], name="adamw_" + n, behind=prev)
            prev = out[n][1][0, :SUBLANES, :LANES]

    own_packed, landed = split_wait(everyone, prev, name="reduce_small_wait")
    summed = sum8(landed[0], own_packed[0], (4 * xi + 2 * yi + ci).astype(jnp.int32).reshape(1), name="reduce_small_sum")
    *small_full, total = _unpack(summed.reshape(-1), [small_global[n].shape for n in SMALL] + [()])
    small_grads = {}
    for n, g in zip(SMALL, small_full):
        if n in SMALL_SHARDED:
            width = a[n].shape[-1]
            g = lax.dynamic_slice_in_dim(g, chip * width, width, axis=g.ndim - 1)
        small_grads[n] = g
    sw, sm, sv, sg = (_pack([src[n] for n in SMALL])[None] for src in
                      ({n: a[n] for n in SMALL}, {n: a["m_" + n] for n in SMALL}, {n: a["v_" + n] for n in SMALL}, small_grads))
    res = adamw(sw, [sg[0]], sm, sv, name="adamw_small")

    _, landed = split_wait(in_scatter, res[1], name="reduce_chips_wait_w_in")
    (r_in,) = sibling_join_halves([final_reduce(in_own, landed[0], who, name="reduce_sum_w_in")], name="reduce_join_w_in")
    w_in_res = adamw(e_w_in.transpose(0, 2, 1), [r_in[:sh_in]], m_e_w_in.transpose(0, 2, 1), v_e_w_in.transpose(0, 2, 1),
                     name="adamw_e_w_in")
    out["e_w_in"] = tuple(r.transpose(0, 2, 1) for r in w_in_res)
    shapes = [a[n].shape for n in SMALL]
    unpacked = [_unpack(r.reshape(-1), shapes) for r in res]
    for i, n in enumerate(SMALL):
        out[n] = tuple(u[i] for u in unpacked)

    result = [total, grad_x[None]]
    for k in range(4):
        result += [out[n][k] for n in WEIGHTS]
    return tuple(result)
```

```python
import functools
import math

import jax
import jax.numpy as jnp
import numpy as np
from jax import lax
from jax.experimental import pallas as pl
from jax.experimental.pallas import tpu as pltpu

F32 = jnp.float32
BF16 = jnp.bfloat16

EPS = 1e-6
CHUNK = 64
PAIR = 2 * CHUNK
HEAD_DIM = 128
N_CHIPS = 4
ADAM_LR, ADAM_B1, ADAM_B2, ADAM_EPS, ADAM_WD, ADAM_STEP = 0.001, 0.9, 0.999, 1e-08, 0.01, 10

VMEM_LIMIT = 56 * 1024 * 1024


def _cparams(sem=None):
    return pltpu.CompilerParams(dimension_semantics=sem, vmem_limit_bytes=VMEM_LIMIT)


def _pick(n, prefs):
    for p in prefs:
        if n % p == 0:
            return p
    return n


def mm_nn(a, w, *, name, epilogue=None, add=None, out_dtype=F32, behind=None):
    m, k = a.shape
    s, _, ns = w.shape
    n = s * ns
    tm = _pick(m, (1024, 512, 256, 128))
    tn = _pick(ns, (1024, 512, 256, 128))
    tk = _pick(k, (2048, 1024, 512, 256, 128))
    nk = k // tk
    npb = ns // tn
    assert add is None or epilogue is None

    def body(a_ref, w_ref, *rest):
        if add is not None:
            add_ref, rest = rest[0], rest[1:]
        if behind is not None:
            rest = rest[1:]
        if epilogue == "relu2":
            o1_ref, o2_ref = rest[0], rest[1]
            acc_ref = rest[2] if nk > 1 else None
        else:
            o1_ref = rest[0]
            acc_ref = rest[1] if nk > 1 else None
        def finish(c, rows=slice(None)):
            if epilogue == "relu2":
                r = jnp.maximum(c, 0.0)
                o1_ref[rows, :] = r.astype(o1_ref.dtype)
                o2_ref[rows, :] = (r * r).astype(o2_ref.dtype)
            elif add is not None:
                o1_ref[rows, :] = (c + add_ref[rows, :].astype(F32)).astype(o1_ref.dtype)
            else:
                o1_ref[rows, :] = c.astype(o1_ref.dtype)

        part = jnp.dot(a_ref[...], w_ref[...], preferred_element_type=F32)
        if nk == 1:
            finish(part)
        else:
            kk = pl.program_id(2)

            @pl.when(kk == 0)
            def _():
                acc_ref[...] = part

            @pl.when(kk > 0)
            def _():
                acc_ref[...] += part

            @pl.when(kk == nk - 1)
            def _():
                finish(acc_ref[...])

    o_spec = pl.BlockSpec((tm, tn), lambda i, j, kk: (i, j))
    if epilogue == "relu2":
        out_shape = (jax.ShapeDtypeStruct((m, n), BF16), jax.ShapeDtypeStruct((m, n), BF16))
        out_specs = (o_spec, o_spec)
    else:
        out_shape = jax.ShapeDtypeStruct((m, n), out_dtype)
        out_specs = o_spec
    return pl.pallas_call(
        body, name=name, out_shape=out_shape,
        grid=(m // tm, n // tn, nk),
        in_specs=[pl.BlockSpec((tm, tk), lambda i, j, kk: (i, kk)),
                  pl.BlockSpec((None, tk, tn), lambda i, j, kk: (j // npb, kk, j % npb))] + ([o_spec] if add is not None else [])
        + ([pl.BlockSpec(behind.shape, lambda i, j, kk: (0, 0))] if behind is not None else []),
        out_specs=out_specs,
        scratch_shapes=[pltpu.VMEM((tm, tn), F32)] if nk > 1 else [],
        compiler_params=_cparams(("parallel", "parallel", "arbitrary")),
    )(*([a, w] + ([add] if add is not None else []) + ([behind] if behind is not None else [])))


def mm_nt(a, w, *, name, mul=None, add=None, out_dtype=F32):
    assert mul is None or add is None
    if add is not None:
        mul = add
    m, n = a.shape
    s, k, ns = w.shape
    assert n == s * ns
    tm = _pick(m, (1024, 512, 256, 128))
    tko = _pick(k, (1024, 512, 256, 128))
    tn = _pick(ns, (2048, 1024, 512, 256, 128))
    nn = n // tn
    npb = ns // tn

    def body(a_ref, w_ref, *rest):
        if mul is not None:
            m_ref, o_ref = rest[0], rest[1]
            acc_ref = rest[2] if nn > 1 else None
        else:
            m_ref, o_ref = None, rest[0]
            acc_ref = rest[1] if nn > 1 else None
        part = lax.dot_general(a_ref[...], w_ref[...], (((1,), (1,)), ((), ())), preferred_element_type=F32)

        def finish(c):
            if add is not None:
                c = c + m_ref[...].astype(F32)
            elif m_ref is not None:
                c = c * (2.0 * m_ref[...].astype(F32))
            o_ref[...] = c.astype(o_ref.dtype)

        if nn == 1:
            finish(part)
        else:
            kk = pl.program_id(2)

            @pl.when(kk == 0)
            def _():
                acc_ref[...] = part

            @pl.when(kk > 0)
            def _():
                acc_ref[...] += part

            @pl.when(kk == nn - 1)
            def _():
                finish(acc_ref[...])

    in_specs = [pl.BlockSpec((tm, tn), lambda i, j, kk: (i, kk)),
                pl.BlockSpec((None, tko, tn), lambda i, j, kk: (kk // npb, j, kk % npb))]
    args = [a, w]
    if mul is not None:
        in_specs.append(pl.BlockSpec((tm, tko), lambda i, j, kk: (i, j)))
        args.append(mul)
    return pl.pallas_call(
        body, name=name, out_shape=jax.ShapeDtypeStruct((m, k), out_dtype),
        grid=(m // tm, k // tko, nn),
        in_specs=in_specs,
        out_specs=pl.BlockSpec((tm, tko), lambda i, j, kk: (i, j)),
        scratch_shapes=[pltpu.VMEM((tm, tko), F32)] if nn > 1 else [],
        compiler_params=_cparams(("parallel", "parallel", "arbitrary")),
    )(*args)


def mm_tn(a, b, *, groups, name):
    m, k = a.shape
    _, n = b.shape
    ns = n // groups
    tko = _pick(k, (1024, 512, 256, 128))
    tn = _pick(ns, (1024, 512, 256, 128))
    tc = _pick(m, (2048, 1024, 512, 256, 128))
    nc = m // tc
    npb = ns // tn

    def body(a_ref, b_ref, o_ref):
        part = lax.dot_general(a_ref[...], b_ref[...], (((0,), (0,)), ((), ())), preferred_element_type=F32)
        kk = pl.program_id(2)

        @pl.when(kk == 0)
        def _():
            o_ref[...] = part

        @pl.when(kk > 0)
        def _():
            o_ref[...] += part

    return pl.pallas_call(
        body, name=name, out_shape=jax.ShapeDtypeStruct((groups, k, ns), F32),
        grid=(k // tko, n // tn, nc),
        in_specs=[pl.BlockSpec((tc, tko), lambda i, j, kk: (kk, i)),
                  pl.BlockSpec((tc, tn), lambda i, j, kk: (kk, j))],
        out_specs=pl.BlockSpec((None, tko, tn), lambda i, j, kk: (j // npb, i, j % npb)),
        compiler_params=_cparams(("parallel", "parallel", "arbitrary")),
    )(a, b)


ROWS = 128


def _full_spec(arr):
    nd = arr.ndim
    return pl.BlockSpec(arr.shape, lambda i, _nd=nd: (0,) * _nd)


def _row_spec(x, rows):
    if isinstance(x, tuple):
        _, w, cb = x
        return pl.BlockSpec((rows, w), lambda i, _cb=cb: (i, _cb))
    return pl.BlockSpec((rows, x.shape[1]), lambda i: (i, 0))


def _arr(x):
    return x[0] if isinstance(x, tuple) else x


def _width(x):
    return x[1] if isinstance(x, tuple) else x.shape[1]


def rows_fwd(fn, params, xs, stores, *, name, rows=ROWS):
    t = _arr(xs[0]).shape[0]
    np_, nx = len(params), len(xs)

    def body(*refs):
        p_refs, x_refs, o_refs = refs[:np_], refs[np_:np_ + nx], refs[np_ + nx:]
        outs = fn(*[r[...].astype(F32) for r in p_refs], *[r[...].astype(F32) for r in x_refs])
        for (idx, dt, _), o_ref in zip(stores, o_refs):
            o_ref[...] = outs[idx].astype(dt)

    res = pl.pallas_call(
        body, name=name,
        out_shape=tuple(jax.ShapeDtypeStruct((t, w), dt) for _, dt, w in stores),
        grid=(t // rows,),
        in_specs=[_full_spec(p) for p in params] + [_row_spec(x, rows) for x in xs],
        out_specs=tuple(pl.BlockSpec((rows, w), lambda i: (i, 0)) for _, _, w in stores),
        compiler_params=_cparams(("parallel",)),
    )(*params, *[_arr(x) for x in xs])
    return res


def rows_bwd(fn, params, xs, cts, dx_stores, *, name, rows=ROWS, concat=False):
    t = _arr(xs[0]).shape[0]
    np_, nx = len(params), len(xs)
    ct_idx = [i for i, c in enumerate(cts) if c is not None]
    ct_arrs = [cts[i] for i in ct_idx]
    nct = len(ct_arrs)
    nds = 1 if concat else len(dx_stores)
    widths = [_width(xs[xi]) for xi, _ in dx_stores]

    def body(*refs):
        p_refs = refs[:np_]
        x_refs = refs[np_:np_ + nx]
        c_refs = refs[np_ + nx:np_ + nx + nct]
        d_refs = refs[np_ + nx + nct:np_ + nx + nct + nds]
        g_refs = refs[np_ + nx + nct + nds:]
        pv = [r[...].astype(F32) for r in p_refs]
        xv = [r[...].astype(F32) for r in x_refs]
        outs, vjp = jax.vjp(lambda *a: tuple(fn(*a)), *pv, *xv)
        ct_full = [jnp.zeros_like(o) for o in outs]
        for i, r in zip(ct_idx, c_refs):
            ct_full[i] = r[...].astype(F32)
        grads = vjp(tuple(ct_full))
        if concat:
            off = 0
            for (xi, dt), wd in zip(dx_stores, widths):
                d_refs[0][:, off:off + wd] = grads[np_ + xi].astype(dt)
                off += wd
        else:
            for (xi, dt), d_ref in zip(dx_stores, d_refs):
                d_ref[...] = grads[np_ + xi].astype(dt)
        step = pl.program_id(0)
        for j, g_ref in enumerate(g_refs):
            @pl.when(step == 0)
            def _(g_ref=g_ref, j=j):
                g_ref[...] = grads[j]

            @pl.when(step > 0)
            def _(g_ref=g_ref, j=j):
                g_ref[...] += grads[j]

    dx_shapes = [(sum(widths), dx_stores[0][1])] if concat else [(wd, dt) for wd, (_, dt) in zip(widths, dx_stores)]
    out_shape = tuple(jax.ShapeDtypeStruct((t, wd), dt) for wd, dt in dx_shapes) + \
        tuple(jax.ShapeDtypeStruct(p.shape, F32) for p in params)
    out_specs = tuple(pl.BlockSpec((rows, wd), lambda i: (i, 0)) for wd, _ in dx_shapes) + \
        tuple(_full_spec(p) for p in params)
    return pl.pallas_call(
        body, name=name, out_shape=out_shape, grid=(t // rows,),
        in_specs=[_full_spec(p) for p in params] + [_row_spec(x, rows) for x in xs] + [_row_spec(c, rows) for c in ct_arrs],
        out_specs=out_specs,
        compiler_params=_cparams(("arbitrary",)),
    )(*params, *[_arr(x) for x in xs], *[_arr(c) for c in ct_arrs])


def _rms(x, g):
    return x * lax.rsqrt(jnp.mean(x * x, axis=-1, keepdims=True) + EPS) * g


def _sigmoid(x):
    return 1.0 / (1.0 + jnp.exp(-x))


def _silu(x):
    return x * _sigmoid(x)


def _gelu(x):
    return 0.5 * x * (1.0 + lax.erf(x * (1.0 / math.sqrt(2.0))))


def stage_norm(g, x):
    return x, _rms(x, g)


def stage_res_norm(g, x, y):
    xn = x + y
    return xn, _rms(xn, g)


def stage_res_bias_norm(b, g, x, y):
    xn = x + y + b
    return xn, _rms(xn, g)


def stage_glu(ba, bb, za, zb):
    return ((za + ba) * _sigmoid(zb + bb),)


def stage_ln_silu(dw_b, ln_g, ln_b, cv):
    z = cv + dw_b
    mu = jnp.mean(z, axis=-1, keepdims=True)
    zc = z - mu
    y = zc * lax.rsqrt(jnp.mean(zc * zc, axis=-1, keepdims=True) + EPS) * ln_g + ln_b
    return (_silu(y),)


CONV_ROWS = 128
CONV_COLS = 256
SUBLANES = 8


def _halo(k):
    return SUBLANES * ((k - 1 + SUBLANES - 1) // SUBLANES)


def _taps_by_roll(k):
    out = {}
    for s in range(k):
        out.setdefault(s % SUBLANES, []).append((s // SUBLANES, s))
    return out


def _shifted_down(win, k, r):
    halo, n = _halo(k), win.shape[0]
    segs = {}
    for b, lst in _taps_by_roll(k).items():
        rolled = win if b == 0 else pltpu.roll(win, b, axis=0)
        for a, s in lst:
            segs[s] = rolled[halo - SUBLANES * a: halo - SUBLANES * a + r]
    return segs


def _shifted_up(win, k, r):
    n = win.shape[0]
    segs = {}
    for b, lst in _taps_by_roll(k).items():
        rolled = win if b == 0 else pltpu.roll(win, n - b, axis=0)
        for a, s in lst:
            segs[s] = rolled[SUBLANES * a: SUBLANES * a + r]
    return segs


def _for_blocks(nblk, fn):
    fn(0, True, nblk == 1)
    if nblk > 2:
        def step(i, c):
            fn(i, False, False)
            return c
        lax.fori_loop(1, nblk - 1, step, 0)
    if nblk > 1:
        fn(nblk - 1, False, True)


def _base(i, r):
    return i * r if isinstance(i, int) else pl.multiple_of(i * r, r)


def _win_top(ref, i, first, r, halo):
    if first:
        return jnp.concatenate([jnp.zeros((halo, ref.shape[1]), F32), ref[pl.ds(0, r), :]], axis=0)
    base = _base(i, r)
    return ref[pl.ds(base - halo, r + halo), :]


def _win_bottom(ref, i, last, r, halo):
    base = _base(i, r)
    if last:
        return jnp.concatenate([ref[pl.ds(base, r), :], jnp.zeros((halo, ref.shape[1]), F32)], axis=0)
    return ref[pl.ds(base, r + halo), :]


def conv_fwd(x, w, *, mode, name, g=None):
    t = x.shape[0]
    k, c = w.shape
    r, cb, halo = min(CONV_ROWS, t), min(CONV_COLS, c), _halo(k)
    nblk = t // r

    def body(*refs):
        if mode == "silu_bwd":
            x_ref, w_ref, g_ref, o_ref = refs
        else:
            x_ref, w_ref, o_ref = refs

        def blk(i, first, last):
            segs = _shifted_down(_win_top(x_ref, i, first, r, halo), k, r)
            acc = None
            for s in range(k):
                term = w_ref[pl.ds(k - 1 - s, 1), :] * segs[s]
                acc = term if acc is None else acc + term
            base = _base(i, r)
            if mode == "silu":
                acc = _silu(acc)
            elif mode == "silu_bwd":
                sg = _sigmoid(acc)
                acc = g_ref[pl.ds(base, r), :] * (sg * (1.0 + acc * (1.0 - sg)))
            o_ref[pl.ds(base, r), :] = acc

        _for_blocks(nblk, blk)

    col = pl.BlockSpec((t, cb), lambda j: (0, j))
    in_specs = [col, pl.BlockSpec((k, cb), lambda j: (0, j))] + ([col] if mode == "silu_bwd" else [])
    args = [x, w] + ([g] if mode == "silu_bwd" else [])
    return pl.pallas_call(
        body, name=name, out_shape=jax.ShapeDtypeStruct((t, c), F32), grid=(c // cb,),
        in_specs=in_specs, out_specs=col, compiler_params=_cparams(("parallel",)),
    )(*args)


def conv_bwd(x, w, dy, *, name, into=None):
    t = x.shape[0]
    k, c = w.shape
    r, cb, halo = min(CONV_ROWS, t), min(CONV_COLS, c), _halo(k)
    nblk = t // r

    def body(x_ref, w_ref, dy_ref, *rest):
        dx_ref, dw_ref = rest[-2], rest[-1]
        dw_ref[...] = jnp.zeros_like(dw_ref)

        def blk(i, first, last):
            base = _base(i, r)
            up = _shifted_up(_win_bottom(dy_ref, i, last, r, halo), k, r)
            down = _shifted_down(_win_top(x_ref, i, first, r, halo), k, r)
            dyb = up[0]
            acc = None
            for s in range(k):
                term = w_ref[pl.ds(k - 1 - s, 1), :] * up[s]
                acc = term if acc is None else acc + term
                dw_ref[pl.ds(k - 1 - s, 1), :] += jnp.sum(down[s] * dyb, axis=0, keepdims=True)
            dx_ref[pl.ds(base, r), :] = acc.astype(dx_ref.dtype)

        _for_blocks(nblk, blk)

    col = pl.BlockSpec((t, cb), lambda j: (0, j))
    wsp = pl.BlockSpec((k, cb), lambda j: (0, j))
    dx_shape = jax.ShapeDtypeStruct((t, c), F32) if into is None else jax.ShapeDtypeStruct(into.shape, into.dtype)
    return pl.pallas_call(
        body, name=name,
        out_shape=(dx_shape, jax.ShapeDtypeStruct((k, c), F32)), grid=(c // cb,),
        in_specs=[col, wsp, col] + ([] if into is None else [pl.BlockSpec(memory_space=pl.ANY)]),
        out_specs=(col, wsp),
        input_output_aliases={} if into is None else {3: 0},
        compiler_params=_cparams(("parallel",)),
    )(*([x, w, dy] + ([] if into is None else [into])))


_DIMS = {"nn": (((1,), (0,)), ((), ())), "nt": (((1,), (1,)), ((), ())), "tn": (((0,), (0,)), ((), ()))}
_DIMS_BATCHED = {"nn": (((2,), (1,)), ((0,), (0,))), "nt": (((2,), (2,)), ((0,), (0,))), "tn": (((1,), (1,)), ((0,), (0,)))}


def _mxu(a, b, mode):
    dims = _DIMS_BATCHED if a.ndim == 3 else _DIMS
    return lax.dot_general(a, b, dims[mode], preferred_element_type=F32)


def _split(x):
    hi = x.astype(BF16)
    return hi, (x - hi.astype(F32)).astype(BF16)


def _dot_raw(a, b, mode, prec):
    if prec == "bf16":
        return _mxu(a.astype(BF16), b.astype(BF16), mode)
    if prec == "x3":
        ah, al = _split(a)
        bh, bl = _split(b)
        return _mxu(ah, bh, mode) + (_mxu(ah, bl, mode) + _mxu(al, bh, mode))
    if prec == "x3r":
        bh, bm = _split(b)
        bl = (b - bh.astype(F32) - bm.astype(F32)).astype(BF16)
        ah = a.astype(BF16)
        return _mxu(ah, bh, mode) + (_mxu(ah, bm, mode) + _mxu(ah, bl, mode))
    raise ValueError(prec)


@functools.lru_cache(maxsize=None)
def _dot_fn(mode, prec):
    bprec = "x3" if prec == "x3r" else prec

    @jax.custom_vjp
    def f(a, b):
        return _dot_raw(a, b, mode, prec)

    def fwd(a, b):
        return _dot_raw(a, b, mode, prec), (a, b)

    def bwd(res, ct):
        a, b = res
        if mode == "nn":
            return _dot_raw(ct, b, "nt", bprec), _dot_raw(a, ct, "tn", bprec)
        if mode == "nt":
            return _dot_raw(ct, b, "nn", bprec), _dot_raw(ct, a, "tn", bprec)
        return _dot_raw(b, ct, "nt", bprec), _dot_raw(a, ct, "nn", bprec)

    f.defvjp(fwd, bwd)
    return f


def _dot(a, b, mode="nn", prec="bf16"):
    return _dot_fn(mode, prec)(a, b)


def _inv_product(l):
    n = l.shape[-1]
    eye = (lax.broadcasted_iota(jnp.int32, (n, n), 0) == lax.broadcasted_iota(jnp.int32, (n, n), 1)).astype(F32)
    p = eye - l
    pw = l
    for _ in range(5):
        pw = _dot_raw(pw, pw, "nn", "x3")
        p = _dot_raw(p, eye + pw, "nn", "x3")
    return p


@jax.custom_vjp
def _inv_unit_lower(l, t_saved):
    return t_saved


def _inv_fwd(l, t_saved):
    return t_saved, t_saved


def _inv_bwd(t, ct):
    tmp = _dot_raw(t, ct, "tn", "x3")
    return -_dot_raw(tmp, t, "nt", "x3"), jnp.zeros_like(t)


_inv_unit_lower.defvjp(_inv_fwd, _inv_bwd)


def _softplus(x):
    pos = x > 0
    return jnp.where(pos, x, 0.0) + jnp.log(1.0 + jnp.exp(jnp.where(pos, -x, x)))


def _l2n(x):
    return x * lax.rsqrt(jnp.sum(x * x, axis=-1, keepdims=True) + EPS)


def delta_pair(s0, qc, kc, vc, z, pba, alog, dtb, onorm, t_saved=None):
    n = PAIR
    nh = qc.shape[0]
    assert qc.shape == (nh, n, HEAD_DIM) and n == HEAD_DIM
    hi = lax.broadcasted_iota(jnp.int32, (nh, 1, pba.shape[1]), 0)
    li = lax.broadcasted_iota(jnp.int32, (nh, 1, pba.shape[1]), 2)
    braw = jnp.sum(pba[None] * (li == hi).astype(F32), axis=2, keepdims=True)
    araw = jnp.sum(pba[None] * (li == hi + nh).astype(F32), axis=2, keepdims=True)
    ri = lax.broadcasted_iota(jnp.int32, (n, n), 0)
    ci = lax.broadcasted_iota(jnp.int32, (n, n), 1)
    same = (ri // CHUNK) == (ci // CHUNK)
    tri = same & (ci <= ri)
    same_f = jnp.broadcast_to(same.astype(F32), (nh, n, n))
    tri_f = jnp.broadcast_to(tri.astype(F32), (nh, n, n))
    strict_f = (same & (ci < ri)).astype(F32)
    m0 = (lax.broadcasted_iota(jnp.int32, (n, 1), 0) < CHUNK).astype(F32)
    m1 = 1.0 - m0

    q = _l2n(qc) * (HEAD_DIM ** -0.5)
    k = _l2n(kc)
    beta = _sigmoid(braw)
    g = -jnp.exp(alog) * _softplus(araw + dtb)
    gb = jnp.broadcast_to(g, (nh, n, n))
    gc = _dot(tri_f, gb, "nn", "x3r")
    gtot = _dot(same_f, gb, "nn", "x3r")
    decay = jnp.exp(jnp.where(tri, gc - jnp.swapaxes(gc, 1, 2), -1e30))
    eg = jnp.exp(gc)
    kb, vb = k * beta, vc * beta
    l = _dot(kb, k, "nt") * decay * strict_f
    if t_saved is None:
        tinv = _inv_product(l)
    else:
        tinv = _inv_unit_lower(l, t_saved)
    u = _dot(tinv, vb, "nn", "x3")
    w = _dot(tinv, kb * eg, "nn", "x3")
    attn = _dot(q, k, "nt") * decay
    q_dec = q * eg
    k_tail = k * jnp.exp(gtot - gc)
    gl0 = jnp.exp(jnp.sum(gb * m0, axis=1, keepdims=True))
    gl1 = jnp.exp(jnp.sum(gb * m1, axis=1, keepdims=True))

    vn0 = m0 * (u - _dot(w, s0))
    s1 = s0 * gl0 + _dot(k_tail, vn0, "tn")
    vn1 = m1 * (u - _dot(w, s1))
    o = m0 * _dot(q_dec, s0) + m1 * _dot(q_dec, s1) + _dot(attn, vn0 + vn1)
    s2 = s1 * gl1 + _dot(k_tail, vn1, "tn")

    on = o * lax.rsqrt(jnp.mean(o * o, axis=-1, keepdims=True) + EPS) * onorm
    return on * _silu(z), s2, tinv


def _hcols(i):
    return slice(i * HEAD_DIM, (i + 1) * HEAD_DIM)


def _heads(ref, first, count):
    return jnp.stack([ref[:, _hcols(first + i)] for i in range(count)])


def delta_fwd(qkv, pm, pba, alog, dtb, onorm, cat_width, *, name):
    t = qkv.shape[0]
    h = alog.shape[0]
    hd = h * HEAD_DIM
    npair = t // PAIR
    mat = pl.BlockSpec((h, None, PAIR, HEAD_DIM), lambda p: (0, p, 0, 0))
    par = pl.BlockSpec((h, 1, 1), lambda p: (0, 0, 0))

    def body(qkv_ref, z_ref, pba_ref, al_ref, dt_ref, on_ref, o_ref, st_ref, ti_ref, s_scr):
        @pl.when(pl.program_id(0) == 0)
        def _():
            s_scr[...] = jnp.zeros_like(s_scr)

        s0 = s_scr[...]
        st_ref[...] = s0
        out, s2, tinv = delta_pair(s0, _heads(qkv_ref, 0, h), _heads(qkv_ref, h, h), _heads(qkv_ref, 2 * h, h), _heads(z_ref, 0, h),
                                   pba_ref[...], al_ref[...], dt_ref[...], on_ref[...])
        for i in range(h):
            o_ref[:, _hcols(i)] = out[i].astype(o_ref.dtype)
        ti_ref[...] = tinv
        s_scr[...] = s2

    return pl.pallas_call(
        body, name=name, grid=(npair,),
        out_shape=(jax.ShapeDtypeStruct((t, cat_width), BF16),
                   jax.ShapeDtypeStruct((h, npair, PAIR, HEAD_DIM), F32),
                   jax.ShapeDtypeStruct((h, npair, PAIR, PAIR), F32)),
        in_specs=[pl.BlockSpec((PAIR, 3 * hd), lambda p: (p, 0)), pl.BlockSpec((PAIR, hd), lambda p: (p, 3)),
                  pl.BlockSpec((PAIR, pba.shape[1]), lambda p: (p, 0)), par, par,
                  pl.BlockSpec((1, HEAD_DIM), lambda p: (0, 0))],
        out_specs=(pl.BlockSpec((PAIR, hd), lambda p: (p, 0)), mat, mat),
        scratch_shapes=[pltpu.VMEM((h, HEAD_DIM, HEAD_DIM), F32)],
        compiler_params=_cparams(("arbitrary",)),
    )(qkv, pm, pba, alog, dtb, onorm)


def delta_bwd(qkv, pm, pba, alog, dtb, onorm, states, tinvs, dcat, *, name):
    t = qkv.shape[0]
    h = alog.shape[0]
    hd = h * HEAD_DIM
    npair = t // PAIR
    rev = lambda p: npair - 1 - p
    mat = pl.BlockSpec((h, None, PAIR, HEAD_DIM), lambda p: (0, rev(p), 0, 0))
    par = pl.BlockSpec((h, 1, 1), lambda p: (0, 0, 0))
    onsp = pl.BlockSpec((1, HEAD_DIM), lambda p: (0, 0))
    wide = pl.BlockSpec((PAIR, 3 * hd), lambda p: (rev(p), 0))
    zsp = pl.BlockSpec((PAIR, hd), lambda p: (rev(p), 3))
    bsp = pl.BlockSpec((PAIR, pba.shape[1]), lambda p: (rev(p), 0))

    def body(qkv_ref, z_ref, pba_ref, al_ref, dt_ref, on_ref, st_ref, ti_ref, dc_ref,
             dqkv_ref, dz_ref, dpba_ref, dal_ref, ddt_ref, don_ref, ds_scr):
        @pl.when(pl.program_id(0) == 0)
        def _():
            ds_scr[...] = jnp.zeros_like(ds_scr)
            dal_ref[...] = jnp.zeros_like(dal_ref)
            ddt_ref[...] = jnp.zeros_like(ddt_ref)
            don_ref[...] = jnp.zeros_like(don_ref)

        tsv = ti_ref[...]
        fn = lambda s0, qc, kc, vc, z, pb, al, dt, on: delta_pair(s0, qc, kc, vc, z, pb, al, dt, on, tsv)[:2]
        _, vjp = jax.vjp(fn, st_ref[...], _heads(qkv_ref, 0, h), _heads(qkv_ref, h, h), _heads(qkv_ref, 2 * h, h),
                         _heads(z_ref, 0, h), pba_ref[...], al_ref[...], dt_ref[...], on_ref[...])
        ds0, dq, dk, dv, dz, dpba, dal, ddt, don = vjp((_heads(dc_ref, 0, h), ds_scr[...]))
        ds_scr[...] = ds0
        for i in range(h):
            dqkv_ref[:, _hcols(i)] = dq[i]
            dqkv_ref[:, _hcols(h + i)] = dk[i]
            dqkv_ref[:, _hcols(2 * h + i)] = dv[i]
            dz_ref[:, _hcols(i)] = dz[i].astype(dz_ref.dtype)
        dal_ref[...] += dal
        ddt_ref[...] += ddt
        dpba_ref[...] = dpba.astype(dpba_ref.dtype)
        don_ref[...] += don

    return pl.pallas_call(
        body, name=name, grid=(npair,),
        out_shape=(jax.ShapeDtypeStruct((t, 3 * hd), F32), jax.ShapeDtypeStruct(pm.shape, BF16),
                   jax.ShapeDtypeStruct(pba.shape, BF16),
                   jax.ShapeDtypeStruct((h, 1, 1), F32), jax.ShapeDtypeStruct((h, 1, 1), F32),
                   jax.ShapeDtypeStruct((1, HEAD_DIM), F32)),
        in_specs=[wide, zsp, bsp, par, par, onsp, mat, mat, pl.BlockSpec((PAIR, hd), lambda p: (rev(p), 0))],
        out_specs=(wide, zsp, bsp, par, par, onsp),
        scratch_shapes=[pltpu.VMEM((h, HEAD_DIM, HEAD_DIM), F32)],
        compiler_params=_cparams(("arbitrary",)),
    )(qkv, pm, pba, alog, dtb, onorm, states, tinvs, dcat)


def gmlp_block(ln_g, ln_b, w, bcol, u_raw, v_raw):
    n = w.shape[-1]
    ri = lax.broadcasted_iota(jnp.int32, (n, n), 0)
    ci = lax.broadcasted_iota(jnp.int32, (n, n), 1)
    mask = ((ci // CHUNK) <= (ri // CHUNK)).astype(F32)
    vg = _gelu(v_raw)
    vc = vg - jnp.mean(vg, axis=-1, keepdims=True)
    vgn = vc * lax.rsqrt(jnp.mean(vc * vc, axis=-1, keepdims=True) + EPS) * ln_g + ln_b
    return _gelu(u_raw) * (_dot(w * mask, vgn) + bcol)


def gmlp_fwd(pm, ln_g, ln_b, w_s, bcol, cat, *, name):
    t = pm.shape[0]
    g = w_s.shape[0]
    gw = g * HEAD_DIM
    assert pm.shape[1] == 6 * gw and cat.shape[1] == 2 * gw

    def body(u_ref, v_ref, lg_ref, lb_ref, w_ref, b_ref, cat_in, o_ref):
        del cat_in
        out = gmlp_block(lg_ref[...], lb_ref[...], w_ref[...], b_ref[...], _heads(u_ref, 0, g), _heads(v_ref, 0, g))
        for i in range(g):
            o_ref[:, _hcols(i)] = out[i].astype(o_ref.dtype)

    full = lambda a: pl.BlockSpec(a.shape, lambda m: (0, 0, 0))
    return pl.pallas_call(
        body, name=name, grid=(t // HEAD_DIM,),
        out_shape=jax.ShapeDtypeStruct(cat.shape, cat.dtype),
        in_specs=[pl.BlockSpec((HEAD_DIM, gw), lambda m: (m, 4)), pl.BlockSpec((HEAD_DIM, gw), lambda m: (m, 5)),
                  full(ln_g), full(ln_b), full(w_s), full(bcol), pl.BlockSpec(memory_space=pl.ANY)],
        out_specs=pl.BlockSpec((HEAD_DIM, gw), lambda m: (m, 1)),
        input_output_aliases={6: 0},
        compiler_params=_cparams(("arbitrary",)),
    )(pm, pm, ln_g, ln_b, w_s, bcol, cat)


def gmlp_bwd(pm, ln_g, ln_b, w_s, bcol, dcat, dpm, *, name):
    t = pm.shape[0]
    g = w_s.shape[0]
    gw = g * HEAD_DIM
    assert pm.shape[1] == 6 * gw and dpm.shape == pm.shape

    def body(u_ref, v_ref, lg_ref, lb_ref, w_ref, b_ref, dc_ref, dpm_in, duv_ref, dlg_ref, dlb_ref, dw_ref, db_ref):
        del dpm_in
        first = pl.program_id(0) == 0
        _, vjp = jax.vjp(gmlp_block, lg_ref[...], lb_ref[...], w_ref[...], b_ref[...], _heads(u_ref, 0, g), _heads(v_ref, 0, g))
        dlg, dlb, dw, db, du, dv = vjp(_heads(dc_ref, 0, g))
        for i in range(g):
            duv_ref[:, _hcols(i)] = du[i].astype(duv_ref.dtype)
            duv_ref[:, _hcols(g + i)] = dv[i].astype(duv_ref.dtype)
        for ref, val in ((dlg_ref, dlg), (dlb_ref, dlb), (dw_ref, dw), (db_ref, db)):
            @pl.when(first)
            def _(ref=ref, val=val):
                ref[...] = val

            @pl.when(jnp.logical_not(first))
            def _(ref=ref, val=val):
                ref[...] += val

    full = lambda a: pl.BlockSpec(a.shape, lambda m: (0, 0, 0))
    return pl.pallas_call(
        body, name=name, grid=(t // HEAD_DIM,),
        out_shape=(jax.ShapeDtypeStruct(dpm.shape, dpm.dtype),
                   jax.ShapeDtypeStruct(ln_g.shape, F32), jax.ShapeDtypeStruct(ln_b.shape, F32),
                   jax.ShapeDtypeStruct(w_s.shape, F32), jax.ShapeDtypeStruct(bcol.shape, F32)),
        in_specs=[pl.BlockSpec((HEAD_DIM, gw), lambda m: (m, 4)), pl.BlockSpec((HEAD_DIM, gw), lambda m: (m, 5)),
                  full(ln_g), full(ln_b), full(w_s), full(bcol), pl.BlockSpec((HEAD_DIM, gw), lambda m: (m, 1)),
                  pl.BlockSpec(memory_space=pl.ANY)],
        out_specs=(pl.BlockSpec((HEAD_DIM, 2 * gw), lambda m: (m, 2)), full(ln_g), full(ln_b), full(w_s), full(bcol)),
        input_output_aliases={7: 0},
        compiler_params=_cparams(("arbitrary",)),
    )(pm, pm, ln_g, ln_b, w_s, bcol, dcat, dpm)


def loss_head(g, x, r, tgt, *, name, rows=ROWS):
    t, d = x.shape

    def body(g_ref, x_ref, r_ref, t_ref, l_ref, dx_ref, dxb_ref, dg_ref):
        y, vjp = jax.vjp(lambda gg, xx: _rms(xx, gg), g_ref[...], x_ref[...] + r_ref[...])
        e = y - t_ref[...]
        part = (0.5 / d) * jnp.sum(jnp.sum(e * e, axis=1, keepdims=True), axis=0, keepdims=True)
        dg, dx = vjp(e * (1.0 / d))
        dx_ref[...] = dx
        dxb_ref[...] = dx.astype(BF16)
        first = pl.program_id(0) == 0

        @pl.when(first)
        def _():
            l_ref[...] = part
            dg_ref[...] = dg

        @pl.when(jnp.logical_not(first))
        def _():
            l_ref[...] += part
            dg_ref[...] += dg

    rs = pl.BlockSpec((rows, d), lambda i: (i, 0))
    gs = pl.BlockSpec((1, d), lambda i: (0, 0))
    return pl.pallas_call(
        body, name=name, grid=(t // rows,),
        out_shape=(jax.ShapeDtypeStruct((1, 1), F32), jax.ShapeDtypeStruct((t, d), F32),
                   jax.ShapeDtypeStruct((t, d), BF16), jax.ShapeDtypeStruct((1, d), F32)),
        in_specs=[gs, rs, rs, rs],
        out_specs=(pl.BlockSpec((1, 1), lambda i: (0, 0)), rs, rs, gs),
        compiler_params=_cparams(("arbitrary",)),
    )(g, x, r, tgt)


def adamw(w, gs, m, v, *, name, behind=None):
    nl, r, c = w.shape
    assert len(gs) == nl
    if r % SUBLANES == 0:
        tr, tc = _pick(r, (256, 128, 64, 32, 16, 8)), c
    else:
        tr, tc = r, _pick(c, (256, 128))
    k1 = 1.0 - ADAM_B1 ** ADAM_STEP
    k2 = 1.0 - ADAM_B2 ** ADAM_STEP

    def body(*refs):
        w_ref, m_ref, v_ref = refs[0], refs[1], refs[2]
        g_refs = refs[3:3 + nl]
        go_ref, d_ref, mo_ref, vo_ref = refs[-4:]
        gg = g_refs[0][...]
        for li in range(1, nl):
            gg = jnp.where(pl.program_id(0) == li, g_refs[li][...], gg)
        mn = ADAM_B1 * m_ref[...] + (1.0 - ADAM_B1) * gg
        vn = ADAM_B2 * v_ref[...] + (1.0 - ADAM_B2) * (gg * gg)
        go_ref[...] = gg
        d_ref[...] = -ADAM_LR * ((mn / k1) / (jnp.sqrt(vn / k2) + ADAM_EPS) + ADAM_WD * w_ref[...])
        mo_ref[...] = mn
        vo_ref[...] = vn

    if tc == c:
        sp = pl.BlockSpec((None, tr, c), lambda l, i: (l, i, 0))
        gsp = pl.BlockSpec((tr, c), lambda l, i: (i, 0))
    else:
        sp = pl.BlockSpec((None, r, tc), lambda l, i: (l, 0, i))
        gsp = pl.BlockSpec((r, tc), lambda l, i: (0, i))
    sds = jax.ShapeDtypeStruct((nl, r, c), F32)
    extra = [] if behind is None else [behind]
    return pl.pallas_call(
        body, name=name, grid=(nl, (r // tr) * (c // tc)), out_shape=(sds, sds, sds, sds),
        in_specs=[sp, sp, sp] + [gsp] * nl + [pl.BlockSpec(e.shape, lambda l, i: (0, 0)) for e in extra], out_specs=(sp, sp, sp, sp),
        compiler_params=_cparams(("parallel", "parallel")),
    )(w, m, v, *gs, *extra)


MESH = pl.DeviceIdType.MESH
ANY = pl.BlockSpec(memory_space=pl.ANY)


def _me():
    return lax.axis_index("x"), lax.axis_index("y"), lax.axis_index("c")


def _other_chips(x, y):
    return [(1 - x, y), (x, 1 - y), (1 - x, 1 - y)]


def gather_shards(arrs, *, name):
    n = len(arrs)
    per = 7

    def body(*refs):
        ins, outs = refs[:n], refs[n:2 * n]
        send_sems, recv_sems = refs[2 * n:]
        x, y, c = _me()
        chip = 2 * x + y
        sib = (x, y, 1 - c)
        chips = _other_chips(x, y)

        def rc(i, k, src, dst, to):
            return pltpu.make_async_remote_copy(src_ref=src, dst_ref=dst, send_sem=send_sems.at[per * i + k],
                                                recv_sem=recv_sems.at[per * i + k], device_id=to, device_id_type=MESH)

        def half(i, which):
            hrows = arrs[i].shape[0] // 2
            return pl.ds(which * hrows, hrows)

        own = [rc(i, 6, ins[i], outs[i].at[chip], sib) for i in range(n)]
        for cp in own:
            cp.start()
        started = []
        for i in range(n):
            for j, ch in enumerate(chips):
                cp = rc(i, j, ins[i].at[half(i, c)], outs[i].at[chip, half(i, c)], (ch[0], ch[1], c))
                cp.start()
                started.append(cp)
        for i in range(n):
            for j, ch in enumerate(chips):
                sc = 2 * ch[0] + ch[1]
                landed = outs[i].at[sc, half(i, c)]
                rc(i, j, ins[i].at[half(i, c)], landed, (ch[0], ch[1], c)).wait_recv()
                fw = rc(i, 3 + j, landed, landed, sib)
                fw.start()
                started.append(fw)
        for i in range(n):
            for j, ch in enumerate(chips):
                sc = 2 * ch[0] + ch[1]
                theirs = outs[i].at[sc, half(i, 1 - c)]
                rc(i, 3 + j, theirs, theirs, sib).wait_recv()
        for cp in own:
            cp.wait_recv()
        for cp in started + own:
            cp.wait_send()

    return pl.pallas_call(
        body, name=name,
        out_shape=tuple(jax.ShapeDtypeStruct((N_CHIPS,) + a.shape, a.dtype) for a in arrs),
        in_specs=[ANY] * n, out_specs=tuple([ANY] * n),
        scratch_shapes=[pltpu.SemaphoreType.DMA((per * n,)), pltpu.SemaphoreType.DMA((per * n,))],
        compiler_params=pltpu.CompilerParams(has_side_effects=True),
    )(*arrs)


HBM = pl.BlockSpec(memory_space=pltpu.HBM)
SEM = pl.BlockSpec(memory_space=pltpu.SEMAPHORE)
EFFECT = pltpu.SideEffectType.DATAFLOW_SIDE_EFFECTING
def _hbm(a):
    return pltpu.with_memory_space_constraint(a, pltpu.HBM)


def _half_rows(arr, which):
    hrows = arr.shape[-2] // 2
    return pl.ds(which * hrows, hrows)


def _ici_copies(srcs, lands, sems):
    x, y, c = _me()
    chip = 2 * x + y
    return [pltpu.make_async_remote_copy(src_ref=srcs[i].at[_half_rows(srcs[i], c)], dst_ref=lands[i].at[chip, _half_rows(srcs[i], c)],
                                         send_sem=sems[0].at[3 * i + j], recv_sem=sems[1].at[3 * i + j],
                                         device_id=(ch[0], ch[1], c), device_id_type=MESH)
            for i in range(len(srcs)) for j, ch in enumerate(_other_chips(x, y))]


def _own_copies(srcs, lands, sems):
    x, y, c = _me()
    return [pltpu.make_async_remote_copy(src_ref=srcs[i], dst_ref=lands[i].at[2 * x + y], send_sem=sems[0].at[i],
                                         recv_sem=sems[1].at[i], device_id=(x, y, 1 - c), device_id_type=MESH)
            for i in range(len(srcs))]


def _fwd_copies(srcs, lands, sems, half):
    x, y, c = _me()
    cps = []
    for i in range(len(srcs)):
        for j, ch in enumerate(_other_chips(x, y)):
            part = lands[i].at[2 * ch[0] + ch[1], _half_rows(srcs[i], half)]
            cps.append(pltpu.make_async_remote_copy(src_ref=part, dst_ref=part, send_sem=sems[0].at[3 * i + j],
                                                    recv_sem=sems[1].at[3 * i + j], device_id=(x, y, 1 - c), device_id_type=MESH))
    return cps


def gather_start(groups, *, name):
    arrs = [a for g in groups for a in g]
    n, ng = len(arrs), len(groups)
    bounds = np.cumsum([0] + [len(g) for g in groups])

    def body(*refs):
        srcs, lands = refs[:n], refs[n:2 * n]
        sems = refs[2 * n:2 * n + 4 * ng]
        token = refs[-1]
        for gi in range(ng):
            lo, hi = bounds[gi], bounds[gi + 1]
            for cp in _ici_copies(srcs[lo:hi], lands[lo:hi], sems[4 * gi:4 * gi + 2]):
                cp.start()
        for gi in range(ng):
            lo, hi = bounds[gi], bounds[gi + 1]
            for cp in _own_copies(srcs[lo:hi], lands[lo:hi], sems[4 * gi + 2:4 * gi + 4]):
                cp.start()
        token[...] = jnp.zeros_like(token)

    sem_shapes = []
    for g in groups:
        sem_shapes += [pltpu.SemaphoreType.DMA((3 * len(g),))] * 2 + [pltpu.SemaphoreType.DMA((len(g),))] * 2
    res = pl.pallas_call(
        body, name=name,
        out_shape=tuple(sem_shapes) + tuple(pltpu.HBM(a.shape, a.dtype) for a in arrs)
        + tuple(pltpu.HBM((N_CHIPS,) + a.shape, a.dtype) for a in arrs) + (jax.ShapeDtypeStruct((SUBLANES, LANES), F32),),
        in_specs=[HBM] * (2 * n),
        out_specs=tuple([SEM] * (4 * ng)) + tuple([HBM] * (2 * n)) + (pl.BlockSpec(memory_space=pltpu.VMEM),),
        input_output_aliases={i: 4 * ng + i for i in range(2 * n)},
        compiler_params=pltpu.CompilerParams(has_side_effects=EFFECT),
    )(*[_hbm(a) for a in arrs], *[_hbm(lax.empty((N_CHIPS,) + a.shape, a.dtype)) for a in arrs])
    sems, thru, lands, token = res[:4 * ng], res[4 * ng:4 * ng + n], res[4 * ng + n:4 * ng + 2 * n], res[-1]
    out = [{"ici": (sems[4 * gi], sems[4 * gi + 1]), "own": (sems[4 * gi + 2], sems[4 * gi + 3]),
            "thru": list(thru[bounds[gi]:bounds[gi + 1]]), "lands": list(lands[bounds[gi]:bounds[gi + 1]])} for gi in range(ng)]
    return out, token


def gather_forward(group, after, *, name):
    thru, lands = group["thru"], group["lands"]
    n = len(thru)

    def body(*refs):
        srcs, lands_r = refs[:n], refs[n:2 * n]
        ici = refs[2 * n:2 * n + 2]
        fwd = refs[2 * n + 3 + 2 * n:2 * n + 3 + 2 * n + 2]
        for cp in _ici_copies(srcs, lands_r, ici):
            cp.wait_send()
            cp.wait_recv()
        for cp in _fwd_copies(srcs, lands_r, fwd, _me()[2]):
            cp.start()
        refs[-1][...] = jnp.zeros_like(refs[-1])

    res = pl.pallas_call(
        body, name=name,
        out_shape=tuple(pltpu.HBM(a.shape, a.dtype) for a in thru) + tuple(pltpu.HBM(a.shape, a.dtype) for a in lands)
        + (pltpu.SemaphoreType.DMA((3 * n,)),) * 2 + (jax.ShapeDtypeStruct((SUBLANES, LANES), F32),),
        in_specs=[HBM] * (2 * n) + [SEM, SEM, pl.BlockSpec(memory_space=pl.ANY)],
        out_specs=tuple([HBM] * (2 * n)) + (SEM, SEM, pl.BlockSpec(memory_space=pltpu.VMEM)),
        input_output_aliases={i: i for i in range(2 * n)},
        compiler_params=pltpu.CompilerParams(has_side_effects=EFFECT),
    )(*thru, *lands, *group["ici"], after)
    return {"own": group["own"], "fwd": (res[2 * n], res[2 * n + 1]), "thru": list(res[:n]), "lands": list(res[n:2 * n]),
            "token": res[-1]}


def gather_wait(group, after, *, name):
    thru, lands = group["thru"], group["lands"]
    n = len(thru)

    def body(*refs):
        srcs, lands_r = refs[:n], refs[n:2 * n]
        own, fwd = refs[2 * n:2 * n + 2], refs[2 * n + 2:2 * n + 4]
        c = _me()[2]
        for mine, theirs in zip(_fwd_copies(srcs, lands_r, fwd, c), _fwd_copies(srcs, lands_r, fwd, 1 - c)):
            mine.wait_send()
            theirs.wait_recv()
        for cp in _own_copies(srcs, lands_r, own):
            cp.wait_send()
            cp.wait_recv()

    res = pl.pallas_call(
        body, name=name,
        out_shape=tuple(pltpu.HBM(a.shape, a.dtype) for a in thru) + tuple(pltpu.HBM(a.shape, a.dtype) for a in lands),
        in_specs=[HBM] * (2 * n) + [SEM] * 4 + [pl.BlockSpec(memory_space=pl.ANY)],
        out_specs=tuple([HBM] * (2 * n)),
        input_output_aliases={i: i for i in range(2 * n)},
        compiler_params=pltpu.CompilerParams(has_side_effects=EFFECT),
    )(*thru, *lands, *group["own"], *group["fwd"], after)
    return list(res[n:])


def _pair_copies(srcs, lands, sems):
    x, y, c = _me()
    return [pltpu.make_async_remote_copy(src_ref=srcs[i].at[:, _half_rows(srcs[i], 1 - c), :], dst_ref=lands[i], send_sem=sems[0].at[i],
                                         recv_sem=sems[1].at[i], device_id=(x, y, 1 - c), device_id_type=MESH)
            for i in range(len(srcs))]


def _scatter_copies(srcs, lands, sems):
    x, y, c = _me()
    return [pltpu.make_async_remote_copy(src_ref=srcs[i].at[2 * ch[0] + ch[1]], dst_ref=lands[i].at[j], send_sem=sems[0].at[3 * i + j],
                                         recv_sem=sems[1].at[3 * i + j], device_id=(ch[0], ch[1], c), device_id_type=MESH)
            for i in range(len(srcs)) for j, ch in enumerate(_other_chips(x, y))]


def split_start(copies, arrs, land_shapes, nsem, *, name):
    n = len(arrs)

    def body(*refs):
        for cp in copies(refs[:n], refs[n:2 * n], refs[2 * n:2 * n + 2]):
            cp.start()
        refs[-1][...] = jnp.zeros_like(refs[-1])

    res = pl.pallas_call(
        body, name=name,
        out_shape=(pltpu.SemaphoreType.DMA((nsem,)),) * 2 + tuple(pltpu.HBM(a.shape, a.dtype) for a in arrs)
        + tuple(pltpu.HBM(s, a.dtype) for s, a in zip(land_shapes, arrs)) + (jax.ShapeDtypeStruct((SUBLANES, LANES), F32),),
        in_specs=[HBM] * (2 * n), out_specs=(SEM, SEM) + tuple([HBM] * (2 * n)) + (pl.BlockSpec(memory_space=pltpu.VMEM),),
        input_output_aliases={i: 2 + i for i in range(2 * n)},
        compiler_params=pltpu.CompilerParams(has_side_effects=EFFECT),
    )(*[_hbm(a) for a in arrs], *[_hbm(lax.empty(s, a.dtype)) for s, a in zip(land_shapes, arrs)])
    return {"copies": copies, "sems": (res[0], res[1]), "thru": list(res[2:2 + n]), "lands": list(res[2 + n:2 + 2 * n]),
            "token": res[-1]}


def split_wait(state, after, *, name):
    thru, lands, copies = state["thru"], state["lands"], state["copies"]
    n = len(thru)

    def body(*refs):
        for cp in copies(refs[:n], refs[n:2 * n], refs[2 * n:2 * n + 2]):
            cp.wait_send()
            cp.wait_recv()

    res = pl.pallas_call(
        body, name=name,
        out_shape=tuple(pltpu.HBM(a.shape, a.dtype) for a in thru) + tuple(pltpu.HBM(a.shape, a.dtype) for a in lands),
        in_specs=[HBM] * (2 * n) + [SEM, SEM, pl.BlockSpec(memory_space=pl.ANY)], out_specs=tuple([HBM] * (2 * n)),
        input_output_aliases={i: i for i in range(2 * n)},
        compiler_params=pltpu.CompilerParams(has_side_effects=EFFECT),
    )(*thru, *lands, *state["sems"], after)
    return list(res[:n]), list(res[n:])


def sibling_join_halves(arrs, *, name):
    n = len(arrs)

    def body(*refs):
        outs = refs[n:2 * n]
        send_sems, recv_sems = refs[2 * n:]
        x, y, c = _me()
        cps = []
        for i in range(n):
            hrows = arrs[i].shape[0] // 2
            mine = outs[i].at[pl.ds(c * hrows, hrows)]
            cps.append(pltpu.make_async_remote_copy(src_ref=mine, dst_ref=mine, send_sem=send_sems.at[i],
                                                    recv_sem=recv_sems.at[i], device_id=(x, y, 1 - c), device_id_type=MESH))
        for cp in cps:
            cp.start()
        for i in range(n):
            hrows = arrs[i].shape[0] // 2
            theirs = outs[i].at[pl.ds((1 - c) * hrows, hrows)]
            pltpu.make_async_remote_copy(src_ref=theirs, dst_ref=theirs, send_sem=send_sems.at[i], recv_sem=recv_sems.at[i],
                                         device_id=(x, y, 1 - c), device_id_type=MESH).wait_recv()
        for cp in cps:
            cp.wait_send()

    return pl.pallas_call(
        body, name=name,
        out_shape=tuple(jax.ShapeDtypeStruct(a.shape, a.dtype) for a in arrs),
        in_specs=[ANY] * n, out_specs=tuple([ANY] * n),
        input_output_aliases={i: i for i in range(n)},
        scratch_shapes=[pltpu.SemaphoreType.DMA((n,)), pltpu.SemaphoreType.DMA((n,))],
        compiler_params=pltpu.CompilerParams(has_side_effects=True),
    )(*arrs)


def pre_reduce(gd, sib, who, *, name):
    _, r, c = gd.shape
    h = r // 2
    tr = _pick(h, (256, 128, 64, 32, 16, 8))
    nrb = h // tr

    def body(who_ref, gd_ref, sib_ref, pb_ref, own_ref):
        p = gd_ref[...] + sib_ref[...]
        pb_ref[...] = p.astype(BF16)

        @pl.when(pl.program_id(1) == who_ref[1])
        def _():
            own_ref[...] = p

    return pl.pallas_call(
        body, name=name,
        out_shape=(jax.ShapeDtypeStruct((N_CHIPS, h, c), BF16), jax.ShapeDtypeStruct((h, c), F32)),
        grid_spec=pltpu.PrefetchScalarGridSpec(
            num_scalar_prefetch=1, grid=(nrb, N_CHIPS),
            in_specs=[pl.BlockSpec((None, tr, c), lambda i, s, w: (s, w[0] * nrb + i, 0)),
                      pl.BlockSpec((None, tr, c), lambda i, s, w: (s, i, 0))],
            out_specs=(pl.BlockSpec((None, tr, c), lambda i, s, w: (s, i, 0)),
                       pl.BlockSpec((tr, c), lambda i, s, w: (i, 0)))),
        compiler_params=_cparams(("parallel", "arbitrary")),
    )(who, gd, sib)


def final_reduce(own, rcv, who, *, name):
    h, c = own.shape
    tr = _pick(h, (256, 128, 64, 32, 16, 8))
    nrb = h // tr

    def body(who_ref, own_ref, rcv_ref, o_ref):
        del who_ref
        acc = own_ref[...]
        for j in range(3):
            acc = acc + rcv_ref[j].astype(F32)
        o_ref[...] = acc

    return pl.pallas_call(
        body, name=name, out_shape=jax.ShapeDtypeStruct((2 * h, c), F32),
        grid_spec=pltpu.PrefetchScalarGridSpec(
            num_scalar_prefetch=1, grid=(nrb,),
            in_specs=[pl.BlockSpec((tr, c), lambda i, w: (i, 0)), pl.BlockSpec((3, tr, c), lambda i, w: (0, i, 0))],
            out_specs=pl.BlockSpec((tr, c), lambda i, w: (w[0] * nrb + i, 0))),
        compiler_params=_cparams(("parallel",)),
    )(who, own, rcv)


def sum8(landed, own, me, *, name):
    _, r, c = landed.shape
    tr = _pick(r, (256, 128, 64, 32, 16, 8))

    def body(me_ref, a_ref, own_ref, o_ref):
        acc = None
        for j in range(8):
            term = jnp.where(me_ref[0] == j, own_ref[...], a_ref[j])
            acc = term if acc is None else acc + term
        o_ref[...] = acc

    return pl.pallas_call(
        body, name=name, out_shape=jax.ShapeDtypeStruct((r, c), F32),
        grid_spec=pltpu.PrefetchScalarGridSpec(
            num_scalar_prefetch=1, grid=(r // tr,),
            in_specs=[pl.BlockSpec((8, tr, c), lambda i, w: (0, i, 0)), pl.BlockSpec((tr, c), lambda i, w: (i, 0))],
            out_specs=pl.BlockSpec((tr, c), lambda i, w: (i, 0))),
        compiler_params=_cparams(("parallel",)),
    )(me, landed, own)


def _everyone_copies(srcs, lands, sems):
    x, y, c = _me()
    me = 4 * x + 2 * y + c
    cps = []
    for k in range(1, 8):
        fx, fy, fc = (k >> 2) & 1, (k >> 1) & 1, k & 1
        to = (x + fx - 2 * x * fx, y + fy - 2 * y * fy, c + fc - 2 * c * fc)
        cps.append(pltpu.make_async_remote_copy(src_ref=srcs[0], dst_ref=lands[0].at[me], send_sem=sems[0].at[k - 1],
                                                recv_sem=sems[1].at[k - 1], device_id=to, device_id_type=MESH))
    return cps


def _behind(p, *tokens):
    for tk in tokens:
        if tk is not None:
            p = p + tk[0, 0]
    return p


def local_step(x, tgt, wb, ws, arrive=lambda layer, event, after: {}, reduce=lambda group, event, gb, after: None):
    t, d = x.shape
    nh = ws["alog"].shape[0]
    ng = ws["w_s"].shape[0]
    assert nh == ng
    mix_w = (nh + ng) * HEAD_DIM

    (h0,) = rows_fwd(stage_norm, [ws["e_norm"]], [x], [(1, BF16, d)], name="f_norm_e")
    pm = mm_nt(h0, wb["w_main_t"], name="f_proj_main")
    pba = mm_nt(h0, wb["w_ba_t"], name="f_proj_ba")
    qkv = conv_fwd(pm, ws["conv_w"], mode="silu", name="f_conv4")
    tok = arrive("layer0", "landed", qkv).get("token")
    cat, states, tinvs = delta_fwd(qkv, pm, pba, ws["alog"], ws["dtb"], _behind(ws["onorm"], tok), mix_w, name="f_delta")
    cat = gmlp_fwd(pm, ws["lng"], ws["lnb"], ws["w_s"], ws["bs"], cat, name="f_gmlp")
    wb = {**wb, **arrive("layer0", "joined", cat)}
    y0 = mm_nn(cat, wb["w_out"], name="f_out")
    x1, h1 = rows_fwd(stage_res_norm, [ws["f_norm0"]], [x, y0], [(0, F32, d), (1, BF16, d)], name="f_norm_f0")
    a1, s1 = mm_nn(h1, wb["w1_0"], name="f_mlp0_up", epilogue="relu2")
    tok = arrive("layer1", "landed", s1).get("token")
    y1 = mm_nn(s1, wb["w2_0"], name="f_mlp0_down", behind=tok)
    x2, h2 = rows_fwd(stage_res_norm, [ws["o_norm"]], [x1, y1], [(0, F32, d), (1, BF16, d)], name="f_norm_o")
    wb = {**wb, **arrive("layer1", "joined", h2)}
    zz = mm_nn(h2, wb["pw1"], name="f_pw1")
    zparts = [(zz, d, 0), (zz, d, 1)]
    (gl,) = rows_fwd(stage_glu, [ws["b1a"], ws["b1b"]], zparts, [(0, F32, d)], name="f_glu")
    cv = conv_fwd(gl, ws["dw"], mode="plain", name="f_conv31")
    ln_params = [ws["dw_b"], ws["ln_g"], ws["ln_b"]]
    (sl,) = rows_fwd(stage_ln_silu, ln_params, [cv], [(0, BF16, d)], name="f_ln_silu")
    y2 = mm_nn(sl, wb["pw2"], name="f_pw2")
    x3, h3 = rows_fwd(stage_res_bias_norm, [ws["b2"], ws["f_norm1"]], [x2, y2], [(0, F32, d), (1, BF16, d)], name="f_norm_f1")
    a3, s3 = mm_nn(h3, wb["w1_1"], name="f_mlp1_up", epilogue="relu2")
    y3 = mm_nn(s3, wb["w2_1"], name="f_mlp1_down")
    loss, d4, d4b, g_final = loss_head(ws["final_norm"], x3, y3, tgt, name="loss_head")

    gb, gs = {}, {"final_norm": g_final}
    s_up = wb["w1_0"].shape[0]

    gb["w2_1"] = mm_tn(s3, d4b, groups=1, name="b_mlp1_down_w")
    dpre3 = mm_nt(d4b, wb["w2_1"], name="b_mlp1_down_x", mul=a3, out_dtype=BF16)
    gb["w1_1"] = mm_tn(h3, dpre3, groups=s_up, name="b_mlp1_up_w")
    tok = reduce("mlp1", "ready", gb, None)
    dh3 = mm_nt(dpre3, wb["w1_1"], name="b_mlp1_up_x", out_dtype=BF16)
    d3, d3b, gs["b2"], gs["f_norm1"] = rows_bwd(stage_res_bias_norm, [ws["b2"], _behind(ws["f_norm1"], tok)], [x2, y2], [d4, dh3],
                                                [(0, F32), (1, BF16)], name="b_norm_f1")
    tok = reduce("mlp1", "paired", gb, d3)
    gb["pw2"] = mm_tn(sl, d3b, groups=1, name="b_pw2_w")
    dsl = mm_nt(d3b, wb["pw2"], name="b_pw2_x", out_dtype=BF16)
    dcv, gs["dw_b"], gs["ln_g"], gs["ln_b"] = rows_bwd(stage_ln_silu, [_behind(ln_params[0], tok)] + ln_params[1:], [cv], [dsl],
                                                       [(0, F32)], name="b_ln_silu")
    dgl, gs["dw"] = conv_bwd(gl, ws["dw"], dcv, name="b_conv31")
    dzz, gs["b1a"], gs["b1b"] = rows_bwd(stage_glu, [ws["b1a"], ws["b1b"]], zparts, [dgl], [(0, BF16), (1, BF16)],
                                         name="b_glu", concat=True)
    gb["pw1"] = mm_tn(h2, dzz, groups=wb["pw1"].shape[0], name="b_pw1_w")
    tok = reduce("conv", "ready", gb, None)
    dh2 = mm_nt(dzz, wb["pw1"], name="b_pw1_x", out_dtype=BF16)
    d2, d2b, gs["o_norm"] = rows_bwd(stage_res_norm, [_behind(ws["o_norm"], tok)], [x1, y1], [d3, dh2], [(0, F32), (1, BF16)],
                                     name="b_norm_o")
    tok_conv = reduce("conv", "paired", gb, d2)
    gb["w2_0"] = mm_tn(s1, d2b, groups=1, name="b_mlp0_down_w")
    dpre1 = mm_nt(d2b, wb["w2_0"], name="b_mlp0_down_x", mul=a1, out_dtype=BF16)
    gb["w1_0"] = mm_tn(h1, dpre1, groups=s_up, name="b_mlp0_up_w")
    tok = reduce("mlp0", "ready", gb, None)
    dh1 = mm_nt(dpre1, wb["w1_0"], name="b_mlp0_up_x", out_dtype=BF16)
    d1, d1b, gs["f_norm0"] = rows_bwd(stage_res_norm, [_behind(ws["f_norm0"], tok_conv, tok)], [x, y0], [d2, dh1],
                                      [(0, F32), (1, BF16)], name="b_norm_f0")
    tok_mlp0 = reduce("mlp0", "paired", gb, d1)
    gb["w_out"] = mm_tn(cat, d1b, groups=1, name="b_out_w")
    tok = reduce("mixer", "ready", gb, None)
    dcat = mm_nt(d1b, wb["w_out"], name="b_out_x")
    dqkv_c, dpm, dpba, gs["alog"], gs["dtb"], gs["onorm"] = delta_bwd(
        qkv, pm, pba, ws["alog"], ws["dtb"], _behind(ws["onorm"], tok_mlp0, tok), states, tinvs, dcat, name="b_delta")
    tok = reduce("mixer", "paired", gb, dqkv_c)
    dpm, gs["lng"], gs["lnb"], gs["w_s"], gs["bs"] = gmlp_bwd(
        pm, ws["lng"], ws["lnb"], ws["w_s"], ws["bs"], dcat, dpm, name="b_gmlp")
    conv_w = _behind(ws["conv_w"], tok)
    dconv = conv_fwd(pm, conv_w, mode="silu_bwd", g=dqkv_c, name="b_conv4_silu")
    dpm, gs["conv_w"] = conv_bwd(pm, conv_w, dconv, name="b_conv4", into=dpm)
    gb["w_main_t"] = mm_tn(dpm, h0, groups=1, name="b_proj_main_w")
    gb["w_ba_t"] = mm_tn(dpba, h0, groups=1, name="b_proj_ba_w")
    dh0 = mm_nn(dpm, wb["w_main_t"], name="b_proj_main_x")
    dh0 = mm_nn(dpba, wb["w_ba_t"], name="b_proj_ba_x", add=dh0, out_dtype=BF16)
    grad_x, gs["e_norm"] = rows_bwd(stage_norm, [ws["e_norm"]], [x], [d1, dh0], [(0, F32)], name="b_norm_e")
    return loss, grad_x, gb, gs


WEIGHTS = ["e_norm", "e_w_in", "e_conv_w", "e_a_log", "e_dt_bias", "e_o_norm", "e_ln_g", "e_ln_b", "e_w_s", "e_b_s", "e_w_out",
           "o_norm", "o_pw1", "o_pw1_b", "o_dw", "o_dw_b", "o_ln_g", "o_ln_b", "o_pw2", "o_pw2_b", "f_norm", "f_w1", "f_w2",
           "final_norm"]
BIG = ["e_w_in", "e_w_out", "o_pw1", "o_pw2", "f_w1", "f_w2"]
SMALL_SHARDED = ["e_conv_w", "o_norm", "o_pw1_b", "o_dw", "o_dw_b", "o_ln_g", "o_ln_b", "o_pw2_b"]
SMALL = [n for n in WEIGHTS if n not in BIG]
LANES = 128
PACK_ROWS = 16
IN_ROW_MULT = 256
REDUCE_GROUPS = {"mlp1": ("w2_1", "w1_1"), "conv": ("pw2", "pw1"), "mlp0": ("w2_0", "w1_0"), "mixer": ("w_out",)}


def _pack(arrs):
    flat = jnp.concatenate([a.reshape(-1).astype(F32) for a in arrs])
    n = flat.shape[0]
    rows = -(-n // (LANES * PACK_ROWS)) * PACK_ROWS
    return jnp.pad(flat, (0, rows * LANES - n)).reshape(rows, LANES)


def _unpack(flat, shapes, lead=()):
    outs, off = [], 0
    for s in shapes:
        n = int(np.prod(s))
        outs.append(flat[..., off:off + n].reshape(lead + tuple(s)))
        off += n
    return outs


def kernel(x, e_norm, e_w_in, e_conv_w, e_a_log, e_dt_bias, e_o_norm, e_ln_g, e_ln_b, e_w_s, e_b_s, e_w_out, o_norm, o_pw1, o_pw1_b, o_dw, o_dw_b, o_ln_g, o_ln_b, o_pw2, o_pw2_b, f_norm, f_w1, f_w2, final_norm, loss_target, m_e_norm, m_e_w_in, m_e_conv_w, m_e_a_log, m_e_dt_bias, m_e_o_norm, m_e_ln_g, m_e_ln_b, m_e_w_s, m_e_b_s, m_e_w_out, m_o_norm, m_o_pw1, m_o_pw1_b, m_o_dw, m_o_dw_b, m_o_ln_g, m_o_ln_b, m_o_pw2, m_o_pw2_b, m_f_norm, m_f_w1, m_f_w2, m_final_norm, v_e_norm, v_e_w_in, v_e_conv_w, v_e_a_log, v_e_dt_bias, v_e_o_norm, v_e_ln_g, v_e_ln_b, v_e_w_s, v_e_b_s, v_e_w_out, v_o_norm, v_o_pw1, v_o_pw1_b, v_o_dw, v_o_dw_b, v_o_ln_g, v_o_ln_b, v_o_pw2, v_o_pw2_b, v_f_norm, v_f_w1, v_f_w2, v_final_norm):
    a = dict(locals())
    xi, yi, ci = _me()
    chip = 2 * xi + yi
    who = jnp.stack([ci, chip]).astype(jnp.int32)
    t, d = x.shape[1], x.shape[2]
    nh, ng = e_a_log.shape[1], e_w_s.shape[1]
    n_qkv, n_av, n_bw = 3 * nh * HEAD_DIM, nh * HEAD_DIM, ng * HEAD_DIM
    in_cols = n_qkv + n_av + 2 * nh + 2 * n_bw
    c_ba = n_qkv + n_av

    sh_in = in_cols // N_CHIPS
    pad_in = -(-sh_in // IN_ROW_MULT) * IN_ROW_MULT - sh_in
    w_in_t_local = jnp.pad(e_w_in[0].T.astype(BF16), ((0, pad_in), (0, 0)))
    small_local = [a[n] for n in SMALL_SHARDED]
    g_in, g_small = gather_shards([w_in_t_local, _pack(small_local)], name="gather_weights_in")
    travelling, token = gather_start([[w.astype(BF16) for w in (e_w_out[0], f_w1[0], f_w2[0])],
                                      [w.astype(BF16) for w in (o_pw1[0], o_pw2[0], f_w1[1], f_w2[1])]], name="gather_start")

    state = {"layer0": travelling[0], "layer1": travelling[1]}

    def arrive(layer, event, after):
        if event == "landed":
            state[layer] = gather_forward(state[layer], after, name="gather_forward_" + layer)
            return {"token": state[layer]["token"]}
        got = gather_wait(state[layer], after, name="gather_wait_" + layer)
        if layer == "layer0":
            g_out, g_w1, g_w2 = got
            return {"w_out": g_out.reshape(1, -1, d), "w1_0": g_w1, "w2_0": g_w2.reshape(1, -1, d)}
        g_pw1, g_pw2, g_w1, g_w2 = got
        return {"pw1": g_pw1, "pw2": g_pw2.reshape(1, -1, d), "w1_1": g_w1, "w2_1": g_w2.reshape(1, -1, d)}

    def in_rows(lo, hi):
        parts = []
        for s in range(N_CHIPS):
            a0, a1 = max(lo, s * sh_in), min(hi, (s + 1) * sh_in)
            if a0 < a1:
                parts.append(g_in[s, a0 - s * sh_in:a1 - s * sh_in])
        return parts

    wb = {
        "w_main_t": jnp.concatenate(in_rows(0, c_ba) + in_rows(c_ba + 2 * nh, in_cols), axis=0)[None],
        "w_ba_t": jnp.pad(jnp.concatenate(in_rows(c_ba, c_ba + 2 * nh), axis=0), ((0, LANES - 2 * nh), (0, 0)))[None],
    }
    pieces = _unpack(g_small.reshape(N_CHIPS, -1), [w.shape for w in small_local], lead=(N_CHIPS,))
    full = {n: jnp.moveaxis(p, 0, -2).reshape(p.shape[1:-1] + (N_CHIPS * p.shape[-1],)) for n, p in zip(SMALL_SHARDED, pieces)}
    ws = {
        "e_norm": e_norm + token[0, 0],
        "conv_w": full["e_conv_w"][0], "alog": e_a_log.reshape(nh, 1, 1), "dtb": e_dt_bias.reshape(nh, 1, 1),
        "onorm": e_o_norm, "lng": e_ln_g.reshape(ng, 1, HEAD_DIM), "lnb": e_ln_b.reshape(ng, 1, HEAD_DIM), "w_s": e_w_s[0],
        "bs": e_b_s.reshape(ng, HEAD_DIM, 1), "o_norm": full["o_norm"], "b1a": full["o_pw1_b"][:, :d], "b1b": full["o_pw1_b"][:, d:],
        "dw": full["o_dw"][0], "dw_b": full["o_dw_b"], "ln_g": full["o_ln_g"], "ln_b": full["o_ln_b"], "b2": full["o_pw2_b"],
        "f_norm0": f_norm[0:1], "f_norm1": f_norm[1:2], "final_norm": final_norm.reshape(1, d),
    }

    pending = {}

    def reduce(group, event, gb, after):
        names = REDUCE_GROUPS[group]
        if event == "ready":
            arrs = [gb[nme] if gb[nme].shape[0] == N_CHIPS else gb[nme].reshape(N_CHIPS, -1, d) for nme in names]
            pending[group] = split_start(_pair_copies, arrs, [(N_CHIPS, g.shape[1] // 2, g.shape[2]) for g in arrs], len(arrs),
                                         name="reduce_sibling_start_" + group)
            return pending[group]["token"]
        mine, theirs = split_wait(pending[group], after, name="reduce_sibling_wait_" + group)
        pairs = [pre_reduce(gd, sb, who, name="reduce_pair_" + nme) for nme, gd, sb in zip(names, mine, theirs)]
        sums = [p[0] for p in pairs]
        pending[group] = {"own": [p[1] for p in pairs],
                          "scatter": split_start(_scatter_copies, sums, [(3,) + s.shape[1:] for s in sums], 3 * len(sums),
                                                 name="reduce_chips_start_" + group)}
        return pending[group]["scatter"]["token"]

    loss, grad_x, gb, gs = local_step(x[0], loss_target[0], wb, ws, arrive, reduce)

    gm, gba = gb["w_main_t"][0], gb["w_ba_t"][0]
    g_in_t = jnp.concatenate([gm[:c_ba], gba[:2 * nh], gm[c_ba:]], axis=0).reshape(N_CHIPS, sh_in, d)
    g_in_padded = jnp.pad(g_in_t, ((0, 0), (0, pad_in), (0, 0)))
    in_pair = split_start(_pair_copies, [g_in_padded], [(N_CHIPS, (sh_in + pad_in) // 2, d)], 1, name="reduce_sibling_start_w_in")
    small_global = {
        "e_norm": gs["e_norm"], "e_conv_w": gs["conv_w"][None], "e_a_log": gs["alog"].reshape(1, nh), "e_dt_bias": gs["dtb"].reshape(1, nh),
        "e_o_norm": gs["onorm"], "e_ln_g": gs["lng"].reshape(1, n_bw), "e_ln_b": gs["lnb"].reshape(1, n_bw), "e_w_s": gs["w_s"][None],
        "e_b_s": gs["bs"].reshape(1, ng, HEAD_DIM), "o_norm": gs["o_norm"], "o_pw1_b": jnp.concatenate([gs["b1a"], gs["b1b"]], axis=1),
        "o_dw": gs["dw"][None], "o_dw_b": gs["dw_b"], "o_ln_g": gs["ln_g"], "o_ln_b": gs["ln_b"], "o_pw2_b": gs["b2"],
        "f_norm": jnp.concatenate([gs["f_norm0"], gs["f_norm1"]], axis=0), "final_norm": gs["final_norm"].reshape(d),
    }
    packed = _pack([small_global[n] for n in SMALL] + [loss])
    everyone = split_start(_everyone_copies, [packed], [(8,) + packed.shape], 7, name="reduce_small_start")

    who_then = who + (in_pair["token"][0, 0] + everyone["token"][0, 0]).astype(jnp.int32)
    half = {}
    for group, names in REDUCE_GROUPS.items():
        _, landed = split_wait(pending[group]["scatter"], grad_x, name="reduce_chips_wait_" + group)
        for nme, ow, rc in zip(names, pending[group]["own"], landed):
            half[nme] = final_reduce(ow, rc, who_then, name="reduce_sum_" + nme)
    r_out, r_pw1, r_pw2, r_w1_0, r_w1_1, r_w2_0, r_w2_1 = sibling_join_halves(
        [half[nme] for nme in ("w_out", "pw1", "pw2", "w1_0", "w1_1", "w2_0", "w2_1")], name="reduce_join")
    big_grads = {"e_w_out": [r_out], "o_pw1": [r_pw1], "o_pw2": [r_pw2], "f_w1": [r_w1_0, r_w1_1], "f_w2": [r_w2_0, r_w2_1]}

    mine, theirs = split_wait(in_pair, r_out, name="reduce_sibling_wait_w_in")
    in_sum, in_own = pre_reduce(mine[0], theirs[0], who, name="reduce_pair_w_in")
    in_scatter = split_start(_scatter_copies, [in_sum], [(3,) + in_sum.shape[1:]], 3, name="reduce_chips_start_w_in")
    out = {}
    prev = in_scatter["token"]
    for n in BIG:
        if n != "e_w_in":
            out[n] = adamw(a[n], big_grads[n], a["m_" + n], a["v_" + n], name="adamw_" + n, behind=prev)
            prev = out[n][1][0, :SUBLANES, :LANES]

    own_packed, landed = split_wait(everyone, prev, name="reduce_small_wait")
    summed = sum8(landed[0], own_packed[0], (4 * xi + 2 * yi + ci).astype(jnp.int32).reshape(1), name="reduce_small_sum")
    *small_full, total = _unpack(summed.reshape(-1), [small_global[n].shape for n in SMALL] + [()])
    small_grads = {}
    for n, g in zip(SMALL, small_full):
        if n in SMALL_SHARDED:
            width = a[n].shape[-1]
            g = lax.dynamic_slice_in_dim(g, chip * width, width, axis=g.ndim - 1)
        small_grads[n] = g
    sw, sm, sv, sg = (_pack([src[n] for n in SMALL])[None] for src in
                      ({n: a[n] for n in SMALL}, {n: a["m_" + n] for n in SMALL}, {n: a["v_" + n] for n in SMALL}, small_grads))
    res = adamw(sw, [sg[0]], sm, sv, name="adamw_small")

    _, landed = split_wait(in_scatter, res[1], name="reduce_chips_wait_w_in")
    (r_in,) = sibling_join_halves([final_reduce(in_own, landed[0], who, name="reduce_sum_w_in")], name="reduce_join_w_in")
    w_in_res = adamw(e_w_in.transpose(0, 2, 1), [r_in[:sh_in]], m_e_w_in.transpose(0, 2, 1), v_e_w_in.transpose(0, 2, 1),
                     name="adamw_e_w_in")
    out["e_w_in"] = tuple(r.transpose(0, 2, 1) for r in w_in_res)
    shapes = [a[n].shape for n in SMALL]
    unpacked = [_unpack(r.reshape(-1), shapes) for r in res]
    for i, n in enumerate(SMALL):
        out[n] = tuple(u[i] for u in unpacked)

    result = [total, grad_x[None]]
    for k in range(4):
        result += [out[n][k] for n in WEIGHTS]
    return tuple(result)
```

```python
import functools
import math

import jax
import jax.numpy as jnp
import numpy as np
from jax import lax
from jax.experimental import pallas as pl
from jax.experimental.pallas import tpu as pltpu

F32 = jnp.float32
BF16 = jnp.bfloat16

EPS = 1e-6
CHUNK = 64
PAIR = 2 * CHUNK
HEAD_DIM = 128
N_CHIPS = 4
ADAM_LR, ADAM_B1, ADAM_B2, ADAM_EPS, ADAM_WD, ADAM_STEP = 0.001, 0.9, 0.999, 1e-08, 0.01, 10

VMEM_LIMIT = 56 * 1024 * 1024


def _cparams(sem=None):
    return pltpu.CompilerParams(dimension_semantics=sem, vmem_limit_bytes=VMEM_LIMIT)


def _pick(n, prefs):
    for p in prefs:
        if n % p == 0:
            return p
    return n


def mm_nn(a, w, *, name, epilogue=None, add=None, out_dtype=F32, behind=None):
    m, k = a.shape
    s, _, ns = w.shape
    n = s * ns
    tm = _pick(m, (1024, 512, 256, 128))
    tn = _pick(ns, (1024, 512, 256, 128))
    tk = _pick(k, (2048, 1024, 512, 256, 128))
    nk = k // tk
    npb = ns // tn
    assert add is None or epilogue is None

    def body(a_ref, w_ref, *rest):
        if add is not None:
            add_ref, rest = rest[0], rest[1:]
        if behind is not None:
            rest = rest[1:]
        if epilogue == "relu2":
            o1_ref, o2_ref = rest[0], rest[1]
            acc_ref = rest[2] if nk > 1 else None
        else:
            o1_ref = rest[0]
            acc_ref = rest[1] if nk > 1 else None
        def finish(c, rows=slice(None)):
            if epilogue == "relu2":
                r = jnp.maximum(c, 0.0)
                o1_ref[rows, :] = r.astype(o1_ref.dtype)
                o2_ref[rows, :] = (r * r).astype(o2_ref.dtype)
            elif add is not None:
                o1_ref[rows, :] = (c + add_ref[rows, :].astype(F32)).astype(o1_ref.dtype)
            else:
                o1_ref[rows, :] = c.astype(o1_ref.dtype)

        part = jnp.dot(a_ref[...], w_ref[...], preferred_element_type=F32)
        if nk == 1:
            finish(part)
        else:
            kk = pl.program_id(2)

            @pl.when(kk == 0)
            def _():
                acc_ref[...] = part

            @pl.when(kk > 0)
            def _():
                acc_ref[...] += part

            @pl.when(kk == nk - 1)
            def _():
                finish(acc_ref[...])

    o_spec = pl.BlockSpec((tm, tn), lambda i, j, kk: (i, j))
    if epilogue == "relu2":
        out_shape = (jax.ShapeDtypeStruct((m, n), BF16), jax.ShapeDtypeStruct((m, n), BF16))
        out_specs = (o_spec, o_spec)
    else:
        out_shape = jax.ShapeDtypeStruct((m, n), out_dtype)
        out_specs = o_spec
    return pl.pallas_call(
        body, name=name, out_shape=out_shape,
        grid=(m // tm, n // tn, nk),
        in_specs=[pl.BlockSpec((tm, tk), lambda i, j, kk: (i, kk)),
                  pl.BlockSpec((None, tk, tn), lambda i, j, kk: (j // npb, kk, j % npb))] + ([o_spec] if add is not None else [])
        + ([pl.BlockSpec(behind.shape, lambda i, j, kk: (0, 0))] if behind is not None else []),
        out_specs=out_specs,
        scratch_shapes=[pltpu.VMEM((tm, tn), F32)] if nk > 1 else [],
        compiler_params=_cparams(("parallel", "parallel", "arbitrary")),
    )(*([a, w] + ([add] if add is not None else []) + ([behind] if behind is not None else [])))


def mm_nt(a, w, *, name, mul=None, add=None, out_dtype=F32):
    assert mul is None or add is None
    if add is not None:
        mul = add
    m, n = a.shape
    s, k, ns = w.shape
    assert n == s * ns
    tm = _pick(m, (1024, 512, 256, 128))
    tko = _pick(k, (1024, 512, 256, 128))
    tn = _pick(ns, (2048, 1024, 512, 256, 128))
    nn = n // tn
    npb = ns // tn

    def body(a_ref, w_ref, *rest):
        if mul is not None:
            m_ref, o_ref = rest[0], rest[1]
            acc_ref = rest[2] if nn > 1 else None
        else:
            m_ref, o_ref = None, rest[0]
            acc_ref = rest[1] if nn > 1 else None
        part = lax.dot_general(a_ref[...], w_ref[...], (((1,), (1,)), ((), ())), preferred_element_type=F32)

        def finish(c):
            if add is not None:
                c = c + m_ref[...].astype(F32)
            elif m_ref is not None:
                c = c * (2.0 * m_ref[...].astype(F32))
            o_ref[...] = c.astype(o_ref.dtype)

        if nn == 1:
            finish(part)
        else:
            kk = pl.program_id(2)

            @pl.when(kk == 0)
            def _():
                acc_ref[...] = part

            @pl.when(kk > 0)
            def _():
                acc_ref[...] += part

            @pl.when(kk == nn - 1)
            def _():
                finish(acc_ref[...])

    in_specs = [pl.BlockSpec((tm, tn), lambda i, j, kk: (i, kk)),
                pl.BlockSpec((None, tko, tn), lambda i, j, kk: (kk // npb, j, kk % npb))]
    args = [a, w]
    if mul is not None:
        in_specs.append(pl.BlockSpec((tm, tko), lambda i, j, kk: (i, j)))
        args.append(mul)
    return pl.pallas_call(
        body, name=name, out_shape=jax.ShapeDtypeStruct((m, k), out_dtype),
        grid=(m // tm, k // tko, nn),
        in_specs=in_specs,
        out_specs=pl.BlockSpec((tm, tko), lambda i, j, kk: (i, j)),
        scratch_shapes=[pltpu.VMEM((tm, tko), F32)] if nn > 1 else [],
        compiler_params=_cparams(("parallel", "parallel", "arbitrary")),
    )(*args)


def mm_tn(a, b, *, groups, name):
    m, k = a.shape
    _, n = b.shape
    ns = n // groups
    tko = _pick(k, (1024, 512, 256, 128))
    tn = _pick(ns, (1024, 512, 256, 128))
    tc = _pick(m, (2048, 1024, 512, 256, 128))
    nc = m // tc
    npb = ns // tn

    def body(a_ref, b_ref, o_ref):
        part = lax.dot_general(a_ref[...], b_ref[...], (((0,), (0,)), ((), ())), preferred_element_type=F32)
        kk = pl.program_id(2)

        @pl.when(kk == 0)
        def _():
            o_ref[...] = part

        @pl.when(kk > 0)
        def _():
            o_ref[...] += part

    return pl.pallas_call(
        body, name=name, out_shape=jax.ShapeDtypeStruct((groups, k, ns), F32),
        grid=(k // tko, n // tn, nc),
        in_specs=[pl.BlockSpec((tc, tko), lambda i, j, kk: (kk, i)),
                  pl.BlockSpec((tc, tn), lambda i, j, kk: (kk, j))],
        out_specs=pl.BlockSpec((None, tko, tn), lambda i, j, kk: (j // npb, i, j % npb)),
        compiler_params=_cparams(("parallel", "parallel", "arbitrary")),
    )(a, b)


ROWS = 128


def _full_spec(arr):
    nd = arr.ndim
    return pl.BlockSpec(arr.shape, lambda i, _nd=nd: (0,) * _nd)


def _row_spec(x, rows):
    if isinstance(x, tuple):
        _, w, cb = x
        return pl.BlockSpec((rows, w), lambda i, _cb=cb: (i, _cb))
    return pl.BlockSpec((rows, x.shape[1]), lambda i: (i, 0))


def _arr(x):
    return x[0] if isinstance(x, tuple) else x


def _width(x):
    return x[1] if isinstance(x, tuple) else x.shape[1]


def rows_fwd(fn, params, xs, stores, *, name, rows=ROWS):
    t = _arr(xs[0]).shape[0]
    np_, nx = len(params), len(xs)

    def body(*refs):
        p_refs, x_refs, o_refs = refs[:np_], refs[np_:np_ + nx], refs[np_ + nx:]
        outs = fn(*[r[...].astype(F32) for r in p_refs], *[r[...].astype(F32) for r in x_refs])
        for (idx, dt, _), o_ref in zip(stores, o_refs):
            o_ref[...] = outs[idx].astype(dt)

    res = pl.pallas_call(
        body, name=name,
        out_shape=tuple(jax.ShapeDtypeStruct((t, w), dt) for _, dt, w in stores),
        grid=(t // rows,),
        in_specs=[_full_spec(p) for p in params] + [_row_spec(x, rows) for x in xs],
        out_specs=tuple(pl.BlockSpec((rows, w), lambda i: (i, 0)) for _, _, w in stores),
        compiler_params=_cparams(("parallel",)),
    )(*params, *[_arr(x) for x in xs])
    return res


def rows_bwd(fn, params, xs, cts, dx_stores, *, name, rows=ROWS, concat=False):
    t = _arr(xs[0]).shape[0]
    np_, nx = len(params), len(xs)
    ct_idx = [i for i, c in enumerate(cts) if c is not None]
    ct_arrs = [cts[i] for i in ct_idx]
    nct = len(ct_arrs)
    nds = 1 if concat else len(dx_stores)
    widths = [_width(xs[xi]) for xi, _ in dx_stores]

    def body(*refs):
        p_refs = refs[:np_]
        x_refs = refs[np_:np_ + nx]
        c_refs = refs[np_ + nx:np_ + nx + nct]
        d_refs = refs[np_ + nx + nct:np_ + nx + nct + nds]
        g_refs = refs[np_ + nx + nct + nds:]
        pv = [r[...].astype(F32) for r in p_refs]
        xv = [r[...].astype(F32) for r in x_refs]
        outs, vjp = jax.vjp(lambda *a: tuple(fn(*a)), *pv, *xv)
        ct_full = [jnp.zeros_like(o) for o in outs]
        for i, r in zip(ct_idx, c_refs):
            ct_full[i] = r[...].astype(F32)
        grads = vjp(tuple(ct_full))
        if concat:
            off = 0
            for (xi, dt), wd in zip(dx_stores, widths):
                d_refs[0][:, off:off + wd] = grads[np_ + xi].astype(dt)
                off += wd
        else:
            for (xi, dt), d_ref in zip(dx_stores, d_refs):
                d_ref[...] = grads[np_ + xi].astype(dt)
        step = pl.program_id(0)
        for j, g_ref in enumerate(g_refs):
            @pl.when(step == 0)
            def _(g_ref=g_ref, j=j):
                g_ref[...] = grads[j]

            @pl.when(step > 0)
            def _(g_ref=g_ref, j=j):
                g_ref[...] += grads[j]

    dx_shapes = [(sum(widths), dx_stores[0][1])] if concat else [(wd, dt) for wd, (_, dt) in zip(widths, dx_stores)]
    out_shape = tuple(jax.ShapeDtypeStruct((t, wd), dt) for wd, dt in dx_shapes) + \
        tuple(jax.ShapeDtypeStruct(p.shape, F32) for p in params)
    out_specs = tuple(pl.BlockSpec((rows, wd), lambda i: (i, 0)) for wd, _ in dx_shapes) + \
        tuple(_full_spec(p) for p in params)
    return pl.pallas_call(
        body, name=name, out_shape=out_shape, grid=(t // rows,),
        in_specs=[_full_spec(p) for p in params] + [_row_spec(x, rows) for x in xs] + [_row_spec(c, rows) for c in ct_arrs],
        out_specs=out_specs,
        compiler_params=_cparams(("arbitrary",)),
    )(*params, *[_arr(x) for x in xs], *[_arr(c) for c in ct_arrs])


def _rms(x, g):
    return x * lax.rsqrt(jnp.mean(x * x, axis=-1, keepdims=True) + EPS) * g


def _sigmoid(x):
    return 1.0 / (1.0 + jnp.exp(-x))


def _silu(x):
    return x * _sigmoid(x)


def _gelu(x):
    return 0.5 * x * (1.0 + lax.erf(x * (1.0 / math.sqrt(2.0))))


def stage_norm(g, x):
    return x, _rms(x, g)


def stage_res_norm(g, x, y):
    xn = x + y
    return xn, _rms(xn, g)


def stage_res_bias_norm(b, g, x, y):
    xn = x + y + b
    return xn, _rms(xn, g)


def stage_glu(ba, bb, za, zb):
    return ((za + ba) * _sigmoid(zb + bb),)


def stage_ln_silu(dw_b, ln_g, ln_b, cv):
    z = cv + dw_b
    mu = jnp.mean(z, axis=-1, keepdims=True)
    zc = z - mu
    y = zc * lax.rsqrt(jnp.mean(zc * zc, axis=-1, keepdims=True) + EPS) * ln_g + ln_b
    return (_silu(y),)


CONV_ROWS = 128
CONV_COLS = 256
SUBLANES = 8


def _halo(k):
    return SUBLANES * ((k - 1 + SUBLANES - 1) // SUBLANES)


def _taps_by_roll(k):
    out = {}
    for s in range(k):
        out.setdefault(s % SUBLANES, []).append((s // SUBLANES, s))
    return out


def _shifted_down(win, k, r):
    halo, n = _halo(k), win.shape[0]
    segs = {}
    for b, lst in _taps_by_roll(k).items():
        rolled = win if b == 0 else pltpu.roll(win, b, axis=0)
        for a, s in lst:
            segs[s] = rolled[halo - SUBLANES * a: halo - SUBLANES * a + r]
    return segs


def _shifted_up(win, k, r):
    n = win.shape[0]
    segs = {}
    for b, lst in _taps_by_roll(k).items():
        rolled = win if b == 0 else pltpu.roll(win, n - b, axis=0)
        for a, s in lst:
            segs[s] = rolled[SUBLANES * a: SUBLANES * a + r]
    return segs


def _for_blocks(nblk, fn):
    fn(0, True, nblk == 1)
    if nblk > 2:
        def step(i, c):
            fn(i, False, False)
            return c
        lax.fori_loop(1, nblk - 1, step, 0)
    if nblk > 1:
        fn(nblk - 1, False, True)


def _base(i, r):
    return i * r if isinstance(i, int) else pl.multiple_of(i * r, r)


def _win_top(ref, i, first, r, halo):
    if first:
        return jnp.concatenate([jnp.zeros((halo, ref.shape[1]), F32), ref[pl.ds(0, r), :]], axis=0)
    base = _base(i, r)
    return ref[pl.ds(base - halo, r + halo), :]


def _win_bottom(ref, i, last, r, halo):
    base = _base(i, r)
    if last:
        return jnp.concatenate([ref[pl.ds(base, r), :], jnp.zeros((halo, ref.shape[1]), F32)], axis=0)
    return ref[pl.ds(base, r + halo), :]


def conv_fwd(x, w, *, mode, name, g=None):
    t = x.shape[0]
    k, c = w.shape
    r, cb, halo = min(CONV_ROWS, t), min(CONV_COLS, c), _halo(k)
    nblk = t // r

    def body(*refs):
        if mode == "silu_bwd":
            x_ref, w_ref, g_ref, o_ref = refs
        else:
            x_ref, w_ref, o_ref = refs

        def blk(i, first, last):
            segs = _shifted_down(_win_top(x_ref, i, first, r, halo), k, r)
            acc = None
            for s in range(k):
                term = w_ref[pl.ds(k - 1 - s, 1), :] * segs[s]
                acc = term if acc is None else acc + term
            base = _base(i, r)
            if mode == "silu":
                acc = _silu(acc)
            elif mode == "silu_bwd":
                sg = _sigmoid(acc)
                acc = g_ref[pl.ds(base, r), :] * (sg * (1.0 + acc * (1.0 - sg)))
            o_ref[pl.ds(base, r), :] = acc

        _for_blocks(nblk, blk)

    col = pl.BlockSpec((t, cb), lambda j: (0, j))
    in_specs = [col, pl.BlockSpec((k, cb), lambda j: (0, j))] + ([col] if mode == "silu_bwd" else [])
    args = [x, w] + ([g] if mode == "silu_bwd" else [])
    return pl.pallas_call(
        body, name=name, out_shape=jax.ShapeDtypeStruct((t, c), F32), grid=(c // cb,),
        in_specs=in_specs, out_specs=col, compiler_params=_cparams(("parallel",)),
    )(*args)


def conv_bwd(x, w, dy, *, name, into=None):
    t = x.shape[0]
    k, c = w.shape
    r, cb, halo = min(CONV_ROWS, t), min(CONV_COLS, c), _halo(k)
    nblk = t // r

    def body(x_ref, w_ref, dy_ref, *rest):
        dx_ref, dw_ref = rest[-2], rest[-1]
        dw_ref[...] = jnp.zeros_like(dw_ref)

        def blk(i, first, last):
            base = _base(i, r)
            up = _shifted_up(_win_bottom(dy_ref, i, last, r, halo), k, r)
            down = _shifted_down(_win_top(x_ref, i, first, r, halo), k, r)
            dyb = up[0]
            acc = None
            for s in range(k):
                term = w_ref[pl.ds(k - 1 - s, 1), :] * up[s]
                acc = term if acc is None else acc + term
                dw_ref[pl.ds(k - 1 - s, 1), :] += jnp.sum(down[s] * dyb, axis=0, keepdims=True)
            dx_ref[pl.ds(base, r), :] = acc.astype(dx_ref.dtype)

        _for_blocks(nblk, blk)

    col = pl.BlockSpec((t, cb), lambda j: (0, j))
    wsp = pl.BlockSpec((k, cb), lambda j: (0, j))
    dx_shape = jax.ShapeDtypeStruct((t, c), F32) if into is None else jax.ShapeDtypeStruct(into.shape, into.dtype)
    return pl.pallas_call(
        body, name=name,
        out_shape=(dx_shape, jax.ShapeDtypeStruct((k, c), F32)), grid=(c // cb,),
        in_specs=[col, wsp, col] + ([] if into is None else [pl.BlockSpec(memory_space=pl.ANY)]),
        out_specs=(col, wsp),
        input_output_aliases={} if into is None else {3: 0},
        compiler_params=_cparams(("parallel",)),
    )(*([x, w, dy] + ([] if into is None else [into])))


_DIMS = {"nn": (((1,), (0,)), ((), ())), "nt": (((1,), (1,)), ((), ())), "tn": (((0,), (0,)), ((), ()))}
_DIMS_BATCHED = {"nn": (((2,), (1,)), ((0,), (0,))), "nt": (((2,), (2,)), ((0,), (0,))), "tn": (((1,), (1,)), ((0,), (0,)))}


def _mxu(a, b, mode):
    dims = _DIMS_BATCHED if a.ndim == 3 else _DIMS
    return lax.dot_general(a, b, dims[mode], preferred_element_type=F32)


def _split(x):
    hi = x.astype(BF16)
    return hi, (x - hi.astype(F32)).astype(BF16)


def _dot_raw(a, b, mode, prec):
    if prec == "bf16":
        return _mxu(a.astype(BF16), b.astype(BF16), mode)
    if prec == "x3":
        ah, al = _split(a)
        bh, bl = _split(b)
        return _mxu(ah, bh, mode) + (_mxu(ah, bl, mode) + _mxu(al, bh, mode))
    if prec == "x3r":
        bh, bm = _split(b)
        bl = (b - bh.astype(F32) - bm.astype(F32)).astype(BF16)
        ah = a.astype(BF16)
        return _mxu(ah, bh, mode) + (_mxu(ah, bm, mode) + _mxu(ah, bl, mode))
    raise ValueError(prec)


@functools.lru_cache(maxsize=None)
def _dot_fn(mode, prec):
    bprec = "x3" if prec == "x3r" else prec

    @jax.custom_vjp
    def f(a, b):
        return _dot_raw(a, b, mode, prec)

    def fwd(a, b):
        return _dot_raw(a, b, mode, prec), (a, b)

    def bwd(res, ct):
        a, b = res
        if mode == "nn":
            return _dot_raw(ct, b, "nt", bprec), _dot_raw(a, ct, "tn", bprec)
        if mode == "nt":
            return _dot_raw(ct, b, "nn", bprec), _dot_raw(ct, a, "tn", bprec)
        return _dot_raw(b, ct, "nt", bprec), _dot_raw(a, ct, "nn", bprec)

    f.defvjp(fwd, bwd)
    return f


def _dot(a, b, mode="nn", prec="bf16"):
    return _dot_fn(mode, prec)(a, b)


def _inv_product(l):
    n = l.shape[-1]
    eye = (lax.broadcasted_iota(jnp.int32, (n, n), 0) == lax.broadcasted_iota(jnp.int32, (n, n), 1)).astype(F32)
    p = eye - l
    pw = l
    for _ in range(5):
        pw = _dot_raw(pw, pw, "nn", "x3")
        p = _dot_raw(p, eye + pw, "nn", "x3")
    return p


@jax.custom_vjp
def _inv_unit_lower(l, t_saved):
    return t_saved


def _inv_fwd(l, t_saved):
    return t_saved, t_saved


def _inv_bwd(t, ct):
    tmp = _dot_raw(t, ct, "tn", "x3")
    return -_dot_raw(tmp, t, "nt", "x3"), jnp.zeros_like(t)


_inv_unit_lower.defvjp(_inv_fwd, _inv_bwd)


def _softplus(x):
    pos = x > 0
    return jnp.where(pos, x, 0.0) + jnp.log(1.0 + jnp.exp(jnp.where(pos, -x, x)))


def _l2n(x):
    return x * lax.rsqrt(jnp.sum(x * x, axis=-1, keepdims=True) + EPS)


def delta_pair(s0, qc, kc, vc, z, pba, alog, dtb, onorm, t_saved=None):
    n = PAIR
    nh = qc.shape[0]
    assert qc.shape == (nh, n, HEAD_DIM) and n == HEAD_DIM
    hi = lax.broadcasted_iota(jnp.int32, (nh, 1, pba.shape[1]), 0)
    li = lax.broadcasted_iota(jnp.int32, (nh, 1, pba.shape[1]), 2)
    braw = jnp.sum(pba[None] * (li == hi).astype(F32), axis=2, keepdims=True)
    araw = jnp.sum(pba[None] * (li == hi + nh).astype(F32), axis=2, keepdims=True)
    ri = lax.broadcasted_iota(jnp.int32, (n, n), 0)
    ci = lax.broadcasted_iota(jnp.int32, (n, n), 1)
    same = (ri // CHUNK) == (ci // CHUNK)
    tri = same & (ci <= ri)
    same_f = jnp.broadcast_to(same.astype(F32), (nh, n, n))
    tri_f = jnp.broadcast_to(tri.astype(F32), (nh, n, n))
    strict_f = (same & (ci < ri)).astype(F32)
    m0 = (lax.broadcasted_iota(jnp.int32, (n, 1), 0) < CHUNK).astype(F32)
    m1 = 1.0 - m0

    q = _l2n(qc) * (HEAD_DIM ** -0.5)
    k = _l2n(kc)
    beta = _sigmoid(braw)
    g = -jnp.exp(alog) * _softplus(araw + dtb)
    gb = jnp.broadcast_to(g, (nh, n, n))
    gc = _dot(tri_f, gb, "nn", "x3r")
    gtot = _dot(same_f, gb, "nn", "x3r")
    decay = jnp.exp(jnp.where(tri, gc - jnp.swapaxes(gc, 1, 2), -1e30))
    eg = jnp.exp(gc)
    kb, vb = k * beta, vc * beta
    l = _dot(kb, k, "nt") * decay * strict_f
    if t_saved is None:
        tinv = _inv_product(l)
    else:
        tinv = _inv_unit_lower(l, t_saved)
    u = _dot(tinv, vb, "nn", "x3")
    w = _dot(tinv, kb * eg, "nn", "x3")
    attn = _dot(q, k, "nt") * decay
    q_dec = q * eg
    k_tail = k * jnp.exp(gtot - gc)
    gl0 = jnp.exp(jnp.sum(gb * m0, axis=1, keepdims=True))
    gl1 = jnp.exp(jnp.sum(gb * m1, axis=1, keepdims=True))

    vn0 = m0 * (u - _dot(w, s0))
    s1 = s0 * gl0 + _dot(k_tail, vn0, "tn")
    vn1 = m1 * (u - _dot(w, s1))
    o = m0 * _dot(q_dec, s0) + m1 * _dot(q_dec, s1) + _dot(attn, vn0 + vn1)
    s2 = s1 * gl1 + _dot(k_tail, vn1, "tn")

    on = o * lax.rsqrt(jnp.mean(o * o, axis=-1, keepdims=True) + EPS) * onorm
    return on * _silu(z), s2, tinv


def _hcols(i):
    return slice(i * HEAD_DIM, (i + 1) * HEAD_DIM)


def _heads(ref, first, count):
    return jnp.stack([ref[:, _hcols(first + i)] for i in range(count)])


def delta_fwd(qkv, pm, pba, alog, dtb, onorm, cat_width, *, name):
    t = qkv.shape[0]
    h = alog.shape[0]
    hd = h * HEAD_DIM
    npair = t // PAIR
    mat = pl.BlockSpec((h, None, PAIR, HEAD_DIM), lambda p: (0, p, 0, 0))
    par = pl.BlockSpec((h, 1, 1), lambda p: (0, 0, 0))

    def body(qkv_ref, z_ref, pba_ref, al_ref, dt_ref, on_ref, o_ref, st_ref, ti_ref, s_scr):
        @pl.when(pl.program_id(0) == 0)
        def _():
            s_scr[...] = jnp.zeros_like(s_scr)

        s0 = s_scr[...]
        st_ref[...] = s0
        out, s2, tinv = delta_pair(s0, _heads(qkv_ref, 0, h), _heads(qkv_ref, h, h), _heads(qkv_ref, 2 * h, h), _heads(z_ref, 0, h),
                                   pba_ref[...], al_ref[...], dt_ref[...], on_ref[...])
        for i in range(h):
            o_ref[:, _hcols(i)] = out[i].astype(o_ref.dtype)
        ti_ref[...] = tinv
        s_scr[...] = s2

    return pl.pallas_call(
        body, name=name, grid=(npair,),
        out_shape=(jax.ShapeDtypeStruct((t, cat_width), BF16),
                   jax.ShapeDtypeStruct((h, npair, PAIR, HEAD_DIM), F32),
                   jax.ShapeDtypeStruct((h, npair, PAIR, PAIR), F32)),
        in_specs=[pl.BlockSpec((PAIR, 3 * hd), lambda p: (p, 0)), pl.BlockSpec((PAIR, hd), lambda p: (p, 3)),
                  pl.BlockSpec((PAIR, pba.shape[1]), lambda p: (p, 0)), par, par,
                  pl.BlockSpec((1, HEAD_DIM), lambda p: (0, 0))],
        out_specs=(pl.BlockSpec((PAIR, hd), lambda p: (p, 0)), mat, mat),
        scratch_shapes=[pltpu.VMEM((h, HEAD_DIM, HEAD_DIM), F32)],
        compiler_params=_cparams(("arbitrary",)),
    )(qkv, pm, pba, alog, dtb, onorm)


def delta_bwd(qkv, pm, pba, alog, dtb, onorm, states, tinvs, dcat, *, name):
    t = qkv.shape[0]
    h = alog.shape[0]
    hd = h * HEAD_DIM
    npair = t // PAIR
    rev = lambda p: npair - 1 - p
    mat = pl.BlockSpec((h, None, PAIR, HEAD_DIM), lambda p: (0, rev(p), 0, 0))
    par = pl.BlockSpec((h, 1, 1), lambda p: (0, 0, 0))
    onsp = pl.BlockSpec((1, HEAD_DIM), lambda p: (0, 0))
    wide = pl.BlockSpec((PAIR, 3 * hd), lambda p: (rev(p), 0))
    zsp = pl.BlockSpec((PAIR, hd), lambda p: (rev(p), 3))
    bsp = pl.BlockSpec((PAIR, pba.shape[1]), lambda p: (rev(p), 0))

    def body(qkv_ref, z_ref, pba_ref, al_ref, dt_ref, on_ref, st_ref, ti_ref, dc_ref,
             dqkv_ref, dz_ref, dpba_ref, dal_ref, ddt_ref, don_ref, ds_scr):
        @pl.when(pl.program_id(0) == 0)
        def _():
            ds_scr[...] = jnp.zeros_like(ds_scr)
            dal_ref[...] = jnp.zeros_like(dal_ref)
            ddt_ref[...] = jnp.zeros_like(ddt_ref)
            don_ref[...] = jnp.zeros_like(don_ref)

        tsv = ti_ref[...]
        fn = lambda s0, qc, kc, vc, z, pb, al, dt, on: delta_pair(s0, qc, kc, vc, z, pb, al, dt, on, tsv)[:2]
        _, vjp = jax.vjp(fn, st_ref[...], _heads(qkv_ref, 0, h), _heads(qkv_ref, h, h), _heads(qkv_ref, 2 * h, h),
                         _heads(z_ref, 0, h), pba_ref[...], al_ref[...], dt_ref[...], on_ref[...])
        ds0, dq, dk, dv, dz, dpba, dal, ddt, don = vjp((_heads(dc_ref, 0, h), ds_scr[...]))
        ds_scr[...] = ds0
        for i in range(h):
            dqkv_ref[:, _hcols(i)] = dq[i]
            dqkv_ref[:, _hcols(h + i)] = dk[i]
            dqkv_ref[:, _hcols(2 * h + i)] = dv[i]
            dz_ref[:, _hcols(i)] = dz[i].astype(dz_ref.dtype)
        dal_ref[...] += dal
        ddt_ref[...] += ddt
        dpba_ref[...] = dpba.astype(dpba_ref.dtype)
        don_ref[...] += don

    return pl.pallas_call(
        body, name=name, grid=(npair,),
        out_shape=(jax.ShapeDtypeStruct((t, 3 * hd), F32), jax.ShapeDtypeStruct(pm.shape, BF16),
                   jax.ShapeDtypeStruct(pba.shape, BF16),
                   jax.ShapeDtypeStruct((h, 1, 1), F32), jax.ShapeDtypeStruct((h, 1, 1), F32),
                   jax.ShapeDtypeStruct((1, HEAD_DIM), F32)),
        in_specs=[wide, zsp, bsp, par, par, onsp, mat, mat, pl.BlockSpec((PAIR, hd), lambda p: (rev(p), 0))],
        out_specs=(wide, zsp, bsp, par, par, onsp),
        scratch_shapes=[pltpu.VMEM((h, HEAD_DIM, HEAD_DIM), F32)],
        compiler_params=_cparams(("arbitrary",)),
    )(qkv, pm, pba, alog, dtb, onorm, states, tinvs, dcat)


def gmlp_block(ln_g, ln_b, w, bcol, u_raw, v_raw):
    n = w.shape[-1]
    ri = lax.broadcasted_iota(jnp.int32, (n, n), 0)
    ci = lax.broadcasted_iota(jnp.int32, (n, n), 1)
    mask = ((ci // CHUNK) <= (ri // CHUNK)).astype(F32)
    vg = _gelu(v_raw)
    vc = vg - jnp.mean(vg, axis=-1, keepdims=True)
    vgn = vc * lax.rsqrt(jnp.mean(vc * vc, axis=-1, keepdims=True) + EPS) * ln_g + ln_b
    return _gelu(u_raw) * (_dot(w * mask, vgn) + bcol)


def gmlp_fwd(pm, ln_g, ln_b, w_s, bcol, cat, *, name):
    t = pm.shape[0]
    g = w_s.shape[0]
    gw = g * HEAD_DIM
    assert pm.shape[1] == 6 * gw and cat.shape[1] == 2 * gw

    def body(u_ref, v_ref, lg_ref, lb_ref, w_ref, b_ref, cat_in, o_ref):
        del cat_in
        out = gmlp_block(lg_ref[...], lb_ref[...], w_ref[...], b_ref[...], _heads(u_ref, 0, g), _heads(v_ref, 0, g))
        for i in range(g):
            o_ref[:, _hcols(i)] = out[i].astype(o_ref.dtype)

    full = lambda a: pl.BlockSpec(a.shape, lambda m: (0, 0, 0))
    return pl.pallas_call(
        body, name=name, grid=(t // HEAD_DIM,),
        out_shape=jax.ShapeDtypeStruct(cat.shape, cat.dtype),
        in_specs=[pl.BlockSpec((HEAD_DIM, gw), lambda m: (m, 4)), pl.BlockSpec((HEAD_DIM, gw), lambda m: (m, 5)),
                  full(ln_g), full(ln_b), full(w_s), full(bcol), pl.BlockSpec(memory_space=pl.ANY)],
        out_specs=pl.BlockSpec((HEAD_DIM, gw), lambda m: (m, 1)),
        input_output_aliases={6: 0},
        compiler_params=_cparams(("arbitrary",)),
    )(pm, pm, ln_g, ln_b, w_s, bcol, cat)


def gmlp_bwd(pm, ln_g, ln_b, w_s, bcol, dcat, dpm, *, name):
    t = pm.shape[0]
    g = w_s.shape[0]
    gw = g * HEAD_DIM
    assert pm.shape[1] == 6 * gw and dpm.shape == pm.shape

    def body(u_ref, v_ref, lg_ref, lb_ref, w_ref, b_ref, dc_ref, dpm_in, duv_ref, dlg_ref, dlb_ref, dw_ref, db_ref):
        del dpm_in
        first = pl.program_id(0) == 0
        _, vjp = jax.vjp(gmlp_block, lg_ref[...], lb_ref[...], w_ref[...], b_ref[...], _heads(u_ref, 0, g), _heads(v_ref, 0, g))
        dlg, dlb, dw, db, du, dv = vjp(_heads(dc_ref, 0, g))
        for i in range(g):
            duv_ref[:, _hcols(i)] = du[i].astype(duv_ref.dtype)
            duv_ref[:, _hcols(g + i)] = dv[i].astype(duv_ref.dtype)
        for ref, val in ((dlg_ref, dlg), (dlb_ref, dlb), (dw_ref, dw), (db_ref, db)):
            @pl.when(first)
            def _(ref=ref, val=val):
                ref[...] = val

            @pl.when(jnp.logical_not(first))
            def _(ref=ref, val=val):
                ref[...] += val

    full = lambda a: pl.BlockSpec(a.shape, lambda m: (0, 0, 0))
    return pl.pallas_call(
        body, name=name, grid=(t // HEAD_DIM,),
        out_shape=(jax.ShapeDtypeStruct(dpm.shape, dpm.dtype),
                   jax.ShapeDtypeStruct(ln_g.shape, F32), jax.ShapeDtypeStruct(ln_b.shape, F32),
                   jax.ShapeDtypeStruct(w_s.shape, F32), jax.ShapeDtypeStruct(bcol.shape, F32)),
        in_specs=[pl.BlockSpec((HEAD_DIM, gw), lambda m: (m, 4)), pl.BlockSpec((HEAD_DIM, gw), lambda m: (m, 5)),
                  full(ln_g), full(ln_b), full(w_s), full(bcol), pl.BlockSpec((HEAD_DIM, gw), lambda m: (m, 1)),
                  pl.BlockSpec(memory_space=pl.ANY)],
        out_specs=(pl.BlockSpec((HEAD_DIM, 2 * gw), lambda m: (m, 2)), full(ln_g), full(ln_b), full(w_s), full(bcol)),
        input_output_aliases={7: 0},
        compiler_params=_cparams(("arbitrary",)),
    )(pm, pm, ln_g, ln_b, w_s, bcol, dcat, dpm)


def loss_head(g, x, r, tgt, *, name, rows=ROWS):
    t, d = x.shape

    def body(g_ref, x_ref, r_ref, t_ref, l_ref, dx_ref, dxb_ref, dg_ref):
        y, vjp = jax.vjp(lambda gg, xx: _rms(xx, gg), g_ref[...], x_ref[...] + r_ref[...])
        e = y - t_ref[...]
        part = (0.5 / d) * jnp.sum(jnp.sum(e * e, axis=1, keepdims=True), axis=0, keepdims=True)
        dg, dx = vjp(e * (1.0 / d))
        dx_ref[...] = dx
        dxb_ref[...] = dx.astype(BF16)
        first = pl.program_id(0) == 0

        @pl.when(first)
        def _():
            l_ref[...] = part
            dg_ref[...] = dg

        @pl.when(jnp.logical_not(first))
        def _():
            l_ref[...] += part
            dg_ref[...] += dg

    rs = pl.BlockSpec((rows, d), lambda i: (i, 0))
    gs = pl.BlockSpec((1, d), lambda i: (0, 0))
    return pl.pallas_call(
        body, name=name, grid=(t // rows,),
        out_shape=(jax.ShapeDtypeStruct((1, 1), F32), jax.ShapeDtypeStruct((t, d), F32),
                   jax.ShapeDtypeStruct((t, d), BF16), jax.ShapeDtypeStruct((1, d), F32)),
        in_specs=[gs, rs, rs, rs],
        out_specs=(pl.BlockSpec((1, 1), lambda i: (0, 0)), rs, rs, gs),
        compiler_params=_cparams(("arbitrary",)),
    )(g, x, r, tgt)


def adamw(w, gs, m, v, *, name, behind=None):
    nl, r, c = w.shape
    assert len(gs) == nl
    if r % SUBLANES == 0:
        tr, tc = _pick(r, (256, 128, 64, 32, 16, 8)), c
    else:
        tr, tc = r, _pick(c, (256, 128))
    k1 = 1.0 - ADAM_B1 ** ADAM_STEP
    k2 = 1.0 - ADAM_B2 ** ADAM_STEP

    def body(*refs):
        w_ref, m_ref, v_ref = refs[0], refs[1], refs[2]
        g_refs = refs[3:3 + nl]
        go_ref, d_ref, mo_ref, vo_ref = refs[-4:]
        gg = g_refs[0][...]
        for li in range(1, nl):
            gg = jnp.where(pl.program_id(0) == li, g_refs[li][...], gg)
        mn = ADAM_B1 * m_ref[...] + (1.0 - ADAM_B1) * gg
        vn = ADAM_B2 * v_ref[...] + (1.0 - ADAM_B2) * (gg * gg)
        go_ref[...] = gg
        d_ref[...] = -ADAM_LR * ((mn / k1) / (jnp.sqrt(vn / k2) + ADAM_EPS) + ADAM_WD * w_ref[...])
        mo_ref[...] = mn
        vo_ref[...] = vn

    if tc == c:
        sp = pl.BlockSpec((None, tr, c), lambda l, i: (l, i, 0))
        gsp = pl.BlockSpec((tr, c), lambda l, i: (i, 0))
    else:
        sp = pl.BlockSpec((None, r, tc), lambda l, i: (l, 0, i))
        gsp = pl.BlockSpec((r, tc), lambda l, i: (0, i))
    sds = jax.ShapeDtypeStruct((nl, r, c), F32)
    extra = [] if behind is None else [behind]
    return pl.pallas_call(
        body, name=name, grid=(nl, (r // tr) * (c // tc)), out_shape=(sds, sds, sds, sds),
        in_specs=[sp, sp, sp] + [gsp] * nl + [pl.BlockSpec(e.shape, lambda l, i: (0, 0)) for e in extra], out_specs=(sp, sp, sp, sp),
        compiler_params=_cparams(("parallel", "parallel")),
    )(w, m, v, *gs, *extra)


MESH = pl.DeviceIdType.MESH
ANY = pl.BlockSpec(memory_space=pl.ANY)


def _me():
    return lax.axis_index("x"), lax.axis_index("y"), lax.axis_index("c")


def _other_chips(x, y):
    return [(1 - x, y), (x, 1 - y), (1 - x, 1 - y)]


def gather_shards(arrs, *, name):
    n = len(arrs)
    per = 7

    def body(*refs):
        ins, outs = refs[:n], refs[n:2 * n]
        send_sems, recv_sems = refs[2 * n:]
        x, y, c = _me()
        chip = 2 * x + y
        sib = (x, y, 1 - c)
        chips = _other_chips(x, y)

        def rc(i, k, src, dst, to):
            return pltpu.make_async_remote_copy(src_ref=src, dst_ref=dst, send_sem=send_sems.at[per * i + k],
                                                recv_sem=recv_sems.at[per * i + k], device_id=to, device_id_type=MESH)

        def half(i, which):
            hrows = arrs[i].shape[0] // 2
            return pl.ds(which * hrows, hrows)

        own = [rc(i, 6, ins[i], outs[i].at[chip], sib) for i in range(n)]
        for cp in own:
            cp.start()
        started = []
        for i in range(n):
            for j, ch in enumerate(chips):
                cp = rc(i, j, ins[i].at[half(i, c)], outs[i].at[chip, half(i, c)], (ch[0], ch[1], c))
                cp.start()
                started.append(cp)
        for i in range(n):
            for j, ch in enumerate(chips):
                sc = 2 * ch[0] + ch[1]
                landed = outs[i].at[sc, half(i, c)]
                rc(i, j, ins[i].at[half(i, c)], landed, (ch[0], ch[1], c)).wait_recv()
                fw = rc(i, 3 + j, landed, landed, sib)
                fw.start()
                started.append(fw)
        for i in range(n):
            for j, ch in enumerate(chips):
                sc = 2 * ch[0] + ch[1]
                theirs = outs[i].at[sc, half(i, 1 - c)]
                rc(i, 3 + j, theirs, theirs, sib).wait_recv()
        for cp in own:
            cp.wait_recv()
        for cp in started + own:
            cp.wait_send()

    return pl.pallas_call(
        body, name=name,
        out_shape=tuple(jax.ShapeDtypeStruct((N_CHIPS,) + a.shape, a.dtype) for a in arrs),
        in_specs=[ANY] * n, out_specs=tuple([ANY] * n),
        scratch_shapes=[pltpu.SemaphoreType.DMA((per * n,)), pltpu.SemaphoreType.DMA((per * n,))],
        compiler_params=pltpu.CompilerParams(has_side_effects=True),
    )(*arrs)


HBM = pl.BlockSpec(memory_space=pltpu.HBM)
SEM = pl.BlockSpec(memory_space=pltpu.SEMAPHORE)
EFFECT = pltpu.SideEffectType.DATAFLOW_SIDE_EFFECTING
def _hbm(a):
    return pltpu.with_memory_space_constraint(a, pltpu.HBM)


def _half_rows(arr, which):
    hrows = arr.shape[-2] // 2
    return pl.ds(which * hrows, hrows)


def _ici_copies(srcs, lands, sems):
    x, y, c = _me()
    chip = 2 * x + y
    return [pltpu.make_async_remote_copy(src_ref=srcs[i].at[_half_rows(srcs[i], c)], dst_ref=lands[i].at[chip, _half_rows(srcs[i], c)],
                                         send_sem=sems[0].at[3 * i + j], recv_sem=sems[1].at[3 * i + j],
                                         device_id=(ch[0], ch[1], c), device_id_type=MESH)
            for i in range(len(srcs)) for j, ch in enumerate(_other_chips(x, y))]


def _own_copies(srcs, lands, sems):
    x, y, c = _me()
    return [pltpu.make_async_remote_copy(src_ref=srcs[i], dst_ref=lands[i].at[2 * x + y], send_sem=sems[0].at[i],
                                         recv_sem=sems[1].at[i], device_id=(x, y, 1 - c), device_id_type=MESH)
            for i in range(len(srcs))]


def _fwd_copies(srcs, lands, sems, half):
    x, y, c = _me()
    cps = []
    for i in range(len(srcs)):
        for j, ch in enumerate(_other_chips(x, y)):
            part = lands[i].at[2 * ch[0] + ch[1], _half_rows(srcs[i], half)]
            cps.append(pltpu.make_async_remote_copy(src_ref=part, dst_ref=part, send_sem=sems[0].at[3 * i + j],
                                                    recv_sem=sems[1].at[3 * i + j], device_id=(x, y, 1 - c), device_id_type=MESH))
    return cps


def gather_start(groups, *, name):
    arrs = [a for g in groups for a in g]
    n, ng = len(arrs), len(groups)
    bounds = np.cumsum([0] + [len(g) for g in groups])

    def body(*refs):
        srcs, lands = refs[:n], refs[n:2 * n]
        sems = refs[2 * n:2 * n + 4 * ng]
        token = refs[-1]
        for gi in range(ng):
            lo, hi = bounds[gi], bounds[gi + 1]
            for cp in _ici_copies(srcs[lo:hi], lands[lo:hi], sems[4 * gi:4 * gi + 2]):
                cp.start()
        for gi in range(ng):
            lo, hi = bounds[gi], bounds[gi + 1]
            for cp in _own_copies(srcs[lo:hi], lands[lo:hi], sems[4 * gi + 2:4 * gi + 4]):
                cp.start()
        token[...] = jnp.zeros_like(token)

    sem_shapes = []
    for g in groups:
        sem_shapes += [pltpu.SemaphoreType.DMA((3 * len(g),))] * 2 + [pltpu.SemaphoreType.DMA((len(g),))] * 2
    res = pl.pallas_call(
        body, name=name,
        out_shape=tuple(sem_shapes) + tuple(pltpu.HBM(a.shape, a.dtype) for a in arrs)
        + tuple(pltpu.HBM((N_CHIPS,) + a.shape, a.dtype) for a in arrs) + (jax.ShapeDtypeStruct((SUBLANES, LANES), F32),),
        in_specs=[HBM] * (2 * n),
        out_specs=tuple([SEM] * (4 * ng)) + tuple([HBM] * (2 * n)) + (pl.BlockSpec(memory_space=pltpu.VMEM),),
        input_output_aliases={i: 4 * ng + i for i in range(2 * n)},
        compiler_params=pltpu.CompilerParams(has_side_effects=EFFECT),
    )(*[_hbm(a) for a in arrs], *[_hbm(lax.empty((N_CHIPS,) + a.shape, a.dtype)) for a in arrs])
    sems, thru, lands, token = res[:4 * ng], res[4 * ng:4 * ng + n], res[4 * ng + n:4 * ng + 2 * n], res[-1]
    out = [{"ici": (sems[4 * gi], sems[4 * gi + 1]), "own": (sems[4 * gi + 2], sems[4 * gi + 3]),
            "thru": list(thru[bounds[gi]:bounds[gi + 1]]), "lands": list(lands[bounds[gi]:bounds[gi + 1]])} for gi in range(ng)]
    return out, token


def gather_forward(group, after, *, name):
    thru, lands = group["thru"], group["lands"]
    n = len(thru)

    def body(*refs):
        srcs, lands_r = refs[:n], refs[n:2 * n]
        ici = refs[2 * n:2 * n + 2]
        fwd = refs[2 * n + 3 + 2 * n:2 * n + 3 + 2 * n + 2]
        for cp in _ici_copies(srcs, lands_r, ici):
            cp.wait_send()
            cp.wait_recv()
        for cp in _fwd_copies(srcs, lands_r, fwd, _me()[2]):
            cp.start()
        refs[-1][...] = jnp.zeros_like(refs[-1])

    res = pl.pallas_call(
        body, name=name,
        out_shape=tuple(pltpu.HBM(a.shape, a.dtype) for a in thru) + tuple(pltpu.HBM(a.shape, a.dtype) for a in lands)
        + (pltpu.SemaphoreType.DMA((3 * n,)),) * 2 + (jax.ShapeDtypeStruct((SUBLANES, LANES), F32),),
        in_specs=[HBM] * (2 * n) + [SEM, SEM, pl.BlockSpec(memory_space=pl.ANY)],
        out_specs=tuple([HBM] * (2 * n)) + (SEM, SEM, pl.BlockSpec(memory_space=pltpu.VMEM)),
        input_output_aliases={i: i for i in range(2 * n)},
        compiler_params=pltpu.CompilerParams(has_side_effects=EFFECT),
    )(*thru, *lands, *group["ici"], after)
    return {"own": group["own"], "fwd": (res[2 * n], res[2 * n + 1]), "thru": list(res[:n]), "lands": list(res[n:2 * n]),
            "token": res[-1]}


def gather_wait(group, after, *, name):
    thru, lands = group["thru"], group["lands"]
    n = len(thru)

    def body(*refs):
        srcs, lands_r = refs[:n], refs[n:2 * n]
        own, fwd = refs[2 * n:2 * n + 2], refs[2 * n + 2:2 * n + 4]
        c = _me()[2]
        for mine, theirs in zip(_fwd_copies(srcs, lands_r, fwd, c), _fwd_copies(srcs, lands_r, fwd, 1 - c)):
            mine.wait_send()
            theirs.wait_recv()
        for cp in _own_copies(srcs, lands_r, own):
            cp.wait_send()
            cp.wait_recv()

    res = pl.pallas_call(
        body, name=name,
        out_shape=tuple(pltpu.HBM(a.shape, a.dtype) for a in thru) + tuple(pltpu.HBM(a.shape, a.dtype) for a in lands),
        in_specs=[HBM] * (2 * n) + [SEM] * 4 + [pl.BlockSpec(memory_space=pl.ANY)],
        out_specs=tuple([HBM] * (2 * n)),
        input_output_aliases={i: i for i in range(2 * n)},
        compiler_params=pltpu.CompilerParams(has_side_effects=EFFECT),
    )(*thru, *lands, *group["own"], *group["fwd"], after)
    return list(res[n:])


def _pair_copies(srcs, lands, sems):
    x, y, c = _me()
    return [pltpu.make_async_remote_copy(src_ref=srcs[i].at[:, _half_rows(srcs[i], 1 - c), :], dst_ref=lands[i], send_sem=sems[0].at[i],
                                         recv_sem=sems[1].at[i], device_id=(x, y, 1 - c), device_id_type=MESH)
            for i in range(len(srcs))]


def _scatter_copies(srcs, lands, sems):
    x, y, c = _me()
    return [pltpu.make_async_remote_copy(src_ref=srcs[i].at[2 * ch[0] + ch[1]], dst_ref=lands[i].at[j], send_sem=sems[0].at[3 * i + j],
                                         recv_sem=sems[1].at[3 * i + j], device_id=(ch[0], ch[1], c), device_id_type=MESH)
            for i in range(len(srcs)) for j, ch in enumerate(_other_chips(x, y))]


def split_start(copies, arrs, land_shapes, nsem, *, name):
    n = len(arrs)

    def body(*refs):
        for cp in copies(refs[:n], refs[n:2 * n], refs[2 * n:2 * n + 2]):
            cp.start()
        refs[-1][...] = jnp.zeros_like(refs[-1])

    res = pl.pallas_call(
        body, name=name,
        out_shape=(pltpu.SemaphoreType.DMA((nsem,)),) * 2 + tuple(pltpu.HBM(a.shape, a.dtype) for a in arrs)
        + tuple(pltpu.HBM(s, a.dtype) for s, a in zip(land_shapes, arrs)) + (jax.ShapeDtypeStruct((SUBLANES, LANES), F32),),
        in_specs=[HBM] * (2 * n), out_specs=(SEM, SEM) + tuple([HBM] * (2 * n)) + (pl.BlockSpec(memory_space=pltpu.VMEM),),
        input_output_aliases={i: 2 + i for i in range(2 * n)},
        compiler_params=pltpu.CompilerParams(has_side_effects=EFFECT),
    )(*[_hbm(a) for a in arrs], *[_hbm(lax.empty(s, a.dtype)) for s, a in zip(land_shapes, arrs)])
    return {"copies": copies, "sems": (res[0], res[1]), "thru": list(res[2:2 + n]), "lands": list(res[2 + n:2 + 2 * n]),
            "token": res[-1]}


def split_wait(state, after, *, name):
    thru, lands, copies = state["thru"], state["lands"], state["copies"]
    n = len(thru)

    def body(*refs):
        for cp in copies(refs[:n], refs[n:2 * n], refs[2 * n:2 * n + 2]):
            cp.wait_send()
            cp.wait_recv()

    res = pl.pallas_call(
        body, name=name,
        out_shape=tuple(pltpu.HBM(a.shape, a.dtype) for a in thru) + tuple(pltpu.HBM(a.shape, a.dtype) for a in lands),
        in_specs=[HBM] * (2 * n) + [SEM, SEM, pl.BlockSpec(memory_space=pl.ANY)], out_specs=tuple([HBM] * (2 * n)),
        input_output_aliases={i: i for i in range(2 * n)},
        compiler_params=pltpu.CompilerParams(has_side_effects=EFFECT),
    )(*thru, *lands, *state["sems"], after)
    return list(res[:n]), list(res[n:])


def sibling_join_halves(arrs, *, name):
    n = len(arrs)

    def body(*refs):
        outs = refs[n:2 * n]
        send_sems, recv_sems = refs[2 * n:]
        x, y, c = _me()
        cps = []
        for i in range(n):
            hrows = arrs[i].shape[0] // 2
            mine = outs[i].at[pl.ds(c * hrows, hrows)]
            cps.append(pltpu.make_async_remote_copy(src_ref=mine, dst_ref=mine, send_sem=send_sems.at[i],
                                                    recv_sem=recv_sems.at[i], device_id=(x, y, 1 - c), device_id_type=MESH))
        for cp in cps:
            cp.start()
        for i in range(n):
            hrows = arrs[i].shape[0] // 2
            theirs = outs[i].at[pl.ds((1 - c) * hrows, hrows)]
            pltpu.make_async_remote_copy(src_ref=theirs, dst_ref=theirs, send_sem=send_sems.at[i], recv_sem=recv_sems.at[i],
                                         device_id=(x, y, 1 - c), device_id_type=MESH).wait_recv()
        for cp in cps:
            cp.wait_send()

    return pl.pallas_call(
        body, name=name,
        out_shape=tuple(jax.ShapeDtypeStruct(a.shape, a.dtype) for a in arrs),
        in_specs=[ANY] * n, out_specs=tuple([ANY] * n),
        input_output_aliases={i: i for i in range(n)},
        scratch_shapes=[pltpu.SemaphoreType.DMA((n,)), pltpu.SemaphoreType.DMA((n,))],
        compiler_params=pltpu.CompilerParams(has_side_effects=True),
    )(*arrs)


def pre_reduce(gd, sib, who, *, name):
    _, r, c = gd.shape
    h = r // 2
    tr = _pick(h, (256, 128, 64, 32, 16, 8))
    nrb = h // tr

    def body(who_ref, gd_ref, sib_ref, pb_ref, own_ref):
        p = gd_ref[...] + sib_ref[...]
        pb_ref[...] = p.astype(BF16)

        @pl.when(pl.program_id(1) == who_ref[1])
        def _():
            own_ref[...] = p

    return pl.pallas_call(
        body, name=name,
        out_shape=(jax.ShapeDtypeStruct((N_CHIPS, h, c), BF16), jax.ShapeDtypeStruct((h, c), F32)),
        grid_spec=pltpu.PrefetchScalarGridSpec(
            num_scalar_prefetch=1, grid=(nrb, N_CHIPS),
            in_specs=[pl.BlockSpec((None, tr, c), lambda i, s, w: (s, w[0] * nrb + i, 0)),
                      pl.BlockSpec((None, tr, c), lambda i, s, w: (s, i, 0))],
            out_specs=(pl.BlockSpec((None, tr, c), lambda i, s, w: (s, i, 0)),
                       pl.BlockSpec((tr, c), lambda i, s, w: (i, 0)))),
        compiler_params=_cparams(("parallel", "arbitrary")),
    )(who, gd, sib)


def final_reduce(own, rcv, who, *, name):
    h, c = own.shape
    tr = _pick(h, (256, 128, 64, 32, 16, 8))
    nrb = h // tr

    def body(who_ref, own_ref, rcv_ref, o_ref):
        del who_ref
        acc = own_ref[...]
        for j in range(3):
            acc = acc + rcv_ref[j].astype(F32)
        o_ref[...] = acc

    return pl.pallas_call(
        body, name=name, out_shape=jax.ShapeDtypeStruct((2 * h, c), F32),
        grid_spec=pltpu.PrefetchScalarGridSpec(
            num_scalar_prefetch=1, grid=(nrb,),
            in_specs=[pl.BlockSpec((tr, c), lambda i, w: (i, 0)), pl.BlockSpec((3, tr, c), lambda i, w: (0, i, 0))],
            out_specs=pl.BlockSpec((tr, c), lambda i, w: (w[0] * nrb + i, 0))),
        compiler_params=_cparams(("parallel",)),
    )(who, own, rcv)


def sum8(landed, own, me, *, name):
    _, r, c = landed.shape
    tr = _pick(r, (256, 128, 64, 32, 16, 8))

    def body(me_ref, a_ref, own_ref, o_ref):
        acc = None
        for j in range(8):
            term = jnp.where(me_ref[0] == j, own_ref[...], a_ref[j])
            acc = term if acc is None else acc + term
        o_ref[...] = acc

    return pl.pallas_call(
        body, name=name, out_shape=jax.ShapeDtypeStruct((r, c), F32),
        grid_spec=pltpu.PrefetchScalarGridSpec(
            num_scalar_prefetch=1, grid=(r // tr,),
            in_specs=[pl.BlockSpec((8, tr, c), lambda i, w: (0, i, 0)), pl.BlockSpec((tr, c), lambda i, w: (i, 0))],
            out_specs=pl.BlockSpec((tr, c), lambda i, w: (i, 0))),
        compiler_params=_cparams(("parallel",)),
    )(me, landed, own)


def _everyone_copies(srcs, lands, sems):
    x, y, c = _me()
    me = 4 * x + 2 * y + c
    cps = []
    for k in range(1, 8):
        fx, fy, fc = (k >> 2) & 1, (k >> 1) & 1, k & 1
        to = (x + fx - 2 * x * fx, y + fy - 2 * y * fy, c + fc - 2 * c * fc)
        cps.append(pltpu.make_async_remote_copy(src_ref=srcs[0], dst_ref=lands[0].at[me], send_sem=sems[0].at[k - 1],
                                                recv_sem=sems[1].at[k - 1], device_id=to, device_id_type=MESH))
    return cps


def _behind(p, *tokens):
    for tk in tokens:
        if tk is not None:
            p = p + tk[0, 0]
    return p


def local_step(x, tgt, wb, ws, arrive=lambda layer, event, after: {}, reduce=lambda group, event, gb, after: None):
    t, d = x.shape
    nh = ws["alog"].shape[0]
    ng = ws["w_s"].shape[0]
    assert nh == ng
    mix_w = (nh + ng) * HEAD_DIM

    (h0,) = rows_fwd(stage_norm, [ws["e_norm"]], [x], [(1, BF16, d)], name="f_norm_e")
    pm = mm_nt(h0, wb["w_main_t"], name="f_proj_main")
    pba = mm_nt(h0, wb["w_ba_t"], name="f_proj_ba")
    qkv = conv_fwd(pm, ws["conv_w"], mode="silu", name="f_conv4")
    cat, states, tinvs = delta_fwd(qkv, pm, pba, ws["alog"], ws["dtb"], ws["onorm"], mix_w, name="f_delta")
    tok = arrive("layer0", "landed", states).get("token")
    cat = gmlp_fwd(pm, _behind(ws["lng"], tok), ws["lnb"], ws["w_s"], ws["bs"], cat, name="f_gmlp")
    wb = {**wb, **arrive("layer0", "joined", cat)}
    y0 = mm_nn(cat, wb["w_out"], name="f_out")
    x1, h1 = rows_fwd(stage_res_norm, [ws["f_norm0"]], [x, y0], [(0, F32, d), (1, BF16, d)], name="f_norm_f0")
    a1, s1 = mm_nn(h1, wb["w1_0"], name="f_mlp0_up", epilogue="relu2")
    tok = arrive("layer1", "landed", s1).get("token")
    y1 = mm_nn(s1, wb["w2_0"], name="f_mlp0_down", behind=tok)
    x2, h2 = rows_fwd(stage_res_norm, [ws["o_norm"]], [x1, y1], [(0, F32, d), (1, BF16, d)], name="f_norm_o")
    wb = {**wb, **arrive("layer1", "joined", h2)}
    zz = mm_nn(h2, wb["pw1"], name="f_pw1")
    zparts = [(zz, d, 0), (zz, d, 1)]
    (gl,) = rows_fwd(stage_glu, [ws["b1a"], ws["b1b"]], zparts, [(0, F32, d)], name="f_glu")
    cv = conv_fwd(gl, ws["dw"], mode="plain", name="f_conv31")
    ln_params = [ws["dw_b"], ws["ln_g"], ws["ln_b"]]
    (sl,) = rows_fwd(stage_ln_silu, ln_params, [cv], [(0, BF16, d)], name="f_ln_silu")
    y2 = mm_nn(sl, wb["pw2"], name="f_pw2")
    x3, h3 = rows_fwd(stage_res_bias_norm, [ws["b2"], ws["f_norm1"]], [x2, y2], [(0, F32, d), (1, BF16, d)], name="f_norm_f1")
    a3, s3 = mm_nn(h3, wb["w1_1"], name="f_mlp1_up", epilogue="relu2")
    y3 = mm_nn(s3, wb["w2_1"], name="f_mlp1_down")
    loss, d4, d4b, g_final = loss_head(ws["final_norm"], x3, y3, tgt, name="loss_head")

    gb, gs = {}, {"final_norm": g_final}
    s_up = wb["w1_0"].shape[0]

    gb["w2_1"] = mm_tn(s3, d4b, groups=1, name="b_mlp1_down_w")
    dpre3 = mm_nt(d4b, wb["w2_1"], name="b_mlp1_down_x", mul=a3, out_dtype=BF16)
    gb["w1_1"] = mm_tn(h3, dpre3, groups=s_up, name="b_mlp1_up_w")
    tok = reduce("mlp1", "ready", gb, None)
    dh3 = mm_nt(dpre3, wb["w1_1"], name="b_mlp1_up_x", out_dtype=BF16)
    d3, d3b, gs["b2"], gs["f_norm1"] = rows_bwd(stage_res_bias_norm, [ws["b2"], _behind(ws["f_norm1"], tok)], [x2, y2], [d4, dh3],
                                                [(0, F32), (1, BF16)], name="b_norm_f1")
    tok = reduce("mlp1", "paired", gb, d3)
    gb["pw2"] = mm_tn(sl, d3b, groups=1, name="b_pw2_w")
    dsl = mm_nt(d3b, wb["pw2"], name="b_pw2_x", out_dtype=BF16)
    dcv, gs["dw_b"], gs["ln_g"], gs["ln_b"] = rows_bwd(stage_ln_silu, [_behind(ln_params[0], tok)] + ln_params[1:], [cv], [dsl],
                                                       [(0, F32)], name="b_ln_silu")
    dgl, gs["dw"] = conv_bwd(gl, ws["dw"], dcv, name="b_conv31")
    dzz, gs["b1a"], gs["b1b"] = rows_bwd(stage_glu, [ws["b1a"], ws["b1b"]], zparts, [dgl], [(0, BF16), (1, BF16)],
                                         name="b_glu", concat=True)
    gb["pw1"] = mm_tn(h2, dzz, groups=wb["pw1"].shape[0], name="b_pw1_w")
    tok = reduce("conv", "ready", gb, None)
    dh2 = mm_nt(dzz, wb["pw1"], name="b_pw1_x", out_dtype=BF16)
    d2, d2b, gs["o_norm"] = rows_bwd(stage_res_norm, [_behind(ws["o_norm"], tok)], [x1, y1], [d3, dh2], [(0, F32), (1, BF16)],
                                     name="b_norm_o")
    tok_conv = reduce("conv", "paired", gb, d2)
    gb["w2_0"] = mm_tn(s1, d2b, groups=1, name="b_mlp0_down_w")
    dpre1 = mm_nt(d2b, wb["w2_0"], name="b_mlp0_down_x", mul=a1, out_dtype=BF16)
    gb["w1_0"] = mm_tn(h1, dpre1, groups=s_up, name="b_mlp0_up_w")
    tok = reduce("mlp0", "ready", gb, None)
    dh1 = mm_nt(dpre1, wb["w1_0"], name="b_mlp0_up_x", out_dtype=BF16)
    d1, d1b, gs["f_norm0"] = rows_bwd(stage_res_norm, [_behind(ws["f_norm0"], tok_conv, tok)], [x, y0], [d2, dh1],
                                      [(0, F32), (1, BF16)], name="b_norm_f0")
    tok_mlp0 = reduce("mlp0", "paired", gb, d1)
    gb["w_out"] = mm_tn(cat, d1b, groups=1, name="b_out_w")
    tok = reduce("mixer", "ready", gb, None)
    dcat = mm_nt(d1b, wb["w_out"], name="b_out_x")
    dqkv_c, dpm, dpba, gs["alog"], gs["dtb"], gs["onorm"] = delta_bwd(
        qkv, pm, pba, ws["alog"], ws["dtb"], _behind(ws["onorm"], tok_mlp0, tok), states, tinvs, dcat, name="b_delta")
    tok = reduce("mixer", "paired", gb, dqkv_c)
    dpm, gs["lng"], gs["lnb"], gs["w_s"], gs["bs"] = gmlp_bwd(
        pm, ws["lng"], ws["lnb"], ws["w_s"], ws["bs"], dcat, dpm, name="b_gmlp")
    conv_w = _behind(ws["conv_w"], tok)
    dconv = conv_fwd(pm, conv_w, mode="silu_bwd", g=dqkv_c, name="b_conv4_silu")
    dpm, gs["conv_w"] = conv_bwd(pm, conv_w, dconv, name="b_conv4", into=dpm)
    gb["w_main_t"] = mm_tn(dpm, h0, groups=1, name="b_proj_main_w")
    gb["w_ba_t"] = mm_tn(dpba, h0, groups=1, name="b_proj_ba_w")
    dh0 = mm_nn(dpm, wb["w_main_t"], name="b_proj_main_x")
    dh0 = mm_nn(dpba, wb["w_ba_t"], name="b_proj_ba_x", add=dh0, out_dtype=BF16)
    grad_x, gs["e_norm"] = rows_bwd(stage_norm, [ws["e_norm"]], [x], [d1, dh0], [(0, F32)], name="b_norm_e")
    return loss, grad_x, gb, gs


WEIGHTS = ["e_norm", "e_w_in", "e_conv_w", "e_a_log", "e_dt_bias", "e_o_norm", "e_ln_g", "e_ln_b", "e_w_s", "e_b_s", "e_w_out",
           "o_norm", "o_pw1", "o_pw1_b", "o_dw", "o_dw_b", "o_ln_g", "o_ln_b", "o_pw2", "o_pw2_b", "f_norm", "f_w1", "f_w2",
           "final_norm"]
BIG = ["e_w_in", "e_w_out", "o_pw1", "o_pw2", "f_w1", "f_w2"]
SMALL_SHARDED = ["e_conv_w", "o_norm", "o_pw1_b", "o_dw", "o_dw_b", "o_ln_g", "o_ln_b", "o_pw2_b"]
SMALL = [n for n in WEIGHTS if n not in BIG]
LANES = 128
PACK_ROWS = 16
IN_ROW_MULT = 256
REDUCE_GROUPS = {"mlp1": ("w2_1", "w1_1"), "conv": ("pw2", "pw1"), "mlp0": ("w2_0", "w1_0"), "mixer": ("w_out",)}


def _pack(arrs):
    flat = jnp.concatenate([a.reshape(-1).astype(F32) for a in arrs])
    n = flat.shape[0]
    rows = -(-n // (LANES * PACK_ROWS)) * PACK_ROWS
    return jnp.pad(flat, (0, rows * LANES - n)).reshape(rows, LANES)


def _unpack(flat, shapes, lead=()):
    outs, off = [], 0
    for s in shapes:
        n = int(np.prod(s))
        outs.append(flat[..., off:off + n].reshape(lead + tuple(s)))
        off += n
    return outs


def kernel(x, e_norm, e_w_in, e_conv_w, e_a_log, e_dt_bias, e_o_norm, e_ln_g, e_ln_b, e_w_s, e_b_s, e_w_out, o_norm, o_pw1, o_pw1_b, o_dw, o_dw_b, o_ln_g, o_ln_b, o_pw2, o_pw2_b, f_norm, f_w1, f_w2, final_norm, loss_target, m_e_norm, m_e_w_in, m_e_conv_w, m_e_a_log, m_e_dt_bias, m_e_o_norm, m_e_ln_g, m_e_ln_b, m_e_w_s, m_e_b_s, m_e_w_out, m_o_norm, m_o_pw1, m_o_pw1_b, m_o_dw, m_o_dw_b, m_o_ln_g, m_o_ln_b, m_o_pw2, m_o_pw2_b, m_f_norm, m_f_w1, m_f_w2, m_final_norm, v_e_norm, v_e_w_in, v_e_conv_w, v_e_a_log, v_e_dt_bias, v_e_o_norm, v_e_ln_g, v_e_ln_b, v_e_w_s, v_e_b_s, v_e_w_out, v_o_norm, v_o_pw1, v_o_pw1_b, v_o_dw, v_o_dw_b, v_o_ln_g, v_o_ln_b, v_o_pw2, v_o_pw2_b, v_f_norm, v_f_w1, v_f_w2, v_final_norm):
    a = dict(locals())
    xi, yi, ci = _me()
    chip = 2 * xi + yi
    who = jnp.stack([ci, chip]).astype(jnp.int32)
    t, d = x.shape[1], x.shape[2]
    nh, ng = e_a_log.shape[1], e_w_s.shape[1]
    n_qkv, n_av, n_bw = 3 * nh * HEAD_DIM, nh * HEAD_DIM, ng * HEAD_DIM
    in_cols = n_qkv + n_av + 2 * nh + 2 * n_bw
    c_ba = n_qkv + n_av

    sh_in = in_cols // N_CHIPS
    pad_in = -(-sh_in // IN_ROW_MULT) * IN_ROW_MULT - sh_in
    w_in_t_local = jnp.pad(e_w_in[0].T.astype(BF16), ((0, pad_in), (0, 0)))
    small_local = [a[n] for n in SMALL_SHARDED]
    g_in, g_small = gather_shards([w_in_t_local, _pack(small_local)], name="gather_weights_in")
    travelling, token = gather_start([[w.astype(BF16) for w in (e_w_out[0], f_w1[0], f_w2[0])],
                                      [w.astype(BF16) for w in (o_pw1[0], o_pw2[0], f_w1[1], f_w2[1])]], name="gather_start")

    state = {"layer0": travelling[0], "layer1": travelling[1]}

    def arrive(layer, event, after):
        if event == "landed":
            state[layer] = gather_forward(state[layer], after, name="gather_forward_" + layer)
            return {"token": state[layer]["token"]}
        got = gather_wait(state[layer], after, name="gather_wait_" + layer)
        if layer == "layer0":
            g_out, g_w1, g_w2 = got
            return {"w_out": g_out.reshape(1, -1, d), "w1_0": g_w1, "w2_0": g_w2.reshape(1, -1, d)}
        g_pw1, g_pw2, g_w1, g_w2 = got
        return {"pw1": g_pw1, "pw2": g_pw2.reshape(1, -1, d), "w1_1": g_w1, "w2_1": g_w2.reshape(1, -1, d)}

    def in_rows(lo, hi):
        parts = []
        for s in range(N_CHIPS):
            a0, a1 = max(lo, s * sh_in), min(hi, (s + 1) * sh_in)
            if a0 < a1:
                parts.append(g_in[s, a0 - s * sh_in:a1 - s * sh_in])
        return parts

    wb = {
        "w_main_t": jnp.concatenate(in_rows(0, c_ba) + in_rows(c_ba + 2 * nh, in_cols), axis=0)[None],
        "w_ba_t": jnp.pad(jnp.concatenate(in_rows(c_ba, c_ba + 2 * nh), axis=0), ((0, LANES - 2 * nh), (0, 0)))[None],
    }
    pieces = _unpack(g_small.reshape(N_CHIPS, -1), [w.shape for w in small_local], lead=(N_CHIPS,))
    full = {n: jnp.moveaxis(p, 0, -2).reshape(p.shape[1:-1] + (N_CHIPS * p.shape[-1],)) for n, p in zip(SMALL_SHARDED, pieces)}
    ws = {
        "e_norm": e_norm + token[0, 0],
        "conv_w": full["e_conv_w"][0], "alog": e_a_log.reshape(nh, 1, 1), "dtb": e_dt_bias.reshape(nh, 1, 1),
        "onorm": e_o_norm, "lng": e_ln_g.reshape(ng, 1, HEAD_DIM), "lnb": e_ln_b.reshape(ng, 1, HEAD_DIM), "w_s": e_w_s[0],
        "bs": e_b_s.reshape(ng, HEAD_DIM, 1), "o_norm": full["o_norm"], "b1a": full["o_pw1_b"][:, :d], "b1b": full["o_pw1_b"][:, d:],
        "dw": full["o_dw"][0], "dw_b": full["o_dw_b"], "ln_g": full["o_ln_g"], "ln_b": full["o_ln_b"], "b2": full["o_pw2_b"],
        "f_norm0": f_norm[0:1], "f_norm1": f_norm[1:2], "final_norm": final_norm.reshape(1, d),
    }

    pending = {}

    def reduce(group, event, gb, after):
        names = REDUCE_GROUPS[group]
        if event == "ready":
            arrs = [gb[nme] if gb[nme].shape[0] == N_CHIPS else gb[nme].reshape(N_CHIPS, -1, d) for nme in names]
            pending[group] = split_start(_pair_copies, arrs, [(N_CHIPS, g.shape[1] // 2, g.shape[2]) for g in arrs], len(arrs),
                                         name="reduce_sibling_start_" + group)
            return pending[group]["token"]
        mine, theirs = split_wait(pending[group], after, name="reduce_sibling_wait_" + group)
        pairs = [pre_reduce(gd, sb, who, name="reduce_pair_" + nme) for nme, gd, sb in zip(names, mine, theirs)]
        sums = [p[0] for p in pairs]
        pending[group] = {"own": [p[1] for p in pairs],
                          "scatter": split_start(_scatter_copies, sums, [(3,) + s.shape[1:] for s in sums], 3 * len(sums),
                                                 name="reduce_chips_start_" + group)}
        return pending[group]["scatter"]["token"]

    loss, grad_x, gb, gs = local_step(x[0], loss_target[0], wb, ws, arrive, reduce)

    gm, gba = gb["w_main_t"][0], gb["w_ba_t"][0]
    g_in_t = jnp.concatenate([gm[:c_ba], gba[:2 * nh], gm[c_ba:]], axis=0).reshape(N_CHIPS, sh_in, d)
    g_in_padded = jnp.pad(g_in_t, ((0, 0), (0, pad_in), (0, 0)))
    in_pair = split_start(_pair_copies, [g_in_padded], [(N_CHIPS, (sh_in + pad_in) // 2, d)], 1, name="reduce_sibling_start_w_in")
    small_global = {
        "e_norm": gs["e_norm"], "e_conv_w": gs["conv_w"][None], "e_a_log": gs["alog"].reshape(1, nh), "e_dt_bias": gs["dtb"].reshape(1, nh),
        "e_o_norm": gs["onorm"], "e_ln_g": gs["lng"].reshape(1, n_bw), "e_ln_b": gs["lnb"].reshape(1, n_bw), "e_w_s": gs["w_s"][None],
        "e_b_s": gs["bs"].reshape(1, ng, HEAD_DIM), "o_norm": gs["o_norm"], "o_pw1_b": jnp.concatenate([gs["b1a"], gs["b1b"]], axis=1),
        "o_dw": gs["dw"][None], "o_dw_b": gs["dw_b"], "o_ln_g": gs["ln_g"], "o_ln_b": gs["ln_b"], "o_pw2_b": gs["b2"],
        "f_norm": jnp.concatenate([gs["f_norm0"], gs["f_norm1"]], axis=0), "final_norm": gs["final_norm"].reshape(d),
    }
    packed = _pack([small_global[n] for n in SMALL] + [loss])
    everyone = split_start(_everyone_copies, [packed], [(8,) + packed.shape], 7, name="reduce_small_start")

    who_then = who + (in_pair["token"][0, 0] + everyone["token"][0, 0]).astype(jnp.int32)
    half = {}
    for group, names in REDUCE_GROUPS.items():
        _, landed = split_wait(pending[group]["scatter"], grad_x, name="reduce_chips_wait_" + group)
        for nme, ow, rc in zip(names, pending[group]["own"], landed):
            half[nme] = final_reduce(ow, rc, who_then, name="reduce_sum_" + nme)
    r_out, r_pw1, r_pw2, r_w1_0, r_w1_1, r_w2_0, r_w2_1 = sibling_join_halves(
        [half[nme] for nme in ("w_out", "pw1", "pw2", "w1_0", "w1_1", "w2_0", "w2_1")], name="reduce_join")
    big_grads = {"e_w_out": [r_out], "o_pw1": [r_pw1], "o_pw2": [r_pw2], "f_w1": [r_w1_0, r_w1_1], "f_w2": [r_w2_0, r_w2_1]}

    mine, theirs = split_wait(in_pair, r_out, name="reduce_sibling_wait_w_in")
    in_sum, in_own = pre_reduce(mine[0], theirs[0], who, name="reduce_pair_w_in")
    in_scatter = split_start(_scatter_copies, [in_sum], [(3,) + in_sum.shape[1:]], 3, name="reduce_chips_start_w_in")
    out = {}
    prev = in_scatter["token"]
    for n in BIG:
        if n != "e_w_in":
            out[n] = adamw(a[n], big_grads[n], a["m_" + n], a["v_" + n], name="adamw_" + n, behind=prev)
            prev = out[n][1][0, :SUBLANES, :LANES]

    own_packed, landed = split_wait(everyone, prev, name="reduce_small_wait")
    summed = sum8(landed[0], own_packed[0], (4 * xi + 2 * yi + ci).astype(jnp.int32).reshape(1), name="reduce_small_sum")
    *small_full, total = _unpack(summed.reshape(-1), [small_global[n].shape for n in SMALL] + [()])
    small_grads = {}
    for n, g in zip(SMALL, small_full):
        if n in SMALL_SHARDED:
            width = a[n].shape[-1]
            g = lax.dynamic_slice_in_dim(g, chip * width, width, axis=g.ndim - 1)
        small_grads[n] = g
    sw, sm, sv, sg = (_pack([src[n] for n in SMALL])[None] for src in
                      ({n: a[n] for n in SMALL}, {n: a["m_" + n] for n in SMALL}, {n: a["v_" + n] for n in SMALL}, small_grads))
    res = adamw(sw, [sg[0]], sm, sv, name="adamw_small")

    _, landed = split_wait(in_scatter, res[1], name="reduce_chips_wait_w_in")
    (r_in,) = sibling_join_halves([final_reduce(in_own, landed[0], who, name="reduce_sum_w_in")], name="reduce_join_w_in")
    w_in_res = adamw(e_w_in.transpose(0, 2, 1), [r_in[:sh_in]], m_e_w_in.transpose(0, 2, 1), v_e_w_in.transpose(0, 2, 1),
                     name="adamw_e_w_in")
    out["e_w_in"] = tuple(r.transpose(0, 2, 1) for r in w_in_res)
    shapes = [a[n].shape for n in SMALL]
    unpacked = [_unpack(r.reshape(-1), shapes) for r in res]
    for i, n in enumerate(SMALL):
        out[n] = tuple(u[i] for u in unpacked)

    result = [total, grad_x[None]]
    for k in range(4):
        result += [out[n][k] for n in WEIGHTS]
    return tuple(result)
```

```python
import functools
import math

import jax
import jax.numpy as jnp
import numpy as np
from jax import lax
from jax.experimental import pallas as pl
from jax.experimental.pallas import tpu as pltpu

F32 = jnp.float32
BF16 = jnp.bfloat16

EPS = 1e-6
CHUNK = 64
PAIR = 2 * CHUNK
HEAD_DIM = 128
N_CHIPS = 4
ADAM_LR, ADAM_B1, ADAM_B2, ADAM_EPS, ADAM_WD, ADAM_STEP = 0.001, 0.9, 0.999, 1e-08, 0.01, 10

VMEM_LIMIT = 56 * 1024 * 1024


def _cparams(sem=None):
    return pltpu.CompilerParams(dimension_semantics=sem, vmem_limit_bytes=VMEM_LIMIT)


def _pick(n, prefs):
    for p in prefs:
        if n % p == 0:
            return p
    return n


def mm_nn(a, w, *, name, epilogue=None, add=None, out_dtype=F32, behind=None):
    m, k = a.shape
    s, _, ns = w.shape
    n = s * ns
    tm = _pick(m, (1024, 512, 256, 128))
    tn = _pick(ns, (1024, 512, 256, 128))
    tk = _pick(k, (2048, 1024, 512, 256, 128))
    nk = k // tk
    npb = ns // tn
    assert add is None or epilogue is None

    def body(a_ref, w_ref, *rest):
        if add is not None:
            add_ref, rest = rest[0], rest[1:]
        if behind is not None:
            rest = rest[1:]
        if epilogue == "relu2":
            o1_ref, o2_ref = rest[0], rest[1]
            acc_ref = rest[2] if nk > 1 else None
        else:
            o1_ref = rest[0]
            acc_ref = rest[1] if nk > 1 else None
        def finish(c, rows=slice(None)):
            if epilogue == "relu2":
                r = jnp.maximum(c, 0.0)
                o1_ref[rows, :] = r.astype(o1_ref.dtype)
                o2_ref[rows, :] = (r * r).astype(o2_ref.dtype)
            elif add is not None:
                o1_ref[rows, :] = (c + add_ref[rows, :].astype(F32)).astype(o1_ref.dtype)
            else:
                o1_ref[rows, :] = c.astype(o1_ref.dtype)

        part = jnp.dot(a_ref[...], w_ref[...], preferred_element_type=F32)
        if nk == 1:
            finish(part)
        else:
            kk = pl.program_id(2)

            @pl.when(kk == 0)
            def _():
                acc_ref[...] = part

            @pl.when(kk > 0)
            def _():
                acc_ref[...] += part

            @pl.when(kk == nk - 1)
            def _():
                finish(acc_ref[...])

    o_spec = pl.BlockSpec((tm, tn), lambda i, j, kk: (i, j))
    if epilogue == "relu2":
        out_shape = (jax.ShapeDtypeStruct((m, n), BF16), jax.ShapeDtypeStruct((m, n), BF16))
        out_specs = (o_spec, o_spec)
    else:
        out_shape = jax.ShapeDtypeStruct((m, n), out_dtype)
        out_specs = o_spec
    return pl.pallas_call(
        body, name=name, out_shape=out_shape,
        grid=(m // tm, n // tn, nk),
        in_specs=[pl.BlockSpec((tm, tk), lambda i, j, kk: (i, kk)),
                  pl.BlockSpec((None, tk, tn), lambda i, j, kk: (j // npb, kk, j % npb))] + ([o_spec] if add is not None else [])
        + ([pl.BlockSpec(behind.shape, lambda i, j, kk: (0, 0))] if behind is not None else []),
        out_specs=out_specs,
        scratch_shapes=[pltpu.VMEM((tm, tn), F32)] if nk > 1 else [],
        compiler_params=_cparams(("parallel", "parallel", "arbitrary")),
    )(*([a, w] + ([add] if add is not None else []) + ([behind] if behind is not None else [])))


def mm_nt(a, w, *, name, mul=None, add=None, out_dtype=F32):
    assert mul is None or add is None
    if add is not None:
        mul = add
    m, n = a.shape
    s, k, ns = w.shape
    assert n == s * ns
    tm = _pick(m, (1024, 512, 256, 128))
    tko = _pick(k, (1024, 512, 256, 128))
    tn = _pick(ns, (2048, 1024, 512, 256, 128))
    nn = n // tn
    npb = ns // tn

    def body(a_ref, w_ref, *rest):
        if mul is not None:
            m_ref, o_ref = rest[0], rest[1]
            acc_ref = rest[2] if nn > 1 else None
        else:
            m_ref, o_ref = None, rest[0]
            acc_ref = rest[1] if nn > 1 else None
        part = lax.dot_general(a_ref[...], w_ref[...], (((1,), (1,)), ((), ())), preferred_element_type=F32)

        def finish(c):
            if add is not None:
                c = c + m_ref[...].astype(F32)
            elif m_ref is not None:
                c = c * (2.0 * m_ref[...].astype(F32))
            o_ref[...] = c.astype(o_ref.dtype)

        if nn == 1:
            finish(part)
        else:
            kk = pl.program_id(2)

            @pl.when(kk == 0)
            def _():
                acc_ref[...] = part

            @pl.when(kk > 0)
            def _():
                acc_ref[...] += part

            @pl.when(kk == nn - 1)
            def _():
                finish(acc_ref[...])

    in_specs = [pl.BlockSpec((tm, tn), lambda i, j, kk: (i, kk)),
                pl.BlockSpec((None, tko, tn), lambda i, j, kk: (kk // npb, j, kk % npb))]
    args = [a, w]
    if mul is not None:
        in_specs.append(pl.BlockSpec((tm, tko), lambda i, j, kk: (i, j)))
        args.append(mul)
    return pl.pallas_call(
        body, name=name, out_shape=jax.ShapeDtypeStruct((m, k), out_dtype),
        grid=(m // tm, k // tko, nn),
        in_specs=in_specs,
        out_specs=pl.BlockSpec((tm, tko), lambda i, j, kk: (i, j)),
        scratch_shapes=[pltpu.VMEM((tm, tko), F32)] if nn > 1 else [],
        compiler_params=_cparams(("parallel", "parallel", "arbitrary")),
    )(*args)


def mm_tn(a, b, *, groups, name):
    m, k = a.shape
    _, n = b.shape
    ns = n // groups
    tko = _pick(k, (1024, 512, 256, 128))
    tn = _pick(ns, (1024, 512, 256, 128))
    tc = _pick(m, (2048, 1024, 512, 256, 128))
    nc = m // tc
    npb = ns // tn

    def body(a_ref, b_ref, o_ref):
        part = lax.dot_general(a_ref[...], b_ref[...], (((0,), (0,)), ((), ())), preferred_element_type=F32)
        kk = pl.program_id(2)

        @pl.when(kk == 0)
        def _():
            o_ref[...] = part

        @pl.when(kk > 0)
        def _():
            o_ref[...] += part

    return pl.pallas_call(
        body, name=name, out_shape=jax.ShapeDtypeStruct((groups, k, ns), F32),
        grid=(k // tko, n // tn, nc),
        in_specs=[pl.BlockSpec((tc, tko), lambda i, j, kk: (kk, i)),
                  pl.BlockSpec((tc, tn), lambda i, j, kk: (kk, j))],
        out_specs=pl.BlockSpec((None, tko, tn), lambda i, j, kk: (j // npb, i, j % npb)),
        compiler_params=_cparams(("parallel", "parallel", "arbitrary")),
    )(a, b)


ROWS = 128


def _full_spec(arr):
    nd = arr.ndim
    return pl.BlockSpec(arr.shape, lambda i, _nd=nd: (0,) * _nd)


def _row_spec(x, rows):
    if isinstance(x, tuple):
        _, w, cb = x
        return pl.BlockSpec((rows, w), lambda i, _cb=cb: (i, _cb))
    return pl.BlockSpec((rows, x.shape[1]), lambda i: (i, 0))


def _arr(x):
    return x[0] if isinstance(x, tuple) else x


def _width(x):
    return x[1] if isinstance(x, tuple) else x.shape[1]


def rows_fwd(fn, params, xs, stores, *, name, rows=ROWS):
    t = _arr(xs[0]).shape[0]
    np_, nx = len(params), len(xs)

    def body(*refs):
        p_refs, x_refs, o_refs = refs[:np_], refs[np_:np_ + nx], refs[np_ + nx:]
        outs = fn(*[r[...].astype(F32) for r in p_refs], *[r[...].astype(F32) for r in x_refs])
        for (idx, dt, _), o_ref in zip(stores, o_refs):
            o_ref[...] = outs[idx].astype(dt)

    res = pl.pallas_call(
        body, name=name,
        out_shape=tuple(jax.ShapeDtypeStruct((t, w), dt) for _, dt, w in stores),
        grid=(t // rows,),
        in_specs=[_full_spec(p) for p in params] + [_row_spec(x, rows) for x in xs],
        out_specs=tuple(pl.BlockSpec((rows, w), lambda i: (i, 0)) for _, _, w in stores),
        compiler_params=_cparams(("parallel",)),
    )(*params, *[_arr(x) for x in xs])
    return res


def rows_bwd(fn, params, xs, cts, dx_stores, *, name, rows=ROWS, concat=False):
    t = _arr(xs[0]).shape[0]
    np_, nx = len(params), len(xs)
    ct_idx = [i for i, c in enumerate(cts) if c is not None]
    ct_arrs = [cts[i] for i in ct_idx]
    nct = len(ct_arrs)
    nds = 1 if concat else len(dx_stores)
    widths = [_width(xs[xi]) for xi, _ in dx_stores]

    def body(*refs):
        p_refs = refs[:np_]
        x_refs = refs[np_:np_ + nx]
        c_refs = refs[np_ + nx:np_ + nx + nct]
        d_refs = refs[np_ + nx + nct:np_ + nx + nct + nds]
        g_refs = refs[np_ + nx + nct + nds:]
        pv = [r[...].astype(F32) for r in p_refs]
        xv = [r[...].astype(F32) for r in x_refs]
        outs, vjp = jax.vjp(lambda *a: tuple(fn(*a)), *pv, *xv)
        ct_full = [jnp.zeros_like(o) for o in outs]
        for i, r in zip(ct_idx, c_refs):
            ct_full[i] = r[...].astype(F32)
        grads = vjp(tuple(ct_full))
        if concat:
            off = 0
            for (xi, dt), wd in zip(dx_stores, widths):
                d_refs[0][:, off:off + wd] = grads[np_ + xi].astype(dt)
                off += wd
        else:
            for (xi, dt), d_ref in zip(dx_stores, d_refs):
                d_ref[...] = grads[np_ + xi].astype(dt)
        step = pl.program_id(0)
        for j, g_ref in enumerate(g_refs):
            @pl.when(step == 0)
            def _(g_ref=g_ref, j=j):
                g_ref[...] = grads[j]

            @pl.when(step > 0)
            def _(g_ref=g_ref, j=j):
                g_ref[...] += grads[j]

    dx_shapes = [(sum(widths), dx_stores[0][1])] if concat else [(wd, dt) for wd, (_, dt) in zip(widths, dx_stores)]
    out_shape = tuple(jax.ShapeDtypeStruct((t, wd), dt) for wd, dt in dx_shapes) + \
        tuple(jax.ShapeDtypeStruct(p.shape, F32) for p in params)
    out_specs = tuple(pl.BlockSpec((rows, wd), lambda i: (i, 0)) for wd, _ in dx_shapes) + \
        tuple(_full_spec(p) for p in params)
    return pl.pallas_call(
        body, name=name, out_shape=out_shape, grid=(t // rows,),
        in_specs=[_full_spec(p) for p in params] + [_row_spec(x, rows) for x in xs] + [_row_spec(c, rows) for c in ct_arrs],
        out_specs=out_specs,
        compiler_params=_cparams(("arbitrary",)),
    )(*params, *[_arr(x) for x in xs], *[_arr(c) for c in ct_arrs])


def _rms(x, g):
    return x * lax.rsqrt(jnp.mean(x * x, axis=-1, keepdims=True) + EPS) * g


def _sigmoid(x):
    return 1.0 / (1.0 + jnp.exp(-x))


def _silu(x):
    return x * _sigmoid(x)


def _gelu(x):
    return 0.5 * x * (1.0 + lax.erf(x * (1.0 / math.sqrt(2.0))))


def stage_norm(g, x):
    return x, _rms(x, g)


def stage_res_norm(g, x, y):
    xn = x + y
    return xn, _rms(xn, g)


def stage_res_bias_norm(b, g, x, y):
    xn = x + y + b
    return xn, _rms(xn, g)


def stage_glu(ba, bb, za, zb):
    return ((za + ba) * _sigmoid(zb + bb),)


def stage_ln_silu(dw_b, ln_g, ln_b, cv):
    z = cv + dw_b
    mu = jnp.mean(z, axis=-1, keepdims=True)
    zc = z - mu
    y = zc * lax.rsqrt(jnp.mean(zc * zc, axis=-1, keepdims=True) + EPS) * ln_g + ln_b
    return (_silu(y),)


CONV_ROWS = 128
CONV_COLS = 256
SUBLANES = 8


def _halo(k):
    return SUBLANES * ((k - 1 + SUBLANES - 1) // SUBLANES)


def _taps_by_roll(k):
    out = {}
    for s in range(k):
        out.setdefault(s % SUBLANES, []).append((s // SUBLANES, s))
    return out


def _shifted_down(win, k, r):
    halo, n = _halo(k), win.shape[0]
    segs = {}
    for b, lst in _taps_by_roll(k).items():
        rolled = win if b == 0 else pltpu.roll(win, b, axis=0)
        for a, s in lst:
            segs[s] = rolled[halo - SUBLANES * a: halo - SUBLANES * a + r]
    return segs


def _shifted_up(win, k, r):
    n = win.shape[0]
    segs = {}
    for b, lst in _taps_by_roll(k).items():
        rolled = win if b == 0 else pltpu.roll(win, n - b, axis=0)
        for a, s in lst:
            segs[s] = rolled[SUBLANES * a: SUBLANES * a + r]
    return segs


def _for_blocks(nblk, fn):
    fn(0, True, nblk == 1)
    if nblk > 2:
        def step(i, c):
            fn(i, False, False)
            return c
        lax.fori_loop(1, nblk - 1, step, 0)
    if nblk > 1:
        fn(nblk - 1, False, True)


def _base(i, r):
    return i * r if isinstance(i, int) else pl.multiple_of(i * r, r)


def _win_top(ref, i, first, r, halo):
    if first:
        return jnp.concatenate([jnp.zeros((halo, ref.shape[1]), F32), ref[pl.ds(0, r), :]], axis=0)
    base = _base(i, r)
    return ref[pl.ds(base - halo, r + halo), :]


def _win_bottom(ref, i, last, r, halo):
    base = _base(i, r)
    if last:
        return jnp.concatenate([ref[pl.ds(base, r), :], jnp.zeros((halo, ref.shape[1]), F32)], axis=0)
    return ref[pl.ds(base, r + halo), :]


def conv_fwd(x, w, *, mode, name, g=None):
    t = x.shape[0]
    k, c = w.shape
    r, cb, halo = min(CONV_ROWS, t), min(CONV_COLS, c), _halo(k)
    nblk = t // r

    def body(*refs):
        if mode == "silu_bwd":
            x_ref, w_ref, g_ref, o_ref = refs
        else:
            x_ref, w_ref, o_ref = refs

        def blk(i, first, last):
            segs = _shifted_down(_win_top(x_ref, i, first, r, halo), k, r)
            acc = None
            for s in range(k):
                term = w_ref[pl.ds(k - 1 - s, 1), :] * segs[s]
                acc = term if acc is None else acc + term
            base = _base(i, r)
            if mode == "silu":
                acc = _silu(acc)
            elif mode == "silu_bwd":
                sg = _sigmoid(acc)
                acc = g_ref[pl.ds(base, r), :] * (sg * (1.0 + acc * (1.0 - sg)))
            o_ref[pl.ds(base, r), :] = acc

        _for_blocks(nblk, blk)

    col = pl.BlockSpec((t, cb), lambda j: (0, j))
    in_specs = [col, pl.BlockSpec((k, cb), lambda j: (0, j))] + ([col] if mode == "silu_bwd" else [])
    args = [x, w] + ([g] if mode == "silu_bwd" else [])
    return pl.pallas_call(
        body, name=name, out_shape=jax.ShapeDtypeStruct((t, c), F32), grid=(c // cb,),
        in_specs=in_specs, out_specs=col, compiler_params=_cparams(("parallel",)),
    )(*args)


def conv_bwd(x, w, dy, *, name, into=None):
    t = x.shape[0]
    k, c = w.shape
    r, cb, halo = min(CONV_ROWS, t), min(CONV_COLS, c), _halo(k)
    nblk = t // r

    def body(x_ref, w_ref, dy_ref, *rest):
        dx_ref, dw_ref = rest[-2], rest[-1]
        dw_ref[...] = jnp.zeros_like(dw_ref)

        def blk(i, first, last):
            base = _base(i, r)
            up = _shifted_up(_win_bottom(dy_ref, i, last, r, halo), k, r)
            down = _shifted_down(_win_top(x_ref, i, first, r, halo), k, r)
            dyb = up[0]
            acc = None
            for s in range(k):
                term = w_ref[pl.ds(k - 1 - s, 1), :] * up[s]
                acc = term if acc is None else acc + term
                dw_ref[pl.ds(k - 1 - s, 1), :] += jnp.sum(down[s] * dyb, axis=0, keepdims=True)
            dx_ref[pl.ds(base, r), :] = acc.astype(dx_ref.dtype)

        _for_blocks(nblk, blk)

    col = pl.BlockSpec((t, cb), lambda j: (0, j))
    wsp = pl.BlockSpec((k, cb), lambda j: (0, j))
    dx_shape = jax.ShapeDtypeStruct((t, c), F32) if into is None else jax.ShapeDtypeStruct(into.shape, into.dtype)
    return pl.pallas_call(
        body, name=name,
        out_shape=(dx_shape, jax.ShapeDtypeStruct((k, c), F32)), grid=(c // cb,),
        in_specs=[col, wsp, col] + ([] if into is None else [pl.BlockSpec(memory_space=pl.ANY)]),
        out_specs=(col, wsp),
        input_output_aliases={} if into is None else {3: 0},
        compiler_params=_cparams(("parallel",)),
    )(*([x, w, dy] + ([] if into is None else [into])))


_DIMS = {"nn": (((1,), (0,)), ((), ())), "nt": (((1,), (1,)), ((), ())), "tn": (((0,), (0,)), ((), ()))}
_DIMS_BATCHED = {"nn": (((2,), (1,)), ((0,), (0,))), "nt": (((2,), (2,)), ((0,), (0,))), "tn": (((1,), (1,)), ((0,), (0,)))}


def _mxu(a, b, mode):
    dims = _DIMS_BATCHED if a.ndim == 3 else _DIMS
    return lax.dot_general(a, b, dims[mode], preferred_element_type=F32)


def _split(x):
    hi = x.astype(BF16)
    return hi, (x - hi.astype(F32)).astype(BF16)


def _dot_raw(a, b, mode, prec):
    if prec == "bf16":
        return _mxu(a.astype(BF16), b.astype(BF16), mode)
    if prec == "x3":
        ah, al = _split(a)
        bh, bl = _split(b)
        return _mxu(ah, bh, mode) + (_mxu(ah, bl, mode) + _mxu(al, bh, mode))
    if prec == "x3r":
        bh, bm = _split(b)
        bl = (b - bh.astype(F32) - bm.astype(F32)).astype(BF16)
        ah = a.astype(BF16)
        return _mxu(ah, bh, mode) + (_mxu(ah, bm, mode) + _mxu(ah, bl, mode))
    raise ValueError(prec)


@functools.lru_cache(maxsize=None)
def _dot_fn(mode, prec):
    bprec = "x3" if prec == "x3r" else prec

    @jax.custom_vjp
    def f(a, b):
        return _dot_raw(a, b, mode, prec)

    def fwd(a, b):
        return _dot_raw(a, b, mode, prec), (a, b)

    def bwd(res, ct):
        a, b = res
        if mode == "nn":
            return _dot_raw(ct, b, "nt", bprec), _dot_raw(a, ct, "tn", bprec)
        if mode == "nt":
            return _dot_raw(ct, b, "nn", bprec), _dot_raw(ct, a, "tn", bprec)
        return _dot_raw(b, ct, "nt", bprec), _dot_raw(a, ct, "nn", bprec)

    f.defvjp(fwd, bwd)
    return f


def _dot(a, b, mode="nn", prec="bf16"):
    return _dot_fn(mode, prec)(a, b)


def _inv_product(l):
    n = l.shape[-1]
    eye = (lax.broadcasted_iota(jnp.int32, (n, n), 0) == lax.broadcasted_iota(jnp.int32, (n, n), 1)).astype(F32)
    p = eye - l
    pw = l
    for _ in range(5):
        pw = _dot_raw(pw, pw, "nn", "x3")
        p = _dot_raw(p, eye + pw, "nn", "x3")
    return p


@jax.custom_vjp
def _inv_unit_lower(l, t_saved):
    return t_saved


def _inv_fwd(l, t_saved):
    return t_saved, t_saved


def _inv_bwd(t, ct):
    tmp = _dot_raw(t, ct, "tn", "x3")
    return -_dot_raw(tmp, t, "nt", "x3"), jnp.zeros_like(t)


_inv_unit_lower.defvjp(_inv_fwd, _inv_bwd)


def _softplus(x):
    pos = x > 0
    return jnp.where(pos, x, 0.0) + jnp.log(1.0 + jnp.exp(jnp.where(pos, -x, x)))


def _l2n(x):
    return x * lax.rsqrt(jnp.sum(x * x, axis=-1, keepdims=True) + EPS)


def delta_pair(s0, qc, kc, vc, z, pba, alog, dtb, onorm, t_saved=None):
    n = PAIR
    nh = qc.shape[0]
    assert qc.shape == (nh, n, HEAD_DIM) and n == HEAD_DIM
    hi = lax.broadcasted_iota(jnp.int32, (nh, 1, pba.shape[1]), 0)
    li = lax.broadcasted_iota(jnp.int32, (nh, 1, pba.shape[1]), 2)
    braw = jnp.sum(pba[None] * (li == hi).astype(F32), axis=2, keepdims=True)
    araw = jnp.sum(pba[None] * (li == hi + nh).astype(F32), axis=2, keepdims=True)
    ri = lax.broadcasted_iota(jnp.int32, (n, n), 0)
    ci = lax.broadcasted_iota(jnp.int32, (n, n), 1)
    same = (ri // CHUNK) == (ci // CHUNK)
    tri = same & (ci <= ri)
    same_f = jnp.broadcast_to(same.astype(F32), (nh, n, n))
    tri_f = jnp.broadcast_to(tri.astype(F32), (nh, n, n))
    strict_f = (same & (ci < ri)).astype(F32)
    m0 = (lax.broadcasted_iota(jnp.int32, (n, 1), 0) < CHUNK).astype(F32)
    m1 = 1.0 - m0

    q = _l2n(qc) * (HEAD_DIM ** -0.5)
    k = _l2n(kc)
    beta = _sigmoid(braw)
    g = -jnp.exp(alog) * _softplus(araw + dtb)
    gb = jnp.broadcast_to(g, (nh, n, n))
    gc = _dot(tri_f, gb, "nn", "x3r")
    gtot = _dot(same_f, gb, "nn", "x3r")
    decay = jnp.exp(jnp.where(tri, gc - jnp.swapaxes(gc, 1, 2), -1e30))
    eg = jnp.exp(gc)
    kb, vb = k * beta, vc * beta
    l = _dot(kb, k, "nt") * decay * strict_f
    if t_saved is None:
        tinv = _inv_product(l)
    else:
        tinv = _inv_unit_lower(l, t_saved)
    u = _dot(tinv, vb, "nn", "x3")
    w = _dot(tinv, kb * eg, "nn", "x3")
    attn = _dot(q, k, "nt") * decay
    q_dec = q * eg
    k_tail = k * jnp.exp(gtot - gc)
    gl0 = jnp.exp(jnp.sum(gb * m0, axis=1, keepdims=True))
    gl1 = jnp.exp(jnp.sum(gb * m1, axis=1, keepdims=True))

    vn0 = m0 * (u - _dot(w, s0))
    s1 = s0 * gl0 + _dot(k_tail, vn0, "tn")
    vn1 = m1 * (u - _dot(w, s1))
    o = m0 * _dot(q_dec, s0) + m1 * _dot(q_dec, s1) + _dot(attn, vn0 + vn1)
    s2 = s1 * gl1 + _dot(k_tail, vn1, "tn")

    on = o * lax.rsqrt(jnp.mean(o * o, axis=-1, keepdims=True) + EPS) * onorm
    return on * _silu(z), s2, tinv


def _hcols(i):
    return slice(i * HEAD_DIM, (i + 1) * HEAD_DIM)


def _heads(ref, first, count):
    return jnp.stack([ref[:, _hcols(first + i)] for i in range(count)])


def delta_fwd(qkv, pm, pba, alog, dtb, onorm, cat_width, *, name):
    t = qkv.shape[0]
    h = alog.shape[0]
    hd = h * HEAD_DIM
    npair = t // PAIR
    mat = pl.BlockSpec((h, None, PAIR, HEAD_DIM), lambda p: (0, p, 0, 0))
    par = pl.BlockSpec((h, 1, 1), lambda p: (0, 0, 0))

    def body(qkv_ref, z_ref, pba_ref, al_ref, dt_ref, on_ref, o_ref, st_ref, ti_ref, s_scr):
        @pl.when(pl.program_id(0) == 0)
        def _():
            s_scr[...] = jnp.zeros_like(s_scr)

        s0 = s_scr[...]
        st_ref[...] = s0
        out, s2, tinv = delta_pair(s0, _heads(qkv_ref, 0, h), _heads(qkv_ref, h, h), _heads(qkv_ref, 2 * h, h), _heads(z_ref, 0, h),
                                   pba_ref[...], al_ref[...], dt_ref[...], on_ref[...])
        for i in range(h):
            o_ref[:, _hcols(i)] = out[i].astype(o_ref.dtype)
        ti_ref[...] = tinv
        s_scr[...] = s2

    return pl.pallas_call(
        body, name=name, grid=(npair,),
        out_shape=(jax.ShapeDtypeStruct((t, cat_width), BF16),
                   jax.ShapeDtypeStruct((h, npair, PAIR, HEAD_DIM), F32),
                   jax.ShapeDtypeStruct((h, npair, PAIR, PAIR), F32)),
        in_specs=[pl.BlockSpec((PAIR, 3 * hd), lambda p: (p, 0)), pl.BlockSpec((PAIR, hd), lambda p: (p, 3)),
                  pl.BlockSpec((PAIR, pba.shape[1]), lambda p: (p, 0)), par, par,
                  pl.BlockSpec((1, HEAD_DIM), lambda p: (0, 0))],
        out_specs=(pl.BlockSpec((PAIR, hd), lambda p: (p, 0)), mat, mat),
        scratch_shapes=[pltpu.VMEM((h, HEAD_DIM, HEAD_DIM), F32)],
        compiler_params=_cparams(("arbitrary",)),
    )(qkv, pm, pba, alog, dtb, onorm)


def delta_bwd(qkv, pm, pba, alog, dtb, onorm, states, tinvs, dcat, *, name):
    t = qkv.shape[0]
    h = alog.shape[0]
    hd = h * HEAD_DIM
    npair = t // PAIR
    rev = lambda p: npair - 1 - p
    mat = pl.BlockSpec((h, None, PAIR, HEAD_DIM), lambda p: (0, rev(p), 0, 0))
    par = pl.BlockSpec((h, 1, 1), lambda p: (0, 0, 0))
    onsp = pl.BlockSpec((1, HEAD_DIM), lambda p: (0, 0))
    wide = pl.BlockSpec((PAIR, 3 * hd), lambda p: (rev(p), 0))
    zsp = pl.BlockSpec((PAIR, hd), lambda p: (rev(p), 3))
    bsp = pl.BlockSpec((PAIR, pba.shape[1]), lambda p: (rev(p), 0))

    def body(qkv_ref, z_ref, pba_ref, al_ref, dt_ref, on_ref, st_ref, ti_ref, dc_ref,
             dqkv_ref, dz_ref, dpba_ref, dal_ref, ddt_ref, don_ref, ds_scr):
        @pl.when(pl.program_id(0) == 0)
        def _():
            ds_scr[...] = jnp.zeros_like(ds_scr)
            dal_ref[...] = jnp.zeros_like(dal_ref)
            ddt_ref[...] = jnp.zeros_like(ddt_ref)
            don_ref[...] = jnp.zeros_like(don_ref)

        tsv = ti_ref[...]
        fn = lambda s0, qc, kc, vc, z, pb, al, dt, on: delta_pair(s0, qc, kc, vc, z, pb, al, dt, on, tsv)[:2]
        _, vjp = jax.vjp(fn, st_ref[...], _heads(qkv_ref, 0, h), _heads(qkv_ref, h, h), _heads(qkv_ref, 2 * h, h),
                         _heads(z_ref, 0, h), pba_ref[...], al_ref[...], dt_ref[...], on_ref[...])
        ds0, dq, dk, dv, dz, dpba, dal, ddt, don = vjp((_heads(dc_ref, 0, h), ds_scr[...]))
        ds_scr[...] = ds0
        for i in range(h):
            dqkv_ref[:, _hcols(i)] = dq[i]
            dqkv_ref[:, _hcols(h + i)] = dk[i]
            dqkv_ref[:, _hcols(2 * h + i)] = dv[i]
            dz_ref[:, _hcols(i)] = dz[i].astype(dz_ref.dtype)
        dal_ref[...] += dal
        ddt_ref[...] += ddt
        dpba_ref[...] = dpba.astype(dpba_ref.dtype)
        don_ref[...] += don

    return pl.pallas_call(
        body, name=name, grid=(npair,),
        out_shape=(jax.ShapeDtypeStruct((t, 3 * hd), F32), jax.ShapeDtypeStruct(pm.shape, BF16),
                   jax.ShapeDtypeStruct(pba.shape, BF16),
                   jax.ShapeDtypeStruct((h, 1, 1), F32), jax.ShapeDtypeStruct((h, 1, 1), F32),
                   jax.ShapeDtypeStruct((1, HEAD_DIM), F32)),
        in_specs=[wide, zsp, bsp, par, par, onsp, mat, mat, pl.BlockSpec((PAIR, hd), lambda p: (rev(p), 0))],
        out_specs=(wide, zsp, bsp, par, par, onsp),
        scratch_shapes=[pltpu.VMEM((h, HEAD_DIM, HEAD_DIM), F32)],
        compiler_params=_cparams(("arbitrary",)),
    )(qkv, pm, pba, alog, dtb, onorm, states, tinvs, dcat)


def gmlp_block(ln_g, ln_b, w, bcol, u_raw, v_raw):
    n = w.shape[-1]
    ri = lax.broadcasted_iota(jnp.int32, (n, n), 0)
    ci = lax.broadcasted_iota(jnp.int32, (n, n), 1)
    mask = ((ci // CHUNK) <= (ri // CHUNK)).astype(F32)
    vg = _gelu(v_raw)
    vc = vg - jnp.mean(vg, axis=-1, keepdims=True)
    vgn = vc * lax.rsqrt(jnp.mean(vc * vc, axis=-1, keepdims=True) + EPS) * ln_g + ln_b
    return _gelu(u_raw) * (_dot(w * mask, vgn) + bcol)


def gmlp_fwd(pm, ln_g, ln_b, w_s, bcol, cat, *, name):
    t = pm.shape[0]
    g = w_s.shape[0]
    gw = g * HEAD_DIM
    assert pm.shape[1] == 6 * gw and cat.shape[1] == 2 * gw

    def body(u_ref, v_ref, lg_ref, lb_ref, w_ref, b_ref, cat_in, o_ref):
        del cat_in
        out = gmlp_block(lg_ref[...], lb_ref[...], w_ref[...], b_ref[...], _heads(u_ref, 0, g), _heads(v_ref, 0, g))
        for i in range(g):
            o_ref[:, _hcols(i)] = out[i].astype(o_ref.dtype)

    full = lambda a: pl.BlockSpec(a.shape, lambda m: (0, 0, 0))
    return pl.pallas_call(
        body, name=name, grid=(t // HEAD_DIM,),
        out_shape=jax.ShapeDtypeStruct(cat.shape, cat.dtype),
        in_specs=[pl.BlockSpec((HEAD_DIM, gw), lambda m: (m, 4)), pl.BlockSpec((HEAD_DIM, gw), lambda m: (m, 5)),
                  full(ln_g), full(ln_b), full(w_s), full(bcol), pl.BlockSpec(memory_space=pl.ANY)],
        out_specs=pl.BlockSpec((HEAD_DIM, gw), lambda m: (m, 1)),
        input_output_aliases={6: 0},
        compiler_params=_cparams(("arbitrary",)),
    )(pm, pm, ln_g, ln_b, w_s, bcol, cat)


def gmlp_bwd(pm, ln_g, ln_b, w_s, bcol, dcat, dpm, *, name):
    t = pm.shape[0]
    g = w_s.shape[0]
    gw = g * HEAD_DIM
    assert pm.shape[1] == 6 * gw and dpm.shape == pm.shape

    def body(u_ref, v_ref, lg_ref, lb_ref, w_ref, b_ref, dc_ref, dpm_in, duv_ref, dlg_ref, dlb_ref, dw_ref, db_ref):
        del dpm_in
        first = pl.program_id(0) == 0
        _, vjp = jax.vjp(gmlp_block, lg_ref[...], lb_ref[...], w_ref[...], b_ref[...], _heads(u_ref, 0, g), _heads(v_ref, 0, g))
        dlg, dlb, dw, db, du, dv = vjp(_heads(dc_ref, 0, g))
        for i in range(g):
            duv_ref[:, _hcols(i)] = du[i].astype(duv_ref.dtype)
            duv_ref[:, _hcols(g + i)] = dv[i].astype(duv_ref.dtype)
        for ref, val in ((dlg_ref, dlg), (dlb_ref, dlb), (dw_ref, dw), (db_ref, db)):
            @pl.when(first)
            def _(ref=ref, val=val):
                ref[...] = val

            @pl.when(jnp.logical_not(first))
            def _(ref=ref, val=val):
                ref[...] += val

    full = lambda a: pl.BlockSpec(a.shape, lambda m: (0, 0, 0))
    return pl.pallas_call(
        body, name=name, grid=(t // HEAD_DIM,),
        out_shape=(jax.ShapeDtypeStruct(dpm.shape, dpm.dtype),
                   jax.ShapeDtypeStruct(ln_g.shape, F32), jax.ShapeDtypeStruct(ln_b.shape, F32),
                   jax.ShapeDtypeStruct(w_s.shape, F32), jax.ShapeDtypeStruct(bcol.shape, F32)),
        in_specs=[pl.BlockSpec((HEAD_DIM, gw), lambda m: (m, 4)), pl.BlockSpec((HEAD_DIM, gw), lambda m: (m, 5)),
                  full(ln_g), full(ln_b), full(w_s), full(bcol), pl.BlockSpec((HEAD_DIM, gw), lambda m: (m, 1)),
                  pl.BlockSpec(memory_space=pl.ANY)],
        out_specs=(pl.BlockSpec((HEAD_DIM, 2 * gw), lambda m: (m, 2)), full(ln_g), full(ln_b), full(w_s), full(bcol)),
        input_output_aliases={7: 0},
        compiler_params=_cparams(("arbitrary",)),
    )(pm, pm, ln_g, ln_b, w_s, bcol, dcat, dpm)


def loss_head(g, x, r, tgt, *, name, rows=ROWS):
    t, d = x.shape

    def body(g_ref, x_ref, r_ref, t_ref, l_ref, dx_ref, dxb_ref, dg_ref):
        y, vjp = jax.vjp(lambda gg, xx: _rms(xx, gg), g_ref[...], x_ref[...] + r_ref[...])
        e = y - t_ref[...]
        part = (0.5 / d) * jnp.sum(jnp.sum(e * e, axis=1, keepdims=True), axis=0, keepdims=True)
        dg, dx = vjp(e * (1.0 / d))
        dx_ref[...] = dx
        dxb_ref[...] = dx.astype(BF16)
        first = pl.program_id(0) == 0

        @pl.when(first)
        def _():
            l_ref[...] = part
            dg_ref[...] = dg

        @pl.when(jnp.logical_not(first))
        def _():
            l_ref[...] += part
            dg_ref[...] += dg

    rs = pl.BlockSpec((rows, d), lambda i: (i, 0))
    gs = pl.BlockSpec((1, d), lambda i: (0, 0))
    return pl.pallas_call(
        body, name=name, grid=(t // rows,),
        out_shape=(jax.ShapeDtypeStruct((1, 1), F32), jax.ShapeDtypeStruct((t, d), F32),
                   jax.ShapeDtypeStruct((t, d), BF16), jax.ShapeDtypeStruct((1, d), F32)),
        in_specs=[gs, rs, rs, rs],
        out_specs=(pl.BlockSpec((1, 1), lambda i: (0, 0)), rs, rs, gs),
        compiler_params=_cparams(("arbitrary",)),
    )(g, x, r, tgt)


def adamw(w, gs, m, v, *, name, behind=None):
    nl, r, c = w.shape
    assert len(gs) == nl
    if r % SUBLANES == 0:
        tr, tc = _pick(r, (256, 128, 64, 32, 16, 8)), c
    else:
        tr, tc = r, _pick(c, (256, 128))
    k1 = 1.0 - ADAM_B1 ** ADAM_STEP
    k2 = 1.0 - ADAM_B2 ** ADAM_STEP

    def body(*refs):
        w_ref, m_ref, v_ref = refs[0], refs[1], refs[2]
        g_refs = refs[3:3 + nl]
        go_ref, d_ref, mo_ref, vo_ref = refs[-4:]
        gg = g_refs[0][...]
        for li in range(1, nl):
            gg = jnp.where(pl.program_id(0) == li, g_refs[li][...], gg)
        mn = ADAM_B1 * m_ref[...] + (1.0 - ADAM_B1) * gg
        vn = ADAM_B2 * v_ref[...] + (1.0 - ADAM_B2) * (gg * gg)
        go_ref[...] = gg
        d_ref[...] = -ADAM_LR * ((mn / k1) / (jnp.sqrt(vn / k2) + ADAM_EPS) + ADAM_WD * w_ref[...])
        mo_ref[...] = mn
        vo_ref[...] = vn

    if tc == c:
        sp = pl.BlockSpec((None, tr, c), lambda l, i: (l, i, 0))
        gsp = pl.BlockSpec((tr, c), lambda l, i: (i, 0))
    else:
        sp = pl.BlockSpec((None, r, tc), lambda l, i: (l, 0, i))
        gsp = pl.BlockSpec((r, tc), lambda l, i: (0, i))
    sds = jax.ShapeDtypeStruct((nl, r, c), F32)
    extra = [] if behind is None else [behind]
    return pl.pallas_call(
        body, name=name, grid=(nl, (r // tr) * (c // tc)), out_shape=(sds, sds, sds, sds),
        in_specs=[sp, sp, sp] + [gsp] * nl + [pl.BlockSpec(e.shape, lambda l, i: (0, 0)) for e in extra], out_specs=(sp, sp, sp, sp),
        compiler_params=_cparams(("parallel", "parallel")),
    )(w, m, v, *gs, *extra)


MESH = pl.DeviceIdType.MESH
ANY = pl.BlockSpec(memory_space=pl.ANY)


def _me():
    return lax.axis_index("x"), lax.axis_index("y"), lax.axis_index("c")


def _other_chips(x, y):
    return [(1 - x, y), (x, 1 - y), (1 - x, 1 - y)]


def gather_shards(arrs, *, name):
    n = len(arrs)
    per = 7

    def body(*refs):
        ins, outs = refs[:n], refs[n:2 * n]
        send_sems, recv_sems = refs[2 * n:]
        x, y, c = _me()
        chip = 2 * x + y
        sib = (x, y, 1 - c)
        chips = _other_chips(x, y)

        def rc(i, k, src, dst, to):
            return pltpu.make_async_remote_copy(src_ref=src, dst_ref=dst, send_sem=send_sems.at[per * i + k],
                                                recv_sem=recv_sems.at[per * i + k], device_id=to, device_id_type=MESH)

        def half(i, which):
            hrows = arrs[i].shape[0] // 2
            return pl.ds(which * hrows, hrows)

        own = [rc(i, 6, ins[i], outs[i].at[chip], sib) for i in range(n)]
        for cp in own:
            cp.start()
        started = []
        for i in range(n):
            for j, ch in enumerate(chips):
                cp = rc(i, j, ins[i].at[half(i, c)], outs[i].at[chip, half(i, c)], (ch[0], ch[1], c))
                cp.start()
                started.append(cp)
        for i in range(n):
            for j, ch in enumerate(chips):
                sc = 2 * ch[0] + ch[1]
                landed = outs[i].at[sc, half(i, c)]
                rc(i, j, ins[i].at[half(i, c)], landed, (ch[0], ch[1], c)).wait_recv()
                fw = rc(i, 3 + j, landed, landed, sib)
                fw.start()
                started.append(fw)
        for i in range(n):
            for j, ch in enumerate(chips):
                sc = 2 * ch[0] + ch[1]
                theirs = outs[i].at[sc, half(i, 1 - c)]
                rc(i, 3 + j, theirs, theirs, sib).wait_recv()
        for cp in own:
            cp.wait_recv()
        for cp in started + own:
            cp.wait_send()

    return pl.pallas_call(
        body, name=name,
        out_shape=tuple(jax.ShapeDtypeStruct((N_CHIPS,) + a.shape, a.dtype) for a in arrs),
        in_specs=[ANY] * n, out_specs=tuple([ANY] * n),
        scratch_shapes=[pltpu.SemaphoreType.DMA((per * n,)), pltpu.SemaphoreType.DMA((per * n,))],
        compiler_params=pltpu.CompilerParams(has_side_effects=True),
    )(*arrs)


HBM = pl.BlockSpec(memory_space=pltpu.HBM)
SEM = pl.BlockSpec(memory_space=pltpu.SEMAPHORE)
EFFECT = pltpu.SideEffectType.DATAFLOW_SIDE_EFFECTING
def _hbm(a):
    return pltpu.with_memory_space_constraint(a, pltpu.HBM)


def _half_rows(arr, which):
    hrows = arr.shape[-2] // 2
    return pl.ds(which * hrows, hrows)


def _ici_copies(srcs, lands, sems):
    x, y, c = _me()
    chip = 2 * x + y
    return [pltpu.make_async_remote_copy(src_ref=srcs[i].at[_half_rows(srcs[i], c)], dst_ref=lands[i].at[chip, _half_rows(srcs[i], c)],
                                         send_sem=sems[0].at[3 * i + j], recv_sem=sems[1].at[3 * i + j],
                                         device_id=(ch[0], ch[1], c), device_id_type=MESH)
            for i in range(len(srcs)) for j, ch in enumerate(_other_chips(x, y))]


def _own_copies(srcs, lands, sems):
    x, y, c = _me()
    return [pltpu.make_async_remote_copy(src_ref=srcs[i], dst_ref=lands[i].at[2 * x + y], send_sem=sems[0].at[i],
                                         recv_sem=sems[1].at[i], device_id=(x, y, 1 - c), device_id_type=MESH)
            for i in range(len(srcs))]


def _fwd_copies(srcs, lands, sems, half):
    x, y, c = _me()
    cps = []
    for i in range(len(srcs)):
        for j, ch in enumerate(_other_chips(x, y)):
            part = lands[i].at[2 * ch[0] + ch[1], _half_rows(srcs[i], half)]
            cps.append(pltpu.make_async_remote_copy(src_ref=part, dst_ref=part, send_sem=sems[0].at[3 * i + j],
                                                    recv_sem=sems[1].at[3 * i + j], device_id=(x, y, 1 - c), device_id_type=MESH))
    return cps


def gather_start(groups, *, name):
    arrs = [a for g in groups for a in g]
    n, ng = len(arrs), len(groups)
    bounds = np.cumsum([0] + [len(g) for g in groups])

    def body(*refs):
        srcs, lands = refs[:n], refs[n:2 * n]
        sems = refs[2 * n:2 * n + 4 * ng]
        token = refs[-1]
        for gi in range(ng):
            lo, hi = bounds[gi], bounds[gi + 1]
            for cp in _ici_copies(srcs[lo:hi], lands[lo:hi], sems[4 * gi:4 * gi + 2]):
                cp.start()
        for gi in range(ng):
            lo, hi = bounds[gi], bounds[gi + 1]
            for cp in _own_copies(srcs[lo:hi], lands[lo:hi], sems[4 * gi + 2:4 * gi + 4]):
                cp.start()
        token[...] = jnp.zeros_like(token)

    sem_shapes = []
    for g in groups:
        sem_shapes += [pltpu.SemaphoreType.DMA((3 * len(g),))] * 2 + [pltpu.SemaphoreType.DMA((len(g),))] * 2
    res = pl.pallas_call(
        body, name=name,
        out_shape=tuple(sem_shapes) + tuple(pltpu.HBM(a.shape, a.dtype) for a in arrs)
        + tuple(pltpu.HBM((N_CHIPS,) + a.shape, a.dtype) for a in arrs) + (jax.ShapeDtypeStruct((SUBLANES, LANES), F32),),
        in_specs=[HBM] * (2 * n),
        out_specs=tuple([SEM] * (4 * ng)) + tuple([HBM] * (2 * n)) + (pl.BlockSpec(memory_space=pltpu.VMEM),),
        input_output_aliases={i: 4 * ng + i for i in range(2 * n)},
        compiler_params=pltpu.CompilerParams(has_side_effects=EFFECT),
    )(*[_hbm(a) for a in arrs], *[_hbm(lax.empty((N_CHIPS,) + a.shape, a.dtype)) for a in arrs])
    sems, thru, lands, token = res[:4 * ng], res[4 * ng:4 * ng + n], res[4 * ng + n:4 * ng + 2 * n], res[-1]
    out = [{"ici": (sems[4 * gi], sems[4 * gi + 1]), "own": (sems[4 * gi + 2], sems[4 * gi + 3]),
            "thru": list(thru[bounds[gi]:bounds[gi + 1]]), "lands": list(lands[bounds[gi]:bounds[gi + 1]])} for gi in range(ng)]
    return out, token


def gather_forward(group, after, *, name):
    thru, lands = group["thru"], group["lands"]
    n = len(thru)

    def body(*refs):
        srcs, lands_r = refs[:n], refs[n:2 * n]
        ici = refs[2 * n:2 * n + 2]
        fwd = refs[2 * n + 3 + 2 * n:2 * n + 3 + 2 * n + 2]
        for cp in _ici_copies(srcs, lands_r, ici):
            cp.wait_send()
            cp.wait_recv()
        for cp in _fwd_copies(srcs, lands_r, fwd, _me()[2]):
            cp.start()
        refs[-1][...] = jnp.zeros_like(refs[-1])

    res = pl.pallas_call(
        body, name=name,
        out_shape=tuple(pltpu.HBM(a.shape, a.dtype) for a in thru) + tuple(pltpu.HBM(a.shape, a.dtype) for a in lands)
        + (pltpu.SemaphoreType.DMA((3 * n,)),) * 2 + (jax.ShapeDtypeStruct((SUBLANES, LANES), F32),),
        in_specs=[HBM] * (2 * n) + [SEM, SEM, pl.BlockSpec(memory_space=pl.ANY)],
        out_specs=tuple([HBM] * (2 * n)) + (SEM, SEM, pl.BlockSpec(memory_space=pltpu.VMEM)),
        input_output_aliases={i: i for i in range(2 * n)},
        compiler_params=pltpu.CompilerParams(has_side_effects=EFFECT),
    )(*thru, *lands, *group["ici"], after)
    return {"own": group["own"], "fwd": (res[2 * n], res[2 * n + 1]), "thru": list(res[:n]), "lands": list(res[n:2 * n]),
            "token": res[-1]}


def gather_wait(group, after, *, name):
    thru, lands = group["thru"], group["lands"]
    n = len(thru)

    def body(*refs):
        srcs, lands_r = refs[:n], refs[n:2 * n]
        own, fwd = refs[2 * n:2 * n + 2], refs[2 * n + 2:2 * n + 4]
        c = _me()[2]
        for mine, theirs in zip(_fwd_copies(srcs, lands_r, fwd, c), _fwd_copies(srcs, lands_r, fwd, 1 - c)):
            mine.wait_send()
            theirs.wait_recv()
        for cp in _own_copies(srcs, lands_r, own):
            cp.wait_send()
            cp.wait_recv()

    res = pl.pallas_call(
        body, name=name,
        out_shape=tuple(pltpu.HBM(a.shape, a.dtype) for a in thru) + tuple(pltpu.HBM(a.shape, a.dtype) for a in lands),
        in_specs=[HBM] * (2 * n) + [SEM] * 4 + [pl.BlockSpec(memory_space=pl.ANY)],
        out_specs=tuple([HBM] * (2 * n)),
        input_output_aliases={i: i for i in range(2 * n)},
        compiler_params=pltpu.CompilerParams(has_side_effects=EFFECT),
    )(*thru, *lands, *group["own"], *group["fwd"], after)
    return list(res[n:])


def _pair_copies(srcs, lands, sems):
    x, y, c = _me()
    return [pltpu.make_async_remote_copy(src_ref=srcs[i].at[:, _half_rows(srcs[i], 1 - c), :], dst_ref=lands[i], send_sem=sems[0].at[i],
                                         recv_sem=sems[1].at[i], device_id=(x, y, 1 - c), device_id_type=MESH)
            for i in range(len(srcs))]


def _scatter_copies(srcs, lands, sems):
    x, y, c = _me()
    return [pltpu.make_async_remote_copy(src_ref=srcs[i].at[2 * ch[0] + ch[1]], dst_ref=lands[i].at[j], send_sem=sems[0].at[3 * i + j],
                                         recv_sem=sems[1].at[3 * i + j], device_id=(ch[0], ch[1], c), device_id_type=MESH)
            for i in range(len(srcs)) for j, ch in enumerate(_other_chips(x, y))]


def split_start(copies, arrs, land_shapes, nsem, *, name):
    n = len(arrs)

    def body(*refs):
        for cp in copies(refs[:n], refs[n:2 * n], refs[2 * n:2 * n + 2]):
            cp.start()
        refs[-1][...] = jnp.zeros_like(refs[-1])

    res = pl.pallas_call(
        body, name=name,
        out_shape=(pltpu.SemaphoreType.DMA((nsem,)),) * 2 + tuple(pltpu.HBM(a.shape, a.dtype) for a in arrs)
        + tuple(pltpu.HBM(s, a.dtype) for s, a in zip(land_shapes, arrs)) + (jax.ShapeDtypeStruct((SUBLANES, LANES), F32),),
        in_specs=[HBM] * (2 * n), out_specs=(SEM, SEM) + tuple([HBM] * (2 * n)) + (pl.BlockSpec(memory_space=pltpu.VMEM),),
        input_output_aliases={i: 2 + i for i in range(2 * n)},
        compiler_params=pltpu.CompilerParams(has_side_effects=EFFECT),
    )(*[_hbm(a) for a in arrs], *[_hbm(lax.empty(s, a.dtype)) for s, a in zip(land_shapes, arrs)])
    return {"copies": copies, "sems": (res[0], res[1]), "thru": list(res[2:2 + n]), "lands": list(res[2 + n:2 + 2 * n]),
            "token": res[-1]}


def split_wait(state, after, *, name):
    thru, lands, copies = state["thru"], state["lands"], state["copies"]
    n = len(thru)

    def body(*refs):
        for cp in copies(refs[:n], refs[n:2 * n], refs[2 * n:2 * n + 2]):
            cp.wait_send()
            cp.wait_recv()

    res = pl.pallas_call(
        body, name=name,
        out_shape=tuple(pltpu.HBM(a.shape, a.dtype) for a in thru) + tuple(pltpu.HBM(a.shape, a.dtype) for a in lands),
        in_specs=[HBM] * (2 * n) + [SEM, SEM, pl.BlockSpec(memory_space=pl.ANY)], out_specs=tuple([HBM] * (2 * n)),
        input_output_aliases={i: i for i in range(2 * n)},
        compiler_params=pltpu.CompilerParams(has_side_effects=EFFECT),
    )(*thru, *lands, *state["sems"], after)
    return list(res[:n]), list(res[n:])


def sibling_join_halves(arrs, *, name):
    n = len(arrs)

    def body(*refs):
        outs = refs[n:2 * n]
        send_sems, recv_sems = refs[2 * n:]
        x, y, c = _me()
        cps = []
        for i in range(n):
            hrows = arrs[i].shape[0] // 2
            mine = outs[i].at[pl.ds(c * hrows, hrows)]
            cps.append(pltpu.make_async_remote_copy(src_ref=mine, dst_ref=mine, send_sem=send_sems.at[i],
                                                    recv_sem=recv_sems.at[i], device_id=(x, y, 1 - c), device_id_type=MESH))
        for cp in cps:
            cp.start()
        for i in range(n):
            hrows = arrs[i].shape[0] // 2
            theirs = outs[i].at[pl.ds((1 - c) * hrows, hrows)]
            pltpu.make_async_remote_copy(src_ref=theirs, dst_ref=theirs, send_sem=send_sems.at[i], recv_sem=recv_sems.at[i],
                                         device_id=(x, y, 1 - c), device_id_type=MESH).wait_recv()
        for cp in cps:
            cp.wait_send()

    return pl.pallas_call(
        body, name=name,
        out_shape=tuple(jax.ShapeDtypeStruct(a.shape, a.dtype) for a in arrs),
        in_specs=[ANY] * n, out_specs=tuple([ANY] * n),
        input_output_aliases={i: i for i in range(n)},
        scratch_shapes=[pltpu.SemaphoreType.DMA((n,)), pltpu.SemaphoreType.DMA((n,))],
        compiler_params=pltpu.CompilerParams(has_side_effects=True),
    )(*arrs)


def pre_reduce(gd, sib, who, *, name):
    _, r, c = gd.shape
    h = r // 2
    tr = _pick(h, (256, 128, 64, 32, 16, 8))
    nrb = h // tr

    def body(who_ref, gd_ref, sib_ref, pb_ref, own_ref):
        p = gd_ref[...] + sib_ref[...]
        pb_ref[...] = p.astype(BF16)

        @pl.when(pl.program_id(1) == who_ref[1])
        def _():
            own_ref[...] = p

    return pl.pallas_call(
        body, name=name,
        out_shape=(jax.ShapeDtypeStruct((N_CHIPS, h, c), BF16), jax.ShapeDtypeStruct((h, c), F32)),
        grid_spec=pltpu.PrefetchScalarGridSpec(
            num_scalar_prefetch=1, grid=(nrb, N_CHIPS),
            in_specs=[pl.BlockSpec((None, tr, c), lambda i, s, w: (s, w[0] * nrb + i, 0)),
                      pl.BlockSpec((None, tr, c), lambda i, s, w: (s, i, 0))],
            out_specs=(pl.BlockSpec((None, tr, c), lambda i, s, w: (s, i, 0)),
                       pl.BlockSpec((tr, c), lambda i, s, w: (i, 0)))),
        compiler_params=_cparams(("parallel", "arbitrary")),
    )(who, gd, sib)


def final_reduce(own, rcv, who, *, name):
    h, c = own.shape
    tr = _pick(h, (256, 128, 64, 32, 16, 8))
    nrb = h // tr

    def body(who_ref, own_ref, rcv_ref, o_ref):
        del who_ref
        acc = own_ref[...]
        for j in range(3):
            acc = acc + rcv_ref[j].astype(F32)
        o_ref[...] = acc

    return pl.pallas_call(
        body, name=name, out_shape=jax.ShapeDtypeStruct((2 * h, c), F32),
        grid_spec=pltpu.PrefetchScalarGridSpec(
            num_scalar_prefetch=1, grid=(nrb,),
            in_specs=[pl.BlockSpec((tr, c), lambda i, w: (i, 0)), pl.BlockSpec((3, tr, c), lambda i, w: (0, i, 0))],
            out_specs=pl.BlockSpec((tr, c), lambda i, w: (w[0] * nrb + i, 0))),
        compiler_params=_cparams(("parallel",)),
    )(who, own, rcv)


def sum8(landed, own, me, *, name):
    _, r, c = landed.shape
    tr = _pick(r, (256, 128, 64, 32, 16, 8))

    def body(me_ref, a_ref, own_ref, o_ref):
        acc = None
        for j in range(8):
            term = jnp.where(me_ref[0] == j, own_ref[...], a_ref[j])
            acc = term if acc is None else acc + term
        o_ref[...] = acc

    return pl.pallas_call(
        body, name=name, out_shape=jax.ShapeDtypeStruct((r, c), F32),
        grid_spec=pltpu.PrefetchScalarGridSpec(
            num_scalar_prefetch=1, grid=(r // tr,),
            in_specs=[pl.BlockSpec((8, tr, c), lambda i, w: (0, i, 0)), pl.BlockSpec((tr, c), lambda i, w: (i, 0))],
            out_specs=pl.BlockSpec((tr, c), lambda i, w: (i, 0))),
        compiler_params=_cparams(("parallel",)),
    )(me, landed, own)


def _everyone_copies(srcs, lands, sems):
    x, y, c = _me()
    me = 4 * x + 2 * y + c
    cps = []
    for k in range(1, 8):
        fx, fy, fc = (k >> 2) & 1, (k >> 1) & 1, k & 1
        to = (x + fx - 2 * x * fx, y + fy - 2 * y * fy, c + fc - 2 * c * fc)
        cps.append(pltpu.make_async_remote_copy(src_ref=srcs[0], dst_ref=lands[0].at[me], send_sem=sems[0].at[k - 1],
                                                recv_sem=sems[1].at[k - 1], device_id=to, device_id_type=MESH))
    return cps


def _behind(p, *tokens):
    for tk in tokens:
        if tk is not None:
            p = p + tk[0, 0]
    return p


def local_step(x, tgt, wb, ws, arrive=lambda layer, event, after: {}, reduce=lambda group, event, gb, after: None):
    t, d = x.shape
    nh = ws["alog"].shape[0]
    ng = ws["w_s"].shape[0]
    assert nh == ng
    mix_w = (nh + ng) * HEAD_DIM

    (h0,) = rows_fwd(stage_norm, [ws["e_norm"]], [x], [(1, BF16, d)], name="f_norm_e")
    pm = mm_nt(h0, wb["w_main_t"], name="f_proj_main")
    pba = mm_nt(h0, wb["w_ba_t"], name="f_proj_ba")
    qkv = conv_fwd(pm, ws["conv_w"], mode="silu", name="f_conv4")
    cat, states, tinvs = delta_fwd(qkv, pm, pba, ws["alog"], ws["dtb"], ws["onorm"], mix_w, name="f_delta")
    tok = arrive("layer0", "landed", states).get("token")
    cat = gmlp_fwd(pm, _behind(ws["lng"], tok), ws["lnb"], ws["w_s"], ws["bs"], cat, name="f_gmlp")
    wb = {**wb, **arrive("layer0", "joined", cat)}
    y0 = mm_nn(cat, wb["w_out"], name="f_out")
    x1, h1 = rows_fwd(stage_res_norm, [ws["f_norm0"]], [x, y0], [(0, F32, d), (1, BF16, d)], name="f_norm_f0")
    a1, s1 = mm_nn(h1, wb["w1_0"], name="f_mlp0_up", epilogue="relu2")
    y1 = mm_nn(s1, wb["w2_0"], name="f_mlp0_down")
    tok = arrive("layer1", "landed", y1).get("token")
    x2, h2 = rows_fwd(stage_res_norm, [_behind(ws["o_norm"], tok)], [x1, y1], [(0, F32, d), (1, BF16, d)], name="f_norm_o")
    wb = {**wb, **arrive("layer1", "joined", h2)}
    zz = mm_nn(h2, wb["pw1"], name="f_pw1")
    zparts = [(zz, d, 0), (zz, d, 1)]
    (gl,) = rows_fwd(stage_glu, [ws["b1a"], ws["b1b"]], zparts, [(0, F32, d)], name="f_glu")
    cv = conv_fwd(gl, ws["dw"], mode="plain", name="f_conv31")
    ln_params = [ws["dw_b"], ws["ln_g"], ws["ln_b"]]
    (sl,) = rows_fwd(stage_ln_silu, ln_params, [cv], [(0, BF16, d)], name="f_ln_silu")
    y2 = mm_nn(sl, wb["pw2"], name="f_pw2")
    x3, h3 = rows_fwd(stage_res_bias_norm, [ws["b2"], ws["f_norm1"]], [x2, y2], [(0, F32, d), (1, BF16, d)], name="f_norm_f1")
    a3, s3 = mm_nn(h3, wb["w1_1"], name="f_mlp1_up", epilogue="relu2")
    y3 = mm_nn(s3, wb["w2_1"], name="f_mlp1_down")
    loss, d4, d4b, g_final = loss_head(ws["final_norm"], x3, y3, tgt, name="loss_head")

    gb, gs = {}, {"final_norm": g_final}
    s_up = wb["w1_0"].shape[0]

    gb["w2_1"] = mm_tn(s3, d4b, groups=1, name="b_mlp1_down_w")
    dpre3 = mm_nt(d4b, wb["w2_1"], name="b_mlp1_down_x", mul=a3, out_dtype=BF16)
    gb["w1_1"] = mm_tn(h3, dpre3, groups=s_up, name="b_mlp1_up_w")
    tok = reduce("mlp1", "ready", gb, None)
    dh3 = mm_nt(dpre3, wb["w1_1"], name="b_mlp1_up_x", out_dtype=BF16)
    d3, d3b, gs["b2"], gs["f_norm1"] = rows_bwd(stage_res_bias_norm, [ws["b2"], _behind(ws["f_norm1"], tok)], [x2, y2], [d4, dh3],
                                                [(0, F32), (1, BF16)], name="b_norm_f1")
    tok = reduce("mlp1", "paired", gb, d3)
    gb["pw2"] = mm_tn(sl, d3b, groups=1, name="b_pw2_w")
    dsl = mm_nt(d3b, wb["pw2"], name="b_pw2_x", out_dtype=BF16)
    dcv, gs["dw_b"], gs["ln_g"], gs["ln_b"] = rows_bwd(stage_ln_silu, [_behind(ln_params[0], tok)] + ln_params[1:], [cv], [dsl],
                                                       [(0, F32)], name="b_ln_silu")
    dgl, gs["dw"] = conv_bwd(gl, ws["dw"], dcv, name="b_conv31")
    dzz, gs["b1a"], gs["b1b"] = rows_bwd(stage_glu, [ws["b1a"], ws["b1b"]], zparts, [dgl], [(0, BF16), (1, BF16)],
                                         name="b_glu", concat=True)
    gb["pw1"] = mm_tn(h2, dzz, groups=wb["pw1"].shape[0], name="b_pw1_w")
    tok = reduce("conv", "ready", gb, None)
    dh2 = mm_nt(dzz, wb["pw1"], name="b_pw1_x", out_dtype=BF16)
    d2, d2b, gs["o_norm"] = rows_bwd(stage_res_norm, [_behind(ws["o_norm"], tok)], [x1, y1], [d3, dh2], [(0, F32), (1, BF16)],
                                     name="b_norm_o")
    tok_conv = reduce("conv", "paired", gb, d2)
    gb["w2_0"] = mm_tn(s1, d2b, groups=1, name="b_mlp0_down_w")
    dpre1 = mm_nt(d2b, wb["w2_0"], name="b_mlp0_down_x", mul=a1, out_dtype=BF16)
    gb["w1_0"] = mm_tn(h1, dpre1, groups=s_up, name="b_mlp0_up_w")
    tok = reduce("mlp0", "ready", gb, None)
    dh1 = mm_nt(dpre1, wb["w1_0"], name="b_mlp0_up_x", out_dtype=BF16)
    d1, d1b, gs["f_norm0"] = rows_bwd(stage_res_norm, [_behind(ws["f_norm0"], tok_conv, tok)], [x, y0], [d2, dh1],
                                      [(0, F32), (1, BF16)], name="b_norm_f0")
    tok_mlp0 = reduce("mlp0", "paired", gb, d1)
    gb["w_out"] = mm_tn(cat, d1b, groups=1, name="b_out_w")
    tok = reduce("mixer", "ready", gb, None)
    dcat = mm_nt(d1b, wb["w_out"], name="b_out_x")
    dqkv_c, dpm, dpba, gs["alog"], gs["dtb"], gs["onorm"] = delta_bwd(
        qkv, pm, pba, ws["alog"], ws["dtb"], _behind(ws["onorm"], tok_mlp0, tok), states, tinvs, dcat, name="b_delta")
    tok = reduce("mixer", "paired", gb, dqkv_c)
    dpm, gs["lng"], gs["lnb"], gs["w_s"], gs["bs"] = gmlp_bwd(
        pm, ws["lng"], ws["lnb"], ws["w_s"], ws["bs"], dcat, dpm, name="b_gmlp")
    conv_w = _behind(ws["conv_w"], tok)
    dconv = conv_fwd(pm, conv_w, mode="silu_bwd", g=dqkv_c, name="b_conv4_silu")
    dpm, gs["conv_w"] = conv_bwd(pm, conv_w, dconv, name="b_conv4", into=dpm)
    gb["w_main_t"] = mm_tn(dpm, h0, groups=1, name="b_proj_main_w")
    gb["w_ba_t"] = mm_tn(dpba, h0, groups=1, name="b_proj_ba_w")
    dh0 = mm_nn(dpm, wb["w_main_t"], name="b_proj_main_x")
    dh0 = mm_nn(dpba, wb["w_ba_t"], name="b_proj_ba_x", add=dh0, out_dtype=BF16)
    grad_x, gs["e_norm"] = rows_bwd(stage_norm, [ws["e_norm"]], [x], [d1, dh0], [(0, F32)], name="b_norm_e")
    return loss, grad_x, gb, gs


WEIGHTS = ["e_norm", "e_w_in", "e_conv_w", "e_a_log", "e_dt_bias", "e_o_norm", "e_ln_g", "e_ln_b", "e_w_s", "e_b_s", "e_w_out",
           "o_norm", "o_pw1", "o_pw1_b", "o_dw", "o_dw_b", "o_ln_g", "o_ln_b", "o_pw2", "o_pw2_b", "f_norm", "f_w1", "f_w2",
           "final_norm"]
BIG = ["e_w_in", "e_w_out", "o_pw1", "o_pw2", "f_w1", "f_w2"]
SMALL_SHARDED = ["e_conv_w", "o_norm", "o_pw1_b", "o_dw", "o_dw_b", "o_ln_g", "o_ln_b", "o_pw2_b"]
SMALL = [n for n in WEIGHTS if n not in BIG]
LANES = 128
PACK_ROWS = 16
IN_ROW_MULT = 256
REDUCE_GROUPS = {"mlp1": ("w2_1", "w1_1"), "conv": ("pw2", "pw1"), "mlp0": ("w2_0", "w1_0"), "mixer": ("w_out",)}


def _pack(arrs):
    flat = jnp.concatenate([a.reshape(-1).astype(F32) for a in arrs])
    n = flat.shape[0]
    rows = -(-n // (LANES * PACK_ROWS)) * PACK_ROWS
    return jnp.pad(flat, (0, rows * LANES - n)).reshape(rows, LANES)


def _unpack(flat, shapes, lead=()):
    outs, off = [], 0
    for s in shapes:
        n = int(np.prod(s))
        outs.append(flat[..., off:off + n].reshape(lead + tuple(s)))
        off += n
    return outs


def kernel(x, e_norm, e_w_in, e_conv_w, e_a_log, e_dt_bias, e_o_norm, e_ln_g, e_ln_b, e_w_s, e_b_s, e_w_out, o_norm, o_pw1, o_pw1_b, o_dw, o_dw_b, o_ln_g, o_ln_b, o_pw2, o_pw2_b, f_norm, f_w1, f_w2, final_norm, loss_target, m_e_norm, m_e_w_in, m_e_conv_w, m_e_a_log, m_e_dt_bias, m_e_o_norm, m_e_ln_g, m_e_ln_b, m_e_w_s, m_e_b_s, m_e_w_out, m_o_norm, m_o_pw1, m_o_pw1_b, m_o_dw, m_o_dw_b, m_o_ln_g, m_o_ln_b, m_o_pw2, m_o_pw2_b, m_f_norm, m_f_w1, m_f_w2, m_final_norm, v_e_norm, v_e_w_in, v_e_conv_w, v_e_a_log, v_e_dt_bias, v_e_o_norm, v_e_ln_g, v_e_ln_b, v_e_w_s, v_e_b_s, v_e_w_out, v_o_norm, v_o_pw1, v_o_pw1_b, v_o_dw, v_o_dw_b, v_o_ln_g, v_o_ln_b, v_o_pw2, v_o_pw2_b, v_f_norm, v_f_w1, v_f_w2, v_final_norm):
    a = dict(locals())
    xi, yi, ci = _me()
    chip = 2 * xi + yi
    who = jnp.stack([ci, chip]).astype(jnp.int32)
    t, d = x.shape[1], x.shape[2]
    nh, ng = e_a_log.shape[1], e_w_s.shape[1]
    n_qkv, n_av, n_bw = 3 * nh * HEAD_DIM, nh * HEAD_DIM, ng * HEAD_DIM
    in_cols = n_qkv + n_av + 2 * nh + 2 * n_bw
    c_ba = n_qkv + n_av

    sh_in = in_cols // N_CHIPS
    pad_in = -(-sh_in // IN_ROW_MULT) * IN_ROW_MULT - sh_in
    w_in_t_local = jnp.pad(e_w_in[0].T.astype(BF16), ((0, pad_in), (0, 0)))
    small_local = [a[n] for n in SMALL_SHARDED]
    g_in, g_small = gather_shards([w_in_t_local, _pack(small_local)], name="gather_weights_in")
    travelling, token = gather_start([[w.astype(BF16) for w in (e_w_out[0], f_w1[0], f_w2[0])],
                                      [w.astype(BF16) for w in (o_pw1[0], o_pw2[0], f_w1[1], f_w2[1])]], name="gather_start")

    state = {"layer0": travelling[0], "layer1": travelling[1]}

    def arrive(layer, event, after):
        if event == "landed":
            state[layer] = gather_forward(state[layer], after, name="gather_forward_" + layer)
            return {"token": state[layer]["token"]}
        got = gather_wait(state[layer], after, name="gather_wait_" + layer)
        if layer == "layer0":
            g_out, g_w1, g_w2 = got
            return {"w_out": g_out.reshape(1, -1, d), "w1_0": g_w1, "w2_0": g_w2.reshape(1, -1, d)}
        g_pw1, g_pw2, g_w1, g_w2 = got
        return {"pw1": g_pw1, "pw2": g_pw2.reshape(1, -1, d), "w1_1": g_w1, "w2_1": g_w2.reshape(1, -1, d)}

    def in_rows(lo, hi):
        parts = []
        for s in range(N_CHIPS):
            a0, a1 = max(lo, s * sh_in), min(hi, (s + 1) * sh_in)
            if a0 < a1:
                parts.append(g_in[s, a0 - s * sh_in:a1 - s * sh_in])
        return parts

    wb = {
        "w_main_t": jnp.concatenate(in_rows(0, c_ba) + in_rows(c_ba + 2 * nh, in_cols), axis=0)[None],
        "w_ba_t": jnp.pad(jnp.concatenate(in_rows(c_ba, c_ba + 2 * nh), axis=0), ((0, LANES - 2 * nh), (0, 0)))[None],
    }
    pieces = _unpack(g_small.reshape(N_CHIPS, -1), [w.shape for w in small_local], lead=(N_CHIPS,))
    full = {n: jnp.moveaxis(p, 0, -2).reshape(p.shape[1:-1] + (N_CHIPS * p.shape[-1],)) for n, p in zip(SMALL_SHARDED, pieces)}
    ws = {
        "e_norm": e_norm + token[0, 0],
        "conv_w": full["e_conv_w"][0], "alog": e_a_log.reshape(nh, 1, 1), "dtb": e_dt_bias.reshape(nh, 1, 1),
        "onorm": e_o_norm, "lng": e_ln_g.reshape(ng, 1, HEAD_DIM), "lnb": e_ln_b.reshape(ng, 1, HEAD_DIM), "w_s": e_w_s[0],
        "bs": e_b_s.reshape(ng, HEAD_DIM, 1), "o_norm": full["o_norm"], "b1a": full["o_pw1_b"][:, :d], "b1b": full["o_pw1_b"][:, d:],
        "dw": full["o_dw"][0], "dw_b": full["o_dw_b"], "ln_g": full["o_ln_g"], "ln_b": full["o_ln_b"], "b2": full["o_pw2_b"],
        "f_norm0": f_norm[0:1], "f_norm1": f_norm[1:2], "final_norm": final_norm.reshape(1, d),
    }

    pending = {}

    def reduce(group, event, gb, after):
        names = REDUCE_GROUPS[group]
        if event == "ready":
            arrs = [gb[nme] if gb[nme].shape[0] == N_CHIPS else gb[nme].reshape(N_CHIPS, -1, d) for nme in names]
            pending[group] = split_start(_pair_copies, arrs, [(N_CHIPS, g.shape[1] // 2, g.shape[2]) for g in arrs], len(arrs),
                                         name="reduce_sibling_start_" + group)
            return pending[group]["token"]
        mine, theirs = split_wait(pending[group], after, name="reduce_sibling_wait_" + group)
        pairs = [pre_reduce(gd, sb, who, name="reduce_pair_" + nme) for nme, gd, sb in zip(names, mine, theirs)]
        sums = [p[0] for p in pairs]
        pending[group] = {"own": [p[1] for p in pairs],
                          "scatter": split_start(_scatter_copies, sums, [(3,) + s.shape[1:] for s in sums], 3 * len(sums),
                                                 name="reduce_chips_start_" + group)}
        return pending[group]["scatter"]["token"]

    loss, grad_x, gb, gs = local_step(x[0], loss_target[0], wb, ws, arrive, reduce)

    gm, gba = gb["w_main_t"][0], gb["w_ba_t"][0]
    g_in_t = jnp.concatenate([gm[:c_ba], gba[:2 * nh], gm[c_ba:]], axis=0).reshape(N_CHIPS, sh_in, d)
    g_in_padded = jnp.pad(g_in_t, ((0, 0), (0, pad_in), (0, 0)))
    in_pair = split_start(_pair_copies, [g_in_padded], [(N_CHIPS, (sh_in + pad_in) // 2, d)], 1, name="reduce_sibling_start_w_in")
    small_global = {
        "e_norm": gs["e_norm"], "e_conv_w": gs["conv_w"][None], "e_a_log": gs["alog"].reshape(1, nh), "e_dt_bias": gs["dtb"].reshape(1, nh),
        "e_o_norm": gs["onorm"], "e_ln_g": gs["lng"].reshape(1, n_bw), "e_ln_b": gs["lnb"].reshape(1, n_bw), "e_w_s": gs["w_s"][None],
        "e_b_s": gs["bs"].reshape(1, ng, HEAD_DIM), "o_norm": gs["o_norm"], "o_pw1_b": jnp.concatenate([gs["b1a"], gs["b1b"]], axis=1),
        "o_dw": gs["dw"][None], "o_dw_b": gs["dw_b"], "o_ln_g": gs["ln_g"], "o_ln_b": gs["ln_b"], "o_pw2_b": gs["b2"],
        "f_norm": jnp.concatenate([gs["f_norm0"], gs["f_norm1"]], axis=0), "final_norm": gs["final_norm"].reshape(d),
    }
    packed = _pack([small_global[n] for n in SMALL] + [loss])
    everyone = split_start(_everyone_copies, [packed], [(8,) + packed.shape], 7, name="reduce_small_start")

    who_then = who + (in_pair["token"][0, 0] + everyone["token"][0, 0]).astype(jnp.int32)
    half = {}
    for group, names in REDUCE_GROUPS.items():
        _, landed = split_wait(pending[group]["scatter"], grad_x, name="reduce_chips_wait_" + group)
        for nme, ow, rc in zip(names, pending[group]["own"], landed):
            half[nme] = final_reduce(ow, rc, who_then, name="reduce_sum_" + nme)
    r_out, r_pw1, r_pw2, r_w1_0, r_w1_1, r_w2_0, r_w2_1 = sibling_join_halves(
        [half[nme] for nme in ("w_out", "pw1", "pw2", "w1_0", "w1_1", "w2_0", "w2_1")], name="reduce_join")
    big_grads = {"e_w_out": [r_out], "o_pw1": [r_pw1], "o_pw2": [r_pw2], "f_w1": [r_w1_0, r_w1_1], "f_w2": [r_w2_0, r_w2_1]}

    mine, theirs = split_wait(in_pair, r_out, name="reduce_sibling_wait_w_in")
    in_sum, in_own = pre_reduce(mine[0], theirs[0], who, name="reduce_pair_w_in")
    in_scatter = split_start(_scatter_copies, [in_sum], [(3,) + in_sum.shape[1:]], 3, name="reduce_chips_start_w_in")
    out = {}
    prev = in_scatter["token"]
    for n in BIG:
        if n != "e_w_in":
            out[n] = adamw(a[n], big_grads[n], a["m_" + n], a["v_" + n], name="adamw_" + n, behind=prev)
            prev = out[n][1][0, :SUBLANES, :LANES]

    own_packed, landed = split_wait(everyone, prev, name="reduce_small_wait")
    summed = sum8(landed[0], own_packed[0], (4 * xi + 2 * yi + ci).astype(jnp.int32).reshape(1), name="reduce_small_sum")
    *small_full, total = _unpack(summed.reshape(-1), [small_global[n].shape for n in SMALL] + [()])
    small_grads = {}
    for n, g in zip(SMALL, small_full):
        if n in SMALL_SHARDED:
            width = a[n].shape[-1]
            g = lax.dynamic_slice_in_dim(g, chip * width, width, axis=g.ndim - 1)
        small_grads[n] = g
    sw, sm, sv, sg = (_pack([src[n] for n in SMALL])[None] for src in
                      ({n: a[n] for n in SMALL}, {n: a["m_" + n] for n in SMALL}, {n: a["v_" + n] for n in SMALL}, small_grads))
    res = adamw(sw, [sg[0]], sm, sv, name="adamw_small")

    _, landed = split_wait(in_scatter, res[1], name="reduce_chips_wait_w_in")
    (r_in,) = sibling_join_halves([final_reduce(in_own, landed[0], who, name="reduce_sum_w_in")], name="reduce_join_w_in")
    w_in_res = adamw(e_w_in.transpose(0, 2, 1), [r_in[:sh_in]], m_e_w_in.transpose(0, 2, 1), v_e_w_in.transpose(0, 2, 1),
                     name="adamw_e_w_in")
    out["e_w_in"] = tuple(r.transpose(0, 2, 1) for r in w_in_res)
    shapes = [a[n].shape for n in SMALL]
    unpacked = [_unpack(r.reshape(-1), shapes) for r in res]
    for i, n in enumerate(SMALL):
        out[n] = tuple(u[i] for u in unpacked)

    result = [total, grad_x[None]]
    for k in range(4):
        result += [out[n][k] for n in WEIGHTS]
    return tuple(result)
```

```python
import functools
import math

import jax
import jax.numpy as jnp
import numpy as np
from jax import lax
from jax.experimental import pallas as pl
from jax.experimental.pallas import tpu as pltpu

F32 = jnp.float32
BF16 = jnp.bfloat16

EPS = 1e-6
CHUNK = 64
PAIR = 2 * CHUNK
HEAD_DIM = 128
N_CHIPS = 4
ADAM_LR, ADAM_B1, ADAM_B2, ADAM_EPS, ADAM_WD, ADAM_STEP = 0.001, 0.9, 0.999, 1e-08, 0.01, 10

VMEM_LIMIT = 56 * 1024 * 1024


def _cparams(sem=None):
    return pltpu.CompilerParams(dimension_semantics=sem, vmem_limit_bytes=VMEM_LIMIT)


def _pick(n, prefs):
    for p in prefs:
        if n % p == 0:
            return p
    return n


def mm_nn(a, w, *, name, epilogue=None, add=None, out_dtype=F32, behind=None):
    m, k = a.shape
    s, _, ns = w.shape
    n = s * ns
    tm = _pick(m, (1024, 512, 256, 128))
    tn = _pick(ns, (1024, 512, 256, 128))
    tk = _pick(k, (2048, 1024, 512, 256, 128))
    nk = k // tk
    npb = ns // tn
    assert add is None or epilogue is None

    def body(a_ref, w_ref, *rest):
        if add is not None:
            add_ref, rest = rest[0], rest[1:]
        if behind is not None:
            rest = rest[1:]
        if epilogue == "relu2":
            o1_ref, o2_ref = rest[0], rest[1]
            acc_ref = rest[2] if nk > 1 else None
        else:
            o1_ref = rest[0]
            acc_ref = rest[1] if nk > 1 else None
        def finish(c, rows=slice(None)):
            if epilogue == "relu2":
                r = jnp.maximum(c, 0.0)
                o1_ref[rows, :] = r.astype(o1_ref.dtype)
                o2_ref[rows, :] = (r * r).astype(o2_ref.dtype)
            elif add is not None:
                o1_ref[rows, :] = (c + add_ref[rows, :].astype(F32)).astype(o1_ref.dtype)
            else:
                o1_ref[rows, :] = c.astype(o1_ref.dtype)

        part = jnp.dot(a_ref[...], w_ref[...], preferred_element_type=F32)
        if nk == 1:
            finish(part)
        else:
            kk = pl.program_id(2)

            @pl.when(kk == 0)
            def _():
                acc_ref[...] = part

            @pl.when(kk > 0)
            def _():
                acc_ref[...] += part

            @pl.when(kk == nk - 1)
            def _():
                finish(acc_ref[...])

    o_spec = pl.BlockSpec((tm, tn), lambda i, j, kk: (i, j))
    if epilogue == "relu2":
        out_shape = (jax.ShapeDtypeStruct((m, n), BF16), jax.ShapeDtypeStruct((m, n), BF16))
        out_specs = (o_spec, o_spec)
    else:
        out_shape = jax.ShapeDtypeStruct((m, n), out_dtype)
        out_specs = o_spec
    return pl.pallas_call(
        body, name=name, out_shape=out_shape,
        grid=(m // tm, n // tn, nk),
        in_specs=[pl.BlockSpec((tm, tk), lambda i, j, kk: (i, kk)),
                  pl.BlockSpec((None, tk, tn), lambda i, j, kk: (j // npb, kk, j % npb))] + ([o_spec] if add is not None else [])
        + ([pl.BlockSpec(behind.shape, lambda i, j, kk: (0, 0))] if behind is not None else []),
        out_specs=out_specs,
        scratch_shapes=[pltpu.VMEM((tm, tn), F32)] if nk > 1 else [],
        compiler_params=_cparams(("parallel", "parallel", "arbitrary")),
    )(*([a, w] + ([add] if add is not None else []) + ([behind] if behind is not None else [])))


def mm_nt(a, w, *, name, mul=None, add=None, out_dtype=F32):
    assert mul is None or add is None
    if add is not None:
        mul = add
    m, n = a.shape
    s, k, ns = w.shape
    assert n == s * ns
    tm = _pick(m, (1024, 512, 256, 128))
    tko = _pick(k, (1024, 512, 256, 128))
    tn = _pick(ns, (2048, 1024, 512, 256, 128))
    nn = n // tn
    npb = ns // tn

    def body(a_ref, w_ref, *rest):
        if mul is not None:
            m_ref, o_ref = rest[0], rest[1]
            acc_ref = rest[2] if nn > 1 else None
        else:
            m_ref, o_ref = None, rest[0]
            acc_ref = rest[1] if nn > 1 else None
        part = lax.dot_general(a_ref[...], w_ref[...], (((1,), (1,)), ((), ())), preferred_element_type=F32)

        def finish(c):
            if add is not None:
                c = c + m_ref[...].astype(F32)
            elif m_ref is not None:
                c = c * (2.0 * m_ref[...].astype(F32))
            o_ref[...] = c.astype(o_ref.dtype)

        if nn == 1:
            finish(part)
        else:
            kk = pl.program_id(2)

            @pl.when(kk == 0)
            def _():
                acc_ref[...] = part

            @pl.when(kk > 0)
            def _():
                acc_ref[...] += part

            @pl.when(kk == nn - 1)
            def _():
                finish(acc_ref[...])

    in_specs = [pl.BlockSpec((tm, tn), lambda i, j, kk: (i, kk)),
                pl.BlockSpec((None, tko, tn), lambda i, j, kk: (kk // npb, j, kk % npb))]
    args = [a, w]
    if mul is not None:
        in_specs.append(pl.BlockSpec((tm, tko), lambda i, j, kk: (i, j)))
        args.append(mul)
    return pl.pallas_call(
        body, name=name, out_shape=jax.ShapeDtypeStruct((m, k), out_dtype),
        grid=(m // tm, k // tko, nn),
        in_specs=in_specs,
        out_specs=pl.BlockSpec((tm, tko), lambda i, j, kk: (i, j)),
        scratch_shapes=[pltpu.VMEM((tm, tko), F32)] if nn > 1 else [],
        compiler_params=_cparams(("parallel", "parallel", "arbitrary")),
    )(*args)


def mm_tn(a, b, *, groups, name):
    m, k = a.shape
    _, n = b.shape
    ns = n // groups
    tko = _pick(k, (1024, 512, 256, 128))
    tn = _pick(ns, (1024, 512, 256, 128))
    tc = _pick(m, (2048, 1024, 512, 256, 128))
    nc = m // tc
    npb = ns // tn

    def body(a_ref, b_ref, o_ref):
        part = lax.dot_general(a_ref[...], b_ref[...], (((0,), (0,)), ((), ())), preferred_element_type=F32)
        kk = pl.program_id(2)

        @pl.when(kk == 0)
        def _():
            o_ref[...] = part

        @pl.when(kk > 0)
        def _():
            o_ref[...] += part

    return pl.pallas_call(
        body, name=name, out_shape=jax.ShapeDtypeStruct((groups, k, ns), F32),
        grid=(k // tko, n // tn, nc),
        in_specs=[pl.BlockSpec((tc, tko), lambda i, j, kk: (kk, i)),
                  pl.BlockSpec((tc, tn), lambda i, j, kk: (kk, j))],
        out_specs=pl.BlockSpec((None, tko, tn), lambda i, j, kk: (j // npb, i, j % npb)),
        compiler_params=_cparams(("parallel", "parallel", "arbitrary")),
    )(a, b)


ROWS = 128


def _full_spec(arr):
    nd = arr.ndim
    return pl.BlockSpec(arr.shape, lambda i, _nd=nd: (0,) * _nd)


def _row_spec(x, rows):
    if isinstance(x, tuple):
        _, w, cb = x
        return pl.BlockSpec((rows, w), lambda i, _cb=cb: (i, _cb))
    return pl.BlockSpec((rows, x.shape[1]), lambda i: (i, 0))


def _arr(x):
    return x[0] if isinstance(x, tuple) else x


def _width(x):
    return x[1] if isinstance(x, tuple) else x.shape[1]


def rows_fwd(fn, params, xs, stores, *, name, rows=ROWS):
    t = _arr(xs[0]).shape[0]
    np_, nx = len(params), len(xs)

    def body(*refs):
        p_refs, x_refs, o_refs = refs[:np_], refs[np_:np_ + nx], refs[np_ + nx:]
        outs = fn(*[r[...].astype(F32) for r in p_refs], *[r[...].astype(F32) for r in x_refs])
        for (idx, dt, _), o_ref in zip(stores, o_refs):
            o_ref[...] = outs[idx].astype(dt)

    res = pl.pallas_call(
        body, name=name,
        out_shape=tuple(jax.ShapeDtypeStruct((t, w), dt) for _, dt, w in stores),
        grid=(t // rows,),
        in_specs=[_full_spec(p) for p in params] + [_row_spec(x, rows) for x in xs],
        out_specs=tuple(pl.BlockSpec((rows, w), lambda i: (i, 0)) for _, _, w in stores),
        compiler_params=_cparams(("parallel",)),
    )(*params, *[_arr(x) for x in xs])
    return res


def rows_bwd(fn, params, xs, cts, dx_stores, *, name, rows=ROWS, concat=False):
    t = _arr(xs[0]).shape[0]
    np_, nx = len(params), len(xs)
    ct_idx = [i for i, c in enumerate(cts) if c is not None]
    ct_arrs = [cts[i] for i in ct_idx]
    nct = len(ct_arrs)
    nds = 1 if concat else len(dx_stores)
    widths = [_width(xs[xi]) for xi, _ in dx_stores]

    def body(*refs):
        p_refs = refs[:np_]
        x_refs = refs[np_:np_ + nx]
        c_refs = refs[np_ + nx:np_ + nx + nct]
        d_refs = refs[np_ + nx + nct:np_ + nx + nct + nds]
        g_refs = refs[np_ + nx + nct + nds:]
        pv = [r[...].astype(F32) for r in p_refs]
        xv = [r[...].astype(F32) for r in x_refs]
        outs, vjp = jax.vjp(lambda *a: tuple(fn(*a)), *pv, *xv)
        ct_full = [jnp.zeros_like(o) for o in outs]
        for i, r in zip(ct_idx, c_refs):
            ct_full[i] = r[...].astype(F32)
        grads = vjp(tuple(ct_full))
        if concat:
            off = 0
            for (xi, dt), wd in zip(dx_stores, widths):
                d_refs[0][:, off:off + wd] = grads[np_ + xi].astype(dt)
                off += wd
        else:
            for (xi, dt), d_ref in zip(dx_stores, d_refs):
                d_ref[...] = grads[np_ + xi].astype(dt)
        step = pl.program_id(0)
        for j, g_ref in enumerate(g_refs):
            @pl.when(step == 0)
            def _(g_ref=g_ref, j=j):
                g_ref[...] = grads[j]

            @pl.when(step > 0)
            def _(g_ref=g_ref, j=j):
                g_ref[...] += grads[j]

    dx_shapes = [(sum(widths), dx_stores[0][1])] if concat else [(wd, dt) for wd, (_, dt) in zip(widths, dx_stores)]
    out_shape = tuple(jax.ShapeDtypeStruct((t, wd), dt) for wd, dt in dx_shapes) + \
        tuple(jax.ShapeDtypeStruct(p.shape, F32) for p in params)
    out_specs = tuple(pl.BlockSpec((rows, wd), lambda i: (i, 0)) for wd, _ in dx_shapes) + \
        tuple(_full_spec(p) for p in params)
    return pl.pallas_call(
        body, name=name, out_shape=out_shape, grid=(t // rows,),
        in_specs=[_full_spec(p) for p in params] + [_row_spec(x, rows) for x in xs] + [_row_spec(c, rows) for c in ct_arrs],
        out_specs=out_specs,
        compiler_params=_cparams(("arbitrary",)),
    )(*params, *[_arr(x) for x in xs], *[_arr(c) for c in ct_arrs])


def _rms(x, g):
    return x * lax.rsqrt(jnp.mean(x * x, axis=-1, keepdims=True) + EPS) * g


def _sigmoid(x):
    return 1.0 / (1.0 + jnp.exp(-x))


def _silu(x):
    return x * _sigmoid(x)


def _gelu(x):
    return 0.5 * x * (1.0 + lax.erf(x * (1.0 / math.sqrt(2.0))))


def stage_norm(g, x):
    return x, _rms(x, g)


def stage_res_norm(g, x, y):
    xn = x + y
    return xn, _rms(xn, g)


def stage_res_bias_norm(b, g, x, y):
    xn = x + y + b
    return xn, _rms(xn, g)


def stage_glu(ba, bb, za, zb):
    return ((za + ba) * _sigmoid(zb + bb),)


def stage_ln_silu(dw_b, ln_g, ln_b, cv):
    z = cv + dw_b
    mu = jnp.mean(z, axis=-1, keepdims=True)
    zc = z - mu
    y = zc * lax.rsqrt(jnp.mean(zc * zc, axis=-1, keepdims=True) + EPS) * ln_g + ln_b
    return (_silu(y),)


CONV_ROWS = 128
CONV_COLS = 256
SUBLANES = 8


def _halo(k):
    return SUBLANES * ((k - 1 + SUBLANES - 1) // SUBLANES)


def _taps_by_roll(k):
    out = {}
    for s in range(k):
        out.setdefault(s % SUBLANES, []).append((s // SUBLANES, s))
    return out


def _shifted_down(win, k, r):
    halo, n = _halo(k), win.shape[0]
    segs = {}
    for b, lst in _taps_by_roll(k).items():
        rolled = win if b == 0 else pltpu.roll(win, b, axis=0)
        for a, s in lst:
            segs[s] = rolled[halo - SUBLANES * a: halo - SUBLANES * a + r]
    return segs


def _shifted_up(win, k, r):
    n = win.shape[0]
    segs = {}
    for b, lst in _taps_by_roll(k).items():
        rolled = win if b == 0 else pltpu.roll(win, n - b, axis=0)
        for a, s in lst:
            segs[s] = rolled[SUBLANES * a: SUBLANES * a + r]
    return segs


def _for_blocks(nblk, fn):
    fn(0, True, nblk == 1)
    if nblk > 2:
        def step(i, c):
            fn(i, False, False)
            return c
        lax.fori_loop(1, nblk - 1, step, 0)
    if nblk > 1:
        fn(nblk - 1, False, True)


def _base(i, r):
    return i * r if isinstance(i, int) else pl.multiple_of(i * r, r)


def _win_top(ref, i, first, r, halo):
    if first:
        return jnp.concatenate([jnp.zeros((halo, ref.shape[1]), F32), ref[pl.ds(0, r), :]], axis=0)
    base = _base(i, r)
    return ref[pl.ds(base - halo, r + halo), :]


def _win_bottom(ref, i, last, r, halo):
    base = _base(i, r)
    if last:
        return jnp.concatenate([ref[pl.ds(base, r), :], jnp.zeros((halo, ref.shape[1]), F32)], axis=0)
    return ref[pl.ds(base, r + halo), :]


def conv_fwd(x, w, *, mode, name, g=None):
    t = x.shape[0]
    k, c = w.shape
    r, cb, halo = min(CONV_ROWS, t), min(CONV_COLS, c), _halo(k)
    nblk = t // r

    def body(*refs):
        if mode == "silu_bwd":
            x_ref, w_ref, g_ref, o_ref = refs
        else:
            x_ref, w_ref, o_ref = refs

        def blk(i, first, last):
            segs = _shifted_down(_win_top(x_ref, i, first, r, halo), k, r)
            acc = None
            for s in range(k):
                term = w_ref[pl.ds(k - 1 - s, 1), :] * segs[s]
                acc = term if acc is None else acc + term
            base = _base(i, r)
            if mode == "silu":
                acc = _silu(acc)
            elif mode == "silu_bwd":
                sg = _sigmoid(acc)
                acc = g_ref[pl.ds(base, r), :] * (sg * (1.0 + acc * (1.0 - sg)))
            o_ref[pl.ds(base, r), :] = acc

        _for_blocks(nblk, blk)

    col = pl.BlockSpec((t, cb), lambda j: (0, j))
    in_specs = [col, pl.BlockSpec((k, cb), lambda j: (0, j))] + ([col] if mode == "silu_bwd" else [])
    args = [x, w] + ([g] if mode == "silu_bwd" else [])
    return pl.pallas_call(
        body, name=name, out_shape=jax.ShapeDtypeStruct((t, c), F32), grid=(c // cb,),
        in_specs=in_specs, out_specs=col, compiler_params=_cparams(("parallel",)),
    )(*args)


def conv_bwd(x, w, dy, *, name, into=None):
    t = x.shape[0]
    k, c = w.shape
    r, cb, halo = min(CONV_ROWS, t), min(CONV_COLS, c), _halo(k)
    nblk = t // r

    def body(x_ref, w_ref, dy_ref, *rest):
        dx_ref, dw_ref = rest[-2], rest[-1]
        dw_ref[...] = jnp.zeros_like(dw_ref)

        def blk(i, first, last):
            base = _base(i, r)
            up = _shifted_up(_win_bottom(dy_ref, i, last, r, halo), k, r)
            down = _shifted_down(_win_top(x_ref, i, first, r, halo), k, r)
            dyb = up[0]
            acc = None
            for s in range(k):
                term = w_ref[pl.ds(k - 1 - s, 1), :] * up[s]
                acc = term if acc is None else acc + term
                dw_ref[pl.ds(k - 1 - s, 1), :] += jnp.sum(down[s] * dyb, axis=0, keepdims=True)
            dx_ref[pl.ds(base, r), :] = acc.astype(dx_ref.dtype)

        _for_blocks(nblk, blk)

    col = pl.BlockSpec((t, cb), lambda j: (0, j))
    wsp = pl.BlockSpec((k, cb), lambda j: (0, j))
    dx_shape = jax.ShapeDtypeStruct((t, c), F32) if into is None else jax.ShapeDtypeStruct(into.shape, into.dtype)
    return pl.pallas_call(
        body, name=name,
        out_shape=(dx_shape, jax.ShapeDtypeStruct((k, c), F32)), grid=(c // cb,),
        in_specs=[col, wsp, col] + ([] if into is None else [pl.BlockSpec(memory_space=pl.ANY)]),
        out_specs=(col, wsp),
        input_output_aliases={} if into is None else {3: 0},
        compiler_params=_cparams(("parallel",)),
    )(*([x, w, dy] + ([] if into is None else [into])))


_DIMS = {"nn": (((1,), (0,)), ((), ())), "nt": (((1,), (1,)), ((), ())), "tn": (((0,), (0,)), ((), ()))}
_DIMS_BATCHED = {"nn": (((2,), (1,)), ((0,), (0,))), "nt": (((2,), (2,)), ((0,), (0,))), "tn": (((1,), (1,)), ((0,), (0,)))}


def _mxu(a, b, mode):
    dims = _DIMS_BATCHED if a.ndim == 3 else _DIMS
    return lax.dot_general(a, b, dims[mode], preferred_element_type=F32)


def _split(x):
    hi = x.astype(BF16)
    return hi, (x - hi.astype(F32)).astype(BF16)


def _dot_raw(a, b, mode, prec):
    if prec == "bf16":
        return _mxu(a.astype(BF16), b.astype(BF16), mode)
    if prec == "x3":
        ah, al = _split(a)
        bh, bl = _split(b)
        return _mxu(ah, bh, mode) + (_mxu(ah, bl, mode) + _mxu(al, bh, mode))
    if prec == "x3r":
        bh, bm = _split(b)
        bl = (b - bh.astype(F32) - bm.astype(F32)).astype(BF16)
        ah = a.astype(BF16)
        return _mxu(ah, bh, mode) + (_mxu(ah, bm, mode) + _mxu(ah, bl, mode))
    raise ValueError(prec)


@functools.lru_cache(maxsize=None)
def _dot_fn(mode, prec):
    bprec = "x3" if prec == "x3r" else prec

    @jax.custom_vjp
    def f(a, b):
        return _dot_raw(a, b, mode, prec)

    def fwd(a, b):
        return _dot_raw(a, b, mode, prec), (a, b)

    def bwd(res, ct):
        a, b = res
        if mode == "nn":
            return _dot_raw(ct, b, "nt", bprec), _dot_raw(a, ct, "tn", bprec)
        if mode == "nt":
            return _dot_raw(ct, b, "nn", bprec), _dot_raw(ct, a, "tn", bprec)
        return _dot_raw(b, ct, "nt", bprec), _dot_raw(a, ct, "nn", bprec)

    f.defvjp(fwd, bwd)
    return f


def _dot(a, b, mode="nn", prec="bf16"):
    return _dot_fn(mode, prec)(a, b)


def _inv_product(l):
    n = l.shape[-1]
    eye = (lax.broadcasted_iota(jnp.int32, (n, n), 0) == lax.broadcasted_iota(jnp.int32, (n, n), 1)).astype(F32)
    p = eye - l
    pw = l
    for _ in range(5):
        pw = _dot_raw(pw, pw, "nn", "x3")
        p = _dot_raw(p, eye + pw, "nn", "x3")
    return p


@jax.custom_vjp
def _inv_unit_lower(l, t_saved):
    return t_saved


def _inv_fwd(l, t_saved):
    return t_saved, t_saved


def _inv_bwd(t, ct):
    tmp = _dot_raw(t, ct, "tn", "x3")
    return -_dot_raw(tmp, t, "nt", "x3"), jnp.zeros_like(t)


_inv_unit_lower.defvjp(_inv_fwd, _inv_bwd)


def _softplus(x):
    pos = x > 0
    return jnp.where(pos, x, 0.0) + jnp.log(1.0 + jnp.exp(jnp.where(pos, -x, x)))


def _l2n(x):
    return x * lax.rsqrt(jnp.sum(x * x, axis=-1, keepdims=True) + EPS)


def delta_pair(s0, qc, kc, vc, z, pba, alog, dtb, onorm, t_saved=None):
    n = PAIR
    nh = qc.shape[0]
    assert qc.shape == (nh, n, HEAD_DIM) and n == HEAD_DIM
    hi = lax.broadcasted_iota(jnp.int32, (nh, 1, pba.shape[1]), 0)
    li = lax.broadcasted_iota(jnp.int32, (nh, 1, pba.shape[1]), 2)
    braw = jnp.sum(pba[None] * (li == hi).astype(F32), axis=2, keepdims=True)
    araw = jnp.sum(pba[None] * (li == hi + nh).astype(F32), axis=2, keepdims=True)
    ri = lax.broadcasted_iota(jnp.int32, (n, n), 0)
    ci = lax.broadcasted_iota(jnp.int32, (n, n), 1)
    same = (ri // CHUNK) == (ci // CHUNK)
    tri = same & (ci <= ri)
    same_f = jnp.broadcast_to(same.astype(F32), (nh, n, n))
    tri_f = jnp.broadcast_to(tri.astype(F32), (nh, n, n))
    strict_f = (same & (ci < ri)).astype(F32)
    m0 = (lax.broadcasted_iota(jnp.int32, (n, 1), 0) < CHUNK).astype(F32)
    m1 = 1.0 - m0

    q = _l2n(qc) * (HEAD_DIM ** -0.5)
    k = _l2n(kc)
    beta = _sigmoid(braw)
    g = -jnp.exp(alog) * _softplus(araw + dtb)
    gb = jnp.broadcast_to(g, (nh, n, n))
    gc = _dot(tri_f, gb, "nn", "x3r")
    gtot = _dot(same_f, gb, "nn", "x3r")
    decay = jnp.exp(jnp.where(tri, gc - jnp.swapaxes(gc, 1, 2), -1e30))
    eg = jnp.exp(gc)
    kb, vb = k * beta, vc * beta
    l = _dot(kb, k, "nt") * decay * strict_f
    if t_saved is None:
        tinv = _inv_product(l)
    else:
        tinv = _inv_unit_lower(l, t_saved)
    u = _dot(tinv, vb, "nn", "x3")
    w = _dot(tinv, kb * eg, "nn", "x3")
    attn = _dot(q, k, "nt") * decay
    q_dec = q * eg
    k_tail = k * jnp.exp(gtot - gc)
    gl0 = jnp.exp(jnp.sum(gb * m0, axis=1, keepdims=True))
    gl1 = jnp.exp(jnp.sum(gb * m1, axis=1, keepdims=True))

    vn0 = m0 * (u - _dot(w, s0))
    s1 = s0 * gl0 + _dot(k_tail, vn0, "tn")
    vn1 = m1 * (u - _dot(w, s1))
    o = m0 * _dot(q_dec, s0) + m1 * _dot(q_dec, s1) + _dot(attn, vn0 + vn1)
    s2 = s1 * gl1 + _dot(k_tail, vn1, "tn")

    on = o * lax.rsqrt(jnp.mean(o * o, axis=-1, keepdims=True) + EPS) * onorm
    return on * _silu(z), s2, tinv


def _hcols(i):
    return slice(i * HEAD_DIM, (i + 1) * HEAD_DIM)


def _heads(ref, first, count):
    return jnp.stack([ref[:, _hcols(first + i)] for i in range(count)])


def delta_fwd(qkv, pm, pba, alog, dtb, onorm, cat_width, *, name):
    t = qkv.shape[0]
    h = alog.shape[0]
    hd = h * HEAD_DIM
    npair = t // PAIR
    mat = pl.BlockSpec((h, None, PAIR, HEAD_DIM), lambda p: (0, p, 0, 0))
    par = pl.BlockSpec((h, 1, 1), lambda p: (0, 0, 0))

    def body(qkv_ref, z_ref, pba_ref, al_ref, dt_ref, on_ref, o_ref, st_ref, ti_ref, s_scr):
        @pl.when(pl.program_id(0) == 0)
        def _():
            s_scr[...] = jnp.zeros_like(s_scr)

        s0 = s_scr[...]
        st_ref[...] = s0
        out, s2, tinv = delta_pair(s0, _heads(qkv_ref, 0, h), _heads(qkv_ref, h, h), _heads(qkv_ref, 2 * h, h), _heads(z_ref, 0, h),
                                   pba_ref[...], al_ref[...], dt_ref[...], on_ref[...])
        for i in range(h):
            o_ref[:, _hcols(i)] = out[i].astype(o_ref.dtype)
        ti_ref[...] = tinv
        s_scr[...] = s2

    return pl.pallas_call(
        body, name=name, grid=(npair,),
        out_shape=(jax.ShapeDtypeStruct((t, cat_width), BF16),
                   jax.ShapeDtypeStruct((h, npair, PAIR, HEAD_DIM), F32),
                   jax.ShapeDtypeStruct((h, npair, PAIR, PAIR), F32)),
        in_specs=[pl.BlockSpec((PAIR, 3 * hd), lambda p: (p, 0)), pl.BlockSpec((PAIR, hd), lambda p: (p, 3)),
                  pl.BlockSpec((PAIR, pba.shape[1]), lambda p: (p, 0)), par, par,
                  pl.BlockSpec((1, HEAD_DIM), lambda p: (0, 0))],
        out_specs=(pl.BlockSpec((PAIR, hd), lambda p: (p, 0)), mat, mat),
        scratch_shapes=[pltpu.VMEM((h, HEAD_DIM, HEAD_DIM), F32)],
        compiler_params=_cparams(("arbitrary",)),
    )(qkv, pm, pba, alog, dtb, onorm)


def delta_bwd(qkv, pm, pba, alog, dtb, onorm, states, tinvs, dcat, *, name):
    t = qkv.shape[0]
    h = alog.shape[0]
    hd = h * HEAD_DIM
    npair = t // PAIR
    rev = lambda p: npair - 1 - p
    mat = pl.BlockSpec((h, None, PAIR, HEAD_DIM), lambda p: (0, rev(p), 0, 0))
    par = pl.BlockSpec((h, 1, 1), lambda p: (0, 0, 0))
    onsp = pl.BlockSpec((1, HEAD_DIM), lambda p: (0, 0))
    wide = pl.BlockSpec((PAIR, 3 * hd), lambda p: (rev(p), 0))
    zsp = pl.BlockSpec((PAIR, hd), lambda p: (rev(p), 3))
    bsp = pl.BlockSpec((PAIR, pba.shape[1]), lambda p: (rev(p), 0))

    def body(qkv_ref, z_ref, pba_ref, al_ref, dt_ref, on_ref, st_ref, ti_ref, dc_ref,
             dqkv_ref, dz_ref, dpba_ref, dal_ref, ddt_ref, don_ref, ds_scr):
        @pl.when(pl.program_id(0) == 0)
        def _():
            ds_scr[...] = jnp.zeros_like(ds_scr)
            dal_ref[...] = jnp.zeros_like(dal_ref)
            ddt_ref[...] = jnp.zeros_like(ddt_ref)
            don_ref[...] = jnp.zeros_like(don_ref)

        tsv = ti_ref[...]
        fn = lambda s0, qc, kc, vc, z, pb, al, dt, on: delta_pair(s0, qc, kc, vc, z, pb, al, dt, on, tsv)[:2]
        _, vjp = jax.vjp(fn, st_ref[...], _heads(qkv_ref, 0, h), _heads(qkv_ref, h, h), _heads(qkv_ref, 2 * h, h),
                         _heads(z_ref, 0, h), pba_ref[...], al_ref[...], dt_ref[...], on_ref[...])
        ds0, dq, dk, dv, dz, dpba, dal, ddt, don = vjp((_heads(dc_ref, 0, h), ds_scr[...]))
        ds_scr[...] = ds0
        for i in range(h):
            dqkv_ref[:, _hcols(i)] = dq[i]
            dqkv_ref[:, _hcols(h + i)] = dk[i]
            dqkv_ref[:, _hcols(2 * h + i)] = dv[i]
            dz_ref[:, _hcols(i)] = dz[i].astype(dz_ref.dtype)
        dal_ref[...] += dal
        ddt_ref[...] += ddt
        dpba_ref[...] = dpba.astype(dpba_ref.dtype)
        don_ref[...] += don

    return pl.pallas_call(
        body, name=name, grid=(npair,),
        out_shape=(jax.ShapeDtypeStruct((t, 3 * hd), F32), jax.ShapeDtypeStruct(pm.shape, BF16),
                   jax.ShapeDtypeStruct(pba.shape, BF16),
                   jax.ShapeDtypeStruct((h, 1, 1), F32), jax.ShapeDtypeStruct((h, 1, 1), F32),
                   jax.ShapeDtypeStruct((1, HEAD_DIM), F32)),
        in_specs=[wide, zsp, bsp, par, par, onsp, mat, mat, pl.BlockSpec((PAIR, hd), lambda p: (rev(p), 0))],
        out_specs=(wide, zsp, bsp, par, par, onsp),
        scratch_shapes=[pltpu.VMEM((h, HEAD_DIM, HEAD_DIM), F32)],
        compiler_params=_cparams(("arbitrary",)),
    )(qkv, pm, pba, alog, dtb, onorm, states, tinvs, dcat)


def gmlp_block(ln_g, ln_b, w, bcol, u_raw, v_raw):
    n = w.shape[-1]
    ri = lax.broadcasted_iota(jnp.int32, (n, n), 0)
    ci = lax.broadcasted_iota(jnp.int32, (n, n), 1)
    mask = ((ci // CHUNK) <= (ri // CHUNK)).astype(F32)
    vg = _gelu(v_raw)
    vc = vg - jnp.mean(vg, axis=-1, keepdims=True)
    vgn = vc * lax.rsqrt(jnp.mean(vc * vc, axis=-1, keepdims=True) + EPS) * ln_g + ln_b
    return _gelu(u_raw) * (_dot(w * mask, vgn) + bcol)


def gmlp_fwd(pm, ln_g, ln_b, w_s, bcol, cat, *, name):
    t = pm.shape[0]
    g = w_s.shape[0]
    gw = g * HEAD_DIM
    assert pm.shape[1] == 6 * gw and cat.shape[1] == 2 * gw

    def body(u_ref, v_ref, lg_ref, lb_ref, w_ref, b_ref, cat_in, o_ref):
        del cat_in
        out = gmlp_block(lg_ref[...], lb_ref[...], w_ref[...], b_ref[...], _heads(u_ref, 0, g), _heads(v_ref, 0, g))
        for i in range(g):
            o_ref[:, _hcols(i)] = out[i].astype(o_ref.dtype)

    full = lambda a: pl.BlockSpec(a.shape, lambda m: (0, 0, 0))
    return pl.pallas_call(
        body, name=name, grid=(t // HEAD_DIM,),
        out_shape=jax.ShapeDtypeStruct(cat.shape, cat.dtype),
        in_specs=[pl.BlockSpec((HEAD_DIM, gw), lambda m: (m, 4)), pl.BlockSpec((HEAD_DIM, gw), lambda m: (m, 5)),
                  full(ln_g), full(ln_b), full(w_s), full(bcol), pl.BlockSpec(memory_space=pl.ANY)],
        out_specs=pl.BlockSpec((HEAD_DIM, gw), lambda m: (m, 1)),
        input_output_aliases={6: 0},
        compiler_params=_cparams(("arbitrary",)),
    )(pm, pm, ln_g, ln_b, w_s, bcol, cat)


def gmlp_bwd(pm, ln_g, ln_b, w_s, bcol, dcat, dpm, *, name):
    t = pm.shape[0]
    g = w_s.shape[0]
    gw = g * HEAD_DIM
    assert pm.shape[1] == 6 * gw and dpm.shape == pm.shape

    def body(u_ref, v_ref, lg_ref, lb_ref, w_ref, b_ref, dc_ref, dpm_in, duv_ref, dlg_ref, dlb_ref, dw_ref, db_ref):
        del dpm_in
        first = pl.program_id(0) == 0
        _, vjp = jax.vjp(gmlp_block, lg_ref[...], lb_ref[...], w_ref[...], b_ref[...], _heads(u_ref, 0, g), _heads(v_ref, 0, g))
        dlg, dlb, dw, db, du, dv = vjp(_heads(dc_ref, 0, g))
        for i in range(g):
            duv_ref[:, _hcols(i)] = du[i].astype(duv_ref.dtype)
            duv_ref[:, _hcols(g + i)] = dv[i].astype(duv_ref.dtype)
        for ref, val in ((dlg_ref, dlg), (dlb_ref, dlb), (dw_ref, dw), (db_ref, db)):
            @pl.when(first)
            def _(ref=ref, val=val):
                ref[...] = val

            @pl.when(jnp.logical_not(first))
            def _(ref=ref, val=val):
                ref[...] += val

    full = lambda a: pl.BlockSpec(a.shape, lambda m: (0, 0, 0))
    return pl.pallas_call(
        body, name=name, grid=(t // HEAD_DIM,),
        out_shape=(jax.ShapeDtypeStruct(dpm.shape, dpm.dtype),
                   jax.ShapeDtypeStruct(ln_g.shape, F32), jax.ShapeDtypeStruct(ln_b.shape, F32),
                   jax.ShapeDtypeStruct(w_s.shape, F32), jax.ShapeDtypeStruct(bcol.shape, F32)),
        in_specs=[pl.BlockSpec((HEAD_DIM, gw), lambda m: (m, 4)), pl.BlockSpec((HEAD_DIM, gw), lambda m: (m, 5)),
                  full(ln_g), full(ln_b), full(w_s), full(bcol), pl.BlockSpec((HEAD_DIM, gw), lambda m: (m, 1)),
                  pl.BlockSpec(memory_space=pl.ANY)],
        out_specs=(pl.BlockSpec((HEAD_DIM, 2 * gw), lambda m: (m, 2)), full(ln_g), full(ln_b), full(w_s), full(bcol)),
        input_output_aliases={7: 0},
        compiler_params=_cparams(("arbitrary",)),
    )(pm, pm, ln_g, ln_b, w_s, bcol, dcat, dpm)


def loss_head(g, x, r, tgt, *, name, rows=ROWS):
    t, d = x.shape

    def body(g_ref, x_ref, r_ref, t_ref, l_ref, dx_ref, dxb_ref, dg_ref):
        y, vjp = jax.vjp(lambda gg, xx: _rms(xx, gg), g_ref[...], x_ref[...] + r_ref[...])
        e = y - t_ref[...]
        part = (0.5 / d) * jnp.sum(jnp.sum(e * e, axis=1, keepdims=True), axis=0, keepdims=True)
        dg, dx = vjp(e * (1.0 / d))
        dx_ref[...] = dx
        dxb_ref[...] = dx.astype(BF16)
        first = pl.program_id(0) == 0

        @pl.when(first)
        def _():
            l_ref[...] = part
            dg_ref[...] = dg

        @pl.when(jnp.logical_not(first))
        def _():
            l_ref[...] += part
            dg_ref[...] += dg

    rs = pl.BlockSpec((rows, d), lambda i: (i, 0))
    gs = pl.BlockSpec((1, d), lambda i: (0, 0))
    return pl.pallas_call(
        body, name=name, grid=(t // rows,),
        out_shape=(jax.ShapeDtypeStruct((1, 1), F32), jax.ShapeDtypeStruct((t, d), F32),
                   jax.ShapeDtypeStruct((t, d), BF16), jax.ShapeDtypeStruct((1, d), F32)),
        in_specs=[gs, rs, rs, rs],
        out_specs=(pl.BlockSpec((1, 1), lambda i: (0, 0)), rs, rs, gs),
        compiler_params=_cparams(("arbitrary",)),
    )(g, x, r, tgt)


def adamw(w, gs, m, v, *, name, behind=None):
    nl, r, c = w.shape
    assert len(gs) == nl
    if r % SUBLANES == 0:
        tr, tc = _pick(r, (256, 128, 64, 32, 16, 8)), c
    else:
        tr, tc = r, _pick(c, (256, 128))
    k1 = 1.0 - ADAM_B1 ** ADAM_STEP
    k2 = 1.0 - ADAM_B2 ** ADAM_STEP

    def body(*refs):
        w_ref, m_ref, v_ref = refs[0], refs[1], refs[2]
        g_refs = refs[3:3 + nl]
        go_ref, d_ref, mo_ref, vo_ref = refs[-4:]
        gg = g_refs[0][...]
        for li in range(1, nl):
            gg = jnp.where(pl.program_id(0) == li, g_refs[li][...], gg)
        mn = ADAM_B1 * m_ref[...] + (1.0 - ADAM_B1) * gg
        vn = ADAM_B2 * v_ref[...] + (1.0 - ADAM_B2) * (gg * gg)
        go_ref[...] = gg
        d_ref[...] = -ADAM_LR * ((mn / k1) / (jnp.sqrt(vn / k2) + ADAM_EPS) + ADAM_WD * w_ref[...])
        mo_ref[...] = mn
        vo_ref[...] = vn

    if tc == c:
        sp = pl.BlockSpec((None, tr, c), lambda l, i: (l, i, 0))
        gsp = pl.BlockSpec((tr, c), lambda l, i: (i, 0))
    else:
        sp = pl.BlockSpec((None, r, tc), lambda l, i: (l, 0, i))
        gsp = pl.BlockSpec((r, tc), lambda l, i: (0, i))
    sds = jax.ShapeDtypeStruct((nl, r, c), F32)
    extra = [] if behind is None else [behind]
    return pl.pallas_call(
        body, name=name, grid=(nl, (r // tr) * (c // tc)), out_shape=(sds, sds, sds, sds),
        in_specs=[sp, sp, sp] + [gsp] * nl + [pl.BlockSpec(e.shape, lambda l, i: (0, 0)) for e in extra], out_specs=(sp, sp, sp, sp),
        compiler_params=_cparams(("parallel", "parallel")),
    )(w, m, v, *gs, *extra)


MESH = pl.DeviceIdType.MESH
ANY = pl.BlockSpec(memory_space=pl.ANY)


def _me():
    return lax.axis_index("x"), lax.axis_index("y"), lax.axis_index("c")


def _other_chips(x, y):
    return [(1 - x, y), (x, 1 - y), (1 - x, 1 - y)]


HBM = pl.BlockSpec(memory_space=pltpu.HBM)
SEM = pl.BlockSpec(memory_space=pltpu.SEMAPHORE)
EFFECT = pltpu.SideEffectType.DATAFLOW_SIDE_EFFECTING


def _hbm(a):
    return pltpu.with_memory_space_constraint(a, pltpu.HBM)


def _half_rows(arr, which):
    hrows = arr.shape[-2] // 2
    return pl.ds(which * hrows, hrows)


def _ici_copies(srcs, lands, sems):
    x, y, c = _me()
    chip = 2 * x + y
    return [pltpu.make_async_remote_copy(src_ref=srcs[i].at[_half_rows(srcs[i], c)], dst_ref=lands[i].at[chip, _half_rows(srcs[i], c)],
                                         send_sem=sems[0].at[3 * i + j], recv_sem=sems[1].at[3 * i + j],
                                         device_id=(ch[0], ch[1], c), device_id_type=MESH)
            for i in range(len(srcs)) for j, ch in enumerate(_other_chips(x, y))]


def _own_copies(srcs, lands, sems):
    x, y, c = _me()
    return [pltpu.make_async_remote_copy(src_ref=srcs[i], dst_ref=lands[i].at[2 * x + y], send_sem=sems[0].at[i],
                                         recv_sem=sems[1].at[i], device_id=(x, y, 1 - c), device_id_type=MESH)
            for i in range(len(srcs))]


def _fwd_copies(srcs, lands, sems, half):
    x, y, c = _me()
    cps = []
    for i in range(len(srcs)):
        for j, ch in enumerate(_other_chips(x, y)):
            part = lands[i].at[2 * ch[0] + ch[1], _half_rows(srcs[i], half)]
            cps.append(pltpu.make_async_remote_copy(src_ref=part, dst_ref=part, send_sem=sems[0].at[3 * i + j],
                                                    recv_sem=sems[1].at[3 * i + j], device_id=(x, y, 1 - c), device_id_type=MESH))
    return cps


def gather_start(groups, *, name):
    arrs = [a for g in groups for a in g]
    n, ng = len(arrs), len(groups)
    bounds = np.cumsum([0] + [len(g) for g in groups])

    def body(*refs):
        srcs, lands = refs[:n], refs[n:2 * n]
        sems = refs[2 * n:2 * n + 4 * ng]
        token = refs[-1]
        for gi in range(ng):
            lo, hi = bounds[gi], bounds[gi + 1]
            for cp in _ici_copies(srcs[lo:hi], lands[lo:hi], sems[4 * gi:4 * gi + 2]):
                cp.start()
        for gi in range(ng):
            lo, hi = bounds[gi], bounds[gi + 1]
            for cp in _own_copies(srcs[lo:hi], lands[lo:hi], sems[4 * gi + 2:4 * gi + 4]):
                cp.start()
        token[...] = jnp.zeros_like(token)

    sem_shapes = []
    for g in groups:
        sem_shapes += [pltpu.SemaphoreType.DMA((3 * len(g),))] * 2 + [pltpu.SemaphoreType.DMA((len(g),))] * 2
    res = pl.pallas_call(
        body, name=name,
        out_shape=tuple(sem_shapes) + tuple(pltpu.HBM(a.shape, a.dtype) for a in arrs)
        + tuple(pltpu.HBM((N_CHIPS,) + a.shape, a.dtype) for a in arrs) + (jax.ShapeDtypeStruct((SUBLANES, LANES), F32),),
        in_specs=[HBM] * (2 * n),
        out_specs=tuple([SEM] * (4 * ng)) + tuple([HBM] * (2 * n)) + (pl.BlockSpec(memory_space=pltpu.VMEM),),
        input_output_aliases={i: 4 * ng + i for i in range(2 * n)},
        compiler_params=pltpu.CompilerParams(has_side_effects=EFFECT),
    )(*[_hbm(a) for a in arrs], *[_hbm(lax.empty((N_CHIPS,) + a.shape, a.dtype)) for a in arrs])
    sems, thru, lands, token = res[:4 * ng], res[4 * ng:4 * ng + n], res[4 * ng + n:4 * ng + 2 * n], res[-1]
    out = [{"ici": (sems[4 * gi], sems[4 * gi + 1]), "own": (sems[4 * gi + 2], sems[4 * gi + 3]),
            "thru": list(thru[bounds[gi]:bounds[gi + 1]]), "lands": list(lands[bounds[gi]:bounds[gi + 1]])} for gi in range(ng)]
    return out, token


def gather_forward(group, after, *, name):
    thru, lands = group["thru"], group["lands"]
    n = len(thru)

    def body(*refs):
        srcs, lands_r = refs[:n], refs[n:2 * n]
        ici = refs[2 * n:2 * n + 2]
        fwd = refs[2 * n + 3 + 2 * n:2 * n + 3 + 2 * n + 2]
        for cp in _ici_copies(srcs, lands_r, ici):
            cp.wait_send()
            cp.wait_recv()
        for cp in _fwd_copies(srcs, lands_r, fwd, _me()[2]):
            cp.start()
        refs[-1][...] = jnp.zeros_like(refs[-1])

    res = pl.pallas_call(
        body, name=name,
        out_shape=tuple(pltpu.HBM(a.shape, a.dtype) for a in thru) + tuple(pltpu.HBM(a.shape, a.dtype) for a in lands)
        + (pltpu.SemaphoreType.DMA((3 * n,)),) * 2 + (jax.ShapeDtypeStruct((SUBLANES, LANES), F32),),
        in_specs=[HBM] * (2 * n) + [SEM, SEM, pl.BlockSpec(memory_space=pl.ANY)],
        out_specs=tuple([HBM] * (2 * n)) + (SEM, SEM, pl.BlockSpec(memory_space=pltpu.VMEM)),
        input_output_aliases={i: i for i in range(2 * n)},
        compiler_params=pltpu.CompilerParams(has_side_effects=EFFECT),
    )(*thru, *lands, *group["ici"], after)
    return {"own": group["own"], "fwd": (res[2 * n], res[2 * n + 1]), "thru": list(res[:n]), "lands": list(res[n:2 * n]),
            "token": res[-1]}


def gather_wait(group, after, *, name):
    thru, lands = group["thru"], group["lands"]
    n = len(thru)

    def body(*refs):
        srcs, lands_r = refs[:n], refs[n:2 * n]
        own, fwd = refs[2 * n:2 * n + 2], refs[2 * n + 2:2 * n + 4]
        c = _me()[2]
        for mine, theirs in zip(_fwd_copies(srcs, lands_r, fwd, c), _fwd_copies(srcs, lands_r, fwd, 1 - c)):
            mine.wait_send()
            theirs.wait_recv()
        for cp in _own_copies(srcs, lands_r, own):
            cp.wait_send()
            cp.wait_recv()

    res = pl.pallas_call(
        body, name=name,
        out_shape=tuple(pltpu.HBM(a.shape, a.dtype) for a in thru) + tuple(pltpu.HBM(a.shape, a.dtype) for a in lands),
        in_specs=[HBM] * (2 * n) + [SEM] * 4 + [pl.BlockSpec(memory_space=pl.ANY)],
        out_specs=tuple([HBM] * (2 * n)),
        input_output_aliases={i: i for i in range(2 * n)},
        compiler_params=pltpu.CompilerParams(has_side_effects=EFFECT),
    )(*thru, *lands, *group["own"], *group["fwd"], after)
    return list(res[n:])


def _pair_copies(srcs, lands, sems):
    x, y, c = _me()
    return [pltpu.make_async_remote_copy(src_ref=srcs[i].at[:, _half_rows(srcs[i], 1 - c), :], dst_ref=lands[i], send_sem=sems[0].at[i],
                                         recv_sem=sems[1].at[i], device_id=(x, y, 1 - c), device_id_type=MESH)
            for i in range(len(srcs))]


def _scatter_copies(srcs, lands, sems):
    x, y, c = _me()
    return [pltpu.make_async_remote_copy(src_ref=srcs[i].at[2 * ch[0] + ch[1]], dst_ref=lands[i].at[j], send_sem=sems[0].at[3 * i + j],
                                         recv_sem=sems[1].at[3 * i + j], device_id=(ch[0], ch[1], c), device_id_type=MESH)
            for i in range(len(srcs)) for j, ch in enumerate(_other_chips(x, y))]


def split_start(copies, arrs, land_shapes, nsem, *, name):
    n = len(arrs)

    def body(*refs):
        for cp in copies(refs[:n], refs[n:2 * n], refs[2 * n:2 * n + 2]):
            cp.start()
        refs[-1][...] = jnp.zeros_like(refs[-1])

    res = pl.pallas_call(
        body, name=name,
        out_shape=(pltpu.SemaphoreType.DMA((nsem,)),) * 2 + tuple(pltpu.HBM(a.shape, a.dtype) for a in arrs)
        + tuple(pltpu.HBM(s, a.dtype) for s, a in zip(land_shapes, arrs)) + (jax.ShapeDtypeStruct((SUBLANES, LANES), F32),),
        in_specs=[HBM] * (2 * n), out_specs=(SEM, SEM) + tuple([HBM] * (2 * n)) + (pl.BlockSpec(memory_space=pltpu.VMEM),),
        input_output_aliases={i: 2 + i for i in range(2 * n)},
        compiler_params=pltpu.CompilerParams(has_side_effects=EFFECT),
    )(*[_hbm(a) for a in arrs], *[_hbm(lax.empty(s, a.dtype)) for s, a in zip(land_shapes, arrs)])
    return {"copies": copies, "sems": (res[0], res[1]), "thru": list(res[2:2 + n]), "lands": list(res[2 + n:2 + 2 * n]),
            "token": res[-1]}


def split_wait(state, after, *, name):
    thru, lands, copies = state["thru"], state["lands"], state["copies"]
    n = len(thru)

    def body(*refs):
        for cp in copies(refs[:n], refs[n:2 * n], refs[2 * n:2 * n + 2]):
            cp.wait_send()
            cp.wait_recv()

    res = pl.pallas_call(
        body, name=name,
        out_shape=tuple(pltpu.HBM(a.shape, a.dtype) for a in thru) + tuple(pltpu.HBM(a.shape, a.dtype) for a in lands),
        in_specs=[HBM] * (2 * n) + [SEM, SEM, pl.BlockSpec(memory_space=pl.ANY)], out_specs=tuple([HBM] * (2 * n)),
        input_output_aliases={i: i for i in range(2 * n)},
        compiler_params=pltpu.CompilerParams(has_side_effects=EFFECT),
    )(*thru, *lands, *state["sems"], after)
    return list(res[:n]), list(res[n:])


def sibling_join_halves(arrs, *, name):
    n = len(arrs)

    def body(*refs):
        outs = refs[n:2 * n]
        send_sems, recv_sems = refs[2 * n:]
        x, y, c = _me()
        cps = []
        for i in range(n):
            hrows = arrs[i].shape[0] // 2
            mine = outs[i].at[pl.ds(c * hrows, hrows)]
            cps.append(pltpu.make_async_remote_copy(src_ref=mine, dst_ref=mine, send_sem=send_sems.at[i],
                                                    recv_sem=recv_sems.at[i], device_id=(x, y, 1 - c), device_id_type=MESH))
        for cp in cps:
            cp.start()
        for i in range(n):
            hrows = arrs[i].shape[0] // 2
            theirs = outs[i].at[pl.ds((1 - c) * hrows, hrows)]
            pltpu.make_async_remote_copy(src_ref=theirs, dst_ref=theirs, send_sem=send_sems.at[i], recv_sem=recv_sems.at[i],
                                         device_id=(x, y, 1 - c), device_id_type=MESH).wait_recv()
        for cp in cps:
            cp.wait_send()

    return pl.pallas_call(
        body, name=name,
        out_shape=tuple(jax.ShapeDtypeStruct(a.shape, a.dtype) for a in arrs),
        in_specs=[ANY] * n, out_specs=tuple([ANY] * n),
        input_output_aliases={i: i for i in range(n)},
        scratch_shapes=[pltpu.SemaphoreType.DMA((n,)), pltpu.SemaphoreType.DMA((n,))],
        compiler_params=pltpu.CompilerParams(has_side_effects=True),
    )(*arrs)


def pre_reduce(gd, sib, who, *, name):
    _, r, c = gd.shape
    h = r // 2
    tr = _pick(h, (256, 128, 64, 32, 16, 8))
    nrb = h // tr

    def body(who_ref, gd_ref, sib_ref, pb_ref, own_ref):
        p = gd_ref[...] + sib_ref[...]
        pb_ref[...] = p.astype(BF16)

        @pl.when(pl.program_id(1) == who_ref[1])
        def _():
            own_ref[...] = p

    return pl.pallas_call(
        body, name=name,
        out_shape=(jax.ShapeDtypeStruct((N_CHIPS, h, c), BF16), jax.ShapeDtypeStruct((h, c), F32)),
        grid_spec=pltpu.PrefetchScalarGridSpec(
            num_scalar_prefetch=1, grid=(nrb, N_CHIPS),
            in_specs=[pl.BlockSpec((None, tr, c), lambda i, s, w: (s, w[0] * nrb + i, 0)),
                      pl.BlockSpec((None, tr, c), lambda i, s, w: (s, i, 0))],
            out_specs=(pl.BlockSpec((None, tr, c), lambda i, s, w: (s, i, 0)),
                       pl.BlockSpec((tr, c), lambda i, s, w: (i, 0)))),
        compiler_params=_cparams(("parallel", "arbitrary")),
    )(who, gd, sib)


def final_reduce(own, rcv, who, *, name):
    h, c = own.shape
    tr = _pick(h, (256, 128, 64, 32, 16, 8))
    nrb = h // tr

    def body(who_ref, own_ref, rcv_ref, o_ref):
        del who_ref
        acc = own_ref[...]
        for j in range(3):
            acc = acc + rcv_ref[j].astype(F32)
        o_ref[...] = acc

    return pl.pallas_call(
        body, name=name, out_shape=jax.ShapeDtypeStruct((2 * h, c), F32),
        grid_spec=pltpu.PrefetchScalarGridSpec(
            num_scalar_prefetch=1, grid=(nrb,),
            in_specs=[pl.BlockSpec((tr, c), lambda i, w: (i, 0)), pl.BlockSpec((3, tr, c), lambda i, w: (0, i, 0))],
            out_specs=pl.BlockSpec((tr, c), lambda i, w: (w[0] * nrb + i, 0))),
        compiler_params=_cparams(("parallel",)),
    )(who, own, rcv)


def sum8(landed, own, me, *, name):
    _, r, c = landed.shape
    tr = _pick(r, (256, 128, 64, 32, 16, 8))

    def body(me_ref, a_ref, own_ref, o_ref):
        acc = None
        for j in range(8):
            term = jnp.where(me_ref[0] == j, own_ref[...], a_ref[j])
            acc = term if acc is None else acc + term
        o_ref[...] = acc

    return pl.pallas_call(
        body, name=name, out_shape=jax.ShapeDtypeStruct((r, c), F32),
        grid_spec=pltpu.PrefetchScalarGridSpec(
            num_scalar_prefetch=1, grid=(r // tr,),
            in_specs=[pl.BlockSpec((8, tr, c), lambda i, w: (0, i, 0)), pl.BlockSpec((tr, c), lambda i, w: (i, 0))],
            out_specs=pl.BlockSpec((tr, c), lambda i, w: (i, 0))),
        compiler_params=_cparams(("parallel",)),
    )(me, landed, own)


def _everyone_copies(srcs, lands, sems):
    x, y, c = _me()
    me = 4 * x + 2 * y + c
    cps = []
    for k in range(1, 8):
        fx, fy, fc = (k >> 2) & 1, (k >> 1) & 1, k & 1
        to = (x + fx - 2 * x * fx, y + fy - 2 * y * fy, c + fc - 2 * c * fc)
        cps.append(pltpu.make_async_remote_copy(src_ref=srcs[0], dst_ref=lands[0].at[me], send_sem=sems[0].at[k - 1],
                                                recv_sem=sems[1].at[k - 1], device_id=to, device_id_type=MESH))
    return cps


def _behind(p, *tokens):
    for tk in tokens:
        if tk is not None:
            p = p + tk[0, 0]
    return p


def local_step(x, tgt, wb, ws, arrive=lambda layer, event, after: {}, reduce=lambda group, event, gb, after: None, h0=None):
    t, d = x.shape
    nh = ws["alog"].shape[0]
    ng = ws["w_s"].shape[0]
    assert nh == ng
    mix_w = (nh + ng) * HEAD_DIM

    if h0 is None:
        (h0,) = rows_fwd(stage_norm, [ws["e_norm"]], [x], [(1, BF16, d)], name="f_norm_e")
    pm = mm_nt(h0, wb["w_main_t"], name="f_proj_main")
    pba = mm_nt(h0, wb["w_ba_t"], name="f_proj_ba")
    qkv = conv_fwd(pm, ws["conv_w"], mode="silu", name="f_conv4")
    cat, states, tinvs = delta_fwd(qkv, pm, pba, ws["alog"], ws["dtb"], ws["onorm"], mix_w, name="f_delta")
    tok = arrive("layer0", "landed", states).get("token")
    cat = gmlp_fwd(pm, _behind(ws["lng"], tok), ws["lnb"], ws["w_s"], ws["bs"], cat, name="f_gmlp")
    wb = {**wb, **arrive("layer0", "joined", cat)}
    y0 = mm_nn(cat, wb["w_out"], name="f_out")
    x1, h1 = rows_fwd(stage_res_norm, [ws["f_norm0"]], [x, y0], [(0, F32, d), (1, BF16, d)], name="f_norm_f0")
    a1, s1 = mm_nn(h1, wb["w1_0"], name="f_mlp0_up", epilogue="relu2")
    y1 = mm_nn(s1, wb["w2_0"], name="f_mlp0_down")
    tok = arrive("layer1", "landed", y1).get("token")
    x2, h2 = rows_fwd(stage_res_norm, [_behind(ws["o_norm"], tok)], [x1, y1], [(0, F32, d), (1, BF16, d)], name="f_norm_o")
    wb = {**wb, **arrive("layer1", "joined", h2)}
    zz = mm_nn(h2, wb["pw1"], name="f_pw1")
    zparts = [(zz, d, 0), (zz, d, 1)]
    (gl,) = rows_fwd(stage_glu, [ws["b1a"], ws["b1b"]], zparts, [(0, F32, d)], name="f_glu")
    cv = conv_fwd(gl, ws["dw"], mode="plain", name="f_conv31")
    ln_params = [ws["dw_b"], ws["ln_g"], ws["ln_b"]]
    (sl,) = rows_fwd(stage_ln_silu, ln_params, [cv], [(0, BF16, d)], name="f_ln_silu")
    y2 = mm_nn(sl, wb["pw2"], name="f_pw2")
    x3, h3 = rows_fwd(stage_res_bias_norm, [ws["b2"], ws["f_norm1"]], [x2, y2], [(0, F32, d), (1, BF16, d)], name="f_norm_f1")
    a3, s3 = mm_nn(h3, wb["w1_1"], name="f_mlp1_up", epilogue="relu2")
    y3 = mm_nn(s3, wb["w2_1"], name="f_mlp1_down")
    loss, d4, d4b, g_final = loss_head(ws["final_norm"], x3, y3, tgt, name="loss_head")

    gb, gs = {}, {"final_norm": g_final}
    s_up = wb["w1_0"].shape[0]

    gb["w2_1"] = mm_tn(s3, d4b, groups=1, name="b_mlp1_down_w")
    dpre3 = mm_nt(d4b, wb["w2_1"], name="b_mlp1_down_x", mul=a3, out_dtype=BF16)
    gb["w1_1"] = mm_tn(h3, dpre3, groups=s_up, name="b_mlp1_up_w")
    tok = reduce("mlp1", "ready", gb, None)
    dh3 = mm_nt(dpre3, wb["w1_1"], name="b_mlp1_up_x", out_dtype=BF16)
    d3, d3b, gs["b2"], gs["f_norm1"] = rows_bwd(stage_res_bias_norm, [ws["b2"], _behind(ws["f_norm1"], tok)], [x2, y2], [d4, dh3],
                                                [(0, F32), (1, BF16)], name="b_norm_f1")
    tok = reduce("mlp1", "paired", gb, d3)
    gb["pw2"] = mm_tn(sl, d3b, groups=1, name="b_pw2_w")
    dsl = mm_nt(d3b, wb["pw2"], name="b_pw2_x", out_dtype=BF16)
    dcv, gs["dw_b"], gs["ln_g"], gs["ln_b"] = rows_bwd(stage_ln_silu, [_behind(ln_params[0], tok)] + ln_params[1:], [cv], [dsl],
                                                       [(0, F32)], name="b_ln_silu")
    dgl, gs["dw"] = conv_bwd(gl, ws["dw"], dcv, name="b_conv31")
    dzz, gs["b1a"], gs["b1b"] = rows_bwd(stage_glu, [ws["b1a"], ws["b1b"]], zparts, [dgl], [(0, BF16), (1, BF16)],
                                         name="b_glu", concat=True)
    gb["pw1"] = mm_tn(h2, dzz, groups=wb["pw1"].shape[0], name="b_pw1_w")
    tok = reduce("conv", "ready", gb, None)
    dh2 = mm_nt(dzz, wb["pw1"], name="b_pw1_x", out_dtype=BF16)
    d2, d2b, gs["o_norm"] = rows_bwd(stage_res_norm, [_behind(ws["o_norm"], tok)], [x1, y1], [d3, dh2], [(0, F32), (1, BF16)],
                                     name="b_norm_o")
    tok_conv = reduce("conv", "paired", gb, d2)
    gb["w2_0"] = mm_tn(s1, d2b, groups=1, name="b_mlp0_down_w")
    dpre1 = mm_nt(d2b, wb["w2_0"], name="b_mlp0_down_x", mul=a1, out_dtype=BF16)
    gb["w1_0"] = mm_tn(h1, dpre1, groups=s_up, name="b_mlp0_up_w")
    tok = reduce("mlp0", "ready", gb, None)
    dh1 = mm_nt(dpre1, wb["w1_0"], name="b_mlp0_up_x", out_dtype=BF16)
    d1, d1b, gs["f_norm0"] = rows_bwd(stage_res_norm, [_behind(ws["f_norm0"], tok_conv, tok)], [x, y0], [d2, dh1],
                                      [(0, F32), (1, BF16)], name="b_norm_f0")
    tok_mlp0 = reduce("mlp0", "paired", gb, d1)
    gb["w_out"] = mm_tn(cat, d1b, groups=1, name="b_out_w")
    tok = reduce("mixer", "ready", gb, None)
    dcat = mm_nt(d1b, wb["w_out"], name="b_out_x")
    dqkv_c, dpm, dpba, gs["alog"], gs["dtb"], gs["onorm"] = delta_bwd(
        qkv, pm, pba, ws["alog"], ws["dtb"], _behind(ws["onorm"], tok_mlp0, tok), states, tinvs, dcat, name="b_delta")
    tok = reduce("mixer", "paired", gb, dqkv_c)
    dpm, gs["lng"], gs["lnb"], gs["w_s"], gs["bs"] = gmlp_bwd(
        pm, ws["lng"], ws["lnb"], ws["w_s"], ws["bs"], dcat, dpm, name="b_gmlp")
    conv_w = _behind(ws["conv_w"], tok)
    dconv = conv_fwd(pm, conv_w, mode="silu_bwd", g=dqkv_c, name="b_conv4_silu")
    dpm, gs["conv_w"] = conv_bwd(pm, conv_w, dconv, name="b_conv4", into=dpm)
    gb["w_main_t"] = mm_tn(dpm, h0, groups=1, name="b_proj_main_w")
    gb["w_ba_t"] = mm_tn(dpba, h0, groups=1, name="b_proj_ba_w")
    dh0 = mm_nn(dpm, wb["w_main_t"], name="b_proj_main_x")
    dh0 = mm_nn(dpba, wb["w_ba_t"], name="b_proj_ba_x", add=dh0, out_dtype=BF16)
    grad_x, gs["e_norm"] = rows_bwd(stage_norm, [ws["e_norm"]], [x], [d1, dh0], [(0, F32)], name="b_norm_e")
    return loss, grad_x, gb, gs


WEIGHTS = ["e_norm", "e_w_in", "e_conv_w", "e_a_log", "e_dt_bias", "e_o_norm", "e_ln_g", "e_ln_b", "e_w_s", "e_b_s", "e_w_out",
           "o_norm", "o_pw1", "o_pw1_b", "o_dw", "o_dw_b", "o_ln_g", "o_ln_b", "o_pw2", "o_pw2_b", "f_norm", "f_w1", "f_w2",
           "final_norm"]
BIG = ["e_w_in", "e_w_out", "o_pw1", "o_pw2", "f_w1", "f_w2"]
SMALL_SHARDED = ["e_conv_w", "o_norm", "o_pw1_b", "o_dw", "o_dw_b", "o_ln_g", "o_ln_b", "o_pw2_b"]
SMALL = [n for n in WEIGHTS if n not in BIG]
LANES = 128
PACK_ROWS = 16
IN_ROW_MULT = 256
REDUCE_GROUPS = {"mlp1": ("w2_1", "w1_1"), "conv": ("pw2", "pw1"), "mlp0": ("w2_0", "w1_0"), "mixer": ("w_out",)}


def _pack(arrs):
    flat = jnp.concatenate([a.reshape(-1).astype(F32) for a in arrs])
    n = flat.shape[0]
    rows = -(-n // (LANES * PACK_ROWS)) * PACK_ROWS
    return jnp.pad(flat, (0, rows * LANES - n)).reshape(rows, LANES)


def _unpack(flat, shapes, lead=()):
    outs, off = [], 0
    for s in shapes:
        n = int(np.prod(s))
        outs.append(flat[..., off:off + n].reshape(lead + tuple(s)))
        off += n
    return outs


def kernel(x, e_norm, e_w_in, e_conv_w, e_a_log, e_dt_bias, e_o_norm, e_ln_g, e_ln_b, e_w_s, e_b_s, e_w_out, o_norm, o_pw1, o_pw1_b, o_dw, o_dw_b, o_ln_g, o_ln_b, o_pw2, o_pw2_b, f_norm, f_w1, f_w2, final_norm, loss_target, m_e_norm, m_e_w_in, m_e_conv_w, m_e_a_log, m_e_dt_bias, m_e_o_norm, m_e_ln_g, m_e_ln_b, m_e_w_s, m_e_b_s, m_e_w_out, m_o_norm, m_o_pw1, m_o_pw1_b, m_o_dw, m_o_dw_b, m_o_ln_g, m_o_ln_b, m_o_pw2, m_o_pw2_b, m_f_norm, m_f_w1, m_f_w2, m_final_norm, v_e_norm, v_e_w_in, v_e_conv_w, v_e_a_log, v_e_dt_bias, v_e_o_norm, v_e_ln_g, v_e_ln_b, v_e_w_s, v_e_b_s, v_e_w_out, v_o_norm, v_o_pw1, v_o_pw1_b, v_o_dw, v_o_dw_b, v_o_ln_g, v_o_ln_b, v_o_pw2, v_o_pw2_b, v_f_norm, v_f_w1, v_f_w2, v_final_norm):
    a = dict(locals())
    xi, yi, ci = _me()
    chip = 2 * xi + yi
    who = jnp.stack([ci, chip]).astype(jnp.int32)
    t, d = x.shape[1], x.shape[2]
    nh, ng = e_a_log.shape[1], e_w_s.shape[1]
    n_qkv, n_av, n_bw = 3 * nh * HEAD_DIM, nh * HEAD_DIM, ng * HEAD_DIM
    in_cols = n_qkv + n_av + 2 * nh + 2 * n_bw
    c_ba = n_qkv + n_av

    sh_in = in_cols // N_CHIPS
    pad_in = -(-sh_in // IN_ROW_MULT) * IN_ROW_MULT - sh_in
    w_in_t_local = jnp.pad(e_w_in[0].T.astype(BF16), ((0, pad_in), (0, 0)))
    small_local = [a[n] for n in SMALL_SHARDED]
    first, tok_in = gather_start([[w_in_t_local, _pack(small_local)]], name="gather_start_in")
    travelling, token = gather_start([[(e_w_out[0] + tok_in[0, 0]).astype(BF16), f_w1[0].astype(BF16), f_w2[0].astype(BF16)],
                                      [w.astype(BF16) for w in (o_pw1[0], o_pw2[0], f_w1[1], f_w2[1])]], name="gather_start")
    (h0,) = rows_fwd(stage_norm, [e_norm + token[0, 0]], [x[0]], [(1, BF16, d)], name="f_norm_e")
    g_in, g_small = gather_wait(gather_forward(first[0], h0, name="gather_forward_in"), h0, name="gather_wait_in")

    state = {"layer0": travelling[0], "layer1": travelling[1]}

    def arrive(layer, event, after):
        if event == "landed":
            state[layer] = gather_forward(state[layer], after, name="gather_forward_" + layer)
            return {"token": state[layer]["token"]}
        got = gather_wait(state[layer], after, name="gather_wait_" + layer)
        if layer == "layer0":
            g_out, g_w1, g_w2 = got
            return {"w_out": g_out.reshape(1, -1, d), "w1_0": g_w1, "w2_0": g_w2.reshape(1, -1, d)}
        g_pw1, g_pw2, g_w1, g_w2 = got
        return {"pw1": g_pw1, "pw2": g_pw2.reshape(1, -1, d), "w1_1": g_w1, "w2_1": g_w2.reshape(1, -1, d)}

    def in_rows(lo, hi):
        parts = []
        for s in range(N_CHIPS):
            a0, a1 = max(lo, s * sh_in), min(hi, (s + 1) * sh_in)
            if a0 < a1:
                parts.append(g_in[s, a0 - s * sh_in:a1 - s * sh_in])
        return parts

    wb = {
        "w_main_t": jnp.concatenate(in_rows(0, c_ba) + in_rows(c_ba + 2 * nh, in_cols), axis=0)[None],
        "w_ba_t": jnp.pad(jnp.concatenate(in_rows(c_ba, c_ba + 2 * nh), axis=0), ((0, LANES - 2 * nh), (0, 0)))[None],
    }
    pieces = _unpack(g_small.reshape(N_CHIPS, -1), [w.shape for w in small_local], lead=(N_CHIPS,))
    full = {n: jnp.moveaxis(p, 0, -2).reshape(p.shape[1:-1] + (N_CHIPS * p.shape[-1],)) for n, p in zip(SMALL_SHARDED, pieces)}
    ws = {
        "e_norm": e_norm, "conv_w": full["e_conv_w"][0], "alog": e_a_log.reshape(nh, 1, 1), "dtb": e_dt_bias.reshape(nh, 1, 1),
        "onorm": e_o_norm, "lng": e_ln_g.reshape(ng, 1, HEAD_DIM), "lnb": e_ln_b.reshape(ng, 1, HEAD_DIM), "w_s": e_w_s[0],
        "bs": e_b_s.reshape(ng, HEAD_DIM, 1), "o_norm": full["o_norm"], "b1a": full["o_pw1_b"][:, :d], "b1b": full["o_pw1_b"][:, d:],
        "dw": full["o_dw"][0], "dw_b": full["o_dw_b"], "ln_g": full["o_ln_g"], "ln_b": full["o_ln_b"], "b2": full["o_pw2_b"],
        "f_norm0": f_norm[0:1], "f_norm1": f_norm[1:2], "final_norm": final_norm.reshape(1, d),
    }

    pending = {}

    def reduce(group, event, gb, after):
        names = REDUCE_GROUPS[group]
        if event == "ready":
            arrs = [gb[nme] if gb[nme].shape[0] == N_CHIPS else gb[nme].reshape(N_CHIPS, -1, d) for nme in names]
            pending[group] = split_start(_pair_copies, arrs, [(N_CHIPS, g.shape[1] // 2, g.shape[2]) for g in arrs], len(arrs),
                                         name="reduce_sibling_start_" + group)
            return pending[group]["token"]
        mine, theirs = split_wait(pending[group], after, name="reduce_sibling_wait_" + group)
        pairs = [pre_reduce(gd, sb, who, name="reduce_pair_" + nme) for nme, gd, sb in zip(names, mine, theirs)]
        sums = [p[0] for p in pairs]
        pending[group] = {"own": [p[1] for p in pairs],
                          "scatter": split_start(_scatter_copies, sums, [(3,) + s.shape[1:] for s in sums], 3 * len(sums),
                                                 name="reduce_chips_start_" + group)}
        return pending[group]["scatter"]["token"]

    loss, grad_x, gb, gs = local_step(x[0], loss_target[0], wb, ws, arrive, reduce, h0)

    gm, gba = gb["w_main_t"][0], gb["w_ba_t"][0]
    g_in_t = jnp.concatenate([gm[:c_ba], gba[:2 * nh], gm[c_ba:]], axis=0).reshape(N_CHIPS, sh_in, d)
    g_in_padded = jnp.pad(g_in_t, ((0, 0), (0, pad_in), (0, 0)))
    in_pair = split_start(_pair_copies, [g_in_padded], [(N_CHIPS, (sh_in + pad_in) // 2, d)], 1, name="reduce_sibling_start_w_in")
    small_global = {
        "e_norm": gs["e_norm"], "e_conv_w": gs["conv_w"][None], "e_a_log": gs["alog"].reshape(1, nh), "e_dt_bias": gs["dtb"].reshape(1, nh),
        "e_o_norm": gs["onorm"], "e_ln_g": gs["lng"].reshape(1, n_bw), "e_ln_b": gs["lnb"].reshape(1, n_bw), "e_w_s": gs["w_s"][None],
        "e_b_s": gs["bs"].reshape(1, ng, HEAD_DIM), "o_norm": gs["o_norm"], "o_pw1_b": jnp.concatenate([gs["b1a"], gs["b1b"]], axis=1),
        "o_dw": gs["dw"][None], "o_dw_b": gs["dw_b"], "o_ln_g": gs["ln_g"], "o_ln_b": gs["ln_b"], "o_pw2_b": gs["b2"],
        "f_norm": jnp.concatenate([gs["f_norm0"], gs["f_norm1"]], axis=0), "final_norm": gs["final_norm"].reshape(d),
    }
    packed = _pack([small_global[n] for n in SMALL] + [loss])
    everyone = split_start(_everyone_copies, [packed], [(8,) + packed.shape], 7, name="reduce_small_start")

    who_then = who + (in_pair["token"][0, 0] + everyone["token"][0, 0]).astype(jnp.int32)
    half = {}
    for group, names in REDUCE_GROUPS.items():
        _, landed = split_wait(pending[group]["scatter"], grad_x, name="reduce_chips_wait_" + group)
        for nme, ow, rc in zip(names, pending[group]["own"], landed):
            half[nme] = final_reduce(ow, rc, who_then, name="reduce_sum_" + nme)
    r_out, r_pw1, r_pw2, r_w1_0, r_w1_1, r_w2_0, r_w2_1 = sibling_join_halves(
        [half[nme] for nme in ("w_out", "pw1", "pw2", "w1_0", "w1_1", "w2_0", "w2_1")], name="reduce_join")
    big_grads = {"e_w_out": [r_out], "o_pw1": [r_pw1], "o_pw2": [r_pw2], "f_w1": [r_w1_0, r_w1_1], "f_w2": [r_w2_0, r_w2_1]}

    mine, theirs = split_wait(in_pair, r_out, name="reduce_sibling_wait_w_in")
    in_sum, in_own = pre_reduce(mine[0], theirs[0], who, name="reduce_pair_w_in")
    in_scatter = split_start(_scatter_copies, [in_sum], [(3,) + in_sum.shape[1:]], 3, name="reduce_chips_start_w_in")
    out = {}
    prev = in_scatter["token"]
    for n in BIG:
        if n != "e_w_in":
            out[n] = adamw(a[n], big_grads[n], a["m_" + n], a["v_" + n], name="adamw_" + n, behind=prev)
            prev = out[n][1][0, :SUBLANES, :LANES]

    own_packed, landed = split_wait(everyone, prev, name="reduce_small_wait")
    summed = sum8(landed[0], own_packed[0], (4 * xi + 2 * yi + ci).astype(jnp.int32).reshape(1), name="reduce_small_sum")
    *small_full, total = _unpack(summed.reshape(-1), [small_global[n].shape for n in SMALL] + [()])
    small_grads = {}
    for n, g in zip(SMALL, small_full):
        if n in SMALL_SHARDED:
            width = a[n].shape[-1]
            g = lax.dynamic_slice_in_dim(g, chip * width, width, axis=g.ndim - 1)
        small_grads[n] = g
    sw, sm, sv, sg = (_pack([src[n] for n in SMALL])[None] for src in
                      ({n: a[n] for n in SMALL}, {n: a["m_" + n] for n in SMALL}, {n: a["v_" + n] for n in SMALL}, small_grads))
    res = adamw(sw, [sg[0]], sm, sv, name="adamw_small")

    _, landed = split_wait(in_scatter, res[1], name="reduce_chips_wait_w_in")
    (r_in,) = sibling_join_halves([final_reduce(in_own, landed[0], who, name="reduce_sum_w_in")], name="reduce_join_w_in")
    w_in_res = adamw(e_w_in.transpose(0, 2, 1), [r_in[:sh_in]], m_e_w_in.transpose(0, 2, 1), v_e_w_in.transpose(0, 2, 1),
                     name="adamw_e_w_in")
    out["e_w_in"] = tuple(r.transpose(0, 2, 1) for r in w_in_res)
    shapes = [a[n].shape for n in SMALL]
    unpacked = [_unpack(r.reshape(-1), shapes) for r in res]
    for i, n in enumerate(SMALL):
        out[n] = tuple(u[i] for u in unpacked)

    result = [total, grad_x[None]]
    for k in range(4):
        result += [out[n][k] for n in WEIGHTS]
    return tuple(result)
```

```python
import functools
import math

import jax
import jax.numpy as jnp
import numpy as np
from jax import lax
from jax.experimental import pallas as pl
from jax.experimental.pallas import tpu as pltpu

F32 = jnp.float32
BF16 = jnp.bfloat16

EPS = 1e-6
CHUNK = 64
PAIR = 2 * CHUNK
HEAD_DIM = 128
N_CHIPS = 4
ADAM_LR, ADAM_B1, ADAM_B2, ADAM_EPS, ADAM_WD, ADAM_STEP = 0.001, 0.9, 0.999, 1e-08, 0.01, 10

VMEM_LIMIT = 56 * 1024 * 1024


def _cparams(sem=None):
    return pltpu.CompilerParams(dimension_semantics=sem, vmem_limit_bytes=VMEM_LIMIT)


def _pick(n, prefs):
    for p in prefs:
        if n % p == 0:
            return p
    return n


def mm_nn(a, w, *, name, epilogue=None, add=None, out_dtype=F32, behind=None):
    m, k = a.shape
    s, _, ns = w.shape
    n = s * ns
    tm = _pick(m, (1024, 512, 256, 128))
    tn = _pick(ns, (1024, 512, 256, 128))
    tk = _pick(k, (2048, 1024, 512, 256, 128))
    nk = k // tk
    npb = ns // tn
    assert add is None or epilogue is None

    def body(a_ref, w_ref, *rest):
        if add is not None:
            add_ref, rest = rest[0], rest[1:]
        if behind is not None:
            rest = rest[1:]
        if epilogue == "relu2":
            o1_ref, o2_ref = rest[0], rest[1]
            acc_ref = rest[2] if nk > 1 else None
        else:
            o1_ref = rest[0]
            acc_ref = rest[1] if nk > 1 else None
        def finish(c, rows=slice(None)):
            if epilogue == "relu2":
                r = jnp.maximum(c, 0.0)
                o1_ref[rows, :] = r.astype(o1_ref.dtype)
                o2_ref[rows, :] = (r * r).astype(o2_ref.dtype)
            elif add is not None:
                o1_ref[rows, :] = (c + add_ref[rows, :].astype(F32)).astype(o1_ref.dtype)
            else:
                o1_ref[rows, :] = c.astype(o1_ref.dtype)

        part = jnp.dot(a_ref[...], w_ref[...], preferred_element_type=F32)
        if nk == 1:
            finish(part)
        else:
            kk = pl.program_id(2)

            @pl.when(kk == 0)
            def _():
                acc_ref[...] = part

            @pl.when(kk > 0)
            def _():
                acc_ref[...] += part

            @pl.when(kk == nk - 1)
            def _():
                finish(acc_ref[...])

    o_spec = pl.BlockSpec((tm, tn), lambda i, j, kk: (i, j))
    if epilogue == "relu2":
        out_shape = (jax.ShapeDtypeStruct((m, n), BF16), jax.ShapeDtypeStruct((m, n), BF16))
        out_specs = (o_spec, o_spec)
    else:
        out_shape = jax.ShapeDtypeStruct((m, n), out_dtype)
        out_specs = o_spec
    return pl.pallas_call(
        body, name=name, out_shape=out_shape,
        grid=(m // tm, n // tn, nk),
        in_specs=[pl.BlockSpec((tm, tk), lambda i, j, kk: (i, kk)),
                  pl.BlockSpec((None, tk, tn), lambda i, j, kk: (j // npb, kk, j % npb))] + ([o_spec] if add is not None else [])
        + ([pl.BlockSpec(behind.shape, lambda i, j, kk: (0, 0))] if behind is not None else []),
        out_specs=out_specs,
        scratch_shapes=[pltpu.VMEM((tm, tn), F32)] if nk > 1 else [],
        compiler_params=_cparams(("parallel", "parallel", "arbitrary")),
    )(*([a, w] + ([add] if add is not None else []) + ([behind] if behind is not None else [])))


def mm_nt(a, w, *, name, mul=None, add=None, out_dtype=F32):
    assert mul is None or add is None
    if add is not None:
        mul = add
    m, n = a.shape
    s, k, ns = w.shape
    assert n == s * ns
    tm = _pick(m, (1024, 512, 256, 128))
    tko = _pick(k, (1024, 512, 256, 128))
    tn = _pick(ns, (2048, 1024, 512, 256, 128))
    nn = n // tn
    npb = ns // tn

    def body(a_ref, w_ref, *rest):
        if mul is not None:
            m_ref, o_ref = rest[0], rest[1]
            acc_ref = rest[2] if nn > 1 else None
        else:
            m_ref, o_ref = None, rest[0]
            acc_ref = rest[1] if nn > 1 else None
        part = lax.dot_general(a_ref[...], w_ref[...], (((1,), (1,)), ((), ())), preferred_element_type=F32)

        def finish(c):
            if add is not None:
                c = c + m_ref[...].astype(F32)
            elif m_ref is not None:
                c = c * (2.0 * m_ref[...].astype(F32))
            o_ref[...] = c.astype(o_ref.dtype)

        if nn == 1:
            finish(part)
        else:
            kk = pl.program_id(2)

            @pl.when(kk == 0)
            def _():
                acc_ref[...] = part

            @pl.when(kk > 0)
            def _():
                acc_ref[...] += part

            @pl.when(kk == nn - 1)
            def _():
                finish(acc_ref[...])

    in_specs = [pl.BlockSpec((tm, tn), lambda i, j, kk: (i, kk)),
                pl.BlockSpec((None, tko, tn), lambda i, j, kk: (kk // npb, j, kk % npb))]
    args = [a, w]
    if mul is not None:
        in_specs.append(pl.BlockSpec((tm, tko), lambda i, j, kk: (i, j)))
        args.append(mul)
    return pl.pallas_call(
        body, name=name, out_shape=jax.ShapeDtypeStruct((m, k), out_dtype),
        grid=(m // tm, k // tko, nn),
        in_specs=in_specs,
        out_specs=pl.BlockSpec((tm, tko), lambda i, j, kk: (i, j)),
        scratch_shapes=[pltpu.VMEM((tm, tko), F32)] if nn > 1 else [],
        compiler_params=_cparams(("parallel", "parallel", "arbitrary")),
    )(*args)


def mm_tn(a, b, *, groups, name, also_bf16=False):
    m, k = a.shape
    _, n = b.shape
    ns = n // groups
    tko = _pick(k, (1024, 512, 256, 128))
    tn = _pick(ns, (1024, 512, 256, 128))
    tc = _pick(m, (2048, 1024, 512, 256, 128))
    nc = m // tc
    npb = ns // tn

    def body(a_ref, b_ref, o_ref, *half):
        part = lax.dot_general(a_ref[...], b_ref[...], (((0,), (0,)), ((), ())), preferred_element_type=F32)
        kk = pl.program_id(2)

        @pl.when(kk == 0)
        def _():
            o_ref[...] = part

        @pl.when(kk > 0)
        def _():
            o_ref[...] += part

        if also_bf16:
            @pl.when(kk == nc - 1)
            def _():
                half[0][...] = o_ref[...].astype(BF16)

    o_spec = pl.BlockSpec((None, tko, tn), lambda i, j, kk: (j // npb, i, j % npb))
    o_shape = jax.ShapeDtypeStruct((groups, k, ns), F32)
    return pl.pallas_call(
        body, name=name, out_shape=(o_shape, jax.ShapeDtypeStruct((groups, k, ns), BF16)) if also_bf16 else o_shape,
        grid=(k // tko, n // tn, nc),
        in_specs=[pl.BlockSpec((tc, tko), lambda i, j, kk: (kk, i)),
                  pl.BlockSpec((tc, tn), lambda i, j, kk: (kk, j))],
        out_specs=(o_spec, o_spec) if also_bf16 else o_spec,
        compiler_params=_cparams(("parallel", "parallel", "arbitrary")),
    )(a, b)


ROWS = 128


def _full_spec(arr):
    nd = arr.ndim
    return pl.BlockSpec(arr.shape, lambda i, _nd=nd: (0,) * _nd)


def _row_spec(x, rows):
    if isinstance(x, tuple):
        _, w, cb = x
        return pl.BlockSpec((rows, w), lambda i, _cb=cb: (i, _cb))
    return pl.BlockSpec((rows, x.shape[1]), lambda i: (i, 0))


def _arr(x):
    return x[0] if isinstance(x, tuple) else x


def _width(x):
    return x[1] if isinstance(x, tuple) else x.shape[1]


def rows_fwd(fn, params, xs, stores, *, name, rows=ROWS):
    t = _arr(xs[0]).shape[0]
    np_, nx = len(params), len(xs)

    def body(*refs):
        p_refs, x_refs, o_refs = refs[:np_], refs[np_:np_ + nx], refs[np_ + nx:]
        outs = fn(*[r[...].astype(F32) for r in p_refs], *[r[...].astype(F32) for r in x_refs])
        for (idx, dt, _), o_ref in zip(stores, o_refs):
            o_ref[...] = outs[idx].astype(dt)

    res = pl.pallas_call(
        body, name=name,
        out_shape=tuple(jax.ShapeDtypeStruct((t, w), dt) for _, dt, w in stores),
        grid=(t // rows,),
        in_specs=[_full_spec(p) for p in params] + [_row_spec(x, rows) for x in xs],
        out_specs=tuple(pl.BlockSpec((rows, w), lambda i: (i, 0)) for _, _, w in stores),
        compiler_params=_cparams(("parallel",)),
    )(*params, *[_arr(x) for x in xs])
    return res


def rows_bwd(fn, params, xs, cts, dx_stores, *, name, rows=ROWS, concat=False):
    t = _arr(xs[0]).shape[0]
    np_, nx = len(params), len(xs)
    ct_idx = [i for i, c in enumerate(cts) if c is not None]
    ct_arrs = [cts[i] for i in ct_idx]
    nct = len(ct_arrs)
    nds = 1 if concat else len(dx_stores)
    widths = [_width(xs[xi]) for xi, _ in dx_stores]

    def body(*refs):
        p_refs = refs[:np_]
        x_refs = refs[np_:np_ + nx]
        c_refs = refs[np_ + nx:np_ + nx + nct]
        d_refs = refs[np_ + nx + nct:np_ + nx + nct + nds]
        g_refs = refs[np_ + nx + nct + nds:]
        pv = [r[...].astype(F32) for r in p_refs]
        xv = [r[...].astype(F32) for r in x_refs]
        outs, vjp = jax.vjp(lambda *a: tuple(fn(*a)), *pv, *xv)
        ct_full = [jnp.zeros_like(o) for o in outs]
        for i, r in zip(ct_idx, c_refs):
            ct_full[i] = r[...].astype(F32)
        grads = vjp(tuple(ct_full))
        if concat:
            off = 0
            for (xi, dt), wd in zip(dx_stores, widths):
                d_refs[0][:, off:off + wd] = grads[np_ + xi].astype(dt)
                off += wd
        else:
            for (xi, dt), d_ref in zip(dx_stores, d_refs):
                d_ref[...] = grads[np_ + xi].astype(dt)
        step = pl.program_id(0)
        for j, g_ref in enumerate(g_refs):
            @pl.when(step == 0)
            def _(g_ref=g_ref, j=j):
                g_ref[...] = grads[j]

            @pl.when(step > 0)
            def _(g_ref=g_ref, j=j):
                g_ref[...] += grads[j]

    dx_shapes = [(sum(widths), dx_stores[0][1])] if concat else [(wd, dt) for wd, (_, dt) in zip(widths, dx_stores)]
    out_shape = tuple(jax.ShapeDtypeStruct((t, wd), dt) for wd, dt in dx_shapes) + \
        tuple(jax.ShapeDtypeStruct(p.shape, F32) for p in params)
    out_specs = tuple(pl.BlockSpec((rows, wd), lambda i: (i, 0)) for wd, _ in dx_shapes) + \
        tuple(_full_spec(p) for p in params)
    return pl.pallas_call(
        body, name=name, out_shape=out_shape, grid=(t // rows,),
        in_specs=[_full_spec(p) for p in params] + [_row_spec(x, rows) for x in xs] + [_row_spec(c, rows) for c in ct_arrs],
        out_specs=out_specs,
        compiler_params=_cparams(("arbitrary",)),
    )(*params, *[_arr(x) for x in xs], *[_arr(c) for c in ct_arrs])


def _rms(x, g):
    return x * lax.rsqrt(jnp.mean(x * x, axis=-1, keepdims=True) + EPS) * g


def _sigmoid(x):
    return 1.0 / (1.0 + jnp.exp(-x))


def _silu(x):
    return x * _sigmoid(x)


def _gelu(x):
    return 0.5 * x * (1.0 + lax.erf(x * (1.0 / math.sqrt(2.0))))


def stage_norm(g, x):
    return x, _rms(x, g)


def stage_res_norm(g, x, y):
    xn = x + y
    return xn, _rms(xn, g)


def stage_res_bias_norm(b, g, x, y):
    xn = x + y + b
    return xn, _rms(xn, g)


def stage_glu(ba, bb, za, zb):
    return ((za + ba) * _sigmoid(zb + bb),)


def stage_ln_silu(dw_b, ln_g, ln_b, cv):
    z = cv + dw_b
    mu = jnp.mean(z, axis=-1, keepdims=True)
    zc = z - mu
    y = zc * lax.rsqrt(jnp.mean(zc * zc, axis=-1, keepdims=True) + EPS) * ln_g + ln_b
    return (_silu(y),)


CONV_ROWS = 128
CONV_COLS = 256
SUBLANES = 8


def _halo(k):
    return SUBLANES * ((k - 1 + SUBLANES - 1) // SUBLANES)


def _taps_by_roll(k):
    out = {}
    for s in range(k):
        out.setdefault(s % SUBLANES, []).append((s // SUBLANES, s))
    return out


def _shifted_down(win, k, r):
    halo, n = _halo(k), win.shape[0]
    segs = {}
    for b, lst in _taps_by_roll(k).items():
        rolled = win if b == 0 else pltpu.roll(win, b, axis=0)
        for a, s in lst:
            segs[s] = rolled[halo - SUBLANES * a: halo - SUBLANES * a + r]
    return segs


def _shifted_up(win, k, r):
    n = win.shape[0]
    segs = {}
    for b, lst in _taps_by_roll(k).items():
        rolled = win if b == 0 else pltpu.roll(win, n - b, axis=0)
        for a, s in lst:
            segs[s] = rolled[SUBLANES * a: SUBLANES * a + r]
    return segs


def _for_blocks(nblk, fn):
    fn(0, True, nblk == 1)
    if nblk > 2:
        def step(i, c):
            fn(i, False, False)
            return c
        lax.fori_loop(1, nblk - 1, step, 0)
    if nblk > 1:
        fn(nblk - 1, False, True)


def _base(i, r):
    return i * r if isinstance(i, int) else pl.multiple_of(i * r, r)


def _win_top(ref, i, first, r, halo):
    if first:
        return jnp.concatenate([jnp.zeros((halo, ref.shape[1]), F32), ref[pl.ds(0, r), :]], axis=0)
    base = _base(i, r)
    return ref[pl.ds(base - halo, r + halo), :]


def _win_bottom(ref, i, last, r, halo):
    base = _base(i, r)
    if last:
        return jnp.concatenate([ref[pl.ds(base, r), :], jnp.zeros((halo, ref.shape[1]), F32)], axis=0)
    return ref[pl.ds(base, r + halo), :]


def conv_fwd(x, w, *, mode, name, g=None):
    t = x.shape[0]
    k, c = w.shape
    r, cb, halo = min(CONV_ROWS, t), min(CONV_COLS, c), _halo(k)
    nblk = t // r

    def body(*refs):
        if mode == "silu_bwd":
            x_ref, w_ref, g_ref, o_ref = refs
        else:
            x_ref, w_ref, o_ref = refs

        def blk(i, first, last):
            segs = _shifted_down(_win_top(x_ref, i, first, r, halo), k, r)
            acc = None
            for s in range(k):
                term = w_ref[pl.ds(k - 1 - s, 1), :] * segs[s]
                acc = term if acc is None else acc + term
            base = _base(i, r)
            if mode == "silu":
                acc = _silu(acc)
            elif mode == "silu_bwd":
                sg = _sigmoid(acc)
                acc = g_ref[pl.ds(base, r), :] * (sg * (1.0 + acc * (1.0 - sg)))
            o_ref[pl.ds(base, r), :] = acc

        _for_blocks(nblk, blk)

    col = pl.BlockSpec((t, cb), lambda j: (0, j))
    in_specs = [col, pl.BlockSpec((k, cb), lambda j: (0, j))] + ([col] if mode == "silu_bwd" else [])
    args = [x, w] + ([g] if mode == "silu_bwd" else [])
    return pl.pallas_call(
        body, name=name, out_shape=jax.ShapeDtypeStruct((t, c), F32), grid=(c // cb,),
        in_specs=in_specs, out_specs=col, compiler_params=_cparams(("parallel",)),
    )(*args)


def conv_bwd(x, w, dy, *, name, into=None):
    t = x.shape[0]
    k, c = w.shape
    r, cb, halo = min(CONV_ROWS, t), min(CONV_COLS, c), _halo(k)
    nblk = t // r

    def body(x_ref, w_ref, dy_ref, *rest):
        dx_ref, dw_ref = rest[-2], rest[-1]
        dw_ref[...] = jnp.zeros_like(dw_ref)

        def blk(i, first, last):
            base = _base(i, r)
            up = _shifted_up(_win_bottom(dy_ref, i, last, r, halo), k, r)
            down = _shifted_down(_win_top(x_ref, i, first, r, halo), k, r)
            dyb = up[0]
            acc = None
            for s in range(k):
                term = w_ref[pl.ds(k - 1 - s, 1), :] * up[s]
                acc = term if acc is None else acc + term
                dw_ref[pl.ds(k - 1 - s, 1), :] += jnp.sum(down[s] * dyb, axis=0, keepdims=True)
            dx_ref[pl.ds(base, r), :] = acc.astype(dx_ref.dtype)

        _for_blocks(nblk, blk)

    col = pl.BlockSpec((t, cb), lambda j: (0, j))
    wsp = pl.BlockSpec((k, cb), lambda j: (0, j))
    dx_shape = jax.ShapeDtypeStruct((t, c), F32) if into is None else jax.ShapeDtypeStruct(into.shape, into.dtype)
    return pl.pallas_call(
        body, name=name,
        out_shape=(dx_shape, jax.ShapeDtypeStruct((k, c), F32)), grid=(c // cb,),
        in_specs=[col, wsp, col] + ([] if into is None else [pl.BlockSpec(memory_space=pl.ANY)]),
        out_specs=(col, wsp),
        input_output_aliases={} if into is None else {3: 0},
        compiler_params=_cparams(("parallel",)),
    )(*([x, w, dy] + ([] if into is None else [into])))


_DIMS = {"nn": (((1,), (0,)), ((), ())), "nt": (((1,), (1,)), ((), ())), "tn": (((0,), (0,)), ((), ()))}
_DIMS_BATCHED = {"nn": (((2,), (1,)), ((0,), (0,))), "nt": (((2,), (2,)), ((0,), (0,))), "tn": (((1,), (1,)), ((0,), (0,)))}


def _mxu(a, b, mode):
    dims = _DIMS_BATCHED if a.ndim == 3 else _DIMS
    return lax.dot_general(a, b, dims[mode], preferred_element_type=F32)


def _split(x):
    hi = x.astype(BF16)
    return hi, (x - hi.astype(F32)).astype(BF16)


def _dot_raw(a, b, mode, prec):
    if prec == "bf16":
        return _mxu(a.astype(BF16), b.astype(BF16), mode)
    if prec == "x3":
        ah, al = _split(a)
        bh, bl = _split(b)
        return _mxu(ah, bh, mode) + (_mxu(ah, bl, mode) + _mxu(al, bh, mode))
    if prec == "x3r":
        bh, bm = _split(b)
        bl = (b - bh.astype(F32) - bm.astype(F32)).astype(BF16)
        ah = a.astype(BF16)
        return _mxu(ah, bh, mode) + (_mxu(ah, bm, mode) + _mxu(ah, bl, mode))
    raise ValueError(prec)


@functools.lru_cache(maxsize=None)
def _dot_fn(mode, prec):
    bprec = "x3" if prec == "x3r" else prec

    @jax.custom_vjp
    def f(a, b):
        return _dot_raw(a, b, mode, prec)

    def fwd(a, b):
        return _dot_raw(a, b, mode, prec), (a, b)

    def bwd(res, ct):
        a, b = res
        if mode == "nn":
            return _dot_raw(ct, b, "nt", bprec), _dot_raw(a, ct, "tn", bprec)
        if mode == "nt":
            return _dot_raw(ct, b, "nn", bprec), _dot_raw(ct, a, "tn", bprec)
        return _dot_raw(b, ct, "nt", bprec), _dot_raw(a, ct, "nn", bprec)

    f.defvjp(fwd, bwd)
    return f


def _dot(a, b, mode="nn", prec="bf16"):
    return _dot_fn(mode, prec)(a, b)


def _inv_product(l):
    n = l.shape[-1]
    eye = (lax.broadcasted_iota(jnp.int32, (n, n), 0) == lax.broadcasted_iota(jnp.int32, (n, n), 1)).astype(F32)
    p = eye - l
    pw = l
    for _ in range(5):
        pw = _dot_raw(pw, pw, "nn", "x3")
        p = _dot_raw(p, eye + pw, "nn", "x3")
    return p


@jax.custom_vjp
def _inv_unit_lower(l, t_saved):
    return t_saved


def _inv_fwd(l, t_saved):
    return t_saved, t_saved


def _inv_bwd(t, ct):
    tmp = _dot_raw(t, ct, "tn", "x3")
    return -_dot_raw(tmp, t, "nt", "x3"), jnp.zeros_like(t)


_inv_unit_lower.defvjp(_inv_fwd, _inv_bwd)


def _softplus(x):
    pos = x > 0
    return jnp.where(pos, x, 0.0) + jnp.log(1.0 + jnp.exp(jnp.where(pos, -x, x)))


def _l2n(x):
    return x * lax.rsqrt(jnp.sum(x * x, axis=-1, keepdims=True) + EPS)


def delta_pair(s0, qc, kc, vc, z, pba, alog, dtb, onorm, t_saved=None):
    n = PAIR
    nh = qc.shape[0]
    assert qc.shape == (nh, n, HEAD_DIM) and n == HEAD_DIM
    hi = lax.broadcasted_iota(jnp.int32, (nh, 1, pba.shape[1]), 0)
    li = lax.broadcasted_iota(jnp.int32, (nh, 1, pba.shape[1]), 2)
    braw = jnp.sum(pba[None] * (li == hi).astype(F32), axis=2, keepdims=True)
    araw = jnp.sum(pba[None] * (li == hi + nh).astype(F32), axis=2, keepdims=True)
    ri = lax.broadcasted_iota(jnp.int32, (n, n), 0)
    ci = lax.broadcasted_iota(jnp.int32, (n, n), 1)
    same = (ri // CHUNK) == (ci // CHUNK)
    tri = same & (ci <= ri)
    same_f = jnp.broadcast_to(same.astype(F32), (nh, n, n))
    tri_f = jnp.broadcast_to(tri.astype(F32), (nh, n, n))
    strict_f = (same & (ci < ri)).astype(F32)
    m0 = (lax.broadcasted_iota(jnp.int32, (n, 1), 0) < CHUNK).astype(F32)
    m1 = 1.0 - m0

    q = _l2n(qc) * (HEAD_DIM ** -0.5)
    k = _l2n(kc)
    beta = _sigmoid(braw)
    g = -jnp.exp(alog) * _softplus(araw + dtb)
    gb = jnp.broadcast_to(g, (nh, n, n))
    gc = _dot(tri_f, gb, "nn", "x3r")
    gtot = _dot(same_f, gb, "nn", "x3r")
    decay = jnp.exp(jnp.where(tri, gc - jnp.swapaxes(gc, 1, 2), -1e30))
    eg = jnp.exp(gc)
    kb, vb = k * beta, vc * beta
    l = _dot(kb, k, "nt") * decay * strict_f
    if t_saved is None:
        tinv = _inv_product(l)
    else:
        tinv = _inv_unit_lower(l, t_saved)
    u = _dot(tinv, vb, "nn", "x3")
    w = _dot(tinv, kb * eg, "nn", "x3")
    attn = _dot(q, k, "nt") * decay
    q_dec = q * eg
    k_tail = k * jnp.exp(gtot - gc)
    gl0 = jnp.exp(jnp.sum(gb * m0, axis=1, keepdims=True))
    gl1 = jnp.exp(jnp.sum(gb * m1, axis=1, keepdims=True))

    vn0 = m0 * (u - _dot(w, s0))
    s1 = s0 * gl0 + _dot(k_tail, vn0, "tn")
    vn1 = m1 * (u - _dot(w, s1))
    o = m0 * _dot(q_dec, s0) + m1 * _dot(q_dec, s1) + _dot(attn, vn0 + vn1)
    s2 = s1 * gl1 + _dot(k_tail, vn1, "tn")

    on = o * lax.rsqrt(jnp.mean(o * o, axis=-1, keepdims=True) + EPS) * onorm
    return on * _silu(z), s2, tinv


def _hcols(i):
    return slice(i * HEAD_DIM, (i + 1) * HEAD_DIM)


def _heads(ref, first, count):
    return jnp.stack([ref[:, _hcols(first + i)] for i in range(count)])


def delta_fwd(qkv, pm, pba, alog, dtb, onorm, cat_width, *, name):
    t = qkv.shape[0]
    h = alog.shape[0]
    hd = h * HEAD_DIM
    npair = t // PAIR
    mat = pl.BlockSpec((h, None, PAIR, HEAD_DIM), lambda p: (0, p, 0, 0))
    par = pl.BlockSpec((h, 1, 1), lambda p: (0, 0, 0))

    def body(qkv_ref, z_ref, pba_ref, al_ref, dt_ref, on_ref, o_ref, st_ref, ti_ref, s_scr):
        @pl.when(pl.program_id(0) == 0)
        def _():
            s_scr[...] = jnp.zeros_like(s_scr)

        s0 = s_scr[...]
        st_ref[...] = s0
        out, s2, tinv = delta_pair(s0, _heads(qkv_ref, 0, h), _heads(qkv_ref, h, h), _heads(qkv_ref, 2 * h, h), _heads(z_ref, 0, h),
                                   pba_ref[...], al_ref[...], dt_ref[...], on_ref[...])
        for i in range(h):
            o_ref[:, _hcols(i)] = out[i].astype(o_ref.dtype)
        ti_ref[...] = tinv
        s_scr[...] = s2

    return pl.pallas_call(
        body, name=name, grid=(npair,),
        out_shape=(jax.ShapeDtypeStruct((t, cat_width), BF16),
                   jax.ShapeDtypeStruct((h, npair, PAIR, HEAD_DIM), F32),
                   jax.ShapeDtypeStruct((h, npair, PAIR, PAIR), F32)),
        in_specs=[pl.BlockSpec((PAIR, 3 * hd), lambda p: (p, 0)), pl.BlockSpec((PAIR, hd), lambda p: (p, 3)),
                  pl.BlockSpec((PAIR, pba.shape[1]), lambda p: (p, 0)), par, par,
                  pl.BlockSpec((1, HEAD_DIM), lambda p: (0, 0))],
        out_specs=(pl.BlockSpec((PAIR, hd), lambda p: (p, 0)), mat, mat),
        scratch_shapes=[pltpu.VMEM((h, HEAD_DIM, HEAD_DIM), F32)],
        compiler_params=_cparams(("arbitrary",)),
    )(qkv, pm, pba, alog, dtb, onorm)


def delta_bwd(qkv, pm, pba, alog, dtb, onorm, states, tinvs, dcat, *, name):
    t = qkv.shape[0]
    h = alog.shape[0]
    hd = h * HEAD_DIM
    npair = t // PAIR
    rev = lambda p: npair - 1 - p
    mat = pl.BlockSpec((h, None, PAIR, HEAD_DIM), lambda p: (0, rev(p), 0, 0))
    par = pl.BlockSpec((h, 1, 1), lambda p: (0, 0, 0))
    onsp = pl.BlockSpec((1, HEAD_DIM), lambda p: (0, 0))
    wide = pl.BlockSpec((PAIR, 3 * hd), lambda p: (rev(p), 0))
    zsp = pl.BlockSpec((PAIR, hd), lambda p: (rev(p), 3))
    bsp = pl.BlockSpec((PAIR, pba.shape[1]), lambda p: (rev(p), 0))

    def body(qkv_ref, z_ref, pba_ref, al_ref, dt_ref, on_ref, st_ref, ti_ref, dc_ref,
             dqkv_ref, dz_ref, dpba_ref, dal_ref, ddt_ref, don_ref, ds_scr):
        @pl.when(pl.program_id(0) == 0)
        def _():
            ds_scr[...] = jnp.zeros_like(ds_scr)
            dal_ref[...] = jnp.zeros_like(dal_ref)
            ddt_ref[...] = jnp.zeros_like(ddt_ref)
            don_ref[...] = jnp.zeros_like(don_ref)

        tsv = ti_ref[...]
        fn = lambda s0, qc, kc, vc, z, pb, al, dt, on: delta_pair(s0, qc, kc, vc, z, pb, al, dt, on, tsv)[:2]
        _, vjp = jax.vjp(fn, st_ref[...], _heads(qkv_ref, 0, h), _heads(qkv_ref, h, h), _heads(qkv_ref, 2 * h, h),
                         _heads(z_ref, 0, h), pba_ref[...], al_ref[...], dt_ref[...], on_ref[...])
        ds0, dq, dk, dv, dz, dpba, dal, ddt, don = vjp((_heads(dc_ref, 0, h), ds_scr[...]))
        ds_scr[...] = ds0
        for i in range(h):
            dqkv_ref[:, _hcols(i)] = dq[i]
            dqkv_ref[:, _hcols(h + i)] = dk[i]
            dqkv_ref[:, _hcols(2 * h + i)] = dv[i]
            dz_ref[:, _hcols(i)] = dz[i].astype(dz_ref.dtype)
        dal_ref[...] += dal
        ddt_ref[...] += ddt
        dpba_ref[...] = dpba.astype(dpba_ref.dtype)
        don_ref[...] += don

    return pl.pallas_call(
        body, name=name, grid=(npair,),
        out_shape=(jax.ShapeDtypeStruct((t, 3 * hd), F32), jax.ShapeDtypeStruct(pm.shape, BF16),
                   jax.ShapeDtypeStruct(pba.shape, BF16),
                   jax.ShapeDtypeStruct((h, 1, 1), F32), jax.ShapeDtypeStruct((h, 1, 1), F32),
                   jax.ShapeDtypeStruct((1, HEAD_DIM), F32)),
        in_specs=[wide, zsp, bsp, par, par, onsp, mat, mat, pl.BlockSpec((PAIR, hd), lambda p: (rev(p), 0))],
        out_specs=(wide, zsp, bsp, par, par, onsp),
        scratch_shapes=[pltpu.VMEM((h, HEAD_DIM, HEAD_DIM), F32)],
        compiler_params=_cparams(("arbitrary",)),
    )(qkv, pm, pba, alog, dtb, onorm, states, tinvs, dcat)


def gmlp_block(ln_g, ln_b, w, bcol, u_raw, v_raw):
    n = w.shape[-1]
    ri = lax.broadcasted_iota(jnp.int32, (n, n), 0)
    ci = lax.broadcasted_iota(jnp.int32, (n, n), 1)
    mask = ((ci // CHUNK) <= (ri // CHUNK)).astype(F32)
    vg = _gelu(v_raw)
    vc = vg - jnp.mean(vg, axis=-1, keepdims=True)
    vgn = vc * lax.rsqrt(jnp.mean(vc * vc, axis=-1, keepdims=True) + EPS) * ln_g + ln_b
    return _gelu(u_raw) * (_dot(w * mask, vgn) + bcol)


def gmlp_fwd(pm, ln_g, ln_b, w_s, bcol, cat, *, name):
    t = pm.shape[0]
    g = w_s.shape[0]
    gw = g * HEAD_DIM
    assert pm.shape[1] == 6 * gw and cat.shape[1] == 2 * gw

    def body(u_ref, v_ref, lg_ref, lb_ref, w_ref, b_ref, cat_in, o_ref):
        del cat_in
        out = gmlp_block(lg_ref[...], lb_ref[...], w_ref[...], b_ref[...], _heads(u_ref, 0, g), _heads(v_ref, 0, g))
        for i in range(g):
            o_ref[:, _hcols(i)] = out[i].astype(o_ref.dtype)

    full = lambda a: pl.BlockSpec(a.shape, lambda m: (0, 0, 0))
    return pl.pallas_call(
        body, name=name, grid=(t // HEAD_DIM,),
        out_shape=jax.ShapeDtypeStruct(cat.shape, cat.dtype),
        in_specs=[pl.BlockSpec((HEAD_DIM, gw), lambda m: (m, 4)), pl.BlockSpec((HEAD_DIM, gw), lambda m: (m, 5)),
                  full(ln_g), full(ln_b), full(w_s), full(bcol), pl.BlockSpec(memory_space=pl.ANY)],
        out_specs=pl.BlockSpec((HEAD_DIM, gw), lambda m: (m, 1)),
        input_output_aliases={6: 0},
        compiler_params=_cparams(("arbitrary",)),
    )(pm, pm, ln_g, ln_b, w_s, bcol, cat)


def gmlp_bwd(pm, ln_g, ln_b, w_s, bcol, dcat, dpm, *, name):
    t = pm.shape[0]
    g = w_s.shape[0]
    gw = g * HEAD_DIM
    assert pm.shape[1] == 6 * gw and dpm.shape == pm.shape

    def body(u_ref, v_ref, lg_ref, lb_ref, w_ref, b_ref, dc_ref, dpm_in, duv_ref, dlg_ref, dlb_ref, dw_ref, db_ref):
        del dpm_in
        first = pl.program_id(0) == 0
        _, vjp = jax.vjp(gmlp_block, lg_ref[...], lb_ref[...], w_ref[...], b_ref[...], _heads(u_ref, 0, g), _heads(v_ref, 0, g))
        dlg, dlb, dw, db, du, dv = vjp(_heads(dc_ref, 0, g))
        for i in range(g):
            duv_ref[:, _hcols(i)] = du[i].astype(duv_ref.dtype)
            duv_ref[:, _hcols(g + i)] = dv[i].astype(duv_ref.dtype)
        for ref, val in ((dlg_ref, dlg), (dlb_ref, dlb), (dw_ref, dw), (db_ref, db)):
            @pl.when(first)
            def _(ref=ref, val=val):
                ref[...] = val

            @pl.when(jnp.logical_not(first))
            def _(ref=ref, val=val):
                ref[...] += val

    full = lambda a: pl.BlockSpec(a.shape, lambda m: (0, 0, 0))
    return pl.pallas_call(
        body, name=name, grid=(t // HEAD_DIM,),
        out_shape=(jax.ShapeDtypeStruct(dpm.shape, dpm.dtype),
                   jax.ShapeDtypeStruct(ln_g.shape, F32), jax.ShapeDtypeStruct(ln_b.shape, F32),
                   jax.ShapeDtypeStruct(w_s.shape, F32), jax.ShapeDtypeStruct(bcol.shape, F32)),
        in_specs=[pl.BlockSpec((HEAD_DIM, gw), lambda m: (m, 4)), pl.BlockSpec((HEAD_DIM, gw), lambda m: (m, 5)),
                  full(ln_g), full(ln_b), full(w_s), full(bcol), pl.BlockSpec((HEAD_DIM, gw), lambda m: (m, 1)),
                  pl.BlockSpec(memory_space=pl.ANY)],
        out_specs=(pl.BlockSpec((HEAD_DIM, 2 * gw), lambda m: (m, 2)), full(ln_g), full(ln_b), full(w_s), full(bcol)),
        input_output_aliases={7: 0},
        compiler_params=_cparams(("arbitrary",)),
    )(pm, pm, ln_g, ln_b, w_s, bcol, dcat, dpm)


def loss_head(g, x, r, tgt, *, name, rows=ROWS):
    t, d = x.shape

    def body(g_ref, x_ref, r_ref, t_ref, l_ref, dx_ref, dxb_ref, dg_ref):
        y, vjp = jax.vjp(lambda gg, xx: _rms(xx, gg), g_ref[...], x_ref[...] + r_ref[...])
        e = y - t_ref[...]
        part = (0.5 / d) * jnp.sum(jnp.sum(e * e, axis=1, keepdims=True), axis=0, keepdims=True)
        dg, dx = vjp(e * (1.0 / d))
        dx_ref[...] = dx
        dxb_ref[...] = dx.astype(BF16)
        first = pl.program_id(0) == 0

        @pl.when(first)
        def _():
            l_ref[...] = part
            dg_ref[...] = dg

        @pl.when(jnp.logical_not(first))
        def _():
            l_ref[...] += part
            dg_ref[...] += dg

    rs = pl.BlockSpec((rows, d), lambda i: (i, 0))
    gs = pl.BlockSpec((1, d), lambda i: (0, 0))
    return pl.pallas_call(
        body, name=name, grid=(t // rows,),
        out_shape=(jax.ShapeDtypeStruct((1, 1), F32), jax.ShapeDtypeStruct((t, d), F32),
                   jax.ShapeDtypeStruct((t, d), BF16), jax.ShapeDtypeStruct((1, d), F32)),
        in_specs=[gs, rs, rs, rs],
        out_specs=(pl.BlockSpec((1, 1), lambda i: (0, 0)), rs, rs, gs),
        compiler_params=_cparams(("arbitrary",)),
    )(g, x, r, tgt)


def adamw(w, gs, m, v, *, name, behind=None):
    nl, r, c = w.shape
    assert len(gs) == nl
    if r % SUBLANES == 0:
        tr, tc = _pick(r, (256, 128, 64, 32, 16, 8)), c
    else:
        tr, tc = r, _pick(c, (256, 128))
    k1 = 1.0 - ADAM_B1 ** ADAM_STEP
    k2 = 1.0 - ADAM_B2 ** ADAM_STEP

    def body(*refs):
        w_ref, m_ref, v_ref = refs[0], refs[1], refs[2]
        g_refs = refs[3:3 + nl]
        go_ref, d_ref, mo_ref, vo_ref = refs[-4:]
        gg = g_refs[0][...]
        for li in range(1, nl):
            gg = jnp.where(pl.program_id(0) == li, g_refs[li][...], gg)
        mn = ADAM_B1 * m_ref[...] + (1.0 - ADAM_B1) * gg
        vn = ADAM_B2 * v_ref[...] + (1.0 - ADAM_B2) * (gg * gg)
        go_ref[...] = gg
        d_ref[...] = -ADAM_LR * ((mn / k1) / (jnp.sqrt(vn / k2) + ADAM_EPS) + ADAM_WD * w_ref[...])
        mo_ref[...] = mn
        vo_ref[...] = vn

    if tc == c:
        sp = pl.BlockSpec((None, tr, c), lambda l, i: (l, i, 0))
        gsp = pl.BlockSpec((tr, c), lambda l, i: (i, 0))
    else:
        sp = pl.BlockSpec((None, r, tc), lambda l, i: (l, 0, i))
        gsp = pl.BlockSpec((r, tc), lambda l, i: (0, i))
    sds = jax.ShapeDtypeStruct((nl, r, c), F32)
    extra = [] if behind is None else [behind]
    return pl.pallas_call(
        body, name=name, grid=(nl, (r // tr) * (c // tc)), out_shape=(sds, sds, sds, sds),
        in_specs=[sp, sp, sp] + [gsp] * nl + [pl.BlockSpec(e.shape, lambda l, i: (0, 0)) for e in extra], out_specs=(sp, sp, sp, sp),
        compiler_params=_cparams(("parallel", "parallel")),
    )(w, m, v, *gs, *extra)


MESH = pl.DeviceIdType.MESH
ANY = pl.BlockSpec(memory_space=pl.ANY)


def _me():
    return lax.axis_index("x"), lax.axis_index("y"), lax.axis_index("c")


def _other_chips(x, y):
    return [(1 - x, y), (x, 1 - y), (1 - x, 1 - y)]


HBM = pl.BlockSpec(memory_space=pltpu.HBM)
SEM = pl.BlockSpec(memory_space=pltpu.SEMAPHORE)
EFFECT = pltpu.SideEffectType.DATAFLOW_SIDE_EFFECTING


def _hbm(a):
    return pltpu.with_memory_space_constraint(a, pltpu.HBM)


def _half_rows(arr, which):
    hrows = arr.shape[-2] // 2
    return pl.ds(which * hrows, hrows)


def _ici_copies(srcs, lands, sems):
    x, y, c = _me()
    chip = 2 * x + y
    return [pltpu.make_async_remote_copy(src_ref=srcs[i].at[_half_rows(srcs[i], c)], dst_ref=lands[i].at[chip, _half_rows(srcs[i], c)],
                                         send_sem=sems[0].at[3 * i + j], recv_sem=sems[1].at[3 * i + j],
                                         device_id=(ch[0], ch[1], c), device_id_type=MESH)
            for i in range(len(srcs)) for j, ch in enumerate(_other_chips(x, y))]


def _own_copies(srcs, lands, sems):
    x, y, c = _me()
    return [pltpu.make_async_remote_copy(src_ref=srcs[i], dst_ref=lands[i].at[2 * x + y], send_sem=sems[0].at[i],
                                         recv_sem=sems[1].at[i], device_id=(x, y, 1 - c), device_id_type=MESH)
            for i in range(len(srcs))]


def _fwd_copies(srcs, lands, sems, half):
    x, y, c = _me()
    cps = []
    for i in range(len(srcs)):
        for j, ch in enumerate(_other_chips(x, y)):
            part = lands[i].at[2 * ch[0] + ch[1], _half_rows(srcs[i], half)]
            cps.append(pltpu.make_async_remote_copy(src_ref=part, dst_ref=part, send_sem=sems[0].at[3 * i + j],
                                                    recv_sem=sems[1].at[3 * i + j], device_id=(x, y, 1 - c), device_id_type=MESH))
    return cps


def gather_start(groups, *, name):
    arrs = [a for g in groups for a in g]
    n, ng = len(arrs), len(groups)
    bounds = np.cumsum([0] + [len(g) for g in groups])

    def body(*refs):
        srcs, lands = refs[:n], refs[n:2 * n]
        sems = refs[2 * n:2 * n + 4 * ng]
        token = refs[-1]
        for gi in range(ng):
            lo, hi = bounds[gi], bounds[gi + 1]
            for cp in _ici_copies(srcs[lo:hi], lands[lo:hi], sems[4 * gi:4 * gi + 2]):
                cp.start()
        for gi in range(ng):
            lo, hi = bounds[gi], bounds[gi + 1]
            for cp in _own_copies(srcs[lo:hi], lands[lo:hi], sems[4 * gi + 2:4 * gi + 4]):
                cp.start()
        token[...] = jnp.zeros_like(token)

    sem_shapes = []
    for g in groups:
        sem_shapes += [pltpu.SemaphoreType.DMA((3 * len(g),))] * 2 + [pltpu.SemaphoreType.DMA((len(g),))] * 2
    res = pl.pallas_call(
        body, name=name,
        out_shape=tuple(sem_shapes) + tuple(pltpu.HBM(a.shape, a.dtype) for a in arrs)
        + tuple(pltpu.HBM((N_CHIPS,) + a.shape, a.dtype) for a in arrs) + (jax.ShapeDtypeStruct((SUBLANES, LANES), F32),),
        in_specs=[HBM] * (2 * n),
        out_specs=tuple([SEM] * (4 * ng)) + tuple([HBM] * (2 * n)) + (pl.BlockSpec(memory_space=pltpu.VMEM),),
        input_output_aliases={i: 4 * ng + i for i in range(2 * n)},
        compiler_params=pltpu.CompilerParams(has_side_effects=EFFECT),
    )(*[_hbm(a) for a in arrs], *[_hbm(lax.empty((N_CHIPS,) + a.shape, a.dtype)) for a in arrs])
    sems, thru, lands, token = res[:4 * ng], res[4 * ng:4 * ng + n], res[4 * ng + n:4 * ng + 2 * n], res[-1]
    out = [{"ici": (sems[4 * gi], sems[4 * gi + 1]), "own": (sems[4 * gi + 2], sems[4 * gi + 3]),
            "thru": list(thru[bounds[gi]:bounds[gi + 1]]), "lands": list(lands[bounds[gi]:bounds[gi + 1]])} for gi in range(ng)]
    return out, token


def gather_forward(group, after, *, name):
    thru, lands = group["thru"], group["lands"]
    n = len(thru)

    def body(*refs):
        srcs, lands_r = refs[:n], refs[n:2 * n]
        ici = refs[2 * n:2 * n + 2]
        fwd = refs[2 * n + 3 + 2 * n:2 * n + 3 + 2 * n + 2]
        for cp in _ici_copies(srcs, lands_r, ici):
            cp.wait_send()
            cp.wait_recv()
        for cp in _fwd_copies(srcs, lands_r, fwd, _me()[2]):
            cp.start()
        refs[-1][...] = jnp.zeros_like(refs[-1])

    res = pl.pallas_call(
        body, name=name,
        out_shape=tuple(pltpu.HBM(a.shape, a.dtype) for a in thru) + tuple(pltpu.HBM(a.shape, a.dtype) for a in lands)
        + (pltpu.SemaphoreType.DMA((3 * n,)),) * 2 + (jax.ShapeDtypeStruct((SUBLANES, LANES), F32),),
        in_specs=[HBM] * (2 * n) + [SEM, SEM, pl.BlockSpec(memory_space=pl.ANY)],
        out_specs=tuple([HBM] * (2 * n)) + (SEM, SEM, pl.BlockSpec(memory_space=pltpu.VMEM)),
        input_output_aliases={i: i for i in range(2 * n)},
        compiler_params=pltpu.CompilerParams(has_side_effects=EFFECT),
    )(*thru, *lands, *group["ici"], after)
    return {"own": group["own"], "fwd": (res[2 * n], res[2 * n + 1]), "thru": list(res[:n]), "lands": list(res[n:2 * n]),
            "token": res[-1]}


def gather_wait(group, after, *, name):
    thru, lands = group["thru"], group["lands"]
    n = len(thru)

    def body(*refs):
        srcs, lands_r = refs[:n], refs[n:2 * n]
        own, fwd = refs[2 * n:2 * n + 2], refs[2 * n + 2:2 * n + 4]
        c = _me()[2]
        for mine, theirs in zip(_fwd_copies(srcs, lands_r, fwd, c), _fwd_copies(srcs, lands_r, fwd, 1 - c)):
            mine.wait_send()
            theirs.wait_recv()
        for cp in _own_copies(srcs, lands_r, own):
            cp.wait_send()
            cp.wait_recv()

    res = pl.pallas_call(
        body, name=name,
        out_shape=tuple(pltpu.HBM(a.shape, a.dtype) for a in thru) + tuple(pltpu.HBM(a.shape, a.dtype) for a in lands),
        in_specs=[HBM] * (2 * n) + [SEM] * 4 + [pl.BlockSpec(memory_space=pl.ANY)],
        out_specs=tuple([HBM] * (2 * n)),
        input_output_aliases={i: i for i in range(2 * n)},
        compiler_params=pltpu.CompilerParams(has_side_effects=EFFECT),
    )(*thru, *lands, *group["own"], *group["fwd"], after)
    return list(res[n:])


def _pair_copies(srcs, lands, sems):
    x, y, c = _me()
    return [pltpu.make_async_remote_copy(src_ref=srcs[i].at[:, _half_rows(srcs[i], 1 - c), :], dst_ref=lands[i], send_sem=sems[0].at[i],
                                         recv_sem=sems[1].at[i], device_id=(x, y, 1 - c), device_id_type=MESH)
            for i in range(len(srcs))]


def _scatter_copies(srcs, lands, sems):
    x, y, c = _me()
    return [pltpu.make_async_remote_copy(src_ref=srcs[i].at[2 * ch[0] + ch[1]], dst_ref=lands[i].at[j], send_sem=sems[0].at[3 * i + j],
                                         recv_sem=sems[1].at[3 * i + j], device_id=(ch[0], ch[1], c), device_id_type=MESH)
            for i in range(len(srcs)) for j, ch in enumerate(_other_chips(x, y))]


def split_start(copies, arrs, land_shapes, nsem, *, name):
    n = len(arrs)

    def body(*refs):
        for cp in copies(refs[:n], refs[n:2 * n], refs[2 * n:2 * n + 2]):
            cp.start()
        refs[-1][...] = jnp.zeros_like(refs[-1])

    res = pl.pallas_call(
        body, name=name,
        out_shape=(pltpu.SemaphoreType.DMA((nsem,)),) * 2 + tuple(pltpu.HBM(a.shape, a.dtype) for a in arrs)
        + tuple(pltpu.HBM(s, a.dtype) for s, a in zip(land_shapes, arrs)) + (jax.ShapeDtypeStruct((SUBLANES, LANES), F32),),
        in_specs=[HBM] * (2 * n), out_specs=(SEM, SEM) + tuple([HBM] * (2 * n)) + (pl.BlockSpec(memory_space=pltpu.VMEM),),
        input_output_aliases={i: 2 + i for i in range(2 * n)},
        compiler_params=pltpu.CompilerParams(has_side_effects=EFFECT),
    )(*[_hbm(a) for a in arrs], *[_hbm(lax.empty(s, a.dtype)) for s, a in zip(land_shapes, arrs)])
    return {"copies": copies, "sems": (res[0], res[1]), "thru": list(res[2:2 + n]), "lands": list(res[2 + n:2 + 2 * n]),
            "token": res[-1]}


def split_wait(state, after, *, name):
    thru, lands, copies = state["thru"], state["lands"], state["copies"]
    n = len(thru)

    def body(*refs):
        for cp in copies(refs[:n], refs[n:2 * n], refs[2 * n:2 * n + 2]):
            cp.wait_send()
            cp.wait_recv()

    res = pl.pallas_call(
        body, name=name,
        out_shape=tuple(pltpu.HBM(a.shape, a.dtype) for a in thru) + tuple(pltpu.HBM(a.shape, a.dtype) for a in lands),
        in_specs=[HBM] * (2 * n) + [SEM, SEM, pl.BlockSpec(memory_space=pl.ANY)], out_specs=tuple([HBM] * (2 * n)),
        input_output_aliases={i: i for i in range(2 * n)},
        compiler_params=pltpu.CompilerParams(has_side_effects=EFFECT),
    )(*thru, *lands, *state["sems"], after)
    return list(res[:n]), list(res[n:])


def sibling_join_halves(arrs, *, name):
    n = len(arrs)

    def body(*refs):
        outs = refs[n:2 * n]
        send_sems, recv_sems = refs[2 * n:]
        x, y, c = _me()
        cps = []
        for i in range(n):
            hrows = arrs[i].shape[0] // 2
            mine = outs[i].at[pl.ds(c * hrows, hrows)]
            cps.append(pltpu.make_async_remote_copy(src_ref=mine, dst_ref=mine, send_sem=send_sems.at[i],
                                                    recv_sem=recv_sems.at[i], device_id=(x, y, 1 - c), device_id_type=MESH))
        for cp in cps:
            cp.start()
        for i in range(n):
            hrows = arrs[i].shape[0] // 2
            theirs = outs[i].at[pl.ds((1 - c) * hrows, hrows)]
            pltpu.make_async_remote_copy(src_ref=theirs, dst_ref=theirs, send_sem=send_sems.at[i], recv_sem=recv_sems.at[i],
                                         device_id=(x, y, 1 - c), device_id_type=MESH).wait_recv()
        for cp in cps:
            cp.wait_send()

    return pl.pallas_call(
        body, name=name,
        out_shape=tuple(jax.ShapeDtypeStruct(a.shape, a.dtype) for a in arrs),
        in_specs=[ANY] * n, out_specs=tuple([ANY] * n),
        input_output_aliases={i: i for i in range(n)},
        scratch_shapes=[pltpu.SemaphoreType.DMA((n,)), pltpu.SemaphoreType.DMA((n,))],
        compiler_params=pltpu.CompilerParams(has_side_effects=True),
    )(*arrs)


def pre_reduce(gd, sib, who, *, name):
    _, r, c = gd.shape
    h = r // 2
    tr = _pick(h, (256, 128, 64, 32, 16, 8))
    nrb = h // tr

    def body(who_ref, gd_ref, sib_ref, pb_ref, own_ref):
        p = gd_ref[...] + sib_ref[...]
        pb_ref[...] = p.astype(BF16)

        @pl.when(pl.program_id(1) == who_ref[1])
        def _():
            own_ref[...] = p

    return pl.pallas_call(
        body, name=name,
        out_shape=(jax.ShapeDtypeStruct((N_CHIPS, h, c), BF16), jax.ShapeDtypeStruct((h, c), F32)),
        grid_spec=pltpu.PrefetchScalarGridSpec(
            num_scalar_prefetch=1, grid=(nrb, N_CHIPS),
            in_specs=[pl.BlockSpec((None, tr, c), lambda i, s, w: (s, w[0] * nrb + i, 0)),
                      pl.BlockSpec((None, tr, c), lambda i, s, w: (s, i, 0))],
            out_specs=(pl.BlockSpec((None, tr, c), lambda i, s, w: (s, i, 0)),
                       pl.BlockSpec((tr, c), lambda i, s, w: (i, 0)))),
        compiler_params=_cparams(("parallel", "arbitrary")),
    )(who, gd, sib)


def final_reduce(own, rcv, who, *, name):
    h, c = own.shape
    tr = _pick(h, (256, 128, 64, 32, 16, 8))
    nrb = h // tr

    def body(who_ref, own_ref, rcv_ref, o_ref):
        del who_ref
        acc = own_ref[...]
        for j in range(3):
            acc = acc + rcv_ref[j].astype(F32)
        o_ref[...] = acc

    return pl.pallas_call(
        body, name=name, out_shape=jax.ShapeDtypeStruct((2 * h, c), F32),
        grid_spec=pltpu.PrefetchScalarGridSpec(
            num_scalar_prefetch=1, grid=(nrb,),
            in_specs=[pl.BlockSpec((tr, c), lambda i, w: (i, 0)), pl.BlockSpec((3, tr, c), lambda i, w: (0, i, 0))],
            out_specs=pl.BlockSpec((tr, c), lambda i, w: (w[0] * nrb + i, 0))),
        compiler_params=_cparams(("parallel",)),
    )(who, own, rcv)


def sum8(landed, own, me, *, name):
    _, r, c = landed.shape
    tr = _pick(r, (256, 128, 64, 32, 16, 8))

    def body(me_ref, a_ref, own_ref, o_ref):
        acc = None
        for j in range(8):
            term = jnp.where(me_ref[0] == j, own_ref[...], a_ref[j])
            acc = term if acc is None else acc + term
        o_ref[...] = acc

    return pl.pallas_call(
        body, name=name, out_shape=jax.ShapeDtypeStruct((r, c), F32),
        grid_spec=pltpu.PrefetchScalarGridSpec(
            num_scalar_prefetch=1, grid=(r // tr,),
            in_specs=[pl.BlockSpec((8, tr, c), lambda i, w: (0, i, 0)), pl.BlockSpec((tr, c), lambda i, w: (i, 0))],
            out_specs=pl.BlockSpec((tr, c), lambda i, w: (i, 0))),
        compiler_params=_cparams(("parallel",)),
    )(me, landed, own)


def _everyone_copies(srcs, lands, sems):
    x, y, c = _me()
    me = 4 * x + 2 * y + c
    cps = []
    for k in range(1, 8):
        fx, fy, fc = (k >> 2) & 1, (k >> 1) & 1, k & 1
        to = (x + fx - 2 * x * fx, y + fy - 2 * y * fy, c + fc - 2 * c * fc)
        cps.append(pltpu.make_async_remote_copy(src_ref=srcs[0], dst_ref=lands[0].at[me], send_sem=sems[0].at[k - 1],
                                                recv_sem=sems[1].at[k - 1], device_id=to, device_id_type=MESH))
    return cps


def _behind(p, *tokens):
    for tk in tokens:
        if tk is not None:
            p = p + tk[0, 0]
    return p


def local_step(x, tgt, wb, ws, arrive=lambda layer, event, after: {}, reduce=lambda group, event, gb, after: None, h0=None):
    t, d = x.shape
    nh = ws["alog"].shape[0]
    ng = ws["w_s"].shape[0]
    assert nh == ng
    mix_w = (nh + ng) * HEAD_DIM

    if h0 is None:
        (h0,) = rows_fwd(stage_norm, [ws["e_norm"]], [x], [(1, BF16, d)], name="f_norm_e")
    pm = mm_nt(h0, wb["w_main_t"], name="f_proj_main")
    pba = mm_nt(h0, wb["w_ba_t"], name="f_proj_ba")
    qkv = conv_fwd(pm, ws["conv_w"], mode="silu", name="f_conv4")
    cat, states, tinvs = delta_fwd(qkv, pm, pba, ws["alog"], ws["dtb"], ws["onorm"], mix_w, name="f_delta")
    tok = arrive("layer0", "landed", states).get("token")
    cat = gmlp_fwd(pm, _behind(ws["lng"], tok), ws["lnb"], ws["w_s"], ws["bs"], cat, name="f_gmlp")
    wb = {**wb, **arrive("layer0", "joined", cat)}
    y0 = mm_nn(cat, wb["w_out"], name="f_out")
    x1, h1 = rows_fwd(stage_res_norm, [ws["f_norm0"]], [x, y0], [(0, F32, d), (1, BF16, d)], name="f_norm_f0")
    a1, s1 = mm_nn(h1, wb["w1_0"], name="f_mlp0_up", epilogue="relu2")
    y1 = mm_nn(s1, wb["w2_0"], name="f_mlp0_down")
    tok = arrive("layer1", "landed", y1).get("token")
    x2, h2 = rows_fwd(stage_res_norm, [_behind(ws["o_norm"], tok)], [x1, y1], [(0, F32, d), (1, BF16, d)], name="f_norm_o")
    wb = {**wb, **arrive("layer1", "joined", h2)}
    zz = mm_nn(h2, wb["pw1"], name="f_pw1")
    zparts = [(zz, d, 0), (zz, d, 1)]
    (gl,) = rows_fwd(stage_glu, [ws["b1a"], ws["b1b"]], zparts, [(0, F32, d)], name="f_glu")
    cv = conv_fwd(gl, ws["dw"], mode="plain", name="f_conv31")
    ln_params = [ws["dw_b"], ws["ln_g"], ws["ln_b"]]
    (sl,) = rows_fwd(stage_ln_silu, ln_params, [cv], [(0, BF16, d)], name="f_ln_silu")
    y2 = mm_nn(sl, wb["pw2"], name="f_pw2")
    x3, h3 = rows_fwd(stage_res_bias_norm, [ws["b2"], ws["f_norm1"]], [x2, y2], [(0, F32, d), (1, BF16, d)], name="f_norm_f1")
    a3, s3 = mm_nn(h3, wb["w1_1"], name="f_mlp1_up", epilogue="relu2")
    y3 = mm_nn(s3, wb["w2_1"], name="f_mlp1_down")
    loss, d4, d4b, g_final = loss_head(ws["final_norm"], x3, y3, tgt, name="loss_head")

    gb, gs = {}, {"final_norm": g_final}
    s_up = wb["w1_0"].shape[0]

    gb["w2_1"], gb["w2_1_half"] = mm_tn(s3, d4b, groups=1, name="b_mlp1_down_w", also_bf16=True)
    dpre3 = mm_nt(d4b, wb["w2_1"], name="b_mlp1_down_x", mul=a3, out_dtype=BF16)
    gb["w1_1"], gb["w1_1_half"] = mm_tn(h3, dpre3, groups=s_up, name="b_mlp1_up_w", also_bf16=True)
    tok = reduce("mlp1", "ready", gb, None)
    dh3 = mm_nt(dpre3, wb["w1_1"], name="b_mlp1_up_x", out_dtype=BF16)
    d3, d3b, gs["b2"], gs["f_norm1"] = rows_bwd(stage_res_bias_norm, [ws["b2"], _behind(ws["f_norm1"], tok)], [x2, y2], [d4, dh3],
                                                [(0, F32), (1, BF16)], name="b_norm_f1")
    tok = reduce("mlp1", "paired", gb, d3)
    gb["pw2"], gb["pw2_half"] = mm_tn(sl, d3b, groups=1, name="b_pw2_w", also_bf16=True)
    dsl = mm_nt(d3b, wb["pw2"], name="b_pw2_x", out_dtype=BF16)
    dcv, gs["dw_b"], gs["ln_g"], gs["ln_b"] = rows_bwd(stage_ln_silu, [_behind(ln_params[0], tok)] + ln_params[1:], [cv], [dsl],
                                                       [(0, F32)], name="b_ln_silu")
    dgl, gs["dw"] = conv_bwd(gl, ws["dw"], dcv, name="b_conv31")
    dzz, gs["b1a"], gs["b1b"] = rows_bwd(stage_glu, [ws["b1a"], ws["b1b"]], zparts, [dgl], [(0, BF16), (1, BF16)],
                                         name="b_glu", concat=True)
    gb["pw1"], gb["pw1_half"] = mm_tn(h2, dzz, groups=wb["pw1"].shape[0], name="b_pw1_w", also_bf16=True)
    tok = reduce("conv", "ready", gb, None)
    dh2 = mm_nt(dzz, wb["pw1"], name="b_pw1_x", out_dtype=BF16)
    d2, d2b, gs["o_norm"] = rows_bwd(stage_res_norm, [_behind(ws["o_norm"], tok)], [x1, y1], [d3, dh2], [(0, F32), (1, BF16)],
                                     name="b_norm_o")
    tok_conv = reduce("conv", "paired", gb, d2)
    gb["w2_0"], gb["w2_0_half"] = mm_tn(s1, d2b, groups=1, name="b_mlp0_down_w", also_bf16=True)
    dpre1 = mm_nt(d2b, wb["w2_0"], name="b_mlp0_down_x", mul=a1, out_dtype=BF16)
    gb["w1_0"], gb["w1_0_half"] = mm_tn(h1, dpre1, groups=s_up, name="b_mlp0_up_w", also_bf16=True)
    tok = reduce("mlp0", "ready", gb, None)
    dh1 = mm_nt(dpre1, wb["w1_0"], name="b_mlp0_up_x", out_dtype=BF16)
    d1, d1b, gs["f_norm0"] = rows_bwd(stage_res_norm, [_behind(ws["f_norm0"], tok_conv, tok)], [x, y0], [d2, dh1],
                                      [(0, F32), (1, BF16)], name="b_norm_f0")
    tok_mlp0 = reduce("mlp0", "paired", gb, d1)
    gb["w_out"], gb["w_out_half"] = mm_tn(cat, d1b, groups=1, name="b_out_w", also_bf16=True)
    tok = reduce("mixer", "ready", gb, None)
    dcat = mm_nt(d1b, wb["w_out"], name="b_out_x")
    dqkv_c, dpm, dpba, gs["alog"], gs["dtb"], gs["onorm"] = delta_bwd(
        qkv, pm, pba, ws["alog"], ws["dtb"], _behind(ws["onorm"], tok_mlp0, tok), states, tinvs, dcat, name="b_delta")
    tok = reduce("mixer", "paired", gb, dqkv_c)
    dpm, gs["lng"], gs["lnb"], gs["w_s"], gs["bs"] = gmlp_bwd(
        pm, ws["lng"], ws["lnb"], ws["w_s"], ws["bs"], dcat, dpm, name="b_gmlp")
    conv_w = _behind(ws["conv_w"], tok)
    dconv = conv_fwd(pm, conv_w, mode="silu_bwd", g=dqkv_c, name="b_conv4_silu")
    dpm, gs["conv_w"] = conv_bwd(pm, conv_w, dconv, name="b_conv4", into=dpm)
    gb["w_main_t"] = mm_tn(dpm, h0, groups=1, name="b_proj_main_w")
    gb["w_ba_t"] = mm_tn(dpba, h0, groups=1, name="b_proj_ba_w")
    dh0 = mm_nn(dpm, wb["w_main_t"], name="b_proj_main_x")
    dh0 = mm_nn(dpba, wb["w_ba_t"], name="b_proj_ba_x", add=dh0, out_dtype=BF16)
    grad_x, gs["e_norm"] = rows_bwd(stage_norm, [ws["e_norm"]], [x], [d1, dh0], [(0, F32)], name="b_norm_e")
    return loss, grad_x, gb, gs


WEIGHTS = ["e_norm", "e_w_in", "e_conv_w", "e_a_log", "e_dt_bias", "e_o_norm", "e_ln_g", "e_ln_b", "e_w_s", "e_b_s", "e_w_out",
           "o_norm", "o_pw1", "o_pw1_b", "o_dw", "o_dw_b", "o_ln_g", "o_ln_b", "o_pw2", "o_pw2_b", "f_norm", "f_w1", "f_w2",
           "final_norm"]
BIG = ["e_w_in", "e_w_out", "o_pw1", "o_pw2", "f_w1", "f_w2"]
SMALL_SHARDED = ["e_conv_w", "o_norm", "o_pw1_b", "o_dw", "o_dw_b", "o_ln_g", "o_ln_b", "o_pw2_b"]
SMALL = [n for n in WEIGHTS if n not in BIG]
LANES = 128
PACK_ROWS = 16
IN_ROW_MULT = 256
REDUCE_GROUPS = {"mlp1": ("w2_1", "w1_1"), "conv": ("pw2", "pw1"), "mlp0": ("w2_0", "w1_0"), "mixer": ("w_out",)}


def _pack(arrs):
    flat = jnp.concatenate([a.reshape(-1).astype(F32) for a in arrs])
    n = flat.shape[0]
    rows = -(-n // (LANES * PACK_ROWS)) * PACK_ROWS
    return jnp.pad(flat, (0, rows * LANES - n)).reshape(rows, LANES)


def _unpack(flat, shapes, lead=()):
    outs, off = [], 0
    for s in shapes:
        n = int(np.prod(s))
        outs.append(flat[..., off:off + n].reshape(lead + tuple(s)))
        off += n
    return outs


def kernel(x, e_norm, e_w_in, e_conv_w, e_a_log, e_dt_bias, e_o_norm, e_ln_g, e_ln_b, e_w_s, e_b_s, e_w_out, o_norm, o_pw1, o_pw1_b, o_dw, o_dw_b, o_ln_g, o_ln_b, o_pw2, o_pw2_b, f_norm, f_w1, f_w2, final_norm, loss_target, m_e_norm, m_e_w_in, m_e_conv_w, m_e_a_log, m_e_dt_bias, m_e_o_norm, m_e_ln_g, m_e_ln_b, m_e_w_s, m_e_b_s, m_e_w_out, m_o_norm, m_o_pw1, m_o_pw1_b, m_o_dw, m_o_dw_b, m_o_ln_g, m_o_ln_b, m_o_pw2, m_o_pw2_b, m_f_norm, m_f_w1, m_f_w2, m_final_norm, v_e_norm, v_e_w_in, v_e_conv_w, v_e_a_log, v_e_dt_bias, v_e_o_norm, v_e_ln_g, v_e_ln_b, v_e_w_s, v_e_b_s, v_e_w_out, v_o_norm, v_o_pw1, v_o_pw1_b, v_o_dw, v_o_dw_b, v_o_ln_g, v_o_ln_b, v_o_pw2, v_o_pw2_b, v_f_norm, v_f_w1, v_f_w2, v_final_norm):
    a = dict(locals())
    xi, yi, ci = _me()
    chip = 2 * xi + yi
    who = jnp.stack([ci, chip]).astype(jnp.int32)
    t, d = x.shape[1], x.shape[2]
    nh, ng = e_a_log.shape[1], e_w_s.shape[1]
    n_qkv, n_av, n_bw = 3 * nh * HEAD_DIM, nh * HEAD_DIM, ng * HEAD_DIM
    in_cols = n_qkv + n_av + 2 * nh + 2 * n_bw
    c_ba = n_qkv + n_av

    sh_in = in_cols // N_CHIPS
    pad_in = -(-sh_in // IN_ROW_MULT) * IN_ROW_MULT - sh_in
    w_in_t_local = jnp.pad(e_w_in[0].T.astype(BF16), ((0, pad_in), (0, 0)))
    small_local = [a[n] for n in SMALL_SHARDED]
    first, tok_in = gather_start([[w_in_t_local, _pack(small_local)]], name="gather_start_in")
    travelling, token = gather_start([[(e_w_out[0] + tok_in[0, 0]).astype(BF16), f_w1[0].astype(BF16), f_w2[0].astype(BF16)],
                                      [w.astype(BF16) for w in (o_pw1[0], o_pw2[0], f_w1[1], f_w2[1])]], name="gather_start")
    (h0,) = rows_fwd(stage_norm, [e_norm + token[0, 0]], [x[0]], [(1, BF16, d)], name="f_norm_e")
    g_in, g_small = gather_wait(gather_forward(first[0], h0, name="gather_forward_in"), h0, name="gather_wait_in")

    state = {"layer0": travelling[0], "layer1": travelling[1]}

    def arrive(layer, event, after):
        if event == "landed":
            state[layer] = gather_forward(state[layer], after, name="gather_forward_" + layer)
            return {"token": state[layer]["token"]}
        got = gather_wait(state[layer], after, name="gather_wait_" + layer)
        if layer == "layer0":
            g_out, g_w1, g_w2 = got
            return {"w_out": g_out.reshape(1, -1, d), "w1_0": g_w1, "w2_0": g_w2.reshape(1, -1, d)}
        g_pw1, g_pw2, g_w1, g_w2 = got
        return {"pw1": g_pw1, "pw2": g_pw2.reshape(1, -1, d), "w1_1": g_w1, "w2_1": g_w2.reshape(1, -1, d)}

    def in_rows(lo, hi):
        parts = []
        for s in range(N_CHIPS):
            a0, a1 = max(lo, s * sh_in), min(hi, (s + 1) * sh_in)
            if a0 < a1:
                parts.append(g_in[s, a0 - s * sh_in:a1 - s * sh_in])
        return parts

    wb = {
        "w_main_t": jnp.concatenate(in_rows(0, c_ba) + in_rows(c_ba + 2 * nh, in_cols), axis=0)[None],
        "w_ba_t": jnp.pad(jnp.concatenate(in_rows(c_ba, c_ba + 2 * nh), axis=0), ((0, LANES - 2 * nh), (0, 0)))[None],
    }
    pieces = _unpack(g_small.reshape(N_CHIPS, -1), [w.shape for w in small_local], lead=(N_CHIPS,))
    full = {n: jnp.moveaxis(p, 0, -2).reshape(p.shape[1:-1] + (N_CHIPS * p.shape[-1],)) for n, p in zip(SMALL_SHARDED, pieces)}
    ws = {
        "e_norm": e_norm, "conv_w": full["e_conv_w"][0], "alog": e_a_log.reshape(nh, 1, 1), "dtb": e_dt_bias.reshape(nh, 1, 1),
        "onorm": e_o_norm, "lng": e_ln_g.reshape(ng, 1, HEAD_DIM), "lnb": e_ln_b.reshape(ng, 1, HEAD_DIM), "w_s": e_w_s[0],
        "bs": e_b_s.reshape(ng, HEAD_DIM, 1), "o_norm": full["o_norm"], "b1a": full["o_pw1_b"][:, :d], "b1b": full["o_pw1_b"][:, d:],
        "dw": full["o_dw"][0], "dw_b": full["o_dw_b"], "ln_g": full["o_ln_g"], "ln_b": full["o_ln_b"], "b2": full["o_pw2_b"],
        "f_norm0": f_norm[0:1], "f_norm1": f_norm[1:2], "final_norm": final_norm.reshape(1, d),
    }

    pending = {}

    def reduce(group, event, gb, after):
        names = REDUCE_GROUPS[group]
        by_chip = lambda g: g if g.shape[0] == N_CHIPS else g.reshape(N_CHIPS, -1, d)
        if event == "ready":
            arrs = [by_chip(gb[nme + "_half"]) for nme in names]
            pending[group] = split_start(_pair_copies, arrs, [(N_CHIPS, g.shape[1] // 2, g.shape[2]) for g in arrs], len(arrs),
                                         name="reduce_sibling_start_" + group)
            return pending[group]["token"]
        _, theirs = split_wait(pending[group], after, name="reduce_sibling_wait_" + group)
        pairs = [pre_reduce(by_chip(gb[nme]), sb, who, name="reduce_pair_" + nme) for nme, sb in zip(names, theirs)]
        sums = [p[0] for p in pairs]
        pending[group] = {"own": [p[1] for p in pairs],
                          "scatter": split_start(_scatter_copies, sums, [(3,) + s.shape[1:] for s in sums], 3 * len(sums),
                                                 name="reduce_chips_start_" + group)}
        return pending[group]["scatter"]["token"]

    loss, grad_x, gb, gs = local_step(x[0], loss_target[0], wb, ws, arrive, reduce, h0)

    gm, gba = gb["w_main_t"][0], gb["w_ba_t"][0]
    g_in_t = jnp.concatenate([gm[:c_ba], gba[:2 * nh], gm[c_ba:]], axis=0).reshape(N_CHIPS, sh_in, d)
    g_in_padded = jnp.pad(g_in_t, ((0, 0), (0, pad_in), (0, 0)))
    in_pair = split_start(_pair_copies, [g_in_padded], [(N_CHIPS, (sh_in + pad_in) // 2, d)], 1, name="reduce_sibling_start_w_in")
    small_global = {
        "e_norm": gs["e_norm"], "e_conv_w": gs["conv_w"][None], "e_a_log": gs["alog"].reshape(1, nh), "e_dt_bias": gs["dtb"].reshape(1, nh),
        "e_o_norm": gs["onorm"], "e_ln_g": gs["lng"].reshape(1, n_bw), "e_ln_b": gs["lnb"].reshape(1, n_bw), "e_w_s": gs["w_s"][None],
        "e_b_s": gs["bs"].reshape(1, ng, HEAD_DIM), "o_norm": gs["o_norm"], "o_pw1_b": jnp.concatenate([gs["b1a"], gs["b1b"]], axis=1),
        "o_dw": gs["dw"][None], "o_dw_b": gs["dw_b"], "o_ln_g": gs["ln_g"], "o_ln_b": gs["ln_b"], "o_pw2_b": gs["b2"],
        "f_norm": jnp.concatenate([gs["f_norm0"], gs["f_norm1"]], axis=0), "final_norm": gs["final_norm"].reshape(d),
    }
    packed = _pack([small_global[n] for n in SMALL] + [loss])
    everyone = split_start(_everyone_copies, [packed], [(8,) + packed.shape], 7, name="reduce_small_start")

    who_then = who + (in_pair["token"][0, 0] + everyone["token"][0, 0]).astype(jnp.int32)
    half = {}
    for group, names in REDUCE_GROUPS.items():
        _, landed = split_wait(pending[group]["scatter"], grad_x, name="reduce_chips_wait_" + group)
        for nme, ow, rc in zip(names, pending[group]["own"], landed):
            half[nme] = final_reduce(ow, rc, who_then, name="reduce_sum_" + nme)
    r_out, r_pw1, r_pw2, r_w1_0, r_w1_1, r_w2_0, r_w2_1 = sibling_join_halves(
        [half[nme] for nme in ("w_out", "pw1", "pw2", "w1_0", "w1_1", "w2_0", "w2_1")], name="reduce_join")
    big_grads = {"e_w_out": [r_out], "o_pw1": [r_pw1], "o_pw2": [r_pw2], "f_w1": [r_w1_0, r_w1_1], "f_w2": [r_w2_0, r_w2_1]}

    mine, theirs = split_wait(in_pair, r_out, name="reduce_sibling_wait_w_in")
    in_sum, in_own = pre_reduce(mine[0], theirs[0], who, name="reduce_pair_w_in")
    in_scatter = split_start(_scatter_copies, [in_sum], [(3,) + in_sum.shape[1:]], 3, name="reduce_chips_start_w_in")
    out = {}
    prev = in_scatter["token"]
    for n in BIG:
        if n != "e_w_in":
            out[n] = adamw(a[n], big_grads[n], a["m_" + n], a["v_" + n], name="adamw_" + n, behind=prev)
            prev = out[n][1][0, :SUBLANES, :LANES]

    own_packed, landed = split_wait(everyone, prev, name="reduce_small_wait")
    summed = sum8(landed[0], own_packed[0], (4 * xi + 2 * yi + ci).astype(jnp.int32).reshape(1), name="reduce_small_sum")
    *small_full, total = _unpack(summed.reshape(-1), [small_global[n].shape for n in SMALL] + [()])
    small_grads = {}
    for n, g in zip(SMALL, small_full):
        if n in SMALL_SHARDED:
            width = a[n].shape[-1]
            g = lax.dynamic_slice_in_dim(g, chip * width, width, axis=g.ndim - 1)
        small_grads[n] = g
    sw, sm, sv, sg = (_pack([src[n] for n in SMALL])[None] for src in
                      ({n: a[n] for n in SMALL}, {n: a["m_" + n] for n in SMALL}, {n: a["v_" + n] for n in SMALL}, small_grads))
    res = adamw(sw, [sg[0]], sm, sv, name="adamw_small")

    _, landed = split_wait(in_scatter, res[1], name="reduce_chips_wait_w_in")
    (r_in,) = sibling_join_halves([final_reduce(in_own, landed[0], who, name="reduce_sum_w_in")], name="reduce_join_w_in")
    w_in_res = adamw(e_w_in.transpose(0, 2, 1), [r_in[:sh_in]], m_e_w_in.transpose(0, 2, 1), v_e_w_in.transpose(0, 2, 1),
                     name="adamw_e_w_in")
    out["e_w_in"] = tuple(r.transpose(0, 2, 1) for r in w_in_res)
    shapes = [a[n].shape for n in SMALL]
    unpacked = [_unpack(r.reshape(-1), shapes) for r in res]
    for i, n in enumerate(SMALL):
        out[n] = tuple(u[i] for u in unpacked)

    result = [total, grad_x[None]]
    for k in range(4):
        result += [out[n][k] for n in WEIGHTS]
    return tuple(result)
```

```python
import functools
import math

import jax
import jax.numpy as jnp
import numpy as np
from jax import lax
from jax.experimental import pallas as pl
from jax.experimental.pallas import tpu as pltpu

F32 = jnp.float32
BF16 = jnp.bfloat16

EPS = 1e-6
CHUNK = 64
PAIR = 2 * CHUNK
HEAD_DIM = 128
N_CHIPS = 4
ADAM_LR, ADAM_B1, ADAM_B2, ADAM_EPS, ADAM_WD, ADAM_STEP = 0.001, 0.9, 0.999, 1e-08, 0.01, 10

VMEM_LIMIT = 56 * 1024 * 1024


def _cparams(sem=None):
    return pltpu.CompilerParams(dimension_semantics=sem, vmem_limit_bytes=VMEM_LIMIT)


def _pick(n, prefs):
    for p in prefs:
        if n % p == 0:
            return p
    return n


def mm_nn(a, w, *, name, epilogue=None, add=None, out_dtype=F32, behind=None):
    m, k = a.shape
    s, _, ns = w.shape
    n = s * ns
    tm = _pick(m, (1024, 512, 256, 128))
    tn = _pick(ns, (1024, 512, 256, 128))
    tk = _pick(k, (2048, 1024, 512, 256, 128))
    nk = k // tk
    npb = ns // tn
    assert add is None or epilogue is None

    def body(a_ref, w_ref, *rest):
        if add is not None:
            add_ref, rest = rest[0], rest[1:]
        if behind is not None:
            rest = rest[1:]
        if epilogue == "relu2":
            o1_ref, o2_ref = rest[0], rest[1]
            acc_ref = rest[2] if nk > 1 else None
        else:
            o1_ref = rest[0]
            acc_ref = rest[1] if nk > 1 else None
        def finish(c, rows=slice(None)):
            if epilogue == "relu2":
                r = jnp.maximum(c, 0.0)
                o1_ref[rows, :] = r.astype(o1_ref.dtype)
                o2_ref[rows, :] = (r * r).astype(o2_ref.dtype)
            elif add is not None:
                o1_ref[rows, :] = (c + add_ref[rows, :].astype(F32)).astype(o1_ref.dtype)
            else:
                o1_ref[rows, :] = c.astype(o1_ref.dtype)

        part = jnp.dot(a_ref[...], w_ref[...], preferred_element_type=F32)
        if nk == 1:
            finish(part)
        else:
            kk = pl.program_id(2)

            @pl.when(kk == 0)
            def _():
                acc_ref[...] = part

            @pl.when(kk > 0)
            def _():
                acc_ref[...] += part

            @pl.when(kk == nk - 1)
            def _():
                finish(acc_ref[...])

    o_spec = pl.BlockSpec((tm, tn), lambda i, j, kk: (i, j))
    if epilogue == "relu2":
        out_shape = (jax.ShapeDtypeStruct((m, n), BF16), jax.ShapeDtypeStruct((m, n), BF16))
        out_specs = (o_spec, o_spec)
    else:
        out_shape = jax.ShapeDtypeStruct((m, n), out_dtype)
        out_specs = o_spec
    return pl.pallas_call(
        body, name=name, out_shape=out_shape,
        grid=(m // tm, n // tn, nk),
        in_specs=[pl.BlockSpec((tm, tk), lambda i, j, kk: (i, kk)),
                  pl.BlockSpec((None, tk, tn), lambda i, j, kk: (j // npb, kk, j % npb))] + ([o_spec] if add is not None else [])
        + ([pl.BlockSpec(behind.shape, lambda i, j, kk: (0, 0))] if behind is not None else []),
        out_specs=out_specs,
        scratch_shapes=[pltpu.VMEM((tm, tn), F32)] if nk > 1 else [],
        compiler_params=_cparams(("parallel", "parallel", "arbitrary")),
    )(*([a, w] + ([add] if add is not None else []) + ([behind] if behind is not None else [])))


def mm_nt(a, w, *, name, mul=None, add=None, out_dtype=F32):
    assert mul is None or add is None
    if add is not None:
        mul = add
    m, n = a.shape
    s, k, ns = w.shape
    assert n == s * ns
    tm = _pick(m, (1024, 512, 256, 128))
    tko = _pick(k, (1024, 512, 256, 128))
    tn = _pick(ns, (2048, 1024, 512, 256, 128))
    nn = n // tn
    npb = ns // tn

    def body(a_ref, w_ref, *rest):
        if mul is not None:
            m_ref, o_ref = rest[0], rest[1]
            acc_ref = rest[2] if nn > 1 else None
        else:
            m_ref, o_ref = None, rest[0]
            acc_ref = rest[1] if nn > 1 else None
        part = lax.dot_general(a_ref[...], w_ref[...], (((1,), (1,)), ((), ())), preferred_element_type=F32)

        def finish(c):
            if add is not None:
                c = c + m_ref[...].astype(F32)
            elif m_ref is not None:
                c = c * (2.0 * m_ref[...].astype(F32))
            o_ref[...] = c.astype(o_ref.dtype)

        if nn == 1:
            finish(part)
        else:
            kk = pl.program_id(2)

            @pl.when(kk == 0)
            def _():
                acc_ref[...] = part

            @pl.when(kk > 0)
            def _():
                acc_ref[...] += part

            @pl.when(kk == nn - 1)
            def _():
                finish(acc_ref[...])

    in_specs = [pl.BlockSpec((tm, tn), lambda i, j, kk: (i, kk)),
                pl.BlockSpec((None, tko, tn), lambda i, j, kk: (kk // npb, j, kk % npb))]
    args = [a, w]
    if mul is not None:
        in_specs.append(pl.BlockSpec((tm, tko), lambda i, j, kk: (i, j)))
        args.append(mul)
    return pl.pallas_call(
        body, name=name, out_shape=jax.ShapeDtypeStruct((m, k), out_dtype),
        grid=(m // tm, k // tko, nn),
        in_specs=in_specs,
        out_specs=pl.BlockSpec((tm, tko), lambda i, j, kk: (i, j)),
        scratch_shapes=[pltpu.VMEM((tm, tko), F32)] if nn > 1 else [],
        compiler_params=_cparams(("parallel", "parallel", "arbitrary")),
    )(*args)


def mm_tn(a, b, *, groups, name):
    m, k = a.shape
    _, n = b.shape
    ns = n // groups
    tko = _pick(k, (1024, 512, 256, 128))
    tn = _pick(ns, (1024, 512, 256, 128))
    tc = _pick(m, (2048, 1024, 512, 256, 128))
    nc = m // tc
    npb = ns // tn

    def body(a_ref, b_ref, o_ref):
        part = lax.dot_general(a_ref[...], b_ref[...], (((0,), (0,)), ((), ())), preferred_element_type=F32)
        kk = pl.program_id(2)

        @pl.when(kk == 0)
        def _():
            o_ref[...] = part

        @pl.when(kk > 0)
        def _():
            o_ref[...] += part

    return pl.pallas_call(
        body, name=name, out_shape=jax.ShapeDtypeStruct((groups, k, ns), F32),
        grid=(k // tko, n // tn, nc),
        in_specs=[pl.BlockSpec((tc, tko), lambda i, j, kk: (kk, i)),
                  pl.BlockSpec((tc, tn), lambda i, j, kk: (kk, j))],
        out_specs=pl.BlockSpec((None, tko, tn), lambda i, j, kk: (j // npb, i, j % npb)),
        compiler_params=_cparams(("parallel", "parallel", "arbitrary")),
    )(a, b)


ROWS = 128


def _full_spec(arr):
    nd = arr.ndim
    return pl.BlockSpec(arr.shape, lambda i, _nd=nd: (0,) * _nd)


def _row_spec(x, rows):
    if isinstance(x, tuple):
        _, w, cb = x
        return pl.BlockSpec((rows, w), lambda i, _cb=cb: (i, _cb))
    return pl.BlockSpec((rows, x.shape[1]), lambda i: (i, 0))


def _arr(x):
    return x[0] if isinstance(x, tuple) else x


def _width(x):
    return x[1] if isinstance(x, tuple) else x.shape[1]


def rows_fwd(fn, params, xs, stores, *, name, rows=ROWS):
    t = _arr(xs[0]).shape[0]
    np_, nx = len(params), len(xs)

    def body(*refs):
        p_refs, x_refs, o_refs = refs[:np_], refs[np_:np_ + nx], refs[np_ + nx:]
        outs = fn(*[r[...].astype(F32) for r in p_refs], *[r[...].astype(F32) for r in x_refs])
        for (idx, dt, _), o_ref in zip(stores, o_refs):
            o_ref[...] = outs[idx].astype(dt)

    res = pl.pallas_call(
        body, name=name,
        out_shape=tuple(jax.ShapeDtypeStruct((t, w), dt) for _, dt, w in stores),
        grid=(t // rows,),
        in_specs=[_full_spec(p) for p in params] + [_row_spec(x, rows) for x in xs],
        out_specs=tuple(pl.BlockSpec((rows, w), lambda i: (i, 0)) for _, _, w in stores),
        compiler_params=_cparams(("parallel",)),
    )(*params, *[_arr(x) for x in xs])
    return res


def rows_bwd(fn, params, xs, cts, dx_stores, *, name, rows=ROWS, concat=False):
    t = _arr(xs[0]).shape[0]
    np_, nx = len(params), len(xs)
    ct_idx = [i for i, c in enumerate(cts) if c is not None]
    ct_arrs = [cts[i] for i in ct_idx]
    nct = len(ct_arrs)
    nds = 1 if concat else len(dx_stores)
    widths = [_width(xs[xi]) for xi, _ in dx_stores]

    def body(*refs):
        p_refs = refs[:np_]
        x_refs = refs[np_:np_ + nx]
        c_refs = refs[np_ + nx:np_ + nx + nct]
        d_refs = refs[np_ + nx + nct:np_ + nx + nct + nds]
        g_refs = refs[np_ + nx + nct + nds:]
        pv = [r[...].astype(F32) for r in p_refs]
        xv = [r[...].astype(F32) for r in x_refs]
        outs, vjp = jax.vjp(lambda *a: tuple(fn(*a)), *pv, *xv)
        ct_full = [jnp.zeros_like(o) for o in outs]
        for i, r in zip(ct_idx, c_refs):
            ct_full[i] = r[...].astype(F32)
        grads = vjp(tuple(ct_full))
        if concat:
            off = 0
            for (xi, dt), wd in zip(dx_stores, widths):
                d_refs[0][:, off:off + wd] = grads[np_ + xi].astype(dt)
                off += wd
        else:
            for (xi, dt), d_ref in zip(dx_stores, d_refs):
                d_ref[...] = grads[np_ + xi].astype(dt)
        step = pl.program_id(0)
        for j, g_ref in enumerate(g_refs):
            @pl.when(step == 0)
            def _(g_ref=g_ref, j=j):
                g_ref[...] = grads[j]

            @pl.when(step > 0)
            def _(g_ref=g_ref, j=j):
                g_ref[...] += grads[j]

    dx_shapes = [(sum(widths), dx_stores[0][1])] if concat else [(wd, dt) for wd, (_, dt) in zip(widths, dx_stores)]
    out_shape = tuple(jax.ShapeDtypeStruct((t, wd), dt) for wd, dt in dx_shapes) + \
        tuple(jax.ShapeDtypeStruct(p.shape, F32) for p in params)
    out_specs = tuple(pl.BlockSpec((rows, wd), lambda i: (i, 0)) for wd, _ in dx_shapes) + \
        tuple(_full_spec(p) for p in params)
    return pl.pallas_call(
        body, name=name, out_shape=out_shape, grid=(t // rows,),
        in_specs=[_full_spec(p) for p in params] + [_row_spec(x, rows) for x in xs] + [_row_spec(c, rows) for c in ct_arrs],
        out_specs=out_specs,
        compiler_params=_cparams(("arbitrary",)),
    )(*params, *[_arr(x) for x in xs], *[_arr(c) for c in ct_arrs])


def _rms(x, g):
    return x * lax.rsqrt(jnp.mean(x * x, axis=-1, keepdims=True) + EPS) * g


def _sigmoid(x):
    return 1.0 / (1.0 + jnp.exp(-x))


def _silu(x):
    return x * _sigmoid(x)


def _gelu(x):
    return 0.5 * x * (1.0 + lax.erf(x * (1.0 / math.sqrt(2.0))))


def stage_norm(g, x):
    return x, _rms(x, g)


def stage_res_norm(g, x, y):
    xn = x + y
    return xn, _rms(xn, g)


def stage_res_bias_norm(b, g, x, y):
    xn = x + y + b
    return xn, _rms(xn, g)


def stage_glu(ba, bb, za, zb):
    return ((za + ba) * _sigmoid(zb + bb),)


def stage_ln_silu(dw_b, ln_g, ln_b, cv):
    z = cv + dw_b
    mu = jnp.mean(z, axis=-1, keepdims=True)
    zc = z - mu
    y = zc * lax.rsqrt(jnp.mean(zc * zc, axis=-1, keepdims=True) + EPS) * ln_g + ln_b
    return (_silu(y),)


CONV_ROWS = 128
CONV_COLS = 256
SUBLANES = 8


def _halo(k):
    return SUBLANES * ((k - 1 + SUBLANES - 1) // SUBLANES)


def _taps_by_roll(k):
    out = {}
    for s in range(k):
        out.setdefault(s % SUBLANES, []).append((s // SUBLANES, s))
    return out


def _shifted_down(win, k, r):
    halo, n = _halo(k), win.shape[0]
    segs = {}
    for b, lst in _taps_by_roll(k).items():
        rolled = win if b == 0 else pltpu.roll(win, b, axis=0)
        for a, s in lst:
            segs[s] = rolled[halo - SUBLANES * a: halo - SUBLANES * a + r]
    return segs


def _shifted_up(win, k, r):
    n = win.shape[0]
    segs = {}
    for b, lst in _taps_by_roll(k).items():
        rolled = win if b == 0 else pltpu.roll(win, n - b, axis=0)
        for a, s in lst:
            segs[s] = rolled[SUBLANES * a: SUBLANES * a + r]
    return segs


def _for_blocks(nblk, fn):
    fn(0, True, nblk == 1)
    if nblk > 2:
        def step(i, c):
            fn(i, False, False)
            return c
        lax.fori_loop(1, nblk - 1, step, 0)
    if nblk > 1:
        fn(nblk - 1, False, True)


def _base(i, r):
    return i * r if isinstance(i, int) else pl.multiple_of(i * r, r)


def _win_top(ref, i, first, r, halo):
    if first:
        return jnp.concatenate([jnp.zeros((halo, ref.shape[1]), F32), ref[pl.ds(0, r), :]], axis=0)
    base = _base(i, r)
    return ref[pl.ds(base - halo, r + halo), :]


def _win_bottom(ref, i, last, r, halo):
    base = _base(i, r)
    if last:
        return jnp.concatenate([ref[pl.ds(base, r), :], jnp.zeros((halo, ref.shape[1]), F32)], axis=0)
    return ref[pl.ds(base, r + halo), :]


def conv_fwd(x, w, *, mode, name, g=None):
    t = x.shape[0]
    k, c = w.shape
    r, cb, halo = min(CONV_ROWS, t), min(CONV_COLS, c), _halo(k)
    nblk = t // r

    def body(*refs):
        if mode == "silu_bwd":
            x_ref, w_ref, g_ref, o_ref = refs
        else:
            x_ref, w_ref, o_ref = refs

        def blk(i, first, last):
            segs = _shifted_down(_win_top(x_ref, i, first, r, halo), k, r)
            acc = None
            for s in range(k):
                term = w_ref[pl.ds(k - 1 - s, 1), :] * segs[s]
                acc = term if acc is None else acc + term
            base = _base(i, r)
            if mode == "silu":
                acc = _silu(acc)
            elif mode == "silu_bwd":
                sg = _sigmoid(acc)
                acc = g_ref[pl.ds(base, r), :] * (sg * (1.0 + acc * (1.0 - sg)))
            o_ref[pl.ds(base, r), :] = acc

        _for_blocks(nblk, blk)

    col = pl.BlockSpec((t, cb), lambda j: (0, j))
    in_specs = [col, pl.BlockSpec((k, cb), lambda j: (0, j))] + ([col] if mode == "silu_bwd" else [])
    args = [x, w] + ([g] if mode == "silu_bwd" else [])
    return pl.pallas_call(
        body, name=name, out_shape=jax.ShapeDtypeStruct((t, c), F32), grid=(c // cb,),
        in_specs=in_specs, out_specs=col, compiler_params=_cparams(("parallel",)),
    )(*args)


def _win_both(ref, i, first, last, r, halo):
    zeros = jnp.zeros((halo, ref.shape[1]), F32)
    base = _base(i, r)
    if first and last:
        return jnp.concatenate([zeros, ref[pl.ds(0, r), :], zeros], axis=0)
    if first:
        return jnp.concatenate([zeros, ref[pl.ds(0, r + halo), :]], axis=0)
    if last:
        return jnp.concatenate([ref[pl.ds(base - halo, r + halo), :], zeros], axis=0)
    return ref[pl.ds(base - halo, r + 2 * halo), :]


def conv_bwd(x, w, dy, *, name, into=None, through_silu=False):
    t = x.shape[0]
    k, c = w.shape
    r, cb, halo = min(CONV_ROWS, t), min(CONV_COLS, c), _halo(k)
    nblk = t // r

    def body(x_ref, w_ref, dy_ref, *rest):
        dx_ref, dw_ref = rest[-2], rest[-1]
        dw_ref[...] = jnp.zeros_like(dw_ref)

        def blk(i, first, last):
            base = _base(i, r)
            dy_win = _win_bottom(dy_ref, i, last, r, halo)
            if through_silu:
                x_win = _win_both(x_ref, i, first, last, r, halo)
                again = _shifted_down(x_win, k, r + halo)
                cv = None
                for s in range(k):
                    term = w_ref[pl.ds(k - 1 - s, 1), :] * again[s]
                    cv = term if cv is None else cv + term
                sg = _sigmoid(cv)
                dy_win = dy_win * (sg * (1.0 + cv * (1.0 - sg)))
                x_top = x_win[:r + halo]
            else:
                x_top = _win_top(x_ref, i, first, r, halo)
            up = _shifted_up(dy_win, k, r)
            down = _shifted_down(x_top, k, r)
            dyb = up[0]
            acc = None
            for s in range(k):
                term = w_ref[pl.ds(k - 1 - s, 1), :] * up[s]
                acc = term if acc is None else acc + term
                dw_ref[pl.ds(k - 1 - s, 1), :] += jnp.sum(down[s] * dyb, axis=0, keepdims=True)
            dx_ref[pl.ds(base, r), :] = acc.astype(dx_ref.dtype)

        _for_blocks(nblk, blk)

    col = pl.BlockSpec((t, cb), lambda j: (0, j))
    wsp = pl.BlockSpec((k, cb), lambda j: (0, j))
    dx_shape = jax.ShapeDtypeStruct((t, c), F32) if into is None else jax.ShapeDtypeStruct(into.shape, into.dtype)
    return pl.pallas_call(
        body, name=name,
        out_shape=(dx_shape, jax.ShapeDtypeStruct((k, c), F32)), grid=(c // cb,),
        in_specs=[col, wsp, col] + ([] if into is None else [pl.BlockSpec(memory_space=pl.ANY)]),
        out_specs=(col, wsp),
        input_output_aliases={} if into is None else {3: 0},
        compiler_params=_cparams(("parallel",)),
    )(*([x, w, dy] + ([] if into is None else [into])))


_DIMS = {"nn": (((1,), (0,)), ((), ())), "nt": (((1,), (1,)), ((), ())), "tn": (((0,), (0,)), ((), ()))}
_DIMS_BATCHED = {"nn": (((2,), (1,)), ((0,), (0,))), "nt": (((2,), (2,)), ((0,), (0,))), "tn": (((1,), (1,)), ((0,), (0,)))}


def _mxu(a, b, mode):
    dims = _DIMS_BATCHED if a.ndim == 3 else _DIMS
    return lax.dot_general(a, b, dims[mode], preferred_element_type=F32)


def _split(x):
    hi = x.astype(BF16)
    return hi, (x - hi.astype(F32)).astype(BF16)


def _dot_raw(a, b, mode, prec):
    if prec == "bf16":
        return _mxu(a.astype(BF16), b.astype(BF16), mode)
    if prec == "x3":
        ah, al = _split(a)
        bh, bl = _split(b)
        return _mxu(ah, bh, mode) + (_mxu(ah, bl, mode) + _mxu(al, bh, mode))
    if prec == "x3r":
        bh, bm = _split(b)
        bl = (b - bh.astype(F32) - bm.astype(F32)).astype(BF16)
        ah = a.astype(BF16)
        return _mxu(ah, bh, mode) + (_mxu(ah, bm, mode) + _mxu(ah, bl, mode))
    raise ValueError(prec)


@functools.lru_cache(maxsize=None)
def _dot_fn(mode, prec):
    bprec = "x3" if prec == "x3r" else prec

    @jax.custom_vjp
    def f(a, b):
        return _dot_raw(a, b, mode, prec)

    def fwd(a, b):
        return _dot_raw(a, b, mode, prec), (a, b)

    def bwd(res, ct):
        a, b = res
        if mode == "nn":
            return _dot_raw(ct, b, "nt", bprec), _dot_raw(a, ct, "tn", bprec)
        if mode == "nt":
            return _dot_raw(ct, b, "nn", bprec), _dot_raw(ct, a, "tn", bprec)
        return _dot_raw(b, ct, "nt", bprec), _dot_raw(a, ct, "nn", bprec)

    f.defvjp(fwd, bwd)
    return f


def _dot(a, b, mode="nn", prec="bf16"):
    return _dot_fn(mode, prec)(a, b)


def _inv_product(l):
    n = l.shape[-1]
    eye = (lax.broadcasted_iota(jnp.int32, (n, n), 0) == lax.broadcasted_iota(jnp.int32, (n, n), 1)).astype(F32)
    p = eye - l
    pw = l
    for _ in range(5):
        pw = _dot_raw(pw, pw, "nn", "x3")
        p = _dot_raw(p, eye + pw, "nn", "x3")
    return p


@jax.custom_vjp
def _inv_unit_lower(l, t_saved):
    return t_saved


def _inv_fwd(l, t_saved):
    return t_saved, t_saved


def _inv_bwd(t, ct):
    tmp = _dot_raw(t, ct, "tn", "x3")
    return -_dot_raw(tmp, t, "nt", "x3"), jnp.zeros_like(t)


_inv_unit_lower.defvjp(_inv_fwd, _inv_bwd)


def _softplus(x):
    pos = x > 0
    return jnp.where(pos, x, 0.0) + jnp.log(1.0 + jnp.exp(jnp.where(pos, -x, x)))


def _l2n(x):
    return x * lax.rsqrt(jnp.sum(x * x, axis=-1, keepdims=True) + EPS)


def delta_pair(s0, qc, kc, vc, z, pba, alog, dtb, onorm, t_saved=None):
    n = PAIR
    nh = qc.shape[0]
    assert qc.shape == (nh, n, HEAD_DIM) and n == HEAD_DIM
    hi = lax.broadcasted_iota(jnp.int32, (nh, 1, pba.shape[1]), 0)
    li = lax.broadcasted_iota(jnp.int32, (nh, 1, pba.shape[1]), 2)
    braw = jnp.sum(pba[None] * (li == hi).astype(F32), axis=2, keepdims=True)
    araw = jnp.sum(pba[None] * (li == hi + nh).astype(F32), axis=2, keepdims=True)
    ri = lax.broadcasted_iota(jnp.int32, (n, n), 0)
    ci = lax.broadcasted_iota(jnp.int32, (n, n), 1)
    same = (ri // CHUNK) == (ci // CHUNK)
    tri = same & (ci <= ri)
    same_f = jnp.broadcast_to(same.astype(F32), (nh, n, n))
    tri_f = jnp.broadcast_to(tri.astype(F32), (nh, n, n))
    strict_f = (same & (ci < ri)).astype(F32)
    m0 = (lax.broadcasted_iota(jnp.int32, (n, 1), 0) < CHUNK).astype(F32)
    m1 = 1.0 - m0

    q = _l2n(qc) * (HEAD_DIM ** -0.5)
    k = _l2n(kc)
    beta = _sigmoid(braw)
    g = -jnp.exp(alog) * _softplus(araw + dtb)
    gb = jnp.broadcast_to(g, (nh, n, n))
    gc = _dot(tri_f, gb, "nn", "x3r")
    gtot = _dot(same_f, gb, "nn", "x3r")
    decay = jnp.exp(jnp.where(tri, gc - jnp.swapaxes(gc, 1, 2), -1e30))
    eg = jnp.exp(gc)
    kb, vb = k * beta, vc * beta
    l = _dot(kb, k, "nt") * decay * strict_f
    if t_saved is None:
        tinv = _inv_product(l)
    else:
        tinv = _inv_unit_lower(l, t_saved)
    u = _dot(tinv, vb, "nn", "x3")
    w = _dot(tinv, kb * eg, "nn", "x3")
    attn = _dot(q, k, "nt") * decay
    q_dec = q * eg
    k_tail = k * jnp.exp(gtot - gc)
    gl0 = jnp.exp(jnp.sum(gb * m0, axis=1, keepdims=True))
    gl1 = jnp.exp(jnp.sum(gb * m1, axis=1, keepdims=True))

    vn0 = m0 * (u - _dot(w, s0))
    s1 = s0 * gl0 + _dot(k_tail, vn0, "tn")
    vn1 = m1 * (u - _dot(w, s1))
    o = m0 * _dot(q_dec, s0) + m1 * _dot(q_dec, s1) + _dot(attn, vn0 + vn1)
    s2 = s1 * gl1 + _dot(k_tail, vn1, "tn")

    on = o * lax.rsqrt(jnp.mean(o * o, axis=-1, keepdims=True) + EPS) * onorm
    return on * _silu(z), s2, tinv


def _hcols(i):
    return slice(i * HEAD_DIM, (i + 1) * HEAD_DIM)


def _heads(ref, first, count):
    return jnp.stack([ref[:, _hcols(first + i)] for i in range(count)])


def delta_fwd(qkv, pm, pba, alog, dtb, onorm, cat_width, *, name):
    t = qkv.shape[0]
    h = alog.shape[0]
    hd = h * HEAD_DIM
    npair = t // PAIR
    mat = pl.BlockSpec((h, None, PAIR, HEAD_DIM), lambda p: (0, p, 0, 0))
    par = pl.BlockSpec((h, 1, 1), lambda p: (0, 0, 0))

    def body(qkv_ref, z_ref, pba_ref, al_ref, dt_ref, on_ref, o_ref, st_ref, ti_ref, s_scr):
        @pl.when(pl.program_id(0) == 0)
        def _():
            s_scr[...] = jnp.zeros_like(s_scr)

        s0 = s_scr[...]
        st_ref[...] = s0
        out, s2, tinv = delta_pair(s0, _heads(qkv_ref, 0, h), _heads(qkv_ref, h, h), _heads(qkv_ref, 2 * h, h), _heads(z_ref, 0, h),
                                   pba_ref[...], al_ref[...], dt_ref[...], on_ref[...])
        for i in range(h):
            o_ref[:, _hcols(i)] = out[i].astype(o_ref.dtype)
        ti_ref[...] = tinv
        s_scr[...] = s2

    return pl.pallas_call(
        body, name=name, grid=(npair,),
        out_shape=(jax.ShapeDtypeStruct((t, cat_width), BF16),
                   jax.ShapeDtypeStruct((h, npair, PAIR, HEAD_DIM), F32),
                   jax.ShapeDtypeStruct((h, npair, PAIR, PAIR), F32)),
        in_specs=[pl.BlockSpec((PAIR, 3 * hd), lambda p: (p, 0)), pl.BlockSpec((PAIR, hd), lambda p: (p, 3)),
                  pl.BlockSpec((PAIR, pba.shape[1]), lambda p: (p, 0)), par, par,
                  pl.BlockSpec((1, HEAD_DIM), lambda p: (0, 0))],
        out_specs=(pl.BlockSpec((PAIR, hd), lambda p: (p, 0)), mat, mat),
        scratch_shapes=[pltpu.VMEM((h, HEAD_DIM, HEAD_DIM), F32)],
        compiler_params=_cparams(("arbitrary",)),
    )(qkv, pm, pba, alog, dtb, onorm)


def delta_bwd(qkv, pm, pba, alog, dtb, onorm, states, tinvs, dcat, *, name):
    t = qkv.shape[0]
    h = alog.shape[0]
    hd = h * HEAD_DIM
    npair = t // PAIR
    rev = lambda p: npair - 1 - p
    mat = pl.BlockSpec((h, None, PAIR, HEAD_DIM), lambda p: (0, rev(p), 0, 0))
    par = pl.BlockSpec((h, 1, 1), lambda p: (0, 0, 0))
    onsp = pl.BlockSpec((1, HEAD_DIM), lambda p: (0, 0))
    wide = pl.BlockSpec((PAIR, 3 * hd), lambda p: (rev(p), 0))
    zsp = pl.BlockSpec((PAIR, hd), lambda p: (rev(p), 3))
    bsp = pl.BlockSpec((PAIR, pba.shape[1]), lambda p: (rev(p), 0))

    def body(qkv_ref, z_ref, pba_ref, al_ref, dt_ref, on_ref, st_ref, ti_ref, dc_ref,
             dqkv_ref, dz_ref, dpba_ref, dal_ref, ddt_ref, don_ref, ds_scr):
        @pl.when(pl.program_id(0) == 0)
        def _():
            ds_scr[...] = jnp.zeros_like(ds_scr)
            dal_ref[...] = jnp.zeros_like(dal_ref)
            ddt_ref[...] = jnp.zeros_like(ddt_ref)
            don_ref[...] = jnp.zeros_like(don_ref)

        tsv = ti_ref[...]
        fn = lambda s0, qc, kc, vc, z, pb, al, dt, on: delta_pair(s0, qc, kc, vc, z, pb, al, dt, on, tsv)[:2]
        _, vjp = jax.vjp(fn, st_ref[...], _heads(qkv_ref, 0, h), _heads(qkv_ref, h, h), _heads(qkv_ref, 2 * h, h),
                         _heads(z_ref, 0, h), pba_ref[...], al_ref[...], dt_ref[...], on_ref[...])
        ds0, dq, dk, dv, dz, dpba, dal, ddt, don = vjp((_heads(dc_ref, 0, h), ds_scr[...]))
        ds_scr[...] = ds0
        for i in range(h):
            dqkv_ref[:, _hcols(i)] = dq[i]
            dqkv_ref[:, _hcols(h + i)] = dk[i]
            dqkv_ref[:, _hcols(2 * h + i)] = dv[i]
            dz_ref[:, _hcols(i)] = dz[i].astype(dz_ref.dtype)
        dal_ref[...] += dal
        ddt_ref[...] += ddt
        dpba_ref[...] = dpba.astype(dpba_ref.dtype)
        don_ref[...] += don

    return pl.pallas_call(
        body, name=name, grid=(npair,),
        out_shape=(jax.ShapeDtypeStruct((t, 3 * hd), F32), jax.ShapeDtypeStruct(pm.shape, BF16),
                   jax.ShapeDtypeStruct(pba.shape, BF16),
                   jax.ShapeDtypeStruct((h, 1, 1), F32), jax.ShapeDtypeStruct((h, 1, 1), F32),
                   jax.ShapeDtypeStruct((1, HEAD_DIM), F32)),
        in_specs=[wide, zsp, bsp, par, par, onsp, mat, mat, pl.BlockSpec((PAIR, hd), lambda p: (rev(p), 0))],
        out_specs=(wide, zsp, bsp, par, par, onsp),
        scratch_shapes=[pltpu.VMEM((h, HEAD_DIM, HEAD_DIM), F32)],
        compiler_params=_cparams(("arbitrary",)),
    )(qkv, pm, pba, alog, dtb, onorm, states, tinvs, dcat)


def gmlp_block(ln_g, ln_b, w, bcol, u_raw, v_raw):
    n = w.shape[-1]
    ri = lax.broadcasted_iota(jnp.int32, (n, n), 0)
    ci = lax.broadcasted_iota(jnp.int32, (n, n), 1)
    mask = ((ci // CHUNK) <= (ri // CHUNK)).astype(F32)
    vg = _gelu(v_raw)
    vc = vg - jnp.mean(vg, axis=-1, keepdims=True)
    vgn = vc * lax.rsqrt(jnp.mean(vc * vc, axis=-1, keepdims=True) + EPS) * ln_g + ln_b
    return _gelu(u_raw) * (_dot(w * mask, vgn) + bcol)


def gmlp_fwd(pm, ln_g, ln_b, w_s, bcol, cat, *, name):
    t = pm.shape[0]
    g = w_s.shape[0]
    gw = g * HEAD_DIM
    assert pm.shape[1] == 6 * gw and cat.shape[1] == 2 * gw

    def body(u_ref, v_ref, lg_ref, lb_ref, w_ref, b_ref, cat_in, o_ref):
        del cat_in
        out = gmlp_block(lg_ref[...], lb_ref[...], w_ref[...], b_ref[...], _heads(u_ref, 0, g), _heads(v_ref, 0, g))
        for i in range(g):
            o_ref[:, _hcols(i)] = out[i].astype(o_ref.dtype)

    full = lambda a: pl.BlockSpec(a.shape, lambda m: (0, 0, 0))
    return pl.pallas_call(
        body, name=name, grid=(t // HEAD_DIM,),
        out_shape=jax.ShapeDtypeStruct(cat.shape, cat.dtype),
        in_specs=[pl.BlockSpec((HEAD_DIM, gw), lambda m: (m, 4)), pl.BlockSpec((HEAD_DIM, gw), lambda m: (m, 5)),
                  full(ln_g), full(ln_b), full(w_s), full(bcol), pl.BlockSpec(memory_space=pl.ANY)],
        out_specs=pl.BlockSpec((HEAD_DIM, gw), lambda m: (m, 1)),
        input_output_aliases={6: 0},
        compiler_params=_cparams(("arbitrary",)),
    )(pm, pm, ln_g, ln_b, w_s, bcol, cat)


def gmlp_bwd(pm, ln_g, ln_b, w_s, bcol, dcat, dpm, *, name):
    t = pm.shape[0]
    g = w_s.shape[0]
    gw = g * HEAD_DIM
    assert pm.shape[1] == 6 * gw and dpm.shape == pm.shape

    def body(u_ref, v_ref, lg_ref, lb_ref, w_ref, b_ref, dc_ref, dpm_in, duv_ref, dlg_ref, dlb_ref, dw_ref, db_ref):
        del dpm_in
        first = pl.program_id(0) == 0
        _, vjp = jax.vjp(gmlp_block, lg_ref[...], lb_ref[...], w_ref[...], b_ref[...], _heads(u_ref, 0, g), _heads(v_ref, 0, g))
        dlg, dlb, dw, db, du, dv = vjp(_heads(dc_ref, 0, g))
        for i in range(g):
            duv_ref[:, _hcols(i)] = du[i].astype(duv_ref.dtype)
            duv_ref[:, _hcols(g + i)] = dv[i].astype(duv_ref.dtype)
        for ref, val in ((dlg_ref, dlg), (dlb_ref, dlb), (dw_ref, dw), (db_ref, db)):
            @pl.when(first)
            def _(ref=ref, val=val):
                ref[...] = val

            @pl.when(jnp.logical_not(first))
            def _(ref=ref, val=val):
                ref[...] += val

    full = lambda a: pl.BlockSpec(a.shape, lambda m: (0, 0, 0))
    return pl.pallas_call(
        body, name=name, grid=(t // HEAD_DIM,),
        out_shape=(jax.ShapeDtypeStruct(dpm.shape, dpm.dtype),
                   jax.ShapeDtypeStruct(ln_g.shape, F32), jax.ShapeDtypeStruct(ln_b.shape, F32),
                   jax.ShapeDtypeStruct(w_s.shape, F32), jax.ShapeDtypeStruct(bcol.shape, F32)),
        in_specs=[pl.BlockSpec((HEAD_DIM, gw), lambda m: (m, 4)), pl.BlockSpec((HEAD_DIM, gw), lambda m: (m, 5)),
                  full(ln_g), full(ln_b), full(w_s), full(bcol), pl.BlockSpec((HEAD_DIM, gw), lambda m: (m, 1)),
                  pl.BlockSpec(memory_space=pl.ANY)],
        out_specs=(pl.BlockSpec((HEAD_DIM, 2 * gw), lambda m: (m, 2)), full(ln_g), full(ln_b), full(w_s), full(bcol)),
        input_output_aliases={7: 0},
        compiler_params=_cparams(("arbitrary",)),
    )(pm, pm, ln_g, ln_b, w_s, bcol, dcat, dpm)


def loss_head(g, x, r, tgt, *, name, rows=ROWS):
    t, d = x.shape

    def body(g_ref, x_ref, r_ref, t_ref, l_ref, dx_ref, dxb_ref, dg_ref):
        y, vjp = jax.vjp(lambda gg, xx: _rms(xx, gg), g_ref[...], x_ref[...] + r_ref[...])
        e = y - t_ref[...]
        part = (0.5 / d) * jnp.sum(jnp.sum(e * e, axis=1, keepdims=True), axis=0, keepdims=True)
        dg, dx = vjp(e * (1.0 / d))
        dx_ref[...] = dx
        dxb_ref[...] = dx.astype(BF16)
        first = pl.program_id(0) == 0

        @pl.when(first)
        def _():
            l_ref[...] = part
            dg_ref[...] = dg

        @pl.when(jnp.logical_not(first))
        def _():
            l_ref[...] += part
            dg_ref[...] += dg

    rs = pl.BlockSpec((rows, d), lambda i: (i, 0))
    gs = pl.BlockSpec((1, d), lambda i: (0, 0))
    return pl.pallas_call(
        body, name=name, grid=(t // rows,),
        out_shape=(jax.ShapeDtypeStruct((1, 1), F32), jax.ShapeDtypeStruct((t, d), F32),
                   jax.ShapeDtypeStruct((t, d), BF16), jax.ShapeDtypeStruct((1, d), F32)),
        in_specs=[gs, rs, rs, rs],
        out_specs=(pl.BlockSpec((1, 1), lambda i: (0, 0)), rs, rs, gs),
        compiler_params=_cparams(("arbitrary",)),
    )(g, x, r, tgt)


def adamw(w, gs, m, v, *, name, behind=None):
    nl, r, c = w.shape
    assert len(gs) == nl
    if r % SUBLANES == 0:
        tr, tc = _pick(r, (256, 128, 64, 32, 16, 8)), c
    else:
        tr, tc = r, _pick(c, (256, 128))
    k1 = 1.0 - ADAM_B1 ** ADAM_STEP
    k2 = 1.0 - ADAM_B2 ** ADAM_STEP

    def body(*refs):
        w_ref, m_ref, v_ref = refs[0], refs[1], refs[2]
        g_refs = refs[3:3 + nl]
        go_ref, d_ref, mo_ref, vo_ref = refs[-4:]
        gg = g_refs[0][...]
        for li in range(1, nl):
            gg = jnp.where(pl.program_id(0) == li, g_refs[li][...], gg)
        mn = ADAM_B1 * m_ref[...] + (1.0 - ADAM_B1) * gg
        vn = ADAM_B2 * v_ref[...] + (1.0 - ADAM_B2) * (gg * gg)
        go_ref[...] = gg
        d_ref[...] = -ADAM_LR * ((mn / k1) / (jnp.sqrt(vn / k2) + ADAM_EPS) + ADAM_WD * w_ref[...])
        mo_ref[...] = mn
        vo_ref[...] = vn

    if tc == c:
        sp = pl.BlockSpec((None, tr, c), lambda l, i: (l, i, 0))
        gsp = pl.BlockSpec((tr, c), lambda l, i: (i, 0))
    else:
        sp = pl.BlockSpec((None, r, tc), lambda l, i: (l, 0, i))
        gsp = pl.BlockSpec((r, tc), lambda l, i: (0, i))
    sds = jax.ShapeDtypeStruct((nl, r, c), F32)
    extra = [] if behind is None else [behind]
    return pl.pallas_call(
        body, name=name, grid=(nl, (r // tr) * (c // tc)), out_shape=(sds, sds, sds, sds),
        in_specs=[sp, sp, sp] + [gsp] * nl + [pl.BlockSpec(e.shape, lambda l, i: (0, 0)) for e in extra], out_specs=(sp, sp, sp, sp),
        compiler_params=_cparams(("parallel", "parallel")),
    )(w, m, v, *gs, *extra)


MESH = pl.DeviceIdType.MESH
ANY = pl.BlockSpec(memory_space=pl.ANY)


def _me():
    return lax.axis_index("x"), lax.axis_index("y"), lax.axis_index("c")


def _other_chips(x, y):
    return [(1 - x, y), (x, 1 - y), (1 - x, 1 - y)]


HBM = pl.BlockSpec(memory_space=pltpu.HBM)
SEM = pl.BlockSpec(memory_space=pltpu.SEMAPHORE)
EFFECT = pltpu.SideEffectType.DATAFLOW_SIDE_EFFECTING


def _hbm(a):
    return pltpu.with_memory_space_constraint(a, pltpu.HBM)


def _half_rows(arr, which):
    hrows = arr.shape[-2] // 2
    return pl.ds(which * hrows, hrows)


def _ici_copies(srcs, lands, sems):
    x, y, c = _me()
    chip = 2 * x + y
    return [pltpu.make_async_remote_copy(src_ref=srcs[i].at[_half_rows(srcs[i], c)], dst_ref=lands[i].at[chip, _half_rows(srcs[i], c)],
                                         send_sem=sems[0].at[3 * i + j], recv_sem=sems[1].at[3 * i + j],
                                         device_id=(ch[0], ch[1], c), device_id_type=MESH)
            for i in range(len(srcs)) for j, ch in enumerate(_other_chips(x, y))]


def _own_copies(srcs, lands, sems):
    x, y, c = _me()
    return [pltpu.make_async_remote_copy(src_ref=srcs[i], dst_ref=lands[i].at[2 * x + y], send_sem=sems[0].at[i],
                                         recv_sem=sems[1].at[i], device_id=(x, y, 1 - c), device_id_type=MESH)
            for i in range(len(srcs))]


def _fwd_copies(srcs, lands, sems, half):
    x, y, c = _me()
    cps = []
    for i in range(len(srcs)):
        for j, ch in enumerate(_other_chips(x, y)):
            part = lands[i].at[2 * ch[0] + ch[1], _half_rows(srcs[i], half)]
            cps.append(pltpu.make_async_remote_copy(src_ref=part, dst_ref=part, send_sem=sems[0].at[3 * i + j],
                                                    recv_sem=sems[1].at[3 * i + j], device_id=(x, y, 1 - c), device_id_type=MESH))
    return cps


def gather_start(groups, *, name):
    arrs = [a for g in groups for a in g]
    n, ng = len(arrs), len(groups)
    bounds = np.cumsum([0] + [len(g) for g in groups])

    def body(*refs):
        srcs, lands = refs[:n], refs[n:2 * n]
        sems = refs[2 * n:2 * n + 4 * ng]
        token = refs[-1]
        for gi in range(ng):
            lo, hi = bounds[gi], bounds[gi + 1]
            for cp in _ici_copies(srcs[lo:hi], lands[lo:hi], sems[4 * gi:4 * gi + 2]):
                cp.start()
        for gi in range(ng):
            lo, hi = bounds[gi], bounds[gi + 1]
            for cp in _own_copies(srcs[lo:hi], lands[lo:hi], sems[4 * gi + 2:4 * gi + 4]):
                cp.start()
        token[...] = jnp.zeros_like(token)

    sem_shapes = []
    for g in groups:
        sem_shapes += [pltpu.SemaphoreType.DMA((3 * len(g),))] * 2 + [pltpu.SemaphoreType.DMA((len(g),))] * 2
    res = pl.pallas_call(
        body, name=name,
        out_shape=tuple(sem_shapes) + tuple(pltpu.HBM(a.shape, a.dtype) for a in arrs)
        + tuple(pltpu.HBM((N_CHIPS,) + a.shape, a.dtype) for a in arrs) + (jax.ShapeDtypeStruct((SUBLANES, LANES), F32),),
        in_specs=[HBM] * (2 * n),
        out_specs=tuple([SEM] * (4 * ng)) + tuple([HBM] * (2 * n)) + (pl.BlockSpec(memory_space=pltpu.VMEM),),
        input_output_aliases={i: 4 * ng + i for i in range(2 * n)},
        compiler_params=pltpu.CompilerParams(has_side_effects=EFFECT),
    )(*[_hbm(a) for a in arrs], *[_hbm(lax.empty((N_CHIPS,) + a.shape, a.dtype)) for a in arrs])
    sems, thru, lands, token = res[:4 * ng], res[4 * ng:4 * ng + n], res[4 * ng + n:4 * ng + 2 * n], res[-1]
    out = [{"ici": (sems[4 * gi], sems[4 * gi + 1]), "own": (sems[4 * gi + 2], sems[4 * gi + 3]),
            "thru": list(thru[bounds[gi]:bounds[gi + 1]]), "lands": list(lands[bounds[gi]:bounds[gi + 1]])} for gi in range(ng)]
    return out, token


def gather_forward(group, after, *, name):
    thru, lands = group["thru"], group["lands"]
    n = len(thru)

    def body(*refs):
        srcs, lands_r = refs[:n], refs[n:2 * n]
        ici = refs[2 * n:2 * n + 2]
        fwd = refs[2 * n + 3 + 2 * n:2 * n + 3 + 2 * n + 2]
        for cp in _ici_copies(srcs, lands_r, ici):
            cp.wait_send()
            cp.wait_recv()
        for cp in _fwd_copies(srcs, lands_r, fwd, _me()[2]):
            cp.start()
        refs[-1][...] = jnp.zeros_like(refs[-1])

    res = pl.pallas_call(
        body, name=name,
        out_shape=tuple(pltpu.HBM(a.shape, a.dtype) for a in thru) + tuple(pltpu.HBM(a.shape, a.dtype) for a in lands)
        + (pltpu.SemaphoreType.DMA((3 * n,)),) * 2 + (jax.ShapeDtypeStruct((SUBLANES, LANES), F32),),
        in_specs=[HBM] * (2 * n) + [SEM, SEM, pl.BlockSpec(memory_space=pl.ANY)],
        out_specs=tuple([HBM] * (2 * n)) + (SEM, SEM, pl.BlockSpec(memory_space=pltpu.VMEM)),
        input_output_aliases={i: i for i in range(2 * n)},
        compiler_params=pltpu.CompilerParams(has_side_effects=EFFECT),
    )(*thru, *lands, *group["ici"], after)
    return {"own": group["own"], "fwd": (res[2 * n], res[2 * n + 1]), "thru": list(res[:n]), "lands": list(res[n:2 * n]),
            "token": res[-1]}


def gather_wait(group, after, *, name):
    thru, lands = group["thru"], group["lands"]
    n = len(thru)

    def body(*refs):
        srcs, lands_r = refs[:n], refs[n:2 * n]
        own, fwd = refs[2 * n:2 * n + 2], refs[2 * n + 2:2 * n + 4]
        c = _me()[2]
        for mine, theirs in zip(_fwd_copies(srcs, lands_r, fwd, c), _fwd_copies(srcs, lands_r, fwd, 1 - c)):
            mine.wait_send()
            theirs.wait_recv()
        for cp in _own_copies(srcs, lands_r, own):
            cp.wait_send()
            cp.wait_recv()

    res = pl.pallas_call(
        body, name=name,
        out_shape=tuple(pltpu.HBM(a.shape, a.dtype) for a in thru) + tuple(pltpu.HBM(a.shape, a.dtype) for a in lands),
        in_specs=[HBM] * (2 * n) + [SEM] * 4 + [pl.BlockSpec(memory_space=pl.ANY)],
        out_specs=tuple([HBM] * (2 * n)),
        input_output_aliases={i: i for i in range(2 * n)},
        compiler_params=pltpu.CompilerParams(has_side_effects=EFFECT),
    )(*thru, *lands, *group["own"], *group["fwd"], after)
    return list(res[n:])


def _pair_copies(srcs, lands, sems):
    x, y, c = _me()
    return [pltpu.make_async_remote_copy(src_ref=srcs[i].at[:, _half_rows(srcs[i], 1 - c), :], dst_ref=lands[i], send_sem=sems[0].at[i],
                                         recv_sem=sems[1].at[i], device_id=(x, y, 1 - c), device_id_type=MESH)
            for i in range(len(srcs))]


def _scatter_copies(srcs, lands, sems):
    x, y, c = _me()
    return [pltpu.make_async_remote_copy(src_ref=srcs[i].at[2 * ch[0] + ch[1]], dst_ref=lands[i].at[j], send_sem=sems[0].at[3 * i + j],
                                         recv_sem=sems[1].at[3 * i + j], device_id=(ch[0], ch[1], c), device_id_type=MESH)
            for i in range(len(srcs)) for j, ch in enumerate(_other_chips(x, y))]


def split_start(copies, arrs, land_shapes, nsem, *, name):
    n = len(arrs)

    def body(*refs):
        for cp in copies(refs[:n], refs[n:2 * n], refs[2 * n:2 * n + 2]):
            cp.start()
        refs[-1][...] = jnp.zeros_like(refs[-1])

    res = pl.pallas_call(
        body, name=name,
        out_shape=(pltpu.SemaphoreType.DMA((nsem,)),) * 2 + tuple(pltpu.HBM(a.shape, a.dtype) for a in arrs)
        + tuple(pltpu.HBM(s, a.dtype) for s, a in zip(land_shapes, arrs)) + (jax.ShapeDtypeStruct((SUBLANES, LANES), F32),),
        in_specs=[HBM] * (2 * n), out_specs=(SEM, SEM) + tuple([HBM] * (2 * n)) + (pl.BlockSpec(memory_space=pltpu.VMEM),),
        input_output_aliases={i: 2 + i for i in range(2 * n)},
        compiler_params=pltpu.CompilerParams(has_side_effects=EFFECT),
    )(*[_hbm(a) for a in arrs], *[_hbm(lax.empty(s, a.dtype)) for s, a in zip(land_shapes, arrs)])
    return {"copies": copies, "sems": (res[0], res[1]), "thru": list(res[2:2 + n]), "lands": list(res[2 + n:2 + 2 * n]),
            "token": res[-1]}


def split_wait(state, after, *, name):
    thru, lands, copies = state["thru"], state["lands"], state["copies"]
    n = len(thru)

    def body(*refs):
        for cp in copies(refs[:n], refs[n:2 * n], refs[2 * n:2 * n + 2]):
            cp.wait_send()
            cp.wait_recv()

    res = pl.pallas_call(
        body, name=name,
        out_shape=tuple(pltpu.HBM(a.shape, a.dtype) for a in thru) + tuple(pltpu.HBM(a.shape, a.dtype) for a in lands),
        in_specs=[HBM] * (2 * n) + [SEM, SEM, pl.BlockSpec(memory_space=pl.ANY)], out_specs=tuple([HBM] * (2 * n)),
        input_output_aliases={i: i for i in range(2 * n)},
        compiler_params=pltpu.CompilerParams(has_side_effects=EFFECT),
    )(*thru, *lands, *state["sems"], after)
    return list(res[:n]), list(res[n:])


def sibling_join_halves(arrs, *, name):
    n = len(arrs)

    def body(*refs):
        outs = refs[n:2 * n]
        send_sems, recv_sems = refs[2 * n:]
        x, y, c = _me()
        cps = []
        for i in range(n):
            hrows = arrs[i].shape[0] // 2
            mine = outs[i].at[pl.ds(c * hrows, hrows)]
            cps.append(pltpu.make_async_remote_copy(src_ref=mine, dst_ref=mine, send_sem=send_sems.at[i],
                                                    recv_sem=recv_sems.at[i], device_id=(x, y, 1 - c), device_id_type=MESH))
        for cp in cps:
            cp.start()
        for i in range(n):
            hrows = arrs[i].shape[0] // 2
            theirs = outs[i].at[pl.ds((1 - c) * hrows, hrows)]
            pltpu.make_async_remote_copy(src_ref=theirs, dst_ref=theirs, send_sem=send_sems.at[i], recv_sem=recv_sems.at[i],
                                         device_id=(x, y, 1 - c), device_id_type=MESH).wait_recv()
        for cp in cps:
            cp.wait_send()

    return pl.pallas_call(
        body, name=name,
        out_shape=tuple(jax.ShapeDtypeStruct(a.shape, a.dtype) for a in arrs),
        in_specs=[ANY] * n, out_specs=tuple([ANY] * n),
        input_output_aliases={i: i for i in range(n)},
        scratch_shapes=[pltpu.SemaphoreType.DMA((n,)), pltpu.SemaphoreType.DMA((n,))],
        compiler_params=pltpu.CompilerParams(has_side_effects=True),
    )(*arrs)


def pre_reduce(gd, sib, who, *, name):
    _, r, c = gd.shape
    h = r // 2
    tr = _pick(h, (256, 128, 64, 32, 16, 8))
    nrb = h // tr

    def body(who_ref, gd_ref, sib_ref, pb_ref, own_ref):
        p = gd_ref[...] + sib_ref[...]
        pb_ref[...] = p.astype(BF16)

        @pl.when(pl.program_id(1) == who_ref[1])
        def _():
            own_ref[...] = p

    return pl.pallas_call(
        body, name=name,
        out_shape=(jax.ShapeDtypeStruct((N_CHIPS, h, c), BF16), jax.ShapeDtypeStruct((h, c), F32)),
        grid_spec=pltpu.PrefetchScalarGridSpec(
            num_scalar_prefetch=1, grid=(nrb, N_CHIPS),
            in_specs=[pl.BlockSpec((None, tr, c), lambda i, s, w: (s, w[0] * nrb + i, 0)),
                      pl.BlockSpec((None, tr, c), lambda i, s, w: (s, i, 0))],
            out_specs=(pl.BlockSpec((None, tr, c), lambda i, s, w: (s, i, 0)),
                       pl.BlockSpec((tr, c), lambda i, s, w: (i, 0)))),
        compiler_params=_cparams(("parallel", "arbitrary")),
    )(who, gd, sib)


def final_reduce(own, rcv, who, *, name):
    h, c = own.shape
    tr = _pick(h, (256, 128, 64, 32, 16, 8))
    nrb = h // tr

    def body(who_ref, own_ref, rcv_ref, o_ref):
        del who_ref
        acc = own_ref[...]
        for j in range(3):
            acc = acc + rcv_ref[j].astype(F32)
        o_ref[...] = acc

    return pl.pallas_call(
        body, name=name, out_shape=jax.ShapeDtypeStruct((2 * h, c), F32),
        grid_spec=pltpu.PrefetchScalarGridSpec(
            num_scalar_prefetch=1, grid=(nrb,),
            in_specs=[pl.BlockSpec((tr, c), lambda i, w: (i, 0)), pl.BlockSpec((3, tr, c), lambda i, w: (0, i, 0))],
            out_specs=pl.BlockSpec((tr, c), lambda i, w: (w[0] * nrb + i, 0))),
        compiler_params=_cparams(("parallel",)),
    )(who, own, rcv)


def sum8(landed, own, me, *, name):
    _, r, c = landed.shape
    tr = _pick(r, (256, 128, 64, 32, 16, 8))

    def body(me_ref, a_ref, own_ref, o_ref):
        acc = None
        for j in range(8):
            term = jnp.where(me_ref[0] == j, own_ref[...], a_ref[j])
            acc = term if acc is None else acc + term
        o_ref[...] = acc

    return pl.pallas_call(
        body, name=name, out_shape=jax.ShapeDtypeStruct((r, c), F32),
        grid_spec=pltpu.PrefetchScalarGridSpec(
            num_scalar_prefetch=1, grid=(r // tr,),
            in_specs=[pl.BlockSpec((8, tr, c), lambda i, w: (0, i, 0)), pl.BlockSpec((tr, c), lambda i, w: (i, 0))],
            out_specs=pl.BlockSpec((tr, c), lambda i, w: (i, 0))),
        compiler_params=_cparams(("parallel",)),
    )(me, landed, own)


def _everyone_copies(srcs, lands, sems):
    x, y, c = _me()
    me = 4 * x + 2 * y + c
    cps = []
    for k in range(1, 8):
        fx, fy, fc = (k >> 2) & 1, (k >> 1) & 1, k & 1
        to = (x + fx - 2 * x * fx, y + fy - 2 * y * fy, c + fc - 2 * c * fc)
        cps.append(pltpu.make_async_remote_copy(src_ref=srcs[0], dst_ref=lands[0].at[me], send_sem=sems[0].at[k - 1],
                                                recv_sem=sems[1].at[k - 1], device_id=to, device_id_type=MESH))
    return cps


def _behind(p, *tokens):
    for tk in tokens:
        if tk is not None:
            p = p + tk[0, 0]
    return p


def local_step(x, tgt, wb, ws, arrive=lambda layer, event, after: {}, reduce=lambda group, event, gb, after: None, h0=None):
    t, d = x.shape
    nh = ws["alog"].shape[0]
    ng = ws["w_s"].shape[0]
    assert nh == ng
    mix_w = (nh + ng) * HEAD_DIM

    if h0 is None:
        (h0,) = rows_fwd(stage_norm, [ws["e_norm"]], [x], [(1, BF16, d)], name="f_norm_e")
    pm = mm_nt(h0, wb["w_main_t"], name="f_proj_main")
    pba = mm_nt(h0, wb["w_ba_t"], name="f_proj_ba")
    qkv = conv_fwd(pm, ws["conv_w"], mode="silu", name="f_conv4")
    cat, states, tinvs = delta_fwd(qkv, pm, pba, ws["alog"], ws["dtb"], ws["onorm"], mix_w, name="f_delta")
    tok = arrive("layer0", "landed", states).get("token")
    cat = gmlp_fwd(pm, _behind(ws["lng"], tok), ws["lnb"], ws["w_s"], ws["bs"], cat, name="f_gmlp")
    wb = {**wb, **arrive("layer0", "joined", cat)}
    y0 = mm_nn(cat, wb["w_out"], name="f_out")
    x1, h1 = rows_fwd(stage_res_norm, [ws["f_norm0"]], [x, y0], [(0, F32, d), (1, BF16, d)], name="f_norm_f0")
    a1, s1 = mm_nn(h1, wb["w1_0"], name="f_mlp0_up", epilogue="relu2")
    y1 = mm_nn(s1, wb["w2_0"], name="f_mlp0_down")
    tok = arrive("layer1", "landed", y1).get("token")
    x2, h2 = rows_fwd(stage_res_norm, [_behind(ws["o_norm"], tok)], [x1, y1], [(0, F32, d), (1, BF16, d)], name="f_norm_o")
    wb = {**wb, **arrive("layer1", "joined", h2)}
    zz = mm_nn(h2, wb["pw1"], name="f_pw1")
    zparts = [(zz, d, 0), (zz, d, 1)]
    (gl,) = rows_fwd(stage_glu, [ws["b1a"], ws["b1b"]], zparts, [(0, F32, d)], name="f_glu")
    cv = conv_fwd(gl, ws["dw"], mode="plain", name="f_conv31")
    ln_params = [ws["dw_b"], ws["ln_g"], ws["ln_b"]]
    (sl,) = rows_fwd(stage_ln_silu, ln_params, [cv], [(0, BF16, d)], name="f_ln_silu")
    y2 = mm_nn(sl, wb["pw2"], name="f_pw2")
    x3, h3 = rows_fwd(stage_res_bias_norm, [ws["b2"], ws["f_norm1"]], [x2, y2], [(0, F32, d), (1, BF16, d)], name="f_norm_f1")
    a3, s3 = mm_nn(h3, wb["w1_1"], name="f_mlp1_up", epilogue="relu2")
    y3 = mm_nn(s3, wb["w2_1"], name="f_mlp1_down")
    loss, d4, d4b, g_final = loss_head(ws["final_norm"], x3, y3, tgt, name="loss_head")

    gb, gs = {}, {"final_norm": g_final}
    s_up = wb["w1_0"].shape[0]

    gb["w2_1"] = mm_tn(s3, d4b, groups=1, name="b_mlp1_down_w")
    dpre3 = mm_nt(d4b, wb["w2_1"], name="b_mlp1_down_x", mul=a3, out_dtype=BF16)
    gb["w1_1"] = mm_tn(h3, dpre3, groups=s_up, name="b_mlp1_up_w")
    tok = reduce("mlp1", "ready", gb, None)
    dh3 = mm_nt(dpre3, wb["w1_1"], name="b_mlp1_up_x", out_dtype=BF16)
    d3, d3b, gs["b2"], gs["f_norm1"] = rows_bwd(stage_res_bias_norm, [ws["b2"], _behind(ws["f_norm1"], tok)], [x2, y2], [d4, dh3],
                                                [(0, F32), (1, BF16)], name="b_norm_f1")
    tok = reduce("mlp1", "paired", gb, d3)
    gb["pw2"] = mm_tn(sl, d3b, groups=1, name="b_pw2_w")
    dsl = mm_nt(d3b, wb["pw2"], name="b_pw2_x", out_dtype=BF16)
    dcv, gs["dw_b"], gs["ln_g"], gs["ln_b"] = rows_bwd(stage_ln_silu, [_behind(ln_params[0], tok)] + ln_params[1:], [cv], [dsl],
                                                       [(0, F32)], name="b_ln_silu")
    dgl, gs["dw"] = conv_bwd(gl, ws["dw"], dcv, name="b_conv31")
    dzz, gs["b1a"], gs["b1b"] = rows_bwd(stage_glu, [ws["b1a"], ws["b1b"]], zparts, [dgl], [(0, BF16), (1, BF16)],
                                         name="b_glu", concat=True)
    gb["pw1"] = mm_tn(h2, dzz, groups=wb["pw1"].shape[0], name="b_pw1_w")
    tok = reduce("conv", "ready", gb, None)
    dh2 = mm_nt(dzz, wb["pw1"], name="b_pw1_x", out_dtype=BF16)
    d2, d2b, gs["o_norm"] = rows_bwd(stage_res_norm, [_behind(ws["o_norm"], tok)], [x1, y1], [d3, dh2], [(0, F32), (1, BF16)],
                                     name="b_norm_o")
    tok_conv = reduce("conv", "paired", gb, d2)
    gb["w2_0"] = mm_tn(s1, d2b, groups=1, name="b_mlp0_down_w")
    dpre1 = mm_nt(d2b, wb["w2_0"], name="b_mlp0_down_x", mul=a1, out_dtype=BF16)
    gb["w1_0"] = mm_tn(h1, dpre1, groups=s_up, name="b_mlp0_up_w")
    tok = reduce("mlp0", "ready", gb, None)
    dh1 = mm_nt(dpre1, wb["w1_0"], name="b_mlp0_up_x", out_dtype=BF16)
    d1, d1b, gs["f_norm0"] = rows_bwd(stage_res_norm, [_behind(ws["f_norm0"], tok_conv, tok)], [x, y0], [d2, dh1],
                                      [(0, F32), (1, BF16)], name="b_norm_f0")
    tok_mlp0 = reduce("mlp0", "paired", gb, d1)
    gb["w_out"] = mm_tn(cat, d1b, groups=1, name="b_out_w")
    tok = reduce("mixer", "ready", gb, None)
    dcat = mm_nt(d1b, wb["w_out"], name="b_out_x")
    dqkv_c, dpm, dpba, gs["alog"], gs["dtb"], gs["onorm"] = delta_bwd(
        qkv, pm, pba, ws["alog"], ws["dtb"], _behind(ws["onorm"], tok_mlp0, tok), states, tinvs, dcat, name="b_delta")
    tok = reduce("mixer", "paired", gb, dqkv_c)
    dpm, gs["lng"], gs["lnb"], gs["w_s"], gs["bs"] = gmlp_bwd(
        pm, ws["lng"], ws["lnb"], ws["w_s"], ws["bs"], dcat, dpm, name="b_gmlp")
    conv_w = _behind(ws["conv_w"], tok)
    dpm, gs["conv_w"] = conv_bwd(pm, conv_w, dqkv_c, name="b_conv4", into=dpm, through_silu=True)
    gb["w_main_t"] = mm_tn(dpm, h0, groups=1, name="b_proj_main_w")
    gb["w_ba_t"] = mm_tn(dpba, h0, groups=1, name="b_proj_ba_w")
    dh0 = mm_nn(dpm, wb["w_main_t"], name="b_proj_main_x")
    dh0 = mm_nn(dpba, wb["w_ba_t"], name="b_proj_ba_x", add=dh0, out_dtype=BF16)
    grad_x, gs["e_norm"] = rows_bwd(stage_norm, [ws["e_norm"]], [x], [d1, dh0], [(0, F32)], name="b_norm_e")
    return loss, grad_x, gb, gs


WEIGHTS = ["e_norm", "e_w_in", "e_conv_w", "e_a_log", "e_dt_bias", "e_o_norm", "e_ln_g", "e_ln_b", "e_w_s", "e_b_s", "e_w_out",
           "o_norm", "o_pw1", "o_pw1_b", "o_dw", "o_dw_b", "o_ln_g", "o_ln_b", "o_pw2", "o_pw2_b", "f_norm", "f_w1", "f_w2",
           "final_norm"]
BIG = ["e_w_in", "e_w_out", "o_pw1", "o_pw2", "f_w1", "f_w2"]
SMALL_SHARDED = ["e_conv_w", "o_norm", "o_pw1_b", "o_dw", "o_dw_b", "o_ln_g", "o_ln_b", "o_pw2_b"]
SMALL = [n for n in WEIGHTS if n not in BIG]
LANES = 128
PACK_ROWS = 16
IN_ROW_MULT = 256
REDUCE_GROUPS = {"mlp1": ("w2_1", "w1_1"), "conv": ("pw2", "pw1"), "mlp0": ("w2_0", "w1_0"), "mixer": ("w_out",)}


def _pack(arrs):
    flat = jnp.concatenate([a.reshape(-1).astype(F32) for a in arrs])
    n = flat.shape[0]
    rows = -(-n // (LANES * PACK_ROWS)) * PACK_ROWS
    return jnp.pad(flat, (0, rows * LANES - n)).reshape(rows, LANES)


def _unpack(flat, shapes, lead=()):
    outs, off = [], 0
    for s in shapes:
        n = int(np.prod(s))
        outs.append(flat[..., off:off + n].reshape(lead + tuple(s)))
        off += n
    return outs


def kernel(x, e_norm, e_w_in, e_conv_w, e_a_log, e_dt_bias, e_o_norm, e_ln_g, e_ln_b, e_w_s, e_b_s, e_w_out, o_norm, o_pw1, o_pw1_b, o_dw, o_dw_b, o_ln_g, o_ln_b, o_pw2, o_pw2_b, f_norm, f_w1, f_w2, final_norm, loss_target, m_e_norm, m_e_w_in, m_e_conv_w, m_e_a_log, m_e_dt_bias, m_e_o_norm, m_e_ln_g, m_e_ln_b, m_e_w_s, m_e_b_s, m_e_w_out, m_o_norm, m_o_pw1, m_o_pw1_b, m_o_dw, m_o_dw_b, m_o_ln_g, m_o_ln_b, m_o_pw2, m_o_pw2_b, m_f_norm, m_f_w1, m_f_w2, m_final_norm, v_e_norm, v_e_w_in, v_e_conv_w, v_e_a_log, v_e_dt_bias, v_e_o_norm, v_e_ln_g, v_e_ln_b, v_e_w_s, v_e_b_s, v_e_w_out, v_o_norm, v_o_pw1, v_o_pw1_b, v_o_dw, v_o_dw_b, v_o_ln_g, v_o_ln_b, v_o_pw2, v_o_pw2_b, v_f_norm, v_f_w1, v_f_w2, v_final_norm):
    a = dict(locals())
    xi, yi, ci = _me()
    chip = 2 * xi + yi
    who = jnp.stack([ci, chip]).astype(jnp.int32)
    t, d = x.shape[1], x.shape[2]
    nh, ng = e_a_log.shape[1], e_w_s.shape[1]
    n_qkv, n_av, n_bw = 3 * nh * HEAD_DIM, nh * HEAD_DIM, ng * HEAD_DIM
    in_cols = n_qkv + n_av + 2 * nh + 2 * n_bw
    c_ba = n_qkv + n_av

    sh_in = in_cols // N_CHIPS
    pad_in = -(-sh_in // IN_ROW_MULT) * IN_ROW_MULT - sh_in
    w_in_t_local = jnp.pad(e_w_in[0].T.astype(BF16), ((0, pad_in), (0, 0)))
    small_local = [a[n] for n in SMALL_SHARDED]
    first, tok_in = gather_start([[w_in_t_local, _pack(small_local)]], name="gather_start_in")
    travelling, token = gather_start([[(e_w_out[0] + tok_in[0, 0]).astype(BF16), f_w1[0].astype(BF16), f_w2[0].astype(BF16)],
                                      [w.astype(BF16) for w in (o_pw1[0], o_pw2[0], f_w1[1], f_w2[1])]], name="gather_start")
    (h0,) = rows_fwd(stage_norm, [e_norm + token[0, 0]], [x[0]], [(1, BF16, d)], name="f_norm_e")
    g_in, g_small = gather_wait(gather_forward(first[0], h0, name="gather_forward_in"), h0, name="gather_wait_in")

    state = {"layer0": travelling[0], "layer1": travelling[1]}

    def arrive(layer, event, after):
        if event == "landed":
            state[layer] = gather_forward(state[layer], after, name="gather_forward_" + layer)
            return {"token": state[layer]["token"]}
        got = gather_wait(state[layer], after, name="gather_wait_" + layer)
        if layer == "layer0":
            g_out, g_w1, g_w2 = got
            return {"w_out": g_out.reshape(1, -1, d), "w1_0": g_w1, "w2_0": g_w2.reshape(1, -1, d)}
        g_pw1, g_pw2, g_w1, g_w2 = got
        return {"pw1": g_pw1, "pw2": g_pw2.reshape(1, -1, d), "w1_1": g_w1, "w2_1": g_w2.reshape(1, -1, d)}

    def in_rows(lo, hi):
        parts = []
        for s in range(N_CHIPS):
            a0, a1 = max(lo, s * sh_in), min(hi, (s + 1) * sh_in)
            if a0 < a1:
                parts.append(g_in[s, a0 - s * sh_in:a1 - s * sh_in])
        return parts

    wb = {
        "w_main_t": jnp.concatenate(in_rows(0, c_ba) + in_rows(c_ba + 2 * nh, in_cols), axis=0)[None],
        "w_ba_t": jnp.pad(jnp.concatenate(in_rows(c_ba, c_ba + 2 * nh), axis=0), ((0, LANES - 2 * nh), (0, 0)))[None],
    }
    pieces = _unpack(g_small.reshape(N_CHIPS, -1), [w.shape for w in small_local], lead=(N_CHIPS,))
    full = {n: jnp.moveaxis(p, 0, -2).reshape(p.shape[1:-1] + (N_CHIPS * p.shape[-1],)) for n, p in zip(SMALL_SHARDED, pieces)}
    ws = {
        "e_norm": e_norm, "conv_w": full["e_conv_w"][0], "alog": e_a_log.reshape(nh, 1, 1), "dtb": e_dt_bias.reshape(nh, 1, 1),
        "onorm": e_o_norm, "lng": e_ln_g.reshape(ng, 1, HEAD_DIM), "lnb": e_ln_b.reshape(ng, 1, HEAD_DIM), "w_s": e_w_s[0],
        "bs": e_b_s.reshape(ng, HEAD_DIM, 1), "o_norm": full["o_norm"], "b1a": full["o_pw1_b"][:, :d], "b1b": full["o_pw1_b"][:, d:],
        "dw": full["o_dw"][0], "dw_b": full["o_dw_b"], "ln_g": full["o_ln_g"], "ln_b": full["o_ln_b"], "b2": full["o_pw2_b"],
        "f_norm0": f_norm[0:1], "f_norm1": f_norm[1:2], "final_norm": final_norm.reshape(1, d),
    }

    pending = {}

    def reduce(group, event, gb, after):
        names = REDUCE_GROUPS[group]
        if event == "ready":
            arrs = [gb[nme] if gb[nme].shape[0] == N_CHIPS else gb[nme].reshape(N_CHIPS, -1, d) for nme in names]
            pending[group] = split_start(_pair_copies, arrs, [(N_CHIPS, g.shape[1] // 2, g.shape[2]) for g in arrs], len(arrs),
                                         name="reduce_sibling_start_" + group)
            return pending[group]["token"]
        mine, theirs = split_wait(pending[group], after, name="reduce_sibling_wait_" + group)
        pairs = [pre_reduce(gd, sb, who, name="reduce_pair_" + nme) for nme, gd, sb in zip(names, mine, theirs)]
        sums = [p[0] for p in pairs]
        pending[group] = {"own": [p[1] for p in pairs],
                          "scatter": split_start(_scatter_copies, sums, [(3,) + s.shape[1:] for s in sums], 3 * len(sums),
                                                 name="reduce_chips_start_" + group)}
        return pending[group]["scatter"]["token"]

    loss, grad_x, gb, gs = local_step(x[0], loss_target[0], wb, ws, arrive, reduce, h0)

    gm, gba = gb["w_main_t"][0], gb["w_ba_t"][0]
    g_in_t = jnp.concatenate([gm[:c_ba], gba[:2 * nh], gm[c_ba:]], axis=0).reshape(N_CHIPS, sh_in, d)
    g_in_padded = jnp.pad(g_in_t, ((0, 0), (0, pad_in), (0, 0)))
    in_pair = split_start(_pair_copies, [g_in_padded], [(N_CHIPS, (sh_in + pad_in) // 2, d)], 1, name="reduce_sibling_start_w_in")
    small_global = {
        "e_norm": gs["e_norm"], "e_conv_w": gs["conv_w"][None], "e_a_log": gs["alog"].reshape(1, nh), "e_dt_bias": gs["dtb"].reshape(1, nh),
        "e_o_norm": gs["onorm"], "e_ln_g": gs["lng"].reshape(1, n_bw), "e_ln_b": gs["lnb"].reshape(1, n_bw), "e_w_s": gs["w_s"][None],
        "e_b_s": gs["bs"].reshape(1, ng, HEAD_DIM), "o_norm": gs["o_norm"], "o_pw1_b": jnp.concatenate([gs["b1a"], gs["b1b"]], axis=1),
        "o_dw": gs["dw"][None], "o_dw_b": gs["dw_b"], "o_ln_g": gs["ln_g"], "o_ln_b": gs["ln_b"], "o_pw2_b": gs["b2"],
        "f_norm": jnp.concatenate([gs["f_norm0"], gs["f_norm1"]], axis=0), "final_norm": gs["final_norm"].reshape(d),
    }
    packed = _pack([small_global[n] for n in SMALL] + [loss])
    everyone = split_start(_everyone_copies, [packed], [(8,) + packed.shape], 7, name="reduce_small_start")

    who_then = who + (in_pair["token"][0, 0] + everyone["token"][0, 0]).astype(jnp.int32)
    half = {}
    for group, names in REDUCE_GROUPS.items():
        _, landed = split_wait(pending[group]["scatter"], grad_x, name="reduce_chips_wait_" + group)
        for nme, ow, rc in zip(names, pending[group]["own"], landed):
            half[nme] = final_reduce(ow, rc, who_then, name="reduce_sum_" + nme)
    r_out, r_pw1, r_pw2, r_w1_0, r_w1_1, r_w2_0, r_w2_1 = sibling_join_halves(
        [half[nme] for nme in ("w_out", "pw1", "pw2", "w1_0", "w1_1", "w2_0", "w2_1")], name="reduce_join")
    big_grads = {"e_w_out": [r_out], "o_pw1": [r_pw1], "o_pw2": [r_pw2], "f_w1": [r_w1_0, r_w1_1], "f_w2": [r_w2_0, r_w2_1]}

    mine, theirs = split_wait(in_pair, r_out, name="reduce_sibling_wait_w_in")
    in_sum, in_own = pre_reduce(mine[0], theirs[0], who, name="reduce_pair_w_in")
    in_scatter = split_start(_scatter_copies, [in_sum], [(3,) + in_sum.shape[1:]], 3, name="reduce_chips_start_w_in")
    out = {}
    prev = in_scatter["token"]
    for n in BIG:
        if n != "e_w_in":
            out[n] = adamw(a[n], big_grads[n], a["m_" + n], a["v_" + n], name="adamw_" + n, behind=prev)
            prev = out[n][1][0, :SUBLANES, :LANES]

    own_packed, landed = split_wait(everyone, prev, name="reduce_small_wait")
    summed = sum8(landed[0], own_packed[0], (4 * xi + 2 * yi + ci).astype(jnp.int32).reshape(1), name="reduce_small_sum")
    *small_full, total = _unpack(summed.reshape(-1), [small_global[n].shape for n in SMALL] + [()])
    small_grads = {}
    for n, g in zip(SMALL, small_full):
        if n in SMALL_SHARDED:
            width = a[n].shape[-1]
            g = lax.dynamic_slice_in_dim(g, chip * width, width, axis=g.ndim - 1)
        small_grads[n] = g
    sw, sm, sv, sg = (_pack([src[n] for n in SMALL])[None] for src in
                      ({n: a[n] for n in SMALL}, {n: a["m_" + n] for n in SMALL}, {n: a["v_" + n] for n in SMALL}, small_grads))
    res = adamw(sw, [sg[0]], sm, sv, name="adamw_small")

    _, landed = split_wait(in_scatter, res[1], name="reduce_chips_wait_w_in")
    (r_in,) = sibling_join_halves([final_reduce(in_own, landed[0], who, name="reduce_sum_w_in")], name="reduce_join_w_in")
    w_in_res = adamw(e_w_in.transpose(0, 2, 1), [r_in[:sh_in]], m_e_w_in.transpose(0, 2, 1), v_e_w_in.transpose(0, 2, 1),
                     name="adamw_e_w_in")
    out["e_w_in"] = tuple(r.transpose(0, 2, 1) for r in w_in_res)
    shapes = [a[n].shape for n in SMALL]
    unpacked = [_unpack(r.reshape(-1), shapes) for r in res]
    for i, n in enumerate(SMALL):
        out[n] = tuple(u[i] for u in unpacked)

    result = [total, grad_x[None]]
    for k in range(4):
        result += [out[n][k] for n in WEIGHTS]
    return tuple(result)
```

```python
import functools
import math

import jax
import jax.numpy as jnp
import numpy as np
from jax import lax
from jax.experimental import pallas as pl
from jax.experimental.pallas import tpu as pltpu

F32 = jnp.float32
BF16 = jnp.bfloat16

EPS = 1e-6
CHUNK = 64
PAIR = 2 * CHUNK
HEAD_DIM = 128
N_CHIPS = 4
ADAM_LR, ADAM_B1, ADAM_B2, ADAM_EPS, ADAM_WD, ADAM_STEP = 0.001, 0.9, 0.999, 1e-08, 0.01, 10

VMEM_LIMIT = 56 * 1024 * 1024


def _cparams(sem=None):
    return pltpu.CompilerParams(dimension_semantics=sem, vmem_limit_bytes=VMEM_LIMIT)


def _pick(n, prefs):
    for p in prefs:
        if n % p == 0:
            return p
    return n


def mm_nn(a, w, *, name, epilogue=None, add=None, out_dtype=F32, behind=None):
    m, k = a.shape
    s, _, ns = w.shape
    n = s * ns
    tm = _pick(m, (1024, 512, 256, 128))
    tn = _pick(ns, (1024, 512, 256, 128))
    tk = _pick(k, (2048, 1024, 512, 256, 128))
    nk = k // tk
    npb = ns // tn
    assert add is None or epilogue is None

    def body(a_ref, w_ref, *rest):
        if add is not None:
            add_ref, rest = rest[0], rest[1:]
        if behind is not None:
            rest = rest[1:]
        if epilogue == "relu2":
            o1_ref, o2_ref = rest[0], rest[1]
            acc_ref = rest[2] if nk > 1 else None
        else:
            o1_ref = rest[0]
            acc_ref = rest[1] if nk > 1 else None
        def finish(c, rows=slice(None)):
            if epilogue == "relu2":
                r = jnp.maximum(c, 0.0)
                o1_ref[rows, :] = r.astype(o1_ref.dtype)
                o2_ref[rows, :] = (r * r).astype(o2_ref.dtype)
            elif add is not None:
                o1_ref[rows, :] = (c + add_ref[rows, :].astype(F32)).astype(o1_ref.dtype)
            else:
                o1_ref[rows, :] = c.astype(o1_ref.dtype)

        part = jnp.dot(a_ref[...], w_ref[...], preferred_element_type=F32)
        if nk == 1:
            finish(part)
        else:
            kk = pl.program_id(2)

            @pl.when(kk == 0)
            def _():
                acc_ref[...] = part

            @pl.when(kk > 0)
            def _():
                acc_ref[...] += part

            @pl.when(kk == nk - 1)
            def _():
                finish(acc_ref[...])

    o_spec = pl.BlockSpec((tm, tn), lambda i, j, kk: (i, j))
    if epilogue == "relu2":
        out_shape = (jax.ShapeDtypeStruct((m, n), BF16), jax.ShapeDtypeStruct((m, n), BF16))
        out_specs = (o_spec, o_spec)
    else:
        out_shape = jax.ShapeDtypeStruct((m, n), out_dtype)
        out_specs = o_spec
    return pl.pallas_call(
        body, name=name, out_shape=out_shape,
        grid=(m // tm, n // tn, nk),
        in_specs=[pl.BlockSpec((tm, tk), lambda i, j, kk: (i, kk)),
                  pl.BlockSpec((None, tk, tn), lambda i, j, kk: (j // npb, kk, j % npb))] + ([o_spec] if add is not None else [])
        + ([pl.BlockSpec(behind.shape, lambda i, j, kk: (0, 0))] if behind is not None else []),
        out_specs=out_specs,
        scratch_shapes=[pltpu.VMEM((tm, tn), F32)] if nk > 1 else [],
        compiler_params=_cparams(("parallel", "parallel", "arbitrary")),
    )(*([a, w] + ([add] if add is not None else []) + ([behind] if behind is not None else [])))


def mm_nt(a, w, *, name, mul=None, add=None, out_dtype=F32):
    assert mul is None or add is None
    if add is not None:
        mul = add
    m, n = a.shape
    s, k, ns = w.shape
    assert n == s * ns
    tm = _pick(m, (1024, 512, 256, 128))
    tko = _pick(k, (1024, 512, 256, 128))
    tn = _pick(ns, (2048, 1024, 512, 256, 128))
    nn = n // tn
    npb = ns // tn

    def body(a_ref, w_ref, *rest):
        if mul is not None:
            m_ref, o_ref = rest[0], rest[1]
            acc_ref = rest[2] if nn > 1 else None
        else:
            m_ref, o_ref = None, rest[0]
            acc_ref = rest[1] if nn > 1 else None
        part = lax.dot_general(a_ref[...], w_ref[...], (((1,), (1,)), ((), ())), preferred_element_type=F32)

        def finish(c):
            if add is not None:
                c = c + m_ref[...].astype(F32)
            elif m_ref is not None:
                c = c * (2.0 * m_ref[...].astype(F32))
            o_ref[...] = c.astype(o_ref.dtype)

        if nn == 1:
            finish(part)
        else:
            kk = pl.program_id(2)

            @pl.when(kk == 0)
            def _():
                acc_ref[...] = part

            @pl.when(kk > 0)
            def _():
                acc_ref[...] += part

            @pl.when(kk == nn - 1)
            def _():
                finish(acc_ref[...])

    in_specs = [pl.BlockSpec((tm, tn), lambda i, j, kk: (i, kk)),
                pl.BlockSpec((None, tko, tn), lambda i, j, kk: (kk // npb, j, kk % npb))]
    args = [a, w]
    if mul is not None:
        in_specs.append(pl.BlockSpec((tm, tko), lambda i, j, kk: (i, j)))
        args.append(mul)
    return pl.pallas_call(
        body, name=name, out_shape=jax.ShapeDtypeStruct((m, k), out_dtype),
        grid=(m // tm, k // tko, nn),
        in_specs=in_specs,
        out_specs=pl.BlockSpec((tm, tko), lambda i, j, kk: (i, j)),
        scratch_shapes=[pltpu.VMEM((tm, tko), F32)] if nn > 1 else [],
        compiler_params=_cparams(("parallel", "parallel", "arbitrary")),
    )(*args)


def mm_tn(a, b, *, groups, name):
    m, k = a.shape
    _, n = b.shape
    ns = n // groups
    tko = _pick(k, (1024, 512, 256, 128))
    tn = _pick(ns, (1024, 512, 256, 128))
    tc = _pick(m, (2048, 1024, 512, 256, 128))
    nc = m // tc
    npb = ns // tn

    def body(a_ref, b_ref, o_ref):
        part = lax.dot_general(a_ref[...], b_ref[...], (((0,), (0,)), ((), ())), preferred_element_type=F32)
        kk = pl.program_id(2)

        @pl.when(kk == 0)
        def _():
            o_ref[...] = part

        @pl.when(kk > 0)
        def _():
            o_ref[...] += part

    return pl.pallas_call(
        body, name=name, out_shape=jax.ShapeDtypeStruct((groups, k, ns), F32),
        grid=(k // tko, n // tn, nc),
        in_specs=[pl.BlockSpec((tc, tko), lambda i, j, kk: (kk, i)),
                  pl.BlockSpec((tc, tn), lambda i, j, kk: (kk, j))],
        out_specs=pl.BlockSpec((None, tko, tn), lambda i, j, kk: (j // npb, i, j % npb)),
        compiler_params=_cparams(("parallel", "parallel", "arbitrary")),
    )(a, b)


ROWS = 128


def _full_spec(arr):
    nd = arr.ndim
    return pl.BlockSpec(arr.shape, lambda i, _nd=nd: (0,) * _nd)


def _row_spec(x, rows):
    if isinstance(x, tuple):
        _, w, cb = x
        return pl.BlockSpec((rows, w), lambda i, _cb=cb: (i, _cb))
    return pl.BlockSpec((rows, x.shape[1]), lambda i: (i, 0))


def _arr(x):
    return x[0] if isinstance(x, tuple) else x


def _width(x):
    return x[1] if isinstance(x, tuple) else x.shape[1]


def rows_fwd(fn, params, xs, stores, *, name, rows=2 * ROWS):
    t = _arr(xs[0]).shape[0]
    np_, nx = len(params), len(xs)

    def body(*refs):
        p_refs, x_refs, o_refs = refs[:np_], refs[np_:np_ + nx], refs[np_ + nx:]
        outs = fn(*[r[...].astype(F32) for r in p_refs], *[r[...].astype(F32) for r in x_refs])
        for (idx, dt, _), o_ref in zip(stores, o_refs):
            o_ref[...] = outs[idx].astype(dt)

    res = pl.pallas_call(
        body, name=name,
        out_shape=tuple(jax.ShapeDtypeStruct((t, w), dt) for _, dt, w in stores),
        grid=(t // rows,),
        in_specs=[_full_spec(p) for p in params] + [_row_spec(x, rows) for x in xs],
        out_specs=tuple(pl.BlockSpec((rows, w), lambda i: (i, 0)) for _, _, w in stores),
        compiler_params=_cparams(("parallel",)),
    )(*params, *[_arr(x) for x in xs])
    return res


def rows_bwd(fn, params, xs, cts, dx_stores, *, name, rows=ROWS, concat=False):
    t = _arr(xs[0]).shape[0]
    np_, nx = len(params), len(xs)
    ct_idx = [i for i, c in enumerate(cts) if c is not None]
    ct_arrs = [cts[i] for i in ct_idx]
    nct = len(ct_arrs)
    nds = 1 if concat else len(dx_stores)
    widths = [_width(xs[xi]) for xi, _ in dx_stores]

    def body(*refs):
        p_refs = refs[:np_]
        x_refs = refs[np_:np_ + nx]
        c_refs = refs[np_ + nx:np_ + nx + nct]
        d_refs = refs[np_ + nx + nct:np_ + nx + nct + nds]
        g_refs = refs[np_ + nx + nct + nds:]
        pv = [r[...].astype(F32) for r in p_refs]
        xv = [r[...].astype(F32) for r in x_refs]
        outs, vjp = jax.vjp(lambda *a: tuple(fn(*a)), *pv, *xv)
        ct_full = [jnp.zeros_like(o) for o in outs]
        for i, r in zip(ct_idx, c_refs):
            ct_full[i] = r[...].astype(F32)
        grads = vjp(tuple(ct_full))
        if concat:
            off = 0
            for (xi, dt), wd in zip(dx_stores, widths):
                d_refs[0][:, off:off + wd] = grads[np_ + xi].astype(dt)
                off += wd
        else:
            for (xi, dt), d_ref in zip(dx_stores, d_refs):
                d_ref[...] = grads[np_ + xi].astype(dt)
        step = pl.program_id(0)
        for j, g_ref in enumerate(g_refs):
            @pl.when(step == 0)
            def _(g_ref=g_ref, j=j):
                g_ref[...] = grads[j]

            @pl.when(step > 0)
            def _(g_ref=g_ref, j=j):
                g_ref[...] += grads[j]

    dx_shapes = [(sum(widths), dx_stores[0][1])] if concat else [(wd, dt) for wd, (_, dt) in zip(widths, dx_stores)]
    out_shape = tuple(jax.ShapeDtypeStruct((t, wd), dt) for wd, dt in dx_shapes) + \
        tuple(jax.ShapeDtypeStruct(p.shape, F32) for p in params)
    out_specs = tuple(pl.BlockSpec((rows, wd), lambda i: (i, 0)) for wd, _ in dx_shapes) + \
        tuple(_full_spec(p) for p in params)
    return pl.pallas_call(
        body, name=name, out_shape=out_shape, grid=(t // rows,),
        in_specs=[_full_spec(p) for p in params] + [_row_spec(x, rows) for x in xs] + [_row_spec(c, rows) for c in ct_arrs],
        out_specs=out_specs,
        compiler_params=_cparams(("arbitrary",)),
    )(*params, *[_arr(x) for x in xs], *[_arr(c) for c in ct_arrs])


def _rms(x, g):
    return x * lax.rsqrt(jnp.mean(x * x, axis=-1, keepdims=True) + EPS) * g


def _sigmoid(x):
    return 1.0 / (1.0 + jnp.exp(-x))


def _silu(x):
    return x * _sigmoid(x)


def _gelu(x):
    return 0.5 * x * (1.0 + lax.erf(x * (1.0 / math.sqrt(2.0))))


def stage_norm(g, x):
    return x, _rms(x, g)


def stage_res_norm(g, x, y):
    xn = x + y
    return xn, _rms(xn, g)


def stage_res_bias_norm(b, g, x, y):
    xn = x + y + b
    return xn, _rms(xn, g)


def stage_glu(ba, bb, za, zb):
    return ((za + ba) * _sigmoid(zb + bb),)


def stage_ln_silu(dw_b, ln_g, ln_b, cv):
    z = cv + dw_b
    mu = jnp.mean(z, axis=-1, keepdims=True)
    zc = z - mu
    y = zc * lax.rsqrt(jnp.mean(zc * zc, axis=-1, keepdims=True) + EPS) * ln_g + ln_b
    return (_silu(y),)


CONV_ROWS = 128
CONV_COLS = 256
SUBLANES = 8


def _halo(k):
    return SUBLANES * ((k - 1 + SUBLANES - 1) // SUBLANES)


def _taps_by_roll(k):
    out = {}
    for s in range(k):
        out.setdefault(s % SUBLANES, []).append((s // SUBLANES, s))
    return out


def _shifted_down(win, k, r):
    halo, n = _halo(k), win.shape[0]
    segs = {}
    for b, lst in _taps_by_roll(k).items():
        rolled = win if b == 0 else pltpu.roll(win, b, axis=0)
        for a, s in lst:
            segs[s] = rolled[halo - SUBLANES * a: halo - SUBLANES * a + r]
    return segs


def _shifted_up(win, k, r):
    n = win.shape[0]
    segs = {}
    for b, lst in _taps_by_roll(k).items():
        rolled = win if b == 0 else pltpu.roll(win, n - b, axis=0)
        for a, s in lst:
            segs[s] = rolled[SUBLANES * a: SUBLANES * a + r]
    return segs


def _for_blocks(nblk, fn):
    fn(0, True, nblk == 1)
    if nblk > 2:
        def step(i, c):
            fn(i, False, False)
            return c
        lax.fori_loop(1, nblk - 1, step, 0)
    if nblk > 1:
        fn(nblk - 1, False, True)


def _base(i, r):
    return i * r if isinstance(i, int) else pl.multiple_of(i * r, r)


def _win_top(ref, i, first, r, halo):
    if first:
        return jnp.concatenate([jnp.zeros((halo, ref.shape[1]), F32), ref[pl.ds(0, r), :]], axis=0)
    base = _base(i, r)
    return ref[pl.ds(base - halo, r + halo), :]


def _win_bottom(ref, i, last, r, halo):
    base = _base(i, r)
    if last:
        return jnp.concatenate([ref[pl.ds(base, r), :], jnp.zeros((halo, ref.shape[1]), F32)], axis=0)
    return ref[pl.ds(base, r + halo), :]


def conv_fwd(x, w, *, mode, name, g=None):
    t = x.shape[0]
    k, c = w.shape
    r, cb, halo = min(CONV_ROWS, t), min(CONV_COLS, c), _halo(k)
    nblk = t // r

    def body(*refs):
        if mode == "silu_bwd":
            x_ref, w_ref, g_ref, o_ref = refs
        else:
            x_ref, w_ref, o_ref = refs

        def blk(i, first, last):
            segs = _shifted_down(_win_top(x_ref, i, first, r, halo), k, r)
            acc = None
            for s in range(k):
                term = w_ref[pl.ds(k - 1 - s, 1), :] * segs[s]
                acc = term if acc is None else acc + term
            base = _base(i, r)
            if mode == "silu":
                acc = _silu(acc)
            elif mode == "silu_bwd":
                sg = _sigmoid(acc)
                acc = g_ref[pl.ds(base, r), :] * (sg * (1.0 + acc * (1.0 - sg)))
            o_ref[pl.ds(base, r), :] = acc

        _for_blocks(nblk, blk)

    col = pl.BlockSpec((t, cb), lambda j: (0, j))
    in_specs = [col, pl.BlockSpec((k, cb), lambda j: (0, j))] + ([col] if mode == "silu_bwd" else [])
    args = [x, w] + ([g] if mode == "silu_bwd" else [])
    return pl.pallas_call(
        body, name=name, out_shape=jax.ShapeDtypeStruct((t, c), F32), grid=(c // cb,),
        in_specs=in_specs, out_specs=col, compiler_params=_cparams(("parallel",)),
    )(*args)


def conv_bwd(x, w, dy, *, name, into=None):
    t = x.shape[0]
    k, c = w.shape
    r, cb, halo = min(CONV_ROWS, t), min(CONV_COLS, c), _halo(k)
    nblk = t // r

    def body(x_ref, w_ref, dy_ref, *rest):
        dx_ref, dw_ref = rest[-2], rest[-1]
        dw_ref[...] = jnp.zeros_like(dw_ref)

        def blk(i, first, last):
            base = _base(i, r)
            up = _shifted_up(_win_bottom(dy_ref, i, last, r, halo), k, r)
            down = _shifted_down(_win_top(x_ref, i, first, r, halo), k, r)
            dyb = up[0]
            acc = None
            for s in range(k):
                term = w_ref[pl.ds(k - 1 - s, 1), :] * up[s]
                acc = term if acc is None else acc + term
                dw_ref[pl.ds(k - 1 - s, 1), :] += jnp.sum(down[s] * dyb, axis=0, keepdims=True)
            dx_ref[pl.ds(base, r), :] = acc.astype(dx_ref.dtype)

        _for_blocks(nblk, blk)

    col = pl.BlockSpec((t, cb), lambda j: (0, j))
    wsp = pl.BlockSpec((k, cb), lambda j: (0, j))
    dx_shape = jax.ShapeDtypeStruct((t, c), F32) if into is None else jax.ShapeDtypeStruct(into.shape, into.dtype)
    return pl.pallas_call(
        body, name=name,
        out_shape=(dx_shape, jax.ShapeDtypeStruct((k, c), F32)), grid=(c // cb,),
        in_specs=[col, wsp, col] + ([] if into is None else [pl.BlockSpec(memory_space=pl.ANY)]),
        out_specs=(col, wsp),
        input_output_aliases={} if into is None else {3: 0},
        compiler_params=_cparams(("parallel",)),
    )(*([x, w, dy] + ([] if into is None else [into])))


_DIMS = {"nn": (((1,), (0,)), ((), ())), "nt": (((1,), (1,)), ((), ())), "tn": (((0,), (0,)), ((), ()))}
_DIMS_BATCHED = {"nn": (((2,), (1,)), ((0,), (0,))), "nt": (((2,), (2,)), ((0,), (0,))), "tn": (((1,), (1,)), ((0,), (0,)))}


def _mxu(a, b, mode):
    dims = _DIMS_BATCHED if a.ndim == 3 else _DIMS
    return lax.dot_general(a, b, dims[mode], preferred_element_type=F32)


def _split(x):
    hi = x.astype(BF16)
    return hi, (x - hi.astype(F32)).astype(BF16)


def _dot_raw(a, b, mode, prec):
    if prec == "bf16":
        return _mxu(a.astype(BF16), b.astype(BF16), mode)
    if prec == "x3":
        ah, al = _split(a)
        bh, bl = _split(b)
        return _mxu(ah, bh, mode) + (_mxu(ah, bl, mode) + _mxu(al, bh, mode))
    if prec == "x3r":
        bh, bm = _split(b)
        bl = (b - bh.astype(F32) - bm.astype(F32)).astype(BF16)
        ah = a.astype(BF16)
        return _mxu(ah, bh, mode) + (_mxu(ah, bm, mode) + _mxu(ah, bl, mode))
    raise ValueError(prec)


@functools.lru_cache(maxsize=None)
def _dot_fn(mode, prec):
    bprec = "x3" if prec == "x3r" else prec

    @jax.custom_vjp
    def f(a, b):
        return _dot_raw(a, b, mode, prec)

    def fwd(a, b):
        return _dot_raw(a, b, mode, prec), (a, b)

    def bwd(res, ct):
        a, b = res
        if mode == "nn":
            return _dot_raw(ct, b, "nt", bprec), _dot_raw(a, ct, "tn", bprec)
        if mode == "nt":
            return _dot_raw(ct, b, "nn", bprec), _dot_raw(ct, a, "tn", bprec)
        return _dot_raw(b, ct, "nt", bprec), _dot_raw(a, ct, "nn", bprec)

    f.defvjp(fwd, bwd)
    return f


def _dot(a, b, mode="nn", prec="bf16"):
    return _dot_fn(mode, prec)(a, b)


def _inv_product(l):
    n = l.shape[-1]
    eye = (lax.broadcasted_iota(jnp.int32, (n, n), 0) == lax.broadcasted_iota(jnp.int32, (n, n), 1)).astype(F32)
    p = eye - l
    pw = l
    for _ in range(5):
        pw = _dot_raw(pw, pw, "nn", "x3")
        p = _dot_raw(p, eye + pw, "nn", "x3")
    return p


@jax.custom_vjp
def _inv_unit_lower(l, t_saved):
    return t_saved


def _inv_fwd(l, t_saved):
    return t_saved, t_saved


def _inv_bwd(t, ct):
    tmp = _dot_raw(t, ct, "tn", "x3")
    return -_dot_raw(tmp, t, "nt", "x3"), jnp.zeros_like(t)


_inv_unit_lower.defvjp(_inv_fwd, _inv_bwd)


def _softplus(x):
    pos = x > 0
    return jnp.where(pos, x, 0.0) + jnp.log(1.0 + jnp.exp(jnp.where(pos, -x, x)))


def _l2n(x):
    return x * lax.rsqrt(jnp.sum(x * x, axis=-1, keepdims=True) + EPS)


def delta_pair(s0, qc, kc, vc, z, pba, alog, dtb, onorm, t_saved=None):
    n = PAIR
    nh = qc.shape[0]
    assert qc.shape == (nh, n, HEAD_DIM) and n == HEAD_DIM
    hi = lax.broadcasted_iota(jnp.int32, (nh, 1, pba.shape[1]), 0)
    li = lax.broadcasted_iota(jnp.int32, (nh, 1, pba.shape[1]), 2)
    braw = jnp.sum(pba[None] * (li == hi).astype(F32), axis=2, keepdims=True)
    araw = jnp.sum(pba[None] * (li == hi + nh).astype(F32), axis=2, keepdims=True)
    ri = lax.broadcasted_iota(jnp.int32, (n, n), 0)
    ci = lax.broadcasted_iota(jnp.int32, (n, n), 1)
    same = (ri // CHUNK) == (ci // CHUNK)
    tri = same & (ci <= ri)
    same_f = jnp.broadcast_to(same.astype(F32), (nh, n, n))
    tri_f = jnp.broadcast_to(tri.astype(F32), (nh, n, n))
    strict_f = (same & (ci < ri)).astype(F32)
    m0 = (lax.broadcasted_iota(jnp.int32, (n, 1), 0) < CHUNK).astype(F32)
    m1 = 1.0 - m0

    q = _l2n(qc) * (HEAD_DIM ** -0.5)
    k = _l2n(kc)
    beta = _sigmoid(braw)
    g = -jnp.exp(alog) * _softplus(araw + dtb)
    gb = jnp.broadcast_to(g, (nh, n, n))
    gc = _dot(tri_f, gb, "nn", "x3r")
    gtot = _dot(same_f, gb, "nn", "x3r")
    decay = jnp.exp(jnp.where(tri, gc - jnp.swapaxes(gc, 1, 2), -1e30))
    eg = jnp.exp(gc)
    kb, vb = k * beta, vc * beta
    l = _dot(kb, k, "nt") * decay * strict_f
    if t_saved is None:
        tinv = _inv_product(l)
    else:
        tinv = _inv_unit_lower(l, t_saved)
    u = _dot(tinv, vb, "nn", "x3")
    w = _dot(tinv, kb * eg, "nn", "x3")
    attn = _dot(q, k, "nt") * decay
    q_dec = q * eg
    k_tail = k * jnp.exp(gtot - gc)
    gl0 = jnp.exp(jnp.sum(gb * m0, axis=1, keepdims=True))
    gl1 = jnp.exp(jnp.sum(gb * m1, axis=1, keepdims=True))

    vn0 = m0 * (u - _dot(w, s0))
    s1 = s0 * gl0 + _dot(k_tail, vn0, "tn")
    vn1 = m1 * (u - _dot(w, s1))
    o = m0 * _dot(q_dec, s0) + m1 * _dot(q_dec, s1) + _dot(attn, vn0 + vn1)
    s2 = s1 * gl1 + _dot(k_tail, vn1, "tn")

    on = o * lax.rsqrt(jnp.mean(o * o, axis=-1, keepdims=True) + EPS) * onorm
    return on * _silu(z), s2, tinv


def _hcols(i):
    return slice(i * HEAD_DIM, (i + 1) * HEAD_DIM)


def _heads(ref, first, count):
    return jnp.stack([ref[:, _hcols(first + i)] for i in range(count)])


def delta_fwd(qkv, pm, pba, alog, dtb, onorm, cat_width, *, name):
    t = qkv.shape[0]
    h = alog.shape[0]
    hd = h * HEAD_DIM
    npair = t // PAIR
    mat = pl.BlockSpec((h, None, PAIR, HEAD_DIM), lambda p: (0, p, 0, 0))
    par = pl.BlockSpec((h, 1, 1), lambda p: (0, 0, 0))

    def body(qkv_ref, z_ref, pba_ref, al_ref, dt_ref, on_ref, o_ref, st_ref, ti_ref, s_scr):
        @pl.when(pl.program_id(0) == 0)
        def _():
            s_scr[...] = jnp.zeros_like(s_scr)

        s0 = s_scr[...]
        st_ref[...] = s0
        out, s2, tinv = delta_pair(s0, _heads(qkv_ref, 0, h), _heads(qkv_ref, h, h), _heads(qkv_ref, 2 * h, h), _heads(z_ref, 0, h),
                                   pba_ref[...], al_ref[...], dt_ref[...], on_ref[...])
        for i in range(h):
            o_ref[:, _hcols(i)] = out[i].astype(o_ref.dtype)
        ti_ref[...] = tinv
        s_scr[...] = s2

    return pl.pallas_call(
        body, name=name, grid=(npair,),
        out_shape=(jax.ShapeDtypeStruct((t, cat_width), BF16),
                   jax.ShapeDtypeStruct((h, npair, PAIR, HEAD_DIM), F32),
                   jax.ShapeDtypeStruct((h, npair, PAIR, PAIR), F32)),
        in_specs=[pl.BlockSpec((PAIR, 3 * hd), lambda p: (p, 0)), pl.BlockSpec((PAIR, hd), lambda p: (p, 3)),
                  pl.BlockSpec((PAIR, pba.shape[1]), lambda p: (p, 0)), par, par,
                  pl.BlockSpec((1, HEAD_DIM), lambda p: (0, 0))],
        out_specs=(pl.BlockSpec((PAIR, hd), lambda p: (p, 0)), mat, mat),
        scratch_shapes=[pltpu.VMEM((h, HEAD_DIM, HEAD_DIM), F32)],
        compiler_params=_cparams(("arbitrary",)),
    )(qkv, pm, pba, alog, dtb, onorm)


def delta_bwd(qkv, pm, pba, alog, dtb, onorm, states, tinvs, dcat, *, name):
    t = qkv.shape[0]
    h = alog.shape[0]
    hd = h * HEAD_DIM
    npair = t // PAIR
    rev = lambda p: npair - 1 - p
    mat = pl.BlockSpec((h, None, PAIR, HEAD_DIM), lambda p: (0, rev(p), 0, 0))
    par = pl.BlockSpec((h, 1, 1), lambda p: (0, 0, 0))
    onsp = pl.BlockSpec((1, HEAD_DIM), lambda p: (0, 0))
    wide = pl.BlockSpec((PAIR, 3 * hd), lambda p: (rev(p), 0))
    zsp = pl.BlockSpec((PAIR, hd), lambda p: (rev(p), 3))
    bsp = pl.BlockSpec((PAIR, pba.shape[1]), lambda p: (rev(p), 0))

    def body(qkv_ref, z_ref, pba_ref, al_ref, dt_ref, on_ref, st_ref, ti_ref, dc_ref,
             dqkv_ref, dz_ref, dpba_ref, dal_ref, ddt_ref, don_ref, ds_scr):
        @pl.when(pl.program_id(0) == 0)
        def _():
            ds_scr[...] = jnp.zeros_like(ds_scr)
            dal_ref[...] = jnp.zeros_like(dal_ref)
            ddt_ref[...] = jnp.zeros_like(ddt_ref)
            don_ref[...] = jnp.zeros_like(don_ref)

        tsv = ti_ref[...]
        fn = lambda s0, qc, kc, vc, z, pb, al, dt, on: delta_pair(s0, qc, kc, vc, z, pb, al, dt, on, tsv)[:2]
        _, vjp = jax.vjp(fn, st_ref[...], _heads(qkv_ref, 0, h), _heads(qkv_ref, h, h), _heads(qkv_ref, 2 * h, h),
                         _heads(z_ref, 0, h), pba_ref[...], al_ref[...], dt_ref[...], on_ref[...])
        ds0, dq, dk, dv, dz, dpba, dal, ddt, don = vjp((_heads(dc_ref, 0, h), ds_scr[...]))
        ds_scr[...] = ds0
        for i in range(h):
            dqkv_ref[:, _hcols(i)] = dq[i]
            dqkv_ref[:, _hcols(h + i)] = dk[i]
            dqkv_ref[:, _hcols(2 * h + i)] = dv[i]
            dz_ref[:, _hcols(i)] = dz[i].astype(dz_ref.dtype)
        dal_ref[...] += dal
        ddt_ref[...] += ddt
        dpba_ref[...] = dpba.astype(dpba_ref.dtype)
        don_ref[...] += don

    return pl.pallas_call(
        body, name=name, grid=(npair,),
        out_shape=(jax.ShapeDtypeStruct((t, 3 * hd), F32), jax.ShapeDtypeStruct(pm.shape, BF16),
                   jax.ShapeDtypeStruct(pba.shape, BF16),
                   jax.ShapeDtypeStruct((h, 1, 1), F32), jax.ShapeDtypeStruct((h, 1, 1), F32),
                   jax.ShapeDtypeStruct((1, HEAD_DIM), F32)),
        in_specs=[wide, zsp, bsp, par, par, onsp, mat, mat, pl.BlockSpec((PAIR, hd), lambda p: (rev(p), 0))],
        out_specs=(wide, zsp, bsp, par, par, onsp),
        scratch_shapes=[pltpu.VMEM((h, HEAD_DIM, HEAD_DIM), F32)],
        compiler_params=_cparams(("arbitrary",)),
    )(qkv, pm, pba, alog, dtb, onorm, states, tinvs, dcat)


def gmlp_block(ln_g, ln_b, w, bcol, u_raw, v_raw):
    n = w.shape[-1]
    ri = lax.broadcasted_iota(jnp.int32, (n, n), 0)
    ci = lax.broadcasted_iota(jnp.int32, (n, n), 1)
    mask = ((ci // CHUNK) <= (ri // CHUNK)).astype(F32)
    vg = _gelu(v_raw)
    vc = vg - jnp.mean(vg, axis=-1, keepdims=True)
    vgn = vc * lax.rsqrt(jnp.mean(vc * vc, axis=-1, keepdims=True) + EPS) * ln_g + ln_b
    return _gelu(u_raw) * (_dot(w * mask, vgn) + bcol)


def gmlp_fwd(pm, ln_g, ln_b, w_s, bcol, cat, *, name):
    t = pm.shape[0]
    g = w_s.shape[0]
    gw = g * HEAD_DIM
    assert pm.shape[1] == 6 * gw and cat.shape[1] == 2 * gw

    def body(u_ref, v_ref, lg_ref, lb_ref, w_ref, b_ref, cat_in, o_ref):
        del cat_in
        out = gmlp_block(lg_ref[...], lb_ref[...], w_ref[...], b_ref[...], _heads(u_ref, 0, g), _heads(v_ref, 0, g))
        for i in range(g):
            o_ref[:, _hcols(i)] = out[i].astype(o_ref.dtype)

    full = lambda a: pl.BlockSpec(a.shape, lambda m: (0, 0, 0))
    return pl.pallas_call(
        body, name=name, grid=(t // HEAD_DIM,),
        out_shape=jax.ShapeDtypeStruct(cat.shape, cat.dtype),
        in_specs=[pl.BlockSpec((HEAD_DIM, gw), lambda m: (m, 4)), pl.BlockSpec((HEAD_DIM, gw), lambda m: (m, 5)),
                  full(ln_g), full(ln_b), full(w_s), full(bcol), pl.BlockSpec(memory_space=pl.ANY)],
        out_specs=pl.BlockSpec((HEAD_DIM, gw), lambda m: (m, 1)),
        input_output_aliases={6: 0},
        compiler_params=_cparams(("arbitrary",)),
    )(pm, pm, ln_g, ln_b, w_s, bcol, cat)


def gmlp_bwd(pm, ln_g, ln_b, w_s, bcol, dcat, dpm, *, name):
    t = pm.shape[0]
    g = w_s.shape[0]
    gw = g * HEAD_DIM
    assert pm.shape[1] == 6 * gw and dpm.shape == pm.shape

    def body(u_ref, v_ref, lg_ref, lb_ref, w_ref, b_ref, dc_ref, dpm_in, duv_ref, dlg_ref, dlb_ref, dw_ref, db_ref):
        del dpm_in
        first = pl.program_id(0) == 0
        _, vjp = jax.vjp(gmlp_block, lg_ref[...], lb_ref[...], w_ref[...], b_ref[...], _heads(u_ref, 0, g), _heads(v_ref, 0, g))
        dlg, dlb, dw, db, du, dv = vjp(_heads(dc_ref, 0, g))
        for i in range(g):
            duv_ref[:, _hcols(i)] = du[i].astype(duv_ref.dtype)
            duv_ref[:, _hcols(g + i)] = dv[i].astype(duv_ref.dtype)
        for ref, val in ((dlg_ref, dlg), (dlb_ref, dlb), (dw_ref, dw), (db_ref, db)):
            @pl.when(first)
            def _(ref=ref, val=val):
                ref[...] = val

            @pl.when(jnp.logical_not(first))
            def _(ref=ref, val=val):
                ref[...] += val

    full = lambda a: pl.BlockSpec(a.shape, lambda m: (0, 0, 0))
    return pl.pallas_call(
        body, name=name, grid=(t // HEAD_DIM,),
        out_shape=(jax.ShapeDtypeStruct(dpm.shape, dpm.dtype),
                   jax.ShapeDtypeStruct(ln_g.shape, F32), jax.ShapeDtypeStruct(ln_b.shape, F32),
                   jax.ShapeDtypeStruct(w_s.shape, F32), jax.ShapeDtypeStruct(bcol.shape, F32)),
        in_specs=[pl.BlockSpec((HEAD_DIM, gw), lambda m: (m, 4)), pl.BlockSpec((HEAD_DIM, gw), lambda m: (m, 5)),
                  full(ln_g), full(ln_b), full(w_s), full(bcol), pl.BlockSpec((HEAD_DIM, gw), lambda m: (m, 1)),
                  pl.BlockSpec(memory_space=pl.ANY)],
        out_specs=(pl.BlockSpec((HEAD_DIM, 2 * gw), lambda m: (m, 2)), full(ln_g), full(ln_b), full(w_s), full(bcol)),
        input_output_aliases={7: 0},
        compiler_params=_cparams(("arbitrary",)),
    )(pm, pm, ln_g, ln_b, w_s, bcol, dcat, dpm)


def loss_head(g, x, r, tgt, *, name, rows=ROWS):
    t, d = x.shape

    def body(g_ref, x_ref, r_ref, t_ref, l_ref, dx_ref, dxb_ref, dg_ref):
        y, vjp = jax.vjp(lambda gg, xx: _rms(xx, gg), g_ref[...], x_ref[...] + r_ref[...])
        e = y - t_ref[...]
        part = (0.5 / d) * jnp.sum(jnp.sum(e * e, axis=1, keepdims=True), axis=0, keepdims=True)
        dg, dx = vjp(e * (1.0 / d))
        dx_ref[...] = dx
        dxb_ref[...] = dx.astype(BF16)
        first = pl.program_id(0) == 0

        @pl.when(first)
        def _():
            l_ref[...] = part
            dg_ref[...] = dg

        @pl.when(jnp.logical_not(first))
        def _():
            l_ref[...] += part
            dg_ref[...] += dg

    rs = pl.BlockSpec((rows, d), lambda i: (i, 0))
    gs = pl.BlockSpec((1, d), lambda i: (0, 0))
    return pl.pallas_call(
        body, name=name, grid=(t // rows,),
        out_shape=(jax.ShapeDtypeStruct((1, 1), F32), jax.ShapeDtypeStruct((t, d), F32),
                   jax.ShapeDtypeStruct((t, d), BF16), jax.ShapeDtypeStruct((1, d), F32)),
        in_specs=[gs, rs, rs, rs],
        out_specs=(pl.BlockSpec((1, 1), lambda i: (0, 0)), rs, rs, gs),
        compiler_params=_cparams(("arbitrary",)),
    )(g, x, r, tgt)


def adamw(w, gs, m, v, *, name, behind=None):
    nl, r, c = w.shape
    assert len(gs) == nl
    if r % SUBLANES == 0:
        tr, tc = _pick(r, (256, 128, 64, 32, 16, 8)), c
    else:
        tr, tc = r, _pick(c, (256, 128))
    k1 = 1.0 - ADAM_B1 ** ADAM_STEP
    k2 = 1.0 - ADAM_B2 ** ADAM_STEP

    def body(*refs):
        w_ref, m_ref, v_ref = refs[0], refs[1], refs[2]
        g_refs = refs[3:3 + nl]
        go_ref, d_ref, mo_ref, vo_ref = refs[-4:]
        gg = g_refs[0][...]
        for li in range(1, nl):
            gg = jnp.where(pl.program_id(0) == li, g_refs[li][...], gg)
        mn = ADAM_B1 * m_ref[...] + (1.0 - ADAM_B1) * gg
        vn = ADAM_B2 * v_ref[...] + (1.0 - ADAM_B2) * (gg * gg)
        go_ref[...] = gg
        d_ref[...] = -ADAM_LR * ((mn / k1) / (jnp.sqrt(vn / k2) + ADAM_EPS) + ADAM_WD * w_ref[...])
        mo_ref[...] = mn
        vo_ref[...] = vn

    if tc == c:
        sp = pl.BlockSpec((None, tr, c), lambda l, i: (l, i, 0))
        gsp = pl.BlockSpec((tr, c), lambda l, i: (i, 0))
    else:
        sp = pl.BlockSpec((None, r, tc), lambda l, i: (l, 0, i))
        gsp = pl.BlockSpec((r, tc), lambda l, i: (0, i))
    sds = jax.ShapeDtypeStruct((nl, r, c), F32)
    extra = [] if behind is None else [behind]
    return pl.pallas_call(
        body, name=name, grid=(nl, (r // tr) * (c // tc)), out_shape=(sds, sds, sds, sds),
        in_specs=[sp, sp, sp] + [gsp] * nl + [pl.BlockSpec(e.shape, lambda l, i: (0, 0)) for e in extra], out_specs=(sp, sp, sp, sp),
        compiler_params=_cparams(("parallel", "parallel")),
    )(w, m, v, *gs, *extra)


MESH = pl.DeviceIdType.MESH
ANY = pl.BlockSpec(memory_space=pl.ANY)


def _me():
    return lax.axis_index("x"), lax.axis_index("y"), lax.axis_index("c")


def _other_chips(x, y):
    return [(1 - x, y), (x, 1 - y), (1 - x, 1 - y)]


HBM = pl.BlockSpec(memory_space=pltpu.HBM)
SEM = pl.BlockSpec(memory_space=pltpu.SEMAPHORE)
EFFECT = pltpu.SideEffectType.DATAFLOW_SIDE_EFFECTING


def _hbm(a):
    return pltpu.with_memory_space_constraint(a, pltpu.HBM)


def _half_rows(arr, which):
    hrows = arr.shape[-2] // 2
    return pl.ds(which * hrows, hrows)


def _ici_copies(srcs, lands, sems):
    x, y, c = _me()
    chip = 2 * x + y
    return [pltpu.make_async_remote_copy(src_ref=srcs[i].at[_half_rows(srcs[i], c)], dst_ref=lands[i].at[chip, _half_rows(srcs[i], c)],
                                         send_sem=sems[0].at[3 * i + j], recv_sem=sems[1].at[3 * i + j],
                                         device_id=(ch[0], ch[1], c), device_id_type=MESH)
            for i in range(len(srcs)) for j, ch in enumerate(_other_chips(x, y))]


def _own_copies(srcs, lands, sems):
    x, y, c = _me()
    return [pltpu.make_async_remote_copy(src_ref=srcs[i], dst_ref=lands[i].at[2 * x + y], send_sem=sems[0].at[i],
                                         recv_sem=sems[1].at[i], device_id=(x, y, 1 - c), device_id_type=MESH)
            for i in range(len(srcs))]


def _fwd_copies(srcs, lands, sems, half):
    x, y, c = _me()
    cps = []
    for i in range(len(srcs)):
        for j, ch in enumerate(_other_chips(x, y)):
            part = lands[i].at[2 * ch[0] + ch[1], _half_rows(srcs[i], half)]
            cps.append(pltpu.make_async_remote_copy(src_ref=part, dst_ref=part, send_sem=sems[0].at[3 * i + j],
                                                    recv_sem=sems[1].at[3 * i + j], device_id=(x, y, 1 - c), device_id_type=MESH))
    return cps


def gather_start(groups, *, name):
    arrs = [a for g in groups for a in g]
    n, ng = len(arrs), len(groups)
    bounds = np.cumsum([0] + [len(g) for g in groups])

    def body(*refs):
        srcs, lands = refs[:n], refs[n:2 * n]
        sems = refs[2 * n:2 * n + 4 * ng]
        token = refs[-1]
        for gi in range(ng):
            lo, hi = bounds[gi], bounds[gi + 1]
            for cp in _ici_copies(srcs[lo:hi], lands[lo:hi], sems[4 * gi:4 * gi + 2]):
                cp.start()
        for gi in range(ng):
            lo, hi = bounds[gi], bounds[gi + 1]
            for cp in _own_copies(srcs[lo:hi], lands[lo:hi], sems[4 * gi + 2:4 * gi + 4]):
                cp.start()
        token[...] = jnp.zeros_like(token)

    sem_shapes = []
    for g in groups:
        sem_shapes += [pltpu.SemaphoreType.DMA((3 * len(g),))] * 2 + [pltpu.SemaphoreType.DMA((len(g),))] * 2
    res = pl.pallas_call(
        body, name=name,
        out_shape=tuple(sem_shapes) + tuple(pltpu.HBM(a.shape, a.dtype) for a in arrs)
        + tuple(pltpu.HBM((N_CHIPS,) + a.shape, a.dtype) for a in arrs) + (jax.ShapeDtypeStruct((SUBLANES, LANES), F32),),
        in_specs=[HBM] * (2 * n),
        out_specs=tuple([SEM] * (4 * ng)) + tuple([HBM] * (2 * n)) + (pl.BlockSpec(memory_space=pltpu.VMEM),),
        input_output_aliases={i: 4 * ng + i for i in range(2 * n)},
        compiler_params=pltpu.CompilerParams(has_side_effects=EFFECT),
    )(*[_hbm(a) for a in arrs], *[_hbm(lax.empty((N_CHIPS,) + a.shape, a.dtype)) for a in arrs])
    sems, thru, lands, token = res[:4 * ng], res[4 * ng:4 * ng + n], res[4 * ng + n:4 * ng + 2 * n], res[-1]
    out = [{"ici": (sems[4 * gi], sems[4 * gi + 1]), "own": (sems[4 * gi + 2], sems[4 * gi + 3]),
            "thru": list(thru[bounds[gi]:bounds[gi + 1]]), "lands": list(lands[bounds[gi]:bounds[gi + 1]])} for gi in range(ng)]
    return out, token


def gather_forward(group, after, *, name):
    thru, lands = group["thru"], group["lands"]
    n = len(thru)

    def body(*refs):
        srcs, lands_r = refs[:n], refs[n:2 * n]
        ici = refs[2 * n:2 * n + 2]
        fwd = refs[2 * n + 3 + 2 * n:2 * n + 3 + 2 * n + 2]
        for cp in _ici_copies(srcs, lands_r, ici):
            cp.wait_send()
            cp.wait_recv()
        for cp in _fwd_copies(srcs, lands_r, fwd, _me()[2]):
            cp.start()
        refs[-1][...] = jnp.zeros_like(refs[-1])

    res = pl.pallas_call(
        body, name=name,
        out_shape=tuple(pltpu.HBM(a.shape, a.dtype) for a in thru) + tuple(pltpu.HBM(a.shape, a.dtype) for a in lands)
        + (pltpu.SemaphoreType.DMA((3 * n,)),) * 2 + (jax.ShapeDtypeStruct((SUBLANES, LANES), F32),),
        in_specs=[HBM] * (2 * n) + [SEM, SEM, pl.BlockSpec(memory_space=pl.ANY)],
        out_specs=tuple([HBM] * (2 * n)) + (SEM, SEM, pl.BlockSpec(memory_space=pltpu.VMEM)),
        input_output_aliases={i: i for i in range(2 * n)},
        compiler_params=pltpu.CompilerParams(has_side_effects=EFFECT),
    )(*thru, *lands, *group["ici"], after)
    return {"own": group["own"], "fwd": (res[2 * n], res[2 * n + 1]), "thru": list(res[:n]), "lands": list(res[n:2 * n]),
            "token": res[-1]}


def gather_wait(group, after, *, name):
    thru, lands = group["thru"], group["lands"]
    n = len(thru)

    def body(*refs):
        srcs, lands_r = refs[:n], refs[n:2 * n]
        own, fwd = refs[2 * n:2 * n + 2], refs[2 * n + 2:2 * n + 4]
        c = _me()[2]
        for mine, theirs in zip(_fwd_copies(srcs, lands_r, fwd, c), _fwd_copies(srcs, lands_r, fwd, 1 - c)):
            mine.wait_send()
            theirs.wait_recv()
        for cp in _own_copies(srcs, lands_r, own):
            cp.wait_send()
            cp.wait_recv()

    res = pl.pallas_call(
        body, name=name,
        out_shape=tuple(pltpu.HBM(a.shape, a.dtype) for a in thru) + tuple(pltpu.HBM(a.shape, a.dtype) for a in lands),
        in_specs=[HBM] * (2 * n) + [SEM] * 4 + [pl.BlockSpec(memory_space=pl.ANY)],
        out_specs=tuple([HBM] * (2 * n)),
        input_output_aliases={i: i for i in range(2 * n)},
        compiler_params=pltpu.CompilerParams(has_side_effects=EFFECT),
    )(*thru, *lands, *group["own"], *group["fwd"], after)
    return list(res[n:])


def _pair_copies(srcs, lands, sems):
    x, y, c = _me()
    return [pltpu.make_async_remote_copy(src_ref=srcs[i].at[:, _half_rows(srcs[i], 1 - c), :], dst_ref=lands[i], send_sem=sems[0].at[i],
                                         recv_sem=sems[1].at[i], device_id=(x, y, 1 - c), device_id_type=MESH)
            for i in range(len(srcs))]


def _scatter_copies(srcs, lands, sems):
    x, y, c = _me()
    return [pltpu.make_async_remote_copy(src_ref=srcs[i].at[2 * ch[0] + ch[1]], dst_ref=lands[i].at[j], send_sem=sems[0].at[3 * i + j],
                                         recv_sem=sems[1].at[3 * i + j], device_id=(ch[0], ch[1], c), device_id_type=MESH)
            for i in range(len(srcs)) for j, ch in enumerate(_other_chips(x, y))]


def split_start(copies, arrs, land_shapes, nsem, *, name):
    n = len(arrs)

    def body(*refs):
        for cp in copies(refs[:n], refs[n:2 * n], refs[2 * n:2 * n + 2]):
            cp.start()
        refs[-1][...] = jnp.zeros_like(refs[-1])

    res = pl.pallas_call(
        body, name=name,
        out_shape=(pltpu.SemaphoreType.DMA((nsem,)),) * 2 + tuple(pltpu.HBM(a.shape, a.dtype) for a in arrs)
        + tuple(pltpu.HBM(s, a.dtype) for s, a in zip(land_shapes, arrs)) + (jax.ShapeDtypeStruct((SUBLANES, LANES), F32),),
        in_specs=[HBM] * (2 * n), out_specs=(SEM, SEM) + tuple([HBM] * (2 * n)) + (pl.BlockSpec(memory_space=pltpu.VMEM),),
        input_output_aliases={i: 2 + i for i in range(2 * n)},
        compiler_params=pltpu.CompilerParams(has_side_effects=EFFECT),
    )(*[_hbm(a) for a in arrs], *[_hbm(lax.empty(s, a.dtype)) for s, a in zip(land_shapes, arrs)])
    return {"copies": copies, "sems": (res[0], res[1]), "thru": list(res[2:2 + n]), "lands": list(res[2 + n:2 + 2 * n]),
            "token": res[-1]}


def split_wait(state, after, *, name):
    thru, lands, copies = state["thru"], state["lands"], state["copies"]
    n = len(thru)

    def body(*refs):
        for cp in copies(refs[:n], refs[n:2 * n], refs[2 * n:2 * n + 2]):
            cp.wait_send()
            cp.wait_recv()

    res = pl.pallas_call(
        body, name=name,
        out_shape=tuple(pltpu.HBM(a.shape, a.dtype) for a in thru) + tuple(pltpu.HBM(a.shape, a.dtype) for a in lands),
        in_specs=[HBM] * (2 * n) + [SEM, SEM, pl.BlockSpec(memory_space=pl.ANY)], out_specs=tuple([HBM] * (2 * n)),
        input_output_aliases={i: i for i in range(2 * n)},
        compiler_params=pltpu.CompilerParams(has_side_effects=EFFECT),
    )(*thru, *lands, *state["sems"], after)
    return list(res[:n]), list(res[n:])


def sibling_join_halves(arrs, *, name):
    n = len(arrs)

    def body(*refs):
        outs = refs[n:2 * n]
        send_sems, recv_sems = refs[2 * n:]
        x, y, c = _me()
        cps = []
        for i in range(n):
            hrows = arrs[i].shape[0] // 2
            mine = outs[i].at[pl.ds(c * hrows, hrows)]
            cps.append(pltpu.make_async_remote_copy(src_ref=mine, dst_ref=mine, send_sem=send_sems.at[i],
                                                    recv_sem=recv_sems.at[i], device_id=(x, y, 1 - c), device_id_type=MESH))
        for cp in cps:
            cp.start()
        for i in range(n):
            hrows = arrs[i].shape[0] // 2
            theirs = outs[i].at[pl.ds((1 - c) * hrows, hrows)]
            pltpu.make_async_remote_copy(src_ref=theirs, dst_ref=theirs, send_sem=send_sems.at[i], recv_sem=recv_sems.at[i],
                                         device_id=(x, y, 1 - c), device_id_type=MESH).wait_recv()
        for cp in cps:
            cp.wait_send()

    return pl.pallas_call(
        body, name=name,
        out_shape=tuple(jax.ShapeDtypeStruct(a.shape, a.dtype) for a in arrs),
        in_specs=[ANY] * n, out_specs=tuple([ANY] * n),
        input_output_aliases={i: i for i in range(n)},
        scratch_shapes=[pltpu.SemaphoreType.DMA((n,)), pltpu.SemaphoreType.DMA((n,))],
        compiler_params=pltpu.CompilerParams(has_side_effects=True),
    )(*arrs)


def pre_reduce(gd, sib, who, *, name):
    _, r, c = gd.shape
    h = r // 2
    tr = _pick(h, (256, 128, 64, 32, 16, 8))
    nrb = h // tr

    def body(who_ref, gd_ref, sib_ref, pb_ref, own_ref):
        p = gd_ref[...] + sib_ref[...]
        pb_ref[...] = p.astype(BF16)

        @pl.when(pl.program_id(1) == who_ref[1])
        def _():
            own_ref[...] = p

    return pl.pallas_call(
        body, name=name,
        out_shape=(jax.ShapeDtypeStruct((N_CHIPS, h, c), BF16), jax.ShapeDtypeStruct((h, c), F32)),
        grid_spec=pltpu.PrefetchScalarGridSpec(
            num_scalar_prefetch=1, grid=(nrb, N_CHIPS),
            in_specs=[pl.BlockSpec((None, tr, c), lambda i, s, w: (s, w[0] * nrb + i, 0)),
                      pl.BlockSpec((None, tr, c), lambda i, s, w: (s, i, 0))],
            out_specs=(pl.BlockSpec((None, tr, c), lambda i, s, w: (s, i, 0)),
                       pl.BlockSpec((tr, c), lambda i, s, w: (i, 0)))),
        compiler_params=_cparams(("parallel", "arbitrary")),
    )(who, gd, sib)


def final_reduce(own, rcv, who, *, name):
    h, c = own.shape
    tr = _pick(h, (256, 128, 64, 32, 16, 8))
    nrb = h // tr

    def body(who_ref, own_ref, rcv_ref, o_ref):
        del who_ref
        acc = own_ref[...]
        for j in range(3):
            acc = acc + rcv_ref[j].astype(F32)
        o_ref[...] = acc

    return pl.pallas_call(
        body, name=name, out_shape=jax.ShapeDtypeStruct((2 * h, c), F32),
        grid_spec=pltpu.PrefetchScalarGridSpec(
            num_scalar_prefetch=1, grid=(nrb,),
            in_specs=[pl.BlockSpec((tr, c), lambda i, w: (i, 0)), pl.BlockSpec((3, tr, c), lambda i, w: (0, i, 0))],
            out_specs=pl.BlockSpec((tr, c), lambda i, w: (w[0] * nrb + i, 0))),
        compiler_params=_cparams(("parallel",)),
    )(who, own, rcv)


def sum8(landed, own, me, *, name):
    _, r, c = landed.shape
    tr = _pick(r, (256, 128, 64, 32, 16, 8))

    def body(me_ref, a_ref, own_ref, o_ref):
        acc = None
        for j in range(8):
            term = jnp.where(me_ref[0] == j, own_ref[...], a_ref[j])
            acc = term if acc is None else acc + term
        o_ref[...] = acc

    return pl.pallas_call(
        body, name=name, out_shape=jax.ShapeDtypeStruct((r, c), F32),
        grid_spec=pltpu.PrefetchScalarGridSpec(
            num_scalar_prefetch=1, grid=(r // tr,),
            in_specs=[pl.BlockSpec((8, tr, c), lambda i, w: (0, i, 0)), pl.BlockSpec((tr, c), lambda i, w: (i, 0))],
            out_specs=pl.BlockSpec((tr, c), lambda i, w: (i, 0))),
        compiler_params=_cparams(("parallel",)),
    )(me, landed, own)


def _everyone_copies(srcs, lands, sems):
    x, y, c = _me()
    me = 4 * x + 2 * y + c
    cps = []
    for k in range(1, 8):
        fx, fy, fc = (k >> 2) & 1, (k >> 1) & 1, k & 1
        to = (x + fx - 2 * x * fx, y + fy - 2 * y * fy, c + fc - 2 * c * fc)
        cps.append(pltpu.make_async_remote_copy(src_ref=srcs[0], dst_ref=lands[0].at[me], send_sem=sems[0].at[k - 1],
                                                recv_sem=sems[1].at[k - 1], device_id=to, device_id_type=MESH))
    return cps


def _behind(p, *tokens):
    for tk in tokens:
        if tk is not None:
            p = p + tk[0, 0]
    return p


def local_step(x, tgt, wb, ws, arrive=lambda layer, event, after: {}, reduce=lambda group, event, gb, after: None, h0=None):
    t, d = x.shape
    nh = ws["alog"].shape[0]
    ng = ws["w_s"].shape[0]
    assert nh == ng
    mix_w = (nh + ng) * HEAD_DIM

    if h0 is None:
        (h0,) = rows_fwd(stage_norm, [ws["e_norm"]], [x], [(1, BF16, d)], name="f_norm_e")
    pm = mm_nt(h0, wb["w_main_t"], name="f_proj_main")
    pba = mm_nt(h0, wb["w_ba_t"], name="f_proj_ba")
    qkv = conv_fwd(pm, ws["conv_w"], mode="silu", name="f_conv4")
    cat, states, tinvs = delta_fwd(qkv, pm, pba, ws["alog"], ws["dtb"], ws["onorm"], mix_w, name="f_delta")
    tok = arrive("layer0", "landed", states).get("token")
    cat = gmlp_fwd(pm, _behind(ws["lng"], tok), ws["lnb"], ws["w_s"], ws["bs"], cat, name="f_gmlp")
    wb = {**wb, **arrive("layer0", "joined", cat)}
    y0 = mm_nn(cat, wb["w_out"], name="f_out")
    x1, h1 = rows_fwd(stage_res_norm, [ws["f_norm0"]], [x, y0], [(0, F32, d), (1, BF16, d)], name="f_norm_f0")
    a1, s1 = mm_nn(h1, wb["w1_0"], name="f_mlp0_up", epilogue="relu2")
    y1 = mm_nn(s1, wb["w2_0"], name="f_mlp0_down")
    tok = arrive("layer1", "landed", y1).get("token")
    x2, h2 = rows_fwd(stage_res_norm, [_behind(ws["o_norm"], tok)], [x1, y1], [(0, F32, d), (1, BF16, d)], name="f_norm_o")
    wb = {**wb, **arrive("layer1", "joined", h2)}
    zz = mm_nn(h2, wb["pw1"], name="f_pw1")
    zparts = [(zz, d, 0), (zz, d, 1)]
    (gl,) = rows_fwd(stage_glu, [ws["b1a"], ws["b1b"]], zparts, [(0, F32, d)], name="f_glu")
    cv = conv_fwd(gl, ws["dw"], mode="plain", name="f_conv31")
    ln_params = [ws["dw_b"], ws["ln_g"], ws["ln_b"]]
    (sl,) = rows_fwd(stage_ln_silu, ln_params, [cv], [(0, BF16, d)], name="f_ln_silu")
    y2 = mm_nn(sl, wb["pw2"], name="f_pw2")
    x3, h3 = rows_fwd(stage_res_bias_norm, [ws["b2"], ws["f_norm1"]], [x2, y2], [(0, F32, d), (1, BF16, d)], name="f_norm_f1")
    a3, s3 = mm_nn(h3, wb["w1_1"], name="f_mlp1_up", epilogue="relu2")
    y3 = mm_nn(s3, wb["w2_1"], name="f_mlp1_down")
    loss, d4, d4b, g_final = loss_head(ws["final_norm"], x3, y3, tgt, name="loss_head")

    gb, gs = {}, {"final_norm": g_final}
    s_up = wb["w1_0"].shape[0]

    gb["w2_1"] = mm_tn(s3, d4b, groups=1, name="b_mlp1_down_w")
    dpre3 = mm_nt(d4b, wb["w2_1"], name="b_mlp1_down_x", mul=a3, out_dtype=BF16)
    gb["w1_1"] = mm_tn(h3, dpre3, groups=s_up, name="b_mlp1_up_w")
    tok = reduce("mlp1", "ready", gb, None)
    dh3 = mm_nt(dpre3, wb["w1_1"], name="b_mlp1_up_x", out_dtype=BF16)
    d3, d3b, gs["b2"], gs["f_norm1"] = rows_bwd(stage_res_bias_norm, [ws["b2"], _behind(ws["f_norm1"], tok)], [x2, y2], [d4, dh3],
                                                [(0, F32), (1, BF16)], name="b_norm_f1")
    tok = reduce("mlp1", "paired", gb, d3)
    gb["pw2"] = mm_tn(sl, d3b, groups=1, name="b_pw2_w")
    dsl = mm_nt(d3b, wb["pw2"], name="b_pw2_x", out_dtype=BF16)
    dcv, gs["dw_b"], gs["ln_g"], gs["ln_b"] = rows_bwd(stage_ln_silu, [_behind(ln_params[0], tok)] + ln_params[1:], [cv], [dsl],
                                                       [(0, F32)], name="b_ln_silu")
    dgl, gs["dw"] = conv_bwd(gl, ws["dw"], dcv, name="b_conv31")
    dzz, gs["b1a"], gs["b1b"] = rows_bwd(stage_glu, [ws["b1a"], ws["b1b"]], zparts, [dgl], [(0, BF16), (1, BF16)],
                                         name="b_glu", concat=True)
    gb["pw1"] = mm_tn(h2, dzz, groups=wb["pw1"].shape[0], name="b_pw1_w")
    tok = reduce("conv", "ready", gb, None)
    dh2 = mm_nt(dzz, wb["pw1"], name="b_pw1_x", out_dtype=BF16)
    d2, d2b, gs["o_norm"] = rows_bwd(stage_res_norm, [_behind(ws["o_norm"], tok)], [x1, y1], [d3, dh2], [(0, F32), (1, BF16)],
                                     name="b_norm_o")
    tok_conv = reduce("conv", "paired", gb, d2)
    gb["w2_0"] = mm_tn(s1, d2b, groups=1, name="b_mlp0_down_w")
    dpre1 = mm_nt(d2b, wb["w2_0"], name="b_mlp0_down_x", mul=a1, out_dtype=BF16)
    gb["w1_0"] = mm_tn(h1, dpre1, groups=s_up, name="b_mlp0_up_w")
    tok = reduce("mlp0", "ready", gb, None)
    dh1 = mm_nt(dpre1, wb["w1_0"], name="b_mlp0_up_x", out_dtype=BF16)
    d1, d1b, gs["f_norm0"] = rows_bwd(stage_res_norm, [_behind(ws["f_norm0"], tok_conv, tok)], [x, y0], [d2, dh1],
                                      [(0, F32), (1, BF16)], name="b_norm_f0")
    tok_mlp0 = reduce("mlp0", "paired", gb, d1)
    gb["w_out"] = mm_tn(cat, d1b, groups=1, name="b_out_w")
    tok = reduce("mixer", "ready", gb, None)
    dcat = mm_nt(d1b, wb["w_out"], name="b_out_x")
    dqkv_c, dpm, dpba, gs["alog"], gs["dtb"], gs["onorm"] = delta_bwd(
        qkv, pm, pba, ws["alog"], ws["dtb"], _behind(ws["onorm"], tok_mlp0, tok), states, tinvs, dcat, name="b_delta")
    tok = reduce("mixer", "paired", gb, dqkv_c)
    dpm, gs["lng"], gs["lnb"], gs["w_s"], gs["bs"] = gmlp_bwd(
        pm, ws["lng"], ws["lnb"], ws["w_s"], ws["bs"], dcat, dpm, name="b_gmlp")
    conv_w = _behind(ws["conv_w"], tok)
    dconv = conv_fwd(pm, conv_w, mode="silu_bwd", g=dqkv_c, name="b_conv4_silu")
    dpm, gs["conv_w"] = conv_bwd(pm, conv_w, dconv, name="b_conv4", into=dpm)
    gb["w_main_t"] = mm_tn(dpm, h0, groups=1, name="b_proj_main_w")
    gb["w_ba_t"] = mm_tn(dpba, h0, groups=1, name="b_proj_ba_w")
    dh0 = mm_nn(dpm, wb["w_main_t"], name="b_proj_main_x")
    dh0 = mm_nn(dpba, wb["w_ba_t"], name="b_proj_ba_x", add=dh0, out_dtype=BF16)
    grad_x, gs["e_norm"] = rows_bwd(stage_norm, [ws["e_norm"]], [x], [d1, dh0], [(0, F32)], name="b_norm_e")
    return loss, grad_x, gb, gs


WEIGHTS = ["e_norm", "e_w_in", "e_conv_w", "e_a_log", "e_dt_bias", "e_o_norm", "e_ln_g", "e_ln_b", "e_w_s", "e_b_s", "e_w_out",
           "o_norm", "o_pw1", "o_pw1_b", "o_dw", "o_dw_b", "o_ln_g", "o_ln_b", "o_pw2", "o_pw2_b", "f_norm", "f_w1", "f_w2",
           "final_norm"]
BIG = ["e_w_in", "e_w_out", "o_pw1", "o_pw2", "f_w1", "f_w2"]
SMALL_SHARDED = ["e_conv_w", "o_norm", "o_pw1_b", "o_dw", "o_dw_b", "o_ln_g", "o_ln_b", "o_pw2_b"]
SMALL = [n for n in WEIGHTS if n not in BIG]
LANES = 128
PACK_ROWS = 16
IN_ROW_MULT = 256
REDUCE_GROUPS = {"mlp1": ("w2_1", "w1_1"), "conv": ("pw2", "pw1"), "mlp0": ("w2_0", "w1_0"), "mixer": ("w_out",)}


def _pack(arrs):
    flat = jnp.concatenate([a.reshape(-1).astype(F32) for a in arrs])
    n = flat.shape[0]
    rows = -(-n // (LANES * PACK_ROWS)) * PACK_ROWS
    return jnp.pad(flat, (0, rows * LANES - n)).reshape(rows, LANES)


def _unpack(flat, shapes, lead=()):
    outs, off = [], 0
    for s in shapes:
        n = int(np.prod(s))
        outs.append(flat[..., off:off + n].reshape(lead + tuple(s)))
        off += n
    return outs


def kernel(x, e_norm, e_w_in, e_conv_w, e_a_log, e_dt_bias, e_o_norm, e_ln_g, e_ln_b, e_w_s, e_b_s, e_w_out, o_norm, o_pw1, o_pw1_b, o_dw, o_dw_b, o_ln_g, o_ln_b, o_pw2, o_pw2_b, f_norm, f_w1, f_w2, final_norm, loss_target, m_e_norm, m_e_w_in, m_e_conv_w, m_e_a_log, m_e_dt_bias, m_e_o_norm, m_e_ln_g, m_e_ln_b, m_e_w_s, m_e_b_s, m_e_w_out, m_o_norm, m_o_pw1, m_o_pw1_b, m_o_dw, m_o_dw_b, m_o_ln_g, m_o_ln_b, m_o_pw2, m_o_pw2_b, m_f_norm, m_f_w1, m_f_w2, m_final_norm, v_e_norm, v_e_w_in, v_e_conv_w, v_e_a_log, v_e_dt_bias, v_e_o_norm, v_e_ln_g, v_e_ln_b, v_e_w_s, v_e_b_s, v_e_w_out, v_o_norm, v_o_pw1, v_o_pw1_b, v_o_dw, v_o_dw_b, v_o_ln_g, v_o_ln_b, v_o_pw2, v_o_pw2_b, v_f_norm, v_f_w1, v_f_w2, v_final_norm):
    a = dict(locals())
    xi, yi, ci = _me()
    chip = 2 * xi + yi
    who = jnp.stack([ci, chip]).astype(jnp.int32)
    t, d = x.shape[1], x.shape[2]
    nh, ng = e_a_log.shape[1], e_w_s.shape[1]
    n_qkv, n_av, n_bw = 3 * nh * HEAD_DIM, nh * HEAD_DIM, ng * HEAD_DIM
    in_cols = n_qkv + n_av + 2 * nh + 2 * n_bw
    c_ba = n_qkv + n_av

    sh_in = in_cols // N_CHIPS
    pad_in = -(-sh_in // IN_ROW_MULT) * IN_ROW_MULT - sh_in
    w_in_t_local = jnp.pad(e_w_in[0].T.astype(BF16), ((0, pad_in), (0, 0)))
    small_local = [a[n] for n in SMALL_SHARDED]
    first, tok_in = gather_start([[w_in_t_local, _pack(small_local)]], name="gather_start_in")
    travelling, token = gather_start([[(e_w_out[0] + tok_in[0, 0]).astype(BF16), f_w1[0].astype(BF16), f_w2[0].astype(BF16)],
                                      [w.astype(BF16) for w in (o_pw1[0], o_pw2[0], f_w1[1], f_w2[1])]], name="gather_start")
    (h0,) = rows_fwd(stage_norm, [e_norm + token[0, 0]], [x[0]], [(1, BF16, d)], name="f_norm_e")
    g_in, g_small = gather_wait(gather_forward(first[0], h0, name="gather_forward_in"), h0, name="gather_wait_in")

    state = {"layer0": travelling[0], "layer1": travelling[1]}

    def arrive(layer, event, after):
        if event == "landed":
            state[layer] = gather_forward(state[layer], after, name="gather_forward_" + layer)
            return {"token": state[layer]["token"]}
        got = gather_wait(state[layer], after, name="gather_wait_" + layer)
        if layer == "layer0":
            g_out, g_w1, g_w2 = got
            return {"w_out": g_out.reshape(1, -1, d), "w1_0": g_w1, "w2_0": g_w2.reshape(1, -1, d)}
        g_pw1, g_pw2, g_w1, g_w2 = got
        return {"pw1": g_pw1, "pw2": g_pw2.reshape(1, -1, d), "w1_1": g_w1, "w2_1": g_w2.reshape(1, -1, d)}

    def in_rows(lo, hi):
        parts = []
        for s in range(N_CHIPS):
            a0, a1 = max(lo, s * sh_in), min(hi, (s + 1) * sh_in)
            if a0 < a1:
                parts.append(g_in[s, a0 - s * sh_in:a1 - s * sh_in])
        return parts

    wb = {
        "w_main_t": jnp.concatenate(in_rows(0, c_ba) + in_rows(c_ba + 2 * nh, in_cols), axis=0)[None],
        "w_ba_t": jnp.pad(jnp.concatenate(in_rows(c_ba, c_ba + 2 * nh), axis=0), ((0, LANES - 2 * nh), (0, 0)))[None],
    }
    pieces = _unpack(g_small.reshape(N_CHIPS, -1), [w.shape for w in small_local], lead=(N_CHIPS,))
    full = {n: jnp.moveaxis(p, 0, -2).reshape(p.shape[1:-1] + (N_CHIPS * p.shape[-1],)) for n, p in zip(SMALL_SHARDED, pieces)}
    ws = {
        "e_norm": e_norm, "conv_w": full["e_conv_w"][0], "alog": e_a_log.reshape(nh, 1, 1), "dtb": e_dt_bias.reshape(nh, 1, 1),
        "onorm": e_o_norm, "lng": e_ln_g.reshape(ng, 1, HEAD_DIM), "lnb": e_ln_b.reshape(ng, 1, HEAD_DIM), "w_s": e_w_s[0],
        "bs": e_b_s.reshape(ng, HEAD_DIM, 1), "o_norm": full["o_norm"], "b1a": full["o_pw1_b"][:, :d], "b1b": full["o_pw1_b"][:, d:],
        "dw": full["o_dw"][0], "dw_b": full["o_dw_b"], "ln_g": full["o_ln_g"], "ln_b": full["o_ln_b"], "b2": full["o_pw2_b"],
        "f_norm0": f_norm[0:1], "f_norm1": f_norm[1:2], "final_norm": final_norm.reshape(1, d),
    }

    pending = {}

    def reduce(group, event, gb, after):
        names = REDUCE_GROUPS[group]
        if event == "ready":
            arrs = [gb[nme] if gb[nme].shape[0] == N_CHIPS else gb[nme].reshape(N_CHIPS, -1, d) for nme in names]
            pending[group] = split_start(_pair_copies, arrs, [(N_CHIPS, g.shape[1] // 2, g.shape[2]) for g in arrs], len(arrs),
                                         name="reduce_sibling_start_" + group)
            return pending[group]["token"]
        mine, theirs = split_wait(pending[group], after, name="reduce_sibling_wait_" + group)
        pairs = [pre_reduce(gd, sb, who, name="reduce_pair_" + nme) for nme, gd, sb in zip(names, mine, theirs)]
        sums = [p[0] for p in pairs]
        pending[group] = {"own": [p[1] for p in pairs],
                          "scatter": split_start(_scatter_copies, sums, [(3,) + s.shape[1:] for s in sums], 3 * len(sums),
                                                 name="reduce_chips_start_" + group)}
        return pending[group]["scatter"]["token"]

    loss, grad_x, gb, gs = local_step(x[0], loss_target[0], wb, ws, arrive, reduce, h0)

    gm, gba = gb["w_main_t"][0], gb["w_ba_t"][0]
    g_in_t = jnp.concatenate([gm[:c_ba], gba[:2 * nh], gm[c_ba:]], axis=0).reshape(N_CHIPS, sh_in, d)
    g_in_padded = jnp.pad(g_in_t, ((0, 0), (0, pad_in), (0, 0)))
    in_pair = split_start(_pair_copies, [g_in_padded], [(N_CHIPS, (sh_in + pad_in) // 2, d)], 1, name="reduce_sibling_start_w_in")
    small_global = {
        "e_norm": gs["e_norm"], "e_conv_w": gs["conv_w"][None], "e_a_log": gs["alog"].reshape(1, nh), "e_dt_bias": gs["dtb"].reshape(1, nh),
        "e_o_norm": gs["onorm"], "e_ln_g": gs["lng"].reshape(1, n_bw), "e_ln_b": gs["lnb"].reshape(1, n_bw), "e_w_s": gs["w_s"][None],
        "e_b_s": gs["bs"].reshape(1, ng, HEAD_DIM), "o_norm": gs["o_norm"], "o_pw1_b": jnp.concatenate([gs["b1a"], gs["b1b"]], axis=1),
        "o_dw": gs["dw"][None], "o_dw_b": gs["dw_b"], "o_ln_g": gs["ln_g"], "o_ln_b": gs["ln_b"], "o_pw2_b": gs["b2"],
        "f_norm": jnp.concatenate([gs["f_norm0"], gs["f_norm1"]], axis=0), "final_norm": gs["final_norm"].reshape(d),
    }
    packed = _pack([small_global[n] for n in SMALL] + [loss])
    everyone = split_start(_everyone_copies, [packed], [(8,) + packed.shape], 7, name="reduce_small_start")

    who_then = who + (in_pair["token"][0, 0] + everyone["token"][0, 0]).astype(jnp.int32)
    half = {}
    for group, names in REDUCE_GROUPS.items():
        _, landed = split_wait(pending[group]["scatter"], grad_x, name="reduce_chips_wait_" + group)
        for nme, ow, rc in zip(names, pending[group]["own"], landed):
            half[nme] = final_reduce(ow, rc, who_then, name="reduce_sum_" + nme)
    r_out, r_pw1, r_pw2, r_w1_0, r_w1_1, r_w2_0, r_w2_1 = sibling_join_halves(
        [half[nme] for nme in ("w_out", "pw1", "pw2", "w1_0", "w1_1", "w2_0", "w2_1")], name="reduce_join")
    big_grads = {"e_w_out": [r_out], "o_pw1": [r_pw1], "o_pw2": [r_pw2], "f_w1": [r_w1_0, r_w1_1], "f_w2": [r_w2_0, r_w2_1]}

    mine, theirs = split_wait(in_pair, r_out, name="reduce_sibling_wait_w_in")
    in_sum, in_own = pre_reduce(mine[0], theirs[0], who, name="reduce_pair_w_in")
    in_scatter = split_start(_scatter_copies, [in_sum], [(3,) + in_sum.shape[1:]], 3, name="reduce_chips_start_w_in")
    out = {}
    prev = in_scatter["token"]
    for n in BIG:
        if n != "e_w_in":
            out[n] = adamw(a[n], big_grads[n], a["m_" + n], a["v_" + n], name="adamw_" + n, behind=prev)
            prev = out[n][1][0, :SUBLANES, :LANES]

    own_packed, landed = split_wait(everyone, prev, name="reduce_small_wait")
    summed = sum8(landed[0], own_packed[0], (4 * xi + 2 * yi + ci).astype(jnp.int32).reshape(1), name="reduce_small_sum")
    *small_full, total = _unpack(summed.reshape(-1), [small_global[n].shape for n in SMALL] + [()])
    small_grads = {}
    for n, g in zip(SMALL, small_full):
        if n in SMALL_SHARDED:
            width = a[n].shape[-1]
            g = lax.dynamic_slice_in_dim(g, chip * width, width, axis=g.ndim - 1)
        small_grads[n] = g
    sw, sm, sv, sg = (_pack([src[n] for n in SMALL])[None] for src in
                      ({n: a[n] for n in SMALL}, {n: a["m_" + n] for n in SMALL}, {n: a["v_" + n] for n in SMALL}, small_grads))
    res = adamw(sw, [sg[0]], sm, sv, name="adamw_small")

    _, landed = split_wait(in_scatter, res[1], name="reduce_chips_wait_w_in")
    (r_in,) = sibling_join_halves([final_reduce(in_own, landed[0], who, name="reduce_sum_w_in")], name="reduce_join_w_in")
    w_in_res = adamw(e_w_in.transpose(0, 2, 1), [r_in[:sh_in]], m_e_w_in.transpose(0, 2, 1), v_e_w_in.transpose(0, 2, 1),
                     name="adamw_e_w_in")
    out["e_w_in"] = tuple(r.transpose(0, 2, 1) for r in w_in_res)
    shapes = [a[n].shape for n in SMALL]
    unpacked = [_unpack(r.reshape(-1), shapes) for r in res]
    for i, n in enumerate(SMALL):
        out[n] = tuple(u[i] for u in unpacked)

    result = [total, grad_x[None]]
    for k in range(4):
        result += [out[n][k] for n in WEIGHTS]
    return tuple(result)
```
